```python
import functools
import jax, jax.numpy as jnp
from jax import lax
import numpy as np

D_MODEL = 1024
BATCH = 4
SEQ = 4096
DEPTH = 1
DEC_BATCH = 128
DEC_SEQ = 1
PAST_LEN = 8192
PAGE_SIZE = 128

LRU_WIDTH = D_MODEL
LRU_BLOCKS = 16
LRU_BLOCK = LRU_WIDTH // LRU_BLOCKS
CONV_W = 4
LRU_C = 8.0
N_HEADS = 16
N_KV = 4
GROUP = N_HEADS // N_KV
HEAD_DIM = 64
WINDOW = 128
NEG_INF = -1e30
N_GROUPS = 4
EXPERTS_PER_GROUP = 8
N_EXPERTS = N_GROUPS * EXPERTS_PER_GROUP
TOP_K = 2
EXPERT_FF = D_MODEL // 2
MOE_BLOCK = 128
ALPHA = (2 * DEPTH) ** 0.25
BETA = (8 * DEPTH) ** -0.25
LN_EPS = 1e-5
SPLITS = (LRU_WIDTH, LRU_WIDTH, N_HEADS * HEAD_DIM, N_KV * HEAD_DIM, N_KV * HEAD_DIM, D_MODEL, D_MODEL)
IN_WIDTH = sum(SPLITS)

kernel_name = "hawk_swa_sink_hiermoe_deepnorm_step"


def layer_norm(x, g, b):
    xf = x.astype(jnp.float32)
    mu = jnp.mean(xf, axis=-1, keepdims=True)
    var = jnp.mean(jnp.square(xf - mu), axis=-1, keepdims=True)
    return ((xf - mu) * lax.rsqrt(var + LN_EPS) * g + b).astype(x.dtype)


def _lin_combine(e1, e2):
    a1, b1 = e1
    a2, b2 = e2
    return a1 * a2, a2 * b1 + b2


def recurrent_branch(xl, yl, conv_state, h0, w_conv, b_conv, w_rg, b_rg, w_ig, b_ig, lru_lambda):
    B, T, _ = xl.shape
    xp = jnp.concatenate([conv_state.astype(xl.dtype), xl], axis=1)
    new_conv = xp[:, -(CONV_W - 1):]
    xc = sum(w_conv[j] * xp[:, j:j + T] for j in range(CONV_W)) + b_conv
    xb = xc.reshape(B, T, LRU_BLOCKS, LRU_BLOCK)
    r = jax.nn.sigmoid(jnp.einsum('bthi,hij->bthj', xb, w_rg) + b_rg).reshape(B, T, LRU_WIDTH)
    i = jax.nn.sigmoid(jnp.einsum('bthi,hij->bthj', xb, w_ig) + b_ig).reshape(B, T, LRU_WIDTH)
    log_a = (-LRU_C * r.astype(jnp.float32)) * jax.nn.softplus(-lru_lambda.astype(jnp.float32))
    a = jnp.exp(log_a)
    u = jnp.sqrt(-jnp.expm1(2.0 * log_a)) * (i * xc).astype(jnp.float32)
    u = u.at[:, 0].add(a[:, 0] * h0.astype(jnp.float32))
    _, h = lax.associative_scan(_lin_combine, (a, u), axis=1)
    out = h.astype(xl.dtype) * jax.nn.gelu(yl)
    return out, new_conv, h[:, -1].astype(h0.dtype)


def _attend(q, k, v, mask, sinks):
    scores = jnp.einsum('...qkgd,...skd->...kgqs', q, k).astype(jnp.float32) * (HEAD_DIM ** -0.5)
    scores = jnp.where(mask, scores, NEG_INF)
    sink = jnp.broadcast_to(sinks.reshape(N_KV, GROUP, 1, 1).astype(jnp.float32), scores.shape[:-1] + (1,))
    probs = jax.nn.softmax(jnp.concatenate([scores, sink], axis=-1), axis=-1)[..., :-1]
    return jnp.einsum('...kgqs,...skd->...qkgd', probs.astype(v.dtype), v)


def prompt_window_attention(q, k, v, sinks):
    B, T = q.shape[:2]
    nb = T // WINDOW
    qb = q.reshape(B, nb, WINDOW, N_KV, GROUP, HEAD_DIM)
    kb = k.reshape(B, nb, WINDOW, N_KV, HEAD_DIM)
    vb = v.reshape(B, nb, WINDOW, N_KV, HEAD_DIM)

    def with_prev(z):
        prev = jnp.concatenate([jnp.zeros_like(z[:, :1]), z[:, :-1]], axis=1)
        return jnp.concatenate([prev, z], axis=2)

    qi = jnp.arange(WINDOW)[:, None]
    kj = jnp.arange(2 * WINDOW)[None, :]
    band = (kj > qi) & (kj <= qi + WINDOW)
    valid = (jnp.arange(nb)[:, None, None] > 0) | (kj >= WINDOW)[None]
    mask = (band[None] & valid)[:, None, None]
    out = _attend(qb, with_prev(kb), with_prev(vb), mask, sinks)
    return out.reshape(B, T, N_HEADS * HEAD_DIM), k[:, -WINDOW:], v[:, -WINDOW:]


def sample_window_attention(q, k, v, sinks, k_buf, v_buf):
    B, T = q.shape[:2]
    kk = jnp.concatenate([k_buf.astype(k.dtype), k], axis=1)
    vv = jnp.concatenate([v_buf.astype(v.dtype), v], axis=1)
    diff = (WINDOW + jnp.arange(T))[:, None] - jnp.arange(WINDOW + T)[None, :]
    mask = (diff >= 0) & (diff < WINDOW)
    out = _attend(q, kk, vv, mask, sinks)
    return out.reshape(B, T, N_HEADS * HEAD_DIM), kk[:, -WINDOW:], vv[:, -WINDOW:]


def hier_moe(x, w_group, b_group, w_router, b_router, w_gate, w_up, w_down):
    N = x.shape[0]
    xf = x.astype(jnp.float32)
    g_logits = xf @ w_group.astype(jnp.float32) + b_group.astype(jnp.float32)
    g_prob = jax.nn.softmax(g_logits, axis=-1)
    g_idx = jnp.argmax(g_logits, axis=-1)
    p_g = jnp.take_along_axis(g_prob, g_idx[:, None], axis=1)[:, 0]
    e_logits = (xf @ w_router.astype(jnp.float32) + b_router.astype(jnp.float32)).reshape(N, N_GROUPS, EXPERTS_PER_GROUP)
    e_sel = jnp.take_along_axis(e_logits, g_idx[:, None, None], axis=1)[:, 0]
    top_v, top_i = lax.top_k(e_sel, TOP_K)
    w_tok = jax.nn.softmax(top_v, axis=-1) * p_g[:, None]
    e_id = g_idx[:, None].astype(jnp.int32) * EXPERTS_PER_GROUP + top_i.astype(jnp.int32)
    A = N * TOP_K
    flat_e = e_id.reshape(A)
    flat_t = jnp.arange(A, dtype=jnp.int32) // TOP_K
    flat_w = w_tok.reshape(A)
    order = jnp.argsort(flat_e)
    se = flat_e[order]
    counts = jnp.bincount(flat_e, length=N_EXPERTS)
    padded = (counts + MOE_BLOCK - 1) // MOE_BLOCK * MOE_BLOCK
    pad_end = jnp.cumsum(padded)
    pad_start = pad_end - padded
    start = jnp.cumsum(counts) - counts
    dest = pad_start[se] + jnp.arange(A, dtype=jnp.int32) - start[se]
    n_blocks = (A + MOE_BLOCK - 1) // MOE_BLOCK + N_EXPERTS
    P = n_blocks * MOE_BLOCK
    slot_t = jnp.full((P,), N, dtype=jnp.int32).at[dest].set(flat_t[order])
    slot_w = jnp.zeros((P,), jnp.float32).at[dest].set(flat_w[order])
    block_e = jnp.minimum(jnp.searchsorted(pad_end, jnp.arange(n_blocks) * MOE_BLOCK, side='right'), N_EXPERTS - 1)
    x_pad = jnp.concatenate([x, jnp.zeros((1, x.shape[1]), x.dtype)], axis=0)
    xb = x_pad[slot_t].reshape(n_blocks, MOE_BLOCK, x.shape[1])

    def expert_block(args):
        xi, e = args
        h = jax.nn.silu(xi @ w_gate[e]) * (xi @ w_up[e])
        return h @ w_down[e]

    yb = lax.map(expert_block, (xb, block_e)).reshape(P, x.shape[1])
    y = jnp.zeros((N + 1, x.shape[1]), jnp.float32).at[slot_t].add(yb.astype(jnp.float32) * slot_w[:, None])
    return y[:N].astype(x.dtype)


def trunk_layer(x, conv_state, h0, attn_fn, w_in, b_in, w_conv, b_conv, w_rg, b_rg, w_ig, b_ig, lru_lambda,
                sinks, w_lru_out, w_attn_out, w_o, ln1_g, ln1_b, w_group, b_group, w_router, b_router,
                w_gate, w_up, w_down, ln2_g, ln2_b):
    B, T, D = x.shape
    u = x @ w_in + b_in
    xl, yl, q, k, v, g_l, g_a = jnp.split(u, np.cumsum(SPLITS)[:-1].tolist(), axis=-1)
    rec, new_conv, h_last = recurrent_branch(xl, yl, conv_state, h0, w_conv, b_conv, w_rg, b_rg, w_ig, b_ig, lru_lambda)
    q = q.reshape(B, T, N_KV, GROUP, HEAD_DIM)
    k = k.reshape(B, T, N_KV, HEAD_DIM)
    v = v.reshape(B, T, N_KV, HEAD_DIM)
    att, k_state, v_state = attn_fn(q, k, v, sinks)
    merged = jax.nn.sigmoid(g_l) * (rec @ w_lru_out) + jax.nn.sigmoid(g_a) * (att @ w_attn_out)
    x1 = layer_norm(ALPHA * x + merged @ w_o, ln1_g, ln1_b)
    ff = hier_moe(x1.reshape(B * T, D), w_group, b_group, w_router, b_router, w_gate, w_up, w_down).reshape(B, T, D)
    y = layer_norm(ALPHA * x1 + ff, ln2_g, ln2_b)
    return y, k_state, v_state, new_conv, h_last


def setup_inputs(seed: int = 0) -> dict:
    key = jax.random.key(seed)
    ks = jax.random.split(key, 32)
    nrm = lambda k, s: jax.random.normal(k, s, jnp.float32)
    s_d = D_MODEL ** -0.5
    a_c = jax.random.uniform(ks[12], (LRU_WIDTH,), jnp.float32, 0.9, 0.999)
    sig = a_c ** (1.0 / LRU_C)
    return {
        "x_prompt": nrm(ks[0], (BATCH, SEQ, D_MODEL)),
        "x_sample": nrm(ks[1], (DEC_BATCH, DEC_SEQ, D_MODEL)),
        "cache_k": nrm(ks[2], (DEC_BATCH, WINDOW, N_KV, HEAD_DIM)),
        "cache_v": nrm(ks[3], (DEC_BATCH, WINDOW, N_KV, HEAD_DIM)),
        "state_conv": nrm(ks[4], (DEC_BATCH, CONV_W - 1, LRU_WIDTH)),
        "state_lru_h": nrm(ks[5], (DEC_BATCH, LRU_WIDTH)),
        "w_in": nrm(ks[6], (D_MODEL, IN_WIDTH)) * s_d,
        "b_in": nrm(ks[7], (IN_WIDTH,)) * 0.01,
        "w_conv": nrm(ks[8], (CONV_W, LRU_WIDTH)) * CONV_W ** -0.5,
        "b_conv": nrm(ks[9], (LRU_WIDTH,)) * 0.01,
        "w_rg": nrm(ks[10], (LRU_BLOCKS, LRU_BLOCK, LRU_BLOCK)) * LRU_BLOCK ** -0.5,
        "b_rg": nrm(ks[11], (LRU_BLOCKS, LRU_BLOCK)) * 0.01,
        "w_ig": nrm(ks[13], (LRU_BLOCKS, LRU_BLOCK, LRU_BLOCK)) * LRU_BLOCK ** -0.5,
        "b_ig": nrm(ks[14], (LRU_BLOCKS, LRU_BLOCK)) * 0.01,
        "lru_lambda": jnp.log(sig) - jnp.log1p(-sig),
        "sinks": nrm(ks[15], (N_HEADS,)),
        "w_lru_out": nrm(ks[16], (LRU_WIDTH, D_MODEL)) * LRU_WIDTH ** -0.5 * BETA,
        "w_attn_out": nrm(ks[17], (N_HEADS * HEAD_DIM, D_MODEL)) * (N_HEADS * HEAD_DIM) ** -0.5 * BETA,
        "w_o": nrm(ks[18], (D_MODEL, D_MODEL)) * s_d * BETA,
        "ln1_g": 1.0 + 0.01 * nrm(ks[19], (D_MODEL,)),
        "ln1_b": 0.01 * nrm(ks[20], (D_MODEL,)),
        "w_group": nrm(ks[21], (D_MODEL, N_GROUPS)) * s_d,
        "b_group": 0.01 * nrm(ks[22], (N_GROUPS,)),
        "w_router": nrm(ks[23], (D_MODEL, N_EXPERTS)) * s_d,
        "b_router": 0.01 * nrm(ks[24], (N_EXPERTS,)),
        "w_gate": nrm(ks[25], (N_EXPERTS, D_MODEL, EXPERT_FF)) * s_d,
        "w_up": nrm(ks[26], (N_EXPERTS, D_MODEL, EXPERT_FF)) * s_d * BETA,
        "w_down": nrm(ks[27], (N_EXPERTS, EXPERT_FF, D_MODEL)) * EXPERT_FF ** -0.5 * BETA,
        "ln2_g": 1.0 + 0.01 * nrm(ks[28], (D_MODEL,)),
        "ln2_b": 0.01 * nrm(ks[29], (D_MODEL,)),
    }


def reference(x_prompt, x_sample, cache_k, cache_v, state_conv, state_lru_h, w_in, b_in, w_conv, b_conv,
              w_rg, b_rg, w_ig, b_ig, lru_lambda, sinks, w_lru_out, w_attn_out, w_o, ln1_g, ln1_b,
              w_group, b_group, w_router, b_router, w_gate, w_up, w_down, ln2_g, ln2_b):
    weights = (w_in, b_in, w_conv, b_conv, w_rg, b_rg, w_ig, b_ig, lru_lambda, sinks, w_lru_out, w_attn_out,
               w_o, ln1_g, ln1_b, w_group, b_group, w_router, b_router, w_gate, w_up, w_down, ln2_g, ln2_b)
    yp, ykp, yvp, ycp, yhp = x_prompt, None, None, None, None
    ys, yks, yvs, ycs, yhs = x_sample, None, None, None, None
    for _ in range(DEPTH):
        conv0 = jnp.zeros((yp.shape[0], CONV_W - 1, LRU_WIDTH), yp.dtype)
        h0 = jnp.zeros((yp.shape[0], LRU_WIDTH), yp.dtype)
        yp, ykp, yvp, ycp, yhp = trunk_layer(yp, conv0, h0, prompt_window_attention, *weights)
        sample_attn = functools.partial(sample_window_attention, k_buf=cache_k, v_buf=cache_v)
        ys, yks, yvs, ycs, yhs = trunk_layer(ys, state_conv, state_lru_h, sample_attn, *weights)
    return (yp, ys, ykp, yvp, ycp, yhp, yks, yvs, ycs, yhs)
```

```python
import functools

import jax
import jax.numpy as jnp
from jax import lax
from jax.experimental import pallas as pl
from jax.experimental.pallas import tpu as pltpu

F32 = jnp.float32
BF16 = jnp.bfloat16

D_MODEL = 1024
LRU_WIDTH = 1024
LRU_BLOCK = 64
CONV_W = 4
LRU_C = 8.0
N_HEADS = 16
N_KV = 4
GROUP = N_HEADS // N_KV
HEAD_DIM = 64
KV_WIDTH = N_KV * HEAD_DIM
WINDOW = 128
NEG_INF = -1e30
N_GROUPS = 4
EXPERTS_PER_GROUP = 8
N_EXPERTS = N_GROUPS * EXPERTS_PER_GROUP
TOP_K = 2
EXPERT_FF = D_MODEL // 2
DEPTH = 1
ALPHA = (2 * DEPTH) ** 0.25
LN_EPS = 1e-5
ATTN_SCALE = HEAD_DIM ** -0.5

OFF_XL = 0
OFF_YL = OFF_XL + LRU_WIDTH
OFF_Q = OFF_YL + LRU_WIDTH
OFF_K = OFF_Q + N_HEADS * HEAD_DIM
OFF_V = OFF_K + KV_WIDTH
OFF_GL = OFF_V + KV_WIDTH
OFF_GA = OFF_GL + D_MODEL
IN_WIDTH = OFF_GA + D_MODEL

LANES = 128
SUBLANES = 8
MXU_DIM = 256
VMEM_LIMIT_BYTES = 56 * 1024 * 1024

GATE_CHUNK = MXU_DIM
N_GATE_CHUNKS = LRU_WIDTH // GATE_CHUNK
ROUTE_WIDTH = LANES

SEQ_TILE = 256
MOE_TILE = 256
COMBINE_TILE = 128
SAMPLE_ATTN_TILE = 16
SAMPLE_PROJ_TILE = 512


def _const_spec(shape):
    nd = len(shape)
    return pl.BlockSpec(shape, lambda *_: (0,) * nd)


def _layer_norm(z, g, b):
    mu = jnp.mean(z, axis=-1, keepdims=True)
    zc = z - mu
    var = jnp.mean(zc * zc, axis=-1, keepdims=True)
    return zc * lax.rsqrt(var + LN_EPS) * g + b


def _sigmoid(x):
    return 1.0 / (1.0 + jnp.exp(-x))


def _softplus(x):
    return jnp.maximum(x, 0.0) + jnp.log1p(jnp.exp(-jnp.abs(x)))


def _bdot(a, b):
    return jnp.dot(a.astype(BF16), b.astype(BF16), preferred_element_type=F32)


def _lru_gates(xc, w_gates_ref, b_rg, b_ig, lam):
    xcb = xc.astype(BF16)
    r_parts, i_parts = [], []
    for c in range(N_GATE_CHUNKS):
        g = jnp.dot(xcb[:, c * GATE_CHUNK:(c + 1) * GATE_CHUNK], w_gates_ref[c], preferred_element_type=F32)
        r_parts.append(g[:, :GATE_CHUNK])
        i_parts.append(g[:, GATE_CHUNK:])
    r = _sigmoid(jnp.concatenate(r_parts, axis=1) + b_rg)
    i = _sigmoid(jnp.concatenate(i_parts, axis=1) + b_ig)
    log_a = (-LRU_C * r) * _softplus(-lam)
    a = jnp.exp(log_a)
    u = jnp.sqrt(1.0 - a * a) * (i * xc)
    return a, u


def _shift_rows(x, d, fill):
    n = x.shape[0]
    if d % SUBLANES == 0:
        return jnp.concatenate([jnp.full((d, x.shape[1]), fill, x.dtype), x[:n - d]], axis=0)
    rolled = pltpu.roll(x, d, axis=0)
    row = lax.broadcasted_iota(jnp.int32, x.shape, 0)
    return jnp.where(row < d, fill, rolled)


def _linear_scan(a, u):
    n = a.shape[0]
    d = 1
    while d < n:
        u = a * _shift_rows(u, d, 0.0) + u
        a = a * _shift_rows(a, d, 1.0)
        d *= 2
    return a, u


def _route(x1, w_hi_ref, w_lo_ref, b_rt):
    x_hi = x1.astype(BF16)
    x_lo = (x1 - x_hi.astype(F32)).astype(BF16)
    w_hi = w_hi_ref[...]
    logits = (jnp.dot(x_hi, w_hi, preferred_element_type=F32)
              + (jnp.dot(x_lo, w_hi, preferred_element_type=F32)
                 + jnp.dot(x_hi, w_lo_ref[...], preferred_element_type=F32))) + b_rt
    col = lax.broadcasted_iota(jnp.int32, logits.shape, 1)
    big = jnp.int32(ROUTE_WIDTH)
    is_g = col < N_GROUPS
    gl = jnp.where(is_g, logits, -jnp.inf)
    gmax = jnp.max(gl, axis=-1, keepdims=True)
    g_idx = jnp.min(jnp.where(gl == gmax, col, big), axis=-1, keepdims=True)
    p_g = 1.0 / jnp.sum(jnp.where(is_g, jnp.exp(gl - gmax), 0.0), axis=-1, keepdims=True)
    lo = N_GROUPS + g_idx * EXPERTS_PER_GROUP
    in_grp = (col >= lo) & (col < lo + EXPERTS_PER_GROUP)
    el = jnp.where(in_grp, logits, -jnp.inf)
    v1 = jnp.max(el, axis=-1, keepdims=True)
    i1 = jnp.min(jnp.where(el == v1, col, big), axis=-1, keepdims=True)
    el2 = jnp.where(col == i1, -jnp.inf, el)
    v2 = jnp.max(el2, axis=-1, keepdims=True)
    i2 = jnp.min(jnp.where(el2 == v2, col, big), axis=-1, keepdims=True)
    e21 = jnp.exp(v2 - v1)
    inv = 1.0 / (1.0 + e21)
    w1 = p_g * inv
    w2 = p_g * (e21 * inv)
    e1 = (i1 - N_GROUPS).astype(F32)
    e2 = (i2 - N_GROUPS).astype(F32)
    return jnp.where(col == 0, e1, jnp.where(col == 1, e2, jnp.where(col == 2, w1, jnp.where(col == 3, w2, 0.0))))


def _merge_norm_route(x, rec, att, g_l, g_a, w_lru_out_ref, w_attn_out_ref, w_o_ref, ln_g, ln_b,
                      w_rt_hi_ref, w_rt_lo_ref, b_rt):
    rec_o = jnp.dot(rec.astype(BF16), w_lru_out_ref[...], preferred_element_type=F32)
    att_o = jnp.dot(att.astype(BF16), w_attn_out_ref[...], preferred_element_type=F32)
    merged = _sigmoid(g_l) * rec_o + _sigmoid(g_a) * att_o
    mix = jnp.dot(merged.astype(BF16), w_o_ref[...], preferred_element_type=F32)
    x1 = _layer_norm(ALPHA * x + mix, ln_g, ln_b)
    return x1, _route(x1, w_rt_hi_ref, w_rt_lo_ref, b_rt)


def _mixer_kernel(sinks_ref, x_ref, w_in_ref, b_in_ref, w_conv_ref, b_conv_ref, w_gates_ref, b_rg_ref, b_ig_ref,
                  lam_ref, w_lru_out_ref, w_attn_out_ref, w_o_ref, ln_g_ref, ln_b_ref, w_rt_hi_ref, w_rt_lo_ref,
                  b_rt_ref, x1_s_ref, route_s_ref,
                  x1_ref, route_ref, kwin_ref, vwin_ref, conv_ref, h_ref,
                  conv_buf, h_carry, kcat, vcat, att_buf, *, tiles_per_seq, n_tiles):
    step = pl.program_id(0)

    @pl.when(step < n_tiles)
    def _():
        _mixer_tile(lax.rem(step, tiles_per_seq), sinks_ref, x_ref, w_in_ref, b_in_ref, w_conv_ref, b_conv_ref,
                    w_gates_ref, b_rg_ref, b_ig_ref, lam_ref, w_lru_out_ref, w_attn_out_ref, w_o_ref, ln_g_ref,
                    ln_b_ref, w_rt_hi_ref, w_rt_lo_ref, b_rt_ref, x1_ref, route_ref, kwin_ref, vwin_ref, conv_ref,
                    h_ref, conv_buf, h_carry, kcat, vcat, att_buf)

    @pl.when(step == n_tiles)
    def _():
        n_s = x1_s_ref.shape[0]
        x1_ref[0:n_s, :] = x1_s_ref[...]
        route_ref[0:n_s, :] = route_s_ref[...]


def _mixer_tile(t, sinks_ref, x_ref, w_in_ref, b_in_ref, w_conv_ref, b_conv_ref, w_gates_ref, b_rg_ref, b_ig_ref,
                lam_ref, w_lru_out_ref, w_attn_out_ref, w_o_ref, ln_g_ref, ln_b_ref, w_rt_hi_ref, w_rt_lo_ref,
                b_rt_ref, x1_ref, route_ref, kwin_ref, vwin_ref, conv_ref, h_ref,
                conv_buf, h_carry, kcat, vcat, att_buf):
    T = SEQ_TILE

    @pl.when(t == 0)
    def _():
        conv_buf[0:SUBLANES, :] = jnp.zeros((SUBLANES, LRU_WIDTH), F32)
        h_carry[...] = jnp.zeros_like(h_carry)
        kcat[0:WINDOW, :] = jnp.zeros((WINDOW, KV_WIDTH), BF16)
        vcat[0:WINDOW, :] = jnp.zeros((WINDOW, KV_WIDTH), BF16)

    x = x_ref[0]
    xb = x.astype(BF16)

    def proj(lo, width):
        return jnp.dot(xb, w_in_ref[:, lo:lo + width], preferred_element_type=F32) + b_in_ref[:, lo:lo + width]

    xl = proj(OFF_XL, LRU_WIDTH)
    conv_buf[SUBLANES:SUBLANES + T, :] = xl
    wc = w_conv_ref[...]
    xc = wc[0:1] * conv_buf[SUBLANES - 3:SUBLANES - 3 + T, :]
    xc = xc + wc[1:2] * conv_buf[SUBLANES - 2:SUBLANES - 2 + T, :]
    xc = xc + wc[2:3] * conv_buf[SUBLANES - 1:SUBLANES - 1 + T, :]
    xc = xc + wc[3:4] * xl + b_conv_ref[...]
    conv_ref[0] = conv_buf[T + SUBLANES - (CONV_W - 1):T + SUBLANES, :]
    conv_buf[0:SUBLANES, :] = conv_buf[T:T + SUBLANES, :]

    a, u = _lru_gates(xc, w_gates_ref, b_rg_ref[...], b_ig_ref[...], lam_ref[...])
    a_cum, h = _linear_scan(a, u)
    h = a_cum * h_carry[0:1, :] + h
    h_last = h[T - 1:T, :]
    h_carry[0:1, :] = h_last
    h_ref[0] = h_last
    rec = h * jax.nn.gelu(proj(OFF_YL, LRU_WIDTH))

    q = proj(OFF_Q, N_HEADS * HEAD_DIM)
    k = proj(OFF_K, KV_WIDTH)
    v = proj(OFF_V, KV_WIDTH)
    kwin_ref[0] = k[T - WINDOW:, :]
    vwin_ref[0] = v[T - WINDOW:, :]
    kcat[WINDOW:WINDOW + T, :] = k.astype(BF16)
    vcat[WINDOW:WINDOW + T, :] = v.astype(BF16)

    qi = lax.broadcasted_iota(jnp.int32, (WINDOW, 2 * WINDOW), 0)
    kj = lax.broadcasted_iota(jnp.int32, (WINDOW, 2 * WINDOW), 1)
    band = (kj > qi) & (kj <= qi + WINDOW)
    grp_row = lax.broadcasted_iota(jnp.int32, (GROUP * WINDOW, 1), 0) // WINDOW
    for qb in range(T // WINDOW):
        if qb == 0:
            first_key = jnp.where(t == 0, WINDOW, 0)
            mask1 = band & (kj >= first_key)
        else:
            mask1 = band
        mask = jnp.concatenate([mask1] * GROUP, axis=0)
        r0 = qb * WINDOW
        qq = q[r0:r0 + WINDOW, :]
        for j in range(N_KV):
            kjb = kcat[r0:r0 + 2 * WINDOW, j * HEAD_DIM:(j + 1) * HEAD_DIM]
            vjb = vcat[r0:r0 + 2 * WINDOW, j * HEAD_DIM:(j + 1) * HEAD_DIM]
            qs = jnp.concatenate(
                [qq[:, (j * GROUP + g) * HEAD_DIM:(j * GROUP + g + 1) * HEAD_DIM] for g in range(GROUP)], axis=0)
            s = lax.dot_general(qs.astype(BF16), kjb, (((1,), (1,)), ((), ())), preferred_element_type=F32)
            s = jnp.where(mask, s * ATTN_SCALE, NEG_INF)
            sink = jnp.zeros((GROUP * WINDOW, 1), F32)
            for g in range(GROUP):
                sink = jnp.where(grp_row == g, sinks_ref[j * GROUP + g], sink)
            m = jnp.maximum(jnp.max(s, axis=-1, keepdims=True), sink)
            p = jnp.exp(s - m)
            inv = 1.0 / (jnp.sum(p, axis=-1, keepdims=True) + jnp.exp(sink - m))
            o = jnp.dot((p * inv).astype(BF16), vjb, preferred_element_type=F32)
            for g in range(GROUP):
                hcol = (j * GROUP + g) * HEAD_DIM
                att_buf[r0:r0 + WINDOW, hcol:hcol + HEAD_DIM] = o[g * WINDOW:(g + 1) * WINDOW, :]
    kcat[0:WINDOW, :] = kcat[T:T + WINDOW, :]
    vcat[0:WINDOW, :] = vcat[T:T + WINDOW, :]

    x1, route = _merge_norm_route(x, rec, att_buf[...], proj(OFF_GL, D_MODEL), proj(OFF_GA, D_MODEL),
                                  w_lru_out_ref, w_attn_out_ref, w_o_ref, ln_g_ref[...], ln_b_ref[...],
                                  w_rt_hi_ref, w_rt_lo_ref, b_rt_ref[...])
    x1_ref[...] = x1
    route_ref[...] = route


def _mixer_call(x_prompt, x1_s, route_s, sinks, wts):
    B, S, _ = x_prompt.shape
    T = SEQ_TILE
    nt = S // T
    n_tiles = B * nt
    n_rows_total = B * S + x1_s.shape[0]
    weight_args = (wts["w_in"], wts["b_in"], wts["w_conv"], wts["b_conv"], wts["w_gates"], wts["b_rg"], wts["b_ig"],
                   wts["lam"], wts["w_lru_out"], wts["w_attn_out"], wts["w_o"], wts["ln1_g"], wts["ln1_b"],
                   wts["w_rt_hi"], wts["w_rt_lo"], wts["b_rt"], x1_s, route_s)
    seq = lambda i: jnp.minimum(i, n_tiles - 1) // nt
    in_specs = [pl.BlockSpec(memory_space=pltpu.SMEM),
                pl.BlockSpec((1, T, D_MODEL), lambda i: (seq(i), lax.rem(jnp.minimum(i, n_tiles - 1), nt), 0))]
    in_specs += [_const_spec(w.shape) for w in weight_args]
    out_shape = (
        jax.ShapeDtypeStruct((n_rows_total, D_MODEL), F32),
        jax.ShapeDtypeStruct((n_rows_total, ROUTE_WIDTH), F32),
        jax.ShapeDtypeStruct((B, WINDOW, KV_WIDTH), F32),
        jax.ShapeDtypeStruct((B, WINDOW, KV_WIDTH), F32),
        jax.ShapeDtypeStruct((B, CONV_W - 1, LRU_WIDTH), F32),
        jax.ShapeDtypeStruct((B, 1, LRU_WIDTH), F32),
    )
    out_specs = (
        pl.BlockSpec((T, D_MODEL), lambda i: (i, 0)),
        pl.BlockSpec((T, ROUTE_WIDTH), lambda i: (i, 0)),
        pl.BlockSpec((1, WINDOW, KV_WIDTH), lambda i: (seq(i), 0, 0)),
        pl.BlockSpec((1, WINDOW, KV_WIDTH), lambda i: (seq(i), 0, 0)),
        pl.BlockSpec((1, CONV_W - 1, LRU_WIDTH), lambda i: (seq(i), 0, 0)),
        pl.BlockSpec((1, 1, LRU_WIDTH), lambda i: (seq(i), 0, 0)),
    )
    scratch = [
        pltpu.VMEM((T + 2 * SUBLANES, LRU_WIDTH), F32),
        pltpu.VMEM((SUBLANES, LRU_WIDTH), F32),
        pltpu.VMEM((T + WINDOW, KV_WIDTH), BF16),
        pltpu.VMEM((T + WINDOW, KV_WIDTH), BF16),
        pltpu.VMEM((T, N_HEADS * HEAD_DIM), F32),
    ]
    return pl.pallas_call(
        functools.partial(_mixer_kernel, tiles_per_seq=nt, n_tiles=n_tiles),
        grid=(n_tiles + 1,),
        in_specs=in_specs,
        out_specs=out_specs,
        out_shape=out_shape,
        scratch_shapes=scratch,
        compiler_params=pltpu.CompilerParams(dimension_semantics=("arbitrary",),
                                             vmem_limit_bytes=VMEM_LIMIT_BYTES),
        name="mixer_prompt",
    )(sinks, x_prompt, *weight_args)


def _sample_proj_kernel(x_ref, w_ref, b_ref, u_ref):
    u_ref[...] = jnp.dot(x_ref[...].astype(BF16), w_ref[...], preferred_element_type=F32) + b_ref[...]


def _sample_proj_call(x_s, w_in, b_in):
    n = x_s.shape[0]
    tn = SAMPLE_PROJ_TILE
    return pl.pallas_call(
        _sample_proj_kernel,
        grid=(IN_WIDTH // tn,),
        in_specs=[pl.BlockSpec((n, D_MODEL), lambda c: (0, 0)),
                  pl.BlockSpec((D_MODEL, tn), lambda c: (0, c)),
                  pl.BlockSpec((1, tn), lambda c: (0, c))],
        out_specs=pl.BlockSpec((n, tn), lambda c: (0, c)),
        out_shape=jax.ShapeDtypeStruct((n, IN_WIDTH), F32),
        compiler_params=pltpu.CompilerParams(dimension_semantics=("arbitrary",)),
        name="sample_proj",
    )(x_s, w_in, b_in)


def _sample_attn_kernel(q_ref, kn_ref, vn_ref, kn_row_ref, vn_row_ref, ck_ref, cv_ref, sinks_ref,
                        att_ref, kwin_ref, vwin_ref):
    tb = q_ref.shape[0]
    key_pos = lax.broadcasted_iota(jnp.int32, (tb, GROUP, WINDOW), 2)
    for j in range(N_KV):
        qj = q_ref[:, j * GROUP:(j + 1) * GROUP, :]
        kc = ck_ref[:, :, j * HEAD_DIM:(j + 1) * HEAD_DIM]
        vc = cv_ref[:, :, j * HEAD_DIM:(j + 1) * HEAD_DIM]
        s_c = jnp.einsum("bgd,bsd->bgs", qj.astype(BF16), kc.astype(BF16), preferred_element_type=F32) * ATTN_SCALE
        s_c = jnp.where(key_pos >= 1, s_c, NEG_INF)
        kn = kn_ref[:, j:j + 1, :]
        vn = vn_ref[:, j:j + 1, :]
        s_n = jnp.sum(qj * kn, axis=-1, keepdims=True) * ATTN_SCALE
        sink = sinks_ref[j][None]
        m = jnp.maximum(jnp.maximum(jnp.max(s_c, axis=-1, keepdims=True), s_n), sink)
        p_c = jnp.exp(s_c - m)
        p_n = jnp.exp(s_n - m)
        inv = 1.0 / (jnp.sum(p_c, axis=-1, keepdims=True) + p_n + jnp.exp(sink - m))
        o = jnp.einsum("bgs,bsd->bgd", (p_c * inv).astype(BF16), vc.astype(BF16), preferred_element_type=F32)
        att_ref[:, j * GROUP:(j + 1) * GROUP, :] = o + (p_n * inv) * vn
    kwin_ref[:, 0:WINDOW - 1, :] = ck_ref[:, 1:WINDOW, :]
    kwin_ref[:, WINDOW - 1:WINDOW, :] = kn_row_ref[...]
    vwin_ref[:, 0:WINDOW - 1, :] = cv_ref[:, 1:WINDOW, :]
    vwin_ref[:, WINDOW - 1:WINDOW, :] = vn_row_ref[...]


def _sample_attn_call(q3, kn3, vn3, kn_row, vn_row, ck, cv, sinks3):
    n = q3.shape[0]
    tb = SAMPLE_ATTN_TILE
    b3 = lambda i: (i, 0, 0)
    return pl.pallas_call(
        _sample_attn_kernel,
        grid=(n // tb,),
        in_specs=[pl.BlockSpec((tb, N_HEADS, HEAD_DIM), b3),
                  pl.BlockSpec((tb, N_KV, HEAD_DIM), b3),
                  pl.BlockSpec((tb, N_KV, HEAD_DIM), b3),
                  pl.BlockSpec((tb, 1, KV_WIDTH), b3),
                  pl.BlockSpec((tb, 1, KV_WIDTH), b3),
                  pl.BlockSpec((tb, WINDOW, KV_WIDTH), b3),
                  pl.BlockSpec((tb, WINDOW, KV_WIDTH), b3),
                  pl.BlockSpec((N_KV, GROUP, 1), lambda i: (0, 0, 0))],
        out_specs=(pl.BlockSpec((tb, N_HEADS, HEAD_DIM), b3),
                   pl.BlockSpec((tb, WINDOW, KV_WIDTH), b3),
                   pl.BlockSpec((tb, WINDOW, KV_WIDTH), b3)),
        out_shape=(jax.ShapeDtypeStruct((n, N_HEADS, HEAD_DIM), F32),
                   jax.ShapeDtypeStruct((n, WINDOW, KV_WIDTH), F32),
                   jax.ShapeDtypeStruct((n, WINDOW, KV_WIDTH), F32)),
        compiler_params=pltpu.CompilerParams(dimension_semantics=("arbitrary",)),
        name="sample_attn",
    )(q3, kn3, vn3, kn_row, vn_row, ck, cv, sinks3)


def _sample_mix_kernel(x_ref, u_ref, att_ref, st_ref, h0_ref, w_conv_ref, b_conv_ref, w_gates_ref, b_rg_ref,
                       b_ig_ref, lam_ref, w_lru_out_ref, w_attn_out_ref, w_o_ref, ln_g_ref, ln_b_ref,
                       w_rt_hi_ref, w_rt_lo_ref, b_rt_ref,
                       x1_ref, route_ref, conv_ref, h_ref):
    xl = u_ref[:, OFF_XL:OFF_XL + LRU_WIDTH]
    wc = w_conv_ref[...]
    xc = wc[0:1] * st_ref[0]
    xc = xc + wc[1:2] * st_ref[1]
    xc = xc + wc[2:3] * st_ref[2]
    xc = xc + wc[3:4] * xl + b_conv_ref[...]
    conv_ref[0] = st_ref[1]
    conv_ref[1] = st_ref[2]
    conv_ref[2] = xl
    a, u = _lru_gates(xc, w_gates_ref, b_rg_ref[...], b_ig_ref[...], lam_ref[...])
    h = a * h0_ref[...] + u
    h_ref[...] = h
    rec = h * jax.nn.gelu(u_ref[:, OFF_YL:OFF_YL + LRU_WIDTH])
    x1, route = _merge_norm_route(x_ref[...], rec, att_ref[...], u_ref[:, OFF_GL:OFF_GL + D_MODEL],
                                  u_ref[:, OFF_GA:OFF_GA + D_MODEL], w_lru_out_ref, w_attn_out_ref, w_o_ref,
                                  ln_g_ref[...], ln_b_ref[...], w_rt_hi_ref, w_rt_lo_ref, b_rt_ref[...])
    x1_ref[...] = x1
    route_ref[...] = route


def _sample_mix_call(x_s, u_s, att, st_t, h0, wts):
    n = x_s.shape[0]
    weight_args = (wts["w_conv"], wts["b_conv"], wts["w_gates"], wts["b_rg"], wts["b_ig"], wts["lam"],
                   wts["w_lru_out"], wts["w_attn_out"], wts["w_o"], wts["ln1_g"], wts["ln1_b"],
                   wts["w_rt_hi"], wts["w_rt_lo"], wts["b_rt"])
    args = (x_s, u_s, att, st_t, h0) + weight_args
    out_shapes = ((n, D_MODEL), (n, ROUTE_WIDTH), (CONV_W - 1, n, LRU_WIDTH), (n, LRU_WIDTH))
    return pl.pallas_call(
        _sample_mix_kernel,
        grid=(1,),
        in_specs=[_const_spec(a.shape) for a in args],
        out_specs=tuple(_const_spec(s) for s in out_shapes),
        out_shape=tuple(jax.ShapeDtypeStruct(s, F32) for s in out_shapes),
        compiler_params=pltpu.CompilerParams(dimension_semantics=("arbitrary",),
                                             vmem_limit_bytes=VMEM_LIMIT_BYTES),
        name="sample_mix",
    )(*args)


def _row_copy(src_hbm, src_row, dst_vmem, dst_row, sem):
    return pltpu.make_async_copy(src_hbm.at[pl.ds(src_row, 1)], dst_vmem.at[pl.ds(dst_row, 1)], sem)


def _moe_kernel(block_e_ref, n_used_ref, slot_ref, x_hbm, wg_ref, wu_ref, wd_ref, ys_ref,
                xbuf, wg_bf, wu_bf, wd_bf, sem):
    b = pl.program_id(0)
    active = b < n_used_ref[0]

    @pl.when(active)
    def _():
        def issue(r, carry):
            _row_copy(x_hbm, slot_ref[0, 0, r], xbuf, r, sem).start()
            return carry
        lax.fori_loop(0, MOE_TILE, issue, 0)

        prev_e = block_e_ref[jnp.maximum(b - 1, 0)]
        @pl.when((b == 0) | (block_e_ref[b] != prev_e))
        def _():
            wg_bf[...] = wg_ref[0].astype(BF16)
            wu_bf[...] = wu_ref[0].astype(BF16)
            wd_bf[...] = wd_ref[0].astype(BF16)

        def wait(r, carry):
            _row_copy(x_hbm, 0, xbuf, r, sem).wait()
            return carry
        lax.fori_loop(0, MOE_TILE, wait, 0)

        xb = xbuf[...].astype(BF16)
        g = jnp.dot(xb, wg_bf[...], preferred_element_type=F32)
        u = jnp.dot(xb, wu_bf[...], preferred_element_type=F32)
        hmid = (g * _sigmoid(g)) * u
        ys_ref[...] = jnp.dot(hmid.astype(BF16), wd_bf[...], preferred_element_type=F32)

    @pl.when(jnp.logical_not(active))
    def _():
        ys_ref[...] = jnp.zeros_like(ys_ref)


def _moe_call(block_e, n_used, slot_t, x1_all, w_gate, w_up, w_down):
    n_blocks = block_e.shape[0]
    tm = MOE_TILE
    grid_spec = pltpu.PrefetchScalarGridSpec(
        num_scalar_prefetch=2,
        grid=(n_blocks,),
        in_specs=[
            pl.BlockSpec((1, 1, tm), lambda b, be, nu: (b, 0, 0), memory_space=pltpu.SMEM),
            pl.BlockSpec(memory_space=pl.ANY),
            pl.BlockSpec((1, D_MODEL, EXPERT_FF), lambda b, be, nu: (be[b], 0, 0)),
            pl.BlockSpec((1, D_MODEL, EXPERT_FF), lambda b, be, nu: (be[b], 0, 0)),
            pl.BlockSpec((1, EXPERT_FF, D_MODEL), lambda b, be, nu: (be[b], 0, 0)),
        ],
        out_specs=pl.BlockSpec((tm, D_MODEL), lambda b, be, nu: (b, 0)),
        scratch_shapes=[
            pltpu.VMEM((tm, D_MODEL), F32),
            pltpu.VMEM((D_MODEL, EXPERT_FF), BF16),
            pltpu.VMEM((D_MODEL, EXPERT_FF), BF16),
            pltpu.VMEM((EXPERT_FF, D_MODEL), BF16),
            pltpu.SemaphoreType.DMA(()),
        ],
    )
    return pl.pallas_call(
        _moe_kernel,
        grid_spec=grid_spec,
        out_shape=jax.ShapeDtypeStruct((n_blocks * tm, D_MODEL), F32),
        compiler_params=pltpu.CompilerParams(dimension_semantics=("arbitrary",),
                                             vmem_limit_bytes=VMEM_LIMIT_BYTES),
        name="moe_experts",
    )(block_e, n_used, slot_t, x1_all, w_gate, w_up, w_down)


def _combine_kernel(pos_ref, x1_ref, route_ref, ys_hbm, ln_g_ref, ln_b_ref, y_ref, ybuf, sem):
    n = x1_ref.shape[0]

    def issue(r, carry):
        for k in range(TOP_K):
            _row_copy(ys_hbm, pos_ref[0, 0, TOP_K * r + k], ybuf.at[k], r, sem).start()
        return carry
    lax.fori_loop(0, n, issue, 0)

    def wait(r, carry):
        for k in range(TOP_K):
            _row_copy(ys_hbm, 0, ybuf.at[k], r, sem).wait()
        return carry
    lax.fori_loop(0, n, wait, 0)

    route = route_ref[...]
    ff = route[:, 2:3] * ybuf[0] + route[:, 3:4] * ybuf[1]
    y_ref[...] = _layer_norm(ALPHA * x1_ref[...] + ff, ln_g_ref[...], ln_b_ref[...])


def _combine_call(pos, x1_all, route_all, ys, ln_g, ln_b, first_block, n_rows, tile):
    grid_spec = pl.GridSpec(
        grid=(n_rows // tile,),
        in_specs=[
            pl.BlockSpec((1, 1, TOP_K * tile), lambda i: (first_block + i, 0, 0), memory_space=pltpu.SMEM),
            pl.BlockSpec((tile, D_MODEL), lambda i: (first_block + i, 0)),
            pl.BlockSpec((tile, ROUTE_WIDTH), lambda i: (first_block + i, 0)),
            pl.BlockSpec(memory_space=pl.ANY),
            _const_spec(ln_g.shape),
            _const_spec(ln_b.shape),
        ],
        out_specs=pl.BlockSpec((tile, D_MODEL), lambda i: (i, 0)),
        scratch_shapes=[pltpu.VMEM((TOP_K, tile, D_MODEL), F32), pltpu.SemaphoreType.DMA(())],
    )
    return pl.pallas_call(
        _combine_kernel,
        grid_spec=grid_spec,
        out_shape=jax.ShapeDtypeStruct((n_rows, D_MODEL), F32),
        compiler_params=pltpu.CompilerParams(dimension_semantics=("arbitrary",)),
        name="moe_combine",
    )(pos, x1_all, route_all, ys, ln_g, ln_b)


def _dispatch_plan(route_all, n_blocks):
    n = route_all.shape[0]
    a_total = n * TOP_K
    flat_e = route_all[:, :TOP_K].astype(jnp.int32).reshape(a_total)
    order = jnp.argsort(flat_e, stable=True).astype(jnp.int32)
    se = flat_e[order]
    counts = jnp.zeros((N_EXPERTS,), jnp.int32).at[flat_e].add(1)
    padded = (counts + MOE_TILE - 1) // MOE_TILE * MOE_TILE
    pad_end = jnp.cumsum(padded)
    pad_start = pad_end - padded
    start = jnp.cumsum(counts) - counts
    dest = pad_start[se] + jnp.arange(a_total, dtype=jnp.int32) - start[se]
    slot_t = jnp.zeros((n_blocks * MOE_TILE,), jnp.int32).at[dest].set(order // TOP_K)
    pos = jnp.zeros((a_total,), jnp.int32).at[order].set(dest)
    n_used = (pad_end[-1] // MOE_TILE).astype(jnp.int32)
    blk = jnp.minimum(jnp.arange(n_blocks, dtype=jnp.int32), n_used - 1) * MOE_TILE
    block_e = jnp.minimum(jnp.searchsorted(pad_end, blk, side="right"), N_EXPERTS - 1).astype(jnp.int32)
    return block_e, n_used.reshape(1), slot_t.reshape(n_blocks, 1, MOE_TILE), pos


def _prepare_weights(w_in, b_in, w_conv, b_conv, w_rg, b_rg, w_ig, b_ig, lru_lambda, w_lru_out, w_attn_out, w_o,
                     ln1_g, ln1_b, w_group, b_group, w_router, b_router):
    blocks_per_chunk = GATE_CHUNK // LRU_BLOCK

    def chunked_block_diag(w):
        w = w.reshape(N_GATE_CHUNKS, blocks_per_chunk, LRU_BLOCK, LRU_BLOCK)
        eye = jnp.eye(blocks_per_chunk, dtype=w.dtype)
        return jnp.einsum("cbij,bd->cbidj", w, eye).reshape(N_GATE_CHUNKS, GATE_CHUNK, GATE_CHUNK)

    w_gates = jnp.concatenate([chunked_block_diag(w_rg), chunked_block_diag(w_ig)], axis=-1).astype(BF16)
    w_rt = jnp.concatenate([w_group, w_router], axis=1)
    w_rt = jnp.pad(w_rt, ((0, 0), (0, ROUTE_WIDTH - w_rt.shape[1])))
    w_rt_hi = w_rt.astype(BF16)
    w_rt_lo = (w_rt - w_rt_hi.astype(F32)).astype(BF16)
    b_rt = jnp.pad(jnp.concatenate([b_group, b_router]), (0, ROUTE_WIDTH - N_GROUPS - N_EXPERTS))
    row = lambda v: v.reshape(1, -1)
    return dict(
        w_in=w_in.astype(BF16), b_in=row(b_in), w_conv=w_conv, b_conv=row(b_conv), w_gates=w_gates,
        b_rg=row(b_rg), b_ig=row(b_ig), lam=row(lru_lambda),
        w_lru_out=w_lru_out.astype(BF16), w_attn_out=w_attn_out.astype(BF16), w_o=w_o.astype(BF16),
        ln1_g=row(ln1_g), ln1_b=row(ln1_b), w_rt_hi=w_rt_hi, w_rt_lo=w_rt_lo, b_rt=row(b_rt))


def kernel(x_prompt, x_sample, cache_k, cache_v, state_conv, state_lru_h, w_in, b_in, w_conv, b_conv, w_rg, b_rg,
           w_ig, b_ig, lru_lambda, sinks, w_lru_out, w_attn_out, w_o, ln1_g, ln1_b, w_group, b_group, w_router,
           b_router, w_gate, w_up, w_down, ln2_g, ln2_b):
    B, S, _ = x_prompt.shape
    n_prompt = B * S
    n_sample = x_sample.shape[0]
    n_all = n_prompt + n_sample
    wts = _prepare_weights(w_in, b_in, w_conv, b_conv, w_rg, b_rg, w_ig, b_ig, lru_lambda, w_lru_out, w_attn_out,
                           w_o, ln1_g, ln1_b, w_group, b_group, w_router, b_router)

    x_s = x_sample.reshape(n_sample, D_MODEL)
    u_s = _sample_proj_call(x_s, wts["w_in"], wts["b_in"])
    q3 = u_s[:, OFF_Q:OFF_K].reshape(n_sample, N_HEADS, HEAD_DIM)
    k_new = u_s[:, OFF_K:OFF_V]
    v_new = u_s[:, OFF_V:OFF_GL]
    att3, k_win_s, v_win_s = _sample_attn_call(
        q3, k_new.reshape(n_sample, N_KV, HEAD_DIM), v_new.reshape(n_sample, N_KV, HEAD_DIM),
        k_new.reshape(n_sample, 1, KV_WIDTH), v_new.reshape(n_sample, 1, KV_WIDTH),
        cache_k.reshape(n_sample, WINDOW, KV_WIDTH), cache_v.reshape(n_sample, WINDOW, KV_WIDTH),
        sinks.reshape(N_KV, GROUP, 1))
    x1_s, route_s, conv_s_t, h_s = _sample_mix_call(
        x_s, u_s, att3.reshape(n_sample, N_HEADS * HEAD_DIM), jnp.transpose(state_conv, (1, 0, 2)), state_lru_h, wts)

    x1_all, route_all, k_win_p, v_win_p, conv_p, h_p = _mixer_call(x_prompt, x1_s, route_s, sinks, wts)

    n_blocks = (n_all * TOP_K + MOE_TILE - 1) // MOE_TILE + N_EXPERTS
    block_e, n_used, slot_t, pos = _dispatch_plan(route_all, n_blocks)
    ys = _moe_call(block_e, n_used, slot_t, x1_all, w_gate, w_up, w_down)
    ln2_g2, ln2_b2 = ln2_g.reshape(1, -1), ln2_b.reshape(1, -1)
    tile = COMBINE_TILE
    pos_blocks = pos.reshape(n_all // tile, 1, TOP_K * tile)
    y_p = _combine_call(pos_blocks, x1_all, route_all, ys, ln2_g2, ln2_b2, 0, n_prompt, tile)
    y_s = _combine_call(pos_blocks, x1_all, route_all, ys, ln2_g2, ln2_b2, n_prompt // tile, n_sample, tile)

    kv_shape = (WINDOW, N_KV, HEAD_DIM)
    return (y_p.reshape(B, S, D_MODEL), y_s.reshape(n_sample, 1, D_MODEL),
            k_win_p.reshape((B,) + kv_shape), v_win_p.reshape((B,) + kv_shape), conv_p, h_p.reshape(B, LRU_WIDTH),
            k_win_s.reshape((n_sample,) + kv_shape), v_win_s.reshape((n_sample,) + kv_shape),
            jnp.transpose(conv_s_t, (1, 0, 2)), h_s)
```

```python
import functools

import jax
import jax.numpy as jnp
from jax import lax
from jax.experimental import pallas as pl
from jax.experimental.pallas import tpu as pltpu

F32 = jnp.float32
BF16 = jnp.bfloat16

D_MODEL = 1024
LRU_WIDTH = 1024
LRU_BLOCK = 64
CONV_W = 4
LRU_C = 8.0
N_HEADS = 16
N_KV = 4
GROUP = N_HEADS // N_KV
HEAD_DIM = 64
KV_WIDTH = N_KV * HEAD_DIM
WINDOW = 128
NEG_INF = -1e30
N_GROUPS = 4
EXPERTS_PER_GROUP = 8
N_EXPERTS = N_GROUPS * EXPERTS_PER_GROUP
TOP_K = 2
EXPERT_FF = D_MODEL // 2
DEPTH = 1
ALPHA = (2 * DEPTH) ** 0.25
LN_EPS = 1e-5
ATTN_SCALE = HEAD_DIM ** -0.5

OFF_XL = 0
OFF_YL = OFF_XL + LRU_WIDTH
OFF_Q = OFF_YL + LRU_WIDTH
OFF_K = OFF_Q + N_HEADS * HEAD_DIM
OFF_V = OFF_K + KV_WIDTH
OFF_GL = OFF_V + KV_WIDTH
OFF_GA = OFF_GL + D_MODEL
IN_WIDTH = OFF_GA + D_MODEL

LANES = 128
SUBLANES = 8
MXU_DIM = 256
VMEM_LIMIT_BYTES = 56 * 1024 * 1024

GATE_CHUNK = MXU_DIM
N_GATE_CHUNKS = LRU_WIDTH // GATE_CHUNK
ROUTE_WIDTH = LANES

SEQ_TILE = 256
MOE_TILE = 256
DISPATCH_TILE = 256
SLOT_CAP = 32
N_SLOTS = N_EXPERTS * SLOT_CAP
SEG_PAD = MOE_TILE + SLOT_CAP
SAMPLE_ATTN_TILE = 16
SAMPLE_PROJ_TILE = 512


def _const_spec(shape):
    nd = len(shape)
    return pl.BlockSpec(shape, lambda *_: (0,) * nd)


def _layer_norm(z, g, b):
    mu = jnp.mean(z, axis=-1, keepdims=True)
    zc = z - mu
    var = jnp.mean(zc * zc, axis=-1, keepdims=True)
    return zc * lax.rsqrt(var + LN_EPS) * g + b


def _sigmoid(x):
    return 1.0 / (1.0 + jnp.exp(-x))


def _softplus(x):
    return jnp.maximum(x, 0.0) + jnp.log1p(jnp.exp(-jnp.abs(x)))


def _bdot(a, b):
    return jnp.dot(a.astype(BF16), b.astype(BF16), preferred_element_type=F32)


def _lru_gates(xc, w_gates_ref, b_rg, b_ig, lam):
    xcb = xc.astype(BF16)
    r_parts, i_parts = [], []
    for c in range(N_GATE_CHUNKS):
        g = jnp.dot(xcb[:, c * GATE_CHUNK:(c + 1) * GATE_CHUNK], w_gates_ref[c], preferred_element_type=F32)
        r_parts.append(g[:, :GATE_CHUNK])
        i_parts.append(g[:, GATE_CHUNK:])
    r = _sigmoid(jnp.concatenate(r_parts, axis=1) + b_rg)
    i = _sigmoid(jnp.concatenate(i_parts, axis=1) + b_ig)
    log_a = (-LRU_C * r) * _softplus(-lam)
    a = jnp.exp(log_a)
    u = jnp.sqrt(1.0 - a * a) * (i * xc)
    return a, u


def _shift_rows(x, d, fill):
    n = x.shape[0]
    if d % SUBLANES == 0:
        return jnp.concatenate([jnp.full((d, x.shape[1]), fill, x.dtype), x[:n - d]], axis=0)
    rolled = pltpu.roll(x, d, axis=0)
    row = lax.broadcasted_iota(jnp.int32, x.shape, 0)
    return jnp.where(row < d, fill, rolled)


def _linear_scan(a, u):
    n = a.shape[0]
    d = 1
    while d < n:
        u = a * _shift_rows(u, d, 0.0) + u
        a = a * _shift_rows(a, d, 1.0)
        d *= 2
    return a, u


def _route(x1, w_hi_ref, w_lo_ref, b_rt):
    x_hi = x1.astype(BF16)
    x_lo = (x1 - x_hi.astype(F32)).astype(BF16)
    w_hi = w_hi_ref[...]
    logits = (jnp.dot(x_hi, w_hi, preferred_element_type=F32)
              + (jnp.dot(x_lo, w_hi, preferred_element_type=F32)
                 + jnp.dot(x_hi, w_lo_ref[...], preferred_element_type=F32))) + b_rt
    col = lax.broadcasted_iota(jnp.int32, logits.shape, 1)
    big = jnp.int32(ROUTE_WIDTH)
    is_g = col < N_GROUPS
    gl = jnp.where(is_g, logits, -jnp.inf)
    gmax = jnp.max(gl, axis=-1, keepdims=True)
    g_idx = jnp.min(jnp.where(gl == gmax, col, big), axis=-1, keepdims=True)
    p_g = 1.0 / jnp.sum(jnp.where(is_g, jnp.exp(gl - gmax), 0.0), axis=-1, keepdims=True)
    lo = N_GROUPS + g_idx * EXPERTS_PER_GROUP
    in_grp = (col >= lo) & (col < lo + EXPERTS_PER_GROUP)
    el = jnp.where(in_grp, logits, -jnp.inf)
    v1 = jnp.max(el, axis=-1, keepdims=True)
    i1 = jnp.min(jnp.where(el == v1, col, big), axis=-1, keepdims=True)
    el2 = jnp.where(col == i1, -jnp.inf, el)
    v2 = jnp.max(el2, axis=-1, keepdims=True)
    i2 = jnp.min(jnp.where(el2 == v2, col, big), axis=-1, keepdims=True)
    e21 = jnp.exp(v2 - v1)
    inv = 1.0 / (1.0 + e21)
    w1 = p_g * inv
    w2 = p_g * (e21 * inv)
    e1 = (i1 - N_GROUPS).astype(F32)
    e2 = (i2 - N_GROUPS).astype(F32)
    return jnp.where(col == 0, e1, jnp.where(col == 1, e2, jnp.where(col == 2, w1, jnp.where(col == 3, w2, 0.0))))


def _merge_norm_route(x, rec, att, g_l, g_a, w_lru_out_ref, w_attn_out_ref, w_o_ref, ln_g, ln_b,
                      w_rt_hi_ref, w_rt_lo_ref, b_rt):
    rec_o = jnp.dot(rec.astype(BF16), w_lru_out_ref[...], preferred_element_type=F32)
    att_o = jnp.dot(att.astype(BF16), w_attn_out_ref[...], preferred_element_type=F32)
    merged = _sigmoid(g_l) * rec_o + _sigmoid(g_a) * att_o
    mix = jnp.dot(merged.astype(BF16), w_o_ref[...], preferred_element_type=F32)
    x1 = _layer_norm(ALPHA * x + mix, ln_g, ln_b)
    return x1, _route(x1, w_rt_hi_ref, w_rt_lo_ref, b_rt)


def _mixer_kernel(sinks_ref, x_ref, w_in_ref, b_in_ref, w_conv_ref, b_conv_ref, w_gates_ref, b_rg_ref, b_ig_ref,
                  lam_ref, w_lru_out_ref, w_attn_out_ref, w_o_ref, ln_g_ref, ln_b_ref, w_rt_hi_ref, w_rt_lo_ref,
                  b_rt_ref, x1_s_ref, route_s_ref,
                  x1_ref, route_ref, kwin_ref, vwin_ref, conv_ref, h_ref,
                  conv_buf, h_carry, kcat, vcat, att_buf, *, tiles_per_seq, n_tiles):
    step = pl.program_id(0)

    @pl.when(step < n_tiles)
    def _():
        _mixer_tile(lax.rem(step, tiles_per_seq), sinks_ref, x_ref, w_in_ref, b_in_ref, w_conv_ref, b_conv_ref,
                    w_gates_ref, b_rg_ref, b_ig_ref, lam_ref, w_lru_out_ref, w_attn_out_ref, w_o_ref, ln_g_ref,
                    ln_b_ref, w_rt_hi_ref, w_rt_lo_ref, b_rt_ref, x1_ref, route_ref, kwin_ref, vwin_ref, conv_ref,
                    h_ref, conv_buf, h_carry, kcat, vcat, att_buf)

    @pl.when(step == n_tiles)
    def _():
        n_s = x1_s_ref.shape[0]
        x1_ref[0:n_s, :] = x1_s_ref[...]
        route_ref[0:n_s, :] = route_s_ref[...]


def _mixer_tile(t, sinks_ref, x_ref, w_in_ref, b_in_ref, w_conv_ref, b_conv_ref, w_gates_ref, b_rg_ref, b_ig_ref,
                lam_ref, w_lru_out_ref, w_attn_out_ref, w_o_ref, ln_g_ref, ln_b_ref, w_rt_hi_ref, w_rt_lo_ref,
                b_rt_ref, x1_ref, route_ref, kwin_ref, vwin_ref, conv_ref, h_ref,
                conv_buf, h_carry, kcat, vcat, att_buf):
    T = SEQ_TILE

    @pl.when(t == 0)
    def _():
        conv_buf[0:SUBLANES, :] = jnp.zeros((SUBLANES, LRU_WIDTH), F32)
        h_carry[...] = jnp.zeros_like(h_carry)
        kcat[0:WINDOW, :] = jnp.zeros((WINDOW, KV_WIDTH), BF16)
        vcat[0:WINDOW, :] = jnp.zeros((WINDOW, KV_WIDTH), BF16)

    x = x_ref[0]
    xb = x.astype(BF16)

    def proj(lo, width):
        return jnp.dot(xb, w_in_ref[:, lo:lo + width], preferred_element_type=F32) + b_in_ref[:, lo:lo + width]

    xl = proj(OFF_XL, LRU_WIDTH)
    conv_buf[SUBLANES:SUBLANES + T, :] = xl
    wc = w_conv_ref[...]
    xc = wc[0:1] * conv_buf[SUBLANES - 3:SUBLANES - 3 + T, :]
    xc = xc + wc[1:2] * conv_buf[SUBLANES - 2:SUBLANES - 2 + T, :]
    xc = xc + wc[2:3] * conv_buf[SUBLANES - 1:SUBLANES - 1 + T, :]
    xc = xc + wc[3:4] * xl + b_conv_ref[...]
    conv_ref[0] = conv_buf[T + SUBLANES - (CONV_W - 1):T + SUBLANES, :]
    conv_buf[0:SUBLANES, :] = conv_buf[T:T + SUBLANES, :]

    a, u = _lru_gates(xc, w_gates_ref, b_rg_ref[...], b_ig_ref[...], lam_ref[...])
    a_cum, h = _linear_scan(a, u)
    h = a_cum * h_carry[0:1, :] + h
    h_last = h[T - 1:T, :]
    h_carry[0:1, :] = h_last
    h_ref[0] = h_last
    rec = h * jax.nn.gelu(proj(OFF_YL, LRU_WIDTH))

    q = proj(OFF_Q, N_HEADS * HEAD_DIM)
    k = proj(OFF_K, KV_WIDTH)
    v = proj(OFF_V, KV_WIDTH)
    kwin_ref[0] = k[T - WINDOW:, :]
    vwin_ref[0] = v[T - WINDOW:, :]
    kcat[WINDOW:WINDOW + T, :] = k.astype(BF16)
    vcat[WINDOW:WINDOW + T, :] = v.astype(BF16)

    qi = lax.broadcasted_iota(jnp.int32, (WINDOW, 2 * WINDOW), 0)
    kj = lax.broadcasted_iota(jnp.int32, (WINDOW, 2 * WINDOW), 1)
    band = (kj > qi) & (kj <= qi + WINDOW)
    grp_row = lax.broadcasted_iota(jnp.int32, (GROUP * WINDOW, 1), 0) // WINDOW
    for qb in range(T // WINDOW):
        if qb == 0:
            first_key = jnp.where(t == 0, WINDOW, 0)
            mask1 = band & (kj >= first_key)
        else:
            mask1 = band
        mask = jnp.concatenate([mask1] * GROUP, axis=0)
        r0 = qb * WINDOW
        qq = q[r0:r0 + WINDOW, :]
        for j in range(N_KV):
            kjb = kcat[r0:r0 + 2 * WINDOW, j * HEAD_DIM:(j + 1) * HEAD_DIM]
            vjb = vcat[r0:r0 + 2 * WINDOW, j * HEAD_DIM:(j + 1) * HEAD_DIM]
            qs = jnp.concatenate(
                [qq[:, (j * GROUP + g) * HEAD_DIM:(j * GROUP + g + 1) * HEAD_DIM] for g in range(GROUP)], axis=0)
            s = lax.dot_general(qs.astype(BF16), kjb, (((1,), (1,)), ((), ())), preferred_element_type=F32)
            s = jnp.where(mask, s * ATTN_SCALE, NEG_INF)
            sink = jnp.zeros((GROUP * WINDOW, 1), F32)
            for g in range(GROUP):
                sink = jnp.where(grp_row == g, sinks_ref[j * GROUP + g], sink)
            m = jnp.maximum(jnp.max(s, axis=-1, keepdims=True), sink)
            p = jnp.exp(s - m)
            inv = 1.0 / (jnp.sum(p, axis=-1, keepdims=True) + jnp.exp(sink - m))
            o = jnp.dot((p * inv).astype(BF16), vjb, preferred_element_type=F32)
            for g in range(GROUP):
                hcol = (j * GROUP + g) * HEAD_DIM
                att_buf[r0:r0 + WINDOW, hcol:hcol + HEAD_DIM] = o[g * WINDOW:(g + 1) * WINDOW, :]
    kcat[0:WINDOW, :] = kcat[T:T + WINDOW, :]
    vcat[0:WINDOW, :] = vcat[T:T + WINDOW, :]

    x1, route = _merge_norm_route(x, rec, att_buf[...], proj(OFF_GL, D_MODEL), proj(OFF_GA, D_MODEL),
                                  w_lru_out_ref, w_attn_out_ref, w_o_ref, ln_g_ref[...], ln_b_ref[...],
                                  w_rt_hi_ref, w_rt_lo_ref, b_rt_ref[...])
    x1_ref[...] = x1
    route_ref[...] = route


def _mixer_call(x_prompt, x1_s, route_s, sinks, wts):
    B, S, _ = x_prompt.shape
    T = SEQ_TILE
    nt = S // T
    n_tiles = B * nt
    n_rows_total = B * S + x1_s.shape[0]
    weight_args = (wts["w_in"], wts["b_in"], wts["w_conv"], wts["b_conv"], wts["w_gates"], wts["b_rg"], wts["b_ig"],
                   wts["lam"], wts["w_lru_out"], wts["w_attn_out"], wts["w_o"], wts["ln1_g"], wts["ln1_b"],
                   wts["w_rt_hi"], wts["w_rt_lo"], wts["b_rt"], x1_s, route_s)
    seq = lambda i: jnp.minimum(i, n_tiles - 1) // nt
    in_specs = [pl.BlockSpec(memory_space=pltpu.SMEM),
                pl.BlockSpec((1, T, D_MODEL), lambda i: (seq(i), lax.rem(jnp.minimum(i, n_tiles - 1), nt), 0))]
    in_specs += [_const_spec(w.shape) for w in weight_args]
    out_shape = (
        jax.ShapeDtypeStruct((n_rows_total, D_MODEL), F32),
        jax.ShapeDtypeStruct((n_rows_total, ROUTE_WIDTH), F32),
        jax.ShapeDtypeStruct((B, WINDOW, KV_WIDTH), F32),
        jax.ShapeDtypeStruct((B, WINDOW, KV_WIDTH), F32),
        jax.ShapeDtypeStruct((B, CONV_W - 1, LRU_WIDTH), F32),
        jax.ShapeDtypeStruct((B, 1, LRU_WIDTH), F32),
    )
    out_specs = (
        pl.BlockSpec((T, D_MODEL), lambda i: (i, 0)),
        pl.BlockSpec((T, ROUTE_WIDTH), lambda i: (i, 0)),
        pl.BlockSpec((1, WINDOW, KV_WIDTH), lambda i: (seq(i), 0, 0)),
        pl.BlockSpec((1, WINDOW, KV_WIDTH), lambda i: (seq(i), 0, 0)),
        pl.BlockSpec((1, CONV_W - 1, LRU_WIDTH), lambda i: (seq(i), 0, 0)),
        pl.BlockSpec((1, 1, LRU_WIDTH), lambda i: (seq(i), 0, 0)),
    )
    scratch = [
        pltpu.VMEM((T + 2 * SUBLANES, LRU_WIDTH), F32),
        pltpu.VMEM((SUBLANES, LRU_WIDTH), F32),
        pltpu.VMEM((T + WINDOW, KV_WIDTH), BF16),
        pltpu.VMEM((T + WINDOW, KV_WIDTH), BF16),
        pltpu.VMEM((T, N_HEADS * HEAD_DIM), F32),
    ]
    return pl.pallas_call(
        functools.partial(_mixer_kernel, tiles_per_seq=nt, n_tiles=n_tiles),
        grid=(n_tiles + 1,),
        in_specs=in_specs,
        out_specs=out_specs,
        out_shape=out_shape,
        scratch_shapes=scratch,
        compiler_params=pltpu.CompilerParams(dimension_semantics=("arbitrary",),
                                             vmem_limit_bytes=VMEM_LIMIT_BYTES),
        name="mixer_prompt",
    )(sinks, x_prompt, *weight_args)


def _sample_proj_kernel(x_ref, w_ref, b_ref, u_ref):
    u_ref[...] = jnp.dot(x_ref[...].astype(BF16), w_ref[...], preferred_element_type=F32) + b_ref[...]


def _sample_proj_call(x_s, w_in, b_in):
    n = x_s.shape[0]
    tn = SAMPLE_PROJ_TILE
    return pl.pallas_call(
        _sample_proj_kernel,
        grid=(IN_WIDTH // tn,),
        in_specs=[pl.BlockSpec((n, D_MODEL), lambda c: (0, 0)),
                  pl.BlockSpec((D_MODEL, tn), lambda c: (0, c)),
                  pl.BlockSpec((1, tn), lambda c: (0, c))],
        out_specs=pl.BlockSpec((n, tn), lambda c: (0, c)),
        out_shape=jax.ShapeDtypeStruct((n, IN_WIDTH), F32),
        compiler_params=pltpu.CompilerParams(dimension_semantics=("arbitrary",)),
        name="sample_proj",
    )(x_s, w_in, b_in)


def _sample_attn_kernel(q_ref, kn_ref, vn_ref, kn_row_ref, vn_row_ref, ck_ref, cv_ref, sinks_ref,
                        att_ref, kwin_ref, vwin_ref):
    tb = q_ref.shape[0]
    key_pos = lax.broadcasted_iota(jnp.int32, (tb, GROUP, WINDOW), 2)
    for j in range(N_KV):
        qj = q_ref[:, j * GROUP:(j + 1) * GROUP, :]
        kc = ck_ref[:, :, j * HEAD_DIM:(j + 1) * HEAD_DIM]
        vc = cv_ref[:, :, j * HEAD_DIM:(j + 1) * HEAD_DIM]
        s_c = jnp.einsum("bgd,bsd->bgs", qj.astype(BF16), kc.astype(BF16), preferred_element_type=F32) * ATTN_SCALE
        s_c = jnp.where(key_pos >= 1, s_c, NEG_INF)
        kn = kn_ref[:, j:j + 1, :]
        vn = vn_ref[:, j:j + 1, :]
        s_n = jnp.sum(qj * kn, axis=-1, keepdims=True) * ATTN_SCALE
        sink = sinks_ref[j][None]
        m = jnp.maximum(jnp.maximum(jnp.max(s_c, axis=-1, keepdims=True), s_n), sink)
        p_c = jnp.exp(s_c - m)
        p_n = jnp.exp(s_n - m)
        inv = 1.0 / (jnp.sum(p_c, axis=-1, keepdims=True) + p_n + jnp.exp(sink - m))
        o = jnp.einsum("bgs,bsd->bgd", (p_c * inv).astype(BF16), vc.astype(BF16), preferred_element_type=F32)
        att_ref[:, j * GROUP:(j + 1) * GROUP, :] = o + (p_n * inv) * vn
    kwin_ref[:, 0:WINDOW - 1, :] = ck_ref[:, 1:WINDOW, :]
    kwin_ref[:, WINDOW - 1:WINDOW, :] = kn_row_ref[...]
    vwin_ref[:, 0:WINDOW - 1, :] = cv_ref[:, 1:WINDOW, :]
    vwin_ref[:, WINDOW - 1:WINDOW, :] = vn_row_ref[...]


def _sample_attn_call(q3, kn3, vn3, kn_row, vn_row, ck, cv, sinks3):
    n = q3.shape[0]
    tb = SAMPLE_ATTN_TILE
    b3 = lambda i: (i, 0, 0)
    return pl.pallas_call(
        _sample_attn_kernel,
        grid=(n // tb,),
        in_specs=[pl.BlockSpec((tb, N_HEADS, HEAD_DIM), b3),
                  pl.BlockSpec((tb, N_KV, HEAD_DIM), b3),
                  pl.BlockSpec((tb, N_KV, HEAD_DIM), b3),
                  pl.BlockSpec((tb, 1, KV_WIDTH), b3),
                  pl.BlockSpec((tb, 1, KV_WIDTH), b3),
                  pl.BlockSpec((tb, WINDOW, KV_WIDTH), b3),
                  pl.BlockSpec((tb, WINDOW, KV_WIDTH), b3),
                  pl.BlockSpec((N_KV, GROUP, 1), lambda i: (0, 0, 0))],
        out_specs=(pl.BlockSpec((tb, N_HEADS, HEAD_DIM), b3),
                   pl.BlockSpec((tb, WINDOW, KV_WIDTH), b3),
                   pl.BlockSpec((tb, WINDOW, KV_WIDTH), b3)),
        out_shape=(jax.ShapeDtypeStruct((n, N_HEADS, HEAD_DIM), F32),
                   jax.ShapeDtypeStruct((n, WINDOW, KV_WIDTH), F32),
                   jax.ShapeDtypeStruct((n, WINDOW, KV_WIDTH), F32)),
        compiler_params=pltpu.CompilerParams(dimension_semantics=("arbitrary",)),
        name="sample_attn",
    )(q3, kn3, vn3, kn_row, vn_row, ck, cv, sinks3)


def _sample_mix_kernel(x_ref, u_ref, att_ref, st_ref, h0_ref, w_conv_ref, b_conv_ref, w_gates_ref, b_rg_ref,
                       b_ig_ref, lam_ref, w_lru_out_ref, w_attn_out_ref, w_o_ref, ln_g_ref, ln_b_ref,
                       w_rt_hi_ref, w_rt_lo_ref, b_rt_ref,
                       x1_ref, route_ref, conv_ref, h_ref):
    xl = u_ref[:, OFF_XL:OFF_XL + LRU_WIDTH]
    wc = w_conv_ref[...]
    xc = wc[0:1] * st_ref[0]
    xc = xc + wc[1:2] * st_ref[1]
    xc = xc + wc[2:3] * st_ref[2]
    xc = xc + wc[3:4] * xl + b_conv_ref[...]
    conv_ref[0] = st_ref[1]
    conv_ref[1] = st_ref[2]
    conv_ref[2] = xl
    a, u = _lru_gates(xc, w_gates_ref, b_rg_ref[...], b_ig_ref[...], lam_ref[...])
    h = a * h0_ref[...] + u
    h_ref[...] = h
    rec = h * jax.nn.gelu(u_ref[:, OFF_YL:OFF_YL + LRU_WIDTH])
    x1, route = _merge_norm_route(x_ref[...], rec, att_ref[...], u_ref[:, OFF_GL:OFF_GL + D_MODEL],
                                  u_ref[:, OFF_GA:OFF_GA + D_MODEL], w_lru_out_ref, w_attn_out_ref, w_o_ref,
                                  ln_g_ref[...], ln_b_ref[...], w_rt_hi_ref, w_rt_lo_ref, b_rt_ref[...])
    x1_ref[...] = x1
    route_ref[...] = route


def _sample_mix_call(x_s, u_s, att, st_t, h0, wts):
    n = x_s.shape[0]
    weight_args = (wts["w_conv"], wts["b_conv"], wts["w_gates"], wts["b_rg"], wts["b_ig"], wts["lam"],
                   wts["w_lru_out"], wts["w_attn_out"], wts["w_o"], wts["ln1_g"], wts["ln1_b"],
                   wts["w_rt_hi"], wts["w_rt_lo"], wts["b_rt"])
    args = (x_s, u_s, att, st_t, h0) + weight_args
    out_shapes = ((n, D_MODEL), (n, ROUTE_WIDTH), (CONV_W - 1, n, LRU_WIDTH), (n, LRU_WIDTH))
    return pl.pallas_call(
        _sample_mix_kernel,
        grid=(1,),
        in_specs=[_const_spec(a.shape) for a in args],
        out_specs=tuple(_const_spec(s) for s in out_shapes),
        out_shape=tuple(jax.ShapeDtypeStruct(s, F32) for s in out_shapes),
        compiler_params=pltpu.CompilerParams(dimension_semantics=("arbitrary",),
                                             vmem_limit_bytes=VMEM_LIMIT_BYTES),
        name="sample_mix",
    )(*args)


def _one_hot(mask):
    return jnp.where(mask, 1.0, 0.0).astype(BF16)


def _dispatch_kernel(x1_ref, route_ref, xs_hbm, goff_ref, gtot_ref,
                     stage, zero_rows, tri, cnt_vmem, gvec, cnt_smem, g_smem, nd_smem, sem_stage, sem_cnt, sem_zero,
                     *, n_tokens, seg_cap):
    i = pl.program_id(0)
    last = pl.num_programs(0) - 1
    td = DISPATCH_TILE
    n_assign = TOP_K * td

    @pl.when(i == 0)
    def _():
        r = lax.broadcasted_iota(jnp.int32, (n_assign, n_assign), 0)
        c = lax.broadcasted_iota(jnp.int32, (n_assign, n_assign), 1)
        tri[...] = _one_hot(r < c)
        gvec[...] = jnp.zeros_like(gvec)
        nd_smem[0] = 0

        def clear(e, carry):
            g_smem[e] = 0
            return carry
        lax.fori_loop(0, N_EXPERTS, clear, 0)

    n_valid = n_tokens - i * td
    route = route_ref[...]
    valid_col = lax.broadcasted_iota(jnp.int32, (td, 1), 0) < n_valid
    lane = lax.broadcasted_iota(jnp.int32, (td, ROUTE_WIDTH), 1).astype(F32)
    e0c = jnp.where(valid_col, route[:, 0:1], -1.0)
    e1c = jnp.where(valid_col, route[:, 1:2], -1.0)
    cnt_row = jnp.sum(jnp.where(lane == e0c, 1.0, 0.0) + jnp.where(lane == e1c, 1.0, 0.0), axis=0, keepdims=True)
    goff_ref[0] = gvec[0:1, :].astype(jnp.int32)
    gvec[0:1, :] = gvec[0:1, :] + jnp.ceil(cnt_row * (1.0 / SUBLANES)) * SUBLANES
    gtot_ref[...] = gvec[0:1, :].astype(jnp.int32)
    cnt_vmem[0:1, :] = cnt_row.astype(jnp.int32)
    cnt_copy = pltpu.make_async_copy(cnt_vmem.at[pl.ds(0, 1)], cnt_smem, sem_cnt)
    cnt_copy.start()

    route_t = route.T
    valid_row = lax.broadcasted_iota(jnp.int32, (1, td), 1) < n_valid
    e_row = jnp.concatenate([jnp.where(valid_row, route_t[0:1, :], -1.0),
                             jnp.where(valid_row, route_t[1:2, :], -1.0)], axis=1)
    expert_sub = lax.broadcasted_iota(jnp.int32, (N_EXPERTS, n_assign), 0).astype(F32)
    onehot_t = expert_sub == e_row
    rank_t = jnp.dot(_one_hot(onehot_t), tri[...], preferred_element_type=F32)
    rank_row = jnp.sum(jnp.where(onehot_t, rank_t, 0.0), axis=0, keepdims=True)
    xb = jnp.where(valid_col, x1_ref[...], 0.0).astype(BF16)
    slot = lax.broadcasted_iota(jnp.int32, (N_SLOTS, td), 0).astype(F32)

    cnt_copy.wait()
    n_max = lax.fori_loop(0, N_EXPERTS, lambda e, m: jnp.maximum(m, cnt_smem[0, e]), 0)
    rounds = (n_max + SLOT_CAP - 1) // SLOT_CAP

    def stage_copy(e, dst_row):
        src = stage.at[pl.ds(pl.multiple_of(e * SLOT_CAP, SLOT_CAP), SLOT_CAP)]
        return pltpu.make_async_copy(src, xs_hbm.at[pl.ds(dst_row, SLOT_CAP)], sem_stage)

    def wait_outstanding():
        def wait_one(_, carry):
            stage_copy(0, 0).wait()
            return carry
        lax.fori_loop(0, nd_smem[0], wait_one, 0)
        nd_smem[0] = 0

    def round_body(r, carry):
        lo = r * SLOT_CAP
        lo_f = lo.astype(F32)
        in_round = (rank_row >= lo_f) & (rank_row < lo_f + SLOT_CAP) & (e_row >= 0.0)
        key = jnp.where(in_round, e_row * SLOT_CAP + (rank_row - lo_f), -1.0)
        perm = jnp.where(slot == key[:, :td], 1.0, jnp.where(slot == key[:, td:], 1.0, 0.0)).astype(BF16)
        sorted_rows = jnp.dot(perm, xb, preferred_element_type=F32)
        wait_outstanding()
        stage[...] = sorted_rows

        def issue(e, n_issued):
            has_rows = cnt_smem[0, e] > lo

            @pl.when(has_rows)
            def _():
                stage_copy(e, pl.multiple_of(e * seg_cap + g_smem[e] + lo, SUBLANES)).start()
            return n_issued + has_rows.astype(jnp.int32)
        nd_smem[0] = lax.fori_loop(0, N_EXPERTS, issue, 0)
        return carry
    lax.fori_loop(0, rounds, round_body, 0)

    def bump(e, carry):
        g_smem[e] = g_smem[e] + (cnt_smem[0, e] + SUBLANES - 1) // SUBLANES * SUBLANES
        return carry
    lax.fori_loop(0, N_EXPERTS, bump, 0)

    @pl.when(i == last)
    def _():
        wait_outstanding()
        zero_rows[...] = jnp.zeros_like(zero_rows)

        def pad_copy(e):
            dst_row = pl.multiple_of(e * seg_cap + g_smem[e], SUBLANES)
            return pltpu.make_async_copy(zero_rows, xs_hbm.at[pl.ds(dst_row, SEG_PAD)], sem_zero)

        def start(e, carry):
            pad_copy(e).start()
            return carry
        lax.fori_loop(0, N_EXPERTS, start, 0)

        def wait(e, carry):
            pad_copy(e).wait()
            return carry
        lax.fori_loop(0, N_EXPERTS, wait, 0)


def _dispatch_tiles(n_tokens):
    return (n_tokens + DISPATCH_TILE - 1) // DISPATCH_TILE


def _segment_capacity(n_tokens):
    alignment_slack = (SUBLANES - 1) * _dispatch_tiles(n_tokens)
    return (n_tokens + alignment_slack + SEG_PAD + MOE_TILE - 1) // MOE_TILE * MOE_TILE


def _dispatch_call(x1_all, route_all):
    n_tokens = x1_all.shape[0]
    td = DISPATCH_TILE
    n_tiles = (n_tokens + td - 1) // td
    seg_cap = _segment_capacity(n_tokens)
    n_assign = TOP_K * td
    return pl.pallas_call(
        functools.partial(_dispatch_kernel, n_tokens=n_tokens, seg_cap=seg_cap),
        grid=(n_tiles,),
        in_specs=[pl.BlockSpec((td, D_MODEL), lambda i: (i, 0)),
                  pl.BlockSpec((td, ROUTE_WIDTH), lambda i: (i, 0))],
        out_specs=(pl.BlockSpec(memory_space=pl.ANY),
                   pl.BlockSpec((1, 1, ROUTE_WIDTH), lambda i: (i, 0, 0)),
                   pl.BlockSpec((1, ROUTE_WIDTH), lambda i: (0, 0))),
        out_shape=(jax.ShapeDtypeStruct((N_EXPERTS * seg_cap, D_MODEL), F32),
                   jax.ShapeDtypeStruct((n_tiles, 1, ROUTE_WIDTH), jnp.int32),
                   jax.ShapeDtypeStruct((1, ROUTE_WIDTH), jnp.int32)),
        scratch_shapes=[
            pltpu.VMEM((N_SLOTS, D_MODEL), F32),
            pltpu.VMEM((SEG_PAD, D_MODEL), F32),
            pltpu.VMEM((n_assign, n_assign), BF16),
            pltpu.VMEM((SUBLANES, ROUTE_WIDTH), jnp.int32),
            pltpu.VMEM((SUBLANES, ROUTE_WIDTH), F32),
            pltpu.SMEM((1, ROUTE_WIDTH), jnp.int32),
            pltpu.SMEM((N_EXPERTS,), jnp.int32),
            pltpu.SMEM((1,), jnp.int32),
            pltpu.SemaphoreType.DMA(()),
            pltpu.SemaphoreType.DMA(()),
            pltpu.SemaphoreType.DMA(()),
        ],
        compiler_params=pltpu.CompilerParams(dimension_semantics=("arbitrary",),
                                             vmem_limit_bytes=VMEM_LIMIT_BYTES),
        name="moe_dispatch",
    )(x1_all, route_all)


def _moe_kernel(blk_e_ref, blk_j_ref, n_used_ref, xs_ref, wg_ref, wu_ref, wd_ref, ys_ref, wg_bf, wu_bf, wd_bf):
    del blk_j_ref
    b = pl.program_id(0)

    @pl.when(b < n_used_ref[0])
    def _():
        prev_e = blk_e_ref[jnp.maximum(b - 1, 0)]

        @pl.when((b == 0) | (blk_e_ref[b] != prev_e))
        def _():
            wg_bf[...] = wg_ref[0].astype(BF16)
            wu_bf[...] = wu_ref[0].astype(BF16)
            wd_bf[...] = wd_ref[0].astype(BF16)

        xb = xs_ref[...].astype(BF16)
        g = jnp.dot(xb, wg_bf[...], preferred_element_type=F32)
        u = jnp.dot(xb, wu_bf[...], preferred_element_type=F32)
        hmid = (g * _sigmoid(g)) * u
        ys_ref[...] = jnp.dot(hmid.astype(BF16), wd_bf[...], preferred_element_type=F32)


def _expert_blocks(gtot, n_blocks):
    rows = gtot[0, :N_EXPERTS]
    nb = (rows + SLOT_CAP + MOE_TILE - 1) // MOE_TILE
    ends = jnp.cumsum(nb)
    n_used = ends[-1]
    step = jnp.minimum(jnp.arange(n_blocks, dtype=jnp.int32), n_used - 1)
    blk_e = jnp.searchsorted(ends, step, side="right").astype(jnp.int32)
    blk_j = step - (ends - nb)[blk_e]
    return blk_e, blk_j.astype(jnp.int32), n_used.reshape(1).astype(jnp.int32)


def _moe_call(gtot, xs, w_gate, w_up, w_down, n_tokens):
    seg_cap = _segment_capacity(n_tokens)
    seg_blocks = seg_cap // MOE_TILE
    max_rows = n_tokens * TOP_K + N_EXPERTS * (SUBLANES - 1) * _dispatch_tiles(n_tokens)
    n_blocks = (max_rows + N_EXPERTS * (SLOT_CAP + MOE_TILE - 1)) // MOE_TILE
    blk_e, blk_j, n_used = _expert_blocks(gtot, n_blocks)
    row_block = lambda b, be, bj, nu: (be[b] * seg_blocks + bj[b], 0)
    weight_block = lambda b, be, bj, nu: (be[b], 0, 0)
    grid_spec = pltpu.PrefetchScalarGridSpec(
        num_scalar_prefetch=3,
        grid=(n_blocks,),
        in_specs=[
            pl.BlockSpec((MOE_TILE, D_MODEL), row_block),
            pl.BlockSpec((1, D_MODEL, EXPERT_FF), weight_block),
            pl.BlockSpec((1, D_MODEL, EXPERT_FF), weight_block),
            pl.BlockSpec((1, EXPERT_FF, D_MODEL), weight_block),
        ],
        out_specs=pl.BlockSpec((MOE_TILE, D_MODEL), row_block),
        scratch_shapes=[
            pltpu.VMEM((D_MODEL, EXPERT_FF), BF16),
            pltpu.VMEM((D_MODEL, EXPERT_FF), BF16),
            pltpu.VMEM((EXPERT_FF, D_MODEL), BF16),
        ],
    )
    return pl.pallas_call(
        _moe_kernel,
        grid_spec=grid_spec,
        out_shape=jax.ShapeDtypeStruct(xs.shape, F32),
        compiler_params=pltpu.CompilerParams(dimension_semantics=("arbitrary",),
                                             vmem_limit_bytes=VMEM_LIMIT_BYTES),
        name="moe_experts",
    )(blk_e, blk_j, n_used, xs, w_gate, w_up, w_down)


def _combine_kernel(gcur_ref, gnext_ref, x1_ref, route_ref, ys_hbm, ln_g_ref, ln_b_ref, y_ref,
                    stage, acc, tri, sem, *, seg_cap):
    td = x1_ref.shape[0]
    n_assign = TOP_K * td

    @pl.when(pl.program_id(0) == 0)
    def _():
        r = lax.broadcasted_iota(jnp.int32, (n_assign, n_assign), 0)
        c = lax.broadcasted_iota(jnp.int32, (n_assign, n_assign), 1)
        tri[...] = _one_hot(c < r)
        stage[...] = jnp.zeros_like(stage)

    def count(e):
        return gnext_ref[0, 0, e] - gcur_ref[0, 0, e]

    n_max = lax.fori_loop(0, N_EXPERTS, lambda e, m: jnp.maximum(m, count(e)), 0)
    rounds = (n_max + SLOT_CAP - 1) // SLOT_CAP

    route = route_ref[...]
    e0, e1, w0, w1 = route[:, 0:1], route[:, 1:2], route[:, 2:3], route[:, 3:4]
    lane = lax.broadcasted_iota(jnp.int32, (td, ROUTE_WIDTH), 1).astype(F32)
    onehot = jnp.concatenate([lane == e0, lane == e1], axis=0)
    onehot_f = jnp.where(onehot, 1.0, 0.0)
    rank_mat = jnp.dot(tri[...], onehot_f.astype(BF16), preferred_element_type=F32)
    rank = jnp.sum(jnp.where(onehot, rank_mat, 0.0), axis=1, keepdims=True)
    r0, r1 = rank[:td], rank[td:]
    cnt_row = jnp.sum(onehot_f, axis=0, keepdims=True)

    slot_col = lax.broadcasted_iota(jnp.int32, (N_SLOTS, 1), 0)
    slot_expert = slot_col // SLOT_CAP
    slot_rank = slot_col - slot_expert * SLOT_CAP
    lane_s = lax.broadcasted_iota(jnp.int32, (N_SLOTS, ROUTE_WIDTH), 1)
    n_col = jnp.sum(jnp.where(lane_s == slot_expert, cnt_row, 0.0), axis=1, keepdims=True)
    slot_lane = lax.broadcasted_iota(jnp.int32, (td, N_SLOTS), 1).astype(F32)

    def stage_copy(e, src_row):
        dst = stage.at[pl.ds(pl.multiple_of(e * SLOT_CAP, SLOT_CAP), SLOT_CAP)]
        return pltpu.make_async_copy(ys_hbm.at[pl.ds(src_row, SLOT_CAP)], dst, sem)

    acc[...] = jnp.zeros_like(acc)

    def round_body(r, carry):
        lo = r * SLOT_CAP

        def issue(e, n_issued):
            has_rows = count(e) > lo

            @pl.when(has_rows)
            def _():
                stage_copy(e, pl.multiple_of(e * seg_cap + gcur_ref[0, 0, e] + lo, SUBLANES)).start()
            return n_issued + has_rows.astype(jnp.int32)
        n_issued = lax.fori_loop(0, N_EXPERTS, issue, 0)

        def wait_one(_, c):
            stage_copy(0, 0).wait()
            return c
        lax.fori_loop(0, n_issued, wait_one, 0)

        lo_f = lo.astype(F32)
        live = (slot_rank + lo).astype(F32) < n_col
        rows = jnp.where(live, stage[...], 0.0).astype(BF16)

        def gathered(e_col, r_col):
            in_round = (r_col >= lo_f) & (r_col < lo_f + SLOT_CAP)
            key = jnp.where(in_round, e_col * SLOT_CAP + (r_col - lo_f), -1.0)
            return jnp.dot(_one_hot(slot_lane == key), rows, preferred_element_type=F32)
        acc[...] = acc[...] + (w0 * gathered(e0, r0) + w1 * gathered(e1, r1))
        return carry
    lax.fori_loop(0, rounds, round_body, 0)

    y_ref[...] = _layer_norm(ALPHA * x1_ref[...] + acc[...], ln_g_ref[...], ln_b_ref[...])


def _combine_call(goff, x1_all, route_all, ys, ln_g, ln_b, first_row, n_rows, tile, n_tokens):
    assert first_row % DISPATCH_TILE == 0 and first_row % tile == 0
    assert tile == DISPATCH_TILE or n_rows == tile
    first_block = first_row // tile
    first_goff = first_row // DISPATCH_TILE
    n_assign = TOP_K * tile
    grid_spec = pl.GridSpec(
        grid=(n_rows // tile,),
        in_specs=[
            pl.BlockSpec((1, 1, ROUTE_WIDTH), lambda i: (first_goff + i, 0, 0), memory_space=pltpu.SMEM),
            pl.BlockSpec((1, 1, ROUTE_WIDTH), lambda i: (first_goff + i + 1, 0, 0), memory_space=pltpu.SMEM),
            pl.BlockSpec((tile, D_MODEL), lambda i: (first_block + i, 0)),
            pl.BlockSpec((tile, ROUTE_WIDTH), lambda i: (first_block + i, 0)),
            pl.BlockSpec(memory_space=pl.ANY),
            _const_spec(ln_g.shape),
            _const_spec(ln_b.shape),
        ],
        out_specs=pl.BlockSpec((tile, D_MODEL), lambda i: (i, 0)),
        scratch_shapes=[pltpu.VMEM((N_SLOTS, D_MODEL), F32),
                        pltpu.VMEM((tile, D_MODEL), F32),
                        pltpu.VMEM((n_assign, n_assign), BF16),
                        pltpu.SemaphoreType.DMA(())],
    )
    return pl.pallas_call(
        functools.partial(_combine_kernel, seg_cap=_segment_capacity(n_tokens)),
        grid_spec=grid_spec,
        out_shape=jax.ShapeDtypeStruct((n_rows, D_MODEL), F32),
        compiler_params=pltpu.CompilerParams(dimension_semantics=("arbitrary",),
                                             vmem_limit_bytes=VMEM_LIMIT_BYTES),
        name="moe_combine",
    )(goff, goff, x1_all, route_all, ys, ln_g, ln_b)


def _prepare_weights(w_in, b_in, w_conv, b_conv, w_rg, b_rg, w_ig, b_ig, lru_lambda, w_lru_out, w_attn_out, w_o,
                     ln1_g, ln1_b, w_group, b_group, w_router, b_router):
    blocks_per_chunk = GATE_CHUNK // LRU_BLOCK

    def chunked_block_diag(w):
        w = w.reshape(N_GATE_CHUNKS, blocks_per_chunk, LRU_BLOCK, LRU_BLOCK)
        eye = jnp.eye(blocks_per_chunk, dtype=w.dtype)
        return jnp.einsum("cbij,bd->cbidj", w, eye).reshape(N_GATE_CHUNKS, GATE_CHUNK, GATE_CHUNK)

    w_gates = jnp.concatenate([chunked_block_diag(w_rg), chunked_block_diag(w_ig)], axis=-1).astype(BF16)
    w_rt = jnp.concatenate([w_group, w_router], axis=1)
    w_rt = jnp.pad(w_rt, ((0, 0), (0, ROUTE_WIDTH - w_rt.shape[1])))
    w_rt_hi = w_rt.astype(BF16)
    w_rt_lo = (w_rt - w_rt_hi.astype(F32)).astype(BF16)
    b_rt = jnp.pad(jnp.concatenate([b_group, b_router]), (0, ROUTE_WIDTH - N_GROUPS - N_EXPERTS))
    row = lambda v: v.reshape(1, -1)
    return dict(
        w_in=w_in.astype(BF16), b_in=row(b_in), w_conv=w_conv, b_conv=row(b_conv), w_gates=w_gates,
        b_rg=row(b_rg), b_ig=row(b_ig), lam=row(lru_lambda),
        w_lru_out=w_lru_out.astype(BF16), w_attn_out=w_attn_out.astype(BF16), w_o=w_o.astype(BF16),
        ln1_g=row(ln1_g), ln1_b=row(ln1_b), w_rt_hi=w_rt_hi, w_rt_lo=w_rt_lo, b_rt=row(b_rt))


def kernel(x_prompt, x_sample, cache_k, cache_v, state_conv, state_lru_h, w_in, b_in, w_conv, b_conv, w_rg, b_rg,
           w_ig, b_ig, lru_lambda, sinks, w_lru_out, w_attn_out, w_o, ln1_g, ln1_b, w_group, b_group, w_router,
           b_router, w_gate, w_up, w_down, ln2_g, ln2_b):
    B, S, _ = x_prompt.shape
    n_prompt = B * S
    n_sample = x_sample.shape[0]
    n_all = n_prompt + n_sample
    wts = _prepare_weights(w_in, b_in, w_conv, b_conv, w_rg, b_rg, w_ig, b_ig, lru_lambda, w_lru_out, w_attn_out,
                           w_o, ln1_g, ln1_b, w_group, b_group, w_router, b_router)

    x_s = x_sample.reshape(n_sample, D_MODEL)
    u_s = _sample_proj_call(x_s, wts["w_in"], wts["b_in"])
    q3 = u_s[:, OFF_Q:OFF_K].reshape(n_sample, N_HEADS, HEAD_DIM)
    k_new = u_s[:, OFF_K:OFF_V]
    v_new = u_s[:, OFF_V:OFF_GL]
    att3, k_win_s, v_win_s = _sample_attn_call(
        q3, k_new.reshape(n_sample, N_KV, HEAD_DIM), v_new.reshape(n_sample, N_KV, HEAD_DIM),
        k_new.reshape(n_sample, 1, KV_WIDTH), v_new.reshape(n_sample, 1, KV_WIDTH),
        cache_k.reshape(n_sample, WINDOW, KV_WIDTH), cache_v.reshape(n_sample, WINDOW, KV_WIDTH),
        sinks.reshape(N_KV, GROUP, 1))
    x1_s, route_s, conv_s_t, h_s = _sample_mix_call(
        x_s, u_s, att3.reshape(n_sample, N_HEADS * HEAD_DIM), jnp.transpose(state_conv, (1, 0, 2)), state_lru_h, wts)

    x1_all, route_all, k_win_p, v_win_p, conv_p, h_p = _mixer_call(x_prompt, x1_s, route_s, sinks, wts)

    xs, goff, gtot = _dispatch_call(x1_all, route_all)
    ys = _moe_call(gtot, xs, w_gate, w_up, w_down, n_all)
    goff = jnp.concatenate([goff, gtot[None]], axis=0)
    ln2_g2, ln2_b2 = ln2_g.reshape(1, -1), ln2_b.reshape(1, -1)
    y_p = _combine_call(goff, x1_all, route_all, ys, ln2_g2, ln2_b2, 0, n_prompt, DISPATCH_TILE, n_all)
    y_s = _combine_call(goff, x1_all, route_all, ys, ln2_g2, ln2_b2, n_prompt, n_sample, n_sample, n_all)

    kv_shape = (WINDOW, N_KV, HEAD_DIM)
    return (y_p.reshape(B, S, D_MODEL), y_s.reshape(n_sample, 1, D_MODEL),
            k_win_p.reshape((B,) + kv_shape), v_win_p.reshape((B,) + kv_shape), conv_p, h_p.reshape(B, LRU_WIDTH),
            k_win_s.reshape((n_sample,) + kv_shape), v_win_s.reshape((n_sample,) + kv_shape),
            jnp.transpose(conv_s_t, (1, 0, 2)), h_s)
```

```python
import functools

import jax
import jax.numpy as jnp
from jax import lax
from jax.experimental import pallas as pl
from jax.experimental.pallas import tpu as pltpu

F32 = jnp.float32
BF16 = jnp.bfloat16

D_MODEL = 1024
LRU_WIDTH = 1024
LRU_BLOCK = 64
CONV_W = 4
LRU_C = 8.0
N_HEADS = 16
N_KV = 4
GROUP = N_HEADS // N_KV
HEAD_DIM = 64
KV_WIDTH = N_KV * HEAD_DIM
WINDOW = 128
NEG_INF = -1e30
N_GROUPS = 4
EXPERTS_PER_GROUP = 8
N_EXPERTS = N_GROUPS * EXPERTS_PER_GROUP
TOP_K = 2
EXPERT_FF = D_MODEL // 2
DEPTH = 1
ALPHA = (2 * DEPTH) ** 0.25
LN_EPS = 1e-5
ATTN_SCALE = HEAD_DIM ** -0.5

OFF_XL = 0
OFF_YL = OFF_XL + LRU_WIDTH
OFF_Q = OFF_YL + LRU_WIDTH
OFF_K = OFF_Q + N_HEADS * HEAD_DIM
OFF_V = OFF_K + KV_WIDTH
OFF_GL = OFF_V + KV_WIDTH
OFF_GA = OFF_GL + D_MODEL
IN_WIDTH = OFF_GA + D_MODEL

LANES = 128
SUBLANES = 8
MXU_DIM = 256
VMEM_LIMIT_BYTES = 56 * 1024 * 1024

GATE_CHUNK = MXU_DIM
N_GATE_CHUNKS = LRU_WIDTH // GATE_CHUNK
ROUTE_WIDTH = LANES

SEQ_TILE = 256
MOE_TILE = 256
DISPATCH_TILE = 256
SLOT_CAP = 32
N_SLOTS = N_EXPERTS * SLOT_CAP
SEG_PAD = MOE_TILE + SLOT_CAP
PACKED_WIDTH = D_MODEL // 2
SAMPLE_ATTN_TILE = 16
SAMPLE_PROJ_TILE = 512


def _const_spec(shape):
    nd = len(shape)
    return pl.BlockSpec(shape, lambda *_: (0,) * nd)


def _layer_norm(z, g, b):
    mu = jnp.mean(z, axis=-1, keepdims=True)
    zc = z - mu
    var = jnp.mean(zc * zc, axis=-1, keepdims=True)
    return zc * lax.rsqrt(var + LN_EPS) * g + b


def _sigmoid(x):
    return 1.0 / (1.0 + jnp.exp(-x))


def _softplus(x):
    return jnp.maximum(x, 0.0) + jnp.log1p(jnp.exp(-jnp.abs(x)))


def _bdot(a, b):
    return jnp.dot(a.astype(BF16), b.astype(BF16), preferred_element_type=F32)


def _lru_gates(xc, w_gates_ref, b_rg, b_ig, lam):
    xcb = xc.astype(BF16)
    r_parts, i_parts = [], []
    for c in range(N_GATE_CHUNKS):
        g = jnp.dot(xcb[:, c * GATE_CHUNK:(c + 1) * GATE_CHUNK], w_gates_ref[c], preferred_element_type=F32)
        r_parts.append(g[:, :GATE_CHUNK])
        i_parts.append(g[:, GATE_CHUNK:])
    r = _sigmoid(jnp.concatenate(r_parts, axis=1) + b_rg)
    i = _sigmoid(jnp.concatenate(i_parts, axis=1) + b_ig)
    log_a = (-LRU_C * r) * _softplus(-lam)
    a = jnp.exp(log_a)
    u = jnp.sqrt(1.0 - a * a) * (i * xc)
    return a, u


def _shift_rows(x, d, fill):
    n = x.shape[0]
    if d % SUBLANES == 0:
        return jnp.concatenate([jnp.full((d, x.shape[1]), fill, x.dtype), x[:n - d]], axis=0)
    rolled = pltpu.roll(x, d, axis=0)
    row = lax.broadcasted_iota(jnp.int32, x.shape, 0)
    return jnp.where(row < d, fill, rolled)


def _linear_scan(a, u):
    n = a.shape[0]
    d = 1
    while d < n:
        u = a * _shift_rows(u, d, 0.0) + u
        a = a * _shift_rows(a, d, 1.0)
        d *= 2
    return a, u


def _route(x1, w_hi_ref, w_lo_ref, b_rt):
    x_hi = x1.astype(BF16)
    x_lo = (x1 - x_hi.astype(F32)).astype(BF16)
    w_hi = w_hi_ref[...]
    logits = (jnp.dot(x_hi, w_hi, preferred_element_type=F32)
              + (jnp.dot(x_lo, w_hi, preferred_element_type=F32)
                 + jnp.dot(x_hi, w_lo_ref[...], preferred_element_type=F32))) + b_rt
    col = lax.broadcasted_iota(jnp.int32, logits.shape, 1)
    big = jnp.int32(ROUTE_WIDTH)
    is_g = col < N_GROUPS
    gl = jnp.where(is_g, logits, -jnp.inf)
    gmax = jnp.max(gl, axis=-1, keepdims=True)
    g_idx = jnp.min(jnp.where(gl == gmax, col, big), axis=-1, keepdims=True)
    p_g = 1.0 / jnp.sum(jnp.where(is_g, jnp.exp(gl - gmax), 0.0), axis=-1, keepdims=True)
    lo = N_GROUPS + g_idx * EXPERTS_PER_GROUP
    in_grp = (col >= lo) & (col < lo + EXPERTS_PER_GROUP)
    el = jnp.where(in_grp, logits, -jnp.inf)
    v1 = jnp.max(el, axis=-1, keepdims=True)
    i1 = jnp.min(jnp.where(el == v1, col, big), axis=-1, keepdims=True)
    el2 = jnp.where(col == i1, -jnp.inf, el)
    v2 = jnp.max(el2, axis=-1, keepdims=True)
    i2 = jnp.min(jnp.where(el2 == v2, col, big), axis=-1, keepdims=True)
    e21 = jnp.exp(v2 - v1)
    inv = 1.0 / (1.0 + e21)
    w1 = p_g * inv
    w2 = p_g * (e21 * inv)
    e1 = (i1 - N_GROUPS).astype(F32)
    e2 = (i2 - N_GROUPS).astype(F32)
    return jnp.where(col == 0, e1, jnp.where(col == 1, e2, jnp.where(col == 2, w1, jnp.where(col == 3, w2, 0.0))))


def _merge_norm_route(x, rec, att, g_l, g_a, w_lru_out_ref, w_attn_out_ref, w_o_ref, ln_g, ln_b,
                      w_rt_hi_ref, w_rt_lo_ref, b_rt):
    rec_o = jnp.dot(rec.astype(BF16), w_lru_out_ref[...], preferred_element_type=F32)
    att_o = jnp.dot(att.astype(BF16), w_attn_out_ref[...], preferred_element_type=F32)
    merged = _sigmoid(g_l) * rec_o + _sigmoid(g_a) * att_o
    mix = jnp.dot(merged.astype(BF16), w_o_ref[...], preferred_element_type=F32)
    x1 = _layer_norm(ALPHA * x + mix, ln_g, ln_b)
    return x1, _route(x1, w_rt_hi_ref, w_rt_lo_ref, b_rt)


def _mixer_kernel(sinks_ref, x_ref, w_in_ref, b_in_ref, w_conv_ref, b_conv_ref, w_gates_ref, b_rg_ref, b_ig_ref,
                  lam_ref, w_lru_out_ref, w_attn_out_ref, w_o_ref, ln_g_ref, ln_b_ref, w_rt_hi_ref, w_rt_lo_ref,
                  b_rt_ref, x1_s_ref, route_s_ref,
                  x1_ref, route_ref, kwin_ref, vwin_ref, conv_ref, h_ref,
                  conv_buf, h_carry, kcat, vcat, att_buf, *, tiles_per_seq, n_tiles):
    step = pl.program_id(0)

    @pl.when(step < n_tiles)
    def _():
        _mixer_tile(lax.rem(step, tiles_per_seq), sinks_ref, x_ref, w_in_ref, b_in_ref, w_conv_ref, b_conv_ref,
                    w_gates_ref, b_rg_ref, b_ig_ref, lam_ref, w_lru_out_ref, w_attn_out_ref, w_o_ref, ln_g_ref,
                    ln_b_ref, w_rt_hi_ref, w_rt_lo_ref, b_rt_ref, x1_ref, route_ref, kwin_ref, vwin_ref, conv_ref,
                    h_ref, conv_buf, h_carry, kcat, vcat, att_buf)

    @pl.when(step == n_tiles)
    def _():
        n_s = x1_s_ref.shape[0]
        x1_ref[0:n_s, :] = x1_s_ref[...]
        route_ref[0:n_s, :] = route_s_ref[...]


def _mixer_tile(t, sinks_ref, x_ref, w_in_ref, b_in_ref, w_conv_ref, b_conv_ref, w_gates_ref, b_rg_ref, b_ig_ref,
                lam_ref, w_lru_out_ref, w_attn_out_ref, w_o_ref, ln_g_ref, ln_b_ref, w_rt_hi_ref, w_rt_lo_ref,
                b_rt_ref, x1_ref, route_ref, kwin_ref, vwin_ref, conv_ref, h_ref,
                conv_buf, h_carry, kcat, vcat, att_buf):
    T = SEQ_TILE

    @pl.when(t == 0)
    def _():
        conv_buf[0:SUBLANES, :] = jnp.zeros((SUBLANES, LRU_WIDTH), F32)
        h_carry[...] = jnp.zeros_like(h_carry)
        kcat[0:WINDOW, :] = jnp.zeros((WINDOW, KV_WIDTH), BF16)
        vcat[0:WINDOW, :] = jnp.zeros((WINDOW, KV_WIDTH), BF16)

    x = x_ref[0]
    xb = x.astype(BF16)

    def proj(lo, width):
        return jnp.dot(xb, w_in_ref[:, lo:lo + width], preferred_element_type=F32) + b_in_ref[:, lo:lo + width]

    xl = proj(OFF_XL, LRU_WIDTH)
    conv_buf[SUBLANES:SUBLANES + T, :] = xl
    wc = w_conv_ref[...]
    xc = wc[0:1] * conv_buf[SUBLANES - 3:SUBLANES - 3 + T, :]
    xc = xc + wc[1:2] * conv_buf[SUBLANES - 2:SUBLANES - 2 + T, :]
    xc = xc + wc[2:3] * conv_buf[SUBLANES - 1:SUBLANES - 1 + T, :]
    xc = xc + wc[3:4] * xl + b_conv_ref[...]
    conv_ref[0] = conv_buf[T + SUBLANES - (CONV_W - 1):T + SUBLANES, :]
    conv_buf[0:SUBLANES, :] = conv_buf[T:T + SUBLANES, :]

    a, u = _lru_gates(xc, w_gates_ref, b_rg_ref[...], b_ig_ref[...], lam_ref[...])
    a_cum, h = _linear_scan(a, u)
    h = a_cum * h_carry[0:1, :] + h
    h_last = h[T - 1:T, :]
    h_carry[0:1, :] = h_last
    h_ref[0] = h_last
    rec = h * jax.nn.gelu(proj(OFF_YL, LRU_WIDTH))

    q = proj(OFF_Q, N_HEADS * HEAD_DIM)
    k = proj(OFF_K, KV_WIDTH)
    v = proj(OFF_V, KV_WIDTH)
    kwin_ref[0] = k[T - WINDOW:, :]
    vwin_ref[0] = v[T - WINDOW:, :]
    kcat[WINDOW:WINDOW + T, :] = k.astype(BF16)
    vcat[WINDOW:WINDOW + T, :] = v.astype(BF16)

    qi = lax.broadcasted_iota(jnp.int32, (WINDOW, 2 * WINDOW), 0)
    kj = lax.broadcasted_iota(jnp.int32, (WINDOW, 2 * WINDOW), 1)
    band = (kj > qi) & (kj <= qi + WINDOW)
    grp_row = lax.broadcasted_iota(jnp.int32, (GROUP * WINDOW, 1), 0) // WINDOW
    for qb in range(T // WINDOW):
        if qb == 0:
            first_key = jnp.where(t == 0, WINDOW, 0)
            mask1 = band & (kj >= first_key)
        else:
            mask1 = band
        mask = jnp.concatenate([mask1] * GROUP, axis=0)
        r0 = qb * WINDOW
        qq = q[r0:r0 + WINDOW, :]
        for j in range(N_KV):
            kjb = kcat[r0:r0 + 2 * WINDOW, j * HEAD_DIM:(j + 1) * HEAD_DIM]
            vjb = vcat[r0:r0 + 2 * WINDOW, j * HEAD_DIM:(j + 1) * HEAD_DIM]
            qs = jnp.concatenate(
                [qq[:, (j * GROUP + g) * HEAD_DIM:(j * GROUP + g + 1) * HEAD_DIM] for g in range(GROUP)], axis=0)
            s = lax.dot_general(qs.astype(BF16), kjb, (((1,), (1,)), ((), ())), preferred_element_type=F32)
            s = jnp.where(mask, s * ATTN_SCALE, NEG_INF)
            sink = jnp.zeros((GROUP * WINDOW, 1), F32)
            for g in range(GROUP):
                sink = jnp.where(grp_row == g, sinks_ref[j * GROUP + g], sink)
            m = jnp.maximum(jnp.max(s, axis=-1, keepdims=True), sink)
            p = jnp.exp(s - m)
            inv = 1.0 / (jnp.sum(p, axis=-1, keepdims=True) + jnp.exp(sink - m))
            o = jnp.dot((p * inv).astype(BF16), vjb, preferred_element_type=F32)
            for g in range(GROUP):
                hcol = (j * GROUP + g) * HEAD_DIM
                att_buf[r0:r0 + WINDOW, hcol:hcol + HEAD_DIM] = o[g * WINDOW:(g + 1) * WINDOW, :]
    kcat[0:WINDOW, :] = kcat[T:T + WINDOW, :]
    vcat[0:WINDOW, :] = vcat[T:T + WINDOW, :]

    x1, route = _merge_norm_route(x, rec, att_buf[...], proj(OFF_GL, D_MODEL), proj(OFF_GA, D_MODEL),
                                  w_lru_out_ref, w_attn_out_ref, w_o_ref, ln_g_ref[...], ln_b_ref[...],
                                  w_rt_hi_ref, w_rt_lo_ref, b_rt_ref[...])
    x1_ref[...] = x1
    route_ref[...] = route


def _mixer_call(x_prompt, x1_s, route_s, sinks, wts):
    B, S, _ = x_prompt.shape
    T = SEQ_TILE
    nt = S // T
    n_tiles = B * nt
    n_rows_total = B * S + x1_s.shape[0]
    weight_args = (wts["w_in"], wts["b_in"], wts["w_conv"], wts["b_conv"], wts["w_gates"], wts["b_rg"], wts["b_ig"],
                   wts["lam"], wts["w_lru_out"], wts["w_attn_out"], wts["w_o"], wts["ln1_g"], wts["ln1_b"],
                   wts["w_rt_hi"], wts["w_rt_lo"], wts["b_rt"], x1_s, route_s)
    seq = lambda i: jnp.minimum(i, n_tiles - 1) // nt
    in_specs = [pl.BlockSpec(memory_space=pltpu.SMEM),
                pl.BlockSpec((1, T, D_MODEL), lambda i: (seq(i), lax.rem(jnp.minimum(i, n_tiles - 1), nt), 0))]
    in_specs += [_const_spec(w.shape) for w in weight_args]
    out_shape = (
        jax.ShapeDtypeStruct((n_rows_total, D_MODEL), F32),
        jax.ShapeDtypeStruct((n_rows_total, ROUTE_WIDTH), F32),
        jax.ShapeDtypeStruct((B, WINDOW, KV_WIDTH), F32),
        jax.ShapeDtypeStruct((B, WINDOW, KV_WIDTH), F32),
        jax.ShapeDtypeStruct((B, CONV_W - 1, LRU_WIDTH), F32),
        jax.ShapeDtypeStruct((B, 1, LRU_WIDTH), F32),
    )
    out_specs = (
        pl.BlockSpec((T, D_MODEL), lambda i: (i, 0)),
        pl.BlockSpec((T, ROUTE_WIDTH), lambda i: (i, 0)),
        pl.BlockSpec((1, WINDOW, KV_WIDTH), lambda i: (seq(i), 0, 0)),
        pl.BlockSpec((1, WINDOW, KV_WIDTH), lambda i: (seq(i), 0, 0)),
        pl.BlockSpec((1, CONV_W - 1, LRU_WIDTH), lambda i: (seq(i), 0, 0)),
        pl.BlockSpec((1, 1, LRU_WIDTH), lambda i: (seq(i), 0, 0)),
    )
    scratch = [
        pltpu.VMEM((T + 2 * SUBLANES, LRU_WIDTH), F32),
        pltpu.VMEM((SUBLANES, LRU_WIDTH), F32),
        pltpu.VMEM((T + WINDOW, KV_WIDTH), BF16),
        pltpu.VMEM((T + WINDOW, KV_WIDTH), BF16),
        pltpu.VMEM((T, N_HEADS * HEAD_DIM), F32),
    ]
    return pl.pallas_call(
        functools.partial(_mixer_kernel, tiles_per_seq=nt, n_tiles=n_tiles),
        grid=(n_tiles + 1,),
        in_specs=in_specs,
        out_specs=out_specs,
        out_shape=out_shape,
        scratch_shapes=scratch,
        compiler_params=pltpu.CompilerParams(dimension_semantics=("arbitrary",),
                                             vmem_limit_bytes=VMEM_LIMIT_BYTES),
        name="mixer_prompt",
    )(sinks, x_prompt, *weight_args)


def _sample_proj_kernel(x_ref, w_ref, b_ref, u_ref):
    u_ref[...] = jnp.dot(x_ref[...].astype(BF16), w_ref[...], preferred_element_type=F32) + b_ref[...]


def _sample_proj_call(x_s, w_in, b_in):
    n = x_s.shape[0]
    tn = SAMPLE_PROJ_TILE
    return pl.pallas_call(
        _sample_proj_kernel,
        grid=(IN_WIDTH // tn,),
        in_specs=[pl.BlockSpec((n, D_MODEL), lambda c: (0, 0)),
                  pl.BlockSpec((D_MODEL, tn), lambda c: (0, c)),
                  pl.BlockSpec((1, tn), lambda c: (0, c))],
        out_specs=pl.BlockSpec((n, tn), lambda c: (0, c)),
        out_shape=jax.ShapeDtypeStruct((n, IN_WIDTH), F32),
        compiler_params=pltpu.CompilerParams(dimension_semantics=("arbitrary",)),
        name="sample_proj",
    )(x_s, w_in, b_in)


def _sample_attn_kernel(q_ref, kn_ref, vn_ref, kn_row_ref, vn_row_ref, ck_ref, cv_ref, sinks_ref,
                        att_ref, kwin_ref, vwin_ref):
    tb = q_ref.shape[0]
    key_pos = lax.broadcasted_iota(jnp.int32, (tb, GROUP, WINDOW), 2)
    for j in range(N_KV):
        qj = q_ref[:, j * GROUP:(j + 1) * GROUP, :]
        kc = ck_ref[:, :, j * HEAD_DIM:(j + 1) * HEAD_DIM]
        vc = cv_ref[:, :, j * HEAD_DIM:(j + 1) * HEAD_DIM]
        s_c = jnp.einsum("bgd,bsd->bgs", qj.astype(BF16), kc.astype(BF16), preferred_element_type=F32) * ATTN_SCALE
        s_c = jnp.where(key_pos >= 1, s_c, NEG_INF)
        kn = kn_ref[:, j:j + 1, :]
        vn = vn_ref[:, j:j + 1, :]
        s_n = jnp.sum(qj * kn, axis=-1, keepdims=True) * ATTN_SCALE
        sink = sinks_ref[j][None]
        m = jnp.maximum(jnp.maximum(jnp.max(s_c, axis=-1, keepdims=True), s_n), sink)
        p_c = jnp.exp(s_c - m)
        p_n = jnp.exp(s_n - m)
        inv = 1.0 / (jnp.sum(p_c, axis=-1, keepdims=True) + p_n + jnp.exp(sink - m))
        o = jnp.einsum("bgs,bsd->bgd", (p_c * inv).astype(BF16), vc.astype(BF16), preferred_element_type=F32)
        att_ref[:, j * GROUP:(j + 1) * GROUP, :] = o + (p_n * inv) * vn
    kwin_ref[:, 0:WINDOW - 1, :] = ck_ref[:, 1:WINDOW, :]
    kwin_ref[:, WINDOW - 1:WINDOW, :] = kn_row_ref[...]
    vwin_ref[:, 0:WINDOW - 1, :] = cv_ref[:, 1:WINDOW, :]
    vwin_ref[:, WINDOW - 1:WINDOW, :] = vn_row_ref[...]


def _sample_attn_call(q3, kn3, vn3, kn_row, vn_row, ck, cv, sinks3):
    n = q3.shape[0]
    tb = SAMPLE_ATTN_TILE
    b3 = lambda i: (i, 0, 0)
    return pl.pallas_call(
        _sample_attn_kernel,
        grid=(n // tb,),
        in_specs=[pl.BlockSpec((tb, N_HEADS, HEAD_DIM), b3),
                  pl.BlockSpec((tb, N_KV, HEAD_DIM), b3),
                  pl.BlockSpec((tb, N_KV, HEAD_DIM), b3),
                  pl.BlockSpec((tb, 1, KV_WIDTH), b3),
                  pl.BlockSpec((tb, 1, KV_WIDTH), b3),
                  pl.BlockSpec((tb, WINDOW, KV_WIDTH), b3),
                  pl.BlockSpec((tb, WINDOW, KV_WIDTH), b3),
                  pl.BlockSpec((N_KV, GROUP, 1), lambda i: (0, 0, 0))],
        out_specs=(pl.BlockSpec((tb, N_HEADS, HEAD_DIM), b3),
                   pl.BlockSpec((tb, WINDOW, KV_WIDTH), b3),
                   pl.BlockSpec((tb, WINDOW, KV_WIDTH), b3)),
        out_shape=(jax.ShapeDtypeStruct((n, N_HEADS, HEAD_DIM), F32),
                   jax.ShapeDtypeStruct((n, WINDOW, KV_WIDTH), F32),
                   jax.ShapeDtypeStruct((n, WINDOW, KV_WIDTH), F32)),
        compiler_params=pltpu.CompilerParams(dimension_semantics=("arbitrary",)),
        name="sample_attn",
    )(q3, kn3, vn3, kn_row, vn_row, ck, cv, sinks3)


def _sample_mix_kernel(x_ref, u_ref, att_ref, st_ref, h0_ref, w_conv_ref, b_conv_ref, w_gates_ref, b_rg_ref,
                       b_ig_ref, lam_ref, w_lru_out_ref, w_attn_out_ref, w_o_ref, ln_g_ref, ln_b_ref,
                       w_rt_hi_ref, w_rt_lo_ref, b_rt_ref,
                       x1_ref, route_ref, conv_ref, h_ref):
    xl = u_ref[:, OFF_XL:OFF_XL + LRU_WIDTH]
    wc = w_conv_ref[...]
    xc = wc[0:1] * st_ref[0]
    xc = xc + wc[1:2] * st_ref[1]
    xc = xc + wc[2:3] * st_ref[2]
    xc = xc + wc[3:4] * xl + b_conv_ref[...]
    conv_ref[0] = st_ref[1]
    conv_ref[1] = st_ref[2]
    conv_ref[2] = xl
    a, u = _lru_gates(xc, w_gates_ref, b_rg_ref[...], b_ig_ref[...], lam_ref[...])
    h = a * h0_ref[...] + u
    h_ref[...] = h
    rec = h * jax.nn.gelu(u_ref[:, OFF_YL:OFF_YL + LRU_WIDTH])
    x1, route = _merge_norm_route(x_ref[...], rec, att_ref[...], u_ref[:, OFF_GL:OFF_GL + D_MODEL],
                                  u_ref[:, OFF_GA:OFF_GA + D_MODEL], w_lru_out_ref, w_attn_out_ref, w_o_ref,
                                  ln_g_ref[...], ln_b_ref[...], w_rt_hi_ref, w_rt_lo_ref, b_rt_ref[...])
    x1_ref[...] = x1
    route_ref[...] = route


def _sample_mix_call(x_s, u_s, att, st_t, h0, wts):
    n = x_s.shape[0]
    weight_args = (wts["w_conv"], wts["b_conv"], wts["w_gates"], wts["b_rg"], wts["b_ig"], wts["lam"],
                   wts["w_lru_out"], wts["w_attn_out"], wts["w_o"], wts["ln1_g"], wts["ln1_b"],
                   wts["w_rt_hi"], wts["w_rt_lo"], wts["b_rt"])
    args = (x_s, u_s, att, st_t, h0) + weight_args
    out_shapes = ((n, D_MODEL), (n, ROUTE_WIDTH), (CONV_W - 1, n, LRU_WIDTH), (n, LRU_WIDTH))
    return pl.pallas_call(
        _sample_mix_kernel,
        grid=(1,),
        in_specs=[_const_spec(a.shape) for a in args],
        out_specs=tuple(_const_spec(s) for s in out_shapes),
        out_shape=tuple(jax.ShapeDtypeStruct(s, F32) for s in out_shapes),
        compiler_params=pltpu.CompilerParams(dimension_semantics=("arbitrary",),
                                             vmem_limit_bytes=VMEM_LIMIT_BYTES),
        name="sample_mix",
    )(*args)


def _one_hot(mask):
    return jnp.where(mask, 1.0, 0.0).astype(BF16)


def _pack_rows(x):
    half = x.shape[1] // 2
    lo = lax.shift_right_logical(lax.bitcast_convert_type(x[:, :half], jnp.uint32), jnp.uint32(16))
    hi = lax.bitcast_convert_type(x[:, half:], jnp.uint32) & jnp.uint32(0xFFFF0000)
    return lo | hi


def _unpack_rows(words):
    lo = lax.bitcast_convert_type(lax.shift_left(words, jnp.uint32(16)), F32)
    hi = lax.bitcast_convert_type(words & jnp.uint32(0xFFFF0000), F32)
    return jnp.concatenate([lo.astype(BF16), hi.astype(BF16)], axis=1)


def _dispatch_kernel(x1_ref, route_ref, xs_hbm, goff_ref, gtot_ref,
                     stage, zero_rows, tri, g_vmem, gvec, g_smem, nd_smem, sem_stage, sem_g, sem_zero,
                     *, n_tokens, seg_cap):
    i = pl.program_id(0)
    last = pl.num_programs(0) - 1
    td = DISPATCH_TILE
    n_assign = TOP_K * td

    def g_copy():
        return pltpu.make_async_copy(g_vmem.at[pl.ds(0, 1)], g_smem, sem_g)

    @pl.when(i == 0)
    def _():
        r = lax.broadcasted_iota(jnp.int32, (n_assign, n_assign), 0)
        c = lax.broadcasted_iota(jnp.int32, (n_assign, n_assign), 1)
        tri[...] = _one_hot(r < c)
        gvec[...] = jnp.zeros_like(gvec)
        g_vmem[...] = jnp.zeros_like(g_vmem)
        nd_smem[0] = 0
        g_copy().start()

    n_valid = n_tokens - i * td
    route = route_ref[...]
    valid_col = lax.broadcasted_iota(jnp.int32, (td, 1), 0) < n_valid
    lane = lax.broadcasted_iota(jnp.int32, (td, ROUTE_WIDTH), 1).astype(F32)
    e0c = jnp.where(valid_col, route[:, 0:1], -1.0)
    e1c = jnp.where(valid_col, route[:, 1:2], -1.0)
    cnt_row = jnp.sum(jnp.where(lane == e0c, 1.0, 0.0) + jnp.where(lane == e1c, 1.0, 0.0), axis=0, keepdims=True)
    rounds = (jnp.max(cnt_row).astype(jnp.int32) + SLOT_CAP - 1) // SLOT_CAP
    goff_ref[0] = gvec[0:1, :].astype(jnp.int32)
    gvec[0:1, :] = gvec[0:1, :] + jnp.ceil(cnt_row * (1.0 / SUBLANES)) * SUBLANES
    gtot_ref[...] = gvec[0:1, :].astype(jnp.int32)

    route_t = route.T
    valid_row = lax.broadcasted_iota(jnp.int32, (1, td), 1) < n_valid
    e_row = jnp.concatenate([jnp.where(valid_row, route_t[0:1, :], -1.0),
                             jnp.where(valid_row, route_t[1:2, :], -1.0)], axis=1)
    expert_sub = lax.broadcasted_iota(jnp.int32, (N_EXPERTS, n_assign), 0).astype(F32)
    onehot_t = expert_sub == e_row
    rank_t = jnp.dot(_one_hot(onehot_t), tri[...], preferred_element_type=F32)
    rank_row = jnp.sum(jnp.where(onehot_t, rank_t, 0.0), axis=0, keepdims=True)
    xb = jnp.where(valid_col, x1_ref[...], 0.0).astype(BF16)
    slot = lax.broadcasted_iota(jnp.int32, (N_SLOTS, td), 0).astype(F32)

    def stage_copy(e, dst_row):
        src = stage.at[pl.ds(pl.multiple_of(e * SLOT_CAP, SLOT_CAP), SLOT_CAP)]
        return pltpu.make_async_copy(src, xs_hbm.at[pl.ds(dst_row, SLOT_CAP)], sem_stage)

    def wait_outstanding():
        def wait_one(_, carry):
            stage_copy(0, 0).wait()
            return carry
        lax.fori_loop(0, nd_smem[0], wait_one, 0)
        nd_smem[0] = 0

    g_copy().wait()

    def round_body(r, carry):
        lo = r * SLOT_CAP
        lo_f = lo.astype(F32)
        in_round = (rank_row >= lo_f) & (rank_row < lo_f + SLOT_CAP) & (e_row >= 0.0)
        key = jnp.where(in_round, e_row * SLOT_CAP + (rank_row - lo_f), -1.0)
        perm = jnp.where(slot == key[:, :td], 1.0, jnp.where(slot == key[:, td:], 1.0, 0.0)).astype(BF16)
        sorted_rows = jnp.dot(perm, xb, preferred_element_type=F32)
        wait_outstanding()
        stage[...] = _pack_rows(sorted_rows)

        def issue(e, carry):
            stage_copy(e, pl.multiple_of(e * seg_cap + g_smem[0, e] + lo, SUBLANES)).start()
            return carry
        lax.fori_loop(0, N_EXPERTS, issue, 0)
        nd_smem[0] = N_EXPERTS
        return carry
    lax.fori_loop(0, rounds, round_body, 0)

    g_vmem[0:1, :] = gvec[0:1, :].astype(jnp.int32)
    g_copy().start()

    @pl.when(i == last)
    def _():
        g_copy().wait()
        wait_outstanding()
        zero_rows[...] = jnp.zeros_like(zero_rows)

        def pad_copy(e):
            dst_row = pl.multiple_of(e * seg_cap + g_smem[0, e], SUBLANES)
            return pltpu.make_async_copy(zero_rows, xs_hbm.at[pl.ds(dst_row, SEG_PAD)], sem_zero)

        def start(e, carry):
            pad_copy(e).start()
            return carry
        lax.fori_loop(0, N_EXPERTS, start, 0)

        def wait(e, carry):
            pad_copy(e).wait()
            return carry
        lax.fori_loop(0, N_EXPERTS, wait, 0)


def _dispatch_tiles(n_tokens):
    return (n_tokens + DISPATCH_TILE - 1) // DISPATCH_TILE


def _segment_capacity(n_tokens):
    alignment_slack = (SUBLANES - 1) * _dispatch_tiles(n_tokens)
    return (n_tokens + alignment_slack + SEG_PAD + MOE_TILE - 1) // MOE_TILE * MOE_TILE


def _dispatch_call(x1_all, route_all):
    n_tokens = x1_all.shape[0]
    td = DISPATCH_TILE
    n_tiles = (n_tokens + td - 1) // td
    seg_cap = _segment_capacity(n_tokens)
    n_assign = TOP_K * td
    return pl.pallas_call(
        functools.partial(_dispatch_kernel, n_tokens=n_tokens, seg_cap=seg_cap),
        grid=(n_tiles,),
        in_specs=[pl.BlockSpec((td, D_MODEL), lambda i: (i, 0)),
                  pl.BlockSpec((td, ROUTE_WIDTH), lambda i: (i, 0))],
        out_specs=(pl.BlockSpec(memory_space=pl.ANY),
                   pl.BlockSpec((1, 1, ROUTE_WIDTH), lambda i: (i, 0, 0)),
                   pl.BlockSpec((1, ROUTE_WIDTH), lambda i: (0, 0))),
        out_shape=(jax.ShapeDtypeStruct((N_EXPERTS * seg_cap, PACKED_WIDTH), jnp.uint32),
                   jax.ShapeDtypeStruct((n_tiles, 1, ROUTE_WIDTH), jnp.int32),
                   jax.ShapeDtypeStruct((1, ROUTE_WIDTH), jnp.int32)),
        scratch_shapes=[
            pltpu.VMEM((N_SLOTS, PACKED_WIDTH), jnp.uint32),
            pltpu.VMEM((SEG_PAD, PACKED_WIDTH), jnp.uint32),
            pltpu.VMEM((n_assign, n_assign), BF16),
            pltpu.VMEM((SUBLANES, ROUTE_WIDTH), jnp.int32),
            pltpu.VMEM((SUBLANES, ROUTE_WIDTH), F32),
            pltpu.SMEM((1, ROUTE_WIDTH), jnp.int32),
            pltpu.SMEM((1,), jnp.int32),
            pltpu.SemaphoreType.DMA(()),
            pltpu.SemaphoreType.DMA(()),
            pltpu.SemaphoreType.DMA(()),
        ],
        compiler_params=pltpu.CompilerParams(dimension_semantics=("arbitrary",),
                                             vmem_limit_bytes=VMEM_LIMIT_BYTES),
        name="moe_dispatch",
    )(x1_all, route_all)


def _moe_kernel(blk_e_ref, blk_j_ref, n_used_ref, xs_ref, wg_ref, wu_ref, wd_ref, ys_ref, wg_bf, wu_bf, wd_bf):
    del blk_j_ref
    b = pl.program_id(0)

    @pl.when(b < n_used_ref[0])
    def _():
        prev_e = blk_e_ref[jnp.maximum(b - 1, 0)]

        @pl.when((b == 0) | (blk_e_ref[b] != prev_e))
        def _():
            wg_bf[...] = wg_ref[0].astype(BF16)
            wu_bf[...] = wu_ref[0].astype(BF16)
            wd_bf[...] = wd_ref[0].astype(BF16)

        xb = _unpack_rows(xs_ref[...])
        g = jnp.dot(xb, wg_bf[...], preferred_element_type=F32)
        u = jnp.dot(xb, wu_bf[...], preferred_element_type=F32)
        hmid = (g * _sigmoid(g)) * u
        y = jnp.dot(hmid.astype(BF16), wd_bf[...], preferred_element_type=F32)
        ys_ref[...] = _pack_rows(y.astype(BF16).astype(F32))


def _expert_blocks(gtot, n_blocks):
    rows = gtot[0, :N_EXPERTS]
    nb = (rows + SLOT_CAP + MOE_TILE - 1) // MOE_TILE
    ends = jnp.cumsum(nb)
    n_used = ends[-1]
    step = jnp.minimum(jnp.arange(n_blocks, dtype=jnp.int32), n_used - 1)
    blk_e = jnp.sum((step[:, None] >= ends[None, :]).astype(jnp.int32), axis=1)
    blk_j = step - (ends - nb)[blk_e]
    return blk_e, blk_j.astype(jnp.int32), n_used.reshape(1).astype(jnp.int32)


def _moe_call(gtot, xs, w_gate, w_up, w_down, n_tokens):
    seg_cap = _segment_capacity(n_tokens)
    seg_blocks = seg_cap // MOE_TILE
    max_rows = n_tokens * TOP_K + N_EXPERTS * (SUBLANES - 1) * _dispatch_tiles(n_tokens)
    n_blocks = (max_rows + N_EXPERTS * (SLOT_CAP + MOE_TILE - 1)) // MOE_TILE
    blk_e, blk_j, n_used = _expert_blocks(gtot, n_blocks)
    row_block = lambda b, be, bj, nu: (be[b] * seg_blocks + bj[b], 0)
    weight_block = lambda b, be, bj, nu: (be[b], 0, 0)
    grid_spec = pltpu.PrefetchScalarGridSpec(
        num_scalar_prefetch=3,
        grid=(n_blocks,),
        in_specs=[
            pl.BlockSpec((MOE_TILE, PACKED_WIDTH), row_block),
            pl.BlockSpec((1, D_MODEL, EXPERT_FF), weight_block),
            pl.BlockSpec((1, D_MODEL, EXPERT_FF), weight_block),
            pl.BlockSpec((1, EXPERT_FF, D_MODEL), weight_block),
        ],
        out_specs=pl.BlockSpec((MOE_TILE, PACKED_WIDTH), row_block),
        scratch_shapes=[
            pltpu.VMEM((D_MODEL, EXPERT_FF), BF16),
            pltpu.VMEM((D_MODEL, EXPERT_FF), BF16),
            pltpu.VMEM((EXPERT_FF, D_MODEL), BF16),
        ],
    )
    return pl.pallas_call(
        _moe_kernel,
        grid_spec=grid_spec,
        out_shape=jax.ShapeDtypeStruct(xs.shape, xs.dtype),
        compiler_params=pltpu.CompilerParams(dimension_semantics=("arbitrary",),
                                             vmem_limit_bytes=VMEM_LIMIT_BYTES),
        name="moe_experts",
    )(blk_e, blk_j, n_used, xs, w_gate, w_up, w_down)


def _combine_kernel(gcur_ref, gnext_ref, x1_ref, route_ref, ys_hbm, ln_g_ref, ln_b_ref, y_ref,
                    stage, acc, tri, sem, *, seg_cap):
    td = x1_ref.shape[0]
    n_assign = TOP_K * td
    i = pl.program_id(0)
    buf = lax.rem(i, 2)

    def stage_copy(g_ref, e, first_rank, to_buf):
        src_row = pl.multiple_of(e * seg_cap + g_ref[0, 0, e] + first_rank, SUBLANES)
        dst = stage.at[to_buf, pl.ds(pl.multiple_of(e * SLOT_CAP, SLOT_CAP), SLOT_CAP)]
        return pltpu.make_async_copy(ys_hbm.at[pl.ds(src_row, SLOT_CAP)], dst, sem.at[to_buf])

    def fetch_first_round(g_ref, to_buf):
        def start(e, carry):
            stage_copy(g_ref, e, 0, to_buf).start()
            return carry
        lax.fori_loop(0, N_EXPERTS, start, 0)

    @pl.when(i == 0)
    def _():
        r = lax.broadcasted_iota(jnp.int32, (n_assign, n_assign), 0)
        c = lax.broadcasted_iota(jnp.int32, (n_assign, n_assign), 1)
        tri[...] = _one_hot(c < r)
        fetch_first_round(gcur_ref, 0)

    def count(e):
        return gnext_ref[0, 0, e] - gcur_ref[0, 0, e]

    n_max = lax.fori_loop(0, N_EXPERTS, lambda e, m: jnp.maximum(m, count(e)), 0)
    rounds = (n_max + SLOT_CAP - 1) // SLOT_CAP

    route = route_ref[...]
    e0, e1, w0, w1 = route[:, 0:1], route[:, 1:2], route[:, 2:3], route[:, 3:4]
    lane = lax.broadcasted_iota(jnp.int32, (td, ROUTE_WIDTH), 1).astype(F32)
    onehot = jnp.concatenate([lane == e0, lane == e1], axis=0)
    onehot_f = jnp.where(onehot, 1.0, 0.0)
    rank_mat = jnp.dot(tri[...], onehot_f.astype(BF16), preferred_element_type=F32)
    rank = jnp.sum(jnp.where(onehot, rank_mat, 0.0), axis=1, keepdims=True)
    r0, r1 = rank[:td], rank[td:]
    cnt_row = jnp.sum(onehot_f, axis=0, keepdims=True)

    slot_col = lax.broadcasted_iota(jnp.int32, (N_SLOTS, 1), 0)
    slot_expert = slot_col // SLOT_CAP
    slot_rank = slot_col - slot_expert * SLOT_CAP
    lane_s = lax.broadcasted_iota(jnp.int32, (N_SLOTS, ROUTE_WIDTH), 1)
    n_col = jnp.sum(jnp.where(lane_s == slot_expert, cnt_row, 0.0), axis=1, keepdims=True)
    slot_lane = lax.broadcasted_iota(jnp.int32, (td, N_SLOTS), 1).astype(F32)

    def wait_copies(n):
        def wait_one(_, c):
            stage_copy(gcur_ref, 0, 0, buf).wait()
            return c
        lax.fori_loop(0, n, wait_one, 0)

    def weighted_rows(lo):
        lo_f = lo.astype(F32)
        live = (slot_rank + lo).astype(F32) < n_col
        rows = _unpack_rows(jnp.where(live, stage[buf], jnp.uint32(0)))

        def gathered(e_col, r_col):
            in_round = (r_col >= lo_f) & (r_col < lo_f + SLOT_CAP)
            key = jnp.where(in_round, e_col * SLOT_CAP + (r_col - lo_f), -1.0)
            return jnp.dot(_one_hot(slot_lane == key), rows, preferred_element_type=F32)
        return w0 * gathered(e0, r0) + w1 * gathered(e1, r1)

    wait_copies(N_EXPERTS)

    @pl.when(i + 1 < pl.num_programs(0))
    def _():
        fetch_first_round(gnext_ref, 1 - buf)

    acc[...] = weighted_rows(jnp.int32(0))

    def later_round(r, carry):
        lo = r * SLOT_CAP

        def start(e, n_started):
            has_rows = count(e) > lo

            @pl.when(has_rows)
            def _():
                stage_copy(gcur_ref, e, lo, buf).start()
            return n_started + has_rows.astype(jnp.int32)
        wait_copies(lax.fori_loop(0, N_EXPERTS, start, 0))
        acc[...] = acc[...] + weighted_rows(lo)
        return carry
    lax.fori_loop(1, rounds, later_round, 0)

    y_ref[...] = _layer_norm(ALPHA * x1_ref[...] + acc[...], ln_g_ref[...], ln_b_ref[...])


def _combine_call(goff, x1_all, route_all, ys, ln_g, ln_b, first_row, n_rows, tile, n_tokens):
    assert first_row % DISPATCH_TILE == 0 and first_row % tile == 0
    assert tile == DISPATCH_TILE or n_rows == tile
    first_block = first_row // tile
    first_goff = first_row // DISPATCH_TILE
    n_assign = TOP_K * tile
    grid_spec = pl.GridSpec(
        grid=(n_rows // tile,),
        in_specs=[
            pl.BlockSpec((1, 1, ROUTE_WIDTH), lambda i: (first_goff + i, 0, 0), memory_space=pltpu.SMEM),
            pl.BlockSpec((1, 1, ROUTE_WIDTH), lambda i: (first_goff + i + 1, 0, 0), memory_space=pltpu.SMEM),
            pl.BlockSpec((tile, D_MODEL), lambda i: (first_block + i, 0)),
            pl.BlockSpec((tile, ROUTE_WIDTH), lambda i: (first_block + i, 0)),
            pl.BlockSpec(memory_space=pl.ANY),
            _const_spec(ln_g.shape),
            _const_spec(ln_b.shape),
        ],
        out_specs=pl.BlockSpec((tile, D_MODEL), lambda i: (i, 0)),
        scratch_shapes=[pltpu.VMEM((2, N_SLOTS, PACKED_WIDTH), jnp.uint32),
                        pltpu.VMEM((tile, D_MODEL), F32),
                        pltpu.VMEM((n_assign, n_assign), BF16),
                        pltpu.SemaphoreType.DMA((2,))],
    )
    return pl.pallas_call(
        functools.partial(_combine_kernel, seg_cap=_segment_capacity(n_tokens)),
        grid_spec=grid_spec,
        out_shape=jax.ShapeDtypeStruct((n_rows, D_MODEL), F32),
        compiler_params=pltpu.CompilerParams(dimension_semantics=("arbitrary",),
                                             vmem_limit_bytes=VMEM_LIMIT_BYTES),
        name="moe_combine",
    )(goff, goff, x1_all, route_all, ys, ln_g, ln_b)


def _prepare_weights(w_in, b_in, w_conv, b_conv, w_rg, b_rg, w_ig, b_ig, lru_lambda, w_lru_out, w_attn_out, w_o,
                     ln1_g, ln1_b, w_group, b_group, w_router, b_router):
    blocks_per_chunk = GATE_CHUNK // LRU_BLOCK

    def chunked_block_diag(w):
        w = w.reshape(N_GATE_CHUNKS, blocks_per_chunk, LRU_BLOCK, LRU_BLOCK)
        eye = jnp.eye(blocks_per_chunk, dtype=w.dtype)
        return jnp.einsum("cbij,bd->cbidj", w, eye).reshape(N_GATE_CHUNKS, GATE_CHUNK, GATE_CHUNK)

    w_gates = jnp.concatenate([chunked_block_diag(w_rg), chunked_block_diag(w_ig)], axis=-1).astype(BF16)
    w_rt = jnp.concatenate([w_group, w_router], axis=1)
    w_rt = jnp.pad(w_rt, ((0, 0), (0, ROUTE_WIDTH - w_rt.shape[1])))
    w_rt_hi = w_rt.astype(BF16)
    w_rt_lo = (w_rt - w_rt_hi.astype(F32)).astype(BF16)
    b_rt = jnp.pad(jnp.concatenate([b_group, b_router]), (0, ROUTE_WIDTH - N_GROUPS - N_EXPERTS))
    row = lambda v: v.reshape(1, -1)
    return dict(
        w_in=w_in.astype(BF16), b_in=row(b_in), w_conv=w_conv, b_conv=row(b_conv), w_gates=w_gates,
        b_rg=row(b_rg), b_ig=row(b_ig), lam=row(lru_lambda),
        w_lru_out=w_lru_out.astype(BF16), w_attn_out=w_attn_out.astype(BF16), w_o=w_o.astype(BF16),
        ln1_g=row(ln1_g), ln1_b=row(ln1_b), w_rt_hi=w_rt_hi, w_rt_lo=w_rt_lo, b_rt=row(b_rt))


def kernel(x_prompt, x_sample, cache_k, cache_v, state_conv, state_lru_h, w_in, b_in, w_conv, b_conv, w_rg, b_rg,
           w_ig, b_ig, lru_lambda, sinks, w_lru_out, w_attn_out, w_o, ln1_g, ln1_b, w_group, b_group, w_router,
           b_router, w_gate, w_up, w_down, ln2_g, ln2_b):
    B, S, _ = x_prompt.shape
    n_prompt = B * S
    n_sample = x_sample.shape[0]
    n_all = n_prompt + n_sample
    wts = _prepare_weights(w_in, b_in, w_conv, b_conv, w_rg, b_rg, w_ig, b_ig, lru_lambda, w_lru_out, w_attn_out,
                           w_o, ln1_g, ln1_b, w_group, b_group, w_router, b_router)

    x_s = x_sample.reshape(n_sample, D_MODEL)
    u_s = _sample_proj_call(x_s, wts["w_in"], wts["b_in"])
    q3 = u_s[:, OFF_Q:OFF_K].reshape(n_sample, N_HEADS, HEAD_DIM)
    k_new = u_s[:, OFF_K:OFF_V]
    v_new = u_s[:, OFF_V:OFF_GL]
    att3, k_win_s, v_win_s = _sample_attn_call(
        q3, k_new.reshape(n_sample, N_KV, HEAD_DIM), v_new.reshape(n_sample, N_KV, HEAD_DIM),
        k_new.reshape(n_sample, 1, KV_WIDTH), v_new.reshape(n_sample, 1, KV_WIDTH),
        cache_k.reshape(n_sample, WINDOW, KV_WIDTH), cache_v.reshape(n_sample, WINDOW, KV_WIDTH),
        sinks.reshape(N_KV, GROUP, 1))
    x1_s, route_s, conv_s_t, h_s = _sample_mix_call(
        x_s, u_s, att3.reshape(n_sample, N_HEADS * HEAD_DIM), jnp.transpose(state_conv, (1, 0, 2)), state_lru_h, wts)

    x1_all, route_all, k_win_p, v_win_p, conv_p, h_p = _mixer_call(x_prompt, x1_s, route_s, sinks, wts)

    xs, goff, gtot = _dispatch_call(x1_all, route_all)
    ys = _moe_call(gtot, xs, w_gate, w_up, w_down, n_all)
    goff = jnp.concatenate([goff, gtot[None]], axis=0)
    ln2_g2, ln2_b2 = ln2_g.reshape(1, -1), ln2_b.reshape(1, -1)
    y_p = _combine_call(goff, x1_all, route_all, ys, ln2_g2, ln2_b2, 0, n_prompt, DISPATCH_TILE, n_all)
    y_s = _combine_call(goff, x1_all, route_all, ys, ln2_g2, ln2_b2, n_prompt, n_sample, n_sample, n_all)

    kv_shape = (WINDOW, N_KV, HEAD_DIM)
    return (y_p.reshape(B, S, D_MODEL), y_s.reshape(n_sample, 1, D_MODEL),
            k_win_p.reshape((B,) + kv_shape), v_win_p.reshape((B,) + kv_shape), conv_p, h_p.reshape(B, LRU_WIDTH),
            k_win_s.reshape((n_sample,) + kv_shape), v_win_s.reshape((n_sample,) + kv_shape),
            jnp.transpose(conv_s_t, (1, 0, 2)), h_s)
```

```python
import functools

import jax
import jax.numpy as jnp
from jax import lax
from jax.experimental import pallas as pl
from jax.experimental.pallas import tpu as pltpu

F32 = jnp.float32
BF16 = jnp.bfloat16

D_MODEL = 1024
LRU_WIDTH = 1024
LRU_BLOCK = 64
CONV_W = 4
LRU_C = 8.0
N_HEADS = 16
N_KV = 4
GROUP = N_HEADS // N_KV
HEAD_DIM = 64
KV_WIDTH = N_KV * HEAD_DIM
WINDOW = 128
NEG_INF = -1e30
N_GROUPS = 4
EXPERTS_PER_GROUP = 8
N_EXPERTS = N_GROUPS * EXPERTS_PER_GROUP
TOP_K = 2
EXPERT_FF = D_MODEL // 2
DEPTH = 1
ALPHA = (2 * DEPTH) ** 0.25
LN_EPS = 1e-5
ATTN_SCALE = HEAD_DIM ** -0.5
LOG2_E = 1.4426950408889634

OFF_XL = 0
OFF_YL = OFF_XL + LRU_WIDTH
OFF_Q = OFF_YL + LRU_WIDTH
OFF_K = OFF_Q + N_HEADS * HEAD_DIM
OFF_V = OFF_K + KV_WIDTH
OFF_GL = OFF_V + KV_WIDTH
OFF_GA = OFF_GL + D_MODEL
IN_WIDTH = OFF_GA + D_MODEL

LANES = 128
SUBLANES = 8
MXU_DIM = 256
VMEM_LIMIT_BYTES = 56 * 1024 * 1024

GATE_CHUNK = MXU_DIM
N_GATE_CHUNKS = LRU_WIDTH // GATE_CHUNK
ROUTE_WIDTH = LANES

SEQ_TILE = 256
MOE_TILE = 512
DISPATCH_TILE = 256
SLOT_CAP = 32
N_SLOTS = N_EXPERTS * SLOT_CAP
SEG_PAD = MOE_TILE + SLOT_CAP
PACKED_WIDTH = D_MODEL // 2
SAMPLE_ATTN_TILE = 16
SAMPLE_PROJ_TILE = 512


def _const_spec(shape):
    nd = len(shape)
    return pl.BlockSpec(shape, lambda *_: (0,) * nd)


def _layer_norm(z, g, b):
    mu = jnp.mean(z, axis=-1, keepdims=True)
    zc = z - mu
    var = jnp.mean(zc * zc, axis=-1, keepdims=True)
    return zc * lax.rsqrt(var + LN_EPS) * g + b


def _sigmoid(x):
    return 1.0 / (1.0 + jnp.exp(-x))


def _softplus(x):
    return jnp.maximum(x, 0.0) + jnp.log1p(jnp.exp(-jnp.abs(x)))


def _bdot(a, b):
    return jnp.dot(a.astype(BF16), b.astype(BF16), preferred_element_type=F32)


def _lru_gates(xc, w_gates_ref, b_rg, b_ig, lam):
    xcb = xc.astype(BF16)
    r_parts, i_parts = [], []
    for c in range(N_GATE_CHUNKS):
        g = jnp.dot(xcb[:, c * GATE_CHUNK:(c + 1) * GATE_CHUNK], w_gates_ref[c], preferred_element_type=F32)
        r_parts.append(g[:, :GATE_CHUNK])
        i_parts.append(g[:, GATE_CHUNK:])
    r = _sigmoid(jnp.concatenate(r_parts, axis=1) + b_rg)
    i = _sigmoid(jnp.concatenate(i_parts, axis=1) + b_ig)
    log_a = (-LRU_C * r) * _softplus(-lam)
    a = jnp.exp(log_a)
    gain_sq = 1.0 - a * a
    gain = jnp.where(gain_sq > 0.0, gain_sq * lax.rsqrt(gain_sq), 0.0)
    u = gain * (i * xc)
    return a, u


def _shift_rows(x, d, fill):
    n = x.shape[0]
    if d % SUBLANES == 0:
        return jnp.concatenate([jnp.full((d, x.shape[1]), fill, x.dtype), x[:n - d]], axis=0)
    rolled = pltpu.roll(x, d, axis=0)
    row = lax.broadcasted_iota(jnp.int32, x.shape, 0)
    return jnp.where(row < d, fill, rolled)


def _linear_scan(a, u):
    n = a.shape[0]
    d = 1
    while d < n:
        u = a * _shift_rows(u, d, 0.0) + u
        a = a * _shift_rows(a, d, 1.0)
        d *= 2
    return a, u


def _route(x1, w_hi_ref, w_lo_ref, b_rt):
    x_hi = x1.astype(BF16)
    x_lo = (x1 - x_hi.astype(F32)).astype(BF16)
    w_hi = w_hi_ref[...]
    logits = (jnp.dot(x_hi, w_hi, preferred_element_type=F32)
              + (jnp.dot(x_lo, w_hi, preferred_element_type=F32)
                 + jnp.dot(x_hi, w_lo_ref[...], preferred_element_type=F32))) + b_rt
    col = lax.broadcasted_iota(jnp.int32, logits.shape, 1)
    big = jnp.int32(ROUTE_WIDTH)
    is_g = col < N_GROUPS
    gl = jnp.where(is_g, logits, -jnp.inf)
    gmax = jnp.max(gl, axis=-1, keepdims=True)
    g_idx = jnp.min(jnp.where(gl == gmax, col, big), axis=-1, keepdims=True)
    p_g = 1.0 / jnp.sum(jnp.where(is_g, jnp.exp(gl - gmax), 0.0), axis=-1, keepdims=True)
    lo = N_GROUPS + g_idx * EXPERTS_PER_GROUP
    in_grp = (col >= lo) & (col < lo + EXPERTS_PER_GROUP)
    el = jnp.where(in_grp, logits, -jnp.inf)
    v1 = jnp.max(el, axis=-1, keepdims=True)
    i1 = jnp.min(jnp.where(el == v1, col, big), axis=-1, keepdims=True)
    el2 = jnp.where(col == i1, -jnp.inf, el)
    v2 = jnp.max(el2, axis=-1, keepdims=True)
    i2 = jnp.min(jnp.where(el2 == v2, col, big), axis=-1, keepdims=True)
    e21 = jnp.exp(v2 - v1)
    inv = 1.0 / (1.0 + e21)
    w1 = p_g * inv
    w2 = p_g * (e21 * inv)
    e1 = (i1 - N_GROUPS).astype(F32)
    e2 = (i2 - N_GROUPS).astype(F32)
    return jnp.where(col == 0, e1, jnp.where(col == 1, e2, jnp.where(col == 2, w1, jnp.where(col == 3, w2, 0.0))))


def _merge_norm_route(x, rec, att, g_l, g_a, w_lru_out_ref, w_attn_out_ref, w_o_ref, ln_g, ln_b,
                      w_rt_hi_ref, w_rt_lo_ref, b_rt):
    rec_o = jnp.dot(rec.astype(BF16), w_lru_out_ref[...], preferred_element_type=F32)
    att_o = jnp.dot(att.astype(BF16), w_attn_out_ref[...], preferred_element_type=F32)
    merged = _sigmoid(g_l) * rec_o + _sigmoid(g_a) * att_o
    mix = jnp.dot(merged.astype(BF16), w_o_ref[...], preferred_element_type=F32)
    x1 = _layer_norm(ALPHA * x + mix, ln_g, ln_b)
    return x1, _route(x1, w_rt_hi_ref, w_rt_lo_ref, b_rt)


def _mixer_kernel(sinks_ref, x_ref, w_in_ref, b_in_ref, w_conv_ref, b_conv_ref, w_gates_ref, b_rg_ref, b_ig_ref,
                  lam_ref, w_lru_out_ref, w_attn_out_ref, w_o_ref, ln_g_ref, ln_b_ref, w_rt_hi_ref, w_rt_lo_ref,
                  b_rt_ref, x1_s_ref, route_s_ref,
                  x1_ref, route_ref, kwin_ref, vwin_ref, conv_ref, h_ref,
                  conv_buf, h_carry, kcat, vcat, att_buf, *, tiles_per_seq, n_tiles):
    step = pl.program_id(0)

    @pl.when(step < n_tiles)
    def _():
        _mixer_tile(lax.rem(step, tiles_per_seq), sinks_ref, x_ref, w_in_ref, b_in_ref, w_conv_ref, b_conv_ref,
                    w_gates_ref, b_rg_ref, b_ig_ref, lam_ref, w_lru_out_ref, w_attn_out_ref, w_o_ref, ln_g_ref,
                    ln_b_ref, w_rt_hi_ref, w_rt_lo_ref, b_rt_ref, x1_ref, route_ref, kwin_ref, vwin_ref, conv_ref,
                    h_ref, conv_buf, h_carry, kcat, vcat, att_buf)

    @pl.when(step == n_tiles)
    def _():
        n_s = x1_s_ref.shape[0]
        x1_ref[0:n_s, :] = x1_s_ref[...]
        route_ref[0:n_s, :] = route_s_ref[...]


def _mixer_tile(t, sinks_ref, x_ref, w_in_ref, b_in_ref, w_conv_ref, b_conv_ref, w_gates_ref, b_rg_ref, b_ig_ref,
                lam_ref, w_lru_out_ref, w_attn_out_ref, w_o_ref, ln_g_ref, ln_b_ref, w_rt_hi_ref, w_rt_lo_ref,
                b_rt_ref, x1_ref, route_ref, kwin_ref, vwin_ref, conv_ref, h_ref,
                conv_buf, h_carry, kcat, vcat, att_buf):
    T = SEQ_TILE

    @pl.when(t == 0)
    def _():
        conv_buf[0:SUBLANES, :] = jnp.zeros((SUBLANES, LRU_WIDTH), F32)
        h_carry[...] = jnp.zeros_like(h_carry)
        kcat[0:WINDOW, :] = jnp.zeros((WINDOW, KV_WIDTH), BF16)
        vcat[0:WINDOW, :] = jnp.zeros((WINDOW, KV_WIDTH), BF16)

    x = x_ref[0]
    xb = x.astype(BF16)

    def proj(lo, width):
        return jnp.dot(xb, w_in_ref[:, lo:lo + width], preferred_element_type=F32) + b_in_ref[:, lo:lo + width]

    xl = proj(OFF_XL, LRU_WIDTH)
    conv_buf[SUBLANES:SUBLANES + T, :] = xl
    wc = w_conv_ref[...]
    xc = wc[0:1] * conv_buf[SUBLANES - 3:SUBLANES - 3 + T, :]
    xc = xc + wc[1:2] * conv_buf[SUBLANES - 2:SUBLANES - 2 + T, :]
    xc = xc + wc[2:3] * conv_buf[SUBLANES - 1:SUBLANES - 1 + T, :]
    xc = xc + wc[3:4] * xl + b_conv_ref[...]
    conv_ref[0] = conv_buf[T + SUBLANES - (CONV_W - 1):T + SUBLANES, :]
    conv_buf[0:SUBLANES, :] = conv_buf[T:T + SUBLANES, :]

    a, u = _lru_gates(xc, w_gates_ref, b_rg_ref[...], b_ig_ref[...], lam_ref[...])
    a_cum, h = _linear_scan(a, u)
    h = a_cum * h_carry[0:1, :] + h
    h_last = h[T - 1:T, :]
    h_carry[0:1, :] = h_last
    h_ref[0] = h_last
    rec = h * jax.nn.gelu(proj(OFF_YL, LRU_WIDTH))

    q2 = (proj(OFF_Q, N_HEADS * HEAD_DIM) * (ATTN_SCALE * LOG2_E)).astype(BF16)
    k = proj(OFF_K, KV_WIDTH)
    v = proj(OFF_V, KV_WIDTH)
    kwin_ref[0] = k[T - WINDOW:, :]
    vwin_ref[0] = v[T - WINDOW:, :]
    kcat[WINDOW:WINDOW + T, :] = k.astype(BF16)
    vcat[WINDOW:WINDOW + T, :] = v.astype(BF16)

    qi = lax.broadcasted_iota(jnp.int32, (WINDOW, 2 * WINDOW), 0)
    kj = lax.broadcasted_iota(jnp.int32, (WINDOW, 2 * WINDOW), 1)
    band = (kj > qi) & (kj <= qi + WINDOW)
    grp_row = lax.broadcasted_iota(jnp.int32, (GROUP * WINDOW, 1), 0) // WINDOW
    for qb in range(T // WINDOW):
        if qb == 0:
            first_key = jnp.where(t == 0, WINDOW, 0)
            mask1 = band & (kj >= first_key)
        else:
            mask1 = band
        bias = jnp.concatenate([jnp.where(mask1, 0.0, NEG_INF)] * GROUP, axis=0)
        r0 = qb * WINDOW
        qq = q2[r0:r0 + WINDOW, :]
        for j in range(N_KV):
            kjb = kcat[r0:r0 + 2 * WINDOW, j * HEAD_DIM:(j + 1) * HEAD_DIM]
            vjb = vcat[r0:r0 + 2 * WINDOW, j * HEAD_DIM:(j + 1) * HEAD_DIM]
            qs = jnp.concatenate(
                [qq[:, (j * GROUP + g) * HEAD_DIM:(j * GROUP + g + 1) * HEAD_DIM] for g in range(GROUP)], axis=0)
            s = lax.dot_general(qs, kjb, (((1,), (1,)), ((), ())), preferred_element_type=F32) + bias
            sink = jnp.zeros((GROUP * WINDOW, 1), F32)
            for g in range(GROUP):
                sink = jnp.where(grp_row == g, sinks_ref[j * GROUP + g] * LOG2_E, sink)
            m = jnp.maximum(jnp.max(s, axis=-1, keepdims=True), sink)
            p = jnp.exp2(s - m)
            inv = 1.0 / (jnp.sum(p, axis=-1, keepdims=True) + jnp.exp2(sink - m))
            o = jnp.dot(p.astype(BF16), vjb, preferred_element_type=F32) * inv
            for g in range(GROUP):
                hcol = (j * GROUP + g) * HEAD_DIM
                att_buf[r0:r0 + WINDOW, hcol:hcol + HEAD_DIM] = o[g * WINDOW:(g + 1) * WINDOW, :]
    kcat[0:WINDOW, :] = kcat[T:T + WINDOW, :]
    vcat[0:WINDOW, :] = vcat[T:T + WINDOW, :]

    x1, route = _merge_norm_route(x, rec, att_buf[...], proj(OFF_GL, D_MODEL), proj(OFF_GA, D_MODEL),
                                  w_lru_out_ref, w_attn_out_ref, w_o_ref, ln_g_ref[...], ln_b_ref[...],
                                  w_rt_hi_ref, w_rt_lo_ref, b_rt_ref[...])
    x1_ref[...] = x1
    route_ref[...] = route


def _mixer_call(x_prompt, x1_s, route_s, sinks, wts):
    B, S, _ = x_prompt.shape
    T = SEQ_TILE
    nt = S // T
    n_tiles = B * nt
    n_rows_total = B * S + x1_s.shape[0]
    weight_args = (wts["w_in"], wts["b_in"], wts["w_conv"], wts["b_conv"], wts["w_gates"], wts["b_rg"], wts["b_ig"],
                   wts["lam"], wts["w_lru_out"], wts["w_attn_out"], wts["w_o"], wts["ln1_g"], wts["ln1_b"],
                   wts["w_rt_hi"], wts["w_rt_lo"], wts["b_rt"], x1_s, route_s)
    seq = lambda i: jnp.minimum(i, n_tiles - 1) // nt
    in_specs = [pl.BlockSpec(memory_space=pltpu.SMEM),
                pl.BlockSpec((1, T, D_MODEL), lambda i: (seq(i), lax.rem(jnp.minimum(i, n_tiles - 1), nt), 0))]
    in_specs += [_const_spec(w.shape) for w in weight_args]
    out_shape = (
        jax.ShapeDtypeStruct((n_rows_total, D_MODEL), F32),
        jax.ShapeDtypeStruct((n_rows_total, ROUTE_WIDTH), F32),
        jax.ShapeDtypeStruct((B, WINDOW, KV_WIDTH), F32),
        jax.ShapeDtypeStruct((B, WINDOW, KV_WIDTH), F32),
        jax.ShapeDtypeStruct((B, CONV_W - 1, LRU_WIDTH), F32),
        jax.ShapeDtypeStruct((B, 1, LRU_WIDTH), F32),
    )
    out_specs = (
        pl.BlockSpec((T, D_MODEL), lambda i: (i, 0)),
        pl.BlockSpec((T, ROUTE_WIDTH), lambda i: (i, 0)),
        pl.BlockSpec((1, WINDOW, KV_WIDTH), lambda i: (seq(i), 0, 0)),
        pl.BlockSpec((1, WINDOW, KV_WIDTH), lambda i: (seq(i), 0, 0)),
        pl.BlockSpec((1, CONV_W - 1, LRU_WIDTH), lambda i: (seq(i), 0, 0)),
        pl.BlockSpec((1, 1, LRU_WIDTH), lambda i: (seq(i), 0, 0)),
    )
    scratch = [
        pltpu.VMEM((T + 2 * SUBLANES, LRU_WIDTH), F32),
        pltpu.VMEM((SUBLANES, LRU_WIDTH), F32),
        pltpu.VMEM((T + WINDOW, KV_WIDTH), BF16),
        pltpu.VMEM((T + WINDOW, KV_WIDTH), BF16),
        pltpu.VMEM((T, N_HEADS * HEAD_DIM), F32),
    ]
    return pl.pallas_call(
        functools.partial(_mixer_kernel, tiles_per_seq=nt, n_tiles=n_tiles),
        grid=(n_tiles + 1,),
        in_specs=in_specs,
        out_specs=out_specs,
        out_shape=out_shape,
        scratch_shapes=scratch,
        compiler_params=pltpu.CompilerParams(dimension_semantics=("arbitrary",),
                                             vmem_limit_bytes=VMEM_LIMIT_BYTES),
        name="mixer_prompt",
    )(sinks, x_prompt, *weight_args)


def _sample_proj_kernel(x_ref, w_ref, b_ref, u_ref):
    u_ref[...] = jnp.dot(x_ref[...].astype(BF16), w_ref[...], preferred_element_type=F32) + b_ref[...]


def _sample_proj_call(x_s, w_in, b_in):
    n = x_s.shape[0]
    tn = SAMPLE_PROJ_TILE
    return pl.pallas_call(
        _sample_proj_kernel,
        grid=(IN_WIDTH // tn,),
        in_specs=[pl.BlockSpec((n, D_MODEL), lambda c: (0, 0)),
                  pl.BlockSpec((D_MODEL, tn), lambda c: (0, c)),
                  pl.BlockSpec((1, tn), lambda c: (0, c))],
        out_specs=pl.BlockSpec((n, tn), lambda c: (0, c)),
        out_shape=jax.ShapeDtypeStruct((n, IN_WIDTH), F32),
        compiler_params=pltpu.CompilerParams(dimension_semantics=("arbitrary",)),
        name="sample_proj",
    )(x_s, w_in, b_in)


def _sample_attn_kernel(q_ref, kn_ref, vn_ref, kn_row_ref, vn_row_ref, ck_ref, cv_ref, sinks_ref,
                        att_ref, kwin_ref, vwin_ref):
    tb = q_ref.shape[0]
    key_pos = lax.broadcasted_iota(jnp.int32, (tb, GROUP, WINDOW), 2)
    for j in range(N_KV):
        qj = q_ref[:, j * GROUP:(j + 1) * GROUP, :]
        kc = ck_ref[:, :, j * HEAD_DIM:(j + 1) * HEAD_DIM]
        vc = cv_ref[:, :, j * HEAD_DIM:(j + 1) * HEAD_DIM]
        s_c = jnp.einsum("bgd,bsd->bgs", qj.astype(BF16), kc.astype(BF16), preferred_element_type=F32) * ATTN_SCALE
        s_c = jnp.where(key_pos >= 1, s_c, NEG_INF)
        kn = kn_ref[:, j:j + 1, :]
        vn = vn_ref[:, j:j + 1, :]
        s_n = jnp.sum(qj * kn, axis=-1, keepdims=True) * ATTN_SCALE
        sink = sinks_ref[j][None]
        m = jnp.maximum(jnp.maximum(jnp.max(s_c, axis=-1, keepdims=True), s_n), sink)
        p_c = jnp.exp(s_c - m)
        p_n = jnp.exp(s_n - m)
        inv = 1.0 / (jnp.sum(p_c, axis=-1, keepdims=True) + p_n + jnp.exp(sink - m))
        o = jnp.einsum("bgs,bsd->bgd", (p_c * inv).astype(BF16), vc.astype(BF16), preferred_element_type=F32)
        att_ref[:, j * GROUP:(j + 1) * GROUP, :] = o + (p_n * inv) * vn
    kwin_ref[:, 0:WINDOW - 1, :] = ck_ref[:, 1:WINDOW, :]
    kwin_ref[:, WINDOW - 1:WINDOW, :] = kn_row_ref[...]
    vwin_ref[:, 0:WINDOW - 1, :] = cv_ref[:, 1:WINDOW, :]
    vwin_ref[:, WINDOW - 1:WINDOW, :] = vn_row_ref[...]


def _sample_attn_call(q3, kn3, vn3, kn_row, vn_row, ck, cv, sinks3):
    n = q3.shape[0]
    tb = SAMPLE_ATTN_TILE
    b3 = lambda i: (i, 0, 0)
    return pl.pallas_call(
        _sample_attn_kernel,
        grid=(n // tb,),
        in_specs=[pl.BlockSpec((tb, N_HEADS, HEAD_DIM), b3),
                  pl.BlockSpec((tb, N_KV, HEAD_DIM), b3),
                  pl.BlockSpec((tb, N_KV, HEAD_DIM), b3),
                  pl.BlockSpec((tb, 1, KV_WIDTH), b3),
                  pl.BlockSpec((tb, 1, KV_WIDTH), b3),
                  pl.BlockSpec((tb, WINDOW, KV_WIDTH), b3),
                  pl.BlockSpec((tb, WINDOW, KV_WIDTH), b3),
                  pl.BlockSpec((N_KV, GROUP, 1), lambda i: (0, 0, 0))],
        out_specs=(pl.BlockSpec((tb, N_HEADS, HEAD_DIM), b3),
                   pl.BlockSpec((tb, WINDOW, KV_WIDTH), b3),
                   pl.BlockSpec((tb, WINDOW, KV_WIDTH), b3)),
        out_shape=(jax.ShapeDtypeStruct((n, N_HEADS, HEAD_DIM), F32),
                   jax.ShapeDtypeStruct((n, WINDOW, KV_WIDTH), F32),
                   jax.ShapeDtypeStruct((n, WINDOW, KV_WIDTH), F32)),
        compiler_params=pltpu.CompilerParams(dimension_semantics=("arbitrary",)),
        name="sample_attn",
    )(q3, kn3, vn3, kn_row, vn_row, ck, cv, sinks3)


def _sample_mix_kernel(x_ref, u_ref, att_ref, st_ref, h0_ref, w_conv_ref, b_conv_ref, w_gates_ref, b_rg_ref,
                       b_ig_ref, lam_ref, w_lru_out_ref, w_attn_out_ref, w_o_ref, ln_g_ref, ln_b_ref,
                       w_rt_hi_ref, w_rt_lo_ref, b_rt_ref,
                       x1_ref, route_ref, conv_ref, h_ref):
    xl = u_ref[:, OFF_XL:OFF_XL + LRU_WIDTH]
    wc = w_conv_ref[...]
    xc = wc[0:1] * st_ref[0]
    xc = xc + wc[1:2] * st_ref[1]
    xc = xc + wc[2:3] * st_ref[2]
    xc = xc + wc[3:4] * xl + b_conv_ref[...]
    conv_ref[0] = st_ref[1]
    conv_ref[1] = st_ref[2]
    conv_ref[2] = xl
    a, u = _lru_gates(xc, w_gates_ref, b_rg_ref[...], b_ig_ref[...], lam_ref[...])
    h = a * h0_ref[...] + u
    h_ref[...] = h
    rec = h * jax.nn.gelu(u_ref[:, OFF_YL:OFF_YL + LRU_WIDTH])
    x1, route = _merge_norm_route(x_ref[...], rec, att_ref[...], u_ref[:, OFF_GL:OFF_GL + D_MODEL],
                                  u_ref[:, OFF_GA:OFF_GA + D_MODEL], w_lru_out_ref, w_attn_out_ref, w_o_ref,
                                  ln_g_ref[...], ln_b_ref[...], w_rt_hi_ref, w_rt_lo_ref, b_rt_ref[...])
    x1_ref[...] = x1
    route_ref[...] = route


def _sample_mix_call(x_s, u_s, att, st_t, h0, wts):
    n = x_s.shape[0]
    weight_args = (wts["w_conv"], wts["b_conv"], wts["w_gates"], wts["b_rg"], wts["b_ig"], wts["lam"],
                   wts["w_lru_out"], wts["w_attn_out"], wts["w_o"], wts["ln1_g"], wts["ln1_b"],
                   wts["w_rt_hi"], wts["w_rt_lo"], wts["b_rt"])
    args = (x_s, u_s, att, st_t, h0) + weight_args
    out_shapes = ((n, D_MODEL), (n, ROUTE_WIDTH), (CONV_W - 1, n, LRU_WIDTH), (n, LRU_WIDTH))
    return pl.pallas_call(
        _sample_mix_kernel,
        grid=(1,),
        in_specs=[_const_spec(a.shape) for a in args],
        out_specs=tuple(_const_spec(s) for s in out_shapes),
        out_shape=tuple(jax.ShapeDtypeStruct(s, F32) for s in out_shapes),
        compiler_params=pltpu.CompilerParams(dimension_semantics=("arbitrary",),
                                             vmem_limit_bytes=VMEM_LIMIT_BYTES),
        name="sample_mix",
    )(*args)


def _one_hot(mask):
    return jnp.where(mask, 1.0, 0.0).astype(BF16)


def _pack_rows(x):
    half = x.shape[1] // 2
    lo = lax.shift_right_logical(lax.bitcast_convert_type(x[:, :half], jnp.uint32), jnp.uint32(16))
    hi = lax.bitcast_convert_type(x[:, half:], jnp.uint32) & jnp.uint32(0xFFFF0000)
    return lo | hi


def _unpack_rows(words):
    lo = lax.bitcast_convert_type(lax.shift_left(words, jnp.uint32(16)), F32)
    hi = lax.bitcast_convert_type(words & jnp.uint32(0xFFFF0000), F32)
    return jnp.concatenate([lo.astype(BF16), hi.astype(BF16)], axis=1)


def _dispatch_kernel(x1_ref, route_ref, xs_hbm, goff_ref, gtot_ref,
                     stage, zero_rows, tri, g_vmem, gvec, g_smem, nd_smem, sem_stage, sem_g, sem_zero,
                     *, n_tokens, seg_cap):
    i = pl.program_id(0)
    last = pl.num_programs(0) - 1
    td = DISPATCH_TILE
    n_assign = TOP_K * td

    def g_copy():
        return pltpu.make_async_copy(g_vmem.at[pl.ds(0, 1)], g_smem, sem_g)

    @pl.when(i == 0)
    def _():
        r = lax.broadcasted_iota(jnp.int32, (n_assign, n_assign), 0)
        c = lax.broadcasted_iota(jnp.int32, (n_assign, n_assign), 1)
        tri[...] = _one_hot(r < c)
        gvec[...] = jnp.zeros_like(gvec)
        g_vmem[...] = jnp.zeros_like(g_vmem)
        nd_smem[0] = 0
        g_copy().start()

    n_valid = n_tokens - i * td
    route = route_ref[...]
    valid_col = lax.broadcasted_iota(jnp.int32, (td, 1), 0) < n_valid
    lane = lax.broadcasted_iota(jnp.int32, (td, ROUTE_WIDTH), 1).astype(F32)
    e0c = jnp.where(valid_col, route[:, 0:1], -1.0)
    e1c = jnp.where(valid_col, route[:, 1:2], -1.0)
    cnt_row = jnp.sum(jnp.where(lane == e0c, 1.0, 0.0) + jnp.where(lane == e1c, 1.0, 0.0), axis=0, keepdims=True)
    rounds = (jnp.max(cnt_row).astype(jnp.int32) + SLOT_CAP - 1) // SLOT_CAP
    goff_ref[0] = gvec[0:1, :].astype(jnp.int32)
    gvec[0:1, :] = gvec[0:1, :] + jnp.ceil(cnt_row * (1.0 / SUBLANES)) * SUBLANES
    gtot_ref[...] = gvec[0:1, :].astype(jnp.int32)

    route_t = route.T
    valid_row = lax.broadcasted_iota(jnp.int32, (1, td), 1) < n_valid
    e_row = jnp.concatenate([jnp.where(valid_row, route_t[0:1, :], -1.0),
                             jnp.where(valid_row, route_t[1:2, :], -1.0)], axis=1)
    expert_sub = lax.broadcasted_iota(jnp.int32, (N_EXPERTS, n_assign), 0).astype(F32)
    onehot_t = expert_sub == e_row
    rank_t = jnp.dot(_one_hot(onehot_t), tri[...], preferred_element_type=F32)
    rank_row = jnp.sum(jnp.where(onehot_t, rank_t, 0.0), axis=0, keepdims=True)
    xb = jnp.where(valid_col, x1_ref[...], 0.0).astype(BF16)
    slot = lax.broadcasted_iota(jnp.int32, (N_SLOTS, td), 0).astype(F32)

    def stage_copy(e, dst_row):
        src = stage.at[pl.ds(pl.multiple_of(e * SLOT_CAP, SLOT_CAP), SLOT_CAP)]
        return pltpu.make_async_copy(src, xs_hbm.at[pl.ds(dst_row, SLOT_CAP)], sem_stage)

    def wait_outstanding():
        def wait_one(_, carry):
            stage_copy(0, 0).wait()
            return carry
        lax.fori_loop(0, nd_smem[0], wait_one, 0)
        nd_smem[0] = 0

    g_copy().wait()

    def round_body(r, carry):
        lo = r * SLOT_CAP
        lo_f = lo.astype(F32)
        in_round = (rank_row >= lo_f) & (rank_row < lo_f + SLOT_CAP) & (e_row >= 0.0)
        key = jnp.where(in_round, e_row * SLOT_CAP + (rank_row - lo_f), -1.0)
        perm = jnp.where(slot == key[:, :td], 1.0, jnp.where(slot == key[:, td:], 1.0, 0.0)).astype(BF16)
        sorted_rows = jnp.dot(perm, xb, preferred_element_type=F32)
        wait_outstanding()
        stage[...] = _pack_rows(sorted_rows)

        def issue(e, carry):
            stage_copy(e, pl.multiple_of(e * seg_cap + g_smem[0, e] + lo, SUBLANES)).start()
            return carry
        lax.fori_loop(0, N_EXPERTS, issue, 0)
        nd_smem[0] = N_EXPERTS
        return carry
    lax.fori_loop(0, rounds, round_body, 0)

    g_vmem[0:1, :] = gvec[0:1, :].astype(jnp.int32)
    g_copy().start()

    @pl.when(i == last)
    def _():
        g_copy().wait()
        wait_outstanding()
        zero_rows[...] = jnp.zeros_like(zero_rows)

        def pad_copy(e):
            dst_row = pl.multiple_of(e * seg_cap + g_smem[0, e], SUBLANES)
            return pltpu.make_async_copy(zero_rows, xs_hbm.at[pl.ds(dst_row, SEG_PAD)], sem_zero)

        def start(e, carry):
            pad_copy(e).start()
            return carry
        lax.fori_loop(0, N_EXPERTS, start, 0)

        def wait(e, carry):
            pad_copy(e).wait()
            return carry
        lax.fori_loop(0, N_EXPERTS, wait, 0)


def _dispatch_tiles(n_tokens):
    return (n_tokens + DISPATCH_TILE - 1) // DISPATCH_TILE


def _segment_capacity(n_tokens):
    alignment_slack = (SUBLANES - 1) * _dispatch_tiles(n_tokens)
    return (n_tokens + alignment_slack + SEG_PAD + MOE_TILE - 1) // MOE_TILE * MOE_TILE


def _dispatch_call(x1_all, route_all):
    n_tokens = x1_all.shape[0]
    td = DISPATCH_TILE
    n_tiles = (n_tokens + td - 1) // td
    seg_cap = _segment_capacity(n_tokens)
    n_assign = TOP_K * td
    return pl.pallas_call(
        functools.partial(_dispatch_kernel, n_tokens=n_tokens, seg_cap=seg_cap),
        grid=(n_tiles,),
        in_specs=[pl.BlockSpec((td, D_MODEL), lambda i: (i, 0)),
                  pl.BlockSpec((td, ROUTE_WIDTH), lambda i: (i, 0))],
        out_specs=(pl.BlockSpec(memory_space=pl.ANY),
                   pl.BlockSpec((1, 1, ROUTE_WIDTH), lambda i: (i, 0, 0)),
                   pl.BlockSpec((1, ROUTE_WIDTH), lambda i: (0, 0))),
        out_shape=(jax.ShapeDtypeStruct((N_EXPERTS * seg_cap, PACKED_WIDTH), jnp.uint32),
                   jax.ShapeDtypeStruct((n_tiles, 1, ROUTE_WIDTH), jnp.int32),
                   jax.ShapeDtypeStruct((1, ROUTE_WIDTH), jnp.int32)),
        scratch_shapes=[
            pltpu.VMEM((N_SLOTS, PACKED_WIDTH), jnp.uint32),
            pltpu.VMEM((SEG_PAD, PACKED_WIDTH), jnp.uint32),
            pltpu.VMEM((n_assign, n_assign), BF16),
            pltpu.VMEM((SUBLANES, ROUTE_WIDTH), jnp.int32),
            pltpu.VMEM((SUBLANES, ROUTE_WIDTH), F32),
            pltpu.SMEM((1, ROUTE_WIDTH), jnp.int32),
            pltpu.SMEM((1,), jnp.int32),
            pltpu.SemaphoreType.DMA(()),
            pltpu.SemaphoreType.DMA(()),
            pltpu.SemaphoreType.DMA(()),
        ],
        compiler_params=pltpu.CompilerParams(dimension_semantics=("arbitrary",),
                                             vmem_limit_bytes=VMEM_LIMIT_BYTES),
        name="moe_dispatch",
    )(x1_all, route_all)


def _moe_kernel(blk_e_ref, blk_j_ref, n_used_ref, xs_ref, wg_ref, wu_ref, wd_ref, ys_ref, wg_bf, wu_bf, wd_bf):
    del blk_j_ref
    b = pl.program_id(0)

    @pl.when(b < n_used_ref[0])
    def _():
        prev_e = blk_e_ref[jnp.maximum(b - 1, 0)]

        @pl.when((b == 0) | (blk_e_ref[b] != prev_e))
        def _():
            wg_bf[...] = wg_ref[0].astype(BF16)
            wu_bf[...] = wu_ref[0].astype(BF16)
            wd_bf[...] = wd_ref[0].astype(BF16)

        xb = _unpack_rows(xs_ref[...])
        g = jnp.dot(xb, wg_bf[...], preferred_element_type=F32)
        u = jnp.dot(xb, wu_bf[...], preferred_element_type=F32)
        hmid = (g * _sigmoid(g)) * u
        y = jnp.dot(hmid.astype(BF16), wd_bf[...], preferred_element_type=F32)
        ys_ref[...] = _pack_rows(y.astype(BF16).astype(F32))


def _expert_blocks(gtot, n_blocks):
    rows = gtot[0, :N_EXPERTS]
    nb = (rows + SLOT_CAP + MOE_TILE - 1) // MOE_TILE
    ends = jnp.cumsum(nb)
    n_used = ends[-1]
    step = jnp.minimum(jnp.arange(n_blocks, dtype=jnp.int32), n_used - 1)
    blk_e = jnp.sum((step[:, None] >= ends[None, :]).astype(jnp.int32), axis=1)
    blk_j = step - (ends - nb)[blk_e]
    return blk_e, blk_j.astype(jnp.int32), n_used.reshape(1).astype(jnp.int32)


def _moe_call(gtot, xs, w_gate, w_up, w_down, n_tokens):
    seg_cap = _segment_capacity(n_tokens)
    seg_blocks = seg_cap // MOE_TILE
    max_rows = n_tokens * TOP_K + N_EXPERTS * (SUBLANES - 1) * _dispatch_tiles(n_tokens)
    n_blocks = (max_rows + N_EXPERTS * (SLOT_CAP + MOE_TILE - 1)) // MOE_TILE
    blk_e, blk_j, n_used = _expert_blocks(gtot, n_blocks)
    row_block = lambda b, be, bj, nu: (be[b] * seg_blocks + bj[b], 0)
    weight_block = lambda b, be, bj, nu: (be[b], 0, 0)
    grid_spec = pltpu.PrefetchScalarGridSpec(
        num_scalar_prefetch=3,
        grid=(n_blocks,),
        in_specs=[
            pl.BlockSpec((MOE_TILE, PACKED_WIDTH), row_block),
            pl.BlockSpec((1, D_MODEL, EXPERT_FF), weight_block),
            pl.BlockSpec((1, D_MODEL, EXPERT_FF), weight_block),
            pl.BlockSpec((1, EXPERT_FF, D_MODEL), weight_block),
        ],
        out_specs=pl.BlockSpec((MOE_TILE, PACKED_WIDTH), row_block),
        scratch_shapes=[
            pltpu.VMEM((D_MODEL, EXPERT_FF), BF16),
            pltpu.VMEM((D_MODEL, EXPERT_FF), BF16),
            pltpu.VMEM((EXPERT_FF, D_MODEL), BF16),
        ],
    )
    return pl.pallas_call(
        _moe_kernel,
        grid_spec=grid_spec,
        out_shape=jax.ShapeDtypeStruct(xs.shape, xs.dtype),
        compiler_params=pltpu.CompilerParams(dimension_semantics=("arbitrary",),
                                             vmem_limit_bytes=VMEM_LIMIT_BYTES),
        name="moe_experts",
    )(blk_e, blk_j, n_used, xs, w_gate, w_up, w_down)


def _combine_kernel(gcur_ref, gnext_ref, x1_ref, route_ref, ys_hbm, ln_g_ref, ln_b_ref, y_ref,
                    stage, acc, tri, sem, *, seg_cap):
    td = x1_ref.shape[0]
    n_assign = TOP_K * td
    i = pl.program_id(0)
    buf = lax.rem(i, 2)

    def stage_copy(g_ref, e, first_rank, to_buf):
        src_row = pl.multiple_of(e * seg_cap + g_ref[0, 0, e] + first_rank, SUBLANES)
        dst = stage.at[to_buf, pl.ds(pl.multiple_of(e * SLOT_CAP, SLOT_CAP), SLOT_CAP)]
        return pltpu.make_async_copy(ys_hbm.at[pl.ds(src_row, SLOT_CAP)], dst, sem.at[to_buf])

    def fetch_first_round(g_ref, to_buf):
        def start(e, carry):
            stage_copy(g_ref, e, 0, to_buf).start()
            return carry
        lax.fori_loop(0, N_EXPERTS, start, 0)

    @pl.when(i == 0)
    def _():
        r = lax.broadcasted_iota(jnp.int32, (n_assign, n_assign), 0)
        c = lax.broadcasted_iota(jnp.int32, (n_assign, n_assign), 1)
        tri[...] = _one_hot(c < r)
        fetch_first_round(gcur_ref, 0)

    def count(e):
        return gnext_ref[0, 0, e] - gcur_ref[0, 0, e]

    n_max = lax.fori_loop(0, N_EXPERTS, lambda e, m: jnp.maximum(m, count(e)), 0)
    rounds = (n_max + SLOT_CAP - 1) // SLOT_CAP

    route = route_ref[...]
    e0, e1, w0, w1 = route[:, 0:1], route[:, 1:2], route[:, 2:3], route[:, 3:4]
    lane = lax.broadcasted_iota(jnp.int32, (td, ROUTE_WIDTH), 1).astype(F32)
    onehot = jnp.concatenate([lane == e0, lane == e1], axis=0)
    onehot_f = jnp.where(onehot, 1.0, 0.0)
    rank_mat = jnp.dot(tri[...], onehot_f.astype(BF16), preferred_element_type=F32)
    rank = jnp.sum(jnp.where(onehot, rank_mat, 0.0), axis=1, keepdims=True)
    r0, r1 = rank[:td], rank[td:]
    cnt_row = jnp.sum(onehot_f, axis=0, keepdims=True)

    slot_col = lax.broadcasted_iota(jnp.int32, (N_SLOTS, 1), 0)
    slot_expert = slot_col // SLOT_CAP
    slot_rank = slot_col - slot_expert * SLOT_CAP
    lane_s = lax.broadcasted_iota(jnp.int32, (N_SLOTS, ROUTE_WIDTH), 1)
    n_col = jnp.sum(jnp.where(lane_s == slot_expert, cnt_row, 0.0), axis=1, keepdims=True)
    slot_lane = lax.broadcasted_iota(jnp.int32, (td, N_SLOTS), 1).astype(F32)

    def wait_copies(n):
        def wait_one(_, c):
            stage_copy(gcur_ref, 0, 0, buf).wait()
            return c
        lax.fori_loop(0, n, wait_one, 0)

    def weighted_rows(lo):
        lo_f = lo.astype(F32)
        live = (slot_rank + lo).astype(F32) < n_col
        rows = _unpack_rows(jnp.where(live, stage[buf], jnp.uint32(0)))

        def gathered(e_col, r_col):
            in_round = (r_col >= lo_f) & (r_col < lo_f + SLOT_CAP)
            key = jnp.where(in_round, e_col * SLOT_CAP + (r_col - lo_f), -1.0)
            return jnp.dot(_one_hot(slot_lane == key), rows, preferred_element_type=F32)
        return w0 * gathered(e0, r0) + w1 * gathered(e1, r1)

    wait_copies(N_EXPERTS)

    @pl.when(i + 1 < pl.num_programs(0))
    def _():
        fetch_first_round(gnext_ref, 1 - buf)

    acc[...] = weighted_rows(jnp.int32(0))

    def later_round(r, carry):
        lo = r * SLOT_CAP

        def start(e, n_started):
            has_rows = count(e) > lo

            @pl.when(has_rows)
            def _():
                stage_copy(gcur_ref, e, lo, buf).start()
            return n_started + has_rows.astype(jnp.int32)
        wait_copies(lax.fori_loop(0, N_EXPERTS, start, 0))
        acc[...] = acc[...] + weighted_rows(lo)
        return carry
    lax.fori_loop(1, rounds, later_round, 0)

    y_ref[...] = _layer_norm(ALPHA * x1_ref[...] + acc[...], ln_g_ref[...], ln_b_ref[...])


def _combine_call(goff, x1_all, route_all, ys, ln_g, ln_b, first_row, n_rows, tile, n_tokens):
    assert first_row % DISPATCH_TILE == 0 and first_row % tile == 0
    assert tile == DISPATCH_TILE or n_rows == tile
    first_block = first_row // tile
    first_goff = first_row // DISPATCH_TILE
    n_assign = TOP_K * tile
    grid_spec = pl.GridSpec(
        grid=(n_rows // tile,),
        in_specs=[
            pl.BlockSpec((1, 1, ROUTE_WIDTH), lambda i: (first_goff + i, 0, 0), memory_space=pltpu.SMEM),
            pl.BlockSpec((1, 1, ROUTE_WIDTH), lambda i: (first_goff + i + 1, 0, 0), memory_space=pltpu.SMEM),
            pl.BlockSpec((tile, D_MODEL), lambda i: (first_block + i, 0)),
            pl.BlockSpec((tile, ROUTE_WIDTH), lambda i: (first_block + i, 0)),
            pl.BlockSpec(memory_space=pl.ANY),
            _const_spec(ln_g.shape),
            _const_spec(ln_b.shape),
        ],
        out_specs=pl.BlockSpec((tile, D_MODEL), lambda i: (i, 0)),
        scratch_shapes=[pltpu.VMEM((2, N_SLOTS, PACKED_WIDTH), jnp.uint32),
                        pltpu.VMEM((tile, D_MODEL), F32),
                        pltpu.VMEM((n_assign, n_assign), BF16),
                        pltpu.SemaphoreType.DMA((2,))],
    )
    return pl.pallas_call(
        functools.partial(_combine_kernel, seg_cap=_segment_capacity(n_tokens)),
        grid_spec=grid_spec,
        out_shape=jax.ShapeDtypeStruct((n_rows, D_MODEL), F32),
        compiler_params=pltpu.CompilerParams(dimension_semantics=("arbitrary",),
                                             vmem_limit_bytes=VMEM_LIMIT_BYTES),
        name="moe_combine",
    )(goff, goff, x1_all, route_all, ys, ln_g, ln_b)


def _prepare_weights(w_in, b_in, w_conv, b_conv, w_rg, b_rg, w_ig, b_ig, lru_lambda, w_lru_out, w_attn_out, w_o,
                     ln1_g, ln1_b, w_group, b_group, w_router, b_router):
    blocks_per_chunk = GATE_CHUNK // LRU_BLOCK

    def chunked_block_diag(w):
        w = w.reshape(N_GATE_CHUNKS, blocks_per_chunk, LRU_BLOCK, LRU_BLOCK)
        eye = jnp.eye(blocks_per_chunk, dtype=w.dtype)
        return jnp.einsum("cbij,bd->cbidj", w, eye).reshape(N_GATE_CHUNKS, GATE_CHUNK, GATE_CHUNK)

    w_gates = jnp.concatenate([chunked_block_diag(w_rg), chunked_block_diag(w_ig)], axis=-1).astype(BF16)
    w_rt = jnp.concatenate([w_group, w_router], axis=1)
    w_rt = jnp.pad(w_rt, ((0, 0), (0, ROUTE_WIDTH - w_rt.shape[1])))
    w_rt_hi = w_rt.astype(BF16)
    w_rt_lo = (w_rt - w_rt_hi.astype(F32)).astype(BF16)
    b_rt = jnp.pad(jnp.concatenate([b_group, b_router]), (0, ROUTE_WIDTH - N_GROUPS - N_EXPERTS))
    row = lambda v: v.reshape(1, -1)
    return dict(
        w_in=w_in.astype(BF16), b_in=row(b_in), w_conv=w_conv, b_conv=row(b_conv), w_gates=w_gates,
        b_rg=row(b_rg), b_ig=row(b_ig), lam=row(lru_lambda),
        w_lru_out=w_lru_out.astype(BF16), w_attn_out=w_attn_out.astype(BF16), w_o=w_o.astype(BF16),
        ln1_g=row(ln1_g), ln1_b=row(ln1_b), w_rt_hi=w_rt_hi, w_rt_lo=w_rt_lo, b_rt=row(b_rt))


def kernel(x_prompt, x_sample, cache_k, cache_v, state_conv, state_lru_h, w_in, b_in, w_conv, b_conv, w_rg, b_rg,
           w_ig, b_ig, lru_lambda, sinks, w_lru_out, w_attn_out, w_o, ln1_g, ln1_b, w_group, b_group, w_router,
           b_router, w_gate, w_up, w_down, ln2_g, ln2_b):
    B, S, _ = x_prompt.shape
    n_prompt = B * S
    n_sample = x_sample.shape[0]
    n_all = n_prompt + n_sample
    wts = _prepare_weights(w_in, b_in, w_conv, b_conv, w_rg, b_rg, w_ig, b_ig, lru_lambda, w_lru_out, w_attn_out,
                           w_o, ln1_g, ln1_b, w_group, b_group, w_router, b_router)

    x_s = x_sample.reshape(n_sample, D_MODEL)
    u_s = _sample_proj_call(x_s, wts["w_in"], wts["b_in"])
    q3 = u_s[:, OFF_Q:OFF_K].reshape(n_sample, N_HEADS, HEAD_DIM)
    k_new = u_s[:, OFF_K:OFF_V]
    v_new = u_s[:, OFF_V:OFF_GL]
    att3, k_win_s, v_win_s = _sample_attn_call(
        q3, k_new.reshape(n_sample, N_KV, HEAD_DIM), v_new.reshape(n_sample, N_KV, HEAD_DIM),
        k_new.reshape(n_sample, 1, KV_WIDTH), v_new.reshape(n_sample, 1, KV_WIDTH),
        cache_k.reshape(n_sample, WINDOW, KV_WIDTH), cache_v.reshape(n_sample, WINDOW, KV_WIDTH),
        sinks.reshape(N_KV, GROUP, 1))
    x1_s, route_s, conv_s_t, h_s = _sample_mix_call(
        x_s, u_s, att3.reshape(n_sample, N_HEADS * HEAD_DIM), jnp.transpose(state_conv, (1, 0, 2)), state_lru_h, wts)

    x1_all, route_all, k_win_p, v_win_p, conv_p, h_p = _mixer_call(x_prompt, x1_s, route_s, sinks, wts)

    xs, goff, gtot = _dispatch_call(x1_all, route_all)
    ys = _moe_call(gtot, xs, w_gate, w_up, w_down, n_all)
    goff = jnp.concatenate([goff, gtot[None]], axis=0)
    ln2_g2, ln2_b2 = ln2_g.reshape(1, -1), ln2_b.reshape(1, -1)
    y_p = _combine_call(goff, x1_all, route_all, ys, ln2_g2, ln2_b2, 0, n_prompt, DISPATCH_TILE, n_all)
    y_s = _combine_call(goff, x1_all, route_all, ys, ln2_g2, ln2_b2, n_prompt, n_sample, n_sample, n_all)

    kv_shape = (WINDOW, N_KV, HEAD_DIM)
    return (y_p.reshape(B, S, D_MODEL), y_s.reshape(n_sample, 1, D_MODEL),
            k_win_p.reshape((B,) + kv_shape), v_win_p.reshape((B,) + kv_shape), conv_p, h_p.reshape(B, LRU_WIDTH),
            k_win_s.reshape((n_sample,) + kv_shape), v_win_s.reshape((n_sample,) + kv_shape),
            jnp.transpose(conv_s_t, (1, 0, 2)), h_s)
```

```python
import functools

import jax
import jax.numpy as jnp
from jax import lax
from jax.experimental import pallas as pl
from jax.experimental.pallas import tpu as pltpu

F32 = jnp.float32
BF16 = jnp.bfloat16

D_MODEL = 1024
LRU_WIDTH = 1024
LRU_BLOCK = 64
CONV_W = 4
LRU_C = 8.0
N_HEADS = 16
N_KV = 4
GROUP = N_HEADS // N_KV
HEAD_DIM = 64
KV_WIDTH = N_KV * HEAD_DIM
WINDOW = 128
NEG_INF = -1e30
N_GROUPS = 4
EXPERTS_PER_GROUP = 8
N_EXPERTS = N_GROUPS * EXPERTS_PER_GROUP
TOP_K = 2
EXPERT_FF = D_MODEL // 2
DEPTH = 1
ALPHA = (2 * DEPTH) ** 0.25
LN_EPS = 1e-5
ATTN_SCALE = HEAD_DIM ** -0.5
LOG2_E = 1.4426950408889634

OFF_XL = 0
OFF_YL = OFF_XL + LRU_WIDTH
OFF_Q = OFF_YL + LRU_WIDTH
OFF_K = OFF_Q + N_HEADS * HEAD_DIM
OFF_V = OFF_K + KV_WIDTH
OFF_GL = OFF_V + KV_WIDTH
OFF_GA = OFF_GL + D_MODEL
IN_WIDTH = OFF_GA + D_MODEL

LANES = 128
SUBLANES = 8
MXU_DIM = 256
VMEM_LIMIT_BYTES = 56 * 1024 * 1024

GATE_CHUNK = MXU_DIM
N_GATE_CHUNKS = LRU_WIDTH // GATE_CHUNK
ROUTE_WIDTH = LANES

SEQ_TILE = 256
MOE_TILE = 512
DISPATCH_TILE = 256
SLOT_CAP = 32
N_SLOTS = N_EXPERTS * SLOT_CAP
SEG_PAD = MOE_TILE + SLOT_CAP
PACKED_WIDTH = D_MODEL // 2
SAMPLE_ATTN_TILE = 16
SAMPLE_PROJ_TILE = 512


def _const_spec(shape):
    nd = len(shape)
    return pl.BlockSpec(shape, lambda *_: (0,) * nd)


def _layer_norm(z, g, b):
    mu = jnp.mean(z, axis=-1, keepdims=True)
    zc = z - mu
    var = jnp.mean(zc * zc, axis=-1, keepdims=True)
    return zc * lax.rsqrt(var + LN_EPS) * g + b


def _sigmoid(x):
    return 1.0 / (1.0 + jnp.exp2(x * -LOG2_E))


def _softplus(x):
    return jnp.maximum(x, 0.0) + jnp.log1p(jnp.exp(-jnp.abs(x)))


def _bdot(a, b):
    return jnp.dot(a.astype(BF16), b.astype(BF16), preferred_element_type=F32)


def _lru_gates(xc, w_gates_ref, b_rg, b_ig, lam):
    xcb = xc.astype(BF16)
    r_parts, i_parts = [], []
    for c in range(N_GATE_CHUNKS):
        g = jnp.dot(xcb[:, c * GATE_CHUNK:(c + 1) * GATE_CHUNK], w_gates_ref[c], preferred_element_type=F32)
        r_parts.append(g[:, :GATE_CHUNK])
        i_parts.append(g[:, GATE_CHUNK:])
    r = _sigmoid(jnp.concatenate(r_parts, axis=1) + b_rg)
    i = _sigmoid(jnp.concatenate(i_parts, axis=1) + b_ig)
    log_a = (-LRU_C * r) * _softplus(-lam)
    a = jnp.exp(log_a)
    u = jnp.sqrt(1.0 - a * a) * (i * xc)
    return a, u


def _linear_scan(a, u, h_in):
    n, w = a.shape
    groups = n // SUBLANES
    a3 = a.reshape(groups, SUBLANES, w)
    u3 = u.reshape(groups, SUBLANES, w)
    row = lax.broadcasted_iota(jnp.int32, a3.shape, 1)
    d = 1
    while d < SUBLANES:
        has_prev = row >= d
        u3 = u3 + a3 * jnp.where(has_prev, pltpu.roll(u3, d, axis=1), 0.0)
        a3 = a3 * jnp.where(has_prev, pltpu.roll(a3, d, axis=1), 1.0)
        d *= 2
    carry = h_in
    out = []
    for g in range(groups):
        h_g = u3[g] + a3[g] * carry
        out.append(h_g)
        carry = h_g[SUBLANES - 1:SUBLANES, :]
    return jnp.concatenate(out, axis=0)


def _route(x1, w_hi_ref, w_lo_ref, b_rt):
    x_hi = x1.astype(BF16)
    x_lo = (x1 - x_hi.astype(F32)).astype(BF16)
    w_hi = w_hi_ref[...]
    logits = (jnp.dot(x_hi, w_hi, preferred_element_type=F32)
              + (jnp.dot(x_lo, w_hi, preferred_element_type=F32)
                 + jnp.dot(x_hi, w_lo_ref[...], preferred_element_type=F32))) + b_rt
    col = lax.broadcasted_iota(jnp.int32, logits.shape, 1)
    big = jnp.int32(ROUTE_WIDTH)
    is_g = col < N_GROUPS
    gl = jnp.where(is_g, logits, -jnp.inf)
    gmax = jnp.max(gl, axis=-1, keepdims=True)
    g_idx = jnp.min(jnp.where(gl == gmax, col, big), axis=-1, keepdims=True)
    p_g = 1.0 / jnp.sum(jnp.where(is_g, jnp.exp(gl - gmax), 0.0), axis=-1, keepdims=True)
    lo = N_GROUPS + g_idx * EXPERTS_PER_GROUP
    in_grp = (col >= lo) & (col < lo + EXPERTS_PER_GROUP)
    el = jnp.where(in_grp, logits, -jnp.inf)
    v1 = jnp.max(el, axis=-1, keepdims=True)
    i1 = jnp.min(jnp.where(el == v1, col, big), axis=-1, keepdims=True)
    el2 = jnp.where(col == i1, -jnp.inf, el)
    v2 = jnp.max(el2, axis=-1, keepdims=True)
    i2 = jnp.min(jnp.where(el2 == v2, col, big), axis=-1, keepdims=True)
    e21 = jnp.exp(v2 - v1)
    inv = 1.0 / (1.0 + e21)
    w1 = p_g * inv
    w2 = p_g * (e21 * inv)
    e1 = (i1 - N_GROUPS).astype(F32)
    e2 = (i2 - N_GROUPS).astype(F32)
    return jnp.where(col == 0, e1, jnp.where(col == 1, e2, jnp.where(col == 2, w1, jnp.where(col == 3, w2, 0.0))))


def _merge_norm_route(x, rec, att, g_l, g_a, w_lru_out_ref, w_attn_out_ref, w_o_ref, ln_g, ln_b,
                      w_rt_hi_ref, w_rt_lo_ref, b_rt):
    rec_o = jnp.dot(rec.astype(BF16), w_lru_out_ref[...], preferred_element_type=F32)
    att_o = jnp.dot(att.astype(BF16), w_attn_out_ref[...], preferred_element_type=F32)
    merged = _sigmoid(g_l) * rec_o + _sigmoid(g_a) * att_o
    mix = jnp.dot(merged.astype(BF16), w_o_ref[...], preferred_element_type=F32)
    x1 = _layer_norm(ALPHA * x + mix, ln_g, ln_b)
    return x1, _route(x1, w_rt_hi_ref, w_rt_lo_ref, b_rt)


def _mixer_kernel(sinks_ref, x_ref, w_in_ref, b_in_ref, w_conv_ref, b_conv_ref, w_gates_ref, b_rg_ref, b_ig_ref,
                  lam_ref, w_lru_out_ref, w_attn_out_ref, w_o_ref, ln_g_ref, ln_b_ref, w_rt_hi_ref, w_rt_lo_ref,
                  b_rt_ref, x1_s_ref, route_s_ref,
                  x1_ref, route_ref, kwin_ref, vwin_ref, conv_ref, h_ref,
                  conv_buf, h_carry, kcat, vcat, att_buf, *, tiles_per_seq, n_tiles):
    step = pl.program_id(0)

    @pl.when(step < n_tiles)
    def _():
        _mixer_tile(lax.rem(step, tiles_per_seq), sinks_ref, x_ref, w_in_ref, b_in_ref, w_conv_ref, b_conv_ref,
                    w_gates_ref, b_rg_ref, b_ig_ref, lam_ref, w_lru_out_ref, w_attn_out_ref, w_o_ref, ln_g_ref,
                    ln_b_ref, w_rt_hi_ref, w_rt_lo_ref, b_rt_ref, x1_ref, route_ref, kwin_ref, vwin_ref, conv_ref,
                    h_ref, conv_buf, h_carry, kcat, vcat, att_buf)

    @pl.when(step == n_tiles)
    def _():
        n_s = x1_s_ref.shape[0]
        x1_ref[0:n_s, :] = x1_s_ref[...]
        route_ref[0:n_s, :] = route_s_ref[...]


def _mixer_tile(t, sinks_ref, x_ref, w_in_ref, b_in_ref, w_conv_ref, b_conv_ref, w_gates_ref, b_rg_ref, b_ig_ref,
                lam_ref, w_lru_out_ref, w_attn_out_ref, w_o_ref, ln_g_ref, ln_b_ref, w_rt_hi_ref, w_rt_lo_ref,
                b_rt_ref, x1_ref, route_ref, kwin_ref, vwin_ref, conv_ref, h_ref,
                conv_buf, h_carry, kcat, vcat, att_buf):
    T = SEQ_TILE

    @pl.when(t == 0)
    def _():
        conv_buf[0:SUBLANES, :] = jnp.zeros((SUBLANES, LRU_WIDTH), F32)
        h_carry[...] = jnp.zeros_like(h_carry)
        kcat[0:WINDOW, :] = jnp.zeros((WINDOW, KV_WIDTH), BF16)
        vcat[0:WINDOW, :] = jnp.zeros((WINDOW, KV_WIDTH), BF16)

    x = x_ref[0]
    xb = x.astype(BF16)

    def proj(lo, width):
        return jnp.dot(xb, w_in_ref[:, lo:lo + width], preferred_element_type=F32) + b_in_ref[:, lo:lo + width]

    xl = proj(OFF_XL, LRU_WIDTH)
    conv_buf[SUBLANES:SUBLANES + T, :] = xl
    wc = w_conv_ref[...]
    xc = wc[0:1] * conv_buf[SUBLANES - 3:SUBLANES - 3 + T, :]
    xc = xc + wc[1:2] * conv_buf[SUBLANES - 2:SUBLANES - 2 + T, :]
    xc = xc + wc[2:3] * conv_buf[SUBLANES - 1:SUBLANES - 1 + T, :]
    xc = xc + wc[3:4] * xl + b_conv_ref[...]
    conv_ref[0] = conv_buf[T + SUBLANES - (CONV_W - 1):T + SUBLANES, :]
    conv_buf[0:SUBLANES, :] = conv_buf[T:T + SUBLANES, :]

    a, u = _lru_gates(xc, w_gates_ref, b_rg_ref[...], b_ig_ref[...], lam_ref[...])
    h = _linear_scan(a, u, h_carry[0:1, :])
    h_last = h[T - 1:T, :]
    h_carry[0:1, :] = h_last
    h_ref[0] = h_last
    rec = h * jax.nn.gelu(proj(OFF_YL, LRU_WIDTH))

    q = proj(OFF_Q, N_HEADS * HEAD_DIM) * (ATTN_SCALE * LOG2_E)
    k = proj(OFF_K, KV_WIDTH)
    v = proj(OFF_V, KV_WIDTH)
    kwin_ref[0] = k[T - WINDOW:, :]
    vwin_ref[0] = v[T - WINDOW:, :]
    kcat[WINDOW:WINDOW + T, :] = k.astype(BF16)
    vcat[WINDOW:WINDOW + T, :] = v.astype(BF16)

    qi = lax.broadcasted_iota(jnp.int32, (WINDOW, 2 * WINDOW), 0)
    kj = lax.broadcasted_iota(jnp.int32, (WINDOW, 2 * WINDOW), 1)
    band = (kj > qi) & (kj <= qi + WINDOW)
    grp_row = lax.broadcasted_iota(jnp.int32, (GROUP * WINDOW, 1), 0) // WINDOW
    for qb in range(T // WINDOW):
        if qb == 0:
            first_key = jnp.where(t == 0, WINDOW, 0)
            mask1 = band & (kj >= first_key)
        else:
            mask1 = band
        bias = jnp.concatenate([jnp.where(mask1, 0.0, NEG_INF)] * GROUP, axis=0)
        r0 = qb * WINDOW
        qq = q[r0:r0 + WINDOW, :]
        for j in range(N_KV):
            kjb = kcat[r0:r0 + 2 * WINDOW, j * HEAD_DIM:(j + 1) * HEAD_DIM]
            vjb = vcat[r0:r0 + 2 * WINDOW, j * HEAD_DIM:(j + 1) * HEAD_DIM]
            qs = jnp.concatenate(
                [qq[:, (j * GROUP + g) * HEAD_DIM:(j * GROUP + g + 1) * HEAD_DIM] for g in range(GROUP)], axis=0)
            s = lax.dot_general(qs.astype(BF16), kjb, (((1,), (1,)), ((), ())), preferred_element_type=F32) + bias
            sink = jnp.zeros((GROUP * WINDOW, 1), F32)
            for g in range(GROUP):
                sink = jnp.where(grp_row == g, sinks_ref[j * GROUP + g] * LOG2_E, sink)
            m = jnp.maximum(jnp.max(s, axis=-1, keepdims=True), sink)
            p = jnp.exp2(s - m)
            inv = 1.0 / (jnp.sum(p, axis=-1, keepdims=True) + jnp.exp2(sink - m))
            o = jnp.dot((p * inv).astype(BF16), vjb, preferred_element_type=F32)
            for g in range(GROUP):
                hcol = (j * GROUP + g) * HEAD_DIM
                att_buf[r0:r0 + WINDOW, hcol:hcol + HEAD_DIM] = o[g * WINDOW:(g + 1) * WINDOW, :]
    kcat[0:WINDOW, :] = kcat[T:T + WINDOW, :]
    vcat[0:WINDOW, :] = vcat[T:T + WINDOW, :]

    x1, route = _merge_norm_route(x, rec, att_buf[...], proj(OFF_GL, D_MODEL), proj(OFF_GA, D_MODEL),
                                  w_lru_out_ref, w_attn_out_ref, w_o_ref, ln_g_ref[...], ln_b_ref[...],
                                  w_rt_hi_ref, w_rt_lo_ref, b_rt_ref[...])
    x1_ref[...] = x1
    route_ref[...] = route


def _mixer_call(x_prompt, x1_s, route_s, sinks, wts):
    B, S, _ = x_prompt.shape
    T = SEQ_TILE
    nt = S // T
    n_tiles = B * nt
    n_rows_total = B * S + x1_s.shape[0]
    weight_args = (wts["w_in"], wts["b_in"], wts["w_conv"], wts["b_conv"], wts["w_gates"], wts["b_rg"], wts["b_ig"],
                   wts["lam"], wts["w_lru_out"], wts["w_attn_out"], wts["w_o"], wts["ln1_g"], wts["ln1_b"],
                   wts["w_rt_hi"], wts["w_rt_lo"], wts["b_rt"], x1_s, route_s)
    seq = lambda i: jnp.minimum(i, n_tiles - 1) // nt
    in_specs = [pl.BlockSpec(memory_space=pltpu.SMEM),
                pl.BlockSpec((1, T, D_MODEL), lambda i: (seq(i), lax.rem(jnp.minimum(i, n_tiles - 1), nt), 0))]
    in_specs += [_const_spec(w.shape) for w in weight_args]
    out_shape = (
        jax.ShapeDtypeStruct((n_rows_total, D_MODEL), F32),
        jax.ShapeDtypeStruct((n_rows_total, ROUTE_WIDTH), F32),
        jax.ShapeDtypeStruct((B, WINDOW, KV_WIDTH), F32),
        jax.ShapeDtypeStruct((B, WINDOW, KV_WIDTH), F32),
        jax.ShapeDtypeStruct((B, CONV_W - 1, LRU_WIDTH), F32),
        jax.ShapeDtypeStruct((B, 1, LRU_WIDTH), F32),
    )
    out_specs = (
        pl.BlockSpec((T, D_MODEL), lambda i: (i, 0)),
        pl.BlockSpec((T, ROUTE_WIDTH), lambda i: (i, 0)),
        pl.BlockSpec((1, WINDOW, KV_WIDTH), lambda i: (seq(i), 0, 0)),
        pl.BlockSpec((1, WINDOW, KV_WIDTH), lambda i: (seq(i), 0, 0)),
        pl.BlockSpec((1, CONV_W - 1, LRU_WIDTH), lambda i: (seq(i), 0, 0)),
        pl.BlockSpec((1, 1, LRU_WIDTH), lambda i: (seq(i), 0, 0)),
    )
    scratch = [
        pltpu.VMEM((T + 2 * SUBLANES, LRU_WIDTH), F32),
        pltpu.VMEM((SUBLANES, LRU_WIDTH), F32),
        pltpu.VMEM((T + WINDOW, KV_WIDTH), BF16),
        pltpu.VMEM((T + WINDOW, KV_WIDTH), BF16),
        pltpu.VMEM((T, N_HEADS * HEAD_DIM), F32),
    ]
    return pl.pallas_call(
        functools.partial(_mixer_kernel, tiles_per_seq=nt, n_tiles=n_tiles),
        grid=(n_tiles + 1,),
        in_specs=in_specs,
        out_specs=out_specs,
        out_shape=out_shape,
        scratch_shapes=scratch,
        compiler_params=pltpu.CompilerParams(dimension_semantics=("arbitrary",),
                                             vmem_limit_bytes=VMEM_LIMIT_BYTES),
        name="mixer_prompt",
    )(sinks, x_prompt, *weight_args)


def _sample_proj_kernel(x_ref, w_ref, b_ref, u_ref):
    u_ref[...] = jnp.dot(x_ref[...].astype(BF16), w_ref[...], preferred_element_type=F32) + b_ref[...]


def _sample_proj_call(x_s, w_in, b_in):
    n = x_s.shape[0]
    tn = SAMPLE_PROJ_TILE
    return pl.pallas_call(
        _sample_proj_kernel,
        grid=(IN_WIDTH // tn,),
        in_specs=[pl.BlockSpec((n, D_MODEL), lambda c: (0, 0)),
                  pl.BlockSpec((D_MODEL, tn), lambda c: (0, c)),
                  pl.BlockSpec((1, tn), lambda c: (0, c))],
        out_specs=pl.BlockSpec((n, tn), lambda c: (0, c)),
        out_shape=jax.ShapeDtypeStruct((n, IN_WIDTH), F32),
        compiler_params=pltpu.CompilerParams(dimension_semantics=("arbitrary",)),
        name="sample_proj",
    )(x_s, w_in, b_in)


def _sample_attn_kernel(q_ref, kn_ref, vn_ref, kn_row_ref, vn_row_ref, ck_ref, cv_ref, sinks_ref,
                        att_ref, kwin_ref, vwin_ref):
    tb = q_ref.shape[0]
    key_pos = lax.broadcasted_iota(jnp.int32, (tb, GROUP, WINDOW), 2)
    for j in range(N_KV):
        qj = q_ref[:, j * GROUP:(j + 1) * GROUP, :]
        kc = ck_ref[:, :, j * HEAD_DIM:(j + 1) * HEAD_DIM]
        vc = cv_ref[:, :, j * HEAD_DIM:(j + 1) * HEAD_DIM]
        s_c = jnp.einsum("bgd,bsd->bgs", qj.astype(BF16), kc.astype(BF16), preferred_element_type=F32) * ATTN_SCALE
        s_c = jnp.where(key_pos >= 1, s_c, NEG_INF)
        kn = kn_ref[:, j:j + 1, :]
        vn = vn_ref[:, j:j + 1, :]
        s_n = jnp.sum(qj * kn, axis=-1, keepdims=True) * ATTN_SCALE
        sink = sinks_ref[j][None]
        m = jnp.maximum(jnp.maximum(jnp.max(s_c, axis=-1, keepdims=True), s_n), sink)
        p_c = jnp.exp(s_c - m)
        p_n = jnp.exp(s_n - m)
        inv = 1.0 / (jnp.sum(p_c, axis=-1, keepdims=True) + p_n + jnp.exp(sink - m))
        o = jnp.einsum("bgs,bsd->bgd", (p_c * inv).astype(BF16), vc.astype(BF16), preferred_element_type=F32)
        att_ref[:, j * GROUP:(j + 1) * GROUP, :] = o + (p_n * inv) * vn
    kwin_ref[:, 0:WINDOW - 1, :] = ck_ref[:, 1:WINDOW, :]
    kwin_ref[:, WINDOW - 1:WINDOW, :] = kn_row_ref[...]
    vwin_ref[:, 0:WINDOW - 1, :] = cv_ref[:, 1:WINDOW, :]
    vwin_ref[:, WINDOW - 1:WINDOW, :] = vn_row_ref[...]


def _sample_attn_call(q3, kn3, vn3, kn_row, vn_row, ck, cv, sinks3):
    n = q3.shape[0]
    tb = SAMPLE_ATTN_TILE
    b3 = lambda i: (i, 0, 0)
    return pl.pallas_call(
        _sample_attn_kernel,
        grid=(n // tb,),
        in_specs=[pl.BlockSpec((tb, N_HEADS, HEAD_DIM), b3),
                  pl.BlockSpec((tb, N_KV, HEAD_DIM), b3),
                  pl.BlockSpec((tb, N_KV, HEAD_DIM), b3),
                  pl.BlockSpec((tb, 1, KV_WIDTH), b3),
                  pl.BlockSpec((tb, 1, KV_WIDTH), b3),
                  pl.BlockSpec((tb, WINDOW, KV_WIDTH), b3),
                  pl.BlockSpec((tb, WINDOW, KV_WIDTH), b3),
                  pl.BlockSpec((N_KV, GROUP, 1), lambda i: (0, 0, 0))],
        out_specs=(pl.BlockSpec((tb, N_HEADS, HEAD_DIM), b3),
                   pl.BlockSpec((tb, WINDOW, KV_WIDTH), b3),
                   pl.BlockSpec((tb, WINDOW, KV_WIDTH), b3)),
        out_shape=(jax.ShapeDtypeStruct((n, N_HEADS, HEAD_DIM), F32),
                   jax.ShapeDtypeStruct((n, WINDOW, KV_WIDTH), F32),
                   jax.ShapeDtypeStruct((n, WINDOW, KV_WIDTH), F32)),
        compiler_params=pltpu.CompilerParams(dimension_semantics=("arbitrary",)),
        name="sample_attn",
    )(q3, kn3, vn3, kn_row, vn_row, ck, cv, sinks3)


def _sample_mix_kernel(x_ref, u_ref, att_ref, st_ref, h0_ref, w_conv_ref, b_conv_ref, w_gates_ref, b_rg_ref,
                       b_ig_ref, lam_ref, w_lru_out_ref, w_attn_out_ref, w_o_ref, ln_g_ref, ln_b_ref,
                       w_rt_hi_ref, w_rt_lo_ref, b_rt_ref,
                       x1_ref, route_ref, conv_ref, h_ref):
    xl = u_ref[:, OFF_XL:OFF_XL + LRU_WIDTH]
    wc = w_conv_ref[...]
    xc = wc[0:1] * st_ref[0]
    xc = xc + wc[1:2] * st_ref[1]
    xc = xc + wc[2:3] * st_ref[2]
    xc = xc + wc[3:4] * xl + b_conv_ref[...]
    conv_ref[0] = st_ref[1]
    conv_ref[1] = st_ref[2]
    conv_ref[2] = xl
    a, u = _lru_gates(xc, w_gates_ref, b_rg_ref[...], b_ig_ref[...], lam_ref[...])
    h = a * h0_ref[...] + u
    h_ref[...] = h
    rec = h * jax.nn.gelu(u_ref[:, OFF_YL:OFF_YL + LRU_WIDTH])
    x1, route = _merge_norm_route(x_ref[...], rec, att_ref[...], u_ref[:, OFF_GL:OFF_GL + D_MODEL],
                                  u_ref[:, OFF_GA:OFF_GA + D_MODEL], w_lru_out_ref, w_attn_out_ref, w_o_ref,
                                  ln_g_ref[...], ln_b_ref[...], w_rt_hi_ref, w_rt_lo_ref, b_rt_ref[...])
    x1_ref[...] = x1
    route_ref[...] = route


def _sample_mix_call(x_s, u_s, att, st_t, h0, wts):
    n = x_s.shape[0]
    weight_args = (wts["w_conv"], wts["b_conv"], wts["w_gates"], wts["b_rg"], wts["b_ig"], wts["lam"],
                   wts["w_lru_out"], wts["w_attn_out"], wts["w_o"], wts["ln1_g"], wts["ln1_b"],
                   wts["w_rt_hi"], wts["w_rt_lo"], wts["b_rt"])
    args = (x_s, u_s, att, st_t, h0) + weight_args
    out_shapes = ((n, D_MODEL), (n, ROUTE_WIDTH), (CONV_W - 1, n, LRU_WIDTH), (n, LRU_WIDTH))
    return pl.pallas_call(
        _sample_mix_kernel,
        grid=(1,),
        in_specs=[_const_spec(a.shape) for a in args],
        out_specs=tuple(_const_spec(s) for s in out_shapes),
        out_shape=tuple(jax.ShapeDtypeStruct(s, F32) for s in out_shapes),
        compiler_params=pltpu.CompilerParams(dimension_semantics=("arbitrary",),
                                             vmem_limit_bytes=VMEM_LIMIT_BYTES),
        name="sample_mix",
    )(*args)


def _one_hot(mask):
    return jnp.where(mask, 1.0, 0.0).astype(BF16)


def _pack_rows(x):
    half = x.shape[1] // 2
    lo = lax.shift_right_logical(lax.bitcast_convert_type(x[:, :half], jnp.uint32), jnp.uint32(16))
    hi = lax.bitcast_convert_type(x[:, half:], jnp.uint32) & jnp.uint32(0xFFFF0000)
    return lo | hi


def _unpack_rows(words):
    lo = lax.bitcast_convert_type(lax.shift_left(words, jnp.uint32(16)), F32)
    hi = lax.bitcast_convert_type(words & jnp.uint32(0xFFFF0000), F32)
    return jnp.concatenate([lo.astype(BF16), hi.astype(BF16)], axis=1)


def _dispatch_kernel(x1_ref, route_ref, xs_hbm, goff_ref, gtot_ref,
                     stage, zero_rows, tri, g_vmem, gvec, g_smem, nd_smem, sem_stage, sem_g, sem_zero,
                     *, n_tokens, seg_cap):
    i = pl.program_id(0)
    last = pl.num_programs(0) - 1
    td = DISPATCH_TILE
    n_assign = TOP_K * td

    def g_copy():
        return pltpu.make_async_copy(g_vmem.at[pl.ds(0, 1)], g_smem, sem_g)

    @pl.when(i == 0)
    def _():
        r = lax.broadcasted_iota(jnp.int32, (n_assign, n_assign), 0)
        c = lax.broadcasted_iota(jnp.int32, (n_assign, n_assign), 1)
        tri[...] = _one_hot(r < c)
        gvec[...] = jnp.zeros_like(gvec)
        g_vmem[...] = jnp.zeros_like(g_vmem)
        nd_smem[0] = 0
        g_copy().start()

    n_valid = n_tokens - i * td
    route = route_ref[...]
    valid_col = lax.broadcasted_iota(jnp.int32, (td, 1), 0) < n_valid
    lane = lax.broadcasted_iota(jnp.int32, (td, ROUTE_WIDTH), 1).astype(F32)
    e0c = jnp.where(valid_col, route[:, 0:1], -1.0)
    e1c = jnp.where(valid_col, route[:, 1:2], -1.0)
    cnt_row = jnp.sum(jnp.where(lane == e0c, 1.0, 0.0) + jnp.where(lane == e1c, 1.0, 0.0), axis=0, keepdims=True)
    rounds = (jnp.max(cnt_row).astype(jnp.int32) + SLOT_CAP - 1) // SLOT_CAP
    goff_ref[0] = gvec[0:1, :].astype(jnp.int32)
    gvec[0:1, :] = gvec[0:1, :] + jnp.ceil(cnt_row * (1.0 / SUBLANES)) * SUBLANES
    gtot_ref[...] = gvec[0:1, :].astype(jnp.int32)

    route_t = route.T
    valid_row = lax.broadcasted_iota(jnp.int32, (1, td), 1) < n_valid
    e_row = jnp.concatenate([jnp.where(valid_row, route_t[0:1, :], -1.0),
                             jnp.where(valid_row, route_t[1:2, :], -1.0)], axis=1)
    expert_sub = lax.broadcasted_iota(jnp.int32, (N_EXPERTS, n_assign), 0).astype(F32)
    onehot_t = expert_sub == e_row
    rank_t = jnp.dot(_one_hot(onehot_t), tri[...], preferred_element_type=F32)
    rank_row = jnp.sum(jnp.where(onehot_t, rank_t, 0.0), axis=0, keepdims=True)
    xb = jnp.where(valid_col, x1_ref[...], 0.0).astype(BF16)
    slot = lax.broadcasted_iota(jnp.int32, (N_SLOTS, td), 0).astype(F32)

    def stage_copy(e, dst_row):
        src = stage.at[pl.ds(pl.multiple_of(e * SLOT_CAP, SLOT_CAP), SLOT_CAP)]
        return pltpu.make_async_copy(src, xs_hbm.at[pl.ds(dst_row, SLOT_CAP)], sem_stage)

    def wait_outstanding():
        def wait_one(_, carry):
            stage_copy(0, 0).wait()
            return carry
        lax.fori_loop(0, nd_smem[0], wait_one, 0)
        nd_smem[0] = 0

    g_copy().wait()

    def round_body(r, carry):
        lo = r * SLOT_CAP
        lo_f = lo.astype(F32)
        in_round = (rank_row >= lo_f) & (rank_row < lo_f + SLOT_CAP) & (e_row >= 0.0)
        key = jnp.where(in_round, e_row * SLOT_CAP + (rank_row - lo_f), -1.0)
        perm = jnp.where(slot == key[:, :td], 1.0, jnp.where(slot == key[:, td:], 1.0, 0.0)).astype(BF16)
        sorted_rows = jnp.dot(perm, xb, preferred_element_type=F32)
        wait_outstanding()
        stage[...] = _pack_rows(sorted_rows)

        def issue(e, carry):
            stage_copy(e, pl.multiple_of(e * seg_cap + g_smem[0, e] + lo, SUBLANES)).start()
            return carry
        lax.fori_loop(0, N_EXPERTS, issue, 0)
        nd_smem[0] = N_EXPERTS
        return carry
    lax.fori_loop(0, rounds, round_body, 0)

    g_vmem[0:1, :] = gvec[0:1, :].astype(jnp.int32)
    g_copy().start()

    @pl.when(i == last)
    def _():
        g_copy().wait()
        wait_outstanding()
        zero_rows[...] = jnp.zeros_like(zero_rows)

        def pad_copy(e):
            dst_row = pl.multiple_of(e * seg_cap + g_smem[0, e], SUBLANES)
            return pltpu.make_async_copy(zero_rows, xs_hbm.at[pl.ds(dst_row, SEG_PAD)], sem_zero)

        def start(e, carry):
            pad_copy(e).start()
            return carry
        lax.fori_loop(0, N_EXPERTS, start, 0)

        def wait(e, carry):
            pad_copy(e).wait()
            return carry
        lax.fori_loop(0, N_EXPERTS, wait, 0)


def _dispatch_tiles(n_tokens):
    return (n_tokens + DISPATCH_TILE - 1) // DISPATCH_TILE


def _segment_capacity(n_tokens):
    alignment_slack = (SUBLANES - 1) * _dispatch_tiles(n_tokens)
    return (n_tokens + alignment_slack + SEG_PAD + MOE_TILE - 1) // MOE_TILE * MOE_TILE


def _dispatch_call(x1_all, route_all):
    n_tokens = x1_all.shape[0]
    td = DISPATCH_TILE
    n_tiles = (n_tokens + td - 1) // td
    seg_cap = _segment_capacity(n_tokens)
    n_assign = TOP_K * td
    return pl.pallas_call(
        functools.partial(_dispatch_kernel, n_tokens=n_tokens, seg_cap=seg_cap),
        grid=(n_tiles,),
        in_specs=[pl.BlockSpec((td, D_MODEL), lambda i: (i, 0)),
                  pl.BlockSpec((td, ROUTE_WIDTH), lambda i: (i, 0))],
        out_specs=(pl.BlockSpec(memory_space=pl.ANY),
                   pl.BlockSpec((1, 1, ROUTE_WIDTH), lambda i: (i, 0, 0)),
                   pl.BlockSpec((1, ROUTE_WIDTH), lambda i: (0, 0))),
        out_shape=(jax.ShapeDtypeStruct((N_EXPERTS * seg_cap, PACKED_WIDTH), jnp.uint32),
                   jax.ShapeDtypeStruct((n_tiles, 1, ROUTE_WIDTH), jnp.int32),
                   jax.ShapeDtypeStruct((1, ROUTE_WIDTH), jnp.int32)),
        scratch_shapes=[
            pltpu.VMEM((N_SLOTS, PACKED_WIDTH), jnp.uint32),
            pltpu.VMEM((SEG_PAD, PACKED_WIDTH), jnp.uint32),
            pltpu.VMEM((n_assign, n_assign), BF16),
            pltpu.VMEM((SUBLANES, ROUTE_WIDTH), jnp.int32),
            pltpu.VMEM((SUBLANES, ROUTE_WIDTH), F32),
            pltpu.SMEM((1, ROUTE_WIDTH), jnp.int32),
            pltpu.SMEM((1,), jnp.int32),
            pltpu.SemaphoreType.DMA(()),
            pltpu.SemaphoreType.DMA(()),
            pltpu.SemaphoreType.DMA(()),
        ],
        compiler_params=pltpu.CompilerParams(dimension_semantics=("arbitrary",),
                                             vmem_limit_bytes=VMEM_LIMIT_BYTES),
        name="moe_dispatch",
    )(x1_all, route_all)


def _moe_kernel(blk_e_ref, blk_j_ref, n_used_ref, xs_ref, wg_ref, wu_ref, wd_ref, ys_ref, wg_bf, wu_bf, wd_bf):
    del blk_j_ref
    b = pl.program_id(0)

    @pl.when(b < n_used_ref[0])
    def _():
        prev_e = blk_e_ref[jnp.maximum(b - 1, 0)]

        @pl.when((b == 0) | (blk_e_ref[b] != prev_e))
        def _():
            wg_bf[...] = wg_ref[0].astype(BF16)
            wu_bf[...] = wu_ref[0].astype(BF16)
            wd_bf[...] = wd_ref[0].astype(BF16)

        xb = _unpack_rows(xs_ref[...])
        g = jnp.dot(xb, wg_bf[...], preferred_element_type=F32)
        u = jnp.dot(xb, wu_bf[...], preferred_element_type=F32)
        hmid = (g * _sigmoid(g)) * u
        y = jnp.dot(hmid.astype(BF16), wd_bf[...], preferred_element_type=F32)
        ys_ref[...] = _pack_rows(y.astype(BF16).astype(F32))


def _expert_blocks(gtot, n_blocks):
    rows = gtot[0, :N_EXPERTS]
    nb = (rows + SLOT_CAP + MOE_TILE - 1) // MOE_TILE
    ends = jnp.cumsum(nb)
    n_used = ends[-1]
    step = jnp.minimum(jnp.arange(n_blocks, dtype=jnp.int32), n_used - 1)
    blk_e = jnp.sum((step[:, None] >= ends[None, :]).astype(jnp.int32), axis=1)
    blk_j = step - (ends - nb)[blk_e]
    return blk_e, blk_j.astype(jnp.int32), n_used.reshape(1).astype(jnp.int32)


def _moe_call(gtot, xs, w_gate, w_up, w_down, n_tokens):
    seg_cap = _segment_capacity(n_tokens)
    seg_blocks = seg_cap // MOE_TILE
    max_rows = n_tokens * TOP_K + N_EXPERTS * (SUBLANES - 1) * _dispatch_tiles(n_tokens)
    n_blocks = (max_rows + N_EXPERTS * (SLOT_CAP + MOE_TILE - 1)) // MOE_TILE
    blk_e, blk_j, n_used = _expert_blocks(gtot, n_blocks)
    row_block = lambda b, be, bj, nu: (be[b] * seg_blocks + bj[b], 0)
    weight_block = lambda b, be, bj, nu: (be[b], 0, 0)
    grid_spec = pltpu.PrefetchScalarGridSpec(
        num_scalar_prefetch=3,
        grid=(n_blocks,),
        in_specs=[
            pl.BlockSpec((MOE_TILE, PACKED_WIDTH), row_block),
            pl.BlockSpec((1, D_MODEL, EXPERT_FF), weight_block),
            pl.BlockSpec((1, D_MODEL, EXPERT_FF), weight_block),
            pl.BlockSpec((1, EXPERT_FF, D_MODEL), weight_block),
        ],
        out_specs=pl.BlockSpec((MOE_TILE, PACKED_WIDTH), row_block),
        scratch_shapes=[
            pltpu.VMEM((D_MODEL, EXPERT_FF), BF16),
            pltpu.VMEM((D_MODEL, EXPERT_FF), BF16),
            pltpu.VMEM((EXPERT_FF, D_MODEL), BF16),
        ],
    )
    return pl.pallas_call(
        _moe_kernel,
        grid_spec=grid_spec,
        out_shape=jax.ShapeDtypeStruct(xs.shape, xs.dtype),
        compiler_params=pltpu.CompilerParams(dimension_semantics=("arbitrary",),
                                             vmem_limit_bytes=VMEM_LIMIT_BYTES),
        name="moe_experts",
    )(blk_e, blk_j, n_used, xs, w_gate, w_up, w_down)


def _combine_kernel(gcur_ref, gnext_ref, x1_ref, route_ref, ys_hbm, ln_g_ref, ln_b_ref, y_ref,
                    stage, acc, tri, sem, *, seg_cap):
    td = x1_ref.shape[0]
    n_assign = TOP_K * td
    i = pl.program_id(0)
    buf = lax.rem(i, 2)

    def stage_copy(g_ref, e, first_rank, to_buf):
        src_row = pl.multiple_of(e * seg_cap + g_ref[0, 0, e] + first_rank, SUBLANES)
        dst = stage.at[to_buf, pl.ds(pl.multiple_of(e * SLOT_CAP, SLOT_CAP), SLOT_CAP)]
        return pltpu.make_async_copy(ys_hbm.at[pl.ds(src_row, SLOT_CAP)], dst, sem.at[to_buf])

    def fetch_first_round(g_ref, to_buf):
        def start(e, carry):
            stage_copy(g_ref, e, 0, to_buf).start()
            return carry
        lax.fori_loop(0, N_EXPERTS, start, 0)

    @pl.when(i == 0)
    def _():
        r = lax.broadcasted_iota(jnp.int32, (n_assign, n_assign), 0)
        c = lax.broadcasted_iota(jnp.int32, (n_assign, n_assign), 1)
        tri[...] = _one_hot(c < r)
        fetch_first_round(gcur_ref, 0)

    def count(e):
        return gnext_ref[0, 0, e] - gcur_ref[0, 0, e]

    n_max = lax.fori_loop(0, N_EXPERTS, lambda e, m: jnp.maximum(m, count(e)), 0)
    rounds = (n_max + SLOT_CAP - 1) // SLOT_CAP

    route = route_ref[...]
    e0, e1, w0, w1 = route[:, 0:1], route[:, 1:2], route[:, 2:3], route[:, 3:4]
    lane = lax.broadcasted_iota(jnp.int32, (td, ROUTE_WIDTH), 1).astype(F32)
    onehot = jnp.concatenate([lane == e0, lane == e1], axis=0)
    onehot_f = jnp.where(onehot, 1.0, 0.0)
    rank_mat = jnp.dot(tri[...], onehot_f.astype(BF16), preferred_element_type=F32)
    rank = jnp.sum(jnp.where(onehot, rank_mat, 0.0), axis=1, keepdims=True)
    r0, r1 = rank[:td], rank[td:]
    cnt_row = jnp.sum(onehot_f, axis=0, keepdims=True)

    slot_col = lax.broadcasted_iota(jnp.int32, (N_SLOTS, 1), 0)
    slot_expert = slot_col // SLOT_CAP
    slot_rank = slot_col - slot_expert * SLOT_CAP
    lane_s = lax.broadcasted_iota(jnp.int32, (N_SLOTS, ROUTE_WIDTH), 1)
    n_col = jnp.sum(jnp.where(lane_s == slot_expert, cnt_row, 0.0), axis=1, keepdims=True)
    slot_lane = lax.broadcasted_iota(jnp.int32, (td, N_SLOTS), 1).astype(F32)

    def wait_copies(n):
        def wait_one(_, c):
            stage_copy(gcur_ref, 0, 0, buf).wait()
            return c
        lax.fori_loop(0, n, wait_one, 0)

    def weighted_rows(lo):
        lo_f = lo.astype(F32)
        live = (slot_rank + lo).astype(F32) < n_col
        rows = _unpack_rows(jnp.where(live, stage[buf], jnp.uint32(0)))

        def gathered(e_col, r_col):
            in_round = (r_col >= lo_f) & (r_col < lo_f + SLOT_CAP)
            key = jnp.where(in_round, e_col * SLOT_CAP + (r_col - lo_f), -1.0)
            return jnp.dot(_one_hot(slot_lane == key), rows, preferred_element_type=F32)
        return w0 * gathered(e0, r0) + w1 * gathered(e1, r1)

    wait_copies(N_EXPERTS)

    @pl.when(i + 1 < pl.num_programs(0))
    def _():
        fetch_first_round(gnext_ref, 1 - buf)

    acc[...] = weighted_rows(jnp.int32(0))

    def later_round(r, carry):
        lo = r * SLOT_CAP

        def start(e, n_started):
            has_rows = count(e) > lo

            @pl.when(has_rows)
            def _():
                stage_copy(gcur_ref, e, lo, buf).start()
            return n_started + has_rows.astype(jnp.int32)
        wait_copies(lax.fori_loop(0, N_EXPERTS, start, 0))
        acc[...] = acc[...] + weighted_rows(lo)
        return carry
    lax.fori_loop(1, rounds, later_round, 0)

    y_ref[...] = _layer_norm(ALPHA * x1_ref[...] + acc[...], ln_g_ref[...], ln_b_ref[...])


def _combine_call(goff, x1_all, route_all, ys, ln_g, ln_b, first_row, n_rows, tile, n_tokens):
    assert first_row % DISPATCH_TILE == 0 and first_row % tile == 0
    assert tile == DISPATCH_TILE or n_rows == tile
    first_block = first_row // tile
    first_goff = first_row // DISPATCH_TILE
    n_assign = TOP_K * tile
    grid_spec = pl.GridSpec(
        grid=(n_rows // tile,),
        in_specs=[
            pl.BlockSpec((1, 1, ROUTE_WIDTH), lambda i: (first_goff + i, 0, 0), memory_space=pltpu.SMEM),
            pl.BlockSpec((1, 1, ROUTE_WIDTH), lambda i: (first_goff + i + 1, 0, 0), memory_space=pltpu.SMEM),
            pl.BlockSpec((tile, D_MODEL), lambda i: (first_block + i, 0)),
            pl.BlockSpec((tile, ROUTE_WIDTH), lambda i: (first_block + i, 0)),
            pl.BlockSpec(memory_space=pl.ANY),
            _const_spec(ln_g.shape),
            _const_spec(ln_b.shape),
        ],
        out_specs=pl.BlockSpec((tile, D_MODEL), lambda i: (i, 0)),
        scratch_shapes=[pltpu.VMEM((2, N_SLOTS, PACKED_WIDTH), jnp.uint32),
                        pltpu.VMEM((tile, D_MODEL), F32),
                        pltpu.VMEM((n_assign, n_assign), BF16),
                        pltpu.SemaphoreType.DMA((2,))],
    )
    return pl.pallas_call(
        functools.partial(_combine_kernel, seg_cap=_segment_capacity(n_tokens)),
        grid_spec=grid_spec,
        out_shape=jax.ShapeDtypeStruct((n_rows, D_MODEL), F32),
        compiler_params=pltpu.CompilerParams(dimension_semantics=("arbitrary",),
                                             vmem_limit_bytes=VMEM_LIMIT_BYTES),
        name="moe_combine",
    )(goff, goff, x1_all, route_all, ys, ln_g, ln_b)


def _prepare_weights(w_in, b_in, w_conv, b_conv, w_rg, b_rg, w_ig, b_ig, lru_lambda, w_lru_out, w_attn_out, w_o,
                     ln1_g, ln1_b, w_group, b_group, w_router, b_router):
    blocks_per_chunk = GATE_CHUNK // LRU_BLOCK

    def chunked_block_diag(w):
        w = w.reshape(N_GATE_CHUNKS, blocks_per_chunk, LRU_BLOCK, LRU_BLOCK)
        eye = jnp.eye(blocks_per_chunk, dtype=w.dtype)
        return jnp.einsum("cbij,bd->cbidj", w, eye).reshape(N_GATE_CHUNKS, GATE_CHUNK, GATE_CHUNK)

    w_gates = jnp.concatenate([chunked_block_diag(w_rg), chunked_block_diag(w_ig)], axis=-1).astype(BF16)
    w_rt = jnp.concatenate([w_group, w_router], axis=1)
    w_rt = jnp.pad(w_rt, ((0, 0), (0, ROUTE_WIDTH - w_rt.shape[1])))
    w_rt_hi = w_rt.astype(BF16)
    w_rt_lo = (w_rt - w_rt_hi.astype(F32)).astype(BF16)
    b_rt = jnp.pad(jnp.concatenate([b_group, b_router]), (0, ROUTE_WIDTH - N_GROUPS - N_EXPERTS))
    row = lambda v: v.reshape(1, -1)
    return dict(
        w_in=w_in.astype(BF16), b_in=row(b_in), w_conv=w_conv, b_conv=row(b_conv), w_gates=w_gates,
        b_rg=row(b_rg), b_ig=row(b_ig), lam=row(lru_lambda),
        w_lru_out=w_lru_out.astype(BF16), w_attn_out=w_attn_out.astype(BF16), w_o=w_o.astype(BF16),
        ln1_g=row(ln1_g), ln1_b=row(ln1_b), w_rt_hi=w_rt_hi, w_rt_lo=w_rt_lo, b_rt=row(b_rt))


def kernel(x_prompt, x_sample, cache_k, cache_v, state_conv, state_lru_h, w_in, b_in, w_conv, b_conv, w_rg, b_rg,
           w_ig, b_ig, lru_lambda, sinks, w_lru_out, w_attn_out, w_o, ln1_g, ln1_b, w_group, b_group, w_router,
           b_router, w_gate, w_up, w_down, ln2_g, ln2_b):
    B, S, _ = x_prompt.shape
    n_prompt = B * S
    n_sample = x_sample.shape[0]
    n_all = n_prompt + n_sample
    wts = _prepare_weights(w_in, b_in, w_conv, b_conv, w_rg, b_rg, w_ig, b_ig, lru_lambda, w_lru_out, w_attn_out,
                           w_o, ln1_g, ln1_b, w_group, b_group, w_router, b_router)

    x_s = x_sample.reshape(n_sample, D_MODEL)
    u_s = _sample_proj_call(x_s, wts["w_in"], wts["b_in"])
    q3 = u_s[:, OFF_Q:OFF_K].reshape(n_sample, N_HEADS, HEAD_DIM)
    k_new = u_s[:, OFF_K:OFF_V]
    v_new = u_s[:, OFF_V:OFF_GL]
    att3, k_win_s, v_win_s = _sample_attn_call(
        q3, k_new.reshape(n_sample, N_KV, HEAD_DIM), v_new.reshape(n_sample, N_KV, HEAD_DIM),
        k_new.reshape(n_sample, 1, KV_WIDTH), v_new.reshape(n_sample, 1, KV_WIDTH),
        cache_k.reshape(n_sample, WINDOW, KV_WIDTH), cache_v.reshape(n_sample, WINDOW, KV_WIDTH),
        sinks.reshape(N_KV, GROUP, 1))
    x1_s, route_s, conv_s_t, h_s = _sample_mix_call(
        x_s, u_s, att3.reshape(n_sample, N_HEADS * HEAD_DIM), jnp.transpose(state_conv, (1, 0, 2)), state_lru_h, wts)

    x1_all, route_all, k_win_p, v_win_p, conv_p, h_p = _mixer_call(x_prompt, x1_s, route_s, sinks, wts)

    xs, goff, gtot = _dispatch_call(x1_all, route_all)
    ys = _moe_call(gtot, xs, w_gate, w_up, w_down, n_all)
    goff = jnp.concatenate([goff, gtot[None]], axis=0)
    ln2_g2, ln2_b2 = ln2_g.reshape(1, -1), ln2_b.reshape(1, -1)
    y_p = _combine_call(goff, x1_all, route_all, ys, ln2_g2, ln2_b2, 0, n_prompt, DISPATCH_TILE, n_all)
    y_s = _combine_call(goff, x1_all, route_all, ys, ln2_g2, ln2_b2, n_prompt, n_sample, n_sample, n_all)

    kv_shape = (WINDOW, N_KV, HEAD_DIM)
    return (y_p.reshape(B, S, D_MODEL), y_s.reshape(n_sample, 1, D_MODEL),
            k_win_p.reshape((B,) + kv_shape), v_win_p.reshape((B,) + kv_shape), conv_p, h_p.reshape(B, LRU_WIDTH),
            k_win_s.reshape((n_sample,) + kv_shape), v_win_s.reshape((n_sample,) + kv_shape),
            jnp.transpose(conv_s_t, (1, 0, 2)), h_s)
```

```python
import functools

import jax
import jax.numpy as jnp
from jax import lax
from jax.experimental import pallas as pl
from jax.experimental.pallas import tpu as pltpu

F32 = jnp.float32
BF16 = jnp.bfloat16

D_MODEL = 1024
LRU_WIDTH = 1024
LRU_BLOCK = 64
CONV_W = 4
LRU_C = 8.0
N_HEADS = 16
N_KV = 4
GROUP = N_HEADS // N_KV
HEAD_DIM = 64
KV_WIDTH = N_KV * HEAD_DIM
WINDOW = 128
NEG_INF = -1e30
N_GROUPS = 4
EXPERTS_PER_GROUP = 8
N_EXPERTS = N_GROUPS * EXPERTS_PER_GROUP
TOP_K = 2
EXPERT_FF = D_MODEL // 2
DEPTH = 1
ALPHA = (2 * DEPTH) ** 0.25
LN_EPS = 1e-5
ATTN_SCALE = HEAD_DIM ** -0.5
LOG2_E = 1.4426950408889634

OFF_XL = 0
OFF_YL = OFF_XL + LRU_WIDTH
OFF_Q = OFF_YL + LRU_WIDTH
OFF_K = OFF_Q + N_HEADS * HEAD_DIM
OFF_V = OFF_K + KV_WIDTH
OFF_GL = OFF_V + KV_WIDTH
OFF_GA = OFF_GL + D_MODEL
IN_WIDTH = OFF_GA + D_MODEL

LANES = 128
SUBLANES = 8
MXU_DIM = 256
VMEM_LIMIT_BYTES = 56 * 1024 * 1024

GATE_CHUNK = MXU_DIM
N_GATE_CHUNKS = LRU_WIDTH // GATE_CHUNK
ROUTE_WIDTH = LANES

SEQ_TILE = 256
MOE_TILE = 512
DISPATCH_TILE = 256
SLOT_CAP = 32
N_SLOTS = N_EXPERTS * SLOT_CAP
SEG_PAD = MOE_TILE + SLOT_CAP
PACKED_WIDTH = D_MODEL // 2
SAMPLE_ATTN_TILE = 16
SAMPLE_PROJ_TILE = 512


def _const_spec(shape):
    nd = len(shape)
    return pl.BlockSpec(shape, lambda *_: (0,) * nd)


def _layer_norm(z, g, b):
    mu = jnp.mean(z, axis=-1, keepdims=True)
    zc = z - mu
    var = jnp.mean(zc * zc, axis=-1, keepdims=True)
    return zc * lax.rsqrt(var + LN_EPS) * g + b


def _sigmoid(x):
    return 1.0 / (1.0 + jnp.exp2(x * -LOG2_E))


def _softplus(x):
    return jnp.maximum(x, 0.0) + jnp.log1p(jnp.exp(-jnp.abs(x)))


def _bdot(a, b):
    return jnp.dot(a.astype(BF16), b.astype(BF16), preferred_element_type=F32)


def _lru_gates(xc, w_gates_ref, b_rg, b_ig, lam):
    xcb = xc.astype(BF16)
    r_parts, i_parts = [], []
    for c in range(N_GATE_CHUNKS):
        g = jnp.dot(xcb[:, c * GATE_CHUNK:(c + 1) * GATE_CHUNK], w_gates_ref[c], preferred_element_type=F32)
        r_parts.append(g[:, :GATE_CHUNK])
        i_parts.append(g[:, GATE_CHUNK:])
    r = _sigmoid(jnp.concatenate(r_parts, axis=1) + b_rg)
    i = _sigmoid(jnp.concatenate(i_parts, axis=1) + b_ig)
    log_a = (-LRU_C * r) * _softplus(-lam)
    a = jnp.exp(log_a)
    u = jnp.sqrt(1.0 - a * a) * (i * xc)
    return a, u


def _linear_scan(a, u, h_in):
    n, w = a.shape
    groups = n // SUBLANES
    a3 = a.reshape(groups, SUBLANES, w)
    u3 = u.reshape(groups, SUBLANES, w)
    row = lax.broadcasted_iota(jnp.int32, a3.shape, 1)
    d = 1
    while d < SUBLANES:
        has_prev = row >= d
        u3 = u3 + a3 * jnp.where(has_prev, pltpu.roll(u3, d, axis=1), 0.0)
        a3 = a3 * jnp.where(has_prev, pltpu.roll(a3, d, axis=1), 1.0)
        d *= 2
    carry = h_in
    out = []
    for g in range(groups):
        h_g = u3[g] + a3[g] * carry
        out.append(h_g)
        carry = h_g[SUBLANES - 1:SUBLANES, :]
    return jnp.concatenate(out, axis=0)


def _route(x1, w_hi_ref, w_lo_ref, b_rt):
    x_hi = x1.astype(BF16)
    x_lo = (x1 - x_hi.astype(F32)).astype(BF16)
    w_hi = w_hi_ref[...]
    logits = (jnp.dot(x_hi, w_hi, preferred_element_type=F32)
              + (jnp.dot(x_lo, w_hi, preferred_element_type=F32)
                 + jnp.dot(x_hi, w_lo_ref[...], preferred_element_type=F32))) + b_rt
    col = lax.broadcasted_iota(jnp.int32, logits.shape, 1)
    big = jnp.int32(ROUTE_WIDTH)
    is_g = col < N_GROUPS
    gl = jnp.where(is_g, logits, -jnp.inf)
    gmax = jnp.max(gl, axis=-1, keepdims=True)
    g_idx = jnp.min(jnp.where(gl == gmax, col, big), axis=-1, keepdims=True)
    p_g = 1.0 / jnp.sum(jnp.where(is_g, jnp.exp(gl - gmax), 0.0), axis=-1, keepdims=True)
    lo = N_GROUPS + g_idx * EXPERTS_PER_GROUP
    in_grp = (col >= lo) & (col < lo + EXPERTS_PER_GROUP)
    el = jnp.where(in_grp, logits, -jnp.inf)
    v1 = jnp.max(el, axis=-1, keepdims=True)
    i1 = jnp.min(jnp.where(el == v1, col, big), axis=-1, keepdims=True)
    el2 = jnp.where(col == i1, -jnp.inf, el)
    v2 = jnp.max(el2, axis=-1, keepdims=True)
    i2 = jnp.min(jnp.where(el2 == v2, col, big), axis=-1, keepdims=True)
    e21 = jnp.exp(v2 - v1)
    inv = 1.0 / (1.0 + e21)
    w1 = p_g * inv
    w2 = p_g * (e21 * inv)
    e1 = (i1 - N_GROUPS).astype(F32)
    e2 = (i2 - N_GROUPS).astype(F32)
    return jnp.where(col == 0, e1, jnp.where(col == 1, e2, jnp.where(col == 2, w1, jnp.where(col == 3, w2, 0.0))))


def _merge_norm(x, rec, att, g_l, g_a, w_lru_out_ref, w_attn_out_ref, w_o_ref, ln_g, ln_b):
    rec_o = jnp.dot(rec.astype(BF16), w_lru_out_ref[...], preferred_element_type=F32)
    att_o = jnp.dot(att.astype(BF16), w_attn_out_ref[...], preferred_element_type=F32)
    merged = _sigmoid(g_l) * rec_o + _sigmoid(g_a) * att_o
    mix = jnp.dot(merged.astype(BF16), w_o_ref[...], preferred_element_type=F32)
    return _layer_norm(ALPHA * x + mix, ln_g, ln_b)


def _mixer_kernel(sinks_ref, x_ref, w_in_ref, b_in_ref, w_conv_ref, b_conv_ref, w_gates_ref, b_rg_ref, b_ig_ref,
                  lam_ref, w_lru_out_ref, w_attn_out_ref, w_o_ref, ln_g_ref, ln_b_ref, w_rt_hi_ref, w_rt_lo_ref,
                  b_rt_ref, x1_s_ref,
                  x1_ref, route_ref, kwin_ref, vwin_ref, conv_ref, h_ref, xs_hbm, goff_ref, gtot_ref,
                  conv_buf, h_carry, kcat, vcat, att_buf, prev_x1, stage, zero_rows, tri, xb_keep, key_keep,
                  g_vmem, gvec, g_smem, nd_smem, rounds_smem, sem_stage, sem_g, sem_zero,
                  *, tiles_per_seq, n_tiles, seg_cap):
    step = pl.program_id(0)
    last_step = pl.num_programs(0) - 1
    buf = lax.rem(step, 2)
    n_sample = x1_s_ref.shape[0]
    disp = _Dispatcher(xs_hbm, goff_ref, gtot_ref, stage, zero_rows, tri, xb_keep, key_keep, g_vmem, gvec,
                       g_smem, nd_smem, rounds_smem, sem_stage, sem_g, sem_zero, seg_cap)
    n_valid = jnp.where(step == 0, 0, jnp.where(step == last_step, n_sample, SEQ_TILE))

    @pl.when(step == 0)
    def _():
        disp.init()
        prev_x1[...] = jnp.zeros_like(prev_x1)

    def route_and_plan():
        x1_prev = prev_x1[...]
        route = _route(x1_prev, w_rt_hi_ref, w_rt_lo_ref, b_rt_ref[...])
        route_ref[...] = route
        disp.plan(x1_prev, route, n_valid, buf)

    @pl.when(step < n_tiles)
    def _():
        @pl.when(lax.rem(step, tiles_per_seq) == 0)
        def _():
            conv_buf[0:SUBLANES, :] = jnp.zeros((SUBLANES, LRU_WIDTH), F32)
            h_carry[...] = jnp.zeros_like(h_carry)
            kcat[0:WINDOW, :] = jnp.zeros((WINDOW, KV_WIDTH), BF16)
            vcat[0:WINDOW, :] = jnp.zeros((WINDOW, KV_WIDTH), BF16)

        route_and_plan()
        _mixer_tile(lax.rem(step, tiles_per_seq), sinks_ref, x_ref, w_in_ref, b_in_ref, w_conv_ref, b_conv_ref,
                    w_gates_ref, b_rg_ref, b_ig_ref, lam_ref, w_lru_out_ref, w_attn_out_ref, w_o_ref, ln_g_ref,
                    ln_b_ref, x1_ref, prev_x1, kwin_ref, vwin_ref, conv_ref, h_ref,
                    conv_buf, h_carry, kcat, vcat, att_buf)

    @pl.when(step >= n_tiles)
    def _():
        route_and_plan()

        @pl.when(step == n_tiles)
        def _():
            x1_ref[0:n_sample, :] = x1_s_ref[...]
            prev_x1[0:n_sample, :] = x1_s_ref[...]

    disp.flush(buf, step == last_step)


def _mixer_tile(t, sinks_ref, x_ref, w_in_ref, b_in_ref, w_conv_ref, b_conv_ref, w_gates_ref, b_rg_ref, b_ig_ref,
                lam_ref, w_lru_out_ref, w_attn_out_ref, w_o_ref, ln_g_ref, ln_b_ref,
                x1_ref, x1_keep, kwin_ref, vwin_ref, conv_ref, h_ref,
                conv_buf, h_carry, kcat, vcat, att_buf):
    T = SEQ_TILE
    x = x_ref[0]
    xb = x.astype(BF16)

    def proj(lo, width):
        return jnp.dot(xb, w_in_ref[:, lo:lo + width], preferred_element_type=F32) + b_in_ref[:, lo:lo + width]

    xl = proj(OFF_XL, LRU_WIDTH)
    conv_buf[SUBLANES:SUBLANES + T, :] = xl
    wc = w_conv_ref[...]
    xc = wc[0:1] * conv_buf[SUBLANES - 3:SUBLANES - 3 + T, :]
    xc = xc + wc[1:2] * conv_buf[SUBLANES - 2:SUBLANES - 2 + T, :]
    xc = xc + wc[2:3] * conv_buf[SUBLANES - 1:SUBLANES - 1 + T, :]
    xc = xc + wc[3:4] * xl + b_conv_ref[...]
    conv_ref[0] = conv_buf[T + SUBLANES - (CONV_W - 1):T + SUBLANES, :]
    conv_buf[0:SUBLANES, :] = conv_buf[T:T + SUBLANES, :]

    a, u = _lru_gates(xc, w_gates_ref, b_rg_ref[...], b_ig_ref[...], lam_ref[...])
    h = _linear_scan(a, u, h_carry[0:1, :])
    h_last = h[T - 1:T, :]
    h_carry[0:1, :] = h_last
    h_ref[0] = h_last
    rec = h * jax.nn.gelu(proj(OFF_YL, LRU_WIDTH))

    q = proj(OFF_Q, N_HEADS * HEAD_DIM) * (ATTN_SCALE * LOG2_E)
    k = proj(OFF_K, KV_WIDTH)
    v = proj(OFF_V, KV_WIDTH)
    kwin_ref[0] = k[T - WINDOW:, :]
    vwin_ref[0] = v[T - WINDOW:, :]
    kcat[WINDOW:WINDOW + T, :] = k.astype(BF16)
    vcat[WINDOW:WINDOW + T, :] = v.astype(BF16)

    qi = lax.broadcasted_iota(jnp.int32, (WINDOW, 2 * WINDOW), 0)
    kj = lax.broadcasted_iota(jnp.int32, (WINDOW, 2 * WINDOW), 1)
    band = (kj > qi) & (kj <= qi + WINDOW)
    grp_row = lax.broadcasted_iota(jnp.int32, (GROUP * WINDOW, 1), 0) // WINDOW
    for qb in range(T // WINDOW):
        if qb == 0:
            first_key = jnp.where(t == 0, WINDOW, 0)
            mask1 = band & (kj >= first_key)
        else:
            mask1 = band
        bias = jnp.concatenate([jnp.where(mask1, 0.0, NEG_INF)] * GROUP, axis=0)
        r0 = qb * WINDOW
        qq = q[r0:r0 + WINDOW, :]
        for j in range(N_KV):
            kjb = kcat[r0:r0 + 2 * WINDOW, j * HEAD_DIM:(j + 1) * HEAD_DIM]
            vjb = vcat[r0:r0 + 2 * WINDOW, j * HEAD_DIM:(j + 1) * HEAD_DIM]
            qs = jnp.concatenate(
                [qq[:, (j * GROUP + g) * HEAD_DIM:(j * GROUP + g + 1) * HEAD_DIM] for g in range(GROUP)], axis=0)
            s = lax.dot_general(qs.astype(BF16), kjb, (((1,), (1,)), ((), ())), preferred_element_type=F32) + bias
            sink = jnp.zeros((GROUP * WINDOW, 1), F32)
            for g in range(GROUP):
                sink = jnp.where(grp_row == g, sinks_ref[j * GROUP + g] * LOG2_E, sink)
            m = jnp.maximum(jnp.max(s, axis=-1, keepdims=True), sink)
            p = jnp.exp2(s - m)
            inv = 1.0 / (jnp.sum(p, axis=-1, keepdims=True) + jnp.exp2(sink - m))
            o = jnp.dot((p * inv).astype(BF16), vjb, preferred_element_type=F32)
            for g in range(GROUP):
                hcol = (j * GROUP + g) * HEAD_DIM
                att_buf[r0:r0 + WINDOW, hcol:hcol + HEAD_DIM] = o[g * WINDOW:(g + 1) * WINDOW, :]
    kcat[0:WINDOW, :] = kcat[T:T + WINDOW, :]
    vcat[0:WINDOW, :] = vcat[T:T + WINDOW, :]

    x1 = _merge_norm(x, rec, att_buf[...], proj(OFF_GL, D_MODEL), proj(OFF_GA, D_MODEL),
                     w_lru_out_ref, w_attn_out_ref, w_o_ref, ln_g_ref[...], ln_b_ref[...])
    x1_ref[...] = x1
    x1_keep[...] = x1


def _mixer_call(x_prompt, x1_s, sinks, wts):
    B, S, _ = x_prompt.shape
    T = SEQ_TILE
    assert T == DISPATCH_TILE and x1_s.shape[0] <= T
    nt = S // T
    n_tiles = B * nt
    n_rows_total = B * S + x1_s.shape[0]
    n_dispatch = _dispatch_tiles(n_rows_total)
    assert n_dispatch == n_tiles + 1
    seg_cap = _segment_capacity(n_rows_total)
    n_assign = TOP_K * T
    weight_args = (wts["w_in"], wts["b_in"], wts["w_conv"], wts["b_conv"], wts["w_gates"], wts["b_rg"], wts["b_ig"],
                   wts["lam"], wts["w_lru_out"], wts["w_attn_out"], wts["w_o"], wts["ln1_g"], wts["ln1_b"],
                   wts["w_rt_hi"], wts["w_rt_lo"], wts["b_rt"], x1_s)
    mixed = lambda i: jnp.minimum(i, n_tiles - 1)
    seq = lambda i: mixed(i) // nt
    routed = lambda i: jnp.clip(i - 1, 0, n_dispatch - 1)
    in_specs = [pl.BlockSpec(memory_space=pltpu.SMEM),
                pl.BlockSpec((1, T, D_MODEL), lambda i: (seq(i), lax.rem(mixed(i), nt), 0))]
    in_specs += [_const_spec(w.shape) for w in weight_args]
    out_shape = (
        jax.ShapeDtypeStruct((n_rows_total, D_MODEL), F32),
        jax.ShapeDtypeStruct((n_rows_total, ROUTE_WIDTH), F32),
        jax.ShapeDtypeStruct((B, WINDOW, KV_WIDTH), F32),
        jax.ShapeDtypeStruct((B, WINDOW, KV_WIDTH), F32),
        jax.ShapeDtypeStruct((B, CONV_W - 1, LRU_WIDTH), F32),
        jax.ShapeDtypeStruct((B, 1, LRU_WIDTH), F32),
        jax.ShapeDtypeStruct((N_EXPERTS * seg_cap, PACKED_WIDTH), jnp.uint32),
        jax.ShapeDtypeStruct((n_dispatch, 1, ROUTE_WIDTH), jnp.int32),
        jax.ShapeDtypeStruct((1, ROUTE_WIDTH), jnp.int32),
    )
    out_specs = (
        pl.BlockSpec((T, D_MODEL), lambda i: (jnp.minimum(i, n_tiles), 0)),
        pl.BlockSpec((T, ROUTE_WIDTH), lambda i: (routed(i), 0)),
        pl.BlockSpec((1, WINDOW, KV_WIDTH), lambda i: (seq(i), 0, 0)),
        pl.BlockSpec((1, WINDOW, KV_WIDTH), lambda i: (seq(i), 0, 0)),
        pl.BlockSpec((1, CONV_W - 1, LRU_WIDTH), lambda i: (seq(i), 0, 0)),
        pl.BlockSpec((1, 1, LRU_WIDTH), lambda i: (seq(i), 0, 0)),
        pl.BlockSpec(memory_space=pl.ANY),
        pl.BlockSpec((1, 1, ROUTE_WIDTH), lambda i: (routed(i), 0, 0)),
        pl.BlockSpec((1, ROUTE_WIDTH), lambda i: (0, 0)),
    )
    scratch = [
        pltpu.VMEM((T + 2 * SUBLANES, LRU_WIDTH), F32),
        pltpu.VMEM((SUBLANES, LRU_WIDTH), F32),
        pltpu.VMEM((T + WINDOW, KV_WIDTH), BF16),
        pltpu.VMEM((T + WINDOW, KV_WIDTH), BF16),
        pltpu.VMEM((T, N_HEADS * HEAD_DIM), F32),
        pltpu.VMEM((T, D_MODEL), F32),
        pltpu.VMEM((2, N_SLOTS, PACKED_WIDTH), jnp.uint32),
        pltpu.VMEM((SEG_PAD, PACKED_WIDTH), jnp.uint32),
        pltpu.VMEM((n_assign, n_assign), BF16),
        pltpu.VMEM((T, D_MODEL), BF16),
        pltpu.VMEM((SUBLANES, n_assign), F32),
        pltpu.VMEM((SUBLANES, ROUTE_WIDTH), jnp.int32),
        pltpu.VMEM((SUBLANES, ROUTE_WIDTH), F32),
        pltpu.SMEM((1, ROUTE_WIDTH), jnp.int32),
        pltpu.SMEM((1,), jnp.int32),
        pltpu.SMEM((1,), jnp.int32),
        pltpu.SemaphoreType.DMA(()),
        pltpu.SemaphoreType.DMA(()),
        pltpu.SemaphoreType.DMA(()),
    ]
    return pl.pallas_call(
        functools.partial(_mixer_kernel, tiles_per_seq=nt, n_tiles=n_tiles, seg_cap=seg_cap),
        grid=(n_tiles + 2,),
        in_specs=in_specs,
        out_specs=out_specs,
        out_shape=out_shape,
        scratch_shapes=scratch,
        compiler_params=pltpu.CompilerParams(dimension_semantics=("arbitrary",),
                                             vmem_limit_bytes=VMEM_LIMIT_BYTES),
        name="mixer_prompt",
    )(sinks, x_prompt, *weight_args)


def _sample_proj_kernel(x_ref, w_ref, b_ref, u_ref):
    u_ref[...] = jnp.dot(x_ref[...].astype(BF16), w_ref[...], preferred_element_type=F32) + b_ref[...]


def _sample_proj_call(x_s, w_in, b_in):
    n = x_s.shape[0]
    tn = SAMPLE_PROJ_TILE
    return pl.pallas_call(
        _sample_proj_kernel,
        grid=(IN_WIDTH // tn,),
        in_specs=[pl.BlockSpec((n, D_MODEL), lambda c: (0, 0)),
                  pl.BlockSpec((D_MODEL, tn), lambda c: (0, c)),
                  pl.BlockSpec((1, tn), lambda c: (0, c))],
        out_specs=pl.BlockSpec((n, tn), lambda c: (0, c)),
        out_shape=jax.ShapeDtypeStruct((n, IN_WIDTH), F32),
        compiler_params=pltpu.CompilerParams(dimension_semantics=("arbitrary",)),
        name="sample_proj",
    )(x_s, w_in, b_in)


def _sample_attn_kernel(q_ref, kn_ref, vn_ref, kn_row_ref, vn_row_ref, ck_ref, cv_ref, sinks_ref,
                        att_ref, kwin_ref, vwin_ref):
    tb = q_ref.shape[0]
    key_pos = lax.broadcasted_iota(jnp.int32, (tb, GROUP, WINDOW), 2)
    for j in range(N_KV):
        qj = q_ref[:, j * GROUP:(j + 1) * GROUP, :]
        kc = ck_ref[:, :, j * HEAD_DIM:(j + 1) * HEAD_DIM]
        vc = cv_ref[:, :, j * HEAD_DIM:(j + 1) * HEAD_DIM]
        s_c = jnp.einsum("bgd,bsd->bgs", qj.astype(BF16), kc.astype(BF16), preferred_element_type=F32) * ATTN_SCALE
        s_c = jnp.where(key_pos >= 1, s_c, NEG_INF)
        kn = kn_ref[:, j:j + 1, :]
        vn = vn_ref[:, j:j + 1, :]
        s_n = jnp.sum(qj * kn, axis=-1, keepdims=True) * ATTN_SCALE
        sink = sinks_ref[j][None]
        m = jnp.maximum(jnp.maximum(jnp.max(s_c, axis=-1, keepdims=True), s_n), sink)
        p_c = jnp.exp(s_c - m)
        p_n = jnp.exp(s_n - m)
        inv = 1.0 / (jnp.sum(p_c, axis=-1, keepdims=True) + p_n + jnp.exp(sink - m))
        o = jnp.einsum("bgs,bsd->bgd", (p_c * inv).astype(BF16), vc.astype(BF16), preferred_element_type=F32)
        att_ref[:, j * GROUP:(j + 1) * GROUP, :] = o + (p_n * inv) * vn
    kwin_ref[:, 0:WINDOW - 1, :] = ck_ref[:, 1:WINDOW, :]
    kwin_ref[:, WINDOW - 1:WINDOW, :] = kn_row_ref[...]
    vwin_ref[:, 0:WINDOW - 1, :] = cv_ref[:, 1:WINDOW, :]
    vwin_ref[:, WINDOW - 1:WINDOW, :] = vn_row_ref[...]


def _sample_attn_call(q3, kn3, vn3, kn_row, vn_row, ck, cv, sinks3):
    n = q3.shape[0]
    tb = SAMPLE_ATTN_TILE
    b3 = lambda i: (i, 0, 0)
    return pl.pallas_call(
        _sample_attn_kernel,
        grid=(n // tb,),
        in_specs=[pl.BlockSpec((tb, N_HEADS, HEAD_DIM), b3),
                  pl.BlockSpec((tb, N_KV, HEAD_DIM), b3),
                  pl.BlockSpec((tb, N_KV, HEAD_DIM), b3),
                  pl.BlockSpec((tb, 1, KV_WIDTH), b3),
                  pl.BlockSpec((tb, 1, KV_WIDTH), b3),
                  pl.BlockSpec((tb, WINDOW, KV_WIDTH), b3),
                  pl.BlockSpec((tb, WINDOW, KV_WIDTH), b3),
                  pl.BlockSpec((N_KV, GROUP, 1), lambda i: (0, 0, 0))],
        out_specs=(pl.BlockSpec((tb, N_HEADS, HEAD_DIM), b3),
                   pl.BlockSpec((tb, WINDOW, KV_WIDTH), b3),
                   pl.BlockSpec((tb, WINDOW, KV_WIDTH), b3)),
        out_shape=(jax.ShapeDtypeStruct((n, N_HEADS, HEAD_DIM), F32),
                   jax.ShapeDtypeStruct((n, WINDOW, KV_WIDTH), F32),
                   jax.ShapeDtypeStruct((n, WINDOW, KV_WIDTH), F32)),
        compiler_params=pltpu.CompilerParams(dimension_semantics=("arbitrary",)),
        name="sample_attn",
    )(q3, kn3, vn3, kn_row, vn_row, ck, cv, sinks3)


def _sample_mix_kernel(x_ref, u_ref, att_ref, st_ref, h0_ref, w_conv_ref, b_conv_ref, w_gates_ref, b_rg_ref,
                       b_ig_ref, lam_ref, w_lru_out_ref, w_attn_out_ref, w_o_ref, ln_g_ref, ln_b_ref,
                       x1_ref, conv_ref, h_ref):
    xl = u_ref[:, OFF_XL:OFF_XL + LRU_WIDTH]
    wc = w_conv_ref[...]
    xc = wc[0:1] * st_ref[0]
    xc = xc + wc[1:2] * st_ref[1]
    xc = xc + wc[2:3] * st_ref[2]
    xc = xc + wc[3:4] * xl + b_conv_ref[...]
    conv_ref[0] = st_ref[1]
    conv_ref[1] = st_ref[2]
    conv_ref[2] = xl
    a, u = _lru_gates(xc, w_gates_ref, b_rg_ref[...], b_ig_ref[...], lam_ref[...])
    h = a * h0_ref[...] + u
    h_ref[...] = h
    rec = h * jax.nn.gelu(u_ref[:, OFF_YL:OFF_YL + LRU_WIDTH])
    x1_ref[...] = _merge_norm(x_ref[...], rec, att_ref[...], u_ref[:, OFF_GL:OFF_GL + D_MODEL],
                              u_ref[:, OFF_GA:OFF_GA + D_MODEL], w_lru_out_ref, w_attn_out_ref, w_o_ref,
                              ln_g_ref[...], ln_b_ref[...])


def _sample_mix_call(x_s, u_s, att, st_t, h0, wts):
    n = x_s.shape[0]
    weight_args = (wts["w_conv"], wts["b_conv"], wts["w_gates"], wts["b_rg"], wts["b_ig"], wts["lam"],
                   wts["w_lru_out"], wts["w_attn_out"], wts["w_o"], wts["ln1_g"], wts["ln1_b"])
    args = (x_s, u_s, att, st_t, h0) + weight_args
    out_shapes = ((n, D_MODEL), (CONV_W - 1, n, LRU_WIDTH), (n, LRU_WIDTH))
    return pl.pallas_call(
        _sample_mix_kernel,
        grid=(1,),
        in_specs=[_const_spec(a.shape) for a in args],
        out_specs=tuple(_const_spec(s) for s in out_shapes),
        out_shape=tuple(jax.ShapeDtypeStruct(s, F32) for s in out_shapes),
        compiler_params=pltpu.CompilerParams(dimension_semantics=("arbitrary",),
                                             vmem_limit_bytes=VMEM_LIMIT_BYTES),
        name="sample_mix",
    )(*args)


def _one_hot(mask):
    return jnp.where(mask, 1.0, 0.0).astype(BF16)


def _pack_rows(x):
    half = x.shape[1] // 2
    lo = lax.shift_right_logical(lax.bitcast_convert_type(x[:, :half], jnp.uint32), jnp.uint32(16))
    hi = lax.bitcast_convert_type(x[:, half:], jnp.uint32) & jnp.uint32(0xFFFF0000)
    return lo | hi


def _unpack_rows(words):
    lo = lax.bitcast_convert_type(lax.shift_left(words, jnp.uint32(16)), F32)
    hi = lax.bitcast_convert_type(words & jnp.uint32(0xFFFF0000), F32)
    return jnp.concatenate([lo.astype(BF16), hi.astype(BF16)], axis=1)


class _Dispatcher:
    def __init__(self, xs_hbm, goff_ref, gtot_ref, stage, zero_rows, tri, xb_keep, key_keep, g_vmem, gvec,
                 g_smem, nd_smem, rounds_smem, sem_stage, sem_g, sem_zero, seg_cap):
        self.xs_hbm, self.goff_ref, self.gtot_ref = xs_hbm, goff_ref, gtot_ref
        self.stage, self.zero_rows, self.tri = stage, zero_rows, tri
        self.xb_keep, self.key_keep = xb_keep, key_keep
        self.g_vmem, self.gvec, self.g_smem = g_vmem, gvec, g_smem
        self.nd_smem, self.rounds_smem = nd_smem, rounds_smem
        self.sem_stage, self.sem_g, self.sem_zero = sem_stage, sem_g, sem_zero
        self.seg_cap = seg_cap

    def _g_copy(self):
        return pltpu.make_async_copy(self.g_vmem.at[pl.ds(0, 1)], self.g_smem, self.sem_g)

    def init(self):
        n_assign = self.tri.shape[0]
        r = lax.broadcasted_iota(jnp.int32, (n_assign, n_assign), 0)
        c = lax.broadcasted_iota(jnp.int32, (n_assign, n_assign), 1)
        self.tri[...] = _one_hot(r < c)
        self.gvec[...] = jnp.zeros_like(self.gvec)
        self.g_vmem[...] = jnp.zeros_like(self.g_vmem)
        self.nd_smem[0] = 0
        self._g_copy().start()

    def _sorted_rows(self, e_row, rank_row, xb, lo):
        td = xb.shape[0]
        slot = lax.broadcasted_iota(jnp.int32, (N_SLOTS, td), 0).astype(F32)
        lo_f = lo.astype(F32)
        in_round = (rank_row >= lo_f) & (rank_row < lo_f + SLOT_CAP) & (e_row >= 0.0)
        key = jnp.where(in_round, e_row * SLOT_CAP + (rank_row - lo_f), -1.0)
        perm = jnp.where(slot == key[:, :td], 1.0, jnp.where(slot == key[:, td:], 1.0, 0.0)).astype(BF16)
        return _pack_rows(jnp.dot(perm, xb, preferred_element_type=F32))

    def plan(self, x1, route, n_valid, buf):
        td = x1.shape[0]
        n_assign = TOP_K * td
        valid_col = lax.broadcasted_iota(jnp.int32, (td, 1), 0) < n_valid
        lane = lax.broadcasted_iota(jnp.int32, (td, ROUTE_WIDTH), 1).astype(F32)
        e0c = jnp.where(valid_col, route[:, 0:1], -1.0)
        e1c = jnp.where(valid_col, route[:, 1:2], -1.0)
        cnt_row = jnp.sum(jnp.where(lane == e0c, 1.0, 0.0) + jnp.where(lane == e1c, 1.0, 0.0),
                          axis=0, keepdims=True)
        self.rounds_smem[0] = (jnp.max(cnt_row).astype(jnp.int32) + SLOT_CAP - 1) // SLOT_CAP
        self.goff_ref[0] = self.gvec[0:1, :].astype(jnp.int32)
        self.gvec[0:1, :] = self.gvec[0:1, :] + jnp.ceil(cnt_row * (1.0 / SUBLANES)) * SUBLANES
        self.gtot_ref[...] = self.gvec[0:1, :].astype(jnp.int32)

        route_t = route.T
        valid_row = lax.broadcasted_iota(jnp.int32, (1, td), 1) < n_valid
        e_row = jnp.concatenate([jnp.where(valid_row, route_t[0:1, :], -1.0),
                                 jnp.where(valid_row, route_t[1:2, :], -1.0)], axis=1)
        expert_sub = lax.broadcasted_iota(jnp.int32, (N_EXPERTS, n_assign), 0).astype(F32)
        onehot_t = expert_sub == e_row
        rank_t = jnp.dot(_one_hot(onehot_t), self.tri[...], preferred_element_type=F32)
        rank_row = jnp.sum(jnp.where(onehot_t, rank_t, 0.0), axis=0, keepdims=True)
        xb = jnp.where(valid_col, x1, 0.0).astype(BF16)
        self.xb_keep[...] = xb
        self.key_keep[0:1, :] = e_row
        self.key_keep[1:2, :] = rank_row
        self.stage[buf] = self._sorted_rows(e_row, rank_row, xb, jnp.int32(0))

    def _stage_copy(self, buf, e, dst_row):
        src = self.stage.at[buf, pl.ds(pl.multiple_of(e * SLOT_CAP, SLOT_CAP), SLOT_CAP)]
        return pltpu.make_async_copy(src, self.xs_hbm.at[pl.ds(dst_row, SLOT_CAP)], self.sem_stage)

    def _wait_outstanding(self):
        def wait_one(_, carry):
            self._stage_copy(0, 0, 0).wait()
            return carry
        lax.fori_loop(0, self.nd_smem[0], wait_one, 0)
        self.nd_smem[0] = 0

    def _start_round(self, buf, lo):
        def issue(e, carry):
            dst_row = pl.multiple_of(e * self.seg_cap + self.g_smem[0, e] + lo, SUBLANES)
            self._stage_copy(buf, e, dst_row).start()
            return carry
        lax.fori_loop(0, N_EXPERTS, issue, 0)
        self.nd_smem[0] = N_EXPERTS

    def flush(self, buf, is_last):
        self._g_copy().wait()
        self._wait_outstanding()
        self._start_round(buf, 0)

        def later_round(r, carry):
            lo = r * SLOT_CAP
            rows = self._sorted_rows(self.key_keep[0:1, :], self.key_keep[1:2, :], self.xb_keep[...], lo)
            self._wait_outstanding()
            self.stage[buf] = rows
            self._start_round(buf, lo)
            return carry
        lax.fori_loop(1, self.rounds_smem[0], later_round, 0)

        self.g_vmem[0:1, :] = self.gvec[0:1, :].astype(jnp.int32)
        self._g_copy().start()

        @pl.when(is_last)
        def _():
            self._g_copy().wait()
            self._wait_outstanding()
            self.zero_rows[...] = jnp.zeros_like(self.zero_rows)

            def pad_copy(e):
                dst_row = pl.multiple_of(e * self.seg_cap + self.g_smem[0, e], SUBLANES)
                return pltpu.make_async_copy(self.zero_rows, self.xs_hbm.at[pl.ds(dst_row, SEG_PAD)], self.sem_zero)

            def start(e, carry):
                pad_copy(e).start()
                return carry
            lax.fori_loop(0, N_EXPERTS, start, 0)

            def wait(e, carry):
                pad_copy(e).wait()
                return carry
            lax.fori_loop(0, N_EXPERTS, wait, 0)


def _dispatch_kernel(x1_ref, route_ref, xs_hbm, goff_ref, gtot_ref,
                     stage, zero_rows, tri, g_vmem, gvec, g_smem, nd_smem, sem_stage, sem_g, sem_zero,
                     *, n_tokens, seg_cap):
    i = pl.program_id(0)
    last = pl.num_programs(0) - 1
    td = DISPATCH_TILE
    n_assign = TOP_K * td

    def g_copy():
        return pltpu.make_async_copy(g_vmem.at[pl.ds(0, 1)], g_smem, sem_g)

    @pl.when(i == 0)
    def _():
        r = lax.broadcasted_iota(jnp.int32, (n_assign, n_assign), 0)
        c = lax.broadcasted_iota(jnp.int32, (n_assign, n_assign), 1)
        tri[...] = _one_hot(r < c)
        gvec[...] = jnp.zeros_like(gvec)
        g_vmem[...] = jnp.zeros_like(g_vmem)
        nd_smem[0] = 0
        g_copy().start()

    n_valid = n_tokens - i * td
    route = route_ref[...]
    valid_col = lax.broadcasted_iota(jnp.int32, (td, 1), 0) < n_valid
    lane = lax.broadcasted_iota(jnp.int32, (td, ROUTE_WIDTH), 1).astype(F32)
    e0c = jnp.where(valid_col, route[:, 0:1], -1.0)
    e1c = jnp.where(valid_col, route[:, 1:2], -1.0)
    cnt_row = jnp.sum(jnp.where(lane == e0c, 1.0, 0.0) + jnp.where(lane == e1c, 1.0, 0.0), axis=0, keepdims=True)
    rounds = (jnp.max(cnt_row).astype(jnp.int32) + SLOT_CAP - 1) // SLOT_CAP
    goff_ref[0] = gvec[0:1, :].astype(jnp.int32)
    gvec[0:1, :] = gvec[0:1, :] + jnp.ceil(cnt_row * (1.0 / SUBLANES)) * SUBLANES
    gtot_ref[...] = gvec[0:1, :].astype(jnp.int32)

    route_t = route.T
    valid_row = lax.broadcasted_iota(jnp.int32, (1, td), 1) < n_valid
    e_row = jnp.concatenate([jnp.where(valid_row, route_t[0:1, :], -1.0),
                             jnp.where(valid_row, route_t[1:2, :], -1.0)], axis=1)
    expert_sub = lax.broadcasted_iota(jnp.int32, (N_EXPERTS, n_assign), 0).astype(F32)
    onehot_t = expert_sub == e_row
    rank_t = jnp.dot(_one_hot(onehot_t), tri[...], preferred_element_type=F32)
    rank_row = jnp.sum(jnp.where(onehot_t, rank_t, 0.0), axis=0, keepdims=True)
    xb = jnp.where(valid_col, x1_ref[...], 0.0).astype(BF16)
    slot = lax.broadcasted_iota(jnp.int32, (N_SLOTS, td), 0).astype(F32)

    def stage_copy(e, dst_row):
        src = stage.at[pl.ds(pl.multiple_of(e * SLOT_CAP, SLOT_CAP), SLOT_CAP)]
        return pltpu.make_async_copy(src, xs_hbm.at[pl.ds(dst_row, SLOT_CAP)], sem_stage)

    def wait_outstanding():
        def wait_one(_, carry):
            stage_copy(0, 0).wait()
            return carry
        lax.fori_loop(0, nd_smem[0], wait_one, 0)
        nd_smem[0] = 0

    g_copy().wait()

    def round_body(r, carry):
        lo = r * SLOT_CAP
        lo_f = lo.astype(F32)
        in_round = (rank_row >= lo_f) & (rank_row < lo_f + SLOT_CAP) & (e_row >= 0.0)
        key = jnp.where(in_round, e_row * SLOT_CAP + (rank_row - lo_f), -1.0)
        perm = jnp.where(slot == key[:, :td], 1.0, jnp.where(slot == key[:, td:], 1.0, 0.0)).astype(BF16)
        sorted_rows = jnp.dot(perm, xb, preferred_element_type=F32)
        wait_outstanding()
        stage[...] = _pack_rows(sorted_rows)

        def issue(e, carry):
            stage_copy(e, pl.multiple_of(e * seg_cap + g_smem[0, e] + lo, SUBLANES)).start()
            return carry
        lax.fori_loop(0, N_EXPERTS, issue, 0)
        nd_smem[0] = N_EXPERTS
        return carry
    lax.fori_loop(0, rounds, round_body, 0)

    g_vmem[0:1, :] = gvec[0:1, :].astype(jnp.int32)
    g_copy().start()

    @pl.when(i == last)
    def _():
        g_copy().wait()
        wait_outstanding()
        zero_rows[...] = jnp.zeros_like(zero_rows)

        def pad_copy(e):
            dst_row = pl.multiple_of(e * seg_cap + g_smem[0, e], SUBLANES)
            return pltpu.make_async_copy(zero_rows, xs_hbm.at[pl.ds(dst_row, SEG_PAD)], sem_zero)

        def start(e, carry):
            pad_copy(e).start()
            return carry
        lax.fori_loop(0, N_EXPERTS, start, 0)

        def wait(e, carry):
            pad_copy(e).wait()
            return carry
        lax.fori_loop(0, N_EXPERTS, wait, 0)


def _dispatch_tiles(n_tokens):
    return (n_tokens + DISPATCH_TILE - 1) // DISPATCH_TILE


def _segment_capacity(n_tokens):
    alignment_slack = (SUBLANES - 1) * _dispatch_tiles(n_tokens)
    return (n_tokens + alignment_slack + SEG_PAD + MOE_TILE - 1) // MOE_TILE * MOE_TILE


def _dispatch_call(x1_all, route_all):
    n_tokens = x1_all.shape[0]
    td = DISPATCH_TILE
    n_tiles = (n_tokens + td - 1) // td
    seg_cap = _segment_capacity(n_tokens)
    n_assign = TOP_K * td
    return pl.pallas_call(
        functools.partial(_dispatch_kernel, n_tokens=n_tokens, seg_cap=seg_cap),
        grid=(n_tiles,),
        in_specs=[pl.BlockSpec((td, D_MODEL), lambda i: (i, 0)),
                  pl.BlockSpec((td, ROUTE_WIDTH), lambda i: (i, 0))],
        out_specs=(pl.BlockSpec(memory_space=pl.ANY),
                   pl.BlockSpec((1, 1, ROUTE_WIDTH), lambda i: (i, 0, 0)),
                   pl.BlockSpec((1, ROUTE_WIDTH), lambda i: (0, 0))),
        out_shape=(jax.ShapeDtypeStruct((N_EXPERTS * seg_cap, PACKED_WIDTH), jnp.uint32),
                   jax.ShapeDtypeStruct((n_tiles, 1, ROUTE_WIDTH), jnp.int32),
                   jax.ShapeDtypeStruct((1, ROUTE_WIDTH), jnp.int32)),
        scratch_shapes=[
            pltpu.VMEM((N_SLOTS, PACKED_WIDTH), jnp.uint32),
            pltpu.VMEM((SEG_PAD, PACKED_WIDTH), jnp.uint32),
            pltpu.VMEM((n_assign, n_assign), BF16),
            pltpu.VMEM((SUBLANES, ROUTE_WIDTH), jnp.int32),
            pltpu.VMEM((SUBLANES, ROUTE_WIDTH), F32),
            pltpu.SMEM((1, ROUTE_WIDTH), jnp.int32),
            pltpu.SMEM((1,), jnp.int32),
            pltpu.SemaphoreType.DMA(()),
            pltpu.SemaphoreType.DMA(()),
            pltpu.SemaphoreType.DMA(()),
        ],
        compiler_params=pltpu.CompilerParams(dimension_semantics=("arbitrary",),
                                             vmem_limit_bytes=VMEM_LIMIT_BYTES),
        name="moe_dispatch",
    )(x1_all, route_all)


def _moe_kernel(blk_e_ref, blk_j_ref, n_used_ref, xs_ref, wg_ref, wu_ref, wd_ref, ys_ref, wg_bf, wu_bf, wd_bf):
    del blk_j_ref
    b = pl.program_id(0)

    @pl.when(b < n_used_ref[0])
    def _():
        prev_e = blk_e_ref[jnp.maximum(b - 1, 0)]

        @pl.when((b == 0) | (blk_e_ref[b] != prev_e))
        def _():
            wg_bf[...] = wg_ref[0].astype(BF16)
            wu_bf[...] = wu_ref[0].astype(BF16)
            wd_bf[...] = wd_ref[0].astype(BF16)

        xb = _unpack_rows(xs_ref[...])
        g = jnp.dot(xb, wg_bf[...], preferred_element_type=F32)
        u = jnp.dot(xb, wu_bf[...], preferred_element_type=F32)
        hmid = (g * _sigmoid(g)) * u
        y = jnp.dot(hmid.astype(BF16), wd_bf[...], preferred_element_type=F32)
        ys_ref[...] = _pack_rows(y.astype(BF16).astype(F32))


def _expert_blocks(gtot, n_blocks):
    rows = gtot[0, :N_EXPERTS]
    nb = (rows + SLOT_CAP + MOE_TILE - 1) // MOE_TILE
    ends = jnp.cumsum(nb)
    n_used = ends[-1]
    step = jnp.minimum(jnp.arange(n_blocks, dtype=jnp.int32), n_used - 1)
    blk_e = jnp.sum((step[:, None] >= ends[None, :]).astype(jnp.int32), axis=1)
    blk_j = step - (ends - nb)[blk_e]
    return blk_e, blk_j.astype(jnp.int32), n_used.reshape(1).astype(jnp.int32)


def _moe_call(gtot, xs, w_gate, w_up, w_down, n_tokens):
    seg_cap = _segment_capacity(n_tokens)
    seg_blocks = seg_cap // MOE_TILE
    max_rows = n_tokens * TOP_K + N_EXPERTS * (SUBLANES - 1) * _dispatch_tiles(n_tokens)
    n_blocks = (max_rows + N_EXPERTS * (SLOT_CAP + MOE_TILE - 1)) // MOE_TILE
    blk_e, blk_j, n_used = _expert_blocks(gtot, n_blocks)
    row_block = lambda b, be, bj, nu: (be[b] * seg_blocks + bj[b], 0)
    weight_block = lambda b, be, bj, nu: (be[b], 0, 0)
    grid_spec = pltpu.PrefetchScalarGridSpec(
        num_scalar_prefetch=3,
        grid=(n_blocks,),
        in_specs=[
            pl.BlockSpec((MOE_TILE, PACKED_WIDTH), row_block),
            pl.BlockSpec((1, D_MODEL, EXPERT_FF), weight_block),
            pl.BlockSpec((1, D_MODEL, EXPERT_FF), weight_block),
            pl.BlockSpec((1, EXPERT_FF, D_MODEL), weight_block),
        ],
        out_specs=pl.BlockSpec((MOE_TILE, PACKED_WIDTH), row_block),
        scratch_shapes=[
            pltpu.VMEM((D_MODEL, EXPERT_FF), BF16),
            pltpu.VMEM((D_MODEL, EXPERT_FF), BF16),
            pltpu.VMEM((EXPERT_FF, D_MODEL), BF16),
        ],
    )
    return pl.pallas_call(
        _moe_kernel,
        grid_spec=grid_spec,
        out_shape=jax.ShapeDtypeStruct(xs.shape, xs.dtype),
        compiler_params=pltpu.CompilerParams(dimension_semantics=("arbitrary",),
                                             vmem_limit_bytes=VMEM_LIMIT_BYTES),
        name="moe_experts",
    )(blk_e, blk_j, n_used, xs, w_gate, w_up, w_down)


def _combine_kernel(gcur_ref, gnext_ref, x1_ref, route_ref, ys_hbm, ln_g_ref, ln_b_ref, y_ref,
                    stage, acc, tri, sem, *, seg_cap):
    td = x1_ref.shape[0]
    n_assign = TOP_K * td
    i = pl.program_id(0)
    buf = lax.rem(i, 2)

    def stage_copy(g_ref, e, first_rank, to_buf):
        src_row = pl.multiple_of(e * seg_cap + g_ref[0, 0, e] + first_rank, SUBLANES)
        dst = stage.at[to_buf, pl.ds(pl.multiple_of(e * SLOT_CAP, SLOT_CAP), SLOT_CAP)]
        return pltpu.make_async_copy(ys_hbm.at[pl.ds(src_row, SLOT_CAP)], dst, sem.at[to_buf])

    def fetch_first_round(g_ref, to_buf):
        def start(e, carry):
            stage_copy(g_ref, e, 0, to_buf).start()
            return carry
        lax.fori_loop(0, N_EXPERTS, start, 0)

    @pl.when(i == 0)
    def _():
        r = lax.broadcasted_iota(jnp.int32, (n_assign, n_assign), 0)
        c = lax.broadcasted_iota(jnp.int32, (n_assign, n_assign), 1)
        tri[...] = _one_hot(c < r)
        fetch_first_round(gcur_ref, 0)

    def count(e):
        return gnext_ref[0, 0, e] - gcur_ref[0, 0, e]

    n_max = lax.fori_loop(0, N_EXPERTS, lambda e, m: jnp.maximum(m, count(e)), 0)
    rounds = (n_max + SLOT_CAP - 1) // SLOT_CAP

    route = route_ref[...]
    e0, e1, w0, w1 = route[:, 0:1], route[:, 1:2], route[:, 2:3], route[:, 3:4]
    lane = lax.broadcasted_iota(jnp.int32, (td, ROUTE_WIDTH), 1).astype(F32)
    onehot = jnp.concatenate([lane == e0, lane == e1], axis=0)
    onehot_f = jnp.where(onehot, 1.0, 0.0)
    rank_mat = jnp.dot(tri[...], onehot_f.astype(BF16), preferred_element_type=F32)
    rank = jnp.sum(jnp.where(onehot, rank_mat, 0.0), axis=1, keepdims=True)
    r0, r1 = rank[:td], rank[td:]
    cnt_row = jnp.sum(onehot_f, axis=0, keepdims=True)

    slot_col = lax.broadcasted_iota(jnp.int32, (N_SLOTS, 1), 0)
    slot_expert = slot_col // SLOT_CAP
    slot_rank = slot_col - slot_expert * SLOT_CAP
    lane_s = lax.broadcasted_iota(jnp.int32, (N_SLOTS, ROUTE_WIDTH), 1)
    n_col = jnp.sum(jnp.where(lane_s == slot_expert, cnt_row, 0.0), axis=1, keepdims=True)
    slot_lane = lax.broadcasted_iota(jnp.int32, (td, N_SLOTS), 1).astype(F32)

    def wait_copies(n):
        def wait_one(_, c):
            stage_copy(gcur_ref, 0, 0, buf).wait()
            return c
        lax.fori_loop(0, n, wait_one, 0)

    def weighted_rows(lo):
        lo_f = lo.astype(F32)
        live = (slot_rank + lo).astype(F32) < n_col
        rows = _unpack_rows(jnp.where(live, stage[buf], jnp.uint32(0)))

        def gathered(e_col, r_col):
            in_round = (r_col >= lo_f) & (r_col < lo_f + SLOT_CAP)
            key = jnp.where(in_round, e_col * SLOT_CAP + (r_col - lo_f), -1.0)
            return jnp.dot(_one_hot(slot_lane == key), rows, preferred_element_type=F32)
        return w0 * gathered(e0, r0) + w1 * gathered(e1, r1)

    wait_copies(N_EXPERTS)

    @pl.when(i + 1 < pl.num_programs(0))
    def _():
        fetch_first_round(gnext_ref, 1 - buf)

    acc[...] = weighted_rows(jnp.int32(0))

    def later_round(r, carry):
        lo = r * SLOT_CAP

        def start(e, n_started):
            has_rows = count(e) > lo

            @pl.when(has_rows)
            def _():
                stage_copy(gcur_ref, e, lo, buf).start()
            return n_started + has_rows.astype(jnp.int32)
        wait_copies(lax.fori_loop(0, N_EXPERTS, start, 0))
        acc[...] = acc[...] + weighted_rows(lo)
        return carry
    lax.fori_loop(1, rounds, later_round, 0)

    y_ref[...] = _layer_norm(ALPHA * x1_ref[...] + acc[...], ln_g_ref[...], ln_b_ref[...])


def _combine_call(goff, x1_all, route_all, ys, ln_g, ln_b, first_row, n_rows, tile, n_tokens):
    assert first_row % DISPATCH_TILE == 0 and first_row % tile == 0
    assert tile == DISPATCH_TILE or n_rows == tile
    first_block = first_row // tile
    first_goff = first_row // DISPATCH_TILE
    n_assign = TOP_K * tile
    grid_spec = pl.GridSpec(
        grid=(n_rows // tile,),
        in_specs=[
            pl.BlockSpec((1, 1, ROUTE_WIDTH), lambda i: (first_goff + i, 0, 0), memory_space=pltpu.SMEM),
            pl.BlockSpec((1, 1, ROUTE_WIDTH), lambda i: (first_goff + i + 1, 0, 0), memory_space=pltpu.SMEM),
            pl.BlockSpec((tile, D_MODEL), lambda i: (first_block + i, 0)),
            pl.BlockSpec((tile, ROUTE_WIDTH), lambda i: (first_block + i, 0)),
            pl.BlockSpec(memory_space=pl.ANY),
            _const_spec(ln_g.shape),
            _const_spec(ln_b.shape),
        ],
        out_specs=pl.BlockSpec((tile, D_MODEL), lambda i: (i, 0)),
        scratch_shapes=[pltpu.VMEM((2, N_SLOTS, PACKED_WIDTH), jnp.uint32),
                        pltpu.VMEM((tile, D_MODEL), F32),
                        pltpu.VMEM((n_assign, n_assign), BF16),
                        pltpu.SemaphoreType.DMA((2,))],
    )
    return pl.pallas_call(
        functools.partial(_combine_kernel, seg_cap=_segment_capacity(n_tokens)),
        grid_spec=grid_spec,
        out_shape=jax.ShapeDtypeStruct((n_rows, D_MODEL), F32),
        compiler_params=pltpu.CompilerParams(dimension_semantics=("arbitrary",),
                                             vmem_limit_bytes=VMEM_LIMIT_BYTES),
        name="moe_combine",
    )(goff, goff, x1_all, route_all, ys, ln_g, ln_b)


def _prepare_weights(w_in, b_in, w_conv, b_conv, w_rg, b_rg, w_ig, b_ig, lru_lambda, w_lru_out, w_attn_out, w_o,
                     ln1_g, ln1_b, w_group, b_group, w_router, b_router):
    blocks_per_chunk = GATE_CHUNK // LRU_BLOCK

    def chunked_block_diag(w):
        w = w.reshape(N_GATE_CHUNKS, blocks_per_chunk, LRU_BLOCK, LRU_BLOCK)
        eye = jnp.eye(blocks_per_chunk, dtype=w.dtype)
        return jnp.einsum("cbij,bd->cbidj", w, eye).reshape(N_GATE_CHUNKS, GATE_CHUNK, GATE_CHUNK)

    w_gates = jnp.concatenate([chunked_block_diag(w_rg), chunked_block_diag(w_ig)], axis=-1).astype(BF16)
    w_rt = jnp.concatenate([w_group, w_router], axis=1)
    w_rt = jnp.pad(w_rt, ((0, 0), (0, ROUTE_WIDTH - w_rt.shape[1])))
    w_rt_hi = w_rt.astype(BF16)
    w_rt_lo = (w_rt - w_rt_hi.astype(F32)).astype(BF16)
    b_rt = jnp.pad(jnp.concatenate([b_group, b_router]), (0, ROUTE_WIDTH - N_GROUPS - N_EXPERTS))
    row = lambda v: v.reshape(1, -1)
    return dict(
        w_in=w_in.astype(BF16), b_in=row(b_in), w_conv=w_conv, b_conv=row(b_conv), w_gates=w_gates,
        b_rg=row(b_rg), b_ig=row(b_ig), lam=row(lru_lambda),
        w_lru_out=w_lru_out.astype(BF16), w_attn_out=w_attn_out.astype(BF16), w_o=w_o.astype(BF16),
        ln1_g=row(ln1_g), ln1_b=row(ln1_b), w_rt_hi=w_rt_hi, w_rt_lo=w_rt_lo, b_rt=row(b_rt))


def kernel(x_prompt, x_sample, cache_k, cache_v, state_conv, state_lru_h, w_in, b_in, w_conv, b_conv, w_rg, b_rg,
           w_ig, b_ig, lru_lambda, sinks, w_lru_out, w_attn_out, w_o, ln1_g, ln1_b, w_group, b_group, w_router,
           b_router, w_gate, w_up, w_down, ln2_g, ln2_b):
    B, S, _ = x_prompt.shape
    n_prompt = B * S
    n_sample = x_sample.shape[0]
    n_all = n_prompt + n_sample
    wts = _prepare_weights(w_in, b_in, w_conv, b_conv, w_rg, b_rg, w_ig, b_ig, lru_lambda, w_lru_out, w_attn_out,
                           w_o, ln1_g, ln1_b, w_group, b_group, w_router, b_router)

    x_s = x_sample.reshape(n_sample, D_MODEL)
    u_s = _sample_proj_call(x_s, wts["w_in"], wts["b_in"])
    q3 = u_s[:, OFF_Q:OFF_K].reshape(n_sample, N_HEADS, HEAD_DIM)
    k_new = u_s[:, OFF_K:OFF_V]
    v_new = u_s[:, OFF_V:OFF_GL]
    att3, k_win_s, v_win_s = _sample_attn_call(
        q3, k_new.reshape(n_sample, N_KV, HEAD_DIM), v_new.reshape(n_sample, N_KV, HEAD_DIM),
        k_new.reshape(n_sample, 1, KV_WIDTH), v_new.reshape(n_sample, 1, KV_WIDTH),
        cache_k.reshape(n_sample, WINDOW, KV_WIDTH), cache_v.reshape(n_sample, WINDOW, KV_WIDTH),
        sinks.reshape(N_KV, GROUP, 1))
    x1_s, conv_s_t, h_s = _sample_mix_call(
        x_s, u_s, att3.reshape(n_sample, N_HEADS * HEAD_DIM), jnp.transpose(state_conv, (1, 0, 2)), state_lru_h, wts)

    x1_all, route_all, k_win_p, v_win_p, conv_p, h_p, xs, goff, gtot = _mixer_call(x_prompt, x1_s, sinks, wts)

    ys = _moe_call(gtot, xs, w_gate, w_up, w_down, n_all)
    goff = jnp.concatenate([goff, gtot[None]], axis=0)
    ln2_g2, ln2_b2 = ln2_g.reshape(1, -1), ln2_b.reshape(1, -1)
    y_p = _combine_call(goff, x1_all, route_all, ys, ln2_g2, ln2_b2, 0, n_prompt, DISPATCH_TILE, n_all)
    y_s = _combine_call(goff, x1_all, route_all, ys, ln2_g2, ln2_b2, n_prompt, n_sample, n_sample, n_all)

    kv_shape = (WINDOW, N_KV, HEAD_DIM)
    return (y_p.reshape(B, S, D_MODEL), y_s.reshape(n_sample, 1, D_MODEL),
            k_win_p.reshape((B,) + kv_shape), v_win_p.reshape((B,) + kv_shape), conv_p, h_p.reshape(B, LRU_WIDTH),
            k_win_s.reshape((n_sample,) + kv_shape), v_win_s.reshape((n_sample,) + kv_shape),
            jnp.transpose(conv_s_t, (1, 0, 2)), h_s)
```

```python
import functools

import jax
import jax.numpy as jnp
from jax import lax
from jax.experimental import pallas as pl
from jax.experimental.pallas import tpu as pltpu

F32 = jnp.float32
BF16 = jnp.bfloat16

D_MODEL = 1024
LRU_WIDTH = 1024
LRU_BLOCK = 64
CONV_W = 4
LRU_C = 8.0
N_HEADS = 16
N_KV = 4
GROUP = N_HEADS // N_KV
HEAD_DIM = 64
KV_WIDTH = N_KV * HEAD_DIM
WINDOW = 128
NEG_INF = -1e30
N_GROUPS = 4
EXPERTS_PER_GROUP = 8
N_EXPERTS = N_GROUPS * EXPERTS_PER_GROUP
TOP_K = 2
EXPERT_FF = D_MODEL // 2
DEPTH = 1
ALPHA = (2 * DEPTH) ** 0.25
LN_EPS = 1e-5
ATTN_SCALE = HEAD_DIM ** -0.5
LOG2_E = 1.4426950408889634

OFF_XL = 0
OFF_YL = OFF_XL + LRU_WIDTH
OFF_Q = OFF_YL + LRU_WIDTH
OFF_K = OFF_Q + N_HEADS * HEAD_DIM
OFF_V = OFF_K + KV_WIDTH
OFF_GL = OFF_V + KV_WIDTH
OFF_GA = OFF_GL + D_MODEL
IN_WIDTH = OFF_GA + D_MODEL

LANES = 128
SUBLANES = 8
MXU_DIM = 256
VMEM_LIMIT_BYTES = 56 * 1024 * 1024

GATE_CHUNK = MXU_DIM
N_GATE_CHUNKS = LRU_WIDTH // GATE_CHUNK
ROUTE_WIDTH = LANES

SEQ_TILE = 256
MOE_TILE = 464
DISPATCH_TILE = 256
SLOT_CAP = 32
N_SLOTS = N_EXPERTS * SLOT_CAP
SEG_PAD = MOE_TILE + SLOT_CAP
PACKED_WIDTH = D_MODEL // 2
SAMPLE_ATTN_TILE = 16
SAMPLE_PROJ_TILE = 512


def _const_spec(shape):
    nd = len(shape)
    return pl.BlockSpec(shape, lambda *_: (0,) * nd)


def _layer_norm(z, g, b):
    mu = jnp.mean(z, axis=-1, keepdims=True)
    zc = z - mu
    var = jnp.mean(zc * zc, axis=-1, keepdims=True)
    return zc * lax.rsqrt(var + LN_EPS) * g + b


def _sigmoid(x):
    return 1.0 / (1.0 + jnp.exp2(x * -LOG2_E))


def _softplus(x):
    return jnp.maximum(x, 0.0) + jnp.log1p(jnp.exp(-jnp.abs(x)))


def _bdot(a, b):
    return jnp.dot(a.astype(BF16), b.astype(BF16), preferred_element_type=F32)


def _lru_gates(xc, w_gates_ref, b_rg, b_ig, lam):
    xcb = xc.astype(BF16)
    r_parts, i_parts = [], []
    for c in range(N_GATE_CHUNKS):
        g = jnp.dot(xcb[:, c * GATE_CHUNK:(c + 1) * GATE_CHUNK], w_gates_ref[c], preferred_element_type=F32)
        r_parts.append(g[:, :GATE_CHUNK])
        i_parts.append(g[:, GATE_CHUNK:])
    r = _sigmoid(jnp.concatenate(r_parts, axis=1) + b_rg)
    i = _sigmoid(jnp.concatenate(i_parts, axis=1) + b_ig)
    log_a = (-LRU_C * r) * _softplus(-lam)
    a = jnp.exp(log_a)
    u = jnp.sqrt(1.0 - a * a) * (i * xc)
    return a, u


def _linear_scan(a, u, h_in):
    n, w = a.shape
    groups = n // SUBLANES
    a3 = a.reshape(groups, SUBLANES, w)
    u3 = u.reshape(groups, SUBLANES, w)
    row = lax.broadcasted_iota(jnp.int32, a3.shape, 1)
    d = 1
    while d < SUBLANES:
        has_prev = row >= d
        u3 = u3 + a3 * jnp.where(has_prev, pltpu.roll(u3, d, axis=1), 0.0)
        a3 = a3 * jnp.where(has_prev, pltpu.roll(a3, d, axis=1), 1.0)
        d *= 2
    carry = h_in
    out = []
    for g in range(groups):
        h_g = u3[g] + a3[g] * carry
        out.append(h_g)
        carry = h_g[SUBLANES - 1:SUBLANES, :]
    return jnp.concatenate(out, axis=0)


def _route(x1, w_hi_ref, w_lo_ref, b_rt):
    x_hi = x1.astype(BF16)
    x_lo = (x1 - x_hi.astype(F32)).astype(BF16)
    w_hi = w_hi_ref[...]
    logits = (jnp.dot(x_hi, w_hi, preferred_element_type=F32)
              + (jnp.dot(x_lo, w_hi, preferred_element_type=F32)
                 + jnp.dot(x_hi, w_lo_ref[...], preferred_element_type=F32))) + b_rt
    col = lax.broadcasted_iota(jnp.int32, logits.shape, 1)
    big = jnp.int32(ROUTE_WIDTH)
    is_g = col < N_GROUPS
    gl = jnp.where(is_g, logits, -jnp.inf)
    gmax = jnp.max(gl, axis=-1, keepdims=True)
    g_idx = jnp.min(jnp.where(gl == gmax, col, big), axis=-1, keepdims=True)
    p_g = 1.0 / jnp.sum(jnp.where(is_g, jnp.exp(gl - gmax), 0.0), axis=-1, keepdims=True)
    lo = N_GROUPS + g_idx * EXPERTS_PER_GROUP
    in_grp = (col >= lo) & (col < lo + EXPERTS_PER_GROUP)
    el = jnp.where(in_grp, logits, -jnp.inf)
    v1 = jnp.max(el, axis=-1, keepdims=True)
    i1 = jnp.min(jnp.where(el == v1, col, big), axis=-1, keepdims=True)
    el2 = jnp.where(col == i1, -jnp.inf, el)
    v2 = jnp.max(el2, axis=-1, keepdims=True)
    i2 = jnp.min(jnp.where(el2 == v2, col, big), axis=-1, keepdims=True)
    e21 = jnp.exp(v2 - v1)
    inv = 1.0 / (1.0 + e21)
    w1 = p_g * inv
    w2 = p_g * (e21 * inv)
    e1 = (i1 - N_GROUPS).astype(F32)
    e2 = (i2 - N_GROUPS).astype(F32)
    return jnp.where(col == 0, e1, jnp.where(col == 1, e2, jnp.where(col == 2, w1, jnp.where(col == 3, w2, 0.0))))


def _merge_norm(x, rec, att, g_l, g_a, w_lru_out_ref, w_attn_out_ref, w_o_ref, ln_g, ln_b):
    rec_o = jnp.dot(rec.astype(BF16), w_lru_out_ref[...], preferred_element_type=F32)
    att_o = jnp.dot(att.astype(BF16), w_attn_out_ref[...], preferred_element_type=F32)
    merged = _sigmoid(g_l) * rec_o + _sigmoid(g_a) * att_o
    mix = jnp.dot(merged.astype(BF16), w_o_ref[...], preferred_element_type=F32)
    return _layer_norm(ALPHA * x + mix, ln_g, ln_b)


def _mixer_kernel(sinks_ref, x_ref, w_in_ref, b_in_ref, w_conv_ref, b_conv_ref, w_gates_ref, b_rg_ref, b_ig_ref,
                  lam_ref, w_lru_out_ref, w_attn_out_ref, w_o_ref, ln_g_ref, ln_b_ref, w_rt_hi_ref, w_rt_lo_ref,
                  b_rt_ref, x1_s_ref,
                  x1_ref, route_ref, kwin_ref, vwin_ref, conv_ref, h_ref, xs_hbm, goff_ref, gtot_ref,
                  conv_buf, h_carry, kcat, vcat, att_buf, prev_x1, stage, zero_rows, tri, xb_keep, key_keep,
                  g_vmem, gvec, g_smem, nd_smem, rounds_smem, sem_stage, sem_g, sem_zero,
                  *, tiles_per_seq, n_tiles, seg_cap):
    step = pl.program_id(0)
    last_step = pl.num_programs(0) - 1
    buf = lax.rem(step, 2)
    n_sample = x1_s_ref.shape[0]
    disp = _Dispatcher(xs_hbm, goff_ref, gtot_ref, stage, zero_rows, tri, xb_keep, key_keep, g_vmem, gvec,
                       g_smem, nd_smem, rounds_smem, sem_stage, sem_g, sem_zero, seg_cap)
    n_valid = jnp.where(step == 0, 0, jnp.where(step == last_step, n_sample, SEQ_TILE))

    @pl.when(step == 0)
    def _():
        disp.init()
        prev_x1[...] = jnp.zeros_like(prev_x1)

    def route_and_plan():
        x1_prev = prev_x1[...]
        route = _route(x1_prev, w_rt_hi_ref, w_rt_lo_ref, b_rt_ref[...])
        route_ref[...] = route
        disp.plan(x1_prev, route, n_valid, buf)

    @pl.when(step < n_tiles)
    def _():
        @pl.when(lax.rem(step, tiles_per_seq) == 0)
        def _():
            conv_buf[0:SUBLANES, :] = jnp.zeros((SUBLANES, LRU_WIDTH), F32)
            h_carry[...] = jnp.zeros_like(h_carry)
            kcat[0:WINDOW, :] = jnp.zeros((WINDOW, KV_WIDTH), BF16)
            vcat[0:WINDOW, :] = jnp.zeros((WINDOW, KV_WIDTH), BF16)

        route_and_plan()
        _mixer_tile(lax.rem(step, tiles_per_seq), sinks_ref, x_ref, w_in_ref, b_in_ref, w_conv_ref, b_conv_ref,
                    w_gates_ref, b_rg_ref, b_ig_ref, lam_ref, w_lru_out_ref, w_attn_out_ref, w_o_ref, ln_g_ref,
                    ln_b_ref, x1_ref, prev_x1, kwin_ref, vwin_ref, conv_ref, h_ref,
                    conv_buf, h_carry, kcat, vcat, att_buf)

    @pl.when(step >= n_tiles)
    def _():
        route_and_plan()

        @pl.when(step == n_tiles)
        def _():
            x1_ref[0:n_sample, :] = x1_s_ref[...]
            prev_x1[0:n_sample, :] = x1_s_ref[...]

    disp.flush(buf, step == last_step)


def _mixer_tile(t, sinks_ref, x_ref, w_in_ref, b_in_ref, w_conv_ref, b_conv_ref, w_gates_ref, b_rg_ref, b_ig_ref,
                lam_ref, w_lru_out_ref, w_attn_out_ref, w_o_ref, ln_g_ref, ln_b_ref,
                x1_ref, x1_keep, kwin_ref, vwin_ref, conv_ref, h_ref,
                conv_buf, h_carry, kcat, vcat, att_buf):
    T = SEQ_TILE
    x = x_ref[0]
    xb = x.astype(BF16)

    def proj(lo, width):
        return jnp.dot(xb, w_in_ref[:, lo:lo + width], preferred_element_type=F32) + b_in_ref[:, lo:lo + width]

    xl = proj(OFF_XL, LRU_WIDTH)
    conv_buf[SUBLANES:SUBLANES + T, :] = xl
    wc = w_conv_ref[...]
    xc = wc[0:1] * conv_buf[SUBLANES - 3:SUBLANES - 3 + T, :]
    xc = xc + wc[1:2] * conv_buf[SUBLANES - 2:SUBLANES - 2 + T, :]
    xc = xc + wc[2:3] * conv_buf[SUBLANES - 1:SUBLANES - 1 + T, :]
    xc = xc + wc[3:4] * xl + b_conv_ref[...]
    conv_ref[0] = conv_buf[T + SUBLANES - (CONV_W - 1):T + SUBLANES, :]
    conv_buf[0:SUBLANES, :] = conv_buf[T:T + SUBLANES, :]

    a, u = _lru_gates(xc, w_gates_ref, b_rg_ref[...], b_ig_ref[...], lam_ref[...])
    h = _linear_scan(a, u, h_carry[0:1, :])
    h_last = h[T - 1:T, :]
    h_carry[0:1, :] = h_last
    h_ref[0] = h_last
    rec = h * jax.nn.gelu(proj(OFF_YL, LRU_WIDTH))

    q = proj(OFF_Q, N_HEADS * HEAD_DIM) * (ATTN_SCALE * LOG2_E)
    k = proj(OFF_K, KV_WIDTH)
    v = proj(OFF_V, KV_WIDTH)
    kwin_ref[0] = k[T - WINDOW:, :]
    vwin_ref[0] = v[T - WINDOW:, :]
    kcat[WINDOW:WINDOW + T, :] = k.astype(BF16)
    vcat[WINDOW:WINDOW + T, :] = v.astype(BF16)

    qi = lax.broadcasted_iota(jnp.int32, (WINDOW, 2 * WINDOW), 0)
    kj = lax.broadcasted_iota(jnp.int32, (WINDOW, 2 * WINDOW), 1)
    band = (kj > qi) & (kj <= qi + WINDOW)
    grp_row = lax.broadcasted_iota(jnp.int32, (GROUP * WINDOW, 1), 0) // WINDOW
    for qb in range(T // WINDOW):
        if qb == 0:
            first_key = jnp.where(t == 0, WINDOW, 0)
            mask1 = band & (kj >= first_key)
        else:
            mask1 = band
        bias = jnp.concatenate([jnp.where(mask1, 0.0, NEG_INF)] * GROUP, axis=0)
        r0 = qb * WINDOW
        qq = q[r0:r0 + WINDOW, :]
        for j in range(N_KV):
            kjb = kcat[r0:r0 + 2 * WINDOW, j * HEAD_DIM:(j + 1) * HEAD_DIM]
            vjb = vcat[r0:r0 + 2 * WINDOW, j * HEAD_DIM:(j + 1) * HEAD_DIM]
            qs = jnp.concatenate(
                [qq[:, (j * GROUP + g) * HEAD_DIM:(j * GROUP + g + 1) * HEAD_DIM] for g in range(GROUP)], axis=0)
            s = lax.dot_general(qs.astype(BF16), kjb, (((1,), (1,)), ((), ())), preferred_element_type=F32) + bias
            sink = jnp.zeros((GROUP * WINDOW, 1), F32)
            for g in range(GROUP):
                sink = jnp.where(grp_row == g, sinks_ref[j * GROUP + g] * LOG2_E, sink)
            m = jnp.maximum(jnp.max(s, axis=-1, keepdims=True), sink)
            p = jnp.exp2(s - m)
            inv = 1.0 / (jnp.sum(p, axis=-1, keepdims=True) + jnp.exp2(sink - m))
            o = jnp.dot((p * inv).astype(BF16), vjb, preferred_element_type=F32)
            for g in range(GROUP):
                hcol = (j * GROUP + g) * HEAD_DIM
                att_buf[r0:r0 + WINDOW, hcol:hcol + HEAD_DIM] = o[g * WINDOW:(g + 1) * WINDOW, :]
    kcat[0:WINDOW, :] = kcat[T:T + WINDOW, :]
    vcat[0:WINDOW, :] = vcat[T:T + WINDOW, :]

    x1 = _merge_norm(x, rec, att_buf[...], proj(OFF_GL, D_MODEL), proj(OFF_GA, D_MODEL),
                     w_lru_out_ref, w_attn_out_ref, w_o_ref, ln_g_ref[...], ln_b_ref[...])
    x1_ref[...] = x1
    x1_keep[...] = x1


def _mixer_call(x_prompt, x1_s, sinks, wts):
    B, S, _ = x_prompt.shape
    T = SEQ_TILE
    assert T == DISPATCH_TILE and x1_s.shape[0] <= T
    nt = S // T
    n_tiles = B * nt
    n_rows_total = B * S + x1_s.shape[0]
    n_dispatch = _dispatch_tiles(n_rows_total)
    assert n_dispatch == n_tiles + 1
    seg_cap = _segment_capacity(n_rows_total)
    n_assign = TOP_K * T
    weight_args = (wts["w_in"], wts["b_in"], wts["w_conv"], wts["b_conv"], wts["w_gates"], wts["b_rg"], wts["b_ig"],
                   wts["lam"], wts["w_lru_out"], wts["w_attn_out"], wts["w_o"], wts["ln1_g"], wts["ln1_b"],
                   wts["w_rt_hi"], wts["w_rt_lo"], wts["b_rt"], x1_s)
    mixed = lambda i: jnp.minimum(i, n_tiles - 1)
    seq = lambda i: mixed(i) // nt
    routed = lambda i: jnp.clip(i - 1, 0, n_dispatch - 1)
    in_specs = [pl.BlockSpec(memory_space=pltpu.SMEM),
                pl.BlockSpec((1, T, D_MODEL), lambda i: (seq(i), lax.rem(mixed(i), nt), 0))]
    in_specs += [_const_spec(w.shape) for w in weight_args]
    out_shape = (
        jax.ShapeDtypeStruct((n_rows_total, D_MODEL), F32),
        jax.ShapeDtypeStruct((n_rows_total, ROUTE_WIDTH), F32),
        jax.ShapeDtypeStruct((B, WINDOW, KV_WIDTH), F32),
        jax.ShapeDtypeStruct((B, WINDOW, KV_WIDTH), F32),
        jax.ShapeDtypeStruct((B, CONV_W - 1, LRU_WIDTH), F32),
        jax.ShapeDtypeStruct((B, 1, LRU_WIDTH), F32),
        jax.ShapeDtypeStruct((N_EXPERTS * seg_cap, PACKED_WIDTH), jnp.uint32),
        jax.ShapeDtypeStruct((n_dispatch, 1, ROUTE_WIDTH), jnp.int32),
        jax.ShapeDtypeStruct((1, ROUTE_WIDTH), jnp.int32),
    )
    out_specs = (
        pl.BlockSpec((T, D_MODEL), lambda i: (jnp.minimum(i, n_tiles), 0)),
        pl.BlockSpec((T, ROUTE_WIDTH), lambda i: (routed(i), 0)),
        pl.BlockSpec((1, WINDOW, KV_WIDTH), lambda i: (seq(i), 0, 0)),
        pl.BlockSpec((1, WINDOW, KV_WIDTH), lambda i: (seq(i), 0, 0)),
        pl.BlockSpec((1, CONV_W - 1, LRU_WIDTH), lambda i: (seq(i), 0, 0)),
        pl.BlockSpec((1, 1, LRU_WIDTH), lambda i: (seq(i), 0, 0)),
        pl.BlockSpec(memory_space=pl.ANY),
        pl.BlockSpec((1, 1, ROUTE_WIDTH), lambda i: (routed(i), 0, 0)),
        pl.BlockSpec((1, ROUTE_WIDTH), lambda i: (0, 0)),
    )
    scratch = [
        pltpu.VMEM((T + 2 * SUBLANES, LRU_WIDTH), F32),
        pltpu.VMEM((SUBLANES, LRU_WIDTH), F32),
        pltpu.VMEM((T + WINDOW, KV_WIDTH), BF16),
        pltpu.VMEM((T + WINDOW, KV_WIDTH), BF16),
        pltpu.VMEM((T, N_HEADS * HEAD_DIM), F32),
        pltpu.VMEM((T, D_MODEL), F32),
        pltpu.VMEM((2, N_SLOTS, PACKED_WIDTH), jnp.uint32),
        pltpu.VMEM((SEG_PAD, PACKED_WIDTH), jnp.uint32),
        pltpu.VMEM((n_assign, n_assign), BF16),
        pltpu.VMEM((T, D_MODEL), BF16),
        pltpu.VMEM((SUBLANES, n_assign), F32),
        pltpu.VMEM((SUBLANES, ROUTE_WIDTH), jnp.int32),
        pltpu.VMEM((SUBLANES, ROUTE_WIDTH), F32),
        pltpu.SMEM((1, ROUTE_WIDTH), jnp.int32),
        pltpu.SMEM((1,), jnp.int32),
        pltpu.SMEM((1,), jnp.int32),
        pltpu.SemaphoreType.DMA(()),
        pltpu.SemaphoreType.DMA(()),
        pltpu.SemaphoreType.DMA(()),
    ]
    return pl.pallas_call(
        functools.partial(_mixer_kernel, tiles_per_seq=nt, n_tiles=n_tiles, seg_cap=seg_cap),
        grid=(n_tiles + 2,),
        in_specs=in_specs,
        out_specs=out_specs,
        out_shape=out_shape,
        scratch_shapes=scratch,
        compiler_params=pltpu.CompilerParams(dimension_semantics=("arbitrary",),
                                             vmem_limit_bytes=VMEM_LIMIT_BYTES),
        name="mixer_prompt",
    )(sinks, x_prompt, *weight_args)


def _sample_proj_kernel(x_ref, w_ref, b_ref, u_ref):
    u_ref[...] = jnp.dot(x_ref[...].astype(BF16), w_ref[...], preferred_element_type=F32) + b_ref[...]


def _sample_proj_call(x_s, w_in, b_in):
    n = x_s.shape[0]
    tn = SAMPLE_PROJ_TILE
    return pl.pallas_call(
        _sample_proj_kernel,
        grid=(IN_WIDTH // tn,),
        in_specs=[pl.BlockSpec((n, D_MODEL), lambda c: (0, 0)),
                  pl.BlockSpec((D_MODEL, tn), lambda c: (0, c)),
                  pl.BlockSpec((1, tn), lambda c: (0, c))],
        out_specs=pl.BlockSpec((n, tn), lambda c: (0, c)),
        out_shape=jax.ShapeDtypeStruct((n, IN_WIDTH), F32),
        compiler_params=pltpu.CompilerParams(dimension_semantics=("arbitrary",)),
        name="sample_proj",
    )(x_s, w_in, b_in)


def _sample_attn_kernel(q_ref, kn_ref, vn_ref, kn_row_ref, vn_row_ref, ck_ref, cv_ref, sinks_ref,
                        att_ref, kwin_ref, vwin_ref):
    tb = q_ref.shape[0]
    key_pos = lax.broadcasted_iota(jnp.int32, (tb, GROUP, WINDOW), 2)
    for j in range(N_KV):
        qj = q_ref[:, j * GROUP:(j + 1) * GROUP, :]
        kc = ck_ref[:, :, j * HEAD_DIM:(j + 1) * HEAD_DIM]
        vc = cv_ref[:, :, j * HEAD_DIM:(j + 1) * HEAD_DIM]
        s_c = jnp.einsum("bgd,bsd->bgs", qj.astype(BF16), kc.astype(BF16), preferred_element_type=F32) * ATTN_SCALE
        s_c = jnp.where(key_pos >= 1, s_c, NEG_INF)
        kn = kn_ref[:, j:j + 1, :]
        vn = vn_ref[:, j:j + 1, :]
        s_n = jnp.sum(qj * kn, axis=-1, keepdims=True) * ATTN_SCALE
        sink = sinks_ref[j][None]
        m = jnp.maximum(jnp.maximum(jnp.max(s_c, axis=-1, keepdims=True), s_n), sink)
        p_c = jnp.exp(s_c - m)
        p_n = jnp.exp(s_n - m)
        inv = 1.0 / (jnp.sum(p_c, axis=-1, keepdims=True) + p_n + jnp.exp(sink - m))
        o = jnp.einsum("bgs,bsd->bgd", (p_c * inv).astype(BF16), vc.astype(BF16), preferred_element_type=F32)
        att_ref[:, j * GROUP:(j + 1) * GROUP, :] = o + (p_n * inv) * vn
    kwin_ref[:, 0:WINDOW - 1, :] = ck_ref[:, 1:WINDOW, :]
    kwin_ref[:, WINDOW - 1:WINDOW, :] = kn_row_ref[...]
    vwin_ref[:, 0:WINDOW - 1, :] = cv_ref[:, 1:WINDOW, :]
    vwin_ref[:, WINDOW - 1:WINDOW, :] = vn_row_ref[...]


def _sample_attn_call(q3, kn3, vn3, kn_row, vn_row, ck, cv, sinks3):
    n = q3.shape[0]
    tb = SAMPLE_ATTN_TILE
    b3 = lambda i: (i, 0, 0)
    return pl.pallas_call(
        _sample_attn_kernel,
        grid=(n // tb,),
        in_specs=[pl.BlockSpec((tb, N_HEADS, HEAD_DIM), b3),
                  pl.BlockSpec((tb, N_KV, HEAD_DIM), b3),
                  pl.BlockSpec((tb, N_KV, HEAD_DIM), b3),
                  pl.BlockSpec((tb, 1, KV_WIDTH), b3),
                  pl.BlockSpec((tb, 1, KV_WIDTH), b3),
                  pl.BlockSpec((tb, WINDOW, KV_WIDTH), b3),
                  pl.BlockSpec((tb, WINDOW, KV_WIDTH), b3),
                  pl.BlockSpec((N_KV, GROUP, 1), lambda i: (0, 0, 0))],
        out_specs=(pl.BlockSpec((tb, N_HEADS, HEAD_DIM), b3),
                   pl.BlockSpec((tb, WINDOW, KV_WIDTH), b3),
                   pl.BlockSpec((tb, WINDOW, KV_WIDTH), b3)),
        out_shape=(jax.ShapeDtypeStruct((n, N_HEADS, HEAD_DIM), F32),
                   jax.ShapeDtypeStruct((n, WINDOW, KV_WIDTH), F32),
                   jax.ShapeDtypeStruct((n, WINDOW, KV_WIDTH), F32)),
        compiler_params=pltpu.CompilerParams(dimension_semantics=("arbitrary",)),
        name="sample_attn",
    )(q3, kn3, vn3, kn_row, vn_row, ck, cv, sinks3)


def _sample_mix_kernel(x_ref, u_ref, att_ref, st_ref, h0_ref, w_conv_ref, b_conv_ref, w_gates_ref, b_rg_ref,
                       b_ig_ref, lam_ref, w_lru_out_ref, w_attn_out_ref, w_o_ref, ln_g_ref, ln_b_ref,
                       x1_ref, conv_ref, h_ref):
    xl = u_ref[:, OFF_XL:OFF_XL + LRU_WIDTH]
    wc = w_conv_ref[...]
    xc = wc[0:1] * st_ref[0]
    xc = xc + wc[1:2] * st_ref[1]
    xc = xc + wc[2:3] * st_ref[2]
    xc = xc + wc[3:4] * xl + b_conv_ref[...]
    conv_ref[0] = st_ref[1]
    conv_ref[1] = st_ref[2]
    conv_ref[2] = xl
    a, u = _lru_gates(xc, w_gates_ref, b_rg_ref[...], b_ig_ref[...], lam_ref[...])
    h = a * h0_ref[...] + u
    h_ref[...] = h
    rec = h * jax.nn.gelu(u_ref[:, OFF_YL:OFF_YL + LRU_WIDTH])
    x1_ref[...] = _merge_norm(x_ref[...], rec, att_ref[...], u_ref[:, OFF_GL:OFF_GL + D_MODEL],
                              u_ref[:, OFF_GA:OFF_GA + D_MODEL], w_lru_out_ref, w_attn_out_ref, w_o_ref,
                              ln_g_ref[...], ln_b_ref[...])


def _sample_mix_call(x_s, u_s, att, st_t, h0, wts):
    n = x_s.shape[0]
    weight_args = (wts["w_conv"], wts["b_conv"], wts["w_gates"], wts["b_rg"], wts["b_ig"], wts["lam"],
                   wts["w_lru_out"], wts["w_attn_out"], wts["w_o"], wts["ln1_g"], wts["ln1_b"])
    args = (x_s, u_s, att, st_t, h0) + weight_args
    out_shapes = ((n, D_MODEL), (CONV_W - 1, n, LRU_WIDTH), (n, LRU_WIDTH))
    return pl.pallas_call(
        _sample_mix_kernel,
        grid=(1,),
        in_specs=[_const_spec(a.shape) for a in args],
        out_specs=tuple(_const_spec(s) for s in out_shapes),
        out_shape=tuple(jax.ShapeDtypeStruct(s, F32) for s in out_shapes),
        compiler_params=pltpu.CompilerParams(dimension_semantics=("arbitrary",),
                                             vmem_limit_bytes=VMEM_LIMIT_BYTES),
        name="sample_mix",
    )(*args)


def _one_hot(mask):
    return jnp.where(mask, 1.0, 0.0).astype(BF16)


def _pack_rows(x):
    half = x.shape[1] // 2
    lo = lax.shift_right_logical(lax.bitcast_convert_type(x[:, :half], jnp.uint32), jnp.uint32(16))
    hi = lax.bitcast_convert_type(x[:, half:], jnp.uint32) & jnp.uint32(0xFFFF0000)
    return lo | hi


def _unpack_rows(words):
    lo = lax.bitcast_convert_type(lax.shift_left(words, jnp.uint32(16)), F32)
    hi = lax.bitcast_convert_type(words & jnp.uint32(0xFFFF0000), F32)
    return jnp.concatenate([lo.astype(BF16), hi.astype(BF16)], axis=1)


class _Dispatcher:
    def __init__(self, xs_hbm, goff_ref, gtot_ref, stage, zero_rows, tri, xb_keep, key_keep, g_vmem, gvec,
                 g_smem, nd_smem, rounds_smem, sem_stage, sem_g, sem_zero, seg_cap):
        self.xs_hbm, self.goff_ref, self.gtot_ref = xs_hbm, goff_ref, gtot_ref
        self.stage, self.zero_rows, self.tri = stage, zero_rows, tri
        self.xb_keep, self.key_keep = xb_keep, key_keep
        self.g_vmem, self.gvec, self.g_smem = g_vmem, gvec, g_smem
        self.nd_smem, self.rounds_smem = nd_smem, rounds_smem
        self.sem_stage, self.sem_g, self.sem_zero = sem_stage, sem_g, sem_zero
        self.seg_cap = seg_cap

    def _g_copy(self):
        return pltpu.make_async_copy(self.g_vmem.at[pl.ds(0, 1)], self.g_smem, self.sem_g)

    def init(self):
        n_assign = self.tri.shape[0]
        r = lax.broadcasted_iota(jnp.int32, (n_assign, n_assign), 0)
        c = lax.broadcasted_iota(jnp.int32, (n_assign, n_assign), 1)
        self.tri[...] = _one_hot(r < c)
        self.gvec[...] = jnp.zeros_like(self.gvec)
        self.g_vmem[...] = jnp.zeros_like(self.g_vmem)
        self.nd_smem[0] = 0
        self._g_copy().start()

    def _sorted_rows(self, e_row, rank_row, xb, lo):
        td = xb.shape[0]
        slot = lax.broadcasted_iota(jnp.int32, (N_SLOTS, td), 0).astype(F32)
        lo_f = lo.astype(F32)
        in_round = (rank_row >= lo_f) & (rank_row < lo_f + SLOT_CAP) & (e_row >= 0.0)
        key = jnp.where(in_round, e_row * SLOT_CAP + (rank_row - lo_f), -1.0)
        perm = jnp.where(slot == key[:, :td], 1.0, jnp.where(slot == key[:, td:], 1.0, 0.0)).astype(BF16)
        return _pack_rows(jnp.dot(perm, xb, preferred_element_type=F32))

    def plan(self, x1, route, n_valid, buf):
        td = x1.shape[0]
        n_assign = TOP_K * td
        valid_col = lax.broadcasted_iota(jnp.int32, (td, 1), 0) < n_valid
        lane = lax.broadcasted_iota(jnp.int32, (td, ROUTE_WIDTH), 1).astype(F32)
        e0c = jnp.where(valid_col, route[:, 0:1], -1.0)
        e1c = jnp.where(valid_col, route[:, 1:2], -1.0)
        cnt_row = jnp.sum(jnp.where(lane == e0c, 1.0, 0.0) + jnp.where(lane == e1c, 1.0, 0.0),
                          axis=0, keepdims=True)
        self.rounds_smem[0] = (jnp.max(cnt_row).astype(jnp.int32) + SLOT_CAP - 1) // SLOT_CAP
        self.goff_ref[0] = self.gvec[0:1, :].astype(jnp.int32)
        self.gvec[0:1, :] = self.gvec[0:1, :] + jnp.ceil(cnt_row * (1.0 / SUBLANES)) * SUBLANES
        self.gtot_ref[...] = self.gvec[0:1, :].astype(jnp.int32)

        route_t = route.T
        valid_row = lax.broadcasted_iota(jnp.int32, (1, td), 1) < n_valid
        e_row = jnp.concatenate([jnp.where(valid_row, route_t[0:1, :], -1.0),
                                 jnp.where(valid_row, route_t[1:2, :], -1.0)], axis=1)
        expert_sub = lax.broadcasted_iota(jnp.int32, (N_EXPERTS, n_assign), 0).astype(F32)
        onehot_t = expert_sub == e_row
        rank_t = jnp.dot(_one_hot(onehot_t), self.tri[...], preferred_element_type=F32)
        rank_row = jnp.sum(jnp.where(onehot_t, rank_t, 0.0), axis=0, keepdims=True)
        xb = jnp.where(valid_col, x1, 0.0).astype(BF16)
        self.xb_keep[...] = xb
        self.key_keep[0:1, :] = e_row
        self.key_keep[1:2, :] = rank_row
        self.stage[buf] = self._sorted_rows(e_row, rank_row, xb, jnp.int32(0))

    def _stage_copy(self, buf, e, dst_row):
        src = self.stage.at[buf, pl.ds(pl.multiple_of(e * SLOT_CAP, SLOT_CAP), SLOT_CAP)]
        return pltpu.make_async_copy(src, self.xs_hbm.at[pl.ds(dst_row, SLOT_CAP)], self.sem_stage)

    def _wait_outstanding(self):
        def wait_one(_, carry):
            self._stage_copy(0, 0, 0).wait()
            return carry
        lax.fori_loop(0, self.nd_smem[0], wait_one, 0)
        self.nd_smem[0] = 0

    def _start_round(self, buf, lo):
        def issue(e, carry):
            dst_row = pl.multiple_of(e * self.seg_cap + self.g_smem[0, e] + lo, SUBLANES)
            self._stage_copy(buf, e, dst_row).start()
            return carry
        lax.fori_loop(0, N_EXPERTS, issue, 0)
        self.nd_smem[0] = N_EXPERTS

    def flush(self, buf, is_last):
        self._g_copy().wait()
        self._wait_outstanding()
        self._start_round(buf, 0)

        def later_round(r, carry):
            lo = r * SLOT_CAP
            rows = self._sorted_rows(self.key_keep[0:1, :], self.key_keep[1:2, :], self.xb_keep[...], lo)
            self._wait_outstanding()
            self.stage[buf] = rows
            self._start_round(buf, lo)
            return carry
        lax.fori_loop(1, self.rounds_smem[0], later_round, 0)

        self.g_vmem[0:1, :] = self.gvec[0:1, :].astype(jnp.int32)
        self._g_copy().start()

        @pl.when(is_last)
        def _():
            self._g_copy().wait()
            self._wait_outstanding()
            self.zero_rows[...] = jnp.zeros_like(self.zero_rows)

            def pad_copy(e):
                dst_row = pl.multiple_of(e * self.seg_cap + self.g_smem[0, e], SUBLANES)
                return pltpu.make_async_copy(self.zero_rows, self.xs_hbm.at[pl.ds(dst_row, SEG_PAD)], self.sem_zero)

            def start(e, carry):
                pad_copy(e).start()
                return carry
            lax.fori_loop(0, N_EXPERTS, start, 0)

            def wait(e, carry):
                pad_copy(e).wait()
                return carry
            lax.fori_loop(0, N_EXPERTS, wait, 0)


def _dispatch_tiles(n_tokens):
    return (n_tokens + DISPATCH_TILE - 1) // DISPATCH_TILE


def _segment_capacity(n_tokens):
    alignment_slack = (SUBLANES - 1) * _dispatch_tiles(n_tokens)
    return (n_tokens + alignment_slack + SEG_PAD + MOE_TILE - 1) // MOE_TILE * MOE_TILE


def _moe_kernel(blk_e_ref, blk_j_ref, n_used_ref, xs_ref, wg_ref, wu_ref, wd_ref, ys_ref, wg_bf, wu_bf, wd_bf):
    del blk_j_ref
    b = pl.program_id(0)

    @pl.when(b < n_used_ref[0])
    def _():
        prev_e = blk_e_ref[jnp.maximum(b - 1, 0)]

        @pl.when((b == 0) | (blk_e_ref[b] != prev_e))
        def _():
            wg_bf[...] = wg_ref[0].astype(BF16)
            wu_bf[...] = wu_ref[0].astype(BF16)
            wd_bf[...] = wd_ref[0].astype(BF16)

        xb = _unpack_rows(xs_ref[...])
        g = jnp.dot(xb, wg_bf[...], preferred_element_type=F32)
        u = jnp.dot(xb, wu_bf[...], preferred_element_type=F32)
        hmid = (g * _sigmoid(g)) * u
        y = jnp.dot(hmid.astype(BF16), wd_bf[...], preferred_element_type=F32)
        ys_ref[...] = _pack_rows(y.astype(BF16).astype(F32))


def _expert_blocks(gtot, n_blocks):
    rows = gtot[0, :N_EXPERTS]
    nb = (rows + SLOT_CAP + MOE_TILE - 1) // MOE_TILE
    ends = jnp.cumsum(nb)
    n_used = ends[-1]
    step = jnp.minimum(jnp.arange(n_blocks, dtype=jnp.int32), n_used - 1)
    blk_e = jnp.sum((step[:, None] >= ends[None, :]).astype(jnp.int32), axis=1)
    blk_j = step - (ends - nb)[blk_e]
    return blk_e, blk_j.astype(jnp.int32), n_used.reshape(1).astype(jnp.int32)


def _moe_call(gtot, xs, w_gate, w_up, w_down, n_tokens):
    seg_cap = _segment_capacity(n_tokens)
    seg_blocks = seg_cap // MOE_TILE
    max_rows = n_tokens * TOP_K + N_EXPERTS * (SUBLANES - 1) * _dispatch_tiles(n_tokens)
    n_blocks = (max_rows + N_EXPERTS * (SLOT_CAP + MOE_TILE - 1)) // MOE_TILE
    blk_e, blk_j, n_used = _expert_blocks(gtot, n_blocks)
    row_block = lambda b, be, bj, nu: (be[b] * seg_blocks + bj[b], 0)
    weight_block = lambda b, be, bj, nu: (be[b], 0, 0)
    grid_spec = pltpu.PrefetchScalarGridSpec(
        num_scalar_prefetch=3,
        grid=(n_blocks,),
        in_specs=[
            pl.BlockSpec((MOE_TILE, PACKED_WIDTH), row_block),
            pl.BlockSpec((1, D_MODEL, EXPERT_FF), weight_block),
            pl.BlockSpec((1, D_MODEL, EXPERT_FF), weight_block),
            pl.BlockSpec((1, EXPERT_FF, D_MODEL), weight_block),
        ],
        out_specs=pl.BlockSpec((MOE_TILE, PACKED_WIDTH), row_block),
        scratch_shapes=[
            pltpu.VMEM((D_MODEL, EXPERT_FF), BF16),
            pltpu.VMEM((D_MODEL, EXPERT_FF), BF16),
            pltpu.VMEM((EXPERT_FF, D_MODEL), BF16),
        ],
    )
    return pl.pallas_call(
        _moe_kernel,
        grid_spec=grid_spec,
        out_shape=jax.ShapeDtypeStruct(xs.shape, xs.dtype),
        compiler_params=pltpu.CompilerParams(dimension_semantics=("arbitrary",),
                                             vmem_limit_bytes=VMEM_LIMIT_BYTES),
        name="moe_experts",
    )(blk_e, blk_j, n_used, xs, w_gate, w_up, w_down)


def _combine_kernel(gcur_ref, gnext_ref, x1_ref, route_ref, ys_hbm, ln_g_ref, ln_b_ref, y_ref,
                    stage, acc, tri, sem, *, seg_cap):
    td = x1_ref.shape[0]
    n_assign = TOP_K * td
    i = pl.program_id(0)
    buf = lax.rem(i, 2)

    def stage_copy(g_ref, e, first_rank, to_buf):
        src_row = pl.multiple_of(e * seg_cap + g_ref[0, 0, e] + first_rank, SUBLANES)
        dst = stage.at[to_buf, pl.ds(pl.multiple_of(e * SLOT_CAP, SLOT_CAP), SLOT_CAP)]
        return pltpu.make_async_copy(ys_hbm.at[pl.ds(src_row, SLOT_CAP)], dst, sem.at[to_buf])

    def fetch_first_round(g_ref, to_buf):
        def start(e, carry):
            stage_copy(g_ref, e, 0, to_buf).start()
            return carry
        lax.fori_loop(0, N_EXPERTS, start, 0)

    @pl.when(i == 0)
    def _():
        r = lax.broadcasted_iota(jnp.int32, (n_assign, n_assign), 0)
        c = lax.broadcasted_iota(jnp.int32, (n_assign, n_assign), 1)
        tri[...] = _one_hot(c < r)
        fetch_first_round(gcur_ref, 0)

    def count(e):
        return gnext_ref[0, 0, e] - gcur_ref[0, 0, e]

    n_max = lax.fori_loop(0, N_EXPERTS, lambda e, m: jnp.maximum(m, count(e)), 0)
    rounds = (n_max + SLOT_CAP - 1) // SLOT_CAP

    route = route_ref[...]
    e0, e1, w0, w1 = route[:, 0:1], route[:, 1:2], route[:, 2:3], route[:, 3:4]
    lane = lax.broadcasted_iota(jnp.int32, (td, ROUTE_WIDTH), 1).astype(F32)
    onehot = jnp.concatenate([lane == e0, lane == e1], axis=0)
    onehot_f = jnp.where(onehot, 1.0, 0.0)
    rank_mat = jnp.dot(tri[...], onehot_f.astype(BF16), preferred_element_type=F32)
    rank = jnp.sum(jnp.where(onehot, rank_mat, 0.0), axis=1, keepdims=True)
    r0, r1 = rank[:td], rank[td:]
    cnt_row = jnp.sum(onehot_f, axis=0, keepdims=True)

    slot_col = lax.broadcasted_iota(jnp.int32, (N_SLOTS, 1), 0)
    slot_expert = slot_col // SLOT_CAP
    slot_rank = slot_col - slot_expert * SLOT_CAP
    lane_s = lax.broadcasted_iota(jnp.int32, (N_SLOTS, ROUTE_WIDTH), 1)
    n_col = jnp.sum(jnp.where(lane_s == slot_expert, cnt_row, 0.0), axis=1, keepdims=True)
    slot_lane = lax.broadcasted_iota(jnp.int32, (td, N_SLOTS), 1).astype(F32)

    def wait_copies(n):
        def wait_one(_, c):
            stage_copy(gcur_ref, 0, 0, buf).wait()
            return c
        lax.fori_loop(0, n, wait_one, 0)

    def weighted_rows(lo):
        lo_f = lo.astype(F32)
        live = (slot_rank + lo).astype(F32) < n_col
        rows = _unpack_rows(jnp.where(live, stage[buf], jnp.uint32(0)))

        def selector(e_col, r_col, w_col):
            in_round = (r_col >= lo_f) & (r_col < lo_f + SLOT_CAP)
            key = jnp.where(in_round, e_col * SLOT_CAP + (r_col - lo_f), -1.0)
            return jnp.where(slot_lane == key, w_col, 0.0)
        mix = (selector(e0, r0, w0) + selector(e1, r1, w1)).astype(BF16)
        return jnp.dot(mix, rows, preferred_element_type=F32)

    wait_copies(N_EXPERTS)

    @pl.when(i + 1 < pl.num_programs(0))
    def _():
        fetch_first_round(gnext_ref, 1 - buf)

    acc[...] = weighted_rows(jnp.int32(0))

    def later_round(r, carry):
        lo = r * SLOT_CAP

        def start(e, n_started):
            has_rows = count(e) > lo

            @pl.when(has_rows)
            def _():
                stage_copy(gcur_ref, e, lo, buf).start()
            return n_started + has_rows.astype(jnp.int32)
        wait_copies(lax.fori_loop(0, N_EXPERTS, start, 0))
        acc[...] = acc[...] + weighted_rows(lo)
        return carry
    lax.fori_loop(1, rounds, later_round, 0)

    y_ref[...] = _layer_norm(ALPHA * x1_ref[...] + acc[...], ln_g_ref[...], ln_b_ref[...])


def _combine_call(goff, x1_all, route_all, ys, ln_g, ln_b, first_row, n_rows, tile, n_tokens):
    assert first_row % DISPATCH_TILE == 0 and first_row % tile == 0
    assert tile == DISPATCH_TILE or n_rows == tile
    first_block = first_row // tile
    first_goff = first_row // DISPATCH_TILE
    n_assign = TOP_K * tile
    grid_spec = pl.GridSpec(
        grid=(n_rows // tile,),
        in_specs=[
            pl.BlockSpec((1, 1, ROUTE_WIDTH), lambda i: (first_goff + i, 0, 0), memory_space=pltpu.SMEM),
            pl.BlockSpec((1, 1, ROUTE_WIDTH), lambda i: (first_goff + i + 1, 0, 0), memory_space=pltpu.SMEM),
            pl.BlockSpec((tile, D_MODEL), lambda i: (first_block + i, 0)),
            pl.BlockSpec((tile, ROUTE_WIDTH), lambda i: (first_block + i, 0)),
            pl.BlockSpec(memory_space=pl.ANY),
            _const_spec(ln_g.shape),
            _const_spec(ln_b.shape),
        ],
        out_specs=pl.BlockSpec((tile, D_MODEL), lambda i: (i, 0)),
        scratch_shapes=[pltpu.VMEM((2, N_SLOTS, PACKED_WIDTH), jnp.uint32),
                        pltpu.VMEM((tile, D_MODEL), F32),
                        pltpu.VMEM((n_assign, n_assign), BF16),
                        pltpu.SemaphoreType.DMA((2,))],
    )
    return pl.pallas_call(
        functools.partial(_combine_kernel, seg_cap=_segment_capacity(n_tokens)),
        grid_spec=grid_spec,
        out_shape=jax.ShapeDtypeStruct((n_rows, D_MODEL), F32),
        compiler_params=pltpu.CompilerParams(dimension_semantics=("arbitrary",),
                                             vmem_limit_bytes=VMEM_LIMIT_BYTES),
        name="moe_combine",
    )(goff, goff, x1_all, route_all, ys, ln_g, ln_b)


def _prepare_weights(w_in, b_in, w_conv, b_conv, w_rg, b_rg, w_ig, b_ig, lru_lambda, w_lru_out, w_attn_out, w_o,
                     ln1_g, ln1_b, w_group, b_group, w_router, b_router):
    blocks_per_chunk = GATE_CHUNK // LRU_BLOCK

    def chunked_block_diag(w):
        w = w.reshape(N_GATE_CHUNKS, blocks_per_chunk, LRU_BLOCK, LRU_BLOCK)
        eye = jnp.eye(blocks_per_chunk, dtype=w.dtype)
        return jnp.einsum("cbij,bd->cbidj", w, eye).reshape(N_GATE_CHUNKS, GATE_CHUNK, GATE_CHUNK)

    w_gates = jnp.concatenate([chunked_block_diag(w_rg), chunked_block_diag(w_ig)], axis=-1).astype(BF16)
    w_rt = jnp.concatenate([w_group, w_router], axis=1)
    w_rt = jnp.pad(w_rt, ((0, 0), (0, ROUTE_WIDTH - w_rt.shape[1])))
    w_rt_hi = w_rt.astype(BF16)
    w_rt_lo = (w_rt - w_rt_hi.astype(F32)).astype(BF16)
    b_rt = jnp.pad(jnp.concatenate([b_group, b_router]), (0, ROUTE_WIDTH - N_GROUPS - N_EXPERTS))
    row = lambda v: v.reshape(1, -1)
    return dict(
        w_in=w_in.astype(BF16), b_in=row(b_in), w_conv=w_conv, b_conv=row(b_conv), w_gates=w_gates,
        b_rg=row(b_rg), b_ig=row(b_ig), lam=row(lru_lambda),
        w_lru_out=w_lru_out.astype(BF16), w_attn_out=w_attn_out.astype(BF16), w_o=w_o.astype(BF16),
        ln1_g=row(ln1_g), ln1_b=row(ln1_b), w_rt_hi=w_rt_hi, w_rt_lo=w_rt_lo, b_rt=row(b_rt))


def kernel(x_prompt, x_sample, cache_k, cache_v, state_conv, state_lru_h, w_in, b_in, w_conv, b_conv, w_rg, b_rg,
           w_ig, b_ig, lru_lambda, sinks, w_lru_out, w_attn_out, w_o, ln1_g, ln1_b, w_group, b_group, w_router,
           b_router, w_gate, w_up, w_down, ln2_g, ln2_b):
    B, S, _ = x_prompt.shape
    n_prompt = B * S
    n_sample = x_sample.shape[0]
    n_all = n_prompt + n_sample
    wts = _prepare_weights(w_in, b_in, w_conv, b_conv, w_rg, b_rg, w_ig, b_ig, lru_lambda, w_lru_out, w_attn_out,
                           w_o, ln1_g, ln1_b, w_group, b_group, w_router, b_router)

    x_s = x_sample.reshape(n_sample, D_MODEL)
    u_s = _sample_proj_call(x_s, wts["w_in"], wts["b_in"])
    q3 = u_s[:, OFF_Q:OFF_K].reshape(n_sample, N_HEADS, HEAD_DIM)
    k_new = u_s[:, OFF_K:OFF_V]
    v_new = u_s[:, OFF_V:OFF_GL]
    att3, k_win_s, v_win_s = _sample_attn_call(
        q3, k_new.reshape(n_sample, N_KV, HEAD_DIM), v_new.reshape(n_sample, N_KV, HEAD_DIM),
        k_new.reshape(n_sample, 1, KV_WIDTH), v_new.reshape(n_sample, 1, KV_WIDTH),
        cache_k.reshape(n_sample, WINDOW, KV_WIDTH), cache_v.reshape(n_sample, WINDOW, KV_WIDTH),
        sinks.reshape(N_KV, GROUP, 1))
    x1_s, conv_s_t, h_s = _sample_mix_call(
        x_s, u_s, att3.reshape(n_sample, N_HEADS * HEAD_DIM), jnp.transpose(state_conv, (1, 0, 2)), state_lru_h, wts)

    x1_all, route_all, k_win_p, v_win_p, conv_p, h_p, xs, goff, gtot = _mixer_call(x_prompt, x1_s, sinks, wts)

    ys = _moe_call(gtot, xs, w_gate, w_up, w_down, n_all)
    goff = jnp.concatenate([goff, gtot[None]], axis=0)
    ln2_g2, ln2_b2 = ln2_g.reshape(1, -1), ln2_b.reshape(1, -1)
    y_p = _combine_call(goff, x1_all, route_all, ys, ln2_g2, ln2_b2, 0, n_prompt, DISPATCH_TILE, n_all)
    y_s = _combine_call(goff, x1_all, route_all, ys, ln2_g2, ln2_b2, n_prompt, n_sample, n_sample, n_all)

    kv_shape = (WINDOW, N_KV, HEAD_DIM)
    return (y_p.reshape(B, S, D_MODEL), y_s.reshape(n_sample, 1, D_MODEL),
            k_win_p.reshape((B,) + kv_shape), v_win_p.reshape((B,) + kv_shape), conv_p, h_p.reshape(B, LRU_WIDTH),
            k_win_s.reshape((n_sample,) + kv_shape), v_win_s.reshape((n_sample,) + kv_shape),
            jnp.transpose(conv_s_t, (1, 0, 2)), h_s)
```

```python
import functools

import jax
import jax.numpy as jnp
from jax import lax
from jax.experimental import pallas as pl
from jax.experimental.pallas import tpu as pltpu

F32 = jnp.float32
BF16 = jnp.bfloat16

D_MODEL = 1024
LRU_WIDTH = 1024
LRU_BLOCK = 64
CONV_W = 4
LRU_C = 8.0
N_HEADS = 16
N_KV = 4
GROUP = N_HEADS // N_KV
HEAD_DIM = 64
KV_WIDTH = N_KV * HEAD_DIM
WINDOW = 128
NEG_INF = -1e30
N_GROUPS = 4
EXPERTS_PER_GROUP = 8
N_EXPERTS = N_GROUPS * EXPERTS_PER_GROUP
TOP_K = 2
EXPERT_FF = D_MODEL // 2
DEPTH = 1
ALPHA = (2 * DEPTH) ** 0.25
LN_EPS = 1e-5
ATTN_SCALE = HEAD_DIM ** -0.5
LOG2_E = 1.4426950408889634

OFF_XL = 0
OFF_YL = OFF_XL + LRU_WIDTH
OFF_Q = OFF_YL + LRU_WIDTH
OFF_K = OFF_Q + N_HEADS * HEAD_DIM
OFF_V = OFF_K + KV_WIDTH
OFF_GL = OFF_V + KV_WIDTH
OFF_GA = OFF_GL + D_MODEL
IN_WIDTH = OFF_GA + D_MODEL

LANES = 128
SUBLANES = 8
MXU_DIM = 256
VMEM_LIMIT_BYTES = 56 * 1024 * 1024

GATE_CHUNK = MXU_DIM
N_GATE_CHUNKS = LRU_WIDTH // GATE_CHUNK
ROUTE_WIDTH = LANES

SEQ_TILE = 256
MOE_TILE = 464
DISPATCH_TILE = 256
SLOT_CAP = 32
N_SLOTS = N_EXPERTS * SLOT_CAP
SEG_PAD = MOE_TILE + SLOT_CAP
PACKED_WIDTH = D_MODEL // 2
SAMPLE_ATTN_TILE = 16
SAMPLE_PROJ_TILE = 512


def _const_spec(shape):
    nd = len(shape)
    return pl.BlockSpec(shape, lambda *_: (0,) * nd)


def _layer_norm(z, g, b):
    mu = jnp.mean(z, axis=-1, keepdims=True)
    zc = z - mu
    var = jnp.mean(zc * zc, axis=-1, keepdims=True)
    return zc * lax.rsqrt(var + LN_EPS) * g + b


def _sigmoid(x):
    return 1.0 / (1.0 + jnp.exp2(x * -LOG2_E))


def _softplus(x):
    return jnp.maximum(x, 0.0) + jnp.log1p(jnp.exp(-jnp.abs(x)))


def _bdot(a, b):
    return jnp.dot(a.astype(BF16), b.astype(BF16), preferred_element_type=F32)


def _lru_gates(xc, w_gates_ref, b_rg, b_ig, lam):
    xcb = xc.astype(BF16)
    r_parts, i_parts = [], []
    for c in range(N_GATE_CHUNKS):
        g = jnp.dot(xcb[:, c * GATE_CHUNK:(c + 1) * GATE_CHUNK], w_gates_ref[c], preferred_element_type=F32)
        r_parts.append(g[:, :GATE_CHUNK])
        i_parts.append(g[:, GATE_CHUNK:])
    r = _sigmoid(jnp.concatenate(r_parts, axis=1) + b_rg)
    i = _sigmoid(jnp.concatenate(i_parts, axis=1) + b_ig)
    log_a = (-LRU_C * r) * _softplus(-lam)
    a = jnp.exp(log_a)
    gain_sq = 1.0 - a * a
    gain = jnp.where(gain_sq > 0.0, gain_sq * lax.rsqrt(gain_sq), 0.0)
    u = gain * (i * xc)
    return a, u


def _linear_scan(a, u, h_in):
    n, w = a.shape
    groups = n // SUBLANES
    a3 = a.reshape(groups, SUBLANES, w)
    u3 = u.reshape(groups, SUBLANES, w)
    row = lax.broadcasted_iota(jnp.int32, a3.shape, 1)
    d = 1
    while d < SUBLANES:
        has_prev = row >= d
        u3 = u3 + a3 * jnp.where(has_prev, pltpu.roll(u3, d, axis=1), 0.0)
        a3 = a3 * jnp.where(has_prev, pltpu.roll(a3, d, axis=1), 1.0)
        d *= 2
    carry = h_in
    out = []
    for g in range(groups):
        h_g = u3[g] + a3[g] * carry
        out.append(h_g)
        carry = h_g[SUBLANES - 1:SUBLANES, :]
    return jnp.concatenate(out, axis=0)


def _route(x1, w_hi_ref, w_lo_ref, b_rt):
    x_hi = x1.astype(BF16)
    x_lo = (x1 - x_hi.astype(F32)).astype(BF16)
    w_hi = w_hi_ref[...]
    logits = (jnp.dot(x_hi, w_hi, preferred_element_type=F32)
              + (jnp.dot(x_lo, w_hi, preferred_element_type=F32)
                 + jnp.dot(x_hi, w_lo_ref[...], preferred_element_type=F32))) + b_rt
    col = lax.broadcasted_iota(jnp.int32, logits.shape, 1)
    big = jnp.int32(ROUTE_WIDTH)
    is_g = col < N_GROUPS
    gl = jnp.where(is_g, logits, -jnp.inf)
    gmax = jnp.max(gl, axis=-1, keepdims=True)
    g_idx = jnp.min(jnp.where(gl == gmax, col, big), axis=-1, keepdims=True)
    p_g = 1.0 / jnp.sum(jnp.where(is_g, jnp.exp(gl - gmax), 0.0), axis=-1, keepdims=True)
    lo = N_GROUPS + g_idx * EXPERTS_PER_GROUP
    in_grp = (col >= lo) & (col < lo + EXPERTS_PER_GROUP)
    el = jnp.where(in_grp, logits, -jnp.inf)
    v1 = jnp.max(el, axis=-1, keepdims=True)
    i1 = jnp.min(jnp.where(el == v1, col, big), axis=-1, keepdims=True)
    el2 = jnp.where(col == i1, -jnp.inf, el)
    v2 = jnp.max(el2, axis=-1, keepdims=True)
    i2 = jnp.min(jnp.where(el2 == v2, col, big), axis=-1, keepdims=True)
    e21 = jnp.exp(v2 - v1)
    inv = 1.0 / (1.0 + e21)
    w1 = p_g * inv
    w2 = p_g * (e21 * inv)
    e1 = (i1 - N_GROUPS).astype(F32)
    e2 = (i2 - N_GROUPS).astype(F32)
    return jnp.where(col == 0, e1, jnp.where(col == 1, e2, jnp.where(col == 2, w1, jnp.where(col == 3, w2, 0.0))))


def _merge_norm(x, rec, att, g_l, g_a, w_lru_out_ref, w_attn_out_ref, w_o_ref, ln_g, ln_b):
    rec_o = jnp.dot(rec.astype(BF16), w_lru_out_ref[...], preferred_element_type=F32)
    att_o = jnp.dot(att.astype(BF16), w_attn_out_ref[...], preferred_element_type=F32)
    merged = _sigmoid(g_l) * rec_o + _sigmoid(g_a) * att_o
    mix = jnp.dot(merged.astype(BF16), w_o_ref[...], preferred_element_type=F32)
    return _layer_norm(ALPHA * x + mix, ln_g, ln_b)


def _mixer_kernel(sinks_ref, x_ref, w_in_ref, b_in_ref, w_conv_ref, b_conv_ref, w_gates_ref, b_rg_ref, b_ig_ref,
                  lam_ref, w_lru_out_ref, w_attn_out_ref, w_o_ref, ln_g_ref, ln_b_ref, w_rt_hi_ref, w_rt_lo_ref,
                  b_rt_ref, x1_s_ref,
                  x1_ref, route_ref, kwin_ref, vwin_ref, conv_ref, h_ref, xs_hbm, goff_ref, gtot_ref,
                  conv_buf, h_carry, kcat, vcat, att_buf, prev_x1, stage, zero_rows, tri, xb_keep, key_keep,
                  g_vmem, gvec, g_smem, nd_smem, rounds_smem, sem_stage, sem_g, sem_zero,
                  *, tiles_per_seq, n_tiles, seg_cap):
    step = pl.program_id(0)
    last_step = pl.num_programs(0) - 1
    buf = lax.rem(step, 2)
    n_sample = x1_s_ref.shape[0]
    disp = _Dispatcher(xs_hbm, goff_ref, gtot_ref, stage, zero_rows, tri, xb_keep, key_keep, g_vmem, gvec,
                       g_smem, nd_smem, rounds_smem, sem_stage, sem_g, sem_zero, seg_cap)
    n_valid = jnp.where(step == 0, 0, jnp.where(step == last_step, n_sample, SEQ_TILE))

    @pl.when(step == 0)
    def _():
        disp.init()
        prev_x1[...] = jnp.zeros_like(prev_x1)

    def route_and_plan():
        x1_prev = prev_x1[...]
        route = _route(x1_prev, w_rt_hi_ref, w_rt_lo_ref, b_rt_ref[...])
        route_ref[...] = route
        disp.plan(x1_prev, route, n_valid, buf)

    @pl.when(step < n_tiles)
    def _():
        @pl.when(lax.rem(step, tiles_per_seq) == 0)
        def _():
            conv_buf[...] = jnp.zeros_like(conv_buf)
            h_carry[...] = jnp.zeros_like(h_carry)
            kcat[0:WINDOW, :] = jnp.zeros((WINDOW, KV_WIDTH), BF16)
            vcat[0:WINDOW, :] = jnp.zeros((WINDOW, KV_WIDTH), BF16)

        route_and_plan()
        _mixer_tile(lax.rem(step, tiles_per_seq), sinks_ref, x_ref, w_in_ref, b_in_ref, w_conv_ref, b_conv_ref,
                    w_gates_ref, b_rg_ref, b_ig_ref, lam_ref, w_lru_out_ref, w_attn_out_ref, w_o_ref, ln_g_ref,
                    ln_b_ref, x1_ref, prev_x1, kwin_ref, vwin_ref, conv_ref, h_ref,
                    conv_buf, h_carry, kcat, vcat, att_buf)

    @pl.when(step >= n_tiles)
    def _():
        route_and_plan()

        @pl.when(step == n_tiles)
        def _():
            x1_ref[0:n_sample, :] = x1_s_ref[...]
            prev_x1[0:n_sample, :] = x1_s_ref[...]

    disp.flush(buf, step == last_step)


def _mixer_tile(t, sinks_ref, x_ref, w_in_ref, b_in_ref, w_conv_ref, b_conv_ref, w_gates_ref, b_rg_ref, b_ig_ref,
                lam_ref, w_lru_out_ref, w_attn_out_ref, w_o_ref, ln_g_ref, ln_b_ref,
                x1_ref, x1_keep, kwin_ref, vwin_ref, conv_ref, h_ref,
                conv_buf, h_carry, kcat, vcat, att_buf):
    T = SEQ_TILE
    x = x_ref[0]
    xb = x.astype(BF16)

    def proj(lo, width):
        return jnp.dot(xb, w_in_ref[:, lo:lo + width], preferred_element_type=F32) + b_in_ref[:, lo:lo + width]

    xl = proj(OFF_XL, LRU_WIDTH)
    xl_ext = jnp.concatenate([conv_buf[...], xl], axis=0)

    def lagged(k):
        return pltpu.roll(xl_ext, k, axis=0)[SUBLANES:, :]
    wc = w_conv_ref[...]
    xc = wc[0:1] * lagged(3)
    xc = xc + wc[1:2] * lagged(2)
    xc = xc + wc[2:3] * lagged(1)
    xc = xc + wc[3:4] * xl + b_conv_ref[...]
    conv_ref[0] = xl[T - (CONV_W - 1):, :]
    conv_buf[...] = xl[T - SUBLANES:, :]

    a, u = _lru_gates(xc, w_gates_ref, b_rg_ref[...], b_ig_ref[...], lam_ref[...])
    h = _linear_scan(a, u, h_carry[0:1, :])
    h_last = h[T - 1:T, :]
    h_carry[0:1, :] = h_last
    h_ref[0] = h_last
    rec = h * jax.nn.gelu(proj(OFF_YL, LRU_WIDTH))

    q = proj(OFF_Q, N_HEADS * HEAD_DIM) * (ATTN_SCALE * LOG2_E)
    k = proj(OFF_K, KV_WIDTH)
    v = proj(OFF_V, KV_WIDTH)
    kwin_ref[0] = k[T - WINDOW:, :]
    vwin_ref[0] = v[T - WINDOW:, :]
    kcat[WINDOW:WINDOW + T, :] = k.astype(BF16)
    vcat[WINDOW:WINDOW + T, :] = v.astype(BF16)

    qi = lax.broadcasted_iota(jnp.int32, (WINDOW, 2 * WINDOW), 0)
    kj = lax.broadcasted_iota(jnp.int32, (WINDOW, 2 * WINDOW), 1)
    band = (kj > qi) & (kj <= qi + WINDOW)
    grp_row = lax.broadcasted_iota(jnp.int32, (GROUP * WINDOW, 1), 0) // WINDOW
    for qb in range(T // WINDOW):
        if qb == 0:
            first_key = jnp.where(t == 0, WINDOW, 0)
            mask1 = band & (kj >= first_key)
        else:
            mask1 = band
        bias = jnp.concatenate([jnp.where(mask1, 0.0, NEG_INF)] * GROUP, axis=0)
        r0 = qb * WINDOW
        qq = q[r0:r0 + WINDOW, :]
        for j in range(N_KV):
            kjb = kcat[r0:r0 + 2 * WINDOW, j * HEAD_DIM:(j + 1) * HEAD_DIM]
            vjb = vcat[r0:r0 + 2 * WINDOW, j * HEAD_DIM:(j + 1) * HEAD_DIM]
            qs = jnp.concatenate(
                [qq[:, (j * GROUP + g) * HEAD_DIM:(j * GROUP + g + 1) * HEAD_DIM] for g in range(GROUP)], axis=0)
            s = lax.dot_general(qs.astype(BF16), kjb, (((1,), (1,)), ((), ())), preferred_element_type=F32) + bias
            sink = jnp.zeros((GROUP * WINDOW, 1), F32)
            for g in range(GROUP):
                sink = jnp.where(grp_row == g, sinks_ref[j * GROUP + g] * LOG2_E, sink)
            m = jnp.maximum(jnp.max(s, axis=-1, keepdims=True), sink)
            p = jnp.exp2(s - m)
            inv = 1.0 / (jnp.sum(p, axis=-1, keepdims=True) + jnp.exp2(sink - m))
            o = jnp.dot((p * inv).astype(BF16), vjb, preferred_element_type=F32)
            for g in range(GROUP):
                hcol = (j * GROUP + g) * HEAD_DIM
                att_buf[r0:r0 + WINDOW, hcol:hcol + HEAD_DIM] = o[g * WINDOW:(g + 1) * WINDOW, :]
    kcat[0:WINDOW, :] = kcat[T:T + WINDOW, :]
    vcat[0:WINDOW, :] = vcat[T:T + WINDOW, :]

    x1 = _merge_norm(x, rec, att_buf[...], proj(OFF_GL, D_MODEL), proj(OFF_GA, D_MODEL),
                     w_lru_out_ref, w_attn_out_ref, w_o_ref, ln_g_ref[...], ln_b_ref[...])
    x1_ref[...] = x1
    x1_keep[...] = x1


def _mixer_call(x_prompt, x1_s, sinks, wts):
    B, S, _ = x_prompt.shape
    T = SEQ_TILE
    assert T == DISPATCH_TILE and x1_s.shape[0] <= T
    nt = S // T
    n_tiles = B * nt
    n_rows_total = B * S + x1_s.shape[0]
    n_dispatch = _dispatch_tiles(n_rows_total)
    assert n_dispatch == n_tiles + 1
    seg_cap = _segment_capacity(n_rows_total)
    n_assign = TOP_K * T
    weight_args = (wts["w_in"], wts["b_in"], wts["w_conv"], wts["b_conv"], wts["w_gates"], wts["b_rg"], wts["b_ig"],
                   wts["lam"], wts["w_lru_out"], wts["w_attn_out"], wts["w_o"], wts["ln1_g"], wts["ln1_b"],
                   wts["w_rt_hi"], wts["w_rt_lo"], wts["b_rt"], x1_s)
    mixed = lambda i: jnp.minimum(i, n_tiles - 1)
    seq = lambda i: mixed(i) // nt
    routed = lambda i: jnp.clip(i - 1, 0, n_dispatch - 1)
    in_specs = [pl.BlockSpec(memory_space=pltpu.SMEM),
                pl.BlockSpec((1, T, D_MODEL), lambda i: (seq(i), lax.rem(mixed(i), nt), 0))]
    in_specs += [_const_spec(w.shape) for w in weight_args]
    out_shape = (
        jax.ShapeDtypeStruct((n_rows_total, D_MODEL), F32),
        jax.ShapeDtypeStruct((n_rows_total, ROUTE_WIDTH), F32),
        jax.ShapeDtypeStruct((B, WINDOW, KV_WIDTH), F32),
        jax.ShapeDtypeStruct((B, WINDOW, KV_WIDTH), F32),
        jax.ShapeDtypeStruct((B, CONV_W - 1, LRU_WIDTH), F32),
        jax.ShapeDtypeStruct((B, 1, LRU_WIDTH), F32),
        jax.ShapeDtypeStruct((N_EXPERTS * seg_cap, PACKED_WIDTH), jnp.uint32),
        jax.ShapeDtypeStruct((n_dispatch, 1, ROUTE_WIDTH), jnp.int32),
        jax.ShapeDtypeStruct((1, ROUTE_WIDTH), jnp.int32),
    )
    out_specs = (
        pl.BlockSpec((T, D_MODEL), lambda i: (jnp.minimum(i, n_tiles), 0)),
        pl.BlockSpec((T, ROUTE_WIDTH), lambda i: (routed(i), 0)),
        pl.BlockSpec((1, WINDOW, KV_WIDTH), lambda i: (seq(i), 0, 0)),
        pl.BlockSpec((1, WINDOW, KV_WIDTH), lambda i: (seq(i), 0, 0)),
        pl.BlockSpec((1, CONV_W - 1, LRU_WIDTH), lambda i: (seq(i), 0, 0)),
        pl.BlockSpec((1, 1, LRU_WIDTH), lambda i: (seq(i), 0, 0)),
        pl.BlockSpec(memory_space=pl.ANY),
        pl.BlockSpec((1, 1, ROUTE_WIDTH), lambda i: (routed(i), 0, 0)),
        pl.BlockSpec((1, ROUTE_WIDTH), lambda i: (0, 0)),
    )
    scratch = [
        pltpu.VMEM((SUBLANES, LRU_WIDTH), F32),
        pltpu.VMEM((SUBLANES, LRU_WIDTH), F32),
        pltpu.VMEM((T + WINDOW, KV_WIDTH), BF16),
        pltpu.VMEM((T + WINDOW, KV_WIDTH), BF16),
        pltpu.VMEM((T, N_HEADS * HEAD_DIM), F32),
        pltpu.VMEM((T, D_MODEL), F32),
        pltpu.VMEM((2, N_SLOTS, PACKED_WIDTH), jnp.uint32),
        pltpu.VMEM((SEG_PAD, PACKED_WIDTH), jnp.uint32),
        pltpu.VMEM((n_assign, n_assign), BF16),
        pltpu.VMEM((T, D_MODEL), BF16),
        pltpu.VMEM((SUBLANES, n_assign), F32),
        pltpu.VMEM((SUBLANES, ROUTE_WIDTH), jnp.int32),
        pltpu.VMEM((SUBLANES, ROUTE_WIDTH), F32),
        pltpu.SMEM((1, ROUTE_WIDTH), jnp.int32),
        pltpu.SMEM((1,), jnp.int32),
        pltpu.SMEM((1,), jnp.int32),
        pltpu.SemaphoreType.DMA(()),
        pltpu.SemaphoreType.DMA(()),
        pltpu.SemaphoreType.DMA(()),
    ]
    return pl.pallas_call(
        functools.partial(_mixer_kernel, tiles_per_seq=nt, n_tiles=n_tiles, seg_cap=seg_cap),
        grid=(n_tiles + 2,),
        in_specs=in_specs,
        out_specs=out_specs,
        out_shape=out_shape,
        scratch_shapes=scratch,
        compiler_params=pltpu.CompilerParams(dimension_semantics=("arbitrary",),
                                             vmem_limit_bytes=VMEM_LIMIT_BYTES),
        name="mixer_prompt",
    )(sinks, x_prompt, *weight_args)


def _sample_proj_kernel(x_ref, w_ref, b_ref, u_ref):
    u_ref[...] = jnp.dot(x_ref[...].astype(BF16), w_ref[...], preferred_element_type=F32) + b_ref[...]


def _sample_proj_call(x_s, w_in, b_in):
    n = x_s.shape[0]
    tn = SAMPLE_PROJ_TILE
    return pl.pallas_call(
        _sample_proj_kernel,
        grid=(IN_WIDTH // tn,),
        in_specs=[pl.BlockSpec((n, D_MODEL), lambda c: (0, 0)),
                  pl.BlockSpec((D_MODEL, tn), lambda c: (0, c)),
                  pl.BlockSpec((1, tn), lambda c: (0, c))],
        out_specs=pl.BlockSpec((n, tn), lambda c: (0, c)),
        out_shape=jax.ShapeDtypeStruct((n, IN_WIDTH), F32),
        compiler_params=pltpu.CompilerParams(dimension_semantics=("arbitrary",)),
        name="sample_proj",
    )(x_s, w_in, b_in)


def _sample_attn_kernel(q_ref, kn_ref, vn_ref, kn_row_ref, vn_row_ref, ck_ref, cv_ref, sinks_ref,
                        att_ref, kwin_ref, vwin_ref):
    tb = q_ref.shape[0]
    key_pos = lax.broadcasted_iota(jnp.int32, (tb, GROUP, WINDOW), 2)
    for j in range(N_KV):
        qj = q_ref[:, j * GROUP:(j + 1) * GROUP, :]
        kc = ck_ref[:, :, j * HEAD_DIM:(j + 1) * HEAD_DIM]
        vc = cv_ref[:, :, j * HEAD_DIM:(j + 1) * HEAD_DIM]
        s_c = jnp.einsum("bgd,bsd->bgs", qj.astype(BF16), kc.astype(BF16), preferred_element_type=F32) * ATTN_SCALE
        s_c = jnp.where(key_pos >= 1, s_c, NEG_INF)
        kn = kn_ref[:, j:j + 1, :]
        vn = vn_ref[:, j:j + 1, :]
        s_n = jnp.sum(qj * kn, axis=-1, keepdims=True) * ATTN_SCALE
        sink = sinks_ref[j][None]
        m = jnp.maximum(jnp.maximum(jnp.max(s_c, axis=-1, keepdims=True), s_n), sink)
        p_c = jnp.exp(s_c - m)
        p_n = jnp.exp(s_n - m)
        inv = 1.0 / (jnp.sum(p_c, axis=-1, keepdims=True) + p_n + jnp.exp(sink - m))
        o = jnp.einsum("bgs,bsd->bgd", (p_c * inv).astype(BF16), vc.astype(BF16), preferred_element_type=F32)
        att_ref[:, j * GROUP:(j + 1) * GROUP, :] = o + (p_n * inv) * vn
    kwin_ref[:, 0:WINDOW - 1, :] = ck_ref[:, 1:WINDOW, :]
    kwin_ref[:, WINDOW - 1:WINDOW, :] = kn_row_ref[...]
    vwin_ref[:, 0:WINDOW - 1, :] = cv_ref[:, 1:WINDOW, :]
    vwin_ref[:, WINDOW - 1:WINDOW, :] = vn_row_ref[...]


def _sample_attn_call(q3, kn3, vn3, kn_row, vn_row, ck, cv, sinks3):
    n = q3.shape[0]
    tb = SAMPLE_ATTN_TILE
    b3 = lambda i: (i, 0, 0)
    return pl.pallas_call(
        _sample_attn_kernel,
        grid=(n // tb,),
        in_specs=[pl.BlockSpec((tb, N_HEADS, HEAD_DIM), b3),
                  pl.BlockSpec((tb, N_KV, HEAD_DIM), b3),
                  pl.BlockSpec((tb, N_KV, HEAD_DIM), b3),
                  pl.BlockSpec((tb, 1, KV_WIDTH), b3),
                  pl.BlockSpec((tb, 1, KV_WIDTH), b3),
                  pl.BlockSpec((tb, WINDOW, KV_WIDTH), b3),
                  pl.BlockSpec((tb, WINDOW, KV_WIDTH), b3),
                  pl.BlockSpec((N_KV, GROUP, 1), lambda i: (0, 0, 0))],
        out_specs=(pl.BlockSpec((tb, N_HEADS, HEAD_DIM), b3),
                   pl.BlockSpec((tb, WINDOW, KV_WIDTH), b3),
                   pl.BlockSpec((tb, WINDOW, KV_WIDTH), b3)),
        out_shape=(jax.ShapeDtypeStruct((n, N_HEADS, HEAD_DIM), F32),
                   jax.ShapeDtypeStruct((n, WINDOW, KV_WIDTH), F32),
                   jax.ShapeDtypeStruct((n, WINDOW, KV_WIDTH), F32)),
        compiler_params=pltpu.CompilerParams(dimension_semantics=("arbitrary",)),
        name="sample_attn",
    )(q3, kn3, vn3, kn_row, vn_row, ck, cv, sinks3)


def _sample_mix_kernel(x_ref, u_ref, att_ref, st_ref, h0_ref, w_conv_ref, b_conv_ref, w_gates_ref, b_rg_ref,
                       b_ig_ref, lam_ref, w_lru_out_ref, w_attn_out_ref, w_o_ref, ln_g_ref, ln_b_ref,
                       x1_ref, conv_ref, h_ref):
    xl = u_ref[:, OFF_XL:OFF_XL + LRU_WIDTH]
    wc = w_conv_ref[...]
    xc = wc[0:1] * st_ref[0]
    xc = xc + wc[1:2] * st_ref[1]
    xc = xc + wc[2:3] * st_ref[2]
    xc = xc + wc[3:4] * xl + b_conv_ref[...]
    conv_ref[0] = st_ref[1]
    conv_ref[1] = st_ref[2]
    conv_ref[2] = xl
    a, u = _lru_gates(xc, w_gates_ref, b_rg_ref[...], b_ig_ref[...], lam_ref[...])
    h = a * h0_ref[...] + u
    h_ref[...] = h
    rec = h * jax.nn.gelu(u_ref[:, OFF_YL:OFF_YL + LRU_WIDTH])
    x1_ref[...] = _merge_norm(x_ref[...], rec, att_ref[...], u_ref[:, OFF_GL:OFF_GL + D_MODEL],
                              u_ref[:, OFF_GA:OFF_GA + D_MODEL], w_lru_out_ref, w_attn_out_ref, w_o_ref,
                              ln_g_ref[...], ln_b_ref[...])


def _sample_mix_call(x_s, u_s, att, st_t, h0, wts):
    n = x_s.shape[0]
    weight_args = (wts["w_conv"], wts["b_conv"], wts["w_gates"], wts["b_rg"], wts["b_ig"], wts["lam"],
                   wts["w_lru_out"], wts["w_attn_out"], wts["w_o"], wts["ln1_g"], wts["ln1_b"])
    args = (x_s, u_s, att, st_t, h0) + weight_args
    out_shapes = ((n, D_MODEL), (CONV_W - 1, n, LRU_WIDTH), (n, LRU_WIDTH))
    return pl.pallas_call(
        _sample_mix_kernel,
        grid=(1,),
        in_specs=[_const_spec(a.shape) for a in args],
        out_specs=tuple(_const_spec(s) for s in out_shapes),
        out_shape=tuple(jax.ShapeDtypeStruct(s, F32) for s in out_shapes),
        compiler_params=pltpu.CompilerParams(dimension_semantics=("arbitrary",),
                                             vmem_limit_bytes=VMEM_LIMIT_BYTES),
        name="sample_mix",
    )(*args)


def _one_hot(mask):
    return jnp.where(mask, 1.0, 0.0).astype(BF16)


def _pack_rows(x):
    half = x.shape[1] // 2
    lo = lax.shift_right_logical(lax.bitcast_convert_type(x[:, :half], jnp.uint32), jnp.uint32(16))
    hi = lax.bitcast_convert_type(x[:, half:], jnp.uint32) & jnp.uint32(0xFFFF0000)
    return lo | hi


def _unpack_rows(words):
    lo = lax.bitcast_convert_type(lax.shift_left(words, jnp.uint32(16)), F32)
    hi = lax.bitcast_convert_type(words & jnp.uint32(0xFFFF0000), F32)
    return jnp.concatenate([lo.astype(BF16), hi.astype(BF16)], axis=1)


class _Dispatcher:
    def __init__(self, xs_hbm, goff_ref, gtot_ref, stage, zero_rows, tri, xb_keep, key_keep, g_vmem, gvec,
                 g_smem, nd_smem, rounds_smem, sem_stage, sem_g, sem_zero, seg_cap):
        self.xs_hbm, self.goff_ref, self.gtot_ref = xs_hbm, goff_ref, gtot_ref
        self.stage, self.zero_rows, self.tri = stage, zero_rows, tri
        self.xb_keep, self.key_keep = xb_keep, key_keep
        self.g_vmem, self.gvec, self.g_smem = g_vmem, gvec, g_smem
        self.nd_smem, self.rounds_smem = nd_smem, rounds_smem
        self.sem_stage, self.sem_g, self.sem_zero = sem_stage, sem_g, sem_zero
        self.seg_cap = seg_cap

    def _g_copy(self):
        return pltpu.make_async_copy(self.g_vmem.at[pl.ds(0, 1)], self.g_smem, self.sem_g)

    def init(self):
        n_assign = self.tri.shape[0]
        r = lax.broadcasted_iota(jnp.int32, (n_assign, n_assign), 0)
        c = lax.broadcasted_iota(jnp.int32, (n_assign, n_assign), 1)
        self.tri[...] = _one_hot(r < c)
        self.gvec[...] = jnp.zeros_like(self.gvec)
        self.g_vmem[...] = jnp.zeros_like(self.g_vmem)
        self.nd_smem[0] = 0
        self._g_copy().start()

    def _sorted_rows(self, e_row, rank_row, xb, lo):
        td = xb.shape[0]
        slot = lax.broadcasted_iota(jnp.int32, (N_SLOTS, td), 0).astype(F32)
        lo_f = lo.astype(F32)
        in_round = (rank_row >= lo_f) & (rank_row < lo_f + SLOT_CAP) & (e_row >= 0.0)
        key = jnp.where(in_round, e_row * SLOT_CAP + (rank_row - lo_f), -1.0)
        perm = jnp.where(slot == key[:, :td], 1.0, jnp.where(slot == key[:, td:], 1.0, 0.0)).astype(BF16)
        return _pack_rows(jnp.dot(perm, xb, preferred_element_type=F32))

    def plan(self, x1, route, n_valid, buf):
        td = x1.shape[0]
        n_assign = TOP_K * td
        valid_col = lax.broadcasted_iota(jnp.int32, (td, 1), 0) < n_valid
        lane = lax.broadcasted_iota(jnp.int32, (td, ROUTE_WIDTH), 1).astype(F32)
        e0c = jnp.where(valid_col, route[:, 0:1], -1.0)
        e1c = jnp.where(valid_col, route[:, 1:2], -1.0)
        cnt_row = jnp.sum(jnp.where(lane == e0c, 1.0, 0.0) + jnp.where(lane == e1c, 1.0, 0.0),
                          axis=0, keepdims=True)
        self.rounds_smem[0] = (jnp.max(cnt_row).astype(jnp.int32) + SLOT_CAP - 1) // SLOT_CAP
        self.goff_ref[0] = self.gvec[0:1, :].astype(jnp.int32)
        self.gvec[0:1, :] = self.gvec[0:1, :] + jnp.ceil(cnt_row * (1.0 / SUBLANES)) * SUBLANES
        self.gtot_ref[...] = self.gvec[0:1, :].astype(jnp.int32)

        route_t = route.T
        valid_row = lax.broadcasted_iota(jnp.int32, (1, td), 1) < n_valid
        e_row = jnp.concatenate([jnp.where(valid_row, route_t[0:1, :], -1.0),
                                 jnp.where(valid_row, route_t[1:2, :], -1.0)], axis=1)
        expert_sub = lax.broadcasted_iota(jnp.int32, (N_EXPERTS, n_assign), 0).astype(F32)
        onehot_t = expert_sub == e_row
        rank_t = jnp.dot(_one_hot(onehot_t), self.tri[...], preferred_element_type=F32)
        rank_row = jnp.sum(jnp.where(onehot_t, rank_t, 0.0), axis=0, keepdims=True)
        xb = jnp.where(valid_col, x1, 0.0).astype(BF16)
        self.xb_keep[...] = xb
        self.key_keep[0:1, :] = e_row
        self.key_keep[1:2, :] = rank_row
        self.stage[buf] = self._sorted_rows(e_row, rank_row, xb, jnp.int32(0))

    def _stage_copy(self, buf, e, dst_row):
        src = self.stage.at[buf, pl.ds(pl.multiple_of(e * SLOT_CAP, SLOT_CAP), SLOT_CAP)]
        return pltpu.make_async_copy(src, self.xs_hbm.at[pl.ds(dst_row, SLOT_CAP)], self.sem_stage)

    def _wait_outstanding(self):
        def wait_one(_, carry):
            self._stage_copy(0, 0, 0).wait()
            return carry
        lax.fori_loop(0, self.nd_smem[0], wait_one, 0)
        self.nd_smem[0] = 0

    def _start_round(self, buf, lo):
        def issue(e, carry):
            dst_row = pl.multiple_of(e * self.seg_cap + self.g_smem[0, e] + lo, SUBLANES)
            self._stage_copy(buf, e, dst_row).start()
            return carry
        lax.fori_loop(0, N_EXPERTS, issue, 0)
        self.nd_smem[0] = N_EXPERTS

    def flush(self, buf, is_last):
        self._g_copy().wait()
        self._wait_outstanding()
        self._start_round(buf, 0)

        def later_round(r, carry):
            lo = r * SLOT_CAP
            rows = self._sorted_rows(self.key_keep[0:1, :], self.key_keep[1:2, :], self.xb_keep[...], lo)
            self._wait_outstanding()
            self.stage[buf] = rows
            self._start_round(buf, lo)
            return carry
        lax.fori_loop(1, self.rounds_smem[0], later_round, 0)

        self.g_vmem[0:1, :] = self.gvec[0:1, :].astype(jnp.int32)
        self._g_copy().start()

        @pl.when(is_last)
        def _():
            self._g_copy().wait()
            self._wait_outstanding()
            self.zero_rows[...] = jnp.zeros_like(self.zero_rows)

            def pad_copy(e):
                dst_row = pl.multiple_of(e * self.seg_cap + self.g_smem[0, e], SUBLANES)
                return pltpu.make_async_copy(self.zero_rows, self.xs_hbm.at[pl.ds(dst_row, SEG_PAD)], self.sem_zero)

            def start(e, carry):
                pad_copy(e).start()
                return carry
            lax.fori_loop(0, N_EXPERTS, start, 0)

            def wait(e, carry):
                pad_copy(e).wait()
                return carry
            lax.fori_loop(0, N_EXPERTS, wait, 0)


def _dispatch_tiles(n_tokens):
    return (n_tokens + DISPATCH_TILE - 1) // DISPATCH_TILE


def _segment_capacity(n_tokens):
    alignment_slack = (SUBLANES - 1) * _dispatch_tiles(n_tokens)
    return (n_tokens + alignment_slack + SEG_PAD + MOE_TILE - 1) // MOE_TILE * MOE_TILE


def _moe_kernel(blk_e_ref, blk_j_ref, n_used_ref, xs_ref, wg_ref, wu_ref, wd_ref, ys_ref, wg_bf, wu_bf, wd_bf):
    del blk_j_ref
    b = pl.program_id(0)

    @pl.when(b < n_used_ref[0])
    def _():
        prev_e = blk_e_ref[jnp.maximum(b - 1, 0)]

        @pl.when((b == 0) | (blk_e_ref[b] != prev_e))
        def _():
            wg_bf[...] = wg_ref[0].astype(BF16)
            wu_bf[...] = wu_ref[0].astype(BF16)
            wd_bf[...] = wd_ref[0].astype(BF16)

        xb = _unpack_rows(xs_ref[...])
        g = jnp.dot(xb, wg_bf[...], preferred_element_type=F32)
        u = jnp.dot(xb, wu_bf[...], preferred_element_type=F32)
        hmid = (g * _sigmoid(g)) * u
        y = jnp.dot(hmid.astype(BF16), wd_bf[...], preferred_element_type=F32)
        ys_ref[...] = _pack_rows(y.astype(BF16).astype(F32))


def _expert_blocks(gtot, n_blocks):
    rows = gtot[0, :N_EXPERTS]
    nb = (rows + SLOT_CAP + MOE_TILE - 1) // MOE_TILE
    ends = jnp.cumsum(nb)
    n_used = ends[-1]
    step = jnp.minimum(jnp.arange(n_blocks, dtype=jnp.int32), n_used - 1)
    blk_e = jnp.sum((step[:, None] >= ends[None, :]).astype(jnp.int32), axis=1)
    blk_j = step - (ends - nb)[blk_e]
    return blk_e, blk_j.astype(jnp.int32), n_used.reshape(1).astype(jnp.int32)


def _moe_call(gtot, xs, w_gate, w_up, w_down, n_tokens):
    seg_cap = _segment_capacity(n_tokens)
    seg_blocks = seg_cap // MOE_TILE
    max_rows = n_tokens * TOP_K + N_EXPERTS * (SUBLANES - 1) * _dispatch_tiles(n_tokens)
    n_blocks = (max_rows + N_EXPERTS * (SLOT_CAP + MOE_TILE - 1)) // MOE_TILE
    blk_e, blk_j, n_used = _expert_blocks(gtot, n_blocks)
    row_block = lambda b, be, bj, nu: (be[b] * seg_blocks + bj[b], 0)
    weight_block = lambda b, be, bj, nu: (be[b], 0, 0)
    grid_spec = pltpu.PrefetchScalarGridSpec(
        num_scalar_prefetch=3,
        grid=(n_blocks,),
        in_specs=[
            pl.BlockSpec((MOE_TILE, PACKED_WIDTH), row_block),
            pl.BlockSpec((1, D_MODEL, EXPERT_FF), weight_block),
            pl.BlockSpec((1, D_MODEL, EXPERT_FF), weight_block),
            pl.BlockSpec((1, EXPERT_FF, D_MODEL), weight_block),
        ],
        out_specs=pl.BlockSpec((MOE_TILE, PACKED_WIDTH), row_block),
        scratch_shapes=[
            pltpu.VMEM((D_MODEL, EXPERT_FF), BF16),
            pltpu.VMEM((D_MODEL, EXPERT_FF), BF16),
            pltpu.VMEM((EXPERT_FF, D_MODEL), BF16),
        ],
    )
    return pl.pallas_call(
        _moe_kernel,
        grid_spec=grid_spec,
        out_shape=jax.ShapeDtypeStruct(xs.shape, xs.dtype),
        compiler_params=pltpu.CompilerParams(dimension_semantics=("arbitrary",),
                                             vmem_limit_bytes=VMEM_LIMIT_BYTES),
        name="moe_experts",
    )(blk_e, blk_j, n_used, xs, w_gate, w_up, w_down)


def _combine_kernel(gcur_ref, gnext_ref, x1_ref, route_ref, ys_hbm, ln_g_ref, ln_b_ref, y_ref,
                    stage, acc, tri, sem, *, seg_cap):
    td = x1_ref.shape[0]
    n_assign = TOP_K * td
    i = pl.program_id(0)
    buf = lax.rem(i, 2)

    def stage_copy(g_ref, e, first_rank, to_buf):
        src_row = pl.multiple_of(e * seg_cap + g_ref[0, 0, e] + first_rank, SUBLANES)
        dst = stage.at[to_buf, pl.ds(pl.multiple_of(e * SLOT_CAP, SLOT_CAP), SLOT_CAP)]
        return pltpu.make_async_copy(ys_hbm.at[pl.ds(src_row, SLOT_CAP)], dst, sem.at[to_buf])

    def fetch_first_round(g_ref, to_buf):
        def start(e, carry):
            stage_copy(g_ref, e, 0, to_buf).start()
            return carry
        lax.fori_loop(0, N_EXPERTS, start, 0)

    @pl.when(i == 0)
    def _():
        r = lax.broadcasted_iota(jnp.int32, (n_assign, n_assign), 0)
        c = lax.broadcasted_iota(jnp.int32, (n_assign, n_assign), 1)
        tri[...] = _one_hot(c < r)
        fetch_first_round(gcur_ref, 0)

    def count(e):
        return gnext_ref[0, 0, e] - gcur_ref[0, 0, e]

    n_max = lax.fori_loop(0, N_EXPERTS, lambda e, m: jnp.maximum(m, count(e)), 0)
    rounds = (n_max + SLOT_CAP - 1) // SLOT_CAP

    route = route_ref[...]
    e0, e1, w0, w1 = route[:, 0:1], route[:, 1:2], route[:, 2:3], route[:, 3:4]
    lane = lax.broadcasted_iota(jnp.int32, (td, ROUTE_WIDTH), 1).astype(F32)
    onehot = jnp.concatenate([lane == e0, lane == e1], axis=0)
    onehot_f = jnp.where(onehot, 1.0, 0.0)
    rank_mat = jnp.dot(tri[...], onehot_f.astype(BF16), preferred_element_type=F32)
    rank = jnp.sum(jnp.where(onehot, rank_mat, 0.0), axis=1, keepdims=True)
    r0, r1 = rank[:td], rank[td:]
    cnt_row = jnp.sum(onehot_f, axis=0, keepdims=True)

    slot_col = lax.broadcasted_iota(jnp.int32, (N_SLOTS, 1), 0)
    slot_expert = slot_col // SLOT_CAP
    slot_rank = slot_col - slot_expert * SLOT_CAP
    lane_s = lax.broadcasted_iota(jnp.int32, (N_SLOTS, ROUTE_WIDTH), 1)
    n_col = jnp.sum(jnp.where(lane_s == slot_expert, cnt_row, 0.0), axis=1, keepdims=True)
    slot_lane = lax.broadcasted_iota(jnp.int32, (td, N_SLOTS), 1).astype(F32)

    def wait_copies(n):
        def wait_one(_, c):
            stage_copy(gcur_ref, 0, 0, buf).wait()
            return c
        lax.fori_loop(0, n, wait_one, 0)

    def weighted_rows(lo):
        lo_f = lo.astype(F32)
        live = (slot_rank + lo).astype(F32) < n_col
        rows = _unpack_rows(jnp.where(live, stage[buf], jnp.uint32(0)))

        def selector(e_col, r_col, w_col):
            in_round = (r_col >= lo_f) & (r_col < lo_f + SLOT_CAP)
            key = jnp.where(in_round, e_col * SLOT_CAP + (r_col - lo_f), -1.0)
            return jnp.where(slot_lane == key, w_col, 0.0)
        mix = (selector(e0, r0, w0) + selector(e1, r1, w1)).astype(BF16)
        return jnp.dot(mix, rows, preferred_element_type=F32)

    wait_copies(N_EXPERTS)

    @pl.when(i + 1 < pl.num_programs(0))
    def _():
        fetch_first_round(gnext_ref, 1 - buf)

    acc[...] = weighted_rows(jnp.int32(0))

    def later_round(r, carry):
        lo = r * SLOT_CAP

        def start(e, n_started):
            has_rows = count(e) > lo

            @pl.when(has_rows)
            def _():
                stage_copy(gcur_ref, e, lo, buf).start()
            return n_started + has_rows.astype(jnp.int32)
        wait_copies(lax.fori_loop(0, N_EXPERTS, start, 0))
        acc[...] = acc[...] + weighted_rows(lo)
        return carry
    lax.fori_loop(1, rounds, later_round, 0)

    y_ref[...] = _layer_norm(ALPHA * x1_ref[...] + acc[...], ln_g_ref[...], ln_b_ref[...])


def _combine_call(goff, x1_all, route_all, ys, ln_g, ln_b, first_row, n_rows, tile, n_tokens):
    assert first_row % DISPATCH_TILE == 0 and first_row % tile == 0
    assert tile == DISPATCH_TILE or n_rows == tile
    first_block = first_row // tile
    first_goff = first_row // DISPATCH_TILE
    n_assign = TOP_K * tile
    grid_spec = pl.GridSpec(
        grid=(n_rows // tile,),
        in_specs=[
            pl.BlockSpec((1, 1, ROUTE_WIDTH), lambda i: (first_goff + i, 0, 0), memory_space=pltpu.SMEM),
            pl.BlockSpec((1, 1, ROUTE_WIDTH), lambda i: (first_goff + i + 1, 0, 0), memory_space=pltpu.SMEM),
            pl.BlockSpec((tile, D_MODEL), lambda i: (first_block + i, 0)),
            pl.BlockSpec((tile, ROUTE_WIDTH), lambda i: (first_block + i, 0)),
            pl.BlockSpec(memory_space=pl.ANY),
            _const_spec(ln_g.shape),
            _const_spec(ln_b.shape),
        ],
        out_specs=pl.BlockSpec((tile, D_MODEL), lambda i: (i, 0)),
        scratch_shapes=[pltpu.VMEM((2, N_SLOTS, PACKED_WIDTH), jnp.uint32),
                        pltpu.VMEM((tile, D_MODEL), F32),
                        pltpu.VMEM((n_assign, n_assign), BF16),
                        pltpu.SemaphoreType.DMA((2,))],
    )
    return pl.pallas_call(
        functools.partial(_combine_kernel, seg_cap=_segment_capacity(n_tokens)),
        grid_spec=grid_spec,
        out_shape=jax.ShapeDtypeStruct((n_rows, D_MODEL), F32),
        compiler_params=pltpu.CompilerParams(dimension_semantics=("arbitrary",),
                                             vmem_limit_bytes=VMEM_LIMIT_BYTES),
        name="moe_combine",
    )(goff, goff, x1_all, route_all, ys, ln_g, ln_b)


def _prepare_weights(w_in, b_in, w_conv, b_conv, w_rg, b_rg, w_ig, b_ig, lru_lambda, w_lru_out, w_attn_out, w_o,
                     ln1_g, ln1_b, w_group, b_group, w_router, b_router):
    blocks_per_chunk = GATE_CHUNK // LRU_BLOCK

    def chunked_block_diag(w):
        w = w.reshape(N_GATE_CHUNKS, blocks_per_chunk, LRU_BLOCK, LRU_BLOCK)
        eye = jnp.eye(blocks_per_chunk, dtype=w.dtype)
        return jnp.einsum("cbij,bd->cbidj", w, eye).reshape(N_GATE_CHUNKS, GATE_CHUNK, GATE_CHUNK)

    w_gates = jnp.concatenate([chunked_block_diag(w_rg), chunked_block_diag(w_ig)], axis=-1).astype(BF16)
    w_rt = jnp.concatenate([w_group, w_router], axis=1)
    w_rt = jnp.pad(w_rt, ((0, 0), (0, ROUTE_WIDTH - w_rt.shape[1])))
    w_rt_hi = w_rt.astype(BF16)
    w_rt_lo = (w_rt - w_rt_hi.astype(F32)).astype(BF16)
    b_rt = jnp.pad(jnp.concatenate([b_group, b_router]), (0, ROUTE_WIDTH - N_GROUPS - N_EXPERTS))
    row = lambda v: v.reshape(1, -1)
    return dict(
        w_in=w_in.astype(BF16), b_in=row(b_in), w_conv=w_conv, b_conv=row(b_conv), w_gates=w_gates,
        b_rg=row(b_rg), b_ig=row(b_ig), lam=row(lru_lambda),
        w_lru_out=w_lru_out.astype(BF16), w_attn_out=w_attn_out.astype(BF16), w_o=w_o.astype(BF16),
        ln1_g=row(ln1_g), ln1_b=row(ln1_b), w_rt_hi=w_rt_hi, w_rt_lo=w_rt_lo, b_rt=row(b_rt))


def kernel(x_prompt, x_sample, cache_k, cache_v, state_conv, state_lru_h, w_in, b_in, w_conv, b_conv, w_rg, b_rg,
           w_ig, b_ig, lru_lambda, sinks, w_lru_out, w_attn_out, w_o, ln1_g, ln1_b, w_group, b_group, w_router,
           b_router, w_gate, w_up, w_down, ln2_g, ln2_b):
    B, S, _ = x_prompt.shape
    n_prompt = B * S
    n_sample = x_sample.shape[0]
    n_all = n_prompt + n_sample
    wts = _prepare_weights(w_in, b_in, w_conv, b_conv, w_rg, b_rg, w_ig, b_ig, lru_lambda, w_lru_out, w_attn_out,
                           w_o, ln1_g, ln1_b, w_group, b_group, w_router, b_router)

    x_s = x_sample.reshape(n_sample, D_MODEL)
    u_s = _sample_proj_call(x_s, wts["w_in"], wts["b_in"])
    q3 = u_s[:, OFF_Q:OFF_K].reshape(n_sample, N_HEADS, HEAD_DIM)
    k_new = u_s[:, OFF_K:OFF_V]
    v_new = u_s[:, OFF_V:OFF_GL]
    att3, k_win_s, v_win_s = _sample_attn_call(
        q3, k_new.reshape(n_sample, N_KV, HEAD_DIM), v_new.reshape(n_sample, N_KV, HEAD_DIM),
        k_new.reshape(n_sample, 1, KV_WIDTH), v_new.reshape(n_sample, 1, KV_WIDTH),
        cache_k.reshape(n_sample, WINDOW, KV_WIDTH), cache_v.reshape(n_sample, WINDOW, KV_WIDTH),
        sinks.reshape(N_KV, GROUP, 1))
    x1_s, conv_s_t, h_s = _sample_mix_call(
        x_s, u_s, att3.reshape(n_sample, N_HEADS * HEAD_DIM), jnp.transpose(state_conv, (1, 0, 2)), state_lru_h, wts)

    x1_all, route_all, k_win_p, v_win_p, conv_p, h_p, xs, goff, gtot = _mixer_call(x_prompt, x1_s, sinks, wts)

    ys = _moe_call(gtot, xs, w_gate, w_up, w_down, n_all)
    goff = jnp.concatenate([goff, gtot[None]], axis=0)
    ln2_g2, ln2_b2 = ln2_g.reshape(1, -1), ln2_b.reshape(1, -1)
    y_p = _combine_call(goff, x1_all, route_all, ys, ln2_g2, ln2_b2, 0, n_prompt, DISPATCH_TILE, n_all)
    y_s = _combine_call(goff, x1_all, route_all, ys, ln2_g2, ln2_b2, n_prompt, n_sample, n_sample, n_all)

    kv_shape = (WINDOW, N_KV, HEAD_DIM)
    return (y_p.reshape(B, S, D_MODEL), y_s.reshape(n_sample, 1, D_MODEL),
            k_win_p.reshape((B,) + kv_shape), v_win_p.reshape((B,) + kv_shape), conv_p, h_p.reshape(B, LRU_WIDTH),
            k_win_s.reshape((n_sample,) + kv_shape), v_win_s.reshape((n_sample,) + kv_shape),
            jnp.transpose(conv_s_t, (1, 0, 2)), h_s)
```

```python
import functools

import jax
import jax.numpy as jnp
from jax import lax
from jax.experimental import pallas as pl
from jax.experimental.pallas import tpu as pltpu

F32 = jnp.float32
BF16 = jnp.bfloat16

D_MODEL = 1024
LRU_WIDTH = 1024
LRU_BLOCK = 64
CONV_W = 4
LRU_C = 8.0
N_HEADS = 16
N_KV = 4
GROUP = N_HEADS // N_KV
HEAD_DIM = 64
KV_WIDTH = N_KV * HEAD_DIM
WINDOW = 128
NEG_INF = -1e30
N_GROUPS = 4
EXPERTS_PER_GROUP = 8
N_EXPERTS = N_GROUPS * EXPERTS_PER_GROUP
TOP_K = 2
EXPERT_FF = D_MODEL // 2
DEPTH = 1
ALPHA = (2 * DEPTH) ** 0.25
LN_EPS = 1e-5
ATTN_SCALE = HEAD_DIM ** -0.5
LOG2_E = 1.4426950408889634

OFF_XL = 0
OFF_YL = OFF_XL + LRU_WIDTH
OFF_Q = OFF_YL + LRU_WIDTH
OFF_K = OFF_Q + N_HEADS * HEAD_DIM
OFF_V = OFF_K + KV_WIDTH
OFF_GL = OFF_V + KV_WIDTH
OFF_GA = OFF_GL + D_MODEL
IN_WIDTH = OFF_GA + D_MODEL

LANES = 128
SUBLANES = 8
MXU_DIM = 256
VMEM_LIMIT_BYTES = 56 * 1024 * 1024

GATE_CHUNK = MXU_DIM
N_GATE_CHUNKS = LRU_WIDTH // GATE_CHUNK
ROUTE_WIDTH = LANES

SEQ_TILE = 256
MOE_TILE = 464
DISPATCH_TILE = 256
SLOT_CAP = 32
N_SLOTS = N_EXPERTS * SLOT_CAP
SEG_PAD = MOE_TILE + SLOT_CAP
PACKED_WIDTH = D_MODEL // 2
SAMPLE_ATTN_TILE = 16
SAMPLE_PROJ_TILE = 512


def _const_spec(shape):
    nd = len(shape)
    return pl.BlockSpec(shape, lambda *_: (0,) * nd)


def _layer_norm(z, g, b):
    mu = jnp.mean(z, axis=-1, keepdims=True)
    zc = z - mu
    var = jnp.mean(zc * zc, axis=-1, keepdims=True)
    return zc * lax.rsqrt(var + LN_EPS) * g + b


def _sigmoid(x):
    return 1.0 / (1.0 + jnp.exp2(x * -LOG2_E))


def _softplus(x):
    return jnp.maximum(x, 0.0) + jnp.log1p(jnp.exp(-jnp.abs(x)))


def _bdot(a, b):
    return jnp.dot(a.astype(BF16), b.astype(BF16), preferred_element_type=F32)


def _lru_gates(xc, w_gates_ref, b_rg, b_ig, lam):
    xcb = xc.astype(BF16)
    r_parts, i_parts = [], []
    for c in range(N_GATE_CHUNKS):
        g = jnp.dot(xcb[:, c * GATE_CHUNK:(c + 1) * GATE_CHUNK], w_gates_ref[c], preferred_element_type=F32)
        r_parts.append(g[:, :GATE_CHUNK])
        i_parts.append(g[:, GATE_CHUNK:])
    r = _sigmoid(jnp.concatenate(r_parts, axis=1) + b_rg)
    i = _sigmoid(jnp.concatenate(i_parts, axis=1) + b_ig)
    log_a = (-LRU_C * r) * _softplus(-lam)
    a = jnp.exp(log_a)
    gain_sq = 1.0 - a * a
    gain = jnp.where(gain_sq > 0.0, gain_sq * lax.rsqrt(gain_sq), 0.0)
    u = gain * (i * xc)
    return a, u


def _linear_scan(a, u, h_in):
    n, w = a.shape
    groups = n // SUBLANES
    a3 = a.reshape(groups, SUBLANES, w)
    u3 = u.reshape(groups, SUBLANES, w)
    row = lax.broadcasted_iota(jnp.int32, a3.shape, 1)
    d = 1
    while d < SUBLANES:
        has_prev = row >= d
        u3 = u3 + a3 * jnp.where(has_prev, pltpu.roll(u3, d, axis=1), 0.0)
        a3 = a3 * jnp.where(has_prev, pltpu.roll(a3, d, axis=1), 1.0)
        d *= 2
    carry = h_in
    out = []
    for g in range(groups):
        h_g = u3[g] + a3[g] * carry
        out.append(h_g)
        carry = h_g[SUBLANES - 1:SUBLANES, :]
    return jnp.concatenate(out, axis=0)


def _route(x1, w_hi_ref, w_lo_ref, b_rt):
    x_hi = x1.astype(BF16)
    x_lo = (x1 - x_hi.astype(F32)).astype(BF16)
    w_hi = w_hi_ref[...]
    logits = (jnp.dot(x_hi, w_hi, preferred_element_type=F32)
              + (jnp.dot(x_lo, w_hi, preferred_element_type=F32)
                 + jnp.dot(x_hi, w_lo_ref[...], preferred_element_type=F32))) + b_rt
    col = lax.broadcasted_iota(jnp.int32, logits.shape, 1)
    big = jnp.int32(ROUTE_WIDTH)
    is_g = col < N_GROUPS
    gl = jnp.where(is_g, logits, -jnp.inf)
    gmax = jnp.max(gl, axis=-1, keepdims=True)
    g_idx = jnp.min(jnp.where(gl == gmax, col, big), axis=-1, keepdims=True)
    p_g = 1.0 / jnp.sum(jnp.where(is_g, jnp.exp(gl - gmax), 0.0), axis=-1, keepdims=True)
    lo = N_GROUPS + g_idx * EXPERTS_PER_GROUP
    in_grp = (col >= lo) & (col < lo + EXPERTS_PER_GROUP)
    el = jnp.where(in_grp, logits, -jnp.inf)
    v1 = jnp.max(el, axis=-1, keepdims=True)
    i1 = jnp.min(jnp.where(el == v1, col, big), axis=-1, keepdims=True)
    el2 = jnp.where(col == i1, -jnp.inf, el)
    v2 = jnp.max(el2, axis=-1, keepdims=True)
    i2 = jnp.min(jnp.where(el2 == v2, col, big), axis=-1, keepdims=True)
    e21 = jnp.exp(v2 - v1)
    inv = 1.0 / (1.0 + e21)
    w1 = p_g * inv
    w2 = p_g * (e21 * inv)
    e1 = (i1 - N_GROUPS).astype(F32)
    e2 = (i2 - N_GROUPS).astype(F32)
    return jnp.where(col == 0, e1, jnp.where(col == 1, e2, jnp.where(col == 2, w1, jnp.where(col == 3, w2, 0.0))))


def _merge_norm(x, rec, att, g_l, g_a, w_out_ref, ln_g, ln_b):
    rec_o = jnp.dot(rec.astype(BF16), w_out_ref[0], preferred_element_type=F32)
    att_o = jnp.dot(att.astype(BF16), w_out_ref[1], preferred_element_type=F32)
    merged = _sigmoid(g_l) * rec_o + _sigmoid(g_a) * att_o
    mix = jnp.dot(merged.astype(BF16), w_out_ref[2], preferred_element_type=F32)
    return _layer_norm(ALPHA * x + mix, ln_g, ln_b)


def _mixer_kernel(sinks_ref, x_ref, w_in_ref, b_in_ref, w_conv_ref, b_conv_ref, w_gates_ref, b_rg_ref, b_ig_ref,
                  lam_ref, w_out_ref, ln_g_ref, ln_b_ref, w_rt_hi_ref, w_rt_lo_ref,
                  b_rt_ref, x1_s_ref,
                  x1_ref, route_ref, kwin_ref, vwin_ref, conv_ref, h_ref, xs_hbm, goff_ref, gtot_ref,
                  conv_buf, h_carry, kcat, vcat, att_buf, prev_x1, stage, zero_rows, tri, xb_keep, key_keep,
                  g_vmem, gvec, g_smem, nd_smem, rounds_smem, sem_stage, sem_g, sem_zero,
                  *, tiles_per_seq, n_tiles, seg_cap):
    step = pl.program_id(0)
    last_step = pl.num_programs(0) - 1
    buf = lax.rem(step, 2)
    n_sample = x1_s_ref.shape[0]
    disp = _Dispatcher(xs_hbm, goff_ref, gtot_ref, stage, zero_rows, tri, xb_keep, key_keep, g_vmem, gvec,
                       g_smem, nd_smem, rounds_smem, sem_stage, sem_g, sem_zero, seg_cap)
    n_valid = jnp.where(step == 0, 0, jnp.where(step == last_step, n_sample, SEQ_TILE))

    @pl.when(step == 0)
    def _():
        disp.init()
        prev_x1[...] = jnp.zeros_like(prev_x1)

    def route_and_plan():
        x1_prev = prev_x1[...]
        route = _route(x1_prev, w_rt_hi_ref, w_rt_lo_ref, b_rt_ref[...])
        route_ref[...] = route
        disp.plan(x1_prev, route, n_valid, buf)

    @pl.when(step < n_tiles)
    def _():
        @pl.when(lax.rem(step, tiles_per_seq) == 0)
        def _():
            conv_buf[...] = jnp.zeros_like(conv_buf)
            h_carry[...] = jnp.zeros_like(h_carry)
            kcat[0:WINDOW, :] = jnp.zeros((WINDOW, KV_WIDTH), BF16)
            vcat[0:WINDOW, :] = jnp.zeros((WINDOW, KV_WIDTH), BF16)

        route_and_plan()
        _mixer_tile(lax.rem(step, tiles_per_seq), sinks_ref, x_ref, w_in_ref, b_in_ref, w_conv_ref, b_conv_ref,
                    w_gates_ref, b_rg_ref, b_ig_ref, lam_ref, w_out_ref, ln_g_ref,
                    ln_b_ref, x1_ref, prev_x1, kwin_ref, vwin_ref, conv_ref, h_ref,
                    conv_buf, h_carry, kcat, vcat, att_buf)

    @pl.when(step >= n_tiles)
    def _():
        route_and_plan()

        @pl.when(step == n_tiles)
        def _():
            x1_ref[0:n_sample, :] = x1_s_ref[...]
            prev_x1[0:n_sample, :] = x1_s_ref[...]

    disp.flush(buf, step == last_step)


def _mixer_tile(t, sinks_ref, x_ref, w_in_ref, b_in_ref, w_conv_ref, b_conv_ref, w_gates_ref, b_rg_ref, b_ig_ref,
                lam_ref, w_out_ref, ln_g_ref, ln_b_ref,
                x1_ref, x1_keep, kwin_ref, vwin_ref, conv_ref, h_ref,
                conv_buf, h_carry, kcat, vcat, att_buf):
    T = SEQ_TILE
    x = x_ref[0]
    xb = x.astype(BF16)

    def proj(lo, width):
        return jnp.dot(xb, w_in_ref[:, lo:lo + width], preferred_element_type=F32) + b_in_ref[:, lo:lo + width]

    xl = proj(OFF_XL, LRU_WIDTH)
    xl_ext = jnp.concatenate([conv_buf[...], xl], axis=0)

    def lagged(k):
        return pltpu.roll(xl_ext, k, axis=0)[SUBLANES:, :]
    wc = w_conv_ref[...]
    xc = wc[0:1] * lagged(3)
    xc = xc + wc[1:2] * lagged(2)
    xc = xc + wc[2:3] * lagged(1)
    xc = xc + wc[3:4] * xl + b_conv_ref[...]
    conv_ref[0] = xl[T - (CONV_W - 1):, :]
    conv_buf[...] = xl[T - SUBLANES:, :]

    a, u = _lru_gates(xc, w_gates_ref, b_rg_ref[...], b_ig_ref[...], lam_ref[...])
    h = _linear_scan(a, u, h_carry[0:1, :])
    h_last = h[T - 1:T, :]
    h_carry[0:1, :] = h_last
    h_ref[0] = h_last
    rec = h * jax.nn.gelu(proj(OFF_YL, LRU_WIDTH))

    q = proj(OFF_Q, N_HEADS * HEAD_DIM) * (ATTN_SCALE * LOG2_E)
    k = proj(OFF_K, KV_WIDTH)
    v = proj(OFF_V, KV_WIDTH)
    kwin_ref[0] = k[T - WINDOW:, :]
    vwin_ref[0] = v[T - WINDOW:, :]
    kcat[WINDOW:WINDOW + T, :] = k.astype(BF16)
    vcat[WINDOW:WINDOW + T, :] = v.astype(BF16)

    qi = lax.broadcasted_iota(jnp.int32, (WINDOW, 2 * WINDOW), 0)
    kj = lax.broadcasted_iota(jnp.int32, (WINDOW, 2 * WINDOW), 1)
    band = (kj > qi) & (kj <= qi + WINDOW)
    grp_row = lax.broadcasted_iota(jnp.int32, (GROUP * WINDOW, 1), 0) // WINDOW
    for qb in range(T // WINDOW):
        if qb == 0:
            first_key = jnp.where(t == 0, WINDOW, 0)
            mask1 = band & (kj >= first_key)
        else:
            mask1 = band
        bias = jnp.concatenate([jnp.where(mask1, 0.0, NEG_INF)] * GROUP, axis=0)
        r0 = qb * WINDOW
        qq = q[r0:r0 + WINDOW, :]
        for j in range(N_KV):
            kjb = kcat[r0:r0 + 2 * WINDOW, j * HEAD_DIM:(j + 1) * HEAD_DIM]
            vjb = vcat[r0:r0 + 2 * WINDOW, j * HEAD_DIM:(j + 1) * HEAD_DIM]
            qs = jnp.concatenate(
                [qq[:, (j * GROUP + g) * HEAD_DIM:(j * GROUP + g + 1) * HEAD_DIM] for g in range(GROUP)], axis=0)
            s = lax.dot_general(qs.astype(BF16), kjb, (((1,), (1,)), ((), ())), preferred_element_type=F32) + bias
            sink = jnp.zeros((GROUP * WINDOW, 1), F32)
            for g in range(GROUP):
                sink = jnp.where(grp_row == g, sinks_ref[j * GROUP + g] * LOG2_E, sink)
            m = jnp.maximum(jnp.max(s, axis=-1, keepdims=True), sink)
            p = jnp.exp2(s - m)
            inv = 1.0 / (jnp.sum(p, axis=-1, keepdims=True) + jnp.exp2(sink - m))
            o = jnp.dot((p * inv).astype(BF16), vjb, preferred_element_type=F32)
            for g in range(GROUP):
                hcol = (j * GROUP + g) * HEAD_DIM
                att_buf[r0:r0 + WINDOW, hcol:hcol + HEAD_DIM] = o[g * WINDOW:(g + 1) * WINDOW, :]
    kcat[0:WINDOW, :] = kcat[T:T + WINDOW, :]
    vcat[0:WINDOW, :] = vcat[T:T + WINDOW, :]

    x1 = _merge_norm(x, rec, att_buf[...], proj(OFF_GL, D_MODEL), proj(OFF_GA, D_MODEL),
                     w_out_ref, ln_g_ref[...], ln_b_ref[...])
    x1_ref[...] = x1
    x1_keep[...] = x1


def _mixer_call(x_prompt, x1_s, sinks, wts):
    B, S, _ = x_prompt.shape
    T = SEQ_TILE
    assert T == DISPATCH_TILE and x1_s.shape[0] <= T
    nt = S // T
    n_tiles = B * nt
    n_rows_total = B * S + x1_s.shape[0]
    n_dispatch = _dispatch_tiles(n_rows_total)
    assert n_dispatch == n_tiles + 1
    seg_cap = _segment_capacity(n_rows_total)
    n_assign = TOP_K * T
    weight_args = (wts["w_in"], wts["b_in"], wts["w_conv"], wts["b_conv"], wts["w_gates"], wts["b_rg"], wts["b_ig"],
                   wts["lam"], wts["w_out"], wts["ln1_g"], wts["ln1_b"],
                   wts["w_rt_hi"], wts["w_rt_lo"], wts["b_rt"], x1_s)
    mixed = lambda i: jnp.minimum(i, n_tiles - 1)
    seq = lambda i: mixed(i) // nt
    routed = lambda i: jnp.clip(i - 1, 0, n_dispatch - 1)
    in_specs = [pl.BlockSpec(memory_space=pltpu.SMEM),
                pl.BlockSpec((1, T, D_MODEL), lambda i: (seq(i), lax.rem(mixed(i), nt), 0))]
    in_specs += [_const_spec(w.shape) for w in weight_args]
    out_shape = (
        jax.ShapeDtypeStruct((n_rows_total, D_MODEL), F32),
        jax.ShapeDtypeStruct((n_rows_total, ROUTE_WIDTH), F32),
        jax.ShapeDtypeStruct((B, WINDOW, KV_WIDTH), F32),
        jax.ShapeDtypeStruct((B, WINDOW, KV_WIDTH), F32),
        jax.ShapeDtypeStruct((B, CONV_W - 1, LRU_WIDTH), F32),
        jax.ShapeDtypeStruct((B, 1, LRU_WIDTH), F32),
        jax.ShapeDtypeStruct((N_EXPERTS * seg_cap, PACKED_WIDTH), jnp.uint32),
        jax.ShapeDtypeStruct((n_dispatch, 1, ROUTE_WIDTH), jnp.int32),
        jax.ShapeDtypeStruct((1, ROUTE_WIDTH), jnp.int32),
    )
    out_specs = (
        pl.BlockSpec((T, D_MODEL), lambda i: (jnp.minimum(i, n_tiles), 0)),
        pl.BlockSpec((T, ROUTE_WIDTH), lambda i: (routed(i), 0)),
        pl.BlockSpec((1, WINDOW, KV_WIDTH), lambda i: (seq(i), 0, 0)),
        pl.BlockSpec((1, WINDOW, KV_WIDTH), lambda i: (seq(i), 0, 0)),
        pl.BlockSpec((1, CONV_W - 1, LRU_WIDTH), lambda i: (seq(i), 0, 0)),
        pl.BlockSpec((1, 1, LRU_WIDTH), lambda i: (seq(i), 0, 0)),
        pl.BlockSpec(memory_space=pl.ANY),
        pl.BlockSpec((1, 1, ROUTE_WIDTH), lambda i: (routed(i), 0, 0)),
        pl.BlockSpec((1, ROUTE_WIDTH), lambda i: (0, 0)),
    )
    scratch = [
        pltpu.VMEM((SUBLANES, LRU_WIDTH), F32),
        pltpu.VMEM((SUBLANES, LRU_WIDTH), F32),
        pltpu.VMEM((T + WINDOW, KV_WIDTH), BF16),
        pltpu.VMEM((T + WINDOW, KV_WIDTH), BF16),
        pltpu.VMEM((T, N_HEADS * HEAD_DIM), F32),
        pltpu.VMEM((T, D_MODEL), F32),
        pltpu.VMEM((2, N_SLOTS, PACKED_WIDTH), jnp.uint32),
        pltpu.VMEM((SEG_PAD, PACKED_WIDTH), jnp.uint32),
        pltpu.VMEM((n_assign, n_assign), BF16),
        pltpu.VMEM((T, D_MODEL), BF16),
        pltpu.VMEM((SUBLANES, n_assign), F32),
        pltpu.VMEM((SUBLANES, ROUTE_WIDTH), jnp.int32),
        pltpu.VMEM((SUBLANES, ROUTE_WIDTH), F32),
        pltpu.SMEM((1, ROUTE_WIDTH), jnp.int32),
        pltpu.SMEM((1,), jnp.int32),
        pltpu.SMEM((1,), jnp.int32),
        pltpu.SemaphoreType.DMA(()),
        pltpu.SemaphoreType.DMA(()),
        pltpu.SemaphoreType.DMA(()),
    ]
    return pl.pallas_call(
        functools.partial(_mixer_kernel, tiles_per_seq=nt, n_tiles=n_tiles, seg_cap=seg_cap),
        grid=(n_tiles + 2,),
        in_specs=in_specs,
        out_specs=out_specs,
        out_shape=out_shape,
        scratch_shapes=scratch,
        compiler_params=pltpu.CompilerParams(dimension_semantics=("arbitrary",),
                                             vmem_limit_bytes=VMEM_LIMIT_BYTES),
        name="mixer_prompt",
    )(sinks, x_prompt, *weight_args)


def _sample_proj_kernel(x_ref, w_ref, b_ref, u_ref):
    u_ref[...] = jnp.dot(x_ref[...].astype(BF16), w_ref[...], preferred_element_type=F32) + b_ref[...]


def _sample_proj_call(x_s, w_in, b_in):
    n = x_s.shape[0]
    tn = SAMPLE_PROJ_TILE
    return pl.pallas_call(
        _sample_proj_kernel,
        grid=(IN_WIDTH // tn,),
        in_specs=[pl.BlockSpec((n, D_MODEL), lambda c: (0, 0)),
                  pl.BlockSpec((D_MODEL, tn), lambda c: (0, c)),
                  pl.BlockSpec((1, tn), lambda c: (0, c))],
        out_specs=pl.BlockSpec((n, tn), lambda c: (0, c)),
        out_shape=jax.ShapeDtypeStruct((n, IN_WIDTH), F32),
        compiler_params=pltpu.CompilerParams(dimension_semantics=("arbitrary",)),
        name="sample_proj",
    )(x_s, w_in, b_in)


def _sample_attn_kernel(q_ref, kn_ref, vn_ref, kn_row_ref, vn_row_ref, ck_ref, cv_ref, sinks_ref,
                        att_ref, kwin_ref, vwin_ref):
    tb = q_ref.shape[0]
    key_pos = lax.broadcasted_iota(jnp.int32, (tb, GROUP, WINDOW), 2)
    for j in range(N_KV):
        qj = q_ref[:, j * GROUP:(j + 1) * GROUP, :]
        kc = ck_ref[:, :, j * HEAD_DIM:(j + 1) * HEAD_DIM]
        vc = cv_ref[:, :, j * HEAD_DIM:(j + 1) * HEAD_DIM]
        s_c = jnp.einsum("bgd,bsd->bgs", qj.astype(BF16), kc.astype(BF16), preferred_element_type=F32) * ATTN_SCALE
        s_c = jnp.where(key_pos >= 1, s_c, NEG_INF)
        kn = kn_ref[:, j:j + 1, :]
        vn = vn_ref[:, j:j + 1, :]
        s_n = jnp.sum(qj * kn, axis=-1, keepdims=True) * ATTN_SCALE
        sink = sinks_ref[j][None]
        m = jnp.maximum(jnp.maximum(jnp.max(s_c, axis=-1, keepdims=True), s_n), sink)
        p_c = jnp.exp(s_c - m)
        p_n = jnp.exp(s_n - m)
        inv = 1.0 / (jnp.sum(p_c, axis=-1, keepdims=True) + p_n + jnp.exp(sink - m))
        o = jnp.einsum("bgs,bsd->bgd", (p_c * inv).astype(BF16), vc.astype(BF16), preferred_element_type=F32)
        att_ref[:, j * GROUP:(j + 1) * GROUP, :] = o + (p_n * inv) * vn
    kwin_ref[:, 0:WINDOW - 1, :] = ck_ref[:, 1:WINDOW, :]
    kwin_ref[:, WINDOW - 1:WINDOW, :] = kn_row_ref[...]
    vwin_ref[:, 0:WINDOW - 1, :] = cv_ref[:, 1:WINDOW, :]
    vwin_ref[:, WINDOW - 1:WINDOW, :] = vn_row_ref[...]


def _sample_attn_call(q3, kn3, vn3, kn_row, vn_row, ck, cv, sinks3):
    n = q3.shape[0]
    tb = SAMPLE_ATTN_TILE
    b3 = lambda i: (i, 0, 0)
    return pl.pallas_call(
        _sample_attn_kernel,
        grid=(n // tb,),
        in_specs=[pl.BlockSpec((tb, N_HEADS, HEAD_DIM), b3),
                  pl.BlockSpec((tb, N_KV, HEAD_DIM), b3),
                  pl.BlockSpec((tb, N_KV, HEAD_DIM), b3),
                  pl.BlockSpec((tb, 1, KV_WIDTH), b3),
                  pl.BlockSpec((tb, 1, KV_WIDTH), b3),
                  pl.BlockSpec((tb, WINDOW, KV_WIDTH), b3),
                  pl.BlockSpec((tb, WINDOW, KV_WIDTH), b3),
                  pl.BlockSpec((N_KV, GROUP, 1), lambda i: (0, 0, 0))],
        out_specs=(pl.BlockSpec((tb, N_HEADS, HEAD_DIM), b3),
                   pl.BlockSpec((tb, WINDOW, KV_WIDTH), b3),
                   pl.BlockSpec((tb, WINDOW, KV_WIDTH), b3)),
        out_shape=(jax.ShapeDtypeStruct((n, N_HEADS, HEAD_DIM), F32),
                   jax.ShapeDtypeStruct((n, WINDOW, KV_WIDTH), F32),
                   jax.ShapeDtypeStruct((n, WINDOW, KV_WIDTH), F32)),
        compiler_params=pltpu.CompilerParams(dimension_semantics=("arbitrary",)),
        name="sample_attn",
    )(q3, kn3, vn3, kn_row, vn_row, ck, cv, sinks3)


def _sample_mix_kernel(x_ref, u_ref, att_ref, st_ref, h0_ref, w_conv_ref, b_conv_ref, w_gates_ref, b_rg_ref,
                       b_ig_ref, lam_ref, w_out_ref, ln_g_ref, ln_b_ref,
                       x1_ref, conv_ref, h_ref):
    xl = u_ref[:, OFF_XL:OFF_XL + LRU_WIDTH]
    wc = w_conv_ref[...]
    xc = wc[0:1] * st_ref[0]
    xc = xc + wc[1:2] * st_ref[1]
    xc = xc + wc[2:3] * st_ref[2]
    xc = xc + wc[3:4] * xl + b_conv_ref[...]
    conv_ref[0] = st_ref[1]
    conv_ref[1] = st_ref[2]
    conv_ref[2] = xl
    a, u = _lru_gates(xc, w_gates_ref, b_rg_ref[...], b_ig_ref[...], lam_ref[...])
    h = a * h0_ref[...] + u
    h_ref[...] = h
    rec = h * jax.nn.gelu(u_ref[:, OFF_YL:OFF_YL + LRU_WIDTH])
    x1_ref[...] = _merge_norm(x_ref[...], rec, att_ref[...], u_ref[:, OFF_GL:OFF_GL + D_MODEL],
                              u_ref[:, OFF_GA:OFF_GA + D_MODEL], w_out_ref,
                              ln_g_ref[...], ln_b_ref[...])


def _sample_mix_call(x_s, u_s, att, st_t, h0, wts):
    n = x_s.shape[0]
    weight_args = (wts["w_conv"], wts["b_conv"], wts["w_gates"], wts["b_rg"], wts["b_ig"], wts["lam"],
                   wts["w_out"], wts["ln1_g"], wts["ln1_b"])
    args = (x_s, u_s, att, st_t, h0) + weight_args
    out_shapes = ((n, D_MODEL), (CONV_W - 1, n, LRU_WIDTH), (n, LRU_WIDTH))
    return pl.pallas_call(
        _sample_mix_kernel,
        grid=(1,),
        in_specs=[_const_spec(a.shape) for a in args],
        out_specs=tuple(_const_spec(s) for s in out_shapes),
        out_shape=tuple(jax.ShapeDtypeStruct(s, F32) for s in out_shapes),
        compiler_params=pltpu.CompilerParams(dimension_semantics=("arbitrary",),
                                             vmem_limit_bytes=VMEM_LIMIT_BYTES),
        name="sample_mix",
    )(*args)


def _one_hot(mask):
    return jnp.where(mask, 1.0, 0.0).astype(BF16)


def _pack_rows(x):
    half = x.shape[1] // 2
    lo = lax.shift_right_logical(lax.bitcast_convert_type(x[:, :half], jnp.uint32), jnp.uint32(16))
    hi = lax.bitcast_convert_type(x[:, half:], jnp.uint32) & jnp.uint32(0xFFFF0000)
    return lo | hi


def _unpack_rows(words):
    lo = lax.bitcast_convert_type(lax.shift_left(words, jnp.uint32(16)), F32)
    hi = lax.bitcast_convert_type(words & jnp.uint32(0xFFFF0000), F32)
    return jnp.concatenate([lo.astype(BF16), hi.astype(BF16)], axis=1)


class _Dispatcher:
    def __init__(self, xs_hbm, goff_ref, gtot_ref, stage, zero_rows, tri, xb_keep, key_keep, g_vmem, gvec,
                 g_smem, nd_smem, rounds_smem, sem_stage, sem_g, sem_zero, seg_cap):
        self.xs_hbm, self.goff_ref, self.gtot_ref = xs_hbm, goff_ref, gtot_ref
        self.stage, self.zero_rows, self.tri = stage, zero_rows, tri
        self.xb_keep, self.key_keep = xb_keep, key_keep
        self.g_vmem, self.gvec, self.g_smem = g_vmem, gvec, g_smem
        self.nd_smem, self.rounds_smem = nd_smem, rounds_smem
        self.sem_stage, self.sem_g, self.sem_zero = sem_stage, sem_g, sem_zero
        self.seg_cap = seg_cap

    def _g_copy(self):
        return pltpu.make_async_copy(self.g_vmem.at[pl.ds(0, 1)], self.g_smem, self.sem_g)

    def init(self):
        n_assign = self.tri.shape[0]
        r = lax.broadcasted_iota(jnp.int32, (n_assign, n_assign), 0)
        c = lax.broadcasted_iota(jnp.int32, (n_assign, n_assign), 1)
        self.tri[...] = _one_hot(r < c)
        self.gvec[...] = jnp.zeros_like(self.gvec)
        self.g_vmem[...] = jnp.zeros_like(self.g_vmem)
        self.nd_smem[0] = 0
        self._g_copy().start()

    def _sorted_rows(self, e_row, rank_row, xb, lo):
        td = xb.shape[0]
        slot = lax.broadcasted_iota(jnp.int32, (N_SLOTS, td), 0).astype(F32)
        lo_f = lo.astype(F32)
        in_round = (rank_row >= lo_f) & (rank_row < lo_f + SLOT_CAP) & (e_row >= 0.0)
        key = jnp.where(in_round, e_row * SLOT_CAP + (rank_row - lo_f), -1.0)
        perm = jnp.where(slot == key[:, :td], 1.0, jnp.where(slot == key[:, td:], 1.0, 0.0)).astype(BF16)
        return _pack_rows(jnp.dot(perm, xb, preferred_element_type=F32))

    def plan(self, x1, route, n_valid, buf):
        td = x1.shape[0]
        n_assign = TOP_K * td
        valid_col = lax.broadcasted_iota(jnp.int32, (td, 1), 0) < n_valid
        lane = lax.broadcasted_iota(jnp.int32, (td, ROUTE_WIDTH), 1).astype(F32)
        e0c = jnp.where(valid_col, route[:, 0:1], -1.0)
        e1c = jnp.where(valid_col, route[:, 1:2], -1.0)
        cnt_row = jnp.sum(jnp.where(lane == e0c, 1.0, 0.0) + jnp.where(lane == e1c, 1.0, 0.0),
                          axis=0, keepdims=True)
        self.rounds_smem[0] = (jnp.max(cnt_row).astype(jnp.int32) + SLOT_CAP - 1) // SLOT_CAP
        self.goff_ref[0] = self.gvec[0:1, :].astype(jnp.int32)
        self.gvec[0:1, :] = self.gvec[0:1, :] + jnp.ceil(cnt_row * (1.0 / SUBLANES)) * SUBLANES
        self.gtot_ref[...] = self.gvec[0:1, :].astype(jnp.int32)

        route_t = route.T
        valid_row = lax.broadcasted_iota(jnp.int32, (1, td), 1) < n_valid
        e_row = jnp.concatenate([jnp.where(valid_row, route_t[0:1, :], -1.0),
                                 jnp.where(valid_row, route_t[1:2, :], -1.0)], axis=1)
        expert_sub = lax.broadcasted_iota(jnp.int32, (N_EXPERTS, n_assign), 0).astype(F32)
        onehot_t = expert_sub == e_row
        rank_t = jnp.dot(_one_hot(onehot_t), self.tri[...], preferred_element_type=F32)
        rank_row = jnp.sum(jnp.where(onehot_t, rank_t, 0.0), axis=0, keepdims=True)
        xb = jnp.where(valid_col, x1, 0.0).astype(BF16)
        self.xb_keep[...] = xb
        self.key_keep[0:1, :] = e_row
        self.key_keep[1:2, :] = rank_row
        self.stage[buf] = self._sorted_rows(e_row, rank_row, xb, jnp.int32(0))

    def _stage_copy(self, buf, e, dst_row):
        src = self.stage.at[buf, pl.ds(pl.multiple_of(e * SLOT_CAP, SLOT_CAP), SLOT_CAP)]
        return pltpu.make_async_copy(src, self.xs_hbm.at[pl.ds(dst_row, SLOT_CAP)], self.sem_stage)

    def _wait_outstanding(self):
        def wait_one(_, carry):
            self._stage_copy(0, 0, 0).wait()
            return carry
        lax.fori_loop(0, self.nd_smem[0], wait_one, 0)
        self.nd_smem[0] = 0

    def _start_round(self, buf, lo):
        def issue(e, carry):
            dst_row = pl.multiple_of(e * self.seg_cap + self.g_smem[0, e] + lo, SUBLANES)
            self._stage_copy(buf, e, dst_row).start()
            return carry
        lax.fori_loop(0, N_EXPERTS, issue, 0)
        self.nd_smem[0] = N_EXPERTS

    def flush(self, buf, is_last):
        self._g_copy().wait()
        self._wait_outstanding()
        self._start_round(buf, 0)

        def later_round(r, carry):
            lo = r * SLOT_CAP
            rows = self._sorted_rows(self.key_keep[0:1, :], self.key_keep[1:2, :], self.xb_keep[...], lo)
            self._wait_outstanding()
            self.stage[buf] = rows
            self._start_round(buf, lo)
            return carry
        lax.fori_loop(1, self.rounds_smem[0], later_round, 0)

        self.g_vmem[0:1, :] = self.gvec[0:1, :].astype(jnp.int32)
        self._g_copy().start()

        @pl.when(is_last)
        def _():
            self._g_copy().wait()
            self._wait_outstanding()
            self.zero_rows[...] = jnp.zeros_like(self.zero_rows)

            def pad_copy(e):
                dst_row = pl.multiple_of(e * self.seg_cap + self.g_smem[0, e], SUBLANES)
                return pltpu.make_async_copy(self.zero_rows, self.xs_hbm.at[pl.ds(dst_row, SEG_PAD)], self.sem_zero)

            def start(e, carry):
                pad_copy(e).start()
                return carry
            lax.fori_loop(0, N_EXPERTS, start, 0)

            def wait(e, carry):
                pad_copy(e).wait()
                return carry
            lax.fori_loop(0, N_EXPERTS, wait, 0)


def _dispatch_tiles(n_tokens):
    return (n_tokens + DISPATCH_TILE - 1) // DISPATCH_TILE


def _segment_capacity(n_tokens):
    alignment_slack = (SUBLANES - 1) * _dispatch_tiles(n_tokens)
    return (n_tokens + alignment_slack + SEG_PAD + MOE_TILE - 1) // MOE_TILE * MOE_TILE


def _moe_kernel(blk_e_ref, blk_j_ref, n_used_ref, xs_ref, wg_ref, wu_ref, wd_ref, ys_ref, wg_bf, wu_bf, wd_bf):
    del blk_j_ref
    b = pl.program_id(0)

    @pl.when(b < n_used_ref[0])
    def _():
        prev_e = blk_e_ref[jnp.maximum(b - 1, 0)]

        @pl.when((b == 0) | (blk_e_ref[b] != prev_e))
        def _():
            wg_bf[...] = wg_ref[0].astype(BF16)
            wu_bf[...] = wu_ref[0].astype(BF16)
            wd_bf[...] = wd_ref[0].astype(BF16)

        xb = _unpack_rows(xs_ref[...])
        g = jnp.dot(xb, wg_bf[...], preferred_element_type=F32)
        u = jnp.dot(xb, wu_bf[...], preferred_element_type=F32)
        hmid = (g * _sigmoid(g)) * u
        y = jnp.dot(hmid.astype(BF16), wd_bf[...], preferred_element_type=F32)
        ys_ref[...] = _pack_rows(y.astype(BF16).astype(F32))


def _expert_blocks(gtot, n_blocks):
    rows = gtot[0, :N_EXPERTS]
    nb = (rows + SLOT_CAP + MOE_TILE - 1) // MOE_TILE
    ends = jnp.cumsum(nb)
    n_used = ends[-1]
    step = jnp.minimum(jnp.arange(n_blocks, dtype=jnp.int32), n_used - 1)
    blk_e = jnp.sum((step[:, None] >= ends[None, :]).astype(jnp.int32), axis=1)
    blk_j = step - (ends - nb)[blk_e]
    return blk_e, blk_j.astype(jnp.int32), n_used.reshape(1).astype(jnp.int32)


def _moe_call(gtot, xs, w_gate, w_up, w_down, n_tokens):
    seg_cap = _segment_capacity(n_tokens)
    seg_blocks = seg_cap // MOE_TILE
    max_rows = n_tokens * TOP_K + N_EXPERTS * (SUBLANES - 1) * _dispatch_tiles(n_tokens)
    n_blocks = (max_rows + N_EXPERTS * (SLOT_CAP + MOE_TILE - 1)) // MOE_TILE
    blk_e, blk_j, n_used = _expert_blocks(gtot, n_blocks)
    row_block = lambda b, be, bj, nu: (be[b] * seg_blocks + bj[b], 0)
    weight_block = lambda b, be, bj, nu: (be[b], 0, 0)
    grid_spec = pltpu.PrefetchScalarGridSpec(
        num_scalar_prefetch=3,
        grid=(n_blocks,),
        in_specs=[
            pl.BlockSpec((MOE_TILE, PACKED_WIDTH), row_block),
            pl.BlockSpec((1, D_MODEL, EXPERT_FF), weight_block),
            pl.BlockSpec((1, D_MODEL, EXPERT_FF), weight_block),
            pl.BlockSpec((1, EXPERT_FF, D_MODEL), weight_block),
        ],
        out_specs=pl.BlockSpec((MOE_TILE, PACKED_WIDTH), row_block),
        scratch_shapes=[
            pltpu.VMEM((D_MODEL, EXPERT_FF), BF16),
            pltpu.VMEM((D_MODEL, EXPERT_FF), BF16),
            pltpu.VMEM((EXPERT_FF, D_MODEL), BF16),
        ],
    )
    return pl.pallas_call(
        _moe_kernel,
        grid_spec=grid_spec,
        out_shape=jax.ShapeDtypeStruct(xs.shape, xs.dtype),
        compiler_params=pltpu.CompilerParams(dimension_semantics=("arbitrary",),
                                             vmem_limit_bytes=VMEM_LIMIT_BYTES),
        name="moe_experts",
    )(blk_e, blk_j, n_used, xs, w_gate, w_up, w_down)


def _combine_kernel(gcur_ref, gnext_ref, x1_ref, route_ref, ys_hbm, ln_g_ref, ln_b_ref, y_ref,
                    stage, acc, tri, sem, *, seg_cap):
    td = x1_ref.shape[0]
    n_assign = TOP_K * td
    i = pl.program_id(0)
    buf = lax.rem(i, 2)

    def stage_copy(g_ref, e, first_rank, to_buf):
        src_row = pl.multiple_of(e * seg_cap + g_ref[0, 0, e] + first_rank, SUBLANES)
        dst = stage.at[to_buf, pl.ds(pl.multiple_of(e * SLOT_CAP, SLOT_CAP), SLOT_CAP)]
        return pltpu.make_async_copy(ys_hbm.at[pl.ds(src_row, SLOT_CAP)], dst, sem.at[to_buf])

    def fetch_first_round(g_ref, to_buf):
        def start(e, carry):
            stage_copy(g_ref, e, 0, to_buf).start()
            return carry
        lax.fori_loop(0, N_EXPERTS, start, 0)

    @pl.when(i == 0)
    def _():
        r = lax.broadcasted_iota(jnp.int32, (n_assign, n_assign), 0)
        c = lax.broadcasted_iota(jnp.int32, (n_assign, n_assign), 1)
        tri[...] = _one_hot(c < r)
        fetch_first_round(gcur_ref, 0)

    def count(e):
        return gnext_ref[0, 0, e] - gcur_ref[0, 0, e]

    n_max = lax.fori_loop(0, N_EXPERTS, lambda e, m: jnp.maximum(m, count(e)), 0)
    rounds = (n_max + SLOT_CAP - 1) // SLOT_CAP

    route = route_ref[...]
    e0, e1, w0, w1 = route[:, 0:1], route[:, 1:2], route[:, 2:3], route[:, 3:4]
    lane = lax.broadcasted_iota(jnp.int32, (td, ROUTE_WIDTH), 1).astype(F32)
    onehot = jnp.concatenate([lane == e0, lane == e1], axis=0)
    onehot_f = jnp.where(onehot, 1.0, 0.0)
    rank_mat = jnp.dot(tri[...], onehot_f.astype(BF16), preferred_element_type=F32)
    rank = jnp.sum(jnp.where(onehot, rank_mat, 0.0), axis=1, keepdims=True)
    r0, r1 = rank[:td], rank[td:]
    cnt_row = jnp.sum(onehot_f, axis=0, keepdims=True)

    slot_col = lax.broadcasted_iota(jnp.int32, (N_SLOTS, 1), 0)
    slot_expert = slot_col // SLOT_CAP
    slot_rank = slot_col - slot_expert * SLOT_CAP
    lane_s = lax.broadcasted_iota(jnp.int32, (N_SLOTS, ROUTE_WIDTH), 1)
    n_col = jnp.sum(jnp.where(lane_s == slot_expert, cnt_row, 0.0), axis=1, keepdims=True)
    slot_lane = lax.broadcasted_iota(jnp.int32, (td, N_SLOTS), 1).astype(F32)

    def wait_copies(n):
        def wait_one(_, c):
            stage_copy(gcur_ref, 0, 0, buf).wait()
            return c
        lax.fori_loop(0, n, wait_one, 0)

    def weighted_rows(lo):
        lo_f = lo.astype(F32)
        live = (slot_rank + lo).astype(F32) < n_col
        rows = _unpack_rows(jnp.where(live, stage[buf], jnp.uint32(0)))

        def selector(e_col, r_col, w_col):
            in_round = (r_col >= lo_f) & (r_col < lo_f + SLOT_CAP)
            key = jnp.where(in_round, e_col * SLOT_CAP + (r_col - lo_f), -1.0)
            return jnp.where(slot_lane == key, w_col, 0.0)
        mix = (selector(e0, r0, w0) + selector(e1, r1, w1)).astype(BF16)
        return jnp.dot(mix, rows, preferred_element_type=F32)

    wait_copies(N_EXPERTS)

    @pl.when(i + 1 < pl.num_programs(0))
    def _():
        fetch_first_round(gnext_ref, 1 - buf)

    acc[...] = weighted_rows(jnp.int32(0))

    def later_round(r, carry):
        lo = r * SLOT_CAP

        def start(e, n_started):
            has_rows = count(e) > lo

            @pl.when(has_rows)
            def _():
                stage_copy(gcur_ref, e, lo, buf).start()
            return n_started + has_rows.astype(jnp.int32)
        wait_copies(lax.fori_loop(0, N_EXPERTS, start, 0))
        acc[...] = acc[...] + weighted_rows(lo)
        return carry
    lax.fori_loop(1, rounds, later_round, 0)

    y_ref[...] = _layer_norm(ALPHA * x1_ref[...] + acc[...], ln_g_ref[...], ln_b_ref[...])


def _combine_call(goff, x1_all, route_all, ys, ln_g, ln_b, first_row, n_rows, tile, n_tokens):
    assert first_row % DISPATCH_TILE == 0 and first_row % tile == 0
    assert tile == DISPATCH_TILE or n_rows == tile
    first_block = first_row // tile
    first_goff = first_row // DISPATCH_TILE
    n_assign = TOP_K * tile
    grid_spec = pl.GridSpec(
        grid=(n_rows // tile,),
        in_specs=[
            pl.BlockSpec((1, 1, ROUTE_WIDTH), lambda i: (first_goff + i, 0, 0), memory_space=pltpu.SMEM),
            pl.BlockSpec((1, 1, ROUTE_WIDTH), lambda i: (first_goff + i + 1, 0, 0), memory_space=pltpu.SMEM),
            pl.BlockSpec((tile, D_MODEL), lambda i: (first_block + i, 0)),
            pl.BlockSpec((tile, ROUTE_WIDTH), lambda i: (first_block + i, 0)),
            pl.BlockSpec(memory_space=pl.ANY),
            _const_spec(ln_g.shape),
            _const_spec(ln_b.shape),
        ],
        out_specs=pl.BlockSpec((tile, D_MODEL), lambda i: (i, 0)),
        scratch_shapes=[pltpu.VMEM((2, N_SLOTS, PACKED_WIDTH), jnp.uint32),
                        pltpu.VMEM((tile, D_MODEL), F32),
                        pltpu.VMEM((n_assign, n_assign), BF16),
                        pltpu.SemaphoreType.DMA((2,))],
    )
    return pl.pallas_call(
        functools.partial(_combine_kernel, seg_cap=_segment_capacity(n_tokens)),
        grid_spec=grid_spec,
        out_shape=jax.ShapeDtypeStruct((n_rows, D_MODEL), F32),
        compiler_params=pltpu.CompilerParams(dimension_semantics=("arbitrary",),
                                             vmem_limit_bytes=VMEM_LIMIT_BYTES),
        name="moe_combine",
    )(goff, goff, x1_all, route_all, ys, ln_g, ln_b)


def _prepare_weights(w_in, b_in, w_conv, b_conv, w_rg, b_rg, w_ig, b_ig, lru_lambda, w_lru_out, w_attn_out, w_o,
                     ln1_g, ln1_b, w_group, b_group, w_router, b_router):
    blocks_per_chunk = GATE_CHUNK // LRU_BLOCK

    def chunked_block_diag(w):
        w = w.reshape(N_GATE_CHUNKS, blocks_per_chunk, LRU_BLOCK, LRU_BLOCK)
        eye = jnp.eye(blocks_per_chunk, dtype=w.dtype)
        return jnp.einsum("cbij,bd->cbidj", w, eye).reshape(N_GATE_CHUNKS, GATE_CHUNK, GATE_CHUNK)

    w_gates = jnp.concatenate([chunked_block_diag(w_rg), chunked_block_diag(w_ig)], axis=-1).astype(BF16)
    w_rt = jnp.concatenate([w_group, w_router], axis=1)
    w_rt = jnp.pad(w_rt, ((0, 0), (0, ROUTE_WIDTH - w_rt.shape[1])))
    w_rt_hi = w_rt.astype(BF16)
    w_rt_lo = (w_rt - w_rt_hi.astype(F32)).astype(BF16)
    b_rt = jnp.pad(jnp.concatenate([b_group, b_router]), (0, ROUTE_WIDTH - N_GROUPS - N_EXPERTS))
    row = lambda v: v.reshape(1, -1)
    return dict(
        w_in=w_in.astype(BF16), b_in=row(b_in), w_conv=w_conv, b_conv=row(b_conv), w_gates=w_gates,
        b_rg=row(b_rg), b_ig=row(b_ig), lam=row(lru_lambda),
        w_out=jnp.stack([w_lru_out, w_attn_out, w_o]).astype(BF16),
        ln1_g=row(ln1_g), ln1_b=row(ln1_b), w_rt_hi=w_rt_hi, w_rt_lo=w_rt_lo, b_rt=row(b_rt))


def kernel(x_prompt, x_sample, cache_k, cache_v, state_conv, state_lru_h, w_in, b_in, w_conv, b_conv, w_rg, b_rg,
           w_ig, b_ig, lru_lambda, sinks, w_lru_out, w_attn_out, w_o, ln1_g, ln1_b, w_group, b_group, w_router,
           b_router, w_gate, w_up, w_down, ln2_g, ln2_b):
    B, S, _ = x_prompt.shape
    n_prompt = B * S
    n_sample = x_sample.shape[0]
    n_all = n_prompt + n_sample
    wts = _prepare_weights(w_in, b_in, w_conv, b_conv, w_rg, b_rg, w_ig, b_ig, lru_lambda, w_lru_out, w_attn_out,
                           w_o, ln1_g, ln1_b, w_group, b_group, w_router, b_router)

    x_s = x_sample.reshape(n_sample, D_MODEL)
    u_s = _sample_proj_call(x_s, wts["w_in"], wts["b_in"])
    q3 = u_s[:, OFF_Q:OFF_K].reshape(n_sample, N_HEADS, HEAD_DIM)
    k_new = u_s[:, OFF_K:OFF_V]
    v_new = u_s[:, OFF_V:OFF_GL]
    att3, k_win_s, v_win_s = _sample_attn_call(
        q3, k_new.reshape(n_sample, N_KV, HEAD_DIM), v_new.reshape(n_sample, N_KV, HEAD_DIM),
        k_new.reshape(n_sample, 1, KV_WIDTH), v_new.reshape(n_sample, 1, KV_WIDTH),
        cache_k.reshape(n_sample, WINDOW, KV_WIDTH), cache_v.reshape(n_sample, WINDOW, KV_WIDTH),
        sinks.reshape(N_KV, GROUP, 1))
    x1_s, conv_s_t, h_s = _sample_mix_call(
        x_s, u_s, att3.reshape(n_sample, N_HEADS * HEAD_DIM), jnp.transpose(state_conv, (1, 0, 2)), state_lru_h, wts)

    x1_all, route_all, k_win_p, v_win_p, conv_p, h_p, xs, goff, gtot = _mixer_call(x_prompt, x1_s, sinks, wts)

    ys = _moe_call(gtot, xs, w_gate, w_up, w_down, n_all)
    goff = jnp.concatenate([goff, gtot[None]], axis=0)
    ln2_g2, ln2_b2 = ln2_g.reshape(1, -1), ln2_b.reshape(1, -1)
    y_p = _combine_call(goff, x1_all, route_all, ys, ln2_g2, ln2_b2, 0, n_prompt, DISPATCH_TILE, n_all)
    y_s = _combine_call(goff, x1_all, route_all, ys, ln2_g2, ln2_b2, n_prompt, n_sample, n_sample, n_all)

    kv_shape = (WINDOW, N_KV, HEAD_DIM)
    return (y_p.reshape(B, S, D_MODEL), y_s.reshape(n_sample, 1, D_MODEL),
            k_win_p.reshape((B,) + kv_shape), v_win_p.reshape((B,) + kv_shape), conv_p, h_p.reshape(B, LRU_WIDTH),
            k_win_s.reshape((n_sample,) + kv_shape), v_win_s.reshape((n_sample,) + kv_shape),
            jnp.transpose(conv_s_t, (1, 0, 2)), h_s)
```

```python
import functools

import jax
import jax.numpy as jnp
from jax import lax
from jax.experimental import pallas as pl
from jax.experimental.pallas import tpu as pltpu

F32 = jnp.float32
BF16 = jnp.bfloat16

D_MODEL = 1024
LRU_WIDTH = 1024
LRU_BLOCK = 64
CONV_W = 4
LRU_C = 8.0
N_HEADS = 16
N_KV = 4
GROUP = N_HEADS // N_KV
HEAD_DIM = 64
KV_WIDTH = N_KV * HEAD_DIM
WINDOW = 128
NEG_INF = -1e30
N_GROUPS = 4
EXPERTS_PER_GROUP = 8
N_EXPERTS = N_GROUPS * EXPERTS_PER_GROUP
TOP_K = 2
EXPERT_FF = D_MODEL // 2
DEPTH = 1
ALPHA = (2 * DEPTH) ** 0.25
LN_EPS = 1e-5
ATTN_SCALE = HEAD_DIM ** -0.5
LOG2_E = 1.4426950408889634

OFF_XL = 0
OFF_YL = OFF_XL + LRU_WIDTH
OFF_Q = OFF_YL + LRU_WIDTH
OFF_K = OFF_Q + N_HEADS * HEAD_DIM
OFF_V = OFF_K + KV_WIDTH
OFF_GL = OFF_V + KV_WIDTH
OFF_GA = OFF_GL + D_MODEL
IN_WIDTH = OFF_GA + D_MODEL

LANES = 128
SUBLANES = 8
MXU_DIM = 256
VMEM_LIMIT_BYTES = 56 * 1024 * 1024

GATE_CHUNK = MXU_DIM
N_GATE_CHUNKS = LRU_WIDTH // GATE_CHUNK
ROUTE_WIDTH = LANES

SEQ_TILE = 256
MOE_TILE = 464
DISPATCH_TILE = 256
SLOT_CAP = 32
N_SLOTS = N_EXPERTS * SLOT_CAP
SEG_PAD = MOE_TILE + SLOT_CAP
PACKED_WIDTH = D_MODEL // 2
SAMPLE_ATTN_TILE = 16
SAMPLE_PROJ_TILE = 512


def _const_spec(shape):
    nd = len(shape)
    return pl.BlockSpec(shape, lambda *_: (0,) * nd)


def _layer_norm(z, g, b):
    mu = jnp.mean(z, axis=-1, keepdims=True)
    zc = z - mu
    var = jnp.mean(zc * zc, axis=-1, keepdims=True)
    return zc * lax.rsqrt(var + LN_EPS) * g + b


def _sigmoid(x):
    return 1.0 / (1.0 + jnp.exp2(x * -LOG2_E))


def _softplus(x):
    return jnp.maximum(x, 0.0) + jnp.log1p(jnp.exp(-jnp.abs(x)))


def _lru_gates(xc, w_gates_ref, b_rg, b_ig, lam):
    xcb = xc.astype(BF16)
    r_parts, i_parts = [], []
    for c in range(N_GATE_CHUNKS):
        g = jnp.dot(xcb[:, c * GATE_CHUNK:(c + 1) * GATE_CHUNK], w_gates_ref[c], preferred_element_type=F32)
        r_parts.append(g[:, :GATE_CHUNK])
        i_parts.append(g[:, GATE_CHUNK:])
    r = _sigmoid(jnp.concatenate(r_parts, axis=1) + b_rg)
    i = _sigmoid(jnp.concatenate(i_parts, axis=1) + b_ig)
    log_a = (-LRU_C * r) * _softplus(-lam)
    a = jnp.exp(log_a)
    gain_sq = 1.0 - a * a
    gain = jnp.where(gain_sq > 0.0, gain_sq * lax.rsqrt(gain_sq), 0.0)
    u = gain * (i * xc)
    return a, u


def _linear_scan(a, u, h_in):
    n, w = a.shape
    groups = n // SUBLANES
    a3 = a.reshape(groups, SUBLANES, w)
    u3 = u.reshape(groups, SUBLANES, w)
    row = lax.broadcasted_iota(jnp.int32, a3.shape, 1)
    d = 1
    while d < SUBLANES:
        has_prev = row >= d
        u3 = u3 + a3 * jnp.where(has_prev, pltpu.roll(u3, d, axis=1), 0.0)
        a3 = a3 * jnp.where(has_prev, pltpu.roll(a3, d, axis=1), 1.0)
        d *= 2
    carry = h_in
    out = []
    for g in range(groups):
        h_g = u3[g] + a3[g] * carry
        out.append(h_g)
        carry = h_g[SUBLANES - 1:SUBLANES, :]
    return jnp.concatenate(out, axis=0)


def _route(x1, w_hi_ref, w_lo_ref, b_rt):
    x_hi = x1.astype(BF16)
    x_lo = (x1 - x_hi.astype(F32)).astype(BF16)
    w_hi = w_hi_ref[...]
    logits = (jnp.dot(x_hi, w_hi, preferred_element_type=F32)
              + (jnp.dot(x_lo, w_hi, preferred_element_type=F32)
                 + jnp.dot(x_hi, w_lo_ref[...], preferred_element_type=F32))) + b_rt
    col = lax.broadcasted_iota(jnp.int32, logits.shape, 1)
    big = jnp.int32(ROUTE_WIDTH)
    is_g = col < N_GROUPS
    gl = jnp.where(is_g, logits, -jnp.inf)
    gmax = jnp.max(gl, axis=-1, keepdims=True)
    g_idx = jnp.min(jnp.where(gl == gmax, col, big), axis=-1, keepdims=True)
    p_g = 1.0 / jnp.sum(jnp.where(is_g, jnp.exp(gl - gmax), 0.0), axis=-1, keepdims=True)
    lo = N_GROUPS + g_idx * EXPERTS_PER_GROUP
    in_grp = (col >= lo) & (col < lo + EXPERTS_PER_GROUP)
    el = jnp.where(in_grp, logits, -jnp.inf)
    v1 = jnp.max(el, axis=-1, keepdims=True)
    i1 = jnp.min(jnp.where(el == v1, col, big), axis=-1, keepdims=True)
    el2 = jnp.where(col == i1, -jnp.inf, el)
    v2 = jnp.max(el2, axis=-1, keepdims=True)
    i2 = jnp.min(jnp.where(el2 == v2, col, big), axis=-1, keepdims=True)
    e21 = jnp.exp(v2 - v1)
    inv = 1.0 / (1.0 + e21)
    w1 = p_g * inv
    w2 = p_g * (e21 * inv)
    e1 = (i1 - N_GROUPS).astype(F32)
    e2 = (i2 - N_GROUPS).astype(F32)
    return jnp.where(col == 0, e1, jnp.where(col == 1, e2, jnp.where(col == 2, w1, jnp.where(col == 3, w2, 0.0))))


def _merge_norm(x, rec, att, g_l, g_a, w_out_ref, ln_g, ln_b):
    rec_o = jnp.dot(rec.astype(BF16), w_out_ref[0], preferred_element_type=F32)
    att_o = jnp.dot(att.astype(BF16), w_out_ref[1], preferred_element_type=F32)
    merged = _sigmoid(g_l) * rec_o + _sigmoid(g_a) * att_o
    mix = jnp.dot(merged.astype(BF16), w_out_ref[2], preferred_element_type=F32)
    return _layer_norm(ALPHA * x + mix, ln_g, ln_b)


def _mixer_kernel(sinks_ref, x_ref, w_in_ref, b_in_ref, w_conv_ref, b_conv_ref, w_gates_ref, b_rg_ref, b_ig_ref,
                  lam_ref, w_out_ref, ln_g_ref, ln_b_ref, w_rt_hi_ref, w_rt_lo_ref,
                  b_rt_ref, x1_s_ref,
                  x1_ref, route_ref, kwin_ref, vwin_ref, conv_ref, h_ref, xs_hbm, goff_ref, gtot_ref,
                  conv_buf, h_carry, kcat, vcat, att_buf, prev_x1, stage, zero_rows, tri, xb_keep, key_keep,
                  g_vmem, gvec, g_smem, nd_smem, rounds_smem, sem_stage, sem_g, sem_zero,
                  *, tiles_per_seq, n_tiles, seg_cap):
    step = pl.program_id(0)
    last_step = pl.num_programs(0) - 1
    buf = lax.rem(step, 2)
    n_sample = x1_s_ref.shape[0]
    disp = _Dispatcher(xs_hbm, goff_ref, gtot_ref, stage, zero_rows, tri, xb_keep, key_keep, g_vmem, gvec,
                       g_smem, nd_smem, rounds_smem, sem_stage, sem_g, sem_zero, seg_cap)
    n_valid = jnp.where(step == 0, 0, jnp.where(step == last_step, n_sample, SEQ_TILE))

    @pl.when(step == 0)
    def _():
        disp.init()
        prev_x1[...] = jnp.zeros_like(prev_x1)

    def route_and_plan():
        x1_prev = prev_x1[...]
        route = _route(x1_prev, w_rt_hi_ref, w_rt_lo_ref, b_rt_ref[...])
        route_ref[...] = route
        disp.plan(x1_prev, route, n_valid, buf)

    @pl.when(step < n_tiles)
    def _():
        @pl.when(lax.rem(step, tiles_per_seq) == 0)
        def _():
            conv_buf[...] = jnp.zeros_like(conv_buf)
            h_carry[...] = jnp.zeros_like(h_carry)
            kcat[:, 0:WINDOW] = jnp.zeros((KV_WIDTH, WINDOW), BF16)
            vcat[0:WINDOW, :] = jnp.zeros((WINDOW, KV_WIDTH), BF16)

        route_and_plan()
        _mixer_tile(lax.rem(step, tiles_per_seq), sinks_ref, x_ref, w_in_ref, b_in_ref, w_conv_ref, b_conv_ref,
                    w_gates_ref, b_rg_ref, b_ig_ref, lam_ref, w_out_ref, ln_g_ref,
                    ln_b_ref, x1_ref, prev_x1, kwin_ref, vwin_ref, conv_ref, h_ref,
                    conv_buf, h_carry, kcat, vcat, att_buf)

    @pl.when(step >= n_tiles)
    def _():
        route_and_plan()

        @pl.when(step == n_tiles)
        def _():
            x1_ref[0:n_sample, :] = x1_s_ref[...]
            prev_x1[0:n_sample, :] = x1_s_ref[...]

    disp.flush(buf, step == last_step)


def _mixer_tile(t, sinks_ref, x_ref, w_in_ref, b_in_ref, w_conv_ref, b_conv_ref, w_gates_ref, b_rg_ref, b_ig_ref,
                lam_ref, w_out_ref, ln_g_ref, ln_b_ref,
                x1_ref, x1_keep, kwin_ref, vwin_ref, conv_ref, h_ref,
                conv_buf, h_carry, kcat, vcat, att_buf):
    T = SEQ_TILE
    x = x_ref[0]
    xb = x.astype(BF16)

    def proj(lo, width):
        return jnp.dot(xb, w_in_ref[:, lo:lo + width], preferred_element_type=F32) + b_in_ref[:, lo:lo + width]

    xl = proj(OFF_XL, LRU_WIDTH)
    xl_ext = jnp.concatenate([conv_buf[...], xl], axis=0)

    def lagged(k):
        return pltpu.roll(xl_ext, k, axis=0)[SUBLANES:, :]
    wc = w_conv_ref[...]
    xc = wc[0:1] * lagged(3)
    xc = xc + wc[1:2] * lagged(2)
    xc = xc + wc[2:3] * lagged(1)
    xc = xc + wc[3:4] * xl + b_conv_ref[...]
    conv_ref[0] = xl[T - (CONV_W - 1):, :]
    conv_buf[...] = xl[T - SUBLANES:, :]

    a, u = _lru_gates(xc, w_gates_ref, b_rg_ref[...], b_ig_ref[...], lam_ref[...])
    h = _linear_scan(a, u, h_carry[0:1, :])
    h_last = h[T - 1:T, :]
    h_carry[0:1, :] = h_last
    h_ref[0] = h_last
    rec = h * jax.nn.gelu(proj(OFF_YL, LRU_WIDTH))

    q = proj(OFF_Q, N_HEADS * HEAD_DIM) * (ATTN_SCALE * LOG2_E)
    k = proj(OFF_K, KV_WIDTH)
    v = proj(OFF_V, KV_WIDTH)
    kwin_ref[0] = k[T - WINDOW:, :]
    vwin_ref[0] = v[T - WINDOW:, :]
    kcat[:, WINDOW:WINDOW + T] = k.T.astype(BF16)
    vcat[WINDOW:WINDOW + T, :] = v.astype(BF16)

    qi = lax.broadcasted_iota(jnp.int32, (WINDOW, 2 * WINDOW), 0)
    kj = lax.broadcasted_iota(jnp.int32, (WINDOW, 2 * WINDOW), 1)
    band = (kj > qi) & (kj <= qi + WINDOW)
    grp_row = lax.broadcasted_iota(jnp.int32, (GROUP * WINDOW, 1), 0) // WINDOW
    for qb in range(T // WINDOW):
        if qb == 0:
            first_key = jnp.where(t == 0, WINDOW, 0)
            mask1 = band & (kj >= first_key)
        else:
            mask1 = band
        bias = jnp.concatenate([jnp.where(mask1, 0.0, NEG_INF)] * GROUP, axis=0)
        r0 = qb * WINDOW
        qq = q[r0:r0 + WINDOW, :]
        for j in range(N_KV):
            kjb = kcat[j * HEAD_DIM:(j + 1) * HEAD_DIM, r0:r0 + 2 * WINDOW]
            vjb = vcat[r0:r0 + 2 * WINDOW, j * HEAD_DIM:(j + 1) * HEAD_DIM]
            qs = jnp.concatenate(
                [qq[:, (j * GROUP + g) * HEAD_DIM:(j * GROUP + g + 1) * HEAD_DIM] for g in range(GROUP)], axis=0)
            s = jnp.dot(qs.astype(BF16), kjb, preferred_element_type=F32) + bias
            sink = jnp.zeros((GROUP * WINDOW, 1), F32)
            for g in range(GROUP):
                sink = jnp.where(grp_row == g, sinks_ref[j * GROUP + g] * LOG2_E, sink)
            m = jnp.maximum(jnp.max(s, axis=-1, keepdims=True), sink)
            p = jnp.exp2(s - m)
            inv = 1.0 / (jnp.sum(p, axis=-1, keepdims=True) + jnp.exp2(sink - m))
            o = jnp.dot((p * inv).astype(BF16), vjb, preferred_element_type=F32)
            for g in range(GROUP):
                hcol = (j * GROUP + g) * HEAD_DIM
                att_buf[r0:r0 + WINDOW, hcol:hcol + HEAD_DIM] = o[g * WINDOW:(g + 1) * WINDOW, :]
    kcat[:, 0:WINDOW] = kcat[:, T:T + WINDOW]
    vcat[0:WINDOW, :] = vcat[T:T + WINDOW, :]

    x1 = _merge_norm(x, rec, att_buf[...], proj(OFF_GL, D_MODEL), proj(OFF_GA, D_MODEL),
                     w_out_ref, ln_g_ref[...], ln_b_ref[...])
    x1_ref[...] = x1
    x1_keep[...] = x1


def _mixer_call(x_prompt, x1_s, sinks, wts):
    B, S, _ = x_prompt.shape
    T = SEQ_TILE
    assert T == DISPATCH_TILE and x1_s.shape[0] <= T
    nt = S // T
    n_tiles = B * nt
    n_rows_total = B * S + x1_s.shape[0]
    n_dispatch = _dispatch_tiles(n_rows_total)
    assert n_dispatch == n_tiles + 1
    seg_cap = _segment_capacity(n_rows_total)
    n_assign = TOP_K * T
    weight_args = (wts["w_in"], wts["b_in"], wts["w_conv"], wts["b_conv"], wts["w_gates"], wts["b_rg"], wts["b_ig"],
                   wts["lam"], wts["w_out"], wts["ln1_g"], wts["ln1_b"],
                   wts["w_rt_hi"], wts["w_rt_lo"], wts["b_rt"], x1_s)
    mixed = lambda i: jnp.minimum(i, n_tiles - 1)
    seq = lambda i: mixed(i) // nt
    routed = lambda i: jnp.clip(i - 1, 0, n_dispatch - 1)
    in_specs = [pl.BlockSpec(memory_space=pltpu.SMEM),
                pl.BlockSpec((1, T, D_MODEL), lambda i: (seq(i), lax.rem(mixed(i), nt), 0))]
    in_specs += [_const_spec(w.shape) for w in weight_args]
    out_shape = (
        jax.ShapeDtypeStruct((n_rows_total, D_MODEL), F32),
        jax.ShapeDtypeStruct((n_rows_total, ROUTE_WIDTH), F32),
        jax.ShapeDtypeStruct((B, WINDOW, KV_WIDTH), F32),
        jax.ShapeDtypeStruct((B, WINDOW, KV_WIDTH), F32),
        jax.ShapeDtypeStruct((B, CONV_W - 1, LRU_WIDTH), F32),
        jax.ShapeDtypeStruct((B, 1, LRU_WIDTH), F32),
        jax.ShapeDtypeStruct((N_EXPERTS * seg_cap, PACKED_WIDTH), jnp.uint32),
        jax.ShapeDtypeStruct((n_dispatch, 1, ROUTE_WIDTH), jnp.int32),
        jax.ShapeDtypeStruct((1, ROUTE_WIDTH), jnp.int32),
    )
    out_specs = (
        pl.BlockSpec((T, D_MODEL), lambda i: (jnp.minimum(i, n_tiles), 0)),
        pl.BlockSpec((T, ROUTE_WIDTH), lambda i: (routed(i), 0)),
        pl.BlockSpec((1, WINDOW, KV_WIDTH), lambda i: (seq(i), 0, 0)),
        pl.BlockSpec((1, WINDOW, KV_WIDTH), lambda i: (seq(i), 0, 0)),
        pl.BlockSpec((1, CONV_W - 1, LRU_WIDTH), lambda i: (seq(i), 0, 0)),
        pl.BlockSpec((1, 1, LRU_WIDTH), lambda i: (seq(i), 0, 0)),
        pl.BlockSpec(memory_space=pl.ANY),
        pl.BlockSpec((1, 1, ROUTE_WIDTH), lambda i: (routed(i), 0, 0)),
        pl.BlockSpec((1, ROUTE_WIDTH), lambda i: (0, 0)),
    )
    scratch = [
        pltpu.VMEM((SUBLANES, LRU_WIDTH), F32),
        pltpu.VMEM((SUBLANES, LRU_WIDTH), F32),
        pltpu.VMEM((KV_WIDTH, T + WINDOW), BF16),
        pltpu.VMEM((T + WINDOW, KV_WIDTH), BF16),
        pltpu.VMEM((T, N_HEADS * HEAD_DIM), F32),
        pltpu.VMEM((T, D_MODEL), F32),
        pltpu.VMEM((2, N_SLOTS, PACKED_WIDTH), jnp.uint32),
        pltpu.VMEM((SEG_PAD, PACKED_WIDTH), jnp.uint32),
        pltpu.VMEM((n_assign, n_assign), BF16),
        pltpu.VMEM((T, D_MODEL), BF16),
        pltpu.VMEM((SUBLANES, n_assign), F32),
        pltpu.VMEM((SUBLANES, ROUTE_WIDTH), jnp.int32),
        pltpu.VMEM((SUBLANES, ROUTE_WIDTH), F32),
        pltpu.SMEM((1, ROUTE_WIDTH), jnp.int32),
        pltpu.SMEM((1,), jnp.int32),
        pltpu.SMEM((1,), jnp.int32),
        pltpu.SemaphoreType.DMA(()),
        pltpu.SemaphoreType.DMA(()),
        pltpu.SemaphoreType.DMA(()),
    ]
    return pl.pallas_call(
        functools.partial(_mixer_kernel, tiles_per_seq=nt, n_tiles=n_tiles, seg_cap=seg_cap),
        grid=(n_tiles + 2,),
        in_specs=in_specs,
        out_specs=out_specs,
        out_shape=out_shape,
        scratch_shapes=scratch,
        compiler_params=pltpu.CompilerParams(dimension_semantics=("arbitrary",),
                                             vmem_limit_bytes=VMEM_LIMIT_BYTES),
        name="mixer_prompt",
    )(sinks, x_prompt, *weight_args)


def _sample_proj_kernel(x_ref, w_ref, b_ref, u_ref):
    u_ref[...] = jnp.dot(x_ref[...].astype(BF16), w_ref[...], preferred_element_type=F32) + b_ref[...]


def _sample_proj_call(x_s, w_in, b_in):
    n = x_s.shape[0]
    tn = SAMPLE_PROJ_TILE
    return pl.pallas_call(
        _sample_proj_kernel,
        grid=(IN_WIDTH // tn,),
        in_specs=[pl.BlockSpec((n, D_MODEL), lambda c: (0, 0)),
                  pl.BlockSpec((D_MODEL, tn), lambda c: (0, c)),
                  pl.BlockSpec((1, tn), lambda c: (0, c))],
        out_specs=pl.BlockSpec((n, tn), lambda c: (0, c)),
        out_shape=jax.ShapeDtypeStruct((n, IN_WIDTH), F32),
        compiler_params=pltpu.CompilerParams(dimension_semantics=("arbitrary",)),
        name="sample_proj",
    )(x_s, w_in, b_in)


def _sample_attn_kernel(q_ref, kn_ref, vn_ref, kn_row_ref, vn_row_ref, ck_ref, cv_ref, sinks_ref,
                        att_ref, kwin_ref, vwin_ref):
    tb = q_ref.shape[0]
    key_pos = lax.broadcasted_iota(jnp.int32, (tb, GROUP, WINDOW), 2)
    for j in range(N_KV):
        qj = q_ref[:, j * GROUP:(j + 1) * GROUP, :]
        kc = ck_ref[:, :, j * HEAD_DIM:(j + 1) * HEAD_DIM]
        vc = cv_ref[:, :, j * HEAD_DIM:(j + 1) * HEAD_DIM]
        s_c = jnp.einsum("bgd,bsd->bgs", qj.astype(BF16), kc.astype(BF16), preferred_element_type=F32) * ATTN_SCALE
        s_c = jnp.where(key_pos >= 1, s_c, NEG_INF)
        kn = kn_ref[:, j:j + 1, :]
        vn = vn_ref[:, j:j + 1, :]
        s_n = jnp.sum(qj * kn, axis=-1, keepdims=True) * ATTN_SCALE
        sink = sinks_ref[j][None]
        m = jnp.maximum(jnp.maximum(jnp.max(s_c, axis=-1, keepdims=True), s_n), sink)
        p_c = jnp.exp(s_c - m)
        p_n = jnp.exp(s_n - m)
        inv = 1.0 / (jnp.sum(p_c, axis=-1, keepdims=True) + p_n + jnp.exp(sink - m))
        o = jnp.einsum("bgs,bsd->bgd", (p_c * inv).astype(BF16), vc.astype(BF16), preferred_element_type=F32)
        att_ref[:, j * GROUP:(j + 1) * GROUP, :] = o + (p_n * inv) * vn
    kwin_ref[:, 0:WINDOW - 1, :] = ck_ref[:, 1:WINDOW, :]
    kwin_ref[:, WINDOW - 1:WINDOW, :] = kn_row_ref[...]
    vwin_ref[:, 0:WINDOW - 1, :] = cv_ref[:, 1:WINDOW, :]
    vwin_ref[:, WINDOW - 1:WINDOW, :] = vn_row_ref[...]


def _sample_attn_call(q3, kn3, vn3, kn_row, vn_row, ck, cv, sinks3):
    n = q3.shape[0]
    tb = SAMPLE_ATTN_TILE
    b3 = lambda i: (i, 0, 0)
    return pl.pallas_call(
        _sample_attn_kernel,
        grid=(n // tb,),
        in_specs=[pl.BlockSpec((tb, N_HEADS, HEAD_DIM), b3),
                  pl.BlockSpec((tb, N_KV, HEAD_DIM), b3),
                  pl.BlockSpec((tb, N_KV, HEAD_DIM), b3),
                  pl.BlockSpec((tb, 1, KV_WIDTH), b3),
                  pl.BlockSpec((tb, 1, KV_WIDTH), b3),
                  pl.BlockSpec((tb, WINDOW, KV_WIDTH), b3),
                  pl.BlockSpec((tb, WINDOW, KV_WIDTH), b3),
                  pl.BlockSpec((N_KV, GROUP, 1), lambda i: (0, 0, 0))],
        out_specs=(pl.BlockSpec((tb, N_HEADS, HEAD_DIM), b3),
                   pl.BlockSpec((tb, WINDOW, KV_WIDTH), b3),
                   pl.BlockSpec((tb, WINDOW, KV_WIDTH), b3)),
        out_shape=(jax.ShapeDtypeStruct((n, N_HEADS, HEAD_DIM), F32),
                   jax.ShapeDtypeStruct((n, WINDOW, KV_WIDTH), F32),
                   jax.ShapeDtypeStruct((n, WINDOW, KV_WIDTH), F32)),
        compiler_params=pltpu.CompilerParams(dimension_semantics=("arbitrary",)),
        name="sample_attn",
    )(q3, kn3, vn3, kn_row, vn_row, ck, cv, sinks3)


def _sample_mix_kernel(x_ref, u_ref, att_ref, st_ref, h0_ref, w_conv_ref, b_conv_ref, w_gates_ref, b_rg_ref,
                       b_ig_ref, lam_ref, w_out_ref, ln_g_ref, ln_b_ref,
                       x1_ref, conv_ref, h_ref):
    xl = u_ref[:, OFF_XL:OFF_XL + LRU_WIDTH]
    wc = w_conv_ref[...]
    xc = wc[0:1] * st_ref[0]
    xc = xc + wc[1:2] * st_ref[1]
    xc = xc + wc[2:3] * st_ref[2]
    xc = xc + wc[3:4] * xl + b_conv_ref[...]
    conv_ref[0] = st_ref[1]
    conv_ref[1] = st_ref[2]
    conv_ref[2] = xl
    a, u = _lru_gates(xc, w_gates_ref, b_rg_ref[...], b_ig_ref[...], lam_ref[...])
    h = a * h0_ref[...] + u
    h_ref[...] = h
    rec = h * jax.nn.gelu(u_ref[:, OFF_YL:OFF_YL + LRU_WIDTH])
    x1_ref[...] = _merge_norm(x_ref[...], rec, att_ref[...], u_ref[:, OFF_GL:OFF_GL + D_MODEL],
                              u_ref[:, OFF_GA:OFF_GA + D_MODEL], w_out_ref,
                              ln_g_ref[...], ln_b_ref[...])


def _sample_mix_call(x_s, u_s, att, st_t, h0, wts):
    n = x_s.shape[0]
    weight_args = (wts["w_conv"], wts["b_conv"], wts["w_gates"], wts["b_rg"], wts["b_ig"], wts["lam"],
                   wts["w_out"], wts["ln1_g"], wts["ln1_b"])
    args = (x_s, u_s, att, st_t, h0) + weight_args
    out_shapes = ((n, D_MODEL), (CONV_W - 1, n, LRU_WIDTH), (n, LRU_WIDTH))
    return pl.pallas_call(
        _sample_mix_kernel,
        grid=(1,),
        in_specs=[_const_spec(a.shape) for a in args],
        out_specs=tuple(_const_spec(s) for s in out_shapes),
        out_shape=tuple(jax.ShapeDtypeStruct(s, F32) for s in out_shapes),
        compiler_params=pltpu.CompilerParams(dimension_semantics=("arbitrary",),
                                             vmem_limit_bytes=VMEM_LIMIT_BYTES),
        name="sample_mix",
    )(*args)


def _one_hot(mask):
    return jnp.where(mask, 1.0, 0.0).astype(BF16)


def _pack_rows(x):
    half = x.shape[1] // 2
    lo = lax.shift_right_logical(lax.bitcast_convert_type(x[:, :half], jnp.uint32), jnp.uint32(16))
    hi = lax.bitcast_convert_type(x[:, half:], jnp.uint32) & jnp.uint32(0xFFFF0000)
    return lo | hi


def _unpack_rows(words):
    lo = lax.bitcast_convert_type(lax.shift_left(words, jnp.uint32(16)), F32)
    hi = lax.bitcast_convert_type(words & jnp.uint32(0xFFFF0000), F32)
    return jnp.concatenate([lo.astype(BF16), hi.astype(BF16)], axis=1)


class _Dispatcher:
    def __init__(self, xs_hbm, goff_ref, gtot_ref, stage, zero_rows, tri, xb_keep, key_keep, g_vmem, gvec,
                 g_smem, nd_smem, rounds_smem, sem_stage, sem_g, sem_zero, seg_cap):
        self.xs_hbm, self.goff_ref, self.gtot_ref = xs_hbm, goff_ref, gtot_ref
        self.stage, self.zero_rows, self.tri = stage, zero_rows, tri
        self.xb_keep, self.key_keep = xb_keep, key_keep
        self.g_vmem, self.gvec, self.g_smem = g_vmem, gvec, g_smem
        self.nd_smem, self.rounds_smem = nd_smem, rounds_smem
        self.sem_stage, self.sem_g, self.sem_zero = sem_stage, sem_g, sem_zero
        self.seg_cap = seg_cap

    def _g_copy(self):
        return pltpu.make_async_copy(self.g_vmem.at[pl.ds(0, 1)], self.g_smem, self.sem_g)

    def init(self):
        n_assign = self.tri.shape[0]
        r = lax.broadcasted_iota(jnp.int32, (n_assign, n_assign), 0)
        c = lax.broadcasted_iota(jnp.int32, (n_assign, n_assign), 1)
        self.tri[...] = _one_hot(r < c)
        self.gvec[...] = jnp.zeros_like(self.gvec)
        self.g_vmem[...] = jnp.zeros_like(self.g_vmem)
        self.nd_smem[0] = 0
        self._g_copy().start()

    def _sorted_rows(self, e_row, rank_row, xb, lo):
        td = xb.shape[0]
        slot = lax.broadcasted_iota(jnp.int32, (N_SLOTS, td), 0).astype(F32)
        lo_f = lo.astype(F32)
        in_round = (rank_row >= lo_f) & (rank_row < lo_f + SLOT_CAP) & (e_row >= 0.0)
        key = jnp.where(in_round, e_row * SLOT_CAP + (rank_row - lo_f), -1.0)
        perm = jnp.where(slot == key[:, :td], 1.0, jnp.where(slot == key[:, td:], 1.0, 0.0)).astype(BF16)
        return _pack_rows(jnp.dot(perm, xb, preferred_element_type=F32))

    def plan(self, x1, route, n_valid, buf):
        td = x1.shape[0]
        n_assign = TOP_K * td
        valid_col = lax.broadcasted_iota(jnp.int32, (td, 1), 0) < n_valid
        lane = lax.broadcasted_iota(jnp.int32, (td, ROUTE_WIDTH), 1).astype(F32)
        e0c = jnp.where(valid_col, route[:, 0:1], -1.0)
        e1c = jnp.where(valid_col, route[:, 1:2], -1.0)
        cnt_row = jnp.sum(jnp.where(lane == e0c, 1.0, 0.0) + jnp.where(lane == e1c, 1.0, 0.0),
                          axis=0, keepdims=True)
        self.rounds_smem[0] = (jnp.max(cnt_row).astype(jnp.int32) + SLOT_CAP - 1) // SLOT_CAP
        self.goff_ref[0] = self.gvec[0:1, :].astype(jnp.int32)
        self.gvec[0:1, :] = self.gvec[0:1, :] + jnp.ceil(cnt_row * (1.0 / SUBLANES)) * SUBLANES
        self.gtot_ref[...] = self.gvec[0:1, :].astype(jnp.int32)

        route_t = route.T
        valid_row = lax.broadcasted_iota(jnp.int32, (1, td), 1) < n_valid
        e_row = jnp.concatenate([jnp.where(valid_row, route_t[0:1, :], -1.0),
                                 jnp.where(valid_row, route_t[1:2, :], -1.0)], axis=1)
        expert_sub = lax.broadcasted_iota(jnp.int32, (N_EXPERTS, n_assign), 0).astype(F32)
        onehot_t = expert_sub == e_row
        rank_t = jnp.dot(_one_hot(onehot_t), self.tri[...], preferred_element_type=F32)
        rank_row = jnp.sum(jnp.where(onehot_t, rank_t, 0.0), axis=0, keepdims=True)
        xb = jnp.where(valid_col, x1, 0.0).astype(BF16)
        self.xb_keep[...] = xb
        self.key_keep[0:1, :] = e_row
        self.key_keep[1:2, :] = rank_row
        self.stage[buf] = self._sorted_rows(e_row, rank_row, xb, jnp.int32(0))

    def _stage_copy(self, buf, e, dst_row):
        src = self.stage.at[buf, pl.ds(pl.multiple_of(e * SLOT_CAP, SLOT_CAP), SLOT_CAP)]
        return pltpu.make_async_copy(src, self.xs_hbm.at[pl.ds(dst_row, SLOT_CAP)], self.sem_stage)

    def _wait_outstanding(self):
        def wait_one(_, carry):
            self._stage_copy(0, 0, 0).wait()
            return carry
        lax.fori_loop(0, self.nd_smem[0], wait_one, 0)
        self.nd_smem[0] = 0

    def _start_round(self, buf, lo):
        def issue(e, carry):
            dst_row = pl.multiple_of(e * self.seg_cap + self.g_smem[0, e] + lo, SUBLANES)
            self._stage_copy(buf, e, dst_row).start()
            return carry
        lax.fori_loop(0, N_EXPERTS, issue, 0)
        self.nd_smem[0] = N_EXPERTS

    def flush(self, buf, is_last):
        self._g_copy().wait()
        self._wait_outstanding()
        self._start_round(buf, 0)

        def later_round(r, carry):
            lo = r * SLOT_CAP
            rows = self._sorted_rows(self.key_keep[0:1, :], self.key_keep[1:2, :], self.xb_keep[...], lo)
            self._wait_outstanding()
            self.stage[buf] = rows
            self._start_round(buf, lo)
            return carry
        lax.fori_loop(1, self.rounds_smem[0], later_round, 0)

        self.g_vmem[0:1, :] = self.gvec[0:1, :].astype(jnp.int32)
        self._g_copy().start()

        @pl.when(is_last)
        def _():
            self._g_copy().wait()
            self._wait_outstanding()
            self.zero_rows[...] = jnp.zeros_like(self.zero_rows)

            def pad_copy(e):
                dst_row = pl.multiple_of(e * self.seg_cap + self.g_smem[0, e], SUBLANES)
                return pltpu.make_async_copy(self.zero_rows, self.xs_hbm.at[pl.ds(dst_row, SEG_PAD)], self.sem_zero)

            def start(e, carry):
                pad_copy(e).start()
                return carry
            lax.fori_loop(0, N_EXPERTS, start, 0)

            def wait(e, carry):
                pad_copy(e).wait()
                return carry
            lax.fori_loop(0, N_EXPERTS, wait, 0)


def _dispatch_tiles(n_tokens):
    return (n_tokens + DISPATCH_TILE - 1) // DISPATCH_TILE


def _segment_capacity(n_tokens):
    alignment_slack = (SUBLANES - 1) * _dispatch_tiles(n_tokens)
    return (n_tokens + alignment_slack + SEG_PAD + MOE_TILE - 1) // MOE_TILE * MOE_TILE


def _moe_kernel(blk_e_ref, blk_j_ref, n_used_ref, xs_ref, wg_ref, wu_ref, wd_ref, ys_ref, wg_bf, wu_bf, wd_bf):
    del blk_j_ref
    b = pl.program_id(0)

    @pl.when(b < n_used_ref[0])
    def _():
        prev_e = blk_e_ref[jnp.maximum(b - 1, 0)]

        @pl.when((b == 0) | (blk_e_ref[b] != prev_e))
        def _():
            wg_bf[...] = wg_ref[0].astype(BF16)
            wu_bf[...] = wu_ref[0].astype(BF16)
            wd_bf[...] = wd_ref[0].astype(BF16)

        xb = _unpack_rows(xs_ref[...])
        g = jnp.dot(xb, wg_bf[...], preferred_element_type=F32)
        u = jnp.dot(xb, wu_bf[...], preferred_element_type=F32)
        hmid = (g * _sigmoid(g)) * u
        y = jnp.dot(hmid.astype(BF16), wd_bf[...], preferred_element_type=F32)
        ys_ref[...] = _pack_rows(y.astype(BF16).astype(F32))


def _expert_blocks(gtot, n_blocks):
    rows = gtot[0, :N_EXPERTS]
    nb = (rows + SLOT_CAP + MOE_TILE - 1) // MOE_TILE
    ends = jnp.cumsum(nb)
    n_used = ends[-1]
    step = jnp.minimum(jnp.arange(n_blocks, dtype=jnp.int32), n_used - 1)
    blk_e = jnp.sum((step[:, None] >= ends[None, :]).astype(jnp.int32), axis=1)
    blk_j = step - (ends - nb)[blk_e]
    return blk_e, blk_j.astype(jnp.int32), n_used.reshape(1).astype(jnp.int32)


def _moe_call(gtot, xs, w_gate, w_up, w_down, n_tokens):
    seg_cap = _segment_capacity(n_tokens)
    seg_blocks = seg_cap // MOE_TILE
    max_rows = n_tokens * TOP_K + N_EXPERTS * (SUBLANES - 1) * _dispatch_tiles(n_tokens)
    n_blocks = (max_rows + N_EXPERTS * (SLOT_CAP + MOE_TILE - 1)) // MOE_TILE
    blk_e, blk_j, n_used = _expert_blocks(gtot, n_blocks)
    row_block = lambda b, be, bj, nu: (be[b] * seg_blocks + bj[b], 0)
    weight_block = lambda b, be, bj, nu: (be[b], 0, 0)
    grid_spec = pltpu.PrefetchScalarGridSpec(
        num_scalar_prefetch=3,
        grid=(n_blocks,),
        in_specs=[
            pl.BlockSpec((MOE_TILE, PACKED_WIDTH), row_block),
            pl.BlockSpec((1, D_MODEL, EXPERT_FF), weight_block),
            pl.BlockSpec((1, D_MODEL, EXPERT_FF), weight_block),
            pl.BlockSpec((1, EXPERT_FF, D_MODEL), weight_block),
        ],
        out_specs=pl.BlockSpec((MOE_TILE, PACKED_WIDTH), row_block),
        scratch_shapes=[
            pltpu.VMEM((D_MODEL, EXPERT_FF), BF16),
            pltpu.VMEM((D_MODEL, EXPERT_FF), BF16),
            pltpu.VMEM((EXPERT_FF, D_MODEL), BF16),
        ],
    )
    return pl.pallas_call(
        _moe_kernel,
        grid_spec=grid_spec,
        out_shape=jax.ShapeDtypeStruct(xs.shape, xs.dtype),
        compiler_params=pltpu.CompilerParams(dimension_semantics=("arbitrary",),
                                             vmem_limit_bytes=VMEM_LIMIT_BYTES),
        name="moe_experts",
    )(blk_e, blk_j, n_used, xs, w_gate, w_up, w_down)


def _combine_kernel(gcur_ref, gnext_ref, x1_ref, route_ref, ys_hbm, ln_g_ref, ln_b_ref, y_ref,
                    stage, acc, tri, sem, *, seg_cap):
    td = x1_ref.shape[0]
    n_assign = TOP_K * td
    i = pl.program_id(0)
    buf = lax.rem(i, 2)

    def stage_copy(g_ref, e, first_rank, to_buf):
        src_row = pl.multiple_of(e * seg_cap + g_ref[0, 0, e] + first_rank, SUBLANES)
        dst = stage.at[to_buf, pl.ds(pl.multiple_of(e * SLOT_CAP, SLOT_CAP), SLOT_CAP)]
        return pltpu.make_async_copy(ys_hbm.at[pl.ds(src_row, SLOT_CAP)], dst, sem.at[to_buf])

    def fetch_first_round(g_ref, to_buf):
        def start(e, carry):
            stage_copy(g_ref, e, 0, to_buf).start()
            return carry
        lax.fori_loop(0, N_EXPERTS, start, 0)

    @pl.when(i == 0)
    def _():
        r = lax.broadcasted_iota(jnp.int32, (n_assign, n_assign), 0)
        c = lax.broadcasted_iota(jnp.int32, (n_assign, n_assign), 1)
        tri[...] = _one_hot(c < r)
        fetch_first_round(gcur_ref, 0)

    def count(e):
        return gnext_ref[0, 0, e] - gcur_ref[0, 0, e]

    n_max = lax.fori_loop(0, N_EXPERTS, lambda e, m: jnp.maximum(m, count(e)), 0)
    rounds = (n_max + SLOT_CAP - 1) // SLOT_CAP

    route = route_ref[...]
    e0, e1, w0, w1 = route[:, 0:1], route[:, 1:2], route[:, 2:3], route[:, 3:4]
    lane = lax.broadcasted_iota(jnp.int32, (td, ROUTE_WIDTH), 1).astype(F32)
    onehot = jnp.concatenate([lane == e0, lane == e1], axis=0)
    onehot_f = jnp.where(onehot, 1.0, 0.0)
    rank_mat = jnp.dot(tri[...], onehot_f.astype(BF16), preferred_element_type=F32)
    rank = jnp.sum(jnp.where(onehot, rank_mat, 0.0), axis=1, keepdims=True)
    r0, r1 = rank[:td], rank[td:]
    cnt_row = jnp.sum(onehot_f, axis=0, keepdims=True)

    slot_col = lax.broadcasted_iota(jnp.int32, (N_SLOTS, 1), 0)
    slot_expert = slot_col // SLOT_CAP
    slot_rank = slot_col - slot_expert * SLOT_CAP
    lane_s = lax.broadcasted_iota(jnp.int32, (N_SLOTS, ROUTE_WIDTH), 1)
    n_col = jnp.sum(jnp.where(lane_s == slot_expert, cnt_row, 0.0), axis=1, keepdims=True)
    slot_lane = lax.broadcasted_iota(jnp.int32, (td, N_SLOTS), 1).astype(F32)

    def wait_copies(n):
        def wait_one(_, c):
            stage_copy(gcur_ref, 0, 0, buf).wait()
            return c
        lax.fori_loop(0, n, wait_one, 0)

    def weighted_rows(lo):
        lo_f = lo.astype(F32)
        live = (slot_rank + lo).astype(F32) < n_col
        rows = _unpack_rows(jnp.where(live, stage[buf], jnp.uint32(0)))

        def selector(e_col, r_col, w_col):
            in_round = (r_col >= lo_f) & (r_col < lo_f + SLOT_CAP)
            key = jnp.where(in_round, e_col * SLOT_CAP + (r_col - lo_f), -1.0)
            return jnp.where(slot_lane == key, w_col, 0.0)
        mix = (selector(e0, r0, w0) + selector(e1, r1, w1)).astype(BF16)
        return jnp.dot(mix, rows, preferred_element_type=F32)

    wait_copies(N_EXPERTS)

    @pl.when(i + 1 < pl.num_programs(0))
    def _():
        fetch_first_round(gnext_ref, 1 - buf)

    acc[...] = weighted_rows(jnp.int32(0))

    def later_round(r, carry):
        lo = r * SLOT_CAP

        def start(e, n_started):
            has_rows = count(e) > lo

            @pl.when(has_rows)
            def _():
                stage_copy(gcur_ref, e, lo, buf).start()
            return n_started + has_rows.astype(jnp.int32)
        wait_copies(lax.fori_loop(0, N_EXPERTS, start, 0))
        acc[...] = acc[...] + weighted_rows(lo)
        return carry
    lax.fori_loop(1, rounds, later_round, 0)

    y_ref[...] = _layer_norm(ALPHA * x1_ref[...] + acc[...], ln_g_ref[...], ln_b_ref[...])


def _combine_call(goff, x1_all, route_all, ys, ln_g, ln_b, first_row, n_rows, tile, n_tokens):
    assert first_row % DISPATCH_TILE == 0 and first_row % tile == 0
    assert tile == DISPATCH_TILE or n_rows == tile
    first_block = first_row // tile
    first_goff = first_row // DISPATCH_TILE
    n_assign = TOP_K * tile
    grid_spec = pl.GridSpec(
        grid=(n_rows // tile,),
        in_specs=[
            pl.BlockSpec((1, 1, ROUTE_WIDTH), lambda i: (first_goff + i, 0, 0), memory_space=pltpu.SMEM),
            pl.BlockSpec((1, 1, ROUTE_WIDTH), lambda i: (first_goff + i + 1, 0, 0), memory_space=pltpu.SMEM),
            pl.BlockSpec((tile, D_MODEL), lambda i: (first_block + i, 0)),
            pl.BlockSpec((tile, ROUTE_WIDTH), lambda i: (first_block + i, 0)),
            pl.BlockSpec(memory_space=pl.ANY),
            _const_spec(ln_g.shape),
            _const_spec(ln_b.shape),
        ],
        out_specs=pl.BlockSpec((tile, D_MODEL), lambda i: (i, 0)),
        scratch_shapes=[pltpu.VMEM((2, N_SLOTS, PACKED_WIDTH), jnp.uint32),
                        pltpu.VMEM((tile, D_MODEL), F32),
                        pltpu.VMEM((n_assign, n_assign), BF16),
                        pltpu.SemaphoreType.DMA((2,))],
    )
    return pl.pallas_call(
        functools.partial(_combine_kernel, seg_cap=_segment_capacity(n_tokens)),
        grid_spec=grid_spec,
        out_shape=jax.ShapeDtypeStruct((n_rows, D_MODEL), F32),
        compiler_params=pltpu.CompilerParams(dimension_semantics=("arbitrary",),
                                             vmem_limit_bytes=VMEM_LIMIT_BYTES),
        name="moe_combine",
    )(goff, goff, x1_all, route_all, ys, ln_g, ln_b)


def _prepare_weights(w_in, b_in, w_conv, b_conv, w_rg, b_rg, w_ig, b_ig, lru_lambda, w_lru_out, w_attn_out, w_o,
                     ln1_g, ln1_b, w_group, b_group, w_router, b_router):
    blocks_per_chunk = GATE_CHUNK // LRU_BLOCK

    def chunked_block_diag(w):
        w = w.reshape(N_GATE_CHUNKS, blocks_per_chunk, LRU_BLOCK, LRU_BLOCK)
        eye = jnp.eye(blocks_per_chunk, dtype=w.dtype)
        return jnp.einsum("cbij,bd->cbidj", w, eye).reshape(N_GATE_CHUNKS, GATE_CHUNK, GATE_CHUNK)

    w_gates = jnp.concatenate([chunked_block_diag(w_rg), chunked_block_diag(w_ig)], axis=-1).astype(BF16)
    w_rt = jnp.concatenate([w_group, w_router], axis=1)
    w_rt = jnp.pad(w_rt, ((0, 0), (0, ROUTE_WIDTH - w_rt.shape[1])))
    w_rt_hi = w_rt.astype(BF16)
    w_rt_lo = (w_rt - w_rt_hi.astype(F32)).astype(BF16)
    b_rt = jnp.pad(jnp.concatenate([b_group, b_router]), (0, ROUTE_WIDTH - N_GROUPS - N_EXPERTS))
    row = lambda v: v.reshape(1, -1)
    return dict(
        w_in=w_in.astype(BF16), b_in=row(b_in), w_conv=w_conv, b_conv=row(b_conv), w_gates=w_gates,
        b_rg=row(b_rg), b_ig=row(b_ig), lam=row(lru_lambda),
        w_out=jnp.stack([w_lru_out, w_attn_out, w_o]).astype(BF16),
        ln1_g=row(ln1_g), ln1_b=row(ln1_b), w_rt_hi=w_rt_hi, w_rt_lo=w_rt_lo, b_rt=row(b_rt))


def kernel(x_prompt, x_sample, cache_k, cache_v, state_conv, state_lru_h, w_in, b_in, w_conv, b_conv, w_rg, b_rg,
           w_ig, b_ig, lru_lambda, sinks, w_lru_out, w_attn_out, w_o, ln1_g, ln1_b, w_group, b_group, w_router,
           b_router, w_gate, w_up, w_down, ln2_g, ln2_b):
    B, S, _ = x_prompt.shape
    n_prompt = B * S
    n_sample = x_sample.shape[0]
    n_all = n_prompt + n_sample
    wts = _prepare_weights(w_in, b_in, w_conv, b_conv, w_rg, b_rg, w_ig, b_ig, lru_lambda, w_lru_out, w_attn_out,
                           w_o, ln1_g, ln1_b, w_group, b_group, w_router, b_router)

    x_s = x_sample.reshape(n_sample, D_MODEL)
    u_s = _sample_proj_call(x_s, wts["w_in"], wts["b_in"])
    q3 = u_s[:, OFF_Q:OFF_K].reshape(n_sample, N_HEADS, HEAD_DIM)
    k_new = u_s[:, OFF_K:OFF_V]
    v_new = u_s[:, OFF_V:OFF_GL]
    att3, k_win_s, v_win_s = _sample_attn_call(
        q3, k_new.reshape(n_sample, N_KV, HEAD_DIM), v_new.reshape(n_sample, N_KV, HEAD_DIM),
        k_new.reshape(n_sample, 1, KV_WIDTH), v_new.reshape(n_sample, 1, KV_WIDTH),
        cache_k.reshape(n_sample, WINDOW, KV_WIDTH), cache_v.reshape(n_sample, WINDOW, KV_WIDTH),
        sinks.reshape(N_KV, GROUP, 1))
    x1_s, conv_s_t, h_s = _sample_mix_call(
        x_s, u_s, att3.reshape(n_sample, N_HEADS * HEAD_DIM), jnp.transpose(state_conv, (1, 0, 2)), state_lru_h, wts)

    x1_all, route_all, k_win_p, v_win_p, conv_p, h_p, xs, goff, gtot = _mixer_call(x_prompt, x1_s, sinks, wts)

    ys = _moe_call(gtot, xs, w_gate, w_up, w_down, n_all)
    goff = jnp.concatenate([goff, gtot[None]], axis=0)
    ln2_g2, ln2_b2 = ln2_g.reshape(1, -1), ln2_b.reshape(1, -1)
    y_p = _combine_call(goff, x1_all, route_all, ys, ln2_g2, ln2_b2, 0, n_prompt, DISPATCH_TILE, n_all)
    y_s = _combine_call(goff, x1_all, route_all, ys, ln2_g2, ln2_b2, n_prompt, n_sample, n_sample, n_all)

    kv_shape = (WINDOW, N_KV, HEAD_DIM)
    return (y_p.reshape(B, S, D_MODEL), y_s.reshape(n_sample, 1, D_MODEL),
            k_win_p.reshape((B,) + kv_shape), v_win_p.reshape((B,) + kv_shape), conv_p, h_p.reshape(B, LRU_WIDTH),
            k_win_s.reshape((n_sample,) + kv_shape), v_win_s.reshape((n_sample,) + kv_shape),
            jnp.transpose(conv_s_t, (1, 0, 2)), h_s)
```

```python
import functools

import jax
import jax.numpy as jnp
from jax import lax
from jax.experimental import pallas as pl
from jax.experimental.pallas import tpu as pltpu

F32 = jnp.float32
BF16 = jnp.bfloat16

D_MODEL = 1024
LRU_WIDTH = 1024
LRU_BLOCK = 64
CONV_W = 4
LRU_C = 8.0
N_HEADS = 16
N_KV = 4
GROUP = N_HEADS // N_KV
HEAD_DIM = 64
KV_WIDTH = N_KV * HEAD_DIM
WINDOW = 128
NEG_INF = -1e30
N_GROUPS = 4
EXPERTS_PER_GROUP = 8
N_EXPERTS = N_GROUPS * EXPERTS_PER_GROUP
TOP_K = 2
EXPERT_FF = D_MODEL // 2
DEPTH = 1
ALPHA = (2 * DEPTH) ** 0.25
LN_EPS = 1e-5
ATTN_SCALE = HEAD_DIM ** -0.5
LOG2_E = 1.4426950408889634

OFF_XL = 0
OFF_YL = OFF_XL + LRU_WIDTH
OFF_Q = OFF_YL + LRU_WIDTH
OFF_K = OFF_Q + N_HEADS * HEAD_DIM
OFF_V = OFF_K + KV_WIDTH
OFF_GL = OFF_V + KV_WIDTH
OFF_GA = OFF_GL + D_MODEL
IN_WIDTH = OFF_GA + D_MODEL

LANES = 128
SUBLANES = 8
MXU_DIM = 256
VMEM_LIMIT_BYTES = 56 * 1024 * 1024

GATE_CHUNK = MXU_DIM
N_GATE_CHUNKS = LRU_WIDTH // GATE_CHUNK
ROUTE_WIDTH = LANES

SEQ_TILE = 256
MOE_TILE = 1392
DISPATCH_TILE = 256
SLOT_CAP = 32
N_SLOTS = N_EXPERTS * SLOT_CAP
SEG_PAD = MOE_TILE + SLOT_CAP
PACKED_WIDTH = D_MODEL // 2
SAMPLE_ATTN_TILE = 16
SAMPLE_PROJ_TILE = 512


def _const_spec(shape):
    nd = len(shape)
    return pl.BlockSpec(shape, lambda *_: (0,) * nd)


def _layer_norm(z, g, b):
    mu = jnp.mean(z, axis=-1, keepdims=True)
    zc = z - mu
    var = jnp.mean(zc * zc, axis=-1, keepdims=True)
    return zc * lax.rsqrt(var + LN_EPS) * g + b


def _sigmoid(x):
    return 1.0 / (1.0 + jnp.exp2(x * -LOG2_E))


def _softplus(x):
    return jnp.maximum(x, 0.0) + jnp.log1p(jnp.exp(-jnp.abs(x)))


def _lru_gates(xc, w_gates_ref, b_rg, b_ig, lam):
    xcb = xc.astype(BF16)
    r_parts, i_parts = [], []
    for c in range(N_GATE_CHUNKS):
        g = jnp.dot(xcb[:, c * GATE_CHUNK:(c + 1) * GATE_CHUNK], w_gates_ref[c], preferred_element_type=F32)
        r_parts.append(g[:, :GATE_CHUNK])
        i_parts.append(g[:, GATE_CHUNK:])
    r = _sigmoid(jnp.concatenate(r_parts, axis=1) + b_rg)
    i = _sigmoid(jnp.concatenate(i_parts, axis=1) + b_ig)
    log_a = (-LRU_C * r) * _softplus(-lam)
    a = jnp.exp(log_a)
    gain_sq = 1.0 - a * a
    gain = jnp.where(gain_sq > 0.0, gain_sq * lax.rsqrt(gain_sq), 0.0)
    u = gain * (i * xc)
    return a, u


def _linear_scan(a, u, h_in):
    n, w = a.shape
    groups = n // SUBLANES
    a3 = a.reshape(groups, SUBLANES, w)
    u3 = u.reshape(groups, SUBLANES, w)
    row = lax.broadcasted_iota(jnp.int32, a3.shape, 1)
    d = 1
    while d < SUBLANES:
        has_prev = row >= d
        u3 = u3 + a3 * jnp.where(has_prev, pltpu.roll(u3, d, axis=1), 0.0)
        a3 = a3 * jnp.where(has_prev, pltpu.roll(a3, d, axis=1), 1.0)
        d *= 2
    carry = h_in
    out = []
    for g in range(groups):
        h_g = u3[g] + a3[g] * carry
        out.append(h_g)
        carry = h_g[SUBLANES - 1:SUBLANES, :]
    return jnp.concatenate(out, axis=0)


def _route(x1, w_hi_ref, w_lo_ref, b_rt):
    x_hi = x1.astype(BF16)
    x_lo = (x1 - x_hi.astype(F32)).astype(BF16)
    w_hi = w_hi_ref[...]
    logits = (jnp.dot(x_hi, w_hi, preferred_element_type=F32)
              + (jnp.dot(x_lo, w_hi, preferred_element_type=F32)
                 + jnp.dot(x_hi, w_lo_ref[...], preferred_element_type=F32))) + b_rt
    col = lax.broadcasted_iota(jnp.int32, logits.shape, 1)
    big = jnp.int32(ROUTE_WIDTH)
    is_g = col < N_GROUPS
    gl = jnp.where(is_g, logits, -jnp.inf)
    gmax = jnp.max(gl, axis=-1, keepdims=True)
    g_idx = jnp.min(jnp.where(gl == gmax, col, big), axis=-1, keepdims=True)
    p_g = 1.0 / jnp.sum(jnp.where(is_g, jnp.exp(gl - gmax), 0.0), axis=-1, keepdims=True)
    lo = N_GROUPS + g_idx * EXPERTS_PER_GROUP
    in_grp = (col >= lo) & (col < lo + EXPERTS_PER_GROUP)
    el = jnp.where(in_grp, logits, -jnp.inf)
    v1 = jnp.max(el, axis=-1, keepdims=True)
    i1 = jnp.min(jnp.where(el == v1, col, big), axis=-1, keepdims=True)
    el2 = jnp.where(col == i1, -jnp.inf, el)
    v2 = jnp.max(el2, axis=-1, keepdims=True)
    i2 = jnp.min(jnp.where(el2 == v2, col, big), axis=-1, keepdims=True)
    e21 = jnp.exp(v2 - v1)
    inv = 1.0 / (1.0 + e21)
    w1 = p_g * inv
    w2 = p_g * (e21 * inv)
    e1 = (i1 - N_GROUPS).astype(F32)
    e2 = (i2 - N_GROUPS).astype(F32)
    return jnp.where(col == 0, e1, jnp.where(col == 1, e2, jnp.where(col == 2, w1, jnp.where(col == 3, w2, 0.0))))


def _merge_norm(x, rec, att, g_l, g_a, w_out_ref, ln_g, ln_b):
    rec_o = jnp.dot(rec.astype(BF16), w_out_ref[0], preferred_element_type=F32)
    att_o = jnp.dot(att.astype(BF16), w_out_ref[1], preferred_element_type=F32)
    merged = _sigmoid(g_l) * rec_o + _sigmoid(g_a) * att_o
    mix = jnp.dot(merged.astype(BF16), w_out_ref[2], preferred_element_type=F32)
    return _layer_norm(ALPHA * x + mix, ln_g, ln_b)


def _mixer_kernel(sinks_ref, x_ref, w_in_ref, b_in_ref, w_conv_ref, b_conv_ref, w_gates_ref, b_rg_ref, b_ig_ref,
                  lam_ref, w_out_ref, ln_g_ref, ln_b_ref, w_rt_hi_ref, w_rt_lo_ref,
                  b_rt_ref, x1_s_ref,
                  x1_ref, route_ref, kwin_ref, vwin_ref, conv_ref, h_ref, xs_hbm, goff_ref, gtot_ref,
                  conv_buf, h_carry, kcat, vcat, att_buf, prev_x1, stage, zero_rows, tri, xb_keep, key_keep,
                  g_vmem, gvec, g_smem, nd_smem, rounds_smem, sem_stage, sem_g, sem_zero,
                  *, tiles_per_seq, n_tiles, seg_cap):
    step = pl.program_id(0)
    last_step = pl.num_programs(0) - 1
    buf = lax.rem(step, 2)
    n_sample = x1_s_ref.shape[0]
    disp = _Dispatcher(xs_hbm, goff_ref, gtot_ref, stage, zero_rows, tri, xb_keep, key_keep, g_vmem, gvec,
                       g_smem, nd_smem, rounds_smem, sem_stage, sem_g, sem_zero, seg_cap)
    n_valid = jnp.where(step == 0, 0, jnp.where(step == last_step, n_sample, SEQ_TILE))

    @pl.when(step == 0)
    def _():
        disp.init()
        prev_x1[...] = jnp.zeros_like(prev_x1)

    def route_and_plan():
        x1_prev = prev_x1[...]
        route = _route(x1_prev, w_rt_hi_ref, w_rt_lo_ref, b_rt_ref[...])
        route_ref[...] = route
        disp.plan(x1_prev, route, n_valid, buf)

    @pl.when(step < n_tiles)
    def _():
        @pl.when(lax.rem(step, tiles_per_seq) == 0)
        def _():
            conv_buf[...] = jnp.zeros_like(conv_buf)
            h_carry[...] = jnp.zeros_like(h_carry)
            kcat[0:WINDOW, :] = jnp.zeros((WINDOW, KV_WIDTH), BF16)
            vcat[0:WINDOW, :] = jnp.zeros((WINDOW, KV_WIDTH), BF16)

        route_and_plan()
        _mixer_tile(lax.rem(step, tiles_per_seq), sinks_ref, x_ref, w_in_ref, b_in_ref, w_conv_ref, b_conv_ref,
                    w_gates_ref, b_rg_ref, b_ig_ref, lam_ref, w_out_ref, ln_g_ref,
                    ln_b_ref, x1_ref, prev_x1, kwin_ref, vwin_ref, conv_ref, h_ref,
                    conv_buf, h_carry, kcat, vcat, att_buf)

    @pl.when(step >= n_tiles)
    def _():
        route_and_plan()

        @pl.when(step == n_tiles)
        def _():
            x1_ref[0:n_sample, :] = x1_s_ref[...]
            prev_x1[0:n_sample, :] = x1_s_ref[...]

    disp.flush(buf, step == last_step)


def _mixer_tile(t, sinks_ref, x_ref, w_in_ref, b_in_ref, w_conv_ref, b_conv_ref, w_gates_ref, b_rg_ref, b_ig_ref,
                lam_ref, w_out_ref, ln_g_ref, ln_b_ref,
                x1_ref, x1_keep, kwin_ref, vwin_ref, conv_ref, h_ref,
                conv_buf, h_carry, kcat, vcat, att_buf):
    T = SEQ_TILE
    x = x_ref[0]
    xb = x.astype(BF16)

    def proj(lo, width):
        return jnp.dot(xb, w_in_ref[:, lo:lo + width], preferred_element_type=F32) + b_in_ref[:, lo:lo + width]

    xl = proj(OFF_XL, LRU_WIDTH)
    xl_ext = jnp.concatenate([conv_buf[...], xl], axis=0)

    def lagged(k):
        return pltpu.roll(xl_ext, k, axis=0)[SUBLANES:, :]
    wc = w_conv_ref[...]
    xc = wc[0:1] * lagged(3)
    xc = xc + wc[1:2] * lagged(2)
    xc = xc + wc[2:3] * lagged(1)
    xc = xc + wc[3:4] * xl + b_conv_ref[...]
    conv_ref[0] = xl[T - (CONV_W - 1):, :]
    conv_buf[...] = xl[T - SUBLANES:, :]

    a, u = _lru_gates(xc, w_gates_ref, b_rg_ref[...], b_ig_ref[...], lam_ref[...])
    h = _linear_scan(a, u, h_carry[0:1, :])
    h_last = h[T - 1:T, :]
    h_carry[0:1, :] = h_last
    h_ref[0] = h_last
    rec = h * jax.nn.gelu(proj(OFF_YL, LRU_WIDTH))

    q = proj(OFF_Q, N_HEADS * HEAD_DIM) * (ATTN_SCALE * LOG2_E)
    k = proj(OFF_K, KV_WIDTH)
    v = proj(OFF_V, KV_WIDTH)
    kwin_ref[0] = k[T - WINDOW:, :]
    vwin_ref[0] = v[T - WINDOW:, :]
    kcat[WINDOW:WINDOW + T, :] = k.astype(BF16)
    vcat[WINDOW:WINDOW + T, :] = v.astype(BF16)

    qi = lax.broadcasted_iota(jnp.int32, (WINDOW, 2 * WINDOW), 0)
    kj = lax.broadcasted_iota(jnp.int32, (WINDOW, 2 * WINDOW), 1)
    band = (kj > qi) & (kj <= qi + WINDOW)
    grp_row = lax.broadcasted_iota(jnp.int32, (GROUP * WINDOW, 1), 0) // WINDOW
    for qb in range(T // WINDOW):
        if qb == 0:
            first_key = jnp.where(t == 0, WINDOW, 0)
            mask1 = band & (kj >= first_key)
        else:
            mask1 = band
        bias = jnp.concatenate([jnp.where(mask1, 0.0, NEG_INF)] * GROUP, axis=0)
        r0 = qb * WINDOW
        qq = q[r0:r0 + WINDOW, :]
        for j in range(N_KV):
            kjb = kcat[r0:r0 + 2 * WINDOW, j * HEAD_DIM:(j + 1) * HEAD_DIM]
            vjb = vcat[r0:r0 + 2 * WINDOW, j * HEAD_DIM:(j + 1) * HEAD_DIM]
            qs = jnp.concatenate(
                [qq[:, (j * GROUP + g) * HEAD_DIM:(j * GROUP + g + 1) * HEAD_DIM] for g in range(GROUP)], axis=0)
            s = lax.dot_general(qs.astype(BF16), kjb, (((1,), (1,)), ((), ())), preferred_element_type=F32) + bias
            sink = jnp.zeros((GROUP * WINDOW, 1), F32)
            for g in range(GROUP):
                sink = jnp.where(grp_row == g, sinks_ref[j * GROUP + g] * LOG2_E, sink)
            m = jnp.maximum(jnp.max(s, axis=-1, keepdims=True), sink)
            p = jnp.exp2(s - m)
            inv = 1.0 / (jnp.sum(p, axis=-1, keepdims=True) + jnp.exp2(sink - m))
            o = jnp.dot((p * inv).astype(BF16), vjb, preferred_element_type=F32)
            for g in range(GROUP):
                hcol = (j * GROUP + g) * HEAD_DIM
                att_buf[r0:r0 + WINDOW, hcol:hcol + HEAD_DIM] = o[g * WINDOW:(g + 1) * WINDOW, :]
    kcat[0:WINDOW, :] = kcat[T:T + WINDOW, :]
    vcat[0:WINDOW, :] = vcat[T:T + WINDOW, :]

    x1 = _merge_norm(x, rec, att_buf[...], proj(OFF_GL, D_MODEL), proj(OFF_GA, D_MODEL),
                     w_out_ref, ln_g_ref[...], ln_b_ref[...])
    x1_ref[...] = x1
    x1_keep[...] = x1


def _mixer_call(x_prompt, x1_s, sinks, wts):
    B, S, _ = x_prompt.shape
    T = SEQ_TILE
    assert T == DISPATCH_TILE and x1_s.shape[0] <= T
    nt = S // T
    n_tiles = B * nt
    n_rows_total = B * S + x1_s.shape[0]
    n_dispatch = _dispatch_tiles(n_rows_total)
    assert n_dispatch == n_tiles + 1
    seg_cap = _segment_capacity(n_rows_total)
    n_assign = TOP_K * T
    weight_args = (wts["w_in"], wts["b_in"], wts["w_conv"], wts["b_conv"], wts["w_gates"], wts["b_rg"], wts["b_ig"],
                   wts["lam"], wts["w_out"], wts["ln1_g"], wts["ln1_b"],
                   wts["w_rt_hi"], wts["w_rt_lo"], wts["b_rt"], x1_s)
    mixed = lambda i: jnp.minimum(i, n_tiles - 1)
    seq = lambda i: mixed(i) // nt
    routed = lambda i: jnp.clip(i - 1, 0, n_dispatch - 1)
    in_specs = [pl.BlockSpec(memory_space=pltpu.SMEM),
                pl.BlockSpec((1, T, D_MODEL), lambda i: (seq(i), lax.rem(mixed(i), nt), 0))]
    in_specs += [_const_spec(w.shape) for w in weight_args]
    out_shape = (
        jax.ShapeDtypeStruct((n_rows_total, D_MODEL), F32),
        jax.ShapeDtypeStruct((n_rows_total, ROUTE_WIDTH), F32),
        jax.ShapeDtypeStruct((B, WINDOW, KV_WIDTH), F32),
        jax.ShapeDtypeStruct((B, WINDOW, KV_WIDTH), F32),
        jax.ShapeDtypeStruct((B, CONV_W - 1, LRU_WIDTH), F32),
        jax.ShapeDtypeStruct((B, 1, LRU_WIDTH), F32),
        jax.ShapeDtypeStruct((N_EXPERTS * seg_cap, PACKED_WIDTH), jnp.uint32),
        jax.ShapeDtypeStruct((n_dispatch, 1, ROUTE_WIDTH), jnp.int32),
        jax.ShapeDtypeStruct((1, ROUTE_WIDTH), jnp.int32),
    )
    out_specs = (
        pl.BlockSpec((T, D_MODEL), lambda i: (jnp.minimum(i, n_tiles), 0)),
        pl.BlockSpec((T, ROUTE_WIDTH), lambda i: (routed(i), 0)),
        pl.BlockSpec((1, WINDOW, KV_WIDTH), lambda i: (seq(i), 0, 0)),
        pl.BlockSpec((1, WINDOW, KV_WIDTH), lambda i: (seq(i), 0, 0)),
        pl.BlockSpec((1, CONV_W - 1, LRU_WIDTH), lambda i: (seq(i), 0, 0)),
        pl.BlockSpec((1, 1, LRU_WIDTH), lambda i: (seq(i), 0, 0)),
        pl.BlockSpec(memory_space=pl.ANY),
        pl.BlockSpec((1, 1, ROUTE_WIDTH), lambda i: (routed(i), 0, 0)),
        pl.BlockSpec((1, ROUTE_WIDTH), lambda i: (0, 0)),
    )
    scratch = [
        pltpu.VMEM((SUBLANES, LRU_WIDTH), F32),
        pltpu.VMEM((SUBLANES, LRU_WIDTH), F32),
        pltpu.VMEM((T + WINDOW, KV_WIDTH), BF16),
        pltpu.VMEM((T + WINDOW, KV_WIDTH), BF16),
        pltpu.VMEM((T, N_HEADS * HEAD_DIM), F32),
        pltpu.VMEM((T, D_MODEL), F32),
        pltpu.VMEM((2, N_SLOTS, PACKED_WIDTH), jnp.uint32),
        pltpu.VMEM((SEG_PAD, PACKED_WIDTH), jnp.uint32),
        pltpu.VMEM((n_assign, n_assign), BF16),
        pltpu.VMEM((T, D_MODEL), BF16),
        pltpu.VMEM((SUBLANES, n_assign), F32),
        pltpu.VMEM((SUBLANES, ROUTE_WIDTH), jnp.int32),
        pltpu.VMEM((SUBLANES, ROUTE_WIDTH), F32),
        pltpu.SMEM((1, ROUTE_WIDTH), jnp.int32),
        pltpu.SMEM((1,), jnp.int32),
        pltpu.SMEM((1,), jnp.int32),
        pltpu.SemaphoreType.DMA(()),
        pltpu.SemaphoreType.DMA(()),
        pltpu.SemaphoreType.DMA(()),
    ]
    return pl.pallas_call(
        functools.partial(_mixer_kernel, tiles_per_seq=nt, n_tiles=n_tiles, seg_cap=seg_cap),
        grid=(n_tiles + 2,),
        in_specs=in_specs,
        out_specs=out_specs,
        out_shape=out_shape,
        scratch_shapes=scratch,
        compiler_params=pltpu.CompilerParams(dimension_semantics=("arbitrary",),
                                             vmem_limit_bytes=VMEM_LIMIT_BYTES),
        name="mixer_prompt",
    )(sinks, x_prompt, *weight_args)


def _sample_proj_kernel(x_ref, w_ref, b_ref, u_ref):
    u_ref[...] = jnp.dot(x_ref[...].astype(BF16), w_ref[...], preferred_element_type=F32) + b_ref[...]


def _sample_proj_call(x_s, w_in, b_in):
    n = x_s.shape[0]
    tn = SAMPLE_PROJ_TILE
    return pl.pallas_call(
        _sample_proj_kernel,
        grid=(IN_WIDTH // tn,),
        in_specs=[pl.BlockSpec((n, D_MODEL), lambda c: (0, 0)),
                  pl.BlockSpec((D_MODEL, tn), lambda c: (0, c)),
                  pl.BlockSpec((1, tn), lambda c: (0, c))],
        out_specs=pl.BlockSpec((n, tn), lambda c: (0, c)),
        out_shape=jax.ShapeDtypeStruct((n, IN_WIDTH), F32),
        compiler_params=pltpu.CompilerParams(dimension_semantics=("arbitrary",)),
        name="sample_proj",
    )(x_s, w_in, b_in)


def _sample_attn_kernel(q_ref, kn_ref, vn_ref, kn_row_ref, vn_row_ref, ck_ref, cv_ref, sinks_ref,
                        att_ref, kwin_ref, vwin_ref):
    tb = q_ref.shape[0]
    key_pos = lax.broadcasted_iota(jnp.int32, (tb, GROUP, WINDOW), 2)
    for j in range(N_KV):
        qj = q_ref[:, j * GROUP:(j + 1) * GROUP, :]
        kc = ck_ref[:, :, j * HEAD_DIM:(j + 1) * HEAD_DIM]
        vc = cv_ref[:, :, j * HEAD_DIM:(j + 1) * HEAD_DIM]
        s_c = jnp.einsum("bgd,bsd->bgs", qj.astype(BF16), kc.astype(BF16), preferred_element_type=F32) * ATTN_SCALE
        s_c = jnp.where(key_pos >= 1, s_c, NEG_INF)
        kn = kn_ref[:, j:j + 1, :]
        vn = vn_ref[:, j:j + 1, :]
        s_n = jnp.sum(qj * kn, axis=-1, keepdims=True) * ATTN_SCALE
        sink = sinks_ref[j][None]
        m = jnp.maximum(jnp.maximum(jnp.max(s_c, axis=-1, keepdims=True), s_n), sink)
        p_c = jnp.exp(s_c - m)
        p_n = jnp.exp(s_n - m)
        inv = 1.0 / (jnp.sum(p_c, axis=-1, keepdims=True) + p_n + jnp.exp(sink - m))
        o = jnp.einsum("bgs,bsd->bgd", (p_c * inv).astype(BF16), vc.astype(BF16), preferred_element_type=F32)
        att_ref[:, j * GROUP:(j + 1) * GROUP, :] = o + (p_n * inv) * vn
    kwin_ref[:, 0:WINDOW - 1, :] = ck_ref[:, 1:WINDOW, :]
    kwin_ref[:, WINDOW - 1:WINDOW, :] = kn_row_ref[...]
    vwin_ref[:, 0:WINDOW - 1, :] = cv_ref[:, 1:WINDOW, :]
    vwin_ref[:, WINDOW - 1:WINDOW, :] = vn_row_ref[...]


def _sample_attn_call(q3, kn3, vn3, kn_row, vn_row, ck, cv, sinks3):
    n = q3.shape[0]
    tb = SAMPLE_ATTN_TILE
    b3 = lambda i: (i, 0, 0)
    return pl.pallas_call(
        _sample_attn_kernel,
        grid=(n // tb,),
        in_specs=[pl.BlockSpec((tb, N_HEADS, HEAD_DIM), b3),
                  pl.BlockSpec((tb, N_KV, HEAD_DIM), b3),
                  pl.BlockSpec((tb, N_KV, HEAD_DIM), b3),
                  pl.BlockSpec((tb, 1, KV_WIDTH), b3),
                  pl.BlockSpec((tb, 1, KV_WIDTH), b3),
                  pl.BlockSpec((tb, WINDOW, KV_WIDTH), b3),
                  pl.BlockSpec((tb, WINDOW, KV_WIDTH), b3),
                  pl.BlockSpec((N_KV, GROUP, 1), lambda i: (0, 0, 0))],
        out_specs=(pl.BlockSpec((tb, N_HEADS, HEAD_DIM), b3),
                   pl.BlockSpec((tb, WINDOW, KV_WIDTH), b3),
                   pl.BlockSpec((tb, WINDOW, KV_WIDTH), b3)),
        out_shape=(jax.ShapeDtypeStruct((n, N_HEADS, HEAD_DIM), F32),
                   jax.ShapeDtypeStruct((n, WINDOW, KV_WIDTH), F32),
                   jax.ShapeDtypeStruct((n, WINDOW, KV_WIDTH), F32)),
        compiler_params=pltpu.CompilerParams(dimension_semantics=("arbitrary",)),
        name="sample_attn",
    )(q3, kn3, vn3, kn_row, vn_row, ck, cv, sinks3)


def _sample_mix_kernel(x_ref, u_ref, att_ref, st_ref, h0_ref, w_conv_ref, b_conv_ref, w_gates_ref, b_rg_ref,
                       b_ig_ref, lam_ref, w_out_ref, ln_g_ref, ln_b_ref,
                       x1_ref, conv_ref, h_ref):
    xl = u_ref[:, OFF_XL:OFF_XL + LRU_WIDTH]
    wc = w_conv_ref[...]
    xc = wc[0:1] * st_ref[0]
    xc = xc + wc[1:2] * st_ref[1]
    xc = xc + wc[2:3] * st_ref[2]
    xc = xc + wc[3:4] * xl + b_conv_ref[...]
    conv_ref[0] = st_ref[1]
    conv_ref[1] = st_ref[2]
    conv_ref[2] = xl
    a, u = _lru_gates(xc, w_gates_ref, b_rg_ref[...], b_ig_ref[...], lam_ref[...])
    h = a * h0_ref[...] + u
    h_ref[...] = h
    rec = h * jax.nn.gelu(u_ref[:, OFF_YL:OFF_YL + LRU_WIDTH])
    x1_ref[...] = _merge_norm(x_ref[...], rec, att_ref[...], u_ref[:, OFF_GL:OFF_GL + D_MODEL],
                              u_ref[:, OFF_GA:OFF_GA + D_MODEL], w_out_ref,
                              ln_g_ref[...], ln_b_ref[...])


def _sample_mix_call(x_s, u_s, att, st_t, h0, wts):
    n = x_s.shape[0]
    weight_args = (wts["w_conv"], wts["b_conv"], wts["w_gates"], wts["b_rg"], wts["b_ig"], wts["lam"],
                   wts["w_out"], wts["ln1_g"], wts["ln1_b"])
    args = (x_s, u_s, att, st_t, h0) + weight_args
    out_shapes = ((n, D_MODEL), (CONV_W - 1, n, LRU_WIDTH), (n, LRU_WIDTH))
    return pl.pallas_call(
        _sample_mix_kernel,
        grid=(1,),
        in_specs=[_const_spec(a.shape) for a in args],
        out_specs=tuple(_const_spec(s) for s in out_shapes),
        out_shape=tuple(jax.ShapeDtypeStruct(s, F32) for s in out_shapes),
        compiler_params=pltpu.CompilerParams(dimension_semantics=("arbitrary",),
                                             vmem_limit_bytes=VMEM_LIMIT_BYTES),
        name="sample_mix",
    )(*args)


def _one_hot(mask):
    return jnp.where(mask, 1.0, 0.0).astype(BF16)


def _pack_rows(x):
    half = x.shape[1] // 2
    lo = lax.shift_right_logical(lax.bitcast_convert_type(x[:, :half], jnp.uint32), jnp.uint32(16))
    hi = lax.bitcast_convert_type(x[:, half:], jnp.uint32) & jnp.uint32(0xFFFF0000)
    return lo | hi


def _unpack_rows(words):
    lo = lax.bitcast_convert_type(lax.shift_left(words, jnp.uint32(16)), F32)
    hi = lax.bitcast_convert_type(words & jnp.uint32(0xFFFF0000), F32)
    return jnp.concatenate([lo.astype(BF16), hi.astype(BF16)], axis=1)


class _Dispatcher:
    def __init__(self, xs_hbm, goff_ref, gtot_ref, stage, zero_rows, tri, xb_keep, key_keep, g_vmem, gvec,
                 g_smem, nd_smem, rounds_smem, sem_stage, sem_g, sem_zero, seg_cap):
        self.xs_hbm, self.goff_ref, self.gtot_ref = xs_hbm, goff_ref, gtot_ref
        self.stage, self.zero_rows, self.tri = stage, zero_rows, tri
        self.xb_keep, self.key_keep = xb_keep, key_keep
        self.g_vmem, self.gvec, self.g_smem = g_vmem, gvec, g_smem
        self.nd_smem, self.rounds_smem = nd_smem, rounds_smem
        self.sem_stage, self.sem_g, self.sem_zero = sem_stage, sem_g, sem_zero
        self.seg_cap = seg_cap

    def _g_copy(self):
        return pltpu.make_async_copy(self.g_vmem.at[pl.ds(0, 1)], self.g_smem, self.sem_g)

    def init(self):
        n_assign = self.tri.shape[0]
        r = lax.broadcasted_iota(jnp.int32, (n_assign, n_assign), 0)
        c = lax.broadcasted_iota(jnp.int32, (n_assign, n_assign), 1)
        self.tri[...] = _one_hot(r < c)
        self.gvec[...] = jnp.zeros_like(self.gvec)
        self.g_vmem[...] = jnp.zeros_like(self.g_vmem)
        self.nd_smem[0] = 0
        self._g_copy().start()

    def _sorted_rows(self, e_row, rank_row, xb, lo):
        td = xb.shape[0]
        slot = lax.broadcasted_iota(jnp.int32, (N_SLOTS, td), 0).astype(F32)
        lo_f = lo.astype(F32)
        in_round = (rank_row >= lo_f) & (rank_row < lo_f + SLOT_CAP) & (e_row >= 0.0)
        key = jnp.where(in_round, e_row * SLOT_CAP + (rank_row - lo_f), -1.0)
        perm = jnp.where(slot == key[:, :td], 1.0, jnp.where(slot == key[:, td:], 1.0, 0.0)).astype(BF16)
        return _pack_rows(jnp.dot(perm, xb, preferred_element_type=F32))

    def plan(self, x1, route, n_valid, buf):
        td = x1.shape[0]
        n_assign = TOP_K * td
        valid_col = lax.broadcasted_iota(jnp.int32, (td, 1), 0) < n_valid
        lane = lax.broadcasted_iota(jnp.int32, (td, ROUTE_WIDTH), 1).astype(F32)
        e0c = jnp.where(valid_col, route[:, 0:1], -1.0)
        e1c = jnp.where(valid_col, route[:, 1:2], -1.0)
        cnt_row = jnp.sum(jnp.where(lane == e0c, 1.0, 0.0) + jnp.where(lane == e1c, 1.0, 0.0),
                          axis=0, keepdims=True)
        self.rounds_smem[0] = (jnp.max(cnt_row).astype(jnp.int32) + SLOT_CAP - 1) // SLOT_CAP
        self.goff_ref[0] = self.gvec[0:1, :].astype(jnp.int32)
        self.gvec[0:1, :] = self.gvec[0:1, :] + jnp.ceil(cnt_row * (1.0 / SUBLANES)) * SUBLANES
        self.gtot_ref[...] = self.gvec[0:1, :].astype(jnp.int32)

        route_t = route.T
        valid_row = lax.broadcasted_iota(jnp.int32, (1, td), 1) < n_valid
        e_row = jnp.concatenate([jnp.where(valid_row, route_t[0:1, :], -1.0),
                                 jnp.where(valid_row, route_t[1:2, :], -1.0)], axis=1)
        expert_sub = lax.broadcasted_iota(jnp.int32, (N_EXPERTS, n_assign), 0).astype(F32)
        onehot_t = expert_sub == e_row
        rank_t = jnp.dot(_one_hot(onehot_t), self.tri[...], preferred_element_type=F32)
        rank_row = jnp.sum(jnp.where(onehot_t, rank_t, 0.0), axis=0, keepdims=True)
        xb = jnp.where(valid_col, x1, 0.0).astype(BF16)
        self.xb_keep[...] = xb
        self.key_keep[0:1, :] = e_row
        self.key_keep[1:2, :] = rank_row
        self.stage[buf] = self._sorted_rows(e_row, rank_row, xb, jnp.int32(0))

    def _stage_copy(self, buf, e, dst_row):
        src = self.stage.at[buf, pl.ds(pl.multiple_of(e * SLOT_CAP, SLOT_CAP), SLOT_CAP)]
        return pltpu.make_async_copy(src, self.xs_hbm.at[pl.ds(dst_row, SLOT_CAP)], self.sem_stage)

    def _wait_outstanding(self):
        def wait_one(_, carry):
            self._stage_copy(0, 0, 0).wait()
            return carry
        lax.fori_loop(0, self.nd_smem[0], wait_one, 0)
        self.nd_smem[0] = 0

    def _start_round(self, buf, lo):
        def issue(e, carry):
            dst_row = pl.multiple_of(e * self.seg_cap + self.g_smem[0, e] + lo, SUBLANES)
            self._stage_copy(buf, e, dst_row).start()
            return carry
        lax.fori_loop(0, N_EXPERTS, issue, 0)
        self.nd_smem[0] = N_EXPERTS

    def flush(self, buf, is_last):
        self._g_copy().wait()
        self._wait_outstanding()
        self._start_round(buf, 0)

        def later_round(r, carry):
            lo = r * SLOT_CAP
            rows = self._sorted_rows(self.key_keep[0:1, :], self.key_keep[1:2, :], self.xb_keep[...], lo)
            self._wait_outstanding()
            self.stage[buf] = rows
            self._start_round(buf, lo)
            return carry
        lax.fori_loop(1, self.rounds_smem[0], later_round, 0)

        self.g_vmem[0:1, :] = self.gvec[0:1, :].astype(jnp.int32)
        self._g_copy().start()

        @pl.when(is_last)
        def _():
            self._g_copy().wait()
            self._wait_outstanding()
            self.zero_rows[...] = jnp.zeros_like(self.zero_rows)

            def pad_copy(e):
                dst_row = pl.multiple_of(e * self.seg_cap + self.g_smem[0, e], SUBLANES)
                return pltpu.make_async_copy(self.zero_rows, self.xs_hbm.at[pl.ds(dst_row, SEG_PAD)], self.sem_zero)

            def start(e, carry):
                pad_copy(e).start()
                return carry
            lax.fori_loop(0, N_EXPERTS, start, 0)

            def wait(e, carry):
                pad_copy(e).wait()
                return carry
            lax.fori_loop(0, N_EXPERTS, wait, 0)


def _dispatch_tiles(n_tokens):
    return (n_tokens + DISPATCH_TILE - 1) // DISPATCH_TILE


def _segment_capacity(n_tokens):
    alignment_slack = (SUBLANES - 1) * _dispatch_tiles(n_tokens)
    return (n_tokens + alignment_slack + SEG_PAD + MOE_TILE - 1) // MOE_TILE * MOE_TILE


def _moe_kernel(blk_e_ref, blk_j_ref, n_used_ref, xs_ref, wg_ref, wu_ref, wd_ref, ys_ref, wg_bf, wu_bf, wd_bf):
    del blk_j_ref
    b = pl.program_id(0)

    @pl.when(b < n_used_ref[0])
    def _():
        prev_e = blk_e_ref[jnp.maximum(b - 1, 0)]

        @pl.when((b == 0) | (blk_e_ref[b] != prev_e))
        def _():
            wg_bf[...] = wg_ref[0].astype(BF16)
            wu_bf[...] = wu_ref[0].astype(BF16)
            wd_bf[...] = wd_ref[0].astype(BF16)

        xb = _unpack_rows(xs_ref[...])
        g = jnp.dot(xb, wg_bf[...], preferred_element_type=F32)
        u = jnp.dot(xb, wu_bf[...], preferred_element_type=F32)
        hmid = (g * _sigmoid(g)) * u
        y = jnp.dot(hmid.astype(BF16), wd_bf[...], preferred_element_type=F32)
        ys_ref[...] = _pack_rows(y.astype(BF16).astype(F32))


def _expert_blocks(gtot, n_blocks):
    rows = gtot[0, :N_EXPERTS]
    nb = (rows + SLOT_CAP + MOE_TILE - 1) // MOE_TILE
    ends = jnp.cumsum(nb)
    n_used = ends[-1]
    step = jnp.minimum(jnp.arange(n_blocks, dtype=jnp.int32), n_used - 1)
    blk_e = jnp.sum((step[:, None] >= ends[None, :]).astype(jnp.int32), axis=1)
    blk_j = step - (ends - nb)[blk_e]
    return blk_e, blk_j.astype(jnp.int32), n_used.reshape(1).astype(jnp.int32)


def _moe_call(gtot, xs, w_gate, w_up, w_down, n_tokens):
    seg_cap = _segment_capacity(n_tokens)
    seg_blocks = seg_cap // MOE_TILE
    max_rows = n_tokens * TOP_K + N_EXPERTS * (SUBLANES - 1) * _dispatch_tiles(n_tokens)
    n_blocks = (max_rows + N_EXPERTS * (SLOT_CAP + MOE_TILE - 1)) // MOE_TILE
    blk_e, blk_j, n_used = _expert_blocks(gtot, n_blocks)
    row_block = lambda b, be, bj, nu: (be[b] * seg_blocks + bj[b], 0)
    weight_block = lambda b, be, bj, nu: (be[b], 0, 0)
    grid_spec = pltpu.PrefetchScalarGridSpec(
        num_scalar_prefetch=3,
        grid=(n_blocks,),
        in_specs=[
            pl.BlockSpec((MOE_TILE, PACKED_WIDTH), row_block),
            pl.BlockSpec((1, D_MODEL, EXPERT_FF), weight_block),
            pl.BlockSpec((1, D_MODEL, EXPERT_FF), weight_block),
            pl.BlockSpec((1, EXPERT_FF, D_MODEL), weight_block),
        ],
        out_specs=pl.BlockSpec((MOE_TILE, PACKED_WIDTH), row_block),
        scratch_shapes=[
            pltpu.VMEM((D_MODEL, EXPERT_FF), BF16),
            pltpu.VMEM((D_MODEL, EXPERT_FF), BF16),
            pltpu.VMEM((EXPERT_FF, D_MODEL), BF16),
        ],
    )
    return pl.pallas_call(
        _moe_kernel,
        grid_spec=grid_spec,
        out_shape=jax.ShapeDtypeStruct(xs.shape, xs.dtype),
        compiler_params=pltpu.CompilerParams(dimension_semantics=("arbitrary",),
                                             vmem_limit_bytes=VMEM_LIMIT_BYTES),
        name="moe_experts",
    )(blk_e, blk_j, n_used, xs, w_gate, w_up, w_down)


def _combine_kernel(gcur_ref, gnext_ref, x1_ref, route_ref, ys_hbm, ln_g_ref, ln_b_ref, y_ref,
                    stage, acc, tri, sem, *, seg_cap):
    td = x1_ref.shape[0]
    n_assign = TOP_K * td
    i = pl.program_id(0)
    buf = lax.rem(i, 2)

    def stage_copy(g_ref, e, first_rank, to_buf):
        src_row = pl.multiple_of(e * seg_cap + g_ref[0, 0, e] + first_rank, SUBLANES)
        dst = stage.at[to_buf, pl.ds(pl.multiple_of(e * SLOT_CAP, SLOT_CAP), SLOT_CAP)]
        return pltpu.make_async_copy(ys_hbm.at[pl.ds(src_row, SLOT_CAP)], dst, sem.at[to_buf])

    def fetch_first_round(g_ref, to_buf):
        def start(e, carry):
            stage_copy(g_ref, e, 0, to_buf).start()
            return carry
        lax.fori_loop(0, N_EXPERTS, start, 0)

    @pl.when(i == 0)
    def _():
        r = lax.broadcasted_iota(jnp.int32, (n_assign, n_assign), 0)
        c = lax.broadcasted_iota(jnp.int32, (n_assign, n_assign), 1)
        tri[...] = _one_hot(c < r)
        fetch_first_round(gcur_ref, 0)

    def count(e):
        return gnext_ref[0, 0, e] - gcur_ref[0, 0, e]

    n_max = lax.fori_loop(0, N_EXPERTS, lambda e, m: jnp.maximum(m, count(e)), 0)
    rounds = (n_max + SLOT_CAP - 1) // SLOT_CAP

    route = route_ref[...]
    e0, e1, w0, w1 = route[:, 0:1], route[:, 1:2], route[:, 2:3], route[:, 3:4]
    lane = lax.broadcasted_iota(jnp.int32, (td, ROUTE_WIDTH), 1).astype(F32)
    onehot = jnp.concatenate([lane == e0, lane == e1], axis=0)
    onehot_f = jnp.where(onehot, 1.0, 0.0)
    rank_mat = jnp.dot(tri[...], onehot_f.astype(BF16), preferred_element_type=F32)
    rank = jnp.sum(jnp.where(onehot, rank_mat, 0.0), axis=1, keepdims=True)
    r0, r1 = rank[:td], rank[td:]
    cnt_row = jnp.sum(onehot_f, axis=0, keepdims=True)

    slot_col = lax.broadcasted_iota(jnp.int32, (N_SLOTS, 1), 0)
    slot_expert = slot_col // SLOT_CAP
    slot_rank = slot_col - slot_expert * SLOT_CAP
    lane_s = lax.broadcasted_iota(jnp.int32, (N_SLOTS, ROUTE_WIDTH), 1)
    n_col = jnp.sum(jnp.where(lane_s == slot_expert, cnt_row, 0.0), axis=1, keepdims=True)
    slot_lane = lax.broadcasted_iota(jnp.int32, (td, N_SLOTS), 1).astype(F32)

    def wait_copies(n):
        def wait_one(_, c):
            stage_copy(gcur_ref, 0, 0, buf).wait()
            return c
        lax.fori_loop(0, n, wait_one, 0)

    def weighted_rows(lo):
        lo_f = lo.astype(F32)
        live = (slot_rank + lo).astype(F32) < n_col
        rows = _unpack_rows(jnp.where(live, stage[buf], jnp.uint32(0)))

        def selector(e_col, r_col, w_col):
            in_round = (r_col >= lo_f) & (r_col < lo_f + SLOT_CAP)
            key = jnp.where(in_round, e_col * SLOT_CAP + (r_col - lo_f), -1.0)
            return jnp.where(slot_lane == key, w_col, 0.0)
        mix = (selector(e0, r0, w0) + selector(e1, r1, w1)).astype(BF16)
        return jnp.dot(mix, rows, preferred_element_type=F32)

    wait_copies(N_EXPERTS)

    @pl.when(i + 1 < pl.num_programs(0))
    def _():
        fetch_first_round(gnext_ref, 1 - buf)

    acc[...] = weighted_rows(jnp.int32(0))

    def later_round(r, carry):
        lo = r * SLOT_CAP

        def start(e, n_started):
            has_rows = count(e) > lo

            @pl.when(has_rows)
            def _():
                stage_copy(gcur_ref, e, lo, buf).start()
            return n_started + has_rows.astype(jnp.int32)
        wait_copies(lax.fori_loop(0, N_EXPERTS, start, 0))
        acc[...] = acc[...] + weighted_rows(lo)
        return carry
    lax.fori_loop(1, rounds, later_round, 0)

    y_ref[...] = _layer_norm(ALPHA * x1_ref[...] + acc[...], ln_g_ref[...], ln_b_ref[...])


def _combine_call(goff, x1_all, route_all, ys, ln_g, ln_b, first_row, n_rows, tile, n_tokens):
    assert first_row % DISPATCH_TILE == 0 and first_row % tile == 0
    assert tile == DISPATCH_TILE or n_rows == tile
    first_block = first_row // tile
    first_goff = first_row // DISPATCH_TILE
    n_assign = TOP_K * tile
    grid_spec = pl.GridSpec(
        grid=(n_rows // tile,),
        in_specs=[
            pl.BlockSpec((1, 1, ROUTE_WIDTH), lambda i: (first_goff + i, 0, 0), memory_space=pltpu.SMEM),
            pl.BlockSpec((1, 1, ROUTE_WIDTH), lambda i: (first_goff + i + 1, 0, 0), memory_space=pltpu.SMEM),
            pl.BlockSpec((tile, D_MODEL), lambda i: (first_block + i, 0)),
            pl.BlockSpec((tile, ROUTE_WIDTH), lambda i: (first_block + i, 0)),
            pl.BlockSpec(memory_space=pl.ANY),
            _const_spec(ln_g.shape),
            _const_spec(ln_b.shape),
        ],
        out_specs=pl.BlockSpec((tile, D_MODEL), lambda i: (i, 0)),
        scratch_shapes=[pltpu.VMEM((2, N_SLOTS, PACKED_WIDTH), jnp.uint32),
                        pltpu.VMEM((tile, D_MODEL), F32),
                        pltpu.VMEM((n_assign, n_assign), BF16),
                        pltpu.SemaphoreType.DMA((2,))],
    )
    return pl.pallas_call(
        functools.partial(_combine_kernel, seg_cap=_segment_capacity(n_tokens)),
        grid_spec=grid_spec,
        out_shape=jax.ShapeDtypeStruct((n_rows, D_MODEL), F32),
        compiler_params=pltpu.CompilerParams(dimension_semantics=("arbitrary",),
                                             vmem_limit_bytes=VMEM_LIMIT_BYTES),
        name="moe_combine",
    )(goff, goff, x1_all, route_all, ys, ln_g, ln_b)


def _prepare_weights(w_in, b_in, w_conv, b_conv, w_rg, b_rg, w_ig, b_ig, lru_lambda, w_lru_out, w_attn_out, w_o,
                     ln1_g, ln1_b, w_group, b_group, w_router, b_router):
    blocks_per_chunk = GATE_CHUNK // LRU_BLOCK

    def chunked_block_diag(w):
        w = w.reshape(N_GATE_CHUNKS, blocks_per_chunk, LRU_BLOCK, LRU_BLOCK)
        eye = jnp.eye(blocks_per_chunk, dtype=w.dtype)
        return jnp.einsum("cbij,bd->cbidj", w, eye).reshape(N_GATE_CHUNKS, GATE_CHUNK, GATE_CHUNK)

    w_gates = jnp.concatenate([chunked_block_diag(w_rg), chunked_block_diag(w_ig)], axis=-1).astype(BF16)
    w_rt = jnp.concatenate([w_group, w_router], axis=1)
    w_rt = jnp.pad(w_rt, ((0, 0), (0, ROUTE_WIDTH - w_rt.shape[1])))
    w_rt_hi = w_rt.astype(BF16)
    w_rt_lo = (w_rt - w_rt_hi.astype(F32)).astype(BF16)
    b_rt = jnp.pad(jnp.concatenate([b_group, b_router]), (0, ROUTE_WIDTH - N_GROUPS - N_EXPERTS))
    row = lambda v: v.reshape(1, -1)
    return dict(
        w_in=w_in.astype(BF16), b_in=row(b_in), w_conv=w_conv, b_conv=row(b_conv), w_gates=w_gates,
        b_rg=row(b_rg), b_ig=row(b_ig), lam=row(lru_lambda),
        w_out=jnp.stack([w_lru_out, w_attn_out, w_o]).astype(BF16),
        ln1_g=row(ln1_g), ln1_b=row(ln1_b), w_rt_hi=w_rt_hi, w_rt_lo=w_rt_lo, b_rt=row(b_rt))


def kernel(x_prompt, x_sample, cache_k, cache_v, state_conv, state_lru_h, w_in, b_in, w_conv, b_conv, w_rg, b_rg,
           w_ig, b_ig, lru_lambda, sinks, w_lru_out, w_attn_out, w_o, ln1_g, ln1_b, w_group, b_group, w_router,
           b_router, w_gate, w_up, w_down, ln2_g, ln2_b):
    B, S, _ = x_prompt.shape
    n_prompt = B * S
    n_sample = x_sample.shape[0]
    n_all = n_prompt + n_sample
    wts = _prepare_weights(w_in, b_in, w_conv, b_conv, w_rg, b_rg, w_ig, b_ig, lru_lambda, w_lru_out, w_attn_out,
                           w_o, ln1_g, ln1_b, w_group, b_group, w_router, b_router)

    x_s = x_sample.reshape(n_sample, D_MODEL)
    u_s = _sample_proj_call(x_s, wts["w_in"], wts["b_in"])
    q3 = u_s[:, OFF_Q:OFF_K].reshape(n_sample, N_HEADS, HEAD_DIM)
    k_new = u_s[:, OFF_K:OFF_V]
    v_new = u_s[:, OFF_V:OFF_GL]
    att3, k_win_s, v_win_s = _sample_attn_call(
        q3, k_new.reshape(n_sample, N_KV, HEAD_DIM), v_new.reshape(n_sample, N_KV, HEAD_DIM),
        k_new.reshape(n_sample, 1, KV_WIDTH), v_new.reshape(n_sample, 1, KV_WIDTH),
        cache_k.reshape(n_sample, WINDOW, KV_WIDTH), cache_v.reshape(n_sample, WINDOW, KV_WIDTH),
        sinks.reshape(N_KV, GROUP, 1))
    x1_s, conv_s_t, h_s = _sample_mix_call(
        x_s, u_s, att3.reshape(n_sample, N_HEADS * HEAD_DIM), jnp.transpose(state_conv, (1, 0, 2)), state_lru_h, wts)

    x1_all, route_all, k_win_p, v_win_p, conv_p, h_p, xs, goff, gtot = _mixer_call(x_prompt, x1_s, sinks, wts)

    ys = _moe_call(gtot, xs, w_gate, w_up, w_down, n_all)
    goff = jnp.concatenate([goff, gtot[None]], axis=0)
    ln2_g2, ln2_b2 = ln2_g.reshape(1, -1), ln2_b.reshape(1, -1)
    y_p = _combine_call(goff, x1_all, route_all, ys, ln2_g2, ln2_b2, 0, n_prompt, DISPATCH_TILE, n_all)
    y_s = _combine_call(goff, x1_all, route_all, ys, ln2_g2, ln2_b2, n_prompt, n_sample, n_sample, n_all)

    kv_shape = (WINDOW, N_KV, HEAD_DIM)
    return (y_p.reshape(B, S, D_MODEL), y_s.reshape(n_sample, 1, D_MODEL),
            k_win_p.reshape((B,) + kv_shape), v_win_p.reshape((B,) + kv_shape), conv_p, h_p.reshape(B, LRU_WIDTH),
            k_win_s.reshape((n_sample,) + kv_shape), v_win_s.reshape((n_sample,) + kv_shape),
            jnp.transpose(conv_s_t, (1, 0, 2)), h_s)
```

```python
import functools

import jax
import jax.numpy as jnp
from jax import lax
from jax.experimental import pallas as pl
from jax.experimental.pallas import tpu as pltpu

F32 = jnp.float32
BF16 = jnp.bfloat16

D_MODEL = 1024
LRU_WIDTH = 1024
LRU_BLOCK = 64
CONV_W = 4
LRU_C = 8.0
N_HEADS = 16
N_KV = 4
GROUP = N_HEADS // N_KV
HEAD_DIM = 64
KV_WIDTH = N_KV * HEAD_DIM
WINDOW = 128
NEG_INF = -1e30
N_GROUPS = 4
EXPERTS_PER_GROUP = 8
N_EXPERTS = N_GROUPS * EXPERTS_PER_GROUP
TOP_K = 2
EXPERT_FF = D_MODEL // 2
DEPTH = 1
ALPHA = (2 * DEPTH) ** 0.25
LN_EPS = 1e-5
ATTN_SCALE = HEAD_DIM ** -0.5
LOG2_E = 1.4426950408889634

OFF_XL = 0
OFF_YL = OFF_XL + LRU_WIDTH
OFF_Q = OFF_YL + LRU_WIDTH
OFF_K = OFF_Q + N_HEADS * HEAD_DIM
OFF_V = OFF_K + KV_WIDTH
OFF_GL = OFF_V + KV_WIDTH
OFF_GA = OFF_GL + D_MODEL
IN_WIDTH = OFF_GA + D_MODEL

LANES = 128
SUBLANES = 8
MXU_DIM = 256
VMEM_LIMIT_BYTES = 56 * 1024 * 1024

GATE_CHUNK = MXU_DIM
N_GATE_CHUNKS = LRU_WIDTH // GATE_CHUNK
ROUTE_WIDTH = LANES

SEQ_TILE = 256
MOE_TILE = 1392
DISPATCH_TILE = 256
SLOT_CAP = 32
N_SLOTS = N_EXPERTS * SLOT_CAP
PAD_CHUNK = 128
SEG_PAD = MOE_TILE + SLOT_CAP + PAD_CHUNK
PACKED_WIDTH = D_MODEL // 2
SAMPLE_ATTN_TILE = 16
SAMPLE_PROJ_TILE = 512


def _const_spec(shape):
    nd = len(shape)
    return pl.BlockSpec(shape, lambda *_: (0,) * nd)


def _layer_norm(z, g, b):
    mu = jnp.mean(z, axis=-1, keepdims=True)
    zc = z - mu
    var = jnp.mean(zc * zc, axis=-1, keepdims=True)
    return zc * lax.rsqrt(var + LN_EPS) * g + b


def _sigmoid(x):
    return 1.0 / (1.0 + jnp.exp2(x * -LOG2_E))


def _softplus(x):
    return jnp.maximum(x, 0.0) + jnp.log1p(jnp.exp(-jnp.abs(x)))


def _lru_gates(xc, w_gates_ref, b_rg, b_ig, lam):
    xcb = xc.astype(BF16)
    r_parts, i_parts = [], []
    for c in range(N_GATE_CHUNKS):
        g = jnp.dot(xcb[:, c * GATE_CHUNK:(c + 1) * GATE_CHUNK], w_gates_ref[c], preferred_element_type=F32)
        r_parts.append(g[:, :GATE_CHUNK])
        i_parts.append(g[:, GATE_CHUNK:])
    r = _sigmoid(jnp.concatenate(r_parts, axis=1) + b_rg)
    i = _sigmoid(jnp.concatenate(i_parts, axis=1) + b_ig)
    log_a = (-LRU_C * r) * _softplus(-lam)
    a = jnp.exp(log_a)
    gain_sq = 1.0 - a * a
    gain = jnp.where(gain_sq > 0.0, gain_sq * lax.rsqrt(gain_sq), 0.0)
    u = gain * (i * xc)
    return a, u


def _linear_scan(a, u, h_in):
    n, w = a.shape
    groups = n // SUBLANES
    a3 = a.reshape(groups, SUBLANES, w)
    u3 = u.reshape(groups, SUBLANES, w)
    row = lax.broadcasted_iota(jnp.int32, a3.shape, 1)
    d = 1
    while d < SUBLANES:
        has_prev = row >= d
        u3 = u3 + a3 * jnp.where(has_prev, pltpu.roll(u3, d, axis=1), 0.0)
        a3 = a3 * jnp.where(has_prev, pltpu.roll(a3, d, axis=1), 1.0)
        d *= 2
    carry = h_in
    out = []
    for g in range(groups):
        h_g = u3[g] + a3[g] * carry
        out.append(h_g)
        carry = h_g[SUBLANES - 1:SUBLANES, :]
    return jnp.concatenate(out, axis=0)


def _route(x1, w_hi_ref, w_lo_ref, b_rt):
    x_hi = x1.astype(BF16)
    x_lo = (x1 - x_hi.astype(F32)).astype(BF16)
    w_hi = w_hi_ref[...]
    logits = (jnp.dot(x_hi, w_hi, preferred_element_type=F32)
              + (jnp.dot(x_lo, w_hi, preferred_element_type=F32)
                 + jnp.dot(x_hi, w_lo_ref[...], preferred_element_type=F32))) + b_rt
    col = lax.broadcasted_iota(jnp.int32, logits.shape, 1)
    big = jnp.int32(ROUTE_WIDTH)
    is_g = col < N_GROUPS
    gl = jnp.where(is_g, logits, -jnp.inf)
    gmax = jnp.max(gl, axis=-1, keepdims=True)
    g_idx = jnp.min(jnp.where(gl == gmax, col, big), axis=-1, keepdims=True)
    p_g = 1.0 / jnp.sum(jnp.where(is_g, jnp.exp(gl - gmax), 0.0), axis=-1, keepdims=True)
    lo = N_GROUPS + g_idx * EXPERTS_PER_GROUP
    in_grp = (col >= lo) & (col < lo + EXPERTS_PER_GROUP)
    el = jnp.where(in_grp, logits, -jnp.inf)
    v1 = jnp.max(el, axis=-1, keepdims=True)
    i1 = jnp.min(jnp.where(el == v1, col, big), axis=-1, keepdims=True)
    el2 = jnp.where(col == i1, -jnp.inf, el)
    v2 = jnp.max(el2, axis=-1, keepdims=True)
    i2 = jnp.min(jnp.where(el2 == v2, col, big), axis=-1, keepdims=True)
    e21 = jnp.exp(v2 - v1)
    inv = 1.0 / (1.0 + e21)
    w1 = p_g * inv
    w2 = p_g * (e21 * inv)
    e1 = (i1 - N_GROUPS).astype(F32)
    e2 = (i2 - N_GROUPS).astype(F32)
    return jnp.where(col == 0, e1, jnp.where(col == 1, e2, jnp.where(col == 2, w1, jnp.where(col == 3, w2, 0.0))))


def _merge_norm(x, rec, att, g_l, g_a, w_out_ref, ln_g, ln_b):
    rec_o = jnp.dot(rec.astype(BF16), w_out_ref[0], preferred_element_type=F32)
    att_o = jnp.dot(att.astype(BF16), w_out_ref[1], preferred_element_type=F32)
    merged = _sigmoid(g_l) * rec_o + _sigmoid(g_a) * att_o
    mix = jnp.dot(merged.astype(BF16), w_out_ref[2], preferred_element_type=F32)
    return _layer_norm(ALPHA * x + mix, ln_g, ln_b)


def _mixer_kernel(sinks_ref, x_ref, w_in_ref, b_in_ref, w_conv_ref, b_conv_ref, w_gates_ref, b_rg_ref, b_ig_ref,
                  lam_ref, w_out_ref, ln_g_ref, ln_b_ref, w_rt_hi_ref, w_rt_lo_ref,
                  b_rt_ref, x1_s_ref,
                  x1_ref, route_ref, kwin_ref, vwin_ref, conv_ref, h_ref, xs_hbm, goff_ref, gtot_ref,
                  conv_buf, h_carry, kcat, vcat, att_buf, prev_x1, stage, zero_rows, tri, xb_keep, key_keep,
                  g_vmem, gvec, g_smem, nd_smem, rounds_smem, sem_stage, sem_g, sem_zero,
                  *, tiles_per_seq, n_tiles, seg_cap):
    step = pl.program_id(0)
    last_step = pl.num_programs(0) - 1
    buf = lax.rem(step, 2)
    n_sample = x1_s_ref.shape[0]
    disp = _Dispatcher(xs_hbm, goff_ref, gtot_ref, stage, zero_rows, tri, xb_keep, key_keep, g_vmem, gvec,
                       g_smem, nd_smem, rounds_smem, sem_stage, sem_g, sem_zero, seg_cap)
    n_valid = jnp.where(step == 0, 0, jnp.where(step == last_step, n_sample, SEQ_TILE))

    @pl.when(step == 0)
    def _():
        disp.init()
        prev_x1[...] = jnp.zeros_like(prev_x1)

    def route_and_plan():
        x1_prev = prev_x1[...]
        route = _route(x1_prev, w_rt_hi_ref, w_rt_lo_ref, b_rt_ref[...])
        route_ref[...] = route
        disp.plan(x1_prev, route, n_valid, buf)

    @pl.when(step < n_tiles)
    def _():
        @pl.when(lax.rem(step, tiles_per_seq) == 0)
        def _():
            conv_buf[...] = jnp.zeros_like(conv_buf)
            h_carry[...] = jnp.zeros_like(h_carry)
            kcat[0:WINDOW, :] = jnp.zeros((WINDOW, KV_WIDTH), BF16)
            vcat[0:WINDOW, :] = jnp.zeros((WINDOW, KV_WIDTH), BF16)

        route_and_plan()
        _mixer_tile(lax.rem(step, tiles_per_seq), sinks_ref, x_ref, w_in_ref, b_in_ref, w_conv_ref, b_conv_ref,
                    w_gates_ref, b_rg_ref, b_ig_ref, lam_ref, w_out_ref, ln_g_ref,
                    ln_b_ref, x1_ref, prev_x1, kwin_ref, vwin_ref, conv_ref, h_ref,
                    conv_buf, h_carry, kcat, vcat, att_buf)

    @pl.when(step >= n_tiles)
    def _():
        route_and_plan()

        @pl.when(step == n_tiles)
        def _():
            x1_ref[0:n_sample, :] = x1_s_ref[...]
            prev_x1[0:n_sample, :] = x1_s_ref[...]

    disp.flush(buf, step == last_step)


def _mixer_tile(t, sinks_ref, x_ref, w_in_ref, b_in_ref, w_conv_ref, b_conv_ref, w_gates_ref, b_rg_ref, b_ig_ref,
                lam_ref, w_out_ref, ln_g_ref, ln_b_ref,
                x1_ref, x1_keep, kwin_ref, vwin_ref, conv_ref, h_ref,
                conv_buf, h_carry, kcat, vcat, att_buf):
    T = SEQ_TILE
    x = x_ref[0]
    xb = x.astype(BF16)

    def proj(lo, width):
        return jnp.dot(xb, w_in_ref[:, lo:lo + width], preferred_element_type=F32) + b_in_ref[:, lo:lo + width]

    xl = proj(OFF_XL, LRU_WIDTH)
    xl_ext = jnp.concatenate([conv_buf[...], xl], axis=0)

    def lagged(k):
        return pltpu.roll(xl_ext, k, axis=0)[SUBLANES:, :]
    wc = w_conv_ref[...]
    xc = wc[0:1] * lagged(3)
    xc = xc + wc[1:2] * lagged(2)
    xc = xc + wc[2:3] * lagged(1)
    xc = xc + wc[3:4] * xl + b_conv_ref[...]
    conv_ref[0] = xl[T - (CONV_W - 1):, :]
    conv_buf[...] = xl[T - SUBLANES:, :]

    a, u = _lru_gates(xc, w_gates_ref, b_rg_ref[...], b_ig_ref[...], lam_ref[...])
    h = _linear_scan(a, u, h_carry[0:1, :])
    h_last = h[T - 1:T, :]
    h_carry[0:1, :] = h_last
    h_ref[0] = h_last
    rec = h * jax.nn.gelu(proj(OFF_YL, LRU_WIDTH))

    q = proj(OFF_Q, N_HEADS * HEAD_DIM) * (ATTN_SCALE * LOG2_E)
    k = proj(OFF_K, KV_WIDTH)
    v = proj(OFF_V, KV_WIDTH)
    kwin_ref[0] = k[T - WINDOW:, :]
    vwin_ref[0] = v[T - WINDOW:, :]
    kcat[WINDOW:WINDOW + T, :] = k.astype(BF16)
    vcat[WINDOW:WINDOW + T, :] = v.astype(BF16)

    qi = lax.broadcasted_iota(jnp.int32, (WINDOW, 2 * WINDOW), 0)
    kj = lax.broadcasted_iota(jnp.int32, (WINDOW, 2 * WINDOW), 1)
    band = (kj > qi) & (kj <= qi + WINDOW)
    grp_row = lax.broadcasted_iota(jnp.int32, (GROUP * WINDOW, 1), 0) // WINDOW
    for qb in range(T // WINDOW):
        if qb == 0:
            first_key = jnp.where(t == 0, WINDOW, 0)
            mask1 = band & (kj >= first_key)
        else:
            mask1 = band
        bias = jnp.concatenate([jnp.where(mask1, 0.0, NEG_INF)] * GROUP, axis=0)
        r0 = qb * WINDOW
        qq = q[r0:r0 + WINDOW, :]
        for j in range(N_KV):
            kjb = kcat[r0:r0 + 2 * WINDOW, j * HEAD_DIM:(j + 1) * HEAD_DIM]
            vjb = vcat[r0:r0 + 2 * WINDOW, j * HEAD_DIM:(j + 1) * HEAD_DIM]
            qs = jnp.concatenate(
                [qq[:, (j * GROUP + g) * HEAD_DIM:(j * GROUP + g + 1) * HEAD_DIM] for g in range(GROUP)], axis=0)
            s = lax.dot_general(qs.astype(BF16), kjb, (((1,), (1,)), ((), ())), preferred_element_type=F32) + bias
            sink = jnp.zeros((GROUP * WINDOW, 1), F32)
            for g in range(GROUP):
                sink = jnp.where(grp_row == g, sinks_ref[j * GROUP + g] * LOG2_E, sink)
            m = jnp.maximum(jnp.max(s, axis=-1, keepdims=True), sink)
            p = jnp.exp2(s - m)
            inv = 1.0 / (jnp.sum(p, axis=-1, keepdims=True) + jnp.exp2(sink - m))
            o = jnp.dot((p * inv).astype(BF16), vjb, preferred_element_type=F32)
            for g in range(GROUP):
                hcol = (j * GROUP + g) * HEAD_DIM
                att_buf[r0:r0 + WINDOW, hcol:hcol + HEAD_DIM] = o[g * WINDOW:(g + 1) * WINDOW, :]
    kcat[0:WINDOW, :] = kcat[T:T + WINDOW, :]
    vcat[0:WINDOW, :] = vcat[T:T + WINDOW, :]

    x1 = _merge_norm(x, rec, att_buf[...], proj(OFF_GL, D_MODEL), proj(OFF_GA, D_MODEL),
                     w_out_ref, ln_g_ref[...], ln_b_ref[...])
    x1_ref[...] = x1
    x1_keep[...] = x1


def _mixer_call(x_prompt, x1_s, sinks, wts):
    B, S, _ = x_prompt.shape
    T = SEQ_TILE
    assert T == DISPATCH_TILE and x1_s.shape[0] <= T
    nt = S // T
    n_tiles = B * nt
    n_rows_total = B * S + x1_s.shape[0]
    n_dispatch = _dispatch_tiles(n_rows_total)
    assert n_dispatch == n_tiles + 1
    seg_cap = _segment_capacity(n_rows_total)
    n_assign = TOP_K * T
    weight_args = (wts["w_in"], wts["b_in"], wts["w_conv"], wts["b_conv"], wts["w_gates"], wts["b_rg"], wts["b_ig"],
                   wts["lam"], wts["w_out"], wts["ln1_g"], wts["ln1_b"],
                   wts["w_rt_hi"], wts["w_rt_lo"], wts["b_rt"], x1_s)
    mixed = lambda i: jnp.minimum(i, n_tiles - 1)
    seq = lambda i: mixed(i) // nt
    routed = lambda i: jnp.clip(i - 1, 0, n_dispatch - 1)
    in_specs = [pl.BlockSpec(memory_space=pltpu.SMEM),
                pl.BlockSpec((1, T, D_MODEL), lambda i: (seq(i), lax.rem(mixed(i), nt), 0))]
    in_specs += [_const_spec(w.shape) for w in weight_args]
    out_shape = (
        jax.ShapeDtypeStruct((n_rows_total, D_MODEL), F32),
        jax.ShapeDtypeStruct((n_rows_total, ROUTE_WIDTH), F32),
        jax.ShapeDtypeStruct((B, WINDOW, KV_WIDTH), F32),
        jax.ShapeDtypeStruct((B, WINDOW, KV_WIDTH), F32),
        jax.ShapeDtypeStruct((B, CONV_W - 1, LRU_WIDTH), F32),
        jax.ShapeDtypeStruct((B, 1, LRU_WIDTH), F32),
        jax.ShapeDtypeStruct((N_EXPERTS * seg_cap, PACKED_WIDTH), jnp.uint32),
        jax.ShapeDtypeStruct((n_dispatch, 1, ROUTE_WIDTH), jnp.int32),
        jax.ShapeDtypeStruct((1, ROUTE_WIDTH), jnp.int32),
    )
    out_specs = (
        pl.BlockSpec((T, D_MODEL), lambda i: (jnp.minimum(i, n_tiles), 0)),
        pl.BlockSpec((T, ROUTE_WIDTH), lambda i: (routed(i), 0)),
        pl.BlockSpec((1, WINDOW, KV_WIDTH), lambda i: (seq(i), 0, 0)),
        pl.BlockSpec((1, WINDOW, KV_WIDTH), lambda i: (seq(i), 0, 0)),
        pl.BlockSpec((1, CONV_W - 1, LRU_WIDTH), lambda i: (seq(i), 0, 0)),
        pl.BlockSpec((1, 1, LRU_WIDTH), lambda i: (seq(i), 0, 0)),
        pl.BlockSpec(memory_space=pl.ANY),
        pl.BlockSpec((1, 1, ROUTE_WIDTH), lambda i: (routed(i), 0, 0)),
        pl.BlockSpec((1, ROUTE_WIDTH), lambda i: (0, 0)),
    )
    scratch = [
        pltpu.VMEM((SUBLANES, LRU_WIDTH), F32),
        pltpu.VMEM((SUBLANES, LRU_WIDTH), F32),
        pltpu.VMEM((T + WINDOW, KV_WIDTH), BF16),
        pltpu.VMEM((T + WINDOW, KV_WIDTH), BF16),
        pltpu.VMEM((T, N_HEADS * HEAD_DIM), F32),
        pltpu.VMEM((T, D_MODEL), F32),
        pltpu.VMEM((2, N_SLOTS, PACKED_WIDTH), jnp.uint32),
        pltpu.VMEM((PAD_CHUNK, PACKED_WIDTH), jnp.uint32),
        pltpu.VMEM((n_assign, n_assign), BF16),
        pltpu.VMEM((T, D_MODEL), BF16),
        pltpu.VMEM((SUBLANES, n_assign), F32),
        pltpu.VMEM((SUBLANES, ROUTE_WIDTH), jnp.int32),
        pltpu.VMEM((SUBLANES, ROUTE_WIDTH), F32),
        pltpu.SMEM((1, ROUTE_WIDTH), jnp.int32),
        pltpu.SMEM((1,), jnp.int32),
        pltpu.SMEM((1,), jnp.int32),
        pltpu.SemaphoreType.DMA(()),
        pltpu.SemaphoreType.DMA(()),
        pltpu.SemaphoreType.DMA(()),
    ]
    return pl.pallas_call(
        functools.partial(_mixer_kernel, tiles_per_seq=nt, n_tiles=n_tiles, seg_cap=seg_cap),
        grid=(n_tiles + 2,),
        in_specs=in_specs,
        out_specs=out_specs,
        out_shape=out_shape,
        scratch_shapes=scratch,
        compiler_params=pltpu.CompilerParams(dimension_semantics=("arbitrary",),
                                             vmem_limit_bytes=VMEM_LIMIT_BYTES),
        name="mixer_prompt",
    )(sinks, x_prompt, *weight_args)


def _sample_proj_kernel(x_ref, w_ref, b_ref, u_ref):
    u_ref[...] = jnp.dot(x_ref[...].astype(BF16), w_ref[...], preferred_element_type=F32) + b_ref[...]


def _sample_proj_call(x_s, w_in, b_in):
    n = x_s.shape[0]
    tn = SAMPLE_PROJ_TILE
    return pl.pallas_call(
        _sample_proj_kernel,
        grid=(IN_WIDTH // tn,),
        in_specs=[pl.BlockSpec((n, D_MODEL), lambda c: (0, 0)),
                  pl.BlockSpec((D_MODEL, tn), lambda c: (0, c)),
                  pl.BlockSpec((1, tn), lambda c: (0, c))],
        out_specs=pl.BlockSpec((n, tn), lambda c: (0, c)),
        out_shape=jax.ShapeDtypeStruct((n, IN_WIDTH), F32),
        compiler_params=pltpu.CompilerParams(dimension_semantics=("arbitrary",)),
        name="sample_proj",
    )(x_s, w_in, b_in)


def _sample_attn_kernel(q_ref, kn_ref, vn_ref, kn_row_ref, vn_row_ref, ck_ref, cv_ref, sinks_ref,
                        att_ref, kwin_ref, vwin_ref):
    tb = q_ref.shape[0]
    key_pos = lax.broadcasted_iota(jnp.int32, (tb, GROUP, WINDOW), 2)
    for j in range(N_KV):
        qj = q_ref[:, j * GROUP:(j + 1) * GROUP, :]
        kc = ck_ref[:, :, j * HEAD_DIM:(j + 1) * HEAD_DIM]
        vc = cv_ref[:, :, j * HEAD_DIM:(j + 1) * HEAD_DIM]
        s_c = jnp.einsum("bgd,bsd->bgs", qj.astype(BF16), kc.astype(BF16), preferred_element_type=F32) * ATTN_SCALE
        s_c = jnp.where(key_pos >= 1, s_c, NEG_INF)
        kn = kn_ref[:, j:j + 1, :]
        vn = vn_ref[:, j:j + 1, :]
        s_n = jnp.sum(qj * kn, axis=-1, keepdims=True) * ATTN_SCALE
        sink = sinks_ref[j][None]
        m = jnp.maximum(jnp.maximum(jnp.max(s_c, axis=-1, keepdims=True), s_n), sink)
        p_c = jnp.exp(s_c - m)
        p_n = jnp.exp(s_n - m)
        inv = 1.0 / (jnp.sum(p_c, axis=-1, keepdims=True) + p_n + jnp.exp(sink - m))
        o = jnp.einsum("bgs,bsd->bgd", (p_c * inv).astype(BF16), vc.astype(BF16), preferred_element_type=F32)
        att_ref[:, j * GROUP:(j + 1) * GROUP, :] = o + (p_n * inv) * vn
    kwin_ref[:, 0:WINDOW - 1, :] = ck_ref[:, 1:WINDOW, :]
    kwin_ref[:, WINDOW - 1:WINDOW, :] = kn_row_ref[...]
    vwin_ref[:, 0:WINDOW - 1, :] = cv_ref[:, 1:WINDOW, :]
    vwin_ref[:, WINDOW - 1:WINDOW, :] = vn_row_ref[...]


def _sample_attn_call(q3, kn3, vn3, kn_row, vn_row, ck, cv, sinks3):
    n = q3.shape[0]
    tb = SAMPLE_ATTN_TILE
    b3 = lambda i: (i, 0, 0)
    return pl.pallas_call(
        _sample_attn_kernel,
        grid=(n // tb,),
        in_specs=[pl.BlockSpec((tb, N_HEADS, HEAD_DIM), b3),
                  pl.BlockSpec((tb, N_KV, HEAD_DIM), b3),
                  pl.BlockSpec((tb, N_KV, HEAD_DIM), b3),
                  pl.BlockSpec((tb, 1, KV_WIDTH), b3),
                  pl.BlockSpec((tb, 1, KV_WIDTH), b3),
                  pl.BlockSpec((tb, WINDOW, KV_WIDTH), b3),
                  pl.BlockSpec((tb, WINDOW, KV_WIDTH), b3),
                  pl.BlockSpec((N_KV, GROUP, 1), lambda i: (0, 0, 0))],
        out_specs=(pl.BlockSpec((tb, N_HEADS, HEAD_DIM), b3),
                   pl.BlockSpec((tb, WINDOW, KV_WIDTH), b3),
                   pl.BlockSpec((tb, WINDOW, KV_WIDTH), b3)),
        out_shape=(jax.ShapeDtypeStruct((n, N_HEADS, HEAD_DIM), F32),
                   jax.ShapeDtypeStruct((n, WINDOW, KV_WIDTH), F32),
                   jax.ShapeDtypeStruct((n, WINDOW, KV_WIDTH), F32)),
        compiler_params=pltpu.CompilerParams(dimension_semantics=("arbitrary",)),
        name="sample_attn",
    )(q3, kn3, vn3, kn_row, vn_row, ck, cv, sinks3)


def _sample_mix_kernel(x_ref, u_ref, att_ref, st_ref, h0_ref, w_conv_ref, b_conv_ref, w_gates_ref, b_rg_ref,
                       b_ig_ref, lam_ref, w_out_ref, ln_g_ref, ln_b_ref,
                       x1_ref, conv_ref, h_ref):
    xl = u_ref[:, OFF_XL:OFF_XL + LRU_WIDTH]
    wc = w_conv_ref[...]
    xc = wc[0:1] * st_ref[0]
    xc = xc + wc[1:2] * st_ref[1]
    xc = xc + wc[2:3] * st_ref[2]
    xc = xc + wc[3:4] * xl + b_conv_ref[...]
    conv_ref[0] = st_ref[1]
    conv_ref[1] = st_ref[2]
    conv_ref[2] = xl
    a, u = _lru_gates(xc, w_gates_ref, b_rg_ref[...], b_ig_ref[...], lam_ref[...])
    h = a * h0_ref[...] + u
    h_ref[...] = h
    rec = h * jax.nn.gelu(u_ref[:, OFF_YL:OFF_YL + LRU_WIDTH])
    x1_ref[...] = _merge_norm(x_ref[...], rec, att_ref[...], u_ref[:, OFF_GL:OFF_GL + D_MODEL],
                              u_ref[:, OFF_GA:OFF_GA + D_MODEL], w_out_ref,
                              ln_g_ref[...], ln_b_ref[...])


def _sample_mix_call(x_s, u_s, att, st_t, h0, wts):
    n = x_s.shape[0]
    weight_args = (wts["w_conv"], wts["b_conv"], wts["w_gates"], wts["b_rg"], wts["b_ig"], wts["lam"],
                   wts["w_out"], wts["ln1_g"], wts["ln1_b"])
    args = (x_s, u_s, att, st_t, h0) + weight_args
    out_shapes = ((n, D_MODEL), (CONV_W - 1, n, LRU_WIDTH), (n, LRU_WIDTH))
    return pl.pallas_call(
        _sample_mix_kernel,
        grid=(1,),
        in_specs=[_const_spec(a.shape) for a in args],
        out_specs=tuple(_const_spec(s) for s in out_shapes),
        out_shape=tuple(jax.ShapeDtypeStruct(s, F32) for s in out_shapes),
        compiler_params=pltpu.CompilerParams(dimension_semantics=("arbitrary",),
                                             vmem_limit_bytes=VMEM_LIMIT_BYTES),
        name="sample_mix",
    )(*args)


def _one_hot(mask):
    return jnp.where(mask, 1.0, 0.0).astype(BF16)


def _pack_rows(x):
    half = x.shape[1] // 2
    lo = lax.shift_right_logical(lax.bitcast_convert_type(x[:, :half], jnp.uint32), jnp.uint32(16))
    hi = lax.bitcast_convert_type(x[:, half:], jnp.uint32) & jnp.uint32(0xFFFF0000)
    return lo | hi


def _unpack_rows(words):
    lo = lax.bitcast_convert_type(lax.shift_left(words, jnp.uint32(16)), F32)
    hi = lax.bitcast_convert_type(words & jnp.uint32(0xFFFF0000), F32)
    return jnp.concatenate([lo.astype(BF16), hi.astype(BF16)], axis=1)


class _Dispatcher:
    def __init__(self, xs_hbm, goff_ref, gtot_ref, stage, zero_rows, tri, xb_keep, key_keep, g_vmem, gvec,
                 g_smem, nd_smem, rounds_smem, sem_stage, sem_g, sem_zero, seg_cap):
        self.xs_hbm, self.goff_ref, self.gtot_ref = xs_hbm, goff_ref, gtot_ref
        self.stage, self.zero_rows, self.tri = stage, zero_rows, tri
        self.xb_keep, self.key_keep = xb_keep, key_keep
        self.g_vmem, self.gvec, self.g_smem = g_vmem, gvec, g_smem
        self.nd_smem, self.rounds_smem = nd_smem, rounds_smem
        self.sem_stage, self.sem_g, self.sem_zero = sem_stage, sem_g, sem_zero
        self.seg_cap = seg_cap

    def _g_copy(self):
        return pltpu.make_async_copy(self.g_vmem.at[pl.ds(0, 1)], self.g_smem, self.sem_g)

    def init(self):
        n_assign = self.tri.shape[0]
        r = lax.broadcasted_iota(jnp.int32, (n_assign, n_assign), 0)
        c = lax.broadcasted_iota(jnp.int32, (n_assign, n_assign), 1)
        self.tri[...] = _one_hot(r < c)
        self.gvec[...] = jnp.zeros_like(self.gvec)
        self.g_vmem[...] = jnp.zeros_like(self.g_vmem)
        self.nd_smem[0] = 0
        self._g_copy().start()

    def _sorted_rows(self, e_row, rank_row, xb, lo):
        td = xb.shape[0]
        slot = lax.broadcasted_iota(jnp.int32, (N_SLOTS, td), 0).astype(F32)
        lo_f = lo.astype(F32)
        in_round = (rank_row >= lo_f) & (rank_row < lo_f + SLOT_CAP) & (e_row >= 0.0)
        key = jnp.where(in_round, e_row * SLOT_CAP + (rank_row - lo_f), -1.0)
        perm = jnp.where(slot == key[:, :td], 1.0, jnp.where(slot == key[:, td:], 1.0, 0.0)).astype(BF16)
        return _pack_rows(jnp.dot(perm, xb, preferred_element_type=F32))

    def plan(self, x1, route, n_valid, buf):
        td = x1.shape[0]
        n_assign = TOP_K * td
        valid_col = lax.broadcasted_iota(jnp.int32, (td, 1), 0) < n_valid
        lane = lax.broadcasted_iota(jnp.int32, (td, ROUTE_WIDTH), 1).astype(F32)
        e0c = jnp.where(valid_col, route[:, 0:1], -1.0)
        e1c = jnp.where(valid_col, route[:, 1:2], -1.0)
        cnt_row = jnp.sum(jnp.where(lane == e0c, 1.0, 0.0) + jnp.where(lane == e1c, 1.0, 0.0),
                          axis=0, keepdims=True)
        self.rounds_smem[0] = (jnp.max(cnt_row).astype(jnp.int32) + SLOT_CAP - 1) // SLOT_CAP
        self.goff_ref[0] = self.gvec[0:1, :].astype(jnp.int32)
        self.gvec[0:1, :] = self.gvec[0:1, :] + jnp.ceil(cnt_row * (1.0 / SUBLANES)) * SUBLANES
        self.gtot_ref[...] = self.gvec[0:1, :].astype(jnp.int32)

        route_t = route.T
        valid_row = lax.broadcasted_iota(jnp.int32, (1, td), 1) < n_valid
        e_row = jnp.concatenate([jnp.where(valid_row, route_t[0:1, :], -1.0),
                                 jnp.where(valid_row, route_t[1:2, :], -1.0)], axis=1)
        expert_sub = lax.broadcasted_iota(jnp.int32, (N_EXPERTS, n_assign), 0).astype(F32)
        onehot_t = expert_sub == e_row
        rank_t = jnp.dot(_one_hot(onehot_t), self.tri[...], preferred_element_type=F32)
        rank_row = jnp.sum(jnp.where(onehot_t, rank_t, 0.0), axis=0, keepdims=True)
        xb = jnp.where(valid_col, x1, 0.0).astype(BF16)
        self.xb_keep[...] = xb
        self.key_keep[0:1, :] = e_row
        self.key_keep[1:2, :] = rank_row
        self.stage[buf] = self._sorted_rows(e_row, rank_row, xb, jnp.int32(0))

    def _stage_copy(self, buf, e, dst_row):
        src = self.stage.at[buf, pl.ds(pl.multiple_of(e * SLOT_CAP, SLOT_CAP), SLOT_CAP)]
        return pltpu.make_async_copy(src, self.xs_hbm.at[pl.ds(dst_row, SLOT_CAP)], self.sem_stage)

    def _wait_outstanding(self):
        def wait_one(_, carry):
            self._stage_copy(0, 0, 0).wait()
            return carry
        lax.fori_loop(0, self.nd_smem[0], wait_one, 0)
        self.nd_smem[0] = 0

    def _start_round(self, buf, lo):
        def issue(e, carry):
            dst_row = pl.multiple_of(e * self.seg_cap + self.g_smem[0, e] + lo, SUBLANES)
            self._stage_copy(buf, e, dst_row).start()
            return carry
        lax.fori_loop(0, N_EXPERTS, issue, 0)
        self.nd_smem[0] = N_EXPERTS

    def flush(self, buf, is_last):
        self._g_copy().wait()
        self._wait_outstanding()
        self._start_round(buf, 0)

        def later_round(r, carry):
            lo = r * SLOT_CAP
            rows = self._sorted_rows(self.key_keep[0:1, :], self.key_keep[1:2, :], self.xb_keep[...], lo)
            self._wait_outstanding()
            self.stage[buf] = rows
            self._start_round(buf, lo)
            return carry
        lax.fori_loop(1, self.rounds_smem[0], later_round, 0)

        self.g_vmem[0:1, :] = self.gvec[0:1, :].astype(jnp.int32)
        self._g_copy().start()

        @pl.when(is_last)
        def _():
            self._g_copy().wait()
            self._wait_outstanding()
            self.zero_rows[...] = jnp.zeros_like(self.zero_rows)

            def pad_copy(dst_row):
                dst = self.xs_hbm.at[pl.ds(pl.multiple_of(dst_row, SUBLANES), PAD_CHUNK)]
                return pltpu.make_async_copy(self.zero_rows, dst, self.sem_zero)

            def pad_segment(e, n_started):
                fill = self.g_smem[0, e]
                block_end = (fill + SLOT_CAP + MOE_TILE - 1) // MOE_TILE * MOE_TILE
                n_chunks = (block_end - fill + PAD_CHUNK - 1) // PAD_CHUNK

                def start(c, carry):
                    pad_copy(e * self.seg_cap + fill + c * PAD_CHUNK).start()
                    return carry
                lax.fori_loop(0, n_chunks, start, 0)
                return n_started + n_chunks
            n_started = lax.fori_loop(0, N_EXPERTS, pad_segment, 0)

            def wait(_, carry):
                pad_copy(0).wait()
                return carry
            lax.fori_loop(0, n_started, wait, 0)


def _dispatch_tiles(n_tokens):
    return (n_tokens + DISPATCH_TILE - 1) // DISPATCH_TILE


def _segment_capacity(n_tokens):
    assert SEG_PAD >= DISPATCH_TILE
    alignment_slack = (SUBLANES - 1) * _dispatch_tiles(n_tokens)
    return (n_tokens + alignment_slack + SEG_PAD + MOE_TILE - 1) // MOE_TILE * MOE_TILE


def _moe_kernel(blk_e_ref, blk_j_ref, n_used_ref, xs_ref, wg_ref, wu_ref, wd_ref, ys_ref, wg_bf, wu_bf, wd_bf):
    del blk_j_ref
    b = pl.program_id(0)

    @pl.when(b < n_used_ref[0])
    def _():
        prev_e = blk_e_ref[jnp.maximum(b - 1, 0)]

        @pl.when((b == 0) | (blk_e_ref[b] != prev_e))
        def _():
            wg_bf[...] = wg_ref[0].astype(BF16)
            wu_bf[...] = wu_ref[0].astype(BF16)
            wd_bf[...] = wd_ref[0].astype(BF16)

        xb = _unpack_rows(xs_ref[...])
        g = jnp.dot(xb, wg_bf[...], preferred_element_type=F32)
        u = jnp.dot(xb, wu_bf[...], preferred_element_type=F32)
        hmid = (g * _sigmoid(g)) * u
        y = jnp.dot(hmid.astype(BF16), wd_bf[...], preferred_element_type=F32)
        ys_ref[...] = _pack_rows(y.astype(BF16).astype(F32))


def _expert_blocks(gtot, n_blocks):
    rows = gtot[0, :N_EXPERTS]
    nb = (rows + SLOT_CAP + MOE_TILE - 1) // MOE_TILE
    ends = jnp.cumsum(nb)
    n_used = ends[-1]
    step = jnp.minimum(jnp.arange(n_blocks, dtype=jnp.int32), n_used - 1)
    blk_e = jnp.sum((step[:, None] >= ends[None, :]).astype(jnp.int32), axis=1)
    blk_j = step - (ends - nb)[blk_e]
    return blk_e, blk_j.astype(jnp.int32), n_used.reshape(1).astype(jnp.int32)


def _moe_call(gtot, xs, w_gate, w_up, w_down, n_tokens):
    seg_cap = _segment_capacity(n_tokens)
    seg_blocks = seg_cap // MOE_TILE
    max_rows = n_tokens * TOP_K + N_EXPERTS * (SUBLANES - 1) * _dispatch_tiles(n_tokens)
    n_blocks = (max_rows + N_EXPERTS * (SLOT_CAP + MOE_TILE - 1)) // MOE_TILE
    blk_e, blk_j, n_used = _expert_blocks(gtot, n_blocks)
    row_block = lambda b, be, bj, nu: (be[b] * seg_blocks + bj[b], 0)
    weight_block = lambda b, be, bj, nu: (be[b], 0, 0)
    grid_spec = pltpu.PrefetchScalarGridSpec(
        num_scalar_prefetch=3,
        grid=(n_blocks,),
        in_specs=[
            pl.BlockSpec((MOE_TILE, PACKED_WIDTH), row_block),
            pl.BlockSpec((1, D_MODEL, EXPERT_FF), weight_block),
            pl.BlockSpec((1, D_MODEL, EXPERT_FF), weight_block),
            pl.BlockSpec((1, EXPERT_FF, D_MODEL), weight_block),
        ],
        out_specs=pl.BlockSpec((MOE_TILE, PACKED_WIDTH), row_block),
        scratch_shapes=[
            pltpu.VMEM((D_MODEL, EXPERT_FF), BF16),
            pltpu.VMEM((D_MODEL, EXPERT_FF), BF16),
            pltpu.VMEM((EXPERT_FF, D_MODEL), BF16),
        ],
    )
    return pl.pallas_call(
        _moe_kernel,
        grid_spec=grid_spec,
        out_shape=jax.ShapeDtypeStruct(xs.shape, xs.dtype),
        compiler_params=pltpu.CompilerParams(dimension_semantics=("arbitrary",),
                                             vmem_limit_bytes=VMEM_LIMIT_BYTES),
        name="moe_experts",
    )(blk_e, blk_j, n_used, xs, w_gate, w_up, w_down)


def _combine_kernel(gcur_ref, gnext_ref, x1_ref, route_ref, ys_hbm, ln_g_ref, ln_b_ref, y_ref,
                    stage, acc, tri, sem, *, seg_cap):
    td = x1_ref.shape[0]
    n_assign = TOP_K * td
    i = pl.program_id(0)
    buf = lax.rem(i, 2)

    def stage_copy(g_ref, e, first_rank, to_buf):
        src_row = pl.multiple_of(e * seg_cap + g_ref[0, 0, e] + first_rank, SUBLANES)
        dst = stage.at[to_buf, pl.ds(pl.multiple_of(e * SLOT_CAP, SLOT_CAP), SLOT_CAP)]
        return pltpu.make_async_copy(ys_hbm.at[pl.ds(src_row, SLOT_CAP)], dst, sem.at[to_buf])

    def fetch_first_round(g_ref, to_buf):
        def start(e, carry):
            stage_copy(g_ref, e, 0, to_buf).start()
            return carry
        lax.fori_loop(0, N_EXPERTS, start, 0)

    @pl.when(i == 0)
    def _():
        r = lax.broadcasted_iota(jnp.int32, (n_assign, n_assign), 0)
        c = lax.broadcasted_iota(jnp.int32, (n_assign, n_assign), 1)
        tri[...] = _one_hot(c < r)
        fetch_first_round(gcur_ref, 0)

    def count(e):
        return gnext_ref[0, 0, e] - gcur_ref[0, 0, e]

    n_max = lax.fori_loop(0, N_EXPERTS, lambda e, m: jnp.maximum(m, count(e)), 0)
    rounds = (n_max + SLOT_CAP - 1) // SLOT_CAP

    route = route_ref[...]
    e0, e1, w0, w1 = route[:, 0:1], route[:, 1:2], route[:, 2:3], route[:, 3:4]
    lane = lax.broadcasted_iota(jnp.int32, (td, ROUTE_WIDTH), 1).astype(F32)
    onehot = jnp.concatenate([lane == e0, lane == e1], axis=0)
    onehot_f = jnp.where(onehot, 1.0, 0.0)
    rank_mat = jnp.dot(tri[...], onehot_f.astype(BF16), preferred_element_type=F32)
    rank = jnp.sum(jnp.where(onehot, rank_mat, 0.0), axis=1, keepdims=True)
    r0, r1 = rank[:td], rank[td:]
    cnt_row = jnp.sum(onehot_f, axis=0, keepdims=True)

    slot_col = lax.broadcasted_iota(jnp.int32, (N_SLOTS, 1), 0)
    slot_expert = slot_col // SLOT_CAP
    slot_rank = slot_col - slot_expert * SLOT_CAP
    lane_s = lax.broadcasted_iota(jnp.int32, (N_SLOTS, ROUTE_WIDTH), 1)
    n_col = jnp.sum(jnp.where(lane_s == slot_expert, cnt_row, 0.0), axis=1, keepdims=True)
    slot_lane = lax.broadcasted_iota(jnp.int32, (td, N_SLOTS), 1).astype(F32)

    def wait_copies(n):
        def wait_one(_, c):
            stage_copy(gcur_ref, 0, 0, buf).wait()
            return c
        lax.fori_loop(0, n, wait_one, 0)

    def weighted_rows(lo):
        lo_f = lo.astype(F32)
        live = (slot_rank + lo).astype(F32) < n_col
        rows = _unpack_rows(jnp.where(live, stage[buf], jnp.uint32(0)))

        def selector(e_col, r_col, w_col):
            in_round = (r_col >= lo_f) & (r_col < lo_f + SLOT_CAP)
            key = jnp.where(in_round, e_col * SLOT_CAP + (r_col - lo_f), -1.0)
            return jnp.where(slot_lane == key, w_col, 0.0)
        mix = (selector(e0, r0, w0) + selector(e1, r1, w1)).astype(BF16)
        return jnp.dot(mix, rows, preferred_element_type=F32)

    wait_copies(N_EXPERTS)

    @pl.when(i + 1 < pl.num_programs(0))
    def _():
        fetch_first_round(gnext_ref, 1 - buf)

    acc[...] = weighted_rows(jnp.int32(0))

    def later_round(r, carry):
        lo = r * SLOT_CAP

        def start(e, n_started):
            has_rows = count(e) > lo

            @pl.when(has_rows)
            def _():
                stage_copy(gcur_ref, e, lo, buf).start()
            return n_started + has_rows.astype(jnp.int32)
        wait_copies(lax.fori_loop(0, N_EXPERTS, start, 0))
        acc[...] = acc[...] + weighted_rows(lo)
        return carry
    lax.fori_loop(1, rounds, later_round, 0)

    y_ref[...] = _layer_norm(ALPHA * x1_ref[...] + acc[...], ln_g_ref[...], ln_b_ref[...])


def _combine_call(goff, x1_all, route_all, ys, ln_g, ln_b, first_row, n_rows, tile, n_tokens):
    assert first_row % DISPATCH_TILE == 0 and first_row % tile == 0
    assert tile == DISPATCH_TILE or n_rows == tile
    first_block = first_row // tile
    first_goff = first_row // DISPATCH_TILE
    n_assign = TOP_K * tile
    grid_spec = pl.GridSpec(
        grid=(n_rows // tile,),
        in_specs=[
            pl.BlockSpec((1, 1, ROUTE_WIDTH), lambda i: (first_goff + i, 0, 0), memory_space=pltpu.SMEM),
            pl.BlockSpec((1, 1, ROUTE_WIDTH), lambda i: (first_goff + i + 1, 0, 0), memory_space=pltpu.SMEM),
            pl.BlockSpec((tile, D_MODEL), lambda i: (first_block + i, 0)),
            pl.BlockSpec((tile, ROUTE_WIDTH), lambda i: (first_block + i, 0)),
            pl.BlockSpec(memory_space=pl.ANY),
            _const_spec(ln_g.shape),
            _const_spec(ln_b.shape),
        ],
        out_specs=pl.BlockSpec((tile, D_MODEL), lambda i: (i, 0)),
        scratch_shapes=[pltpu.VMEM((2, N_SLOTS, PACKED_WIDTH), jnp.uint32),
                        pltpu.VMEM((tile, D_MODEL), F32),
                        pltpu.VMEM((n_assign, n_assign), BF16),
                        pltpu.SemaphoreType.DMA((2,))],
    )
    return pl.pallas_call(
        functools.partial(_combine_kernel, seg_cap=_segment_capacity(n_tokens)),
        grid_spec=grid_spec,
        out_shape=jax.ShapeDtypeStruct((n_rows, D_MODEL), F32),
        compiler_params=pltpu.CompilerParams(dimension_semantics=("arbitrary",),
                                             vmem_limit_bytes=VMEM_LIMIT_BYTES),
        name="moe_combine",
    )(goff, goff, x1_all, route_all, ys, ln_g, ln_b)


def _prepare_weights(w_in, b_in, w_conv, b_conv, w_rg, b_rg, w_ig, b_ig, lru_lambda, w_lru_out, w_attn_out, w_o,
                     ln1_g, ln1_b, w_group, b_group, w_router, b_router):
    blocks_per_chunk = GATE_CHUNK // LRU_BLOCK

    def chunked_block_diag(w):
        w = w.reshape(N_GATE_CHUNKS, blocks_per_chunk, LRU_BLOCK, LRU_BLOCK)
        eye = jnp.eye(blocks_per_chunk, dtype=w.dtype)
        return jnp.einsum("cbij,bd->cbidj", w, eye).reshape(N_GATE_CHUNKS, GATE_CHUNK, GATE_CHUNK)

    w_gates = jnp.concatenate([chunked_block_diag(w_rg), chunked_block_diag(w_ig)], axis=-1).astype(BF16)
    w_rt = jnp.concatenate([w_group, w_router], axis=1)
    w_rt = jnp.pad(w_rt, ((0, 0), (0, ROUTE_WIDTH - w_rt.shape[1])))
    w_rt_hi = w_rt.astype(BF16)
    w_rt_lo = (w_rt - w_rt_hi.astype(F32)).astype(BF16)
    b_rt = jnp.pad(jnp.concatenate([b_group, b_router]), (0, ROUTE_WIDTH - N_GROUPS - N_EXPERTS))
    row = lambda v: v.reshape(1, -1)
    return dict(
        w_in=w_in.astype(BF16), b_in=row(b_in), w_conv=w_conv, b_conv=row(b_conv), w_gates=w_gates,
        b_rg=row(b_rg), b_ig=row(b_ig), lam=row(lru_lambda),
        w_out=jnp.stack([w_lru_out, w_attn_out, w_o]).astype(BF16),
        ln1_g=row(ln1_g), ln1_b=row(ln1_b), w_rt_hi=w_rt_hi, w_rt_lo=w_rt_lo, b_rt=row(b_rt))


def kernel(x_prompt, x_sample, cache_k, cache_v, state_conv, state_lru_h, w_in, b_in, w_conv, b_conv, w_rg, b_rg,
           w_ig, b_ig, lru_lambda, sinks, w_lru_out, w_attn_out, w_o, ln1_g, ln1_b, w_group, b_group, w_router,
           b_router, w_gate, w_up, w_down, ln2_g, ln2_b):
    B, S, _ = x_prompt.shape
    n_prompt = B * S
    n_sample = x_sample.shape[0]
    n_all = n_prompt + n_sample
    wts = _prepare_weights(w_in, b_in, w_conv, b_conv, w_rg, b_rg, w_ig, b_ig, lru_lambda, w_lru_out, w_attn_out,
                           w_o, ln1_g, ln1_b, w_group, b_group, w_router, b_router)

    x_s = x_sample.reshape(n_sample, D_MODEL)
    u_s = _sample_proj_call(x_s, wts["w_in"], wts["b_in"])
    q3 = u_s[:, OFF_Q:OFF_K].reshape(n_sample, N_HEADS, HEAD_DIM)
    k_new = u_s[:, OFF_K:OFF_V]
    v_new = u_s[:, OFF_V:OFF_GL]
    att3, k_win_s, v_win_s = _sample_attn_call(
        q3, k_new.reshape(n_sample, N_KV, HEAD_DIM), v_new.reshape(n_sample, N_KV, HEAD_DIM),
        k_new.reshape(n_sample, 1, KV_WIDTH), v_new.reshape(n_sample, 1, KV_WIDTH),
        cache_k.reshape(n_sample, WINDOW, KV_WIDTH), cache_v.reshape(n_sample, WINDOW, KV_WIDTH),
        sinks.reshape(N_KV, GROUP, 1))
    x1_s, conv_s_t, h_s = _sample_mix_call(
        x_s, u_s, att3.reshape(n_sample, N_HEADS * HEAD_DIM), jnp.transpose(state_conv, (1, 0, 2)), state_lru_h, wts)

    x1_all, route_all, k_win_p, v_win_p, conv_p, h_p, xs, goff, gtot = _mixer_call(x_prompt, x1_s, sinks, wts)

    ys = _moe_call(gtot, xs, w_gate, w_up, w_down, n_all)
    goff = jnp.concatenate([goff, gtot[None]], axis=0)
    ln2_g2, ln2_b2 = ln2_g.reshape(1, -1), ln2_b.reshape(1, -1)
    y_p = _combine_call(goff, x1_all, route_all, ys, ln2_g2, ln2_b2, 0, n_prompt, DISPATCH_TILE, n_all)
    y_s = _combine_call(goff, x1_all, route_all, ys, ln2_g2, ln2_b2, n_prompt, n_sample, n_sample, n_all)

    kv_shape = (WINDOW, N_KV, HEAD_DIM)
    return (y_p.reshape(B, S, D_MODEL), y_s.reshape(n_sample, 1, D_MODEL),
            k_win_p.reshape((B,) + kv_shape), v_win_p.reshape((B,) + kv_shape), conv_p, h_p.reshape(B, LRU_WIDTH),
            k_win_s.reshape((n_sample,) + kv_shape), v_win_s.reshape((n_sample,) + kv_shape),
            jnp.transpose(conv_s_t, (1, 0, 2)), h_s)
```

```python
import functools

import jax
import jax.numpy as jnp
from jax import lax
from jax.experimental import pallas as pl
from jax.experimental.pallas import tpu as pltpu

F32 = jnp.float32
BF16 = jnp.bfloat16

D_MODEL = 1024
LRU_WIDTH = 1024
LRU_BLOCK = 64
CONV_W = 4
LRU_C = 8.0
N_HEADS = 16
N_KV = 4
GROUP = N_HEADS // N_KV
HEAD_DIM = 64
KV_WIDTH = N_KV * HEAD_DIM
WINDOW = 128
NEG_INF = -1e30
N_GROUPS = 4
EXPERTS_PER_GROUP = 8
N_EXPERTS = N_GROUPS * EXPERTS_PER_GROUP
TOP_K = 2
EXPERT_FF = D_MODEL // 2
DEPTH = 1
ALPHA = (2 * DEPTH) ** 0.25
LN_EPS = 1e-5
ATTN_SCALE = HEAD_DIM ** -0.5
LOG2_E = 1.4426950408889634

OFF_XL = 0
OFF_YL = OFF_XL + LRU_WIDTH
OFF_Q = OFF_YL + LRU_WIDTH
OFF_K = OFF_Q + N_HEADS * HEAD_DIM
OFF_V = OFF_K + KV_WIDTH
OFF_GL = OFF_V + KV_WIDTH
OFF_GA = OFF_GL + D_MODEL
IN_WIDTH = OFF_GA + D_MODEL

LANES = 128
SUBLANES = 8
MXU_DIM = 256
VMEM_LIMIT_BYTES = 56 * 1024 * 1024

GATE_CHUNK = MXU_DIM
N_GATE_CHUNKS = LRU_WIDTH // GATE_CHUNK
ROUTE_WIDTH = LANES

SEQ_TILE = 256
MOE_TILE = 1392
DISPATCH_TILE = 256
SLOT_CAP = 32
N_SLOTS = N_EXPERTS * SLOT_CAP
PAD_CHUNK = 128
SEG_PAD = MOE_TILE + SLOT_CAP + PAD_CHUNK
PACKED_WIDTH = D_MODEL // 2
SAMPLE_ATTN_TILE = 16
SAMPLE_PROJ_TILE = 512


def _const_spec(shape):
    nd = len(shape)
    return pl.BlockSpec(shape, lambda *_: (0,) * nd)


def _layer_norm(z, g, b):
    mu = jnp.mean(z, axis=-1, keepdims=True)
    zc = z - mu
    var = jnp.mean(zc * zc, axis=-1, keepdims=True)
    return zc * lax.rsqrt(var + LN_EPS) * g + b


def _sigmoid(x):
    return 1.0 / (1.0 + jnp.exp2(x * -LOG2_E))


def _softplus(x):
    return jnp.maximum(x, 0.0) + jnp.log1p(jnp.exp(-jnp.abs(x)))


def _lru_gates(xc, w_gates_ref, b_rg, b_ig, lam):
    xcb = xc.astype(BF16)
    r_parts, i_parts = [], []
    for c in range(N_GATE_CHUNKS):
        g = jnp.dot(xcb[:, c * GATE_CHUNK:(c + 1) * GATE_CHUNK], w_gates_ref[c], preferred_element_type=F32)
        r_parts.append(g[:, :GATE_CHUNK])
        i_parts.append(g[:, GATE_CHUNK:])
    r = _sigmoid(jnp.concatenate(r_parts, axis=1) + b_rg)
    i = _sigmoid(jnp.concatenate(i_parts, axis=1) + b_ig)
    log_a = (-LRU_C * r) * _softplus(-lam)
    a = jnp.exp(log_a)
    gain_sq = 1.0 - a * a
    gain = jnp.where(gain_sq > 0.0, gain_sq * lax.rsqrt(gain_sq), 0.0)
    u = gain * (i * xc)
    return a, u


def _linear_scan(a, u, h_in):
    n, w = a.shape
    groups = n // SUBLANES
    a3 = a.reshape(groups, SUBLANES, w)
    u3 = u.reshape(groups, SUBLANES, w)
    row = lax.broadcasted_iota(jnp.int32, a3.shape, 1)
    d = 1
    while d < SUBLANES:
        has_prev = row >= d
        u3 = u3 + a3 * jnp.where(has_prev, pltpu.roll(u3, d, axis=1), 0.0)
        a3 = a3 * jnp.where(has_prev, pltpu.roll(a3, d, axis=1), 1.0)
        d *= 2
    carry = h_in
    out = []
    for g in range(groups):
        h_g = u3[g] + a3[g] * carry
        out.append(h_g)
        carry = h_g[SUBLANES - 1:SUBLANES, :]
    return jnp.concatenate(out, axis=0)


def _route(x1, w_hi_ref, w_hilo_ref, b_rt):
    x_hi = x1.astype(BF16)
    x_lo = (x1 - x_hi.astype(F32)).astype(BF16)
    both = jnp.dot(x_hi, w_hilo_ref[...], preferred_element_type=F32)
    logits = (both[:, :ROUTE_WIDTH]
              + (jnp.dot(x_lo, w_hi_ref[...], preferred_element_type=F32) + both[:, ROUTE_WIDTH:])) + b_rt
    col = lax.broadcasted_iota(jnp.int32, logits.shape, 1)
    big = jnp.int32(ROUTE_WIDTH)
    is_g = col < N_GROUPS
    gl = jnp.where(is_g, logits, -jnp.inf)
    gmax = jnp.max(gl, axis=-1, keepdims=True)
    g_idx = jnp.min(jnp.where(gl == gmax, col, big), axis=-1, keepdims=True)
    p_g = 1.0 / jnp.sum(jnp.where(is_g, jnp.exp(gl - gmax), 0.0), axis=-1, keepdims=True)
    lo = N_GROUPS + g_idx * EXPERTS_PER_GROUP
    in_grp = (col >= lo) & (col < lo + EXPERTS_PER_GROUP)
    el = jnp.where(in_grp, logits, -jnp.inf)
    v1 = jnp.max(el, axis=-1, keepdims=True)
    i1 = jnp.min(jnp.where(el == v1, col, big), axis=-1, keepdims=True)
    el2 = jnp.where(col == i1, -jnp.inf, el)
    v2 = jnp.max(el2, axis=-1, keepdims=True)
    i2 = jnp.min(jnp.where(el2 == v2, col, big), axis=-1, keepdims=True)
    e21 = jnp.exp(v2 - v1)
    inv = 1.0 / (1.0 + e21)
    w1 = p_g * inv
    w2 = p_g * (e21 * inv)
    e1 = (i1 - N_GROUPS).astype(F32)
    e2 = (i2 - N_GROUPS).astype(F32)
    return jnp.where(col == 0, e1, jnp.where(col == 1, e2, jnp.where(col == 2, w1, jnp.where(col == 3, w2, 0.0))))


def _merge_norm(x, rec, att, g_l, g_a, w_out_ref, ln_g, ln_b):
    rec_o = jnp.dot(rec.astype(BF16), w_out_ref[0], preferred_element_type=F32)
    att_o = jnp.dot(att.astype(BF16), w_out_ref[1], preferred_element_type=F32)
    merged = _sigmoid(g_l) * rec_o + _sigmoid(g_a) * att_o
    mix = jnp.dot(merged.astype(BF16), w_out_ref[2], preferred_element_type=F32)
    return _layer_norm(ALPHA * x + mix, ln_g, ln_b)


def _mixer_kernel(sinks_ref, x_ref, w_in_ref, b_in_ref, w_conv_ref, b_conv_ref, w_gates_ref, b_rg_ref, b_ig_ref,
                  lam_ref, w_out_ref, ln_g_ref, ln_b_ref, w_rt_hi_ref, w_rt_lo_ref,
                  b_rt_ref, x1_s_ref,
                  x1_ref, route_ref, kwin_ref, vwin_ref, conv_ref, h_ref, xs_hbm, goff_ref, gtot_ref,
                  conv_buf, h_carry, kcat, vcat, att_buf, prev_x1, stage, zero_rows, tri, xb_keep, key_keep,
                  g_vmem, gvec, g_smem, nd_smem, rounds_smem, sem_stage, sem_g, sem_zero,
                  *, tiles_per_seq, n_tiles, seg_cap):
    step = pl.program_id(0)
    last_step = pl.num_programs(0) - 1
    buf = lax.rem(step, 2)
    n_sample = x1_s_ref.shape[0]
    disp = _Dispatcher(xs_hbm, goff_ref, gtot_ref, stage, zero_rows, tri, xb_keep, key_keep, g_vmem, gvec,
                       g_smem, nd_smem, rounds_smem, sem_stage, sem_g, sem_zero, seg_cap)
    n_valid = jnp.where(step == 0, 0, jnp.where(step == last_step, n_sample, SEQ_TILE))

    @pl.when(step == 0)
    def _():
        disp.init()
        prev_x1[...] = jnp.zeros_like(prev_x1)

    def route_and_plan():
        x1_prev = prev_x1[...]
        route = _route(x1_prev, w_rt_hi_ref, w_rt_lo_ref, b_rt_ref[...])
        route_ref[...] = route
        disp.plan(x1_prev, route, n_valid, buf)

    @pl.when(step < n_tiles)
    def _():
        @pl.when(lax.rem(step, tiles_per_seq) == 0)
        def _():
            conv_buf[...] = jnp.zeros_like(conv_buf)
            h_carry[...] = jnp.zeros_like(h_carry)
            kcat[0:WINDOW, :] = jnp.zeros((WINDOW, KV_WIDTH), BF16)
            vcat[0:WINDOW, :] = jnp.zeros((WINDOW, KV_WIDTH), BF16)

        route_and_plan()
        _mixer_tile(lax.rem(step, tiles_per_seq), sinks_ref, x_ref, w_in_ref, b_in_ref, w_conv_ref, b_conv_ref,
                    w_gates_ref, b_rg_ref, b_ig_ref, lam_ref, w_out_ref, ln_g_ref,
                    ln_b_ref, x1_ref, prev_x1, kwin_ref, vwin_ref, conv_ref, h_ref,
                    conv_buf, h_carry, kcat, vcat, att_buf)

    @pl.when(step >= n_tiles)
    def _():
        route_and_plan()

        @pl.when(step == n_tiles)
        def _():
            x1_ref[0:n_sample, :] = x1_s_ref[...]
            prev_x1[0:n_sample, :] = x1_s_ref[...]

    disp.flush(buf, step == last_step)


def _mixer_tile(t, sinks_ref, x_ref, w_in_ref, b_in_ref, w_conv_ref, b_conv_ref, w_gates_ref, b_rg_ref, b_ig_ref,
                lam_ref, w_out_ref, ln_g_ref, ln_b_ref,
                x1_ref, x1_keep, kwin_ref, vwin_ref, conv_ref, h_ref,
                conv_buf, h_carry, kcat, vcat, att_buf):
    T = SEQ_TILE
    x = x_ref[0]
    xb = x.astype(BF16)

    def proj(lo, width):
        return jnp.dot(xb, w_in_ref[:, lo:lo + width], preferred_element_type=F32) + b_in_ref[:, lo:lo + width]

    xl = proj(OFF_XL, LRU_WIDTH)
    xl_ext = jnp.concatenate([conv_buf[...], xl], axis=0)

    def lagged(k):
        return pltpu.roll(xl_ext, k, axis=0)[SUBLANES:, :]
    wc = w_conv_ref[...]
    xc = wc[0:1] * lagged(3)
    xc = xc + wc[1:2] * lagged(2)
    xc = xc + wc[2:3] * lagged(1)
    xc = xc + wc[3:4] * xl + b_conv_ref[...]
    conv_ref[0] = xl[T - (CONV_W - 1):, :]
    conv_buf[...] = xl[T - SUBLANES:, :]

    a, u = _lru_gates(xc, w_gates_ref, b_rg_ref[...], b_ig_ref[...], lam_ref[...])
    h = _linear_scan(a, u, h_carry[0:1, :])
    h_last = h[T - 1:T, :]
    h_carry[0:1, :] = h_last
    h_ref[0] = h_last
    rec = h * jax.nn.gelu(proj(OFF_YL, LRU_WIDTH))

    q = proj(OFF_Q, N_HEADS * HEAD_DIM) * (ATTN_SCALE * LOG2_E)
    k = proj(OFF_K, KV_WIDTH)
    v = proj(OFF_V, KV_WIDTH)
    kwin_ref[0] = k[T - WINDOW:, :]
    vwin_ref[0] = v[T - WINDOW:, :]
    kcat[WINDOW:WINDOW + T, :] = k.astype(BF16)
    vcat[WINDOW:WINDOW + T, :] = v.astype(BF16)

    qi = lax.broadcasted_iota(jnp.int32, (WINDOW, 2 * WINDOW), 0)
    kj = lax.broadcasted_iota(jnp.int32, (WINDOW, 2 * WINDOW), 1)
    band = (kj > qi) & (kj <= qi + WINDOW)
    grp_row = lax.broadcasted_iota(jnp.int32, (GROUP * WINDOW, 1), 0) // WINDOW
    for qb in range(T // WINDOW):
        if qb == 0:
            first_key = jnp.where(t == 0, WINDOW, 0)
            mask1 = band & (kj >= first_key)
        else:
            mask1 = band
        bias = jnp.concatenate([jnp.where(mask1, 0.0, NEG_INF)] * GROUP, axis=0)
        r0 = qb * WINDOW
        qq = q[r0:r0 + WINDOW, :]
        for j in range(N_KV):
            kjb = kcat[r0:r0 + 2 * WINDOW, j * HEAD_DIM:(j + 1) * HEAD_DIM]
            vjb = vcat[r0:r0 + 2 * WINDOW, j * HEAD_DIM:(j + 1) * HEAD_DIM]
            qs = jnp.concatenate(
                [qq[:, (j * GROUP + g) * HEAD_DIM:(j * GROUP + g + 1) * HEAD_DIM] for g in range(GROUP)], axis=0)
            s = lax.dot_general(qs.astype(BF16), kjb, (((1,), (1,)), ((), ())), preferred_element_type=F32) + bias
            sink = jnp.zeros((GROUP * WINDOW, 1), F32)
            for g in range(GROUP):
                sink = jnp.where(grp_row == g, sinks_ref[j * GROUP + g] * LOG2_E, sink)
            m = jnp.maximum(jnp.max(s, axis=-1, keepdims=True), sink)
            p = jnp.exp2(s - m)
            inv = 1.0 / (jnp.sum(p, axis=-1, keepdims=True) + jnp.exp2(sink - m))
            o = jnp.dot((p * inv).astype(BF16), vjb, preferred_element_type=F32)
            for g in range(GROUP):
                hcol = (j * GROUP + g) * HEAD_DIM
                att_buf[r0:r0 + WINDOW, hcol:hcol + HEAD_DIM] = o[g * WINDOW:(g + 1) * WINDOW, :]
    kcat[0:WINDOW, :] = kcat[T:T + WINDOW, :]
    vcat[0:WINDOW, :] = vcat[T:T + WINDOW, :]

    x1 = _merge_norm(x, rec, att_buf[...], proj(OFF_GL, D_MODEL), proj(OFF_GA, D_MODEL),
                     w_out_ref, ln_g_ref[...], ln_b_ref[...])
    x1_ref[...] = x1
    x1_keep[...] = x1


def _mixer_call(x_prompt, x1_s, sinks, wts):
    B, S, _ = x_prompt.shape
    T = SEQ_TILE
    assert T == DISPATCH_TILE and x1_s.shape[0] <= T
    nt = S // T
    n_tiles = B * nt
    n_rows_total = B * S + x1_s.shape[0]
    n_dispatch = _dispatch_tiles(n_rows_total)
    assert n_dispatch == n_tiles + 1
    seg_cap = _segment_capacity(n_rows_total)
    n_assign = TOP_K * T
    weight_args = (wts["w_in"], wts["b_in"], wts["w_conv"], wts["b_conv"], wts["w_gates"], wts["b_rg"], wts["b_ig"],
                   wts["lam"], wts["w_out"], wts["ln1_g"], wts["ln1_b"],
                   wts["w_rt_hi"], wts["w_rt_lo"], wts["b_rt"], x1_s)
    mixed = lambda i: jnp.minimum(i, n_tiles - 1)
    seq = lambda i: mixed(i) // nt
    routed = lambda i: jnp.clip(i - 1, 0, n_dispatch - 1)
    in_specs = [pl.BlockSpec(memory_space=pltpu.SMEM),
                pl.BlockSpec((1, T, D_MODEL), lambda i: (seq(i), lax.rem(mixed(i), nt), 0))]
    in_specs += [_const_spec(w.shape) for w in weight_args]
    out_shape = (
        jax.ShapeDtypeStruct((n_rows_total, D_MODEL), F32),
        jax.ShapeDtypeStruct((n_rows_total, ROUTE_WIDTH), F32),
        jax.ShapeDtypeStruct((B, WINDOW, KV_WIDTH), F32),
        jax.ShapeDtypeStruct((B, WINDOW, KV_WIDTH), F32),
        jax.ShapeDtypeStruct((B, CONV_W - 1, LRU_WIDTH), F32),
        jax.ShapeDtypeStruct((B, 1, LRU_WIDTH), F32),
        jax.ShapeDtypeStruct((N_EXPERTS * seg_cap, PACKED_WIDTH), jnp.uint32),
        jax.ShapeDtypeStruct((n_dispatch, 1, ROUTE_WIDTH), jnp.int32),
        jax.ShapeDtypeStruct((1, ROUTE_WIDTH), jnp.int32),
    )
    out_specs = (
        pl.BlockSpec((T, D_MODEL), lambda i: (jnp.minimum(i, n_tiles), 0)),
        pl.BlockSpec((T, ROUTE_WIDTH), lambda i: (routed(i), 0)),
        pl.BlockSpec((1, WINDOW, KV_WIDTH), lambda i: (seq(i), 0, 0)),
        pl.BlockSpec((1, WINDOW, KV_WIDTH), lambda i: (seq(i), 0, 0)),
        pl.BlockSpec((1, CONV_W - 1, LRU_WIDTH), lambda i: (seq(i), 0, 0)),
        pl.BlockSpec((1, 1, LRU_WIDTH), lambda i: (seq(i), 0, 0)),
        pl.BlockSpec(memory_space=pl.ANY),
        pl.BlockSpec((1, 1, ROUTE_WIDTH), lambda i: (routed(i), 0, 0)),
        pl.BlockSpec((1, ROUTE_WIDTH), lambda i: (0, 0)),
    )
    scratch = [
        pltpu.VMEM((SUBLANES, LRU_WIDTH), F32),
        pltpu.VMEM((SUBLANES, LRU_WIDTH), F32),
        pltpu.VMEM((T + WINDOW, KV_WIDTH), BF16),
        pltpu.VMEM((T + WINDOW, KV_WIDTH), BF16),
        pltpu.VMEM((T, N_HEADS * HEAD_DIM), F32),
        pltpu.VMEM((T, D_MODEL), F32),
        pltpu.VMEM((2, N_SLOTS, PACKED_WIDTH), jnp.uint32),
        pltpu.VMEM((PAD_CHUNK, PACKED_WIDTH), jnp.uint32),
        pltpu.VMEM((n_assign, n_assign), BF16),
        pltpu.VMEM((T, D_MODEL), BF16),
        pltpu.VMEM((SUBLANES, n_assign), F32),
        pltpu.VMEM((SUBLANES, ROUTE_WIDTH), jnp.int32),
        pltpu.VMEM((SUBLANES, ROUTE_WIDTH), F32),
        pltpu.SMEM((1, ROUTE_WIDTH), jnp.int32),
        pltpu.SMEM((1,), jnp.int32),
        pltpu.SMEM((1,), jnp.int32),
        pltpu.SemaphoreType.DMA(()),
        pltpu.SemaphoreType.DMA(()),
        pltpu.SemaphoreType.DMA(()),
    ]
    return pl.pallas_call(
        functools.partial(_mixer_kernel, tiles_per_seq=nt, n_tiles=n_tiles, seg_cap=seg_cap),
        grid=(n_tiles + 2,),
        in_specs=in_specs,
        out_specs=out_specs,
        out_shape=out_shape,
        scratch_shapes=scratch,
        compiler_params=pltpu.CompilerParams(dimension_semantics=("arbitrary",),
                                             vmem_limit_bytes=VMEM_LIMIT_BYTES),
        name="mixer_prompt",
    )(sinks, x_prompt, *weight_args)


def _sample_proj_kernel(x_ref, w_ref, b_ref, u_ref):
    u_ref[...] = jnp.dot(x_ref[...].astype(BF16), w_ref[...], preferred_element_type=F32) + b_ref[...]


def _sample_proj_call(x_s, w_in, b_in):
    n = x_s.shape[0]
    tn = SAMPLE_PROJ_TILE
    return pl.pallas_call(
        _sample_proj_kernel,
        grid=(IN_WIDTH // tn,),
        in_specs=[pl.BlockSpec((n, D_MODEL), lambda c: (0, 0)),
                  pl.BlockSpec((D_MODEL, tn), lambda c: (0, c)),
                  pl.BlockSpec((1, tn), lambda c: (0, c))],
        out_specs=pl.BlockSpec((n, tn), lambda c: (0, c)),
        out_shape=jax.ShapeDtypeStruct((n, IN_WIDTH), F32),
        compiler_params=pltpu.CompilerParams(dimension_semantics=("arbitrary",)),
        name="sample_proj",
    )(x_s, w_in, b_in)


def _sample_attn_kernel(q_ref, kn_ref, vn_ref, kn_row_ref, vn_row_ref, ck_ref, cv_ref, sinks_ref,
                        att_ref, kwin_ref, vwin_ref):
    tb = q_ref.shape[0]
    key_pos = lax.broadcasted_iota(jnp.int32, (tb, GROUP, WINDOW), 2)
    for j in range(N_KV):
        qj = q_ref[:, j * GROUP:(j + 1) * GROUP, :]
        kc = ck_ref[:, :, j * HEAD_DIM:(j + 1) * HEAD_DIM]
        vc = cv_ref[:, :, j * HEAD_DIM:(j + 1) * HEAD_DIM]
        s_c = jnp.einsum("bgd,bsd->bgs", qj.astype(BF16), kc.astype(BF16), preferred_element_type=F32) * ATTN_SCALE
        s_c = jnp.where(key_pos >= 1, s_c, NEG_INF)
        kn = kn_ref[:, j:j + 1, :]
        vn = vn_ref[:, j:j + 1, :]
        s_n = jnp.sum(qj * kn, axis=-1, keepdims=True) * ATTN_SCALE
        sink = sinks_ref[j][None]
        m = jnp.maximum(jnp.maximum(jnp.max(s_c, axis=-1, keepdims=True), s_n), sink)
        p_c = jnp.exp(s_c - m)
        p_n = jnp.exp(s_n - m)
        inv = 1.0 / (jnp.sum(p_c, axis=-1, keepdims=True) + p_n + jnp.exp(sink - m))
        o = jnp.einsum("bgs,bsd->bgd", (p_c * inv).astype(BF16), vc.astype(BF16), preferred_element_type=F32)
        att_ref[:, j * GROUP:(j + 1) * GROUP, :] = o + (p_n * inv) * vn
    kwin_ref[:, 0:WINDOW - 1, :] = ck_ref[:, 1:WINDOW, :]
    kwin_ref[:, WINDOW - 1:WINDOW, :] = kn_row_ref[...]
    vwin_ref[:, 0:WINDOW - 1, :] = cv_ref[:, 1:WINDOW, :]
    vwin_ref[:, WINDOW - 1:WINDOW, :] = vn_row_ref[...]


def _sample_attn_call(q3, kn3, vn3, kn_row, vn_row, ck, cv, sinks3):
    n = q3.shape[0]
    tb = SAMPLE_ATTN_TILE
    b3 = lambda i: (i, 0, 0)
    return pl.pallas_call(
        _sample_attn_kernel,
        grid=(n // tb,),
        in_specs=[pl.BlockSpec((tb, N_HEADS, HEAD_DIM), b3),
                  pl.BlockSpec((tb, N_KV, HEAD_DIM), b3),
                  pl.BlockSpec((tb, N_KV, HEAD_DIM), b3),
                  pl.BlockSpec((tb, 1, KV_WIDTH), b3),
                  pl.BlockSpec((tb, 1, KV_WIDTH), b3),
                  pl.BlockSpec((tb, WINDOW, KV_WIDTH), b3),
                  pl.BlockSpec((tb, WINDOW, KV_WIDTH), b3),
                  pl.BlockSpec((N_KV, GROUP, 1), lambda i: (0, 0, 0))],
        out_specs=(pl.BlockSpec((tb, N_HEADS, HEAD_DIM), b3),
                   pl.BlockSpec((tb, WINDOW, KV_WIDTH), b3),
                   pl.BlockSpec((tb, WINDOW, KV_WIDTH), b3)),
        out_shape=(jax.ShapeDtypeStruct((n, N_HEADS, HEAD_DIM), F32),
                   jax.ShapeDtypeStruct((n, WINDOW, KV_WIDTH), F32),
                   jax.ShapeDtypeStruct((n, WINDOW, KV_WIDTH), F32)),
        compiler_params=pltpu.CompilerParams(dimension_semantics=("arbitrary",)),
        name="sample_attn",
    )(q3, kn3, vn3, kn_row, vn_row, ck, cv, sinks3)


def _sample_mix_kernel(x_ref, u_ref, att_ref, st_ref, h0_ref, w_conv_ref, b_conv_ref, w_gates_ref, b_rg_ref,
                       b_ig_ref, lam_ref, w_out_ref, ln_g_ref, ln_b_ref,
                       x1_ref, conv_ref, h_ref):
    xl = u_ref[:, OFF_XL:OFF_XL + LRU_WIDTH]
    wc = w_conv_ref[...]
    xc = wc[0:1] * st_ref[0]
    xc = xc + wc[1:2] * st_ref[1]
    xc = xc + wc[2:3] * st_ref[2]
    xc = xc + wc[3:4] * xl + b_conv_ref[...]
    conv_ref[0] = st_ref[1]
    conv_ref[1] = st_ref[2]
    conv_ref[2] = xl
    a, u = _lru_gates(xc, w_gates_ref, b_rg_ref[...], b_ig_ref[...], lam_ref[...])
    h = a * h0_ref[...] + u
    h_ref[...] = h
    rec = h * jax.nn.gelu(u_ref[:, OFF_YL:OFF_YL + LRU_WIDTH])
    x1_ref[...] = _merge_norm(x_ref[...], rec, att_ref[...], u_ref[:, OFF_GL:OFF_GL + D_MODEL],
                              u_ref[:, OFF_GA:OFF_GA + D_MODEL], w_out_ref,
                              ln_g_ref[...], ln_b_ref[...])


def _sample_mix_call(x_s, u_s, att, st_t, h0, wts):
    n = x_s.shape[0]
    weight_args = (wts["w_conv"], wts["b_conv"], wts["w_gates"], wts["b_rg"], wts["b_ig"], wts["lam"],
                   wts["w_out"], wts["ln1_g"], wts["ln1_b"])
    args = (x_s, u_s, att, st_t, h0) + weight_args
    out_shapes = ((n, D_MODEL), (CONV_W - 1, n, LRU_WIDTH), (n, LRU_WIDTH))
    return pl.pallas_call(
        _sample_mix_kernel,
        grid=(1,),
        in_specs=[_const_spec(a.shape) for a in args],
        out_specs=tuple(_const_spec(s) for s in out_shapes),
        out_shape=tuple(jax.ShapeDtypeStruct(s, F32) for s in out_shapes),
        compiler_params=pltpu.CompilerParams(dimension_semantics=("arbitrary",),
                                             vmem_limit_bytes=VMEM_LIMIT_BYTES),
        name="sample_mix",
    )(*args)


def _one_hot(mask):
    return jnp.where(mask, 1.0, 0.0).astype(BF16)


def _pack_rows(x):
    half = x.shape[1] // 2
    lo = lax.shift_right_logical(lax.bitcast_convert_type(x[:, :half], jnp.uint32), jnp.uint32(16))
    hi = lax.bitcast_convert_type(x[:, half:], jnp.uint32) & jnp.uint32(0xFFFF0000)
    return lo | hi


def _unpack_rows(words):
    lo = lax.bitcast_convert_type(lax.shift_left(words, jnp.uint32(16)), F32)
    hi = lax.bitcast_convert_type(words & jnp.uint32(0xFFFF0000), F32)
    return jnp.concatenate([lo.astype(BF16), hi.astype(BF16)], axis=1)


class _Dispatcher:
    def __init__(self, xs_hbm, goff_ref, gtot_ref, stage, zero_rows, tri, xb_keep, key_keep, g_vmem, gvec,
                 g_smem, nd_smem, rounds_smem, sem_stage, sem_g, sem_zero, seg_cap):
        self.xs_hbm, self.goff_ref, self.gtot_ref = xs_hbm, goff_ref, gtot_ref
        self.stage, self.zero_rows, self.tri = stage, zero_rows, tri
        self.xb_keep, self.key_keep = xb_keep, key_keep
        self.g_vmem, self.gvec, self.g_smem = g_vmem, gvec, g_smem
        self.nd_smem, self.rounds_smem = nd_smem, rounds_smem
        self.sem_stage, self.sem_g, self.sem_zero = sem_stage, sem_g, sem_zero
        self.seg_cap = seg_cap

    def _g_copy(self):
        return pltpu.make_async_copy(self.g_vmem.at[pl.ds(0, 1)], self.g_smem, self.sem_g)

    def init(self):
        n_assign = self.tri.shape[0]
        r = lax.broadcasted_iota(jnp.int32, (n_assign, n_assign), 0)
        c = lax.broadcasted_iota(jnp.int32, (n_assign, n_assign), 1)
        self.tri[...] = _one_hot(r < c)
        self.gvec[...] = jnp.zeros_like(self.gvec)
        self.g_vmem[...] = jnp.zeros_like(self.g_vmem)
        self.nd_smem[0] = 0
        self._g_copy().start()

    def _sorted_rows(self, e_row, rank_row, xb, lo):
        td = xb.shape[0]
        slot = lax.broadcasted_iota(jnp.int32, (N_SLOTS, td), 0).astype(F32)
        lo_f = lo.astype(F32)
        in_round = (rank_row >= lo_f) & (rank_row < lo_f + SLOT_CAP) & (e_row >= 0.0)
        key = jnp.where(in_round, e_row * SLOT_CAP + (rank_row - lo_f), -1.0)
        perm = jnp.where(slot == key[:, :td], 1.0, jnp.where(slot == key[:, td:], 1.0, 0.0)).astype(BF16)
        return _pack_rows(jnp.dot(perm, xb, preferred_element_type=F32))

    def plan(self, x1, route, n_valid, buf):
        td = x1.shape[0]
        n_assign = TOP_K * td
        valid_col = lax.broadcasted_iota(jnp.int32, (td, 1), 0) < n_valid
        lane = lax.broadcasted_iota(jnp.int32, (td, ROUTE_WIDTH), 1).astype(F32)
        e0c = jnp.where(valid_col, route[:, 0:1], -1.0)
        e1c = jnp.where(valid_col, route[:, 1:2], -1.0)
        cnt_row = jnp.sum(jnp.where(lane == e0c, 1.0, 0.0) + jnp.where(lane == e1c, 1.0, 0.0),
                          axis=0, keepdims=True)
        self.rounds_smem[0] = (jnp.max(cnt_row).astype(jnp.int32) + SLOT_CAP - 1) // SLOT_CAP
        self.goff_ref[0] = self.gvec[0:1, :].astype(jnp.int32)
        self.gvec[0:1, :] = self.gvec[0:1, :] + jnp.ceil(cnt_row * (1.0 / SUBLANES)) * SUBLANES
        self.gtot_ref[...] = self.gvec[0:1, :].astype(jnp.int32)

        route_t = route.T
        valid_row = lax.broadcasted_iota(jnp.int32, (1, td), 1) < n_valid
        e_row = jnp.concatenate([jnp.where(valid_row, route_t[0:1, :], -1.0),
                                 jnp.where(valid_row, route_t[1:2, :], -1.0)], axis=1)
        expert_sub = lax.broadcasted_iota(jnp.int32, (N_EXPERTS, n_assign), 0).astype(F32)
        onehot_t = expert_sub == e_row
        rank_t = jnp.dot(_one_hot(onehot_t), self.tri[...], preferred_element_type=F32)
        rank_row = jnp.sum(jnp.where(onehot_t, rank_t, 0.0), axis=0, keepdims=True)
        xb = jnp.where(valid_col, x1, 0.0).astype(BF16)
        self.xb_keep[...] = xb
        self.key_keep[0:1, :] = e_row
        self.key_keep[1:2, :] = rank_row
        self.stage[buf] = self._sorted_rows(e_row, rank_row, xb, jnp.int32(0))

    def _stage_copy(self, buf, e, dst_row):
        src = self.stage.at[buf, pl.ds(pl.multiple_of(e * SLOT_CAP, SLOT_CAP), SLOT_CAP)]
        return pltpu.make_async_copy(src, self.xs_hbm.at[pl.ds(dst_row, SLOT_CAP)], self.sem_stage)

    def _wait_outstanding(self):
        def wait_one(_, carry):
            self._stage_copy(0, 0, 0).wait()
            return carry
        lax.fori_loop(0, self.nd_smem[0], wait_one, 0)
        self.nd_smem[0] = 0

    def _start_round(self, buf, lo):
        def issue(e, carry):
            dst_row = pl.multiple_of(e * self.seg_cap + self.g_smem[0, e] + lo, SUBLANES)
            self._stage_copy(buf, e, dst_row).start()
            return carry
        lax.fori_loop(0, N_EXPERTS, issue, 0)
        self.nd_smem[0] = N_EXPERTS

    def flush(self, buf, is_last):
        self._g_copy().wait()
        self._wait_outstanding()
        self._start_round(buf, 0)

        def later_round(r, carry):
            lo = r * SLOT_CAP
            rows = self._sorted_rows(self.key_keep[0:1, :], self.key_keep[1:2, :], self.xb_keep[...], lo)
            self._wait_outstanding()
            self.stage[buf] = rows
            self._start_round(buf, lo)
            return carry
        lax.fori_loop(1, self.rounds_smem[0], later_round, 0)

        self.g_vmem[0:1, :] = self.gvec[0:1, :].astype(jnp.int32)
        self._g_copy().start()

        @pl.when(is_last)
        def _():
            self._g_copy().wait()
            self._wait_outstanding()
            self.zero_rows[...] = jnp.zeros_like(self.zero_rows)

            def pad_copy(dst_row):
                dst = self.xs_hbm.at[pl.ds(pl.multiple_of(dst_row, SUBLANES), PAD_CHUNK)]
                return pltpu.make_async_copy(self.zero_rows, dst, self.sem_zero)

            def pad_segment(e, n_started):
                fill = self.g_smem[0, e]
                block_end = (fill + SLOT_CAP + MOE_TILE - 1) // MOE_TILE * MOE_TILE
                n_chunks = (block_end - fill + PAD_CHUNK - 1) // PAD_CHUNK

                def start(c, carry):
                    pad_copy(e * self.seg_cap + fill + c * PAD_CHUNK).start()
                    return carry
                lax.fori_loop(0, n_chunks, start, 0)
                return n_started + n_chunks
            n_started = lax.fori_loop(0, N_EXPERTS, pad_segment, 0)

            def wait(_, carry):
                pad_copy(0).wait()
                return carry
            lax.fori_loop(0, n_started, wait, 0)


def _dispatch_tiles(n_tokens):
    return (n_tokens + DISPATCH_TILE - 1) // DISPATCH_TILE


def _segment_capacity(n_tokens):
    assert SEG_PAD >= DISPATCH_TILE
    alignment_slack = (SUBLANES - 1) * _dispatch_tiles(n_tokens)
    return (n_tokens + alignment_slack + SEG_PAD + MOE_TILE - 1) // MOE_TILE * MOE_TILE


def _moe_kernel(blk_e_ref, blk_j_ref, n_used_ref, xs_ref, wg_ref, wu_ref, wd_ref, ys_ref, wg_bf, wu_bf, wd_bf):
    del blk_j_ref
    b = pl.program_id(0)

    @pl.when(b < n_used_ref[0])
    def _():
        prev_e = blk_e_ref[jnp.maximum(b - 1, 0)]

        @pl.when((b == 0) | (blk_e_ref[b] != prev_e))
        def _():
            wg_bf[...] = wg_ref[0].astype(BF16)
            wu_bf[...] = wu_ref[0].astype(BF16)
            wd_bf[...] = wd_ref[0].astype(BF16)

        xb = _unpack_rows(xs_ref[...])
        g = jnp.dot(xb, wg_bf[...], preferred_element_type=F32)
        u = jnp.dot(xb, wu_bf[...], preferred_element_type=F32)
        hmid = (g * _sigmoid(g)) * u
        y = jnp.dot(hmid.astype(BF16), wd_bf[...], preferred_element_type=F32)
        ys_ref[...] = _pack_rows(y.astype(BF16).astype(F32))


def _expert_blocks(gtot, n_blocks):
    rows = gtot[0, :N_EXPERTS]
    nb = (rows + SLOT_CAP + MOE_TILE - 1) // MOE_TILE
    ends = jnp.cumsum(nb)
    n_used = ends[-1]
    step = jnp.minimum(jnp.arange(n_blocks, dtype=jnp.int32), n_used - 1)
    blk_e = jnp.sum((step[:, None] >= ends[None, :]).astype(jnp.int32), axis=1)
    blk_j = step - (ends - nb)[blk_e]
    return blk_e, blk_j.astype(jnp.int32), n_used.reshape(1).astype(jnp.int32)


def _moe_call(gtot, xs, w_gate, w_up, w_down, n_tokens):
    seg_cap = _segment_capacity(n_tokens)
    seg_blocks = seg_cap // MOE_TILE
    max_rows = n_tokens * TOP_K + N_EXPERTS * (SUBLANES - 1) * _dispatch_tiles(n_tokens)
    n_blocks = (max_rows + N_EXPERTS * (SLOT_CAP + MOE_TILE - 1)) // MOE_TILE
    blk_e, blk_j, n_used = _expert_blocks(gtot, n_blocks)
    row_block = lambda b, be, bj, nu: (be[b] * seg_blocks + bj[b], 0)
    weight_block = lambda b, be, bj, nu: (be[b], 0, 0)
    grid_spec = pltpu.PrefetchScalarGridSpec(
        num_scalar_prefetch=3,
        grid=(n_blocks,),
        in_specs=[
            pl.BlockSpec((MOE_TILE, PACKED_WIDTH), row_block),
            pl.BlockSpec((1, D_MODEL, EXPERT_FF), weight_block),
            pl.BlockSpec((1, D_MODEL, EXPERT_FF), weight_block),
            pl.BlockSpec((1, EXPERT_FF, D_MODEL), weight_block),
        ],
        out_specs=pl.BlockSpec((MOE_TILE, PACKED_WIDTH), row_block),
        scratch_shapes=[
            pltpu.VMEM((D_MODEL, EXPERT_FF), BF16),
            pltpu.VMEM((D_MODEL, EXPERT_FF), BF16),
            pltpu.VMEM((EXPERT_FF, D_MODEL), BF16),
        ],
    )
    return pl.pallas_call(
        _moe_kernel,
        grid_spec=grid_spec,
        out_shape=jax.ShapeDtypeStruct(xs.shape, xs.dtype),
        compiler_params=pltpu.CompilerParams(dimension_semantics=("arbitrary",),
                                             vmem_limit_bytes=VMEM_LIMIT_BYTES),
        name="moe_experts",
    )(blk_e, blk_j, n_used, xs, w_gate, w_up, w_down)


def _combine_kernel(gcur_ref, gnext_ref, x1_ref, route_ref, ys_hbm, ln_g_ref, ln_b_ref, y_ref,
                    stage, acc, tri, sem, *, seg_cap):
    td = x1_ref.shape[0]
    n_assign = TOP_K * td
    i = pl.program_id(0)
    buf = lax.rem(i, 2)

    def stage_copy(g_ref, e, first_rank, to_buf):
        src_row = pl.multiple_of(e * seg_cap + g_ref[0, 0, e] + first_rank, SUBLANES)
        dst = stage.at[to_buf, pl.ds(pl.multiple_of(e * SLOT_CAP, SLOT_CAP), SLOT_CAP)]
        return pltpu.make_async_copy(ys_hbm.at[pl.ds(src_row, SLOT_CAP)], dst, sem.at[to_buf])

    def fetch_first_round(g_ref, to_buf):
        def start(e, carry):
            stage_copy(g_ref, e, 0, to_buf).start()
            return carry
        lax.fori_loop(0, N_EXPERTS, start, 0)

    @pl.when(i == 0)
    def _():
        r = lax.broadcasted_iota(jnp.int32, (n_assign, n_assign), 0)
        c = lax.broadcasted_iota(jnp.int32, (n_assign, n_assign), 1)
        tri[...] = _one_hot(c < r)
        fetch_first_round(gcur_ref, 0)

    def count(e):
        return gnext_ref[0, 0, e] - gcur_ref[0, 0, e]

    n_max = lax.fori_loop(0, N_EXPERTS, lambda e, m: jnp.maximum(m, count(e)), 0)
    rounds = (n_max + SLOT_CAP - 1) // SLOT_CAP

    route = route_ref[...]
    e0, e1, w0, w1 = route[:, 0:1], route[:, 1:2], route[:, 2:3], route[:, 3:4]
    lane = lax.broadcasted_iota(jnp.int32, (td, ROUTE_WIDTH), 1).astype(F32)
    onehot = jnp.concatenate([lane == e0, lane == e1], axis=0)
    onehot_f = jnp.where(onehot, 1.0, 0.0)
    rank_mat = jnp.dot(tri[...], onehot_f.astype(BF16), preferred_element_type=F32)
    rank = jnp.sum(jnp.where(onehot, rank_mat, 0.0), axis=1, keepdims=True)
    r0, r1 = rank[:td], rank[td:]
    cnt_row = jnp.sum(onehot_f, axis=0, keepdims=True)

    slot_col = lax.broadcasted_iota(jnp.int32, (N_SLOTS, 1), 0)
    slot_expert = slot_col // SLOT_CAP
    slot_rank = slot_col - slot_expert * SLOT_CAP
    lane_s = lax.broadcasted_iota(jnp.int32, (N_SLOTS, ROUTE_WIDTH), 1)
    n_col = jnp.sum(jnp.where(lane_s == slot_expert, cnt_row, 0.0), axis=1, keepdims=True)
    slot_lane = lax.broadcasted_iota(jnp.int32, (td, N_SLOTS), 1).astype(F32)

    def wait_copies(n):
        def wait_one(_, c):
            stage_copy(gcur_ref, 0, 0, buf).wait()
            return c
        lax.fori_loop(0, n, wait_one, 0)

    def weighted_rows(lo):
        lo_f = lo.astype(F32)
        live = (slot_rank + lo).astype(F32) < n_col
        rows = _unpack_rows(jnp.where(live, stage[buf], jnp.uint32(0)))

        def selector(e_col, r_col, w_col):
            in_round = (r_col >= lo_f) & (r_col < lo_f + SLOT_CAP)
            key = jnp.where(in_round, e_col * SLOT_CAP + (r_col - lo_f), -1.0)
            return jnp.where(slot_lane == key, w_col, 0.0)
        mix = (selector(e0, r0, w0) + selector(e1, r1, w1)).astype(BF16)
        return jnp.dot(mix, rows, preferred_element_type=F32)

    wait_copies(N_EXPERTS)

    @pl.when(i + 1 < pl.num_programs(0))
    def _():
        fetch_first_round(gnext_ref, 1 - buf)

    acc[...] = weighted_rows(jnp.int32(0))

    def later_round(r, carry):
        lo = r * SLOT_CAP

        def start(e, n_started):
            has_rows = count(e) > lo

            @pl.when(has_rows)
            def _():
                stage_copy(gcur_ref, e, lo, buf).start()
            return n_started + has_rows.astype(jnp.int32)
        wait_copies(lax.fori_loop(0, N_EXPERTS, start, 0))
        acc[...] = acc[...] + weighted_rows(lo)
        return carry
    lax.fori_loop(1, rounds, later_round, 0)

    y_ref[...] = _layer_norm(ALPHA * x1_ref[...] + acc[...], ln_g_ref[...], ln_b_ref[...])


def _combine_call(goff, x1_all, route_all, ys, ln_g, ln_b, first_row, n_rows, tile, n_tokens):
    assert first_row % DISPATCH_TILE == 0 and first_row % tile == 0
    assert tile == DISPATCH_TILE or n_rows == tile
    first_block = first_row // tile
    first_goff = first_row // DISPATCH_TILE
    n_assign = TOP_K * tile
    grid_spec = pl.GridSpec(
        grid=(n_rows // tile,),
        in_specs=[
            pl.BlockSpec((1, 1, ROUTE_WIDTH), lambda i: (first_goff + i, 0, 0), memory_space=pltpu.SMEM),
            pl.BlockSpec((1, 1, ROUTE_WIDTH), lambda i: (first_goff + i + 1, 0, 0), memory_space=pltpu.SMEM),
            pl.BlockSpec((tile, D_MODEL), lambda i: (first_block + i, 0)),
            pl.BlockSpec((tile, ROUTE_WIDTH), lambda i: (first_block + i, 0)),
            pl.BlockSpec(memory_space=pl.ANY),
            _const_spec(ln_g.shape),
            _const_spec(ln_b.shape),
        ],
        out_specs=pl.BlockSpec((tile, D_MODEL), lambda i: (i, 0)),
        scratch_shapes=[pltpu.VMEM((2, N_SLOTS, PACKED_WIDTH), jnp.uint32),
                        pltpu.VMEM((tile, D_MODEL), F32),
                        pltpu.VMEM((n_assign, n_assign), BF16),
                        pltpu.SemaphoreType.DMA((2,))],
    )
    return pl.pallas_call(
        functools.partial(_combine_kernel, seg_cap=_segment_capacity(n_tokens)),
        grid_spec=grid_spec,
        out_shape=jax.ShapeDtypeStruct((n_rows, D_MODEL), F32),
        compiler_params=pltpu.CompilerParams(dimension_semantics=("arbitrary",),
                                             vmem_limit_bytes=VMEM_LIMIT_BYTES),
        name="moe_combine",
    )(goff, goff, x1_all, route_all, ys, ln_g, ln_b)


def _prepare_weights(w_in, b_in, w_conv, b_conv, w_rg, b_rg, w_ig, b_ig, lru_lambda, w_lru_out, w_attn_out, w_o,
                     ln1_g, ln1_b, w_group, b_group, w_router, b_router):
    blocks_per_chunk = GATE_CHUNK // LRU_BLOCK

    def chunked_block_diag(w):
        w = w.reshape(N_GATE_CHUNKS, blocks_per_chunk, LRU_BLOCK, LRU_BLOCK)
        eye = jnp.eye(blocks_per_chunk, dtype=w.dtype)
        return jnp.einsum("cbij,bd->cbidj", w, eye).reshape(N_GATE_CHUNKS, GATE_CHUNK, GATE_CHUNK)

    w_gates = jnp.concatenate([chunked_block_diag(w_rg), chunked_block_diag(w_ig)], axis=-1).astype(BF16)
    w_rt = jnp.concatenate([w_group, w_router], axis=1)
    w_rt = jnp.pad(w_rt, ((0, 0), (0, ROUTE_WIDTH - w_rt.shape[1])))
    w_rt_hi = w_rt.astype(BF16)
    w_rt_lo = jnp.concatenate([w_rt_hi, (w_rt - w_rt_hi.astype(F32)).astype(BF16)], axis=1)
    b_rt = jnp.pad(jnp.concatenate([b_group, b_router]), (0, ROUTE_WIDTH - N_GROUPS - N_EXPERTS))
    row = lambda v: v.reshape(1, -1)
    return dict(
        w_in=w_in.astype(BF16), b_in=row(b_in), w_conv=w_conv, b_conv=row(b_conv), w_gates=w_gates,
        b_rg=row(b_rg), b_ig=row(b_ig), lam=row(lru_lambda),
        w_out=jnp.stack([w_lru_out, w_attn_out, w_o]).astype(BF16),
        ln1_g=row(ln1_g), ln1_b=row(ln1_b), w_rt_hi=w_rt_hi, w_rt_lo=w_rt_lo, b_rt=row(b_rt))


def kernel(x_prompt, x_sample, cache_k, cache_v, state_conv, state_lru_h, w_in, b_in, w_conv, b_conv, w_rg, b_rg,
           w_ig, b_ig, lru_lambda, sinks, w_lru_out, w_attn_out, w_o, ln1_g, ln1_b, w_group, b_group, w_router,
           b_router, w_gate, w_up, w_down, ln2_g, ln2_b):
    B, S, _ = x_prompt.shape
    n_prompt = B * S
    n_sample = x_sample.shape[0]
    n_all = n_prompt + n_sample
    wts = _prepare_weights(w_in, b_in, w_conv, b_conv, w_rg, b_rg, w_ig, b_ig, lru_lambda, w_lru_out, w_attn_out,
                           w_o, ln1_g, ln1_b, w_group, b_group, w_router, b_router)

    x_s = x_sample.reshape(n_sample, D_MODEL)
    u_s = _sample_proj_call(x_s, wts["w_in"], wts["b_in"])
    q3 = u_s[:, OFF_Q:OFF_K].reshape(n_sample, N_HEADS, HEAD_DIM)
    k_new = u_s[:, OFF_K:OFF_V]
    v_new = u_s[:, OFF_V:OFF_GL]
    att3, k_win_s, v_win_s = _sample_attn_call(
        q3, k_new.reshape(n_sample, N_KV, HEAD_DIM), v_new.reshape(n_sample, N_KV, HEAD_DIM),
        k_new.reshape(n_sample, 1, KV_WIDTH), v_new.reshape(n_sample, 1, KV_WIDTH),
        cache_k.reshape(n_sample, WINDOW, KV_WIDTH), cache_v.reshape(n_sample, WINDOW, KV_WIDTH),
        sinks.reshape(N_KV, GROUP, 1))
    x1_s, conv_s_t, h_s = _sample_mix_call(
        x_s, u_s, att3.reshape(n_sample, N_HEADS * HEAD_DIM), jnp.transpose(state_conv, (1, 0, 2)), state_lru_h, wts)

    x1_all, route_all, k_win_p, v_win_p, conv_p, h_p, xs, goff, gtot = _mixer_call(x_prompt, x1_s, sinks, wts)

    ys = _moe_call(gtot, xs, w_gate, w_up, w_down, n_all)
    goff = jnp.concatenate([goff, gtot[None]], axis=0)
    ln2_g2, ln2_b2 = ln2_g.reshape(1, -1), ln2_b.reshape(1, -1)
    y_p = _combine_call(goff, x1_all, route_all, ys, ln2_g2, ln2_b2, 0, n_prompt, DISPATCH_TILE, n_all)
    y_s = _combine_call(goff, x1_all, route_all, ys, ln2_g2, ln2_b2, n_prompt, n_sample, n_sample, n_all)

    kv_shape = (WINDOW, N_KV, HEAD_DIM)
    return (y_p.reshape(B, S, D_MODEL), y_s.reshape(n_sample, 1, D_MODEL),
            k_win_p.reshape((B,) + kv_shape), v_win_p.reshape((B,) + kv_shape), conv_p, h_p.reshape(B, LRU_WIDTH),
            k_win_s.reshape((n_sample,) + kv_shape), v_win_s.reshape((n_sample,) + kv_shape),
            jnp.transpose(conv_s_t, (1, 0, 2)), h_s)
```

```python
import functools

import jax
import jax.numpy as jnp
from jax import lax
from jax.experimental import pallas as pl
from jax.experimental.pallas import tpu as pltpu

F32 = jnp.float32
BF16 = jnp.bfloat16

D_MODEL = 1024
LRU_WIDTH = 1024
LRU_BLOCK = 64
CONV_W = 4
LRU_C = 8.0
N_HEADS = 16
N_KV = 4
GROUP = N_HEADS // N_KV
HEAD_DIM = 64
KV_WIDTH = N_KV * HEAD_DIM
WINDOW = 128
NEG_INF = -1e30
N_GROUPS = 4
EXPERTS_PER_GROUP = 8
N_EXPERTS = N_GROUPS * EXPERTS_PER_GROUP
TOP_K = 2
EXPERT_FF = D_MODEL // 2
DEPTH = 1
ALPHA = (2 * DEPTH) ** 0.25
LN_EPS = 1e-5
ATTN_SCALE = HEAD_DIM ** -0.5
LOG2_E = 1.4426950408889634

OFF_XL = 0
OFF_YL = OFF_XL + LRU_WIDTH
OFF_Q = OFF_YL + LRU_WIDTH
OFF_K = OFF_Q + N_HEADS * HEAD_DIM
OFF_V = OFF_K + KV_WIDTH
OFF_GL = OFF_V + KV_WIDTH
OFF_GA = OFF_GL + D_MODEL
IN_WIDTH = OFF_GA + D_MODEL

LANES = 128
SUBLANES = 8
MXU_DIM = 256
VMEM_LIMIT_BYTES = 56 * 1024 * 1024

GATE_CHUNK = MXU_DIM
N_GATE_CHUNKS = LRU_WIDTH // GATE_CHUNK
ROUTE_WIDTH = LANES

SEQ_TILE = 256
MOE_TILE = 1392
DISPATCH_TILE = 256
SLOT_CAP = 32
N_SLOTS = N_EXPERTS * SLOT_CAP
PAD_CHUNK = 128
SEG_PAD = MOE_TILE + SLOT_CAP + PAD_CHUNK
PACKED_WIDTH = D_MODEL // 2
SAMPLE_ATTN_TILE = 16
SAMPLE_PROJ_TILE = 512


def _const_spec(shape):
    nd = len(shape)
    return pl.BlockSpec(shape, lambda *_: (0,) * nd)


def _layer_norm(z, g, b):
    mu = jnp.mean(z, axis=-1, keepdims=True)
    zc = z - mu
    var = jnp.mean(zc * zc, axis=-1, keepdims=True)
    return zc * lax.rsqrt(var + LN_EPS) * g + b


def _sigmoid(x):
    return 1.0 / (1.0 + jnp.exp2(x * -LOG2_E))


def _softplus(x):
    return jnp.maximum(x, 0.0) + jnp.log1p(jnp.exp(-jnp.abs(x)))


def _lru_gates(xc, w_gates_ref, b_rg, b_ig, lam):
    xcb = xc.astype(BF16)
    r_parts, i_parts = [], []
    for c in range(N_GATE_CHUNKS):
        g = jnp.dot(xcb[:, c * GATE_CHUNK:(c + 1) * GATE_CHUNK], w_gates_ref[c], preferred_element_type=F32)
        r_parts.append(g[:, :GATE_CHUNK])
        i_parts.append(g[:, GATE_CHUNK:])
    r = _sigmoid(jnp.concatenate(r_parts, axis=1) + b_rg)
    i = _sigmoid(jnp.concatenate(i_parts, axis=1) + b_ig)
    log_a = (-LRU_C * r) * _softplus(-lam)
    a = jnp.exp(log_a)
    gain_sq = 1.0 - a * a
    gain = jnp.where(gain_sq > 0.0, gain_sq * lax.rsqrt(gain_sq), 0.0)
    u = gain * (i * xc)
    return a, u


def _linear_scan(a, u, h_in):
    n, w = a.shape
    groups = n // SUBLANES
    a3 = a.reshape(groups, SUBLANES, w)
    u3 = u.reshape(groups, SUBLANES, w)
    row = lax.broadcasted_iota(jnp.int32, a3.shape, 1)
    d = 1
    while d < SUBLANES:
        has_prev = row >= d
        u3 = u3 + a3 * jnp.where(has_prev, pltpu.roll(u3, d, axis=1), 0.0)
        a3 = a3 * jnp.where(has_prev, pltpu.roll(a3, d, axis=1), 1.0)
        d *= 2
    carry = h_in
    out = []
    for g in range(groups):
        h_g = u3[g] + a3[g] * carry
        out.append(h_g)
        carry = h_g[SUBLANES - 1:SUBLANES, :]
    return jnp.concatenate(out, axis=0)


def _route(x1, w_hi_ref, w_hilo_ref, b_rt):
    x_hi = x1.astype(BF16)
    x_lo = (x1 - x_hi.astype(F32)).astype(BF16)
    both = jnp.dot(x_hi, w_hilo_ref[...], preferred_element_type=F32)
    logits = (both[:, :ROUTE_WIDTH]
              + (jnp.dot(x_lo, w_hi_ref[...], preferred_element_type=F32) + both[:, ROUTE_WIDTH:])) + b_rt
    col = lax.broadcasted_iota(jnp.int32, logits.shape, 1)
    big = jnp.int32(ROUTE_WIDTH)
    is_g = col < N_GROUPS
    gl = jnp.where(is_g, logits, -jnp.inf)
    gmax = jnp.max(gl, axis=-1, keepdims=True)
    g_idx = jnp.min(jnp.where(gl == gmax, col, big), axis=-1, keepdims=True)
    p_g = 1.0 / jnp.sum(jnp.where(is_g, jnp.exp(gl - gmax), 0.0), axis=-1, keepdims=True)
    lo = N_GROUPS + g_idx * EXPERTS_PER_GROUP
    in_grp = (col >= lo) & (col < lo + EXPERTS_PER_GROUP)
    el = jnp.where(in_grp, logits, -jnp.inf)
    v1 = jnp.max(el, axis=-1, keepdims=True)
    i1 = jnp.min(jnp.where(el == v1, col, big), axis=-1, keepdims=True)
    el2 = jnp.where(col == i1, -jnp.inf, el)
    v2 = jnp.max(el2, axis=-1, keepdims=True)
    i2 = jnp.min(jnp.where(el2 == v2, col, big), axis=-1, keepdims=True)
    e21 = jnp.exp(v2 - v1)
    inv = 1.0 / (1.0 + e21)
    w1 = p_g * inv
    w2 = p_g * (e21 * inv)
    e1 = (i1 - N_GROUPS).astype(F32)
    e2 = (i2 - N_GROUPS).astype(F32)
    return jnp.where(col == 0, e1, jnp.where(col == 1, e2, jnp.where(col == 2, w1, jnp.where(col == 3, w2, 0.0))))


def _merge_norm(x, rec, att, g_l, g_a, w_out_ref, ln_g, ln_b):
    rec_o = jnp.dot(rec.astype(BF16), w_out_ref[0], preferred_element_type=F32)
    att_o = jnp.dot(att.astype(BF16), w_out_ref[1], preferred_element_type=F32)
    merged = _sigmoid(g_l) * rec_o + _sigmoid(g_a) * att_o
    mix = jnp.dot(merged.astype(BF16), w_out_ref[2], preferred_element_type=F32)
    return _layer_norm(ALPHA * x + mix, ln_g, ln_b)


def _mixer_kernel(sinks_ref, x_ref, w_in_ref, b_in_ref, w_conv_ref, b_conv_ref, w_gates_ref, b_rg_ref, b_ig_ref,
                  lam_ref, w_out_ref, ln_g_ref, ln_b_ref, w_rt_hi_ref, w_rt_lo_ref,
                  b_rt_ref, x1_s_ref,
                  x1_ref, route_ref, kwin_ref, vwin_ref, conv_ref, h_ref, xs_hbm, goff_ref, gtot_ref,
                  conv_buf, h_carry, kcat, vcat, att_buf, prev_x1, stage, zero_rows, tri, xb_keep, key_keep,
                  g_vmem, gvec, g_smem, nd_smem, rounds_smem, sem_stage, sem_g, sem_zero,
                  *, tiles_per_seq, n_tiles, seg_cap):
    step = pl.program_id(0)
    last_step = pl.num_programs(0) - 1
    buf = lax.rem(step, 2)
    n_sample = x1_s_ref.shape[0]
    disp = _Dispatcher(xs_hbm, goff_ref, gtot_ref, stage, zero_rows, tri, xb_keep, key_keep, g_vmem, gvec,
                       g_smem, nd_smem, rounds_smem, sem_stage, sem_g, sem_zero, seg_cap)
    n_valid = jnp.where(step == 0, 0, jnp.where(step == last_step, n_sample, SEQ_TILE))

    @pl.when(step == 0)
    def _():
        disp.init()
        prev_x1[...] = jnp.zeros_like(prev_x1)

    def route_and_plan():
        x1_prev = prev_x1[...]
        route = _route(x1_prev, w_rt_hi_ref, w_rt_lo_ref, b_rt_ref[...])
        route_ref[...] = route
        disp.plan(x1_prev, route, n_valid, buf)

    @pl.when(step < n_tiles)
    def _():
        @pl.when(lax.rem(step, tiles_per_seq) == 0)
        def _():
            conv_buf[...] = jnp.zeros_like(conv_buf)
            h_carry[...] = jnp.zeros_like(h_carry)
            kcat[0:WINDOW, :] = jnp.zeros((WINDOW, KV_WIDTH), BF16)
            vcat[0:WINDOW, :] = jnp.zeros((WINDOW, KV_WIDTH), BF16)

        route_and_plan()
        _mixer_tile(lax.rem(step, tiles_per_seq), sinks_ref, x_ref, w_in_ref, b_in_ref, w_conv_ref, b_conv_ref,
                    w_gates_ref, b_rg_ref, b_ig_ref, lam_ref, w_out_ref, ln_g_ref,
                    ln_b_ref, x1_ref, prev_x1, kwin_ref, vwin_ref, conv_ref, h_ref,
                    conv_buf, h_carry, kcat, vcat, att_buf)

    @pl.when(step >= n_tiles)
    def _():
        route_and_plan()

        @pl.when(step == n_tiles)
        def _():
            x1_ref[0:n_sample, :] = x1_s_ref[...]
            prev_x1[0:n_sample, :] = x1_s_ref[...]

    disp.flush(buf, step == last_step)


def _mixer_tile(t, sinks_ref, x_ref, w_in_ref, b_in_ref, w_conv_ref, b_conv_ref, w_gates_ref, b_rg_ref, b_ig_ref,
                lam_ref, w_out_ref, ln_g_ref, ln_b_ref,
                x1_ref, x1_keep, kwin_ref, vwin_ref, conv_ref, h_ref,
                conv_buf, h_carry, kcat, vcat, att_buf):
    T = SEQ_TILE
    x = x_ref[0]
    xb = x.astype(BF16)

    def proj(lo, width):
        return jnp.dot(xb, w_in_ref[:, lo:lo + width], preferred_element_type=F32) + b_in_ref[:, lo:lo + width]

    xl = proj(OFF_XL, LRU_WIDTH)
    xl_ext = jnp.concatenate([conv_buf[...], xl], axis=0)

    def lagged(k):
        return pltpu.roll(xl_ext, k, axis=0)[SUBLANES:, :]
    wc = w_conv_ref[...]
    xc = wc[0:1] * lagged(3)
    xc = xc + wc[1:2] * lagged(2)
    xc = xc + wc[2:3] * lagged(1)
    xc = xc + wc[3:4] * xl + b_conv_ref[...]
    conv_ref[0] = xl[T - (CONV_W - 1):, :]
    conv_buf[...] = xl[T - SUBLANES:, :]

    a, u = _lru_gates(xc, w_gates_ref, b_rg_ref[...], b_ig_ref[...], lam_ref[...])
    h = _linear_scan(a, u, h_carry[0:1, :])
    h_last = h[T - 1:T, :]
    h_carry[0:1, :] = h_last
    h_ref[0] = h_last
    rec = h * jax.nn.gelu(proj(OFF_YL, LRU_WIDTH))

    q = proj(OFF_Q, N_HEADS * HEAD_DIM) * (ATTN_SCALE * LOG2_E)
    k = proj(OFF_K, KV_WIDTH)
    v = proj(OFF_V, KV_WIDTH)
    kwin_ref[0] = k[T - WINDOW:, :]
    vwin_ref[0] = v[T - WINDOW:, :]
    kcat[WINDOW:WINDOW + T, :] = k.astype(BF16)
    vcat[WINDOW:WINDOW + T, :] = v.astype(BF16)

    qi = lax.broadcasted_iota(jnp.int32, (WINDOW, 2 * WINDOW), 0)
    kj = lax.broadcasted_iota(jnp.int32, (WINDOW, 2 * WINDOW), 1)
    band = (kj > qi) & (kj <= qi + WINDOW)
    grp_row = lax.broadcasted_iota(jnp.int32, (GROUP * WINDOW, 1), 0) // WINDOW
    for qb in range(T // WINDOW):
        if qb == 0:
            first_key = jnp.where(t == 0, WINDOW, 0)
            mask1 = band & (kj >= first_key)
        else:
            mask1 = band
        bias = jnp.concatenate([jnp.where(mask1, 0.0, NEG_INF)] * GROUP, axis=0)
        r0 = qb * WINDOW
        qq = q[r0:r0 + WINDOW, :]
        for j in range(N_KV):
            kjb = kcat[r0:r0 + 2 * WINDOW, j * HEAD_DIM:(j + 1) * HEAD_DIM]
            vjb = vcat[r0:r0 + 2 * WINDOW, j * HEAD_DIM:(j + 1) * HEAD_DIM]
            qs = jnp.concatenate(
                [qq[:, (j * GROUP + g) * HEAD_DIM:(j * GROUP + g + 1) * HEAD_DIM] for g in range(GROUP)], axis=0)
            s = lax.dot_general(qs.astype(BF16), kjb, (((1,), (1,)), ((), ())), preferred_element_type=F32) + bias
            sink = jnp.zeros((GROUP * WINDOW, 1), F32)
            for g in range(GROUP):
                sink = jnp.where(grp_row == g, sinks_ref[j * GROUP + g] * LOG2_E, sink)
            m = jnp.maximum(jnp.max(s, axis=-1, keepdims=True), sink)
            p = jnp.exp2(s - m)
            inv = 1.0 / (jnp.sum(p, axis=-1, keepdims=True) + jnp.exp2(sink - m))
            o = jnp.dot((p * inv).astype(BF16), vjb, preferred_element_type=F32)
            for g in range(GROUP):
                hcol = (j * GROUP + g) * HEAD_DIM
                att_buf[r0:r0 + WINDOW, hcol:hcol + HEAD_DIM] = o[g * WINDOW:(g + 1) * WINDOW, :]
    kcat[0:WINDOW, :] = kcat[T:T + WINDOW, :]
    vcat[0:WINDOW, :] = vcat[T:T + WINDOW, :]

    x1 = _merge_norm(x, rec, att_buf[...], proj(OFF_GL, D_MODEL), proj(OFF_GA, D_MODEL),
                     w_out_ref, ln_g_ref[...], ln_b_ref[...])
    x1_ref[...] = x1
    x1_keep[...] = x1


def _mixer_call(x_prompt, x1_s, sinks, wts):
    B, S, _ = x_prompt.shape
    T = SEQ_TILE
    assert T == DISPATCH_TILE and x1_s.shape[0] <= T
    nt = S // T
    n_tiles = B * nt
    n_rows_total = B * S + x1_s.shape[0]
    n_dispatch = _dispatch_tiles(n_rows_total)
    assert n_dispatch == n_tiles + 1
    seg_cap = _segment_capacity(n_rows_total)
    n_assign = TOP_K * T
    weight_args = (wts["w_in"], wts["b_in"], wts["w_conv"], wts["b_conv"], wts["w_gates"], wts["b_rg"], wts["b_ig"],
                   wts["lam"], wts["w_out"], wts["ln1_g"], wts["ln1_b"],
                   wts["w_rt_hi"], wts["w_rt_lo"], wts["b_rt"], x1_s)
    mixed = lambda i: jnp.minimum(i, n_tiles - 1)
    seq = lambda i: mixed(i) // nt
    routed = lambda i: jnp.clip(i - 1, 0, n_dispatch - 1)
    in_specs = [pl.BlockSpec(memory_space=pltpu.SMEM),
                pl.BlockSpec((1, T, D_MODEL), lambda i: (seq(i), lax.rem(mixed(i), nt), 0))]
    in_specs += [_const_spec(w.shape) for w in weight_args]
    out_shape = (
        jax.ShapeDtypeStruct((n_rows_total, D_MODEL), F32),
        jax.ShapeDtypeStruct((n_rows_total, ROUTE_WIDTH), F32),
        jax.ShapeDtypeStruct((B, WINDOW, KV_WIDTH), F32),
        jax.ShapeDtypeStruct((B, WINDOW, KV_WIDTH), F32),
        jax.ShapeDtypeStruct((B, CONV_W - 1, LRU_WIDTH), F32),
        jax.ShapeDtypeStruct((B, 1, LRU_WIDTH), F32),
        jax.ShapeDtypeStruct((N_EXPERTS * seg_cap, PACKED_WIDTH), jnp.uint32),
        jax.ShapeDtypeStruct((n_dispatch, 1, ROUTE_WIDTH), jnp.int32),
        jax.ShapeDtypeStruct((1, ROUTE_WIDTH), jnp.int32),
    )
    out_specs = (
        pl.BlockSpec((T, D_MODEL), lambda i: (jnp.minimum(i, n_tiles), 0)),
        pl.BlockSpec((T, ROUTE_WIDTH), lambda i: (routed(i), 0)),
        pl.BlockSpec((1, WINDOW, KV_WIDTH), lambda i: (seq(i), 0, 0)),
        pl.BlockSpec((1, WINDOW, KV_WIDTH), lambda i: (seq(i), 0, 0)),
        pl.BlockSpec((1, CONV_W - 1, LRU_WIDTH), lambda i: (seq(i), 0, 0)),
        pl.BlockSpec((1, 1, LRU_WIDTH), lambda i: (seq(i), 0, 0)),
        pl.BlockSpec(memory_space=pl.ANY),
        pl.BlockSpec((1, 1, ROUTE_WIDTH), lambda i: (routed(i), 0, 0)),
        pl.BlockSpec((1, ROUTE_WIDTH), lambda i: (0, 0)),
    )
    scratch = [
        pltpu.VMEM((SUBLANES, LRU_WIDTH), F32),
        pltpu.VMEM((SUBLANES, LRU_WIDTH), F32),
        pltpu.VMEM((T + WINDOW, KV_WIDTH), BF16),
        pltpu.VMEM((T + WINDOW, KV_WIDTH), BF16),
        pltpu.VMEM((T, N_HEADS * HEAD_DIM), F32),
        pltpu.VMEM((T, D_MODEL), F32),
        pltpu.VMEM((2, N_SLOTS, PACKED_WIDTH), jnp.uint32),
        pltpu.VMEM((PAD_CHUNK, PACKED_WIDTH), jnp.uint32),
        pltpu.VMEM((n_assign, n_assign), BF16),
        pltpu.VMEM((T, D_MODEL), BF16),
        pltpu.VMEM((SUBLANES, n_assign), F32),
        pltpu.VMEM((SUBLANES, ROUTE_WIDTH), jnp.int32),
        pltpu.VMEM((SUBLANES, ROUTE_WIDTH), F32),
        pltpu.SMEM((1, ROUTE_WIDTH), jnp.int32),
        pltpu.SMEM((1,), jnp.int32),
        pltpu.SMEM((1,), jnp.int32),
        pltpu.SemaphoreType.DMA(()),
        pltpu.SemaphoreType.DMA(()),
        pltpu.SemaphoreType.DMA(()),
    ]
    return pl.pallas_call(
        functools.partial(_mixer_kernel, tiles_per_seq=nt, n_tiles=n_tiles, seg_cap=seg_cap),
        grid=(n_tiles + 2,),
        in_specs=in_specs,
        out_specs=out_specs,
        out_shape=out_shape,
        scratch_shapes=scratch,
        compiler_params=pltpu.CompilerParams(dimension_semantics=("arbitrary",),
                                             vmem_limit_bytes=VMEM_LIMIT_BYTES),
        name="mixer_prompt",
    )(sinks, x_prompt, *weight_args)


def _sample_proj_kernel(x_ref, w_ref, b_ref, u_ref):
    u_ref[...] = jnp.dot(x_ref[...].astype(BF16), w_ref[...], preferred_element_type=F32) + b_ref[...]


def _sample_proj_call(x_s, w_in, b_in):
    n = x_s.shape[0]
    tn = SAMPLE_PROJ_TILE
    return pl.pallas_call(
        _sample_proj_kernel,
        grid=(IN_WIDTH // tn,),
        in_specs=[pl.BlockSpec((n, D_MODEL), lambda c: (0, 0)),
                  pl.BlockSpec((D_MODEL, tn), lambda c: (0, c)),
                  pl.BlockSpec((1, tn), lambda c: (0, c))],
        out_specs=pl.BlockSpec((n, tn), lambda c: (0, c)),
        out_shape=jax.ShapeDtypeStruct((n, IN_WIDTH), F32),
        compiler_params=pltpu.CompilerParams(dimension_semantics=("arbitrary",)),
        name="sample_proj",
    )(x_s, w_in, b_in)


def _sample_attn_kernel(q_ref, kn_ref, vn_ref, kn_row_ref, vn_row_ref, ck_ref, cv_ref, sinks_ref,
                        att_ref, kwin_ref, vwin_ref):
    tb = q_ref.shape[0]
    key_pos = lax.broadcasted_iota(jnp.int32, (tb, GROUP, WINDOW), 2)
    for j in range(N_KV):
        qj = q_ref[:, j * GROUP:(j + 1) * GROUP, :]
        kc = ck_ref[:, :, j * HEAD_DIM:(j + 1) * HEAD_DIM]
        vc = cv_ref[:, :, j * HEAD_DIM:(j + 1) * HEAD_DIM]
        s_c = jnp.einsum("bgd,bsd->bgs", qj.astype(BF16), kc.astype(BF16), preferred_element_type=F32) * ATTN_SCALE
        s_c = jnp.where(key_pos >= 1, s_c, NEG_INF)
        kn = kn_ref[:, j:j + 1, :]
        vn = vn_ref[:, j:j + 1, :]
        s_n = jnp.sum(qj * kn, axis=-1, keepdims=True) * ATTN_SCALE
        sink = sinks_ref[j][None]
        m = jnp.maximum(jnp.maximum(jnp.max(s_c, axis=-1, keepdims=True), s_n), sink)
        p_c = jnp.exp(s_c - m)
        p_n = jnp.exp(s_n - m)
        inv = 1.0 / (jnp.sum(p_c, axis=-1, keepdims=True) + p_n + jnp.exp(sink - m))
        o = jnp.einsum("bgs,bsd->bgd", (p_c * inv).astype(BF16), vc.astype(BF16), preferred_element_type=F32)
        att_ref[:, j * GROUP:(j + 1) * GROUP, :] = o + (p_n * inv) * vn
    kwin_ref[:, 0:WINDOW - 1, :] = ck_ref[:, 1:WINDOW, :]
    kwin_ref[:, WINDOW - 1:WINDOW, :] = kn_row_ref[...]
    vwin_ref[:, 0:WINDOW - 1, :] = cv_ref[:, 1:WINDOW, :]
    vwin_ref[:, WINDOW - 1:WINDOW, :] = vn_row_ref[...]


def _sample_attn_call(q3, kn3, vn3, kn_row, vn_row, ck, cv, sinks3):
    n = q3.shape[0]
    tb = SAMPLE_ATTN_TILE
    b3 = lambda i: (i, 0, 0)
    return pl.pallas_call(
        _sample_attn_kernel,
        grid=(n // tb,),
        in_specs=[pl.BlockSpec((tb, N_HEADS, HEAD_DIM), b3),
                  pl.BlockSpec((tb, N_KV, HEAD_DIM), b3),
                  pl.BlockSpec((tb, N_KV, HEAD_DIM), b3),
                  pl.BlockSpec((tb, 1, KV_WIDTH), b3),
                  pl.BlockSpec((tb, 1, KV_WIDTH), b3),
                  pl.BlockSpec((tb, WINDOW, KV_WIDTH), b3),
                  pl.BlockSpec((tb, WINDOW, KV_WIDTH), b3),
                  pl.BlockSpec((N_KV, GROUP, 1), lambda i: (0, 0, 0))],
        out_specs=(pl.BlockSpec((tb, N_HEADS, HEAD_DIM), b3),
                   pl.BlockSpec((tb, WINDOW, KV_WIDTH), b3),
                   pl.BlockSpec((tb, WINDOW, KV_WIDTH), b3)),
        out_shape=(jax.ShapeDtypeStruct((n, N_HEADS, HEAD_DIM), F32),
                   jax.ShapeDtypeStruct((n, WINDOW, KV_WIDTH), F32),
                   jax.ShapeDtypeStruct((n, WINDOW, KV_WIDTH), F32)),
        compiler_params=pltpu.CompilerParams(dimension_semantics=("arbitrary",)),
        name="sample_attn",
    )(q3, kn3, vn3, kn_row, vn_row, ck, cv, sinks3)


def _sample_mix_kernel(x_ref, u_ref, att_ref, st_ref, h0_ref, w_conv_ref, b_conv_ref, w_gates_ref, b_rg_ref,
                       b_ig_ref, lam_ref, w_out_ref, ln_g_ref, ln_b_ref,
                       x1_ref, conv_ref, h_ref):
    xl = u_ref[:, OFF_XL:OFF_XL + LRU_WIDTH]
    wc = w_conv_ref[...]
    xc = wc[0:1] * st_ref[0]
    xc = xc + wc[1:2] * st_ref[1]
    xc = xc + wc[2:3] * st_ref[2]
    xc = xc + wc[3:4] * xl + b_conv_ref[...]
    conv_ref[0] = st_ref[1]
    conv_ref[1] = st_ref[2]
    conv_ref[2] = xl
    a, u = _lru_gates(xc, w_gates_ref, b_rg_ref[...], b_ig_ref[...], lam_ref[...])
    h = a * h0_ref[...] + u
    h_ref[...] = h
    rec = h * jax.nn.gelu(u_ref[:, OFF_YL:OFF_YL + LRU_WIDTH])
    x1_ref[...] = _merge_norm(x_ref[...], rec, att_ref[...], u_ref[:, OFF_GL:OFF_GL + D_MODEL],
                              u_ref[:, OFF_GA:OFF_GA + D_MODEL], w_out_ref,
                              ln_g_ref[...], ln_b_ref[...])


def _sample_mix_call(x_s, u_s, att, st_t, h0, wts):
    n = x_s.shape[0]
    weight_args = (wts["w_conv"], wts["b_conv"], wts["w_gates"], wts["b_rg"], wts["b_ig"], wts["lam"],
                   wts["w_out"], wts["ln1_g"], wts["ln1_b"])
    args = (x_s, u_s, att, st_t, h0) + weight_args
    out_shapes = ((n, D_MODEL), (CONV_W - 1, n, LRU_WIDTH), (n, LRU_WIDTH))
    return pl.pallas_call(
        _sample_mix_kernel,
        grid=(1,),
        in_specs=[_const_spec(a.shape) for a in args],
        out_specs=tuple(_const_spec(s) for s in out_shapes),
        out_shape=tuple(jax.ShapeDtypeStruct(s, F32) for s in out_shapes),
        compiler_params=pltpu.CompilerParams(dimension_semantics=("arbitrary",),
                                             vmem_limit_bytes=VMEM_LIMIT_BYTES),
        name="sample_mix",
    )(*args)


def _one_hot(mask):
    return jnp.where(mask, 1.0, 0.0).astype(BF16)


def _pack_rows(x):
    half = x.shape[1] // 2
    lo = lax.shift_right_logical(lax.bitcast_convert_type(x[:, :half], jnp.uint32), jnp.uint32(16))
    hi = lax.bitcast_convert_type(x[:, half:], jnp.uint32) & jnp.uint32(0xFFFF0000)
    return lo | hi


def _unpack_rows(words):
    lo = lax.bitcast_convert_type(lax.shift_left(words, jnp.uint32(16)), F32)
    hi = lax.bitcast_convert_type(words & jnp.uint32(0xFFFF0000), F32)
    return jnp.concatenate([lo.astype(BF16), hi.astype(BF16)], axis=1)


class _Dispatcher:
    def __init__(self, xs_hbm, goff_ref, gtot_ref, stage, zero_rows, tri, xb_keep, key_keep, g_vmem, gvec,
                 g_smem, nd_smem, rounds_smem, sem_stage, sem_g, sem_zero, seg_cap):
        self.xs_hbm, self.goff_ref, self.gtot_ref = xs_hbm, goff_ref, gtot_ref
        self.stage, self.zero_rows, self.tri = stage, zero_rows, tri
        self.xb_keep, self.key_keep = xb_keep, key_keep
        self.g_vmem, self.gvec, self.g_smem = g_vmem, gvec, g_smem
        self.nd_smem, self.rounds_smem = nd_smem, rounds_smem
        self.sem_stage, self.sem_g, self.sem_zero = sem_stage, sem_g, sem_zero
        self.seg_cap = seg_cap

    def _g_copy(self):
        return pltpu.make_async_copy(self.g_vmem.at[pl.ds(0, 1)], self.g_smem, self.sem_g)

    def init(self):
        n_assign = self.tri.shape[0]
        r = lax.broadcasted_iota(jnp.int32, (n_assign, n_assign), 0)
        c = lax.broadcasted_iota(jnp.int32, (n_assign, n_assign), 1)
        self.tri[...] = _one_hot(r < c)
        self.gvec[...] = jnp.zeros_like(self.gvec)
        self.g_vmem[...] = jnp.zeros_like(self.g_vmem)
        self.nd_smem[0] = 0
        self._g_copy().start()

    def _sorted_rows(self, e_row, rank_row, xb, lo):
        td = xb.shape[0]
        slot = lax.broadcasted_iota(jnp.int32, (N_SLOTS, td), 0).astype(F32)
        lo_f = lo.astype(F32)
        in_round = (rank_row >= lo_f) & (rank_row < lo_f + SLOT_CAP) & (e_row >= 0.0)
        key = jnp.where(in_round, e_row * SLOT_CAP + (rank_row - lo_f), -1.0)
        perm = jnp.where(slot == key[:, :td], 1.0, jnp.where(slot == key[:, td:], 1.0, 0.0)).astype(BF16)
        return _pack_rows(jnp.dot(perm, xb, preferred_element_type=F32))

    def plan(self, x1, route, n_valid, buf):
        td = x1.shape[0]
        n_assign = TOP_K * td
        valid_col = lax.broadcasted_iota(jnp.int32, (td, 1), 0) < n_valid
        lane = lax.broadcasted_iota(jnp.int32, (td, ROUTE_WIDTH), 1).astype(F32)
        e0c = jnp.where(valid_col, route[:, 0:1], -1.0)
        e1c = jnp.where(valid_col, route[:, 1:2], -1.0)
        cnt_row = jnp.sum(jnp.where(lane == e0c, 1.0, 0.0) + jnp.where(lane == e1c, 1.0, 0.0),
                          axis=0, keepdims=True)
        self.rounds_smem[0] = (jnp.max(cnt_row).astype(jnp.int32) + SLOT_CAP - 1) // SLOT_CAP
        self.goff_ref[0] = self.gvec[0:1, :].astype(jnp.int32)
        self.gvec[0:1, :] = self.gvec[0:1, :] + jnp.ceil(cnt_row * (1.0 / SUBLANES)) * SUBLANES
        self.gtot_ref[...] = self.gvec[0:1, :].astype(jnp.int32)

        route_t = route.T
        valid_row = lax.broadcasted_iota(jnp.int32, (1, td), 1) < n_valid
        e_row = jnp.concatenate([jnp.where(valid_row, route_t[0:1, :], -1.0),
                                 jnp.where(valid_row, route_t[1:2, :], -1.0)], axis=1)
        expert_sub = lax.broadcasted_iota(jnp.int32, (N_EXPERTS, n_assign), 0).astype(F32)
        onehot_t = expert_sub == e_row
        rank_t = jnp.dot(_one_hot(onehot_t), self.tri[...], preferred_element_type=F32)
        rank_row = jnp.sum(jnp.where(onehot_t, rank_t, 0.0), axis=0, keepdims=True)
        xb = jnp.where(valid_col, x1, 0.0).astype(BF16)
        self.xb_keep[...] = xb
        self.key_keep[0:1, :] = e_row
        self.key_keep[1:2, :] = rank_row
        self.stage[buf] = self._sorted_rows(e_row, rank_row, xb, jnp.int32(0))

    def _stage_copy(self, buf, e, dst_row):
        src = self.stage.at[buf, pl.ds(e * SLOT_CAP, SLOT_CAP)]
        return pltpu.make_async_copy(src, self.xs_hbm.at[pl.ds(dst_row, SLOT_CAP)], self.sem_stage)

    def _wait_outstanding(self):
        @pl.when(self.nd_smem[0] > 0)
        def _():
            for _ in range(N_EXPERTS):
                self._stage_copy(0, 0, 0).wait()
        self.nd_smem[0] = 0

    def _start_round(self, buf, lo):
        for e in range(N_EXPERTS):
            dst_row = pl.multiple_of(e * self.seg_cap + self.g_smem[0, e] + lo, SUBLANES)
            self._stage_copy(buf, e, dst_row).start()
        self.nd_smem[0] = N_EXPERTS

    def flush(self, buf, is_last):
        self._g_copy().wait()
        self._wait_outstanding()
        self._start_round(buf, 0)

        def later_round(r, carry):
            lo = r * SLOT_CAP
            rows = self._sorted_rows(self.key_keep[0:1, :], self.key_keep[1:2, :], self.xb_keep[...], lo)
            self._wait_outstanding()
            self.stage[buf] = rows
            self._start_round(buf, lo)
            return carry
        lax.fori_loop(1, self.rounds_smem[0], later_round, 0)

        self.g_vmem[0:1, :] = self.gvec[0:1, :].astype(jnp.int32)
        self._g_copy().start()

        @pl.when(is_last)
        def _():
            self._g_copy().wait()
            self._wait_outstanding()
            self.zero_rows[...] = jnp.zeros_like(self.zero_rows)

            def pad_copy(dst_row):
                dst = self.xs_hbm.at[pl.ds(pl.multiple_of(dst_row, SUBLANES), PAD_CHUNK)]
                return pltpu.make_async_copy(self.zero_rows, dst, self.sem_zero)

            def pad_segment(e, n_started):
                fill = self.g_smem[0, e]
                block_end = (fill + SLOT_CAP + MOE_TILE - 1) // MOE_TILE * MOE_TILE
                n_chunks = (block_end - fill + PAD_CHUNK - 1) // PAD_CHUNK

                def start(c, carry):
                    pad_copy(e * self.seg_cap + fill + c * PAD_CHUNK).start()
                    return carry
                lax.fori_loop(0, n_chunks, start, 0)
                return n_started + n_chunks
            n_started = lax.fori_loop(0, N_EXPERTS, pad_segment, 0)

            def wait(_, carry):
                pad_copy(0).wait()
                return carry
            lax.fori_loop(0, n_started, wait, 0)


def _dispatch_tiles(n_tokens):
    return (n_tokens + DISPATCH_TILE - 1) // DISPATCH_TILE


def _segment_capacity(n_tokens):
    assert SEG_PAD >= DISPATCH_TILE
    alignment_slack = (SUBLANES - 1) * _dispatch_tiles(n_tokens)
    return (n_tokens + alignment_slack + SEG_PAD + MOE_TILE - 1) // MOE_TILE * MOE_TILE


def _moe_kernel(blk_e_ref, blk_j_ref, n_used_ref, xs_ref, wg_ref, wu_ref, wd_ref, ys_ref, wg_bf, wu_bf, wd_bf):
    del blk_j_ref
    b = pl.program_id(0)

    @pl.when(b < n_used_ref[0])
    def _():
        prev_e = blk_e_ref[jnp.maximum(b - 1, 0)]

        @pl.when((b == 0) | (blk_e_ref[b] != prev_e))
        def _():
            wg_bf[...] = wg_ref[0].astype(BF16)
            wu_bf[...] = wu_ref[0].astype(BF16)
            wd_bf[...] = wd_ref[0].astype(BF16)

        xb = _unpack_rows(xs_ref[...])
        g = jnp.dot(xb, wg_bf[...], preferred_element_type=F32)
        u = jnp.dot(xb, wu_bf[...], preferred_element_type=F32)
        hmid = (g * _sigmoid(g)) * u
        y = jnp.dot(hmid.astype(BF16), wd_bf[...], preferred_element_type=F32)
        ys_ref[...] = _pack_rows(y.astype(BF16).astype(F32))


def _expert_blocks(gtot, n_blocks):
    rows = gtot[0, :N_EXPERTS]
    nb = (rows + SLOT_CAP + MOE_TILE - 1) // MOE_TILE
    ends = jnp.cumsum(nb)
    n_used = ends[-1]
    step = jnp.minimum(jnp.arange(n_blocks, dtype=jnp.int32), n_used - 1)
    blk_e = jnp.sum((step[:, None] >= ends[None, :]).astype(jnp.int32), axis=1)
    blk_j = step - (ends - nb)[blk_e]
    return blk_e, blk_j.astype(jnp.int32), n_used.reshape(1).astype(jnp.int32)


def _moe_call(gtot, xs, w_gate, w_up, w_down, n_tokens):
    seg_cap = _segment_capacity(n_tokens)
    seg_blocks = seg_cap // MOE_TILE
    max_rows = n_tokens * TOP_K + N_EXPERTS * (SUBLANES - 1) * _dispatch_tiles(n_tokens)
    n_blocks = (max_rows + N_EXPERTS * (SLOT_CAP + MOE_TILE - 1)) // MOE_TILE
    blk_e, blk_j, n_used = _expert_blocks(gtot, n_blocks)
    row_block = lambda b, be, bj, nu: (be[b] * seg_blocks + bj[b], 0)
    weight_block = lambda b, be, bj, nu: (be[b], 0, 0)
    grid_spec = pltpu.PrefetchScalarGridSpec(
        num_scalar_prefetch=3,
        grid=(n_blocks,),
        in_specs=[
            pl.BlockSpec((MOE_TILE, PACKED_WIDTH), row_block),
            pl.BlockSpec((1, D_MODEL, EXPERT_FF), weight_block),
            pl.BlockSpec((1, D_MODEL, EXPERT_FF), weight_block),
            pl.BlockSpec((1, EXPERT_FF, D_MODEL), weight_block),
        ],
        out_specs=pl.BlockSpec((MOE_TILE, PACKED_WIDTH), row_block),
        scratch_shapes=[
            pltpu.VMEM((D_MODEL, EXPERT_FF), BF16),
            pltpu.VMEM((D_MODEL, EXPERT_FF), BF16),
            pltpu.VMEM((EXPERT_FF, D_MODEL), BF16),
        ],
    )
    return pl.pallas_call(
        _moe_kernel,
        grid_spec=grid_spec,
        out_shape=jax.ShapeDtypeStruct(xs.shape, xs.dtype),
        compiler_params=pltpu.CompilerParams(dimension_semantics=("arbitrary",),
                                             vmem_limit_bytes=VMEM_LIMIT_BYTES),
        name="moe_experts",
    )(blk_e, blk_j, n_used, xs, w_gate, w_up, w_down)


def _combine_kernel(gcur_ref, gnext_ref, x1_ref, route_ref, ys_hbm, ln_g_ref, ln_b_ref, y_ref,
                    stage, acc, tri, sem, *, seg_cap):
    td = x1_ref.shape[0]
    n_assign = TOP_K * td
    i = pl.program_id(0)
    buf = lax.rem(i, 2)

    def stage_copy(g_ref, e, first_rank, to_buf):
        src_row = pl.multiple_of(e * seg_cap + g_ref[0, 0, e] + first_rank, SUBLANES)
        slot0 = e * SLOT_CAP
        if not isinstance(slot0, int):
            slot0 = pl.multiple_of(slot0, SLOT_CAP)
        dst = stage.at[to_buf, pl.ds(slot0, SLOT_CAP)]
        return pltpu.make_async_copy(ys_hbm.at[pl.ds(src_row, SLOT_CAP)], dst, sem.at[to_buf])

    def fetch_first_round(g_ref, to_buf):
        for e in range(N_EXPERTS):
            stage_copy(g_ref, e, 0, to_buf).start()

    @pl.when(i == 0)
    def _():
        r = lax.broadcasted_iota(jnp.int32, (n_assign, n_assign), 0)
        c = lax.broadcasted_iota(jnp.int32, (n_assign, n_assign), 1)
        tri[...] = _one_hot(c < r)
        fetch_first_round(gcur_ref, 0)

    def count(e):
        return gnext_ref[0, 0, e] - gcur_ref[0, 0, e]

    route = route_ref[...]
    e0, e1, w0, w1 = route[:, 0:1], route[:, 1:2], route[:, 2:3], route[:, 3:4]
    lane = lax.broadcasted_iota(jnp.int32, (td, ROUTE_WIDTH), 1).astype(F32)
    onehot = jnp.concatenate([lane == e0, lane == e1], axis=0)
    onehot_f = jnp.where(onehot, 1.0, 0.0)
    rank_mat = jnp.dot(tri[...], onehot_f.astype(BF16), preferred_element_type=F32)
    rank = jnp.sum(jnp.where(onehot, rank_mat, 0.0), axis=1, keepdims=True)
    r0, r1 = rank[:td], rank[td:]
    cnt_row = jnp.sum(onehot_f, axis=0, keepdims=True)
    rounds = (jnp.max(cnt_row).astype(jnp.int32) + SLOT_CAP - 1) // SLOT_CAP

    slot_col = lax.broadcasted_iota(jnp.int32, (N_SLOTS, 1), 0)
    slot_expert = slot_col // SLOT_CAP
    slot_rank = slot_col - slot_expert * SLOT_CAP
    lane_s = lax.broadcasted_iota(jnp.int32, (N_SLOTS, ROUTE_WIDTH), 1)
    n_col = jnp.sum(jnp.where(lane_s == slot_expert, cnt_row, 0.0), axis=1, keepdims=True)
    slot_lane = lax.broadcasted_iota(jnp.int32, (td, N_SLOTS), 1).astype(F32)

    def wait_copies(n):
        def wait_one(_, c):
            stage_copy(gcur_ref, 0, 0, buf).wait()
            return c
        lax.fori_loop(0, n, wait_one, 0)

    def weighted_rows(lo):
        lo_f = lo.astype(F32)
        live = (slot_rank + lo).astype(F32) < n_col
        rows = _unpack_rows(jnp.where(live, stage[buf], jnp.uint32(0)))

        def selector(e_col, r_col, w_col):
            in_round = (r_col >= lo_f) & (r_col < lo_f + SLOT_CAP)
            key = jnp.where(in_round, e_col * SLOT_CAP + (r_col - lo_f), -1.0)
            return jnp.where(slot_lane == key, w_col, 0.0)
        mix = (selector(e0, r0, w0) + selector(e1, r1, w1)).astype(BF16)
        return jnp.dot(mix, rows, preferred_element_type=F32)

    for _ in range(N_EXPERTS):
        stage_copy(gcur_ref, 0, 0, buf).wait()

    @pl.when(i + 1 < pl.num_programs(0))
    def _():
        fetch_first_round(gnext_ref, 1 - buf)

    acc[...] = weighted_rows(jnp.int32(0))

    def later_round(r, carry):
        lo = r * SLOT_CAP

        def start(e, n_started):
            has_rows = count(e) > lo

            @pl.when(has_rows)
            def _():
                stage_copy(gcur_ref, e, lo, buf).start()
            return n_started + has_rows.astype(jnp.int32)
        wait_copies(lax.fori_loop(0, N_EXPERTS, start, 0))
        acc[...] = acc[...] + weighted_rows(lo)
        return carry
    lax.fori_loop(1, rounds, later_round, 0)

    y_ref[...] = _layer_norm(ALPHA * x1_ref[...] + acc[...], ln_g_ref[...], ln_b_ref[...])


def _combine_call(goff, x1_all, route_all, ys, ln_g, ln_b, first_row, n_rows, tile, n_tokens):
    assert first_row % DISPATCH_TILE == 0 and first_row % tile == 0
    assert tile == DISPATCH_TILE or n_rows == tile
    first_block = first_row // tile
    first_goff = first_row // DISPATCH_TILE
    n_assign = TOP_K * tile
    grid_spec = pl.GridSpec(
        grid=(n_rows // tile,),
        in_specs=[
            pl.BlockSpec((1, 1, ROUTE_WIDTH), lambda i: (first_goff + i, 0, 0), memory_space=pltpu.SMEM),
            pl.BlockSpec((1, 1, ROUTE_WIDTH), lambda i: (first_goff + i + 1, 0, 0), memory_space=pltpu.SMEM),
            pl.BlockSpec((tile, D_MODEL), lambda i: (first_block + i, 0)),
            pl.BlockSpec((tile, ROUTE_WIDTH), lambda i: (first_block + i, 0)),
            pl.BlockSpec(memory_space=pl.ANY),
            _const_spec(ln_g.shape),
            _const_spec(ln_b.shape),
        ],
        out_specs=pl.BlockSpec((tile, D_MODEL), lambda i: (i, 0)),
        scratch_shapes=[pltpu.VMEM((2, N_SLOTS, PACKED_WIDTH), jnp.uint32),
                        pltpu.VMEM((tile, D_MODEL), F32),
                        pltpu.VMEM((n_assign, n_assign), BF16),
                        pltpu.SemaphoreType.DMA((2,))],
    )
    return pl.pallas_call(
        functools.partial(_combine_kernel, seg_cap=_segment_capacity(n_tokens)),
        grid_spec=grid_spec,
        out_shape=jax.ShapeDtypeStruct((n_rows, D_MODEL), F32),
        compiler_params=pltpu.CompilerParams(dimension_semantics=("arbitrary",),
                                             vmem_limit_bytes=VMEM_LIMIT_BYTES),
        name="moe_combine",
    )(goff, goff, x1_all, route_all, ys, ln_g, ln_b)


def _prepare_weights(w_in, b_in, w_conv, b_conv, w_rg, b_rg, w_ig, b_ig, lru_lambda, w_lru_out, w_attn_out, w_o,
                     ln1_g, ln1_b, w_group, b_group, w_router, b_router):
    blocks_per_chunk = GATE_CHUNK // LRU_BLOCK

    def chunked_block_diag(w):
        w = w.reshape(N_GATE_CHUNKS, blocks_per_chunk, LRU_BLOCK, LRU_BLOCK)
        eye = jnp.eye(blocks_per_chunk, dtype=w.dtype)
        return jnp.einsum("cbij,bd->cbidj", w, eye).reshape(N_GATE_CHUNKS, GATE_CHUNK, GATE_CHUNK)

    w_gates = jnp.concatenate([chunked_block_diag(w_rg), chunked_block_diag(w_ig)], axis=-1).astype(BF16)
    w_rt = jnp.concatenate([w_group, w_router], axis=1)
    w_rt = jnp.pad(w_rt, ((0, 0), (0, ROUTE_WIDTH - w_rt.shape[1])))
    w_rt_hi = w_rt.astype(BF16)
    w_rt_lo = jnp.concatenate([w_rt_hi, (w_rt - w_rt_hi.astype(F32)).astype(BF16)], axis=1)
    b_rt = jnp.pad(jnp.concatenate([b_group, b_router]), (0, ROUTE_WIDTH - N_GROUPS - N_EXPERTS))
    row = lambda v: v.reshape(1, -1)
    return dict(
        w_in=w_in.astype(BF16), b_in=row(b_in), w_conv=w_conv, b_conv=row(b_conv), w_gates=w_gates,
        b_rg=row(b_rg), b_ig=row(b_ig), lam=row(lru_lambda),
        w_out=jnp.stack([w_lru_out, w_attn_out, w_o]).astype(BF16),
        ln1_g=row(ln1_g), ln1_b=row(ln1_b), w_rt_hi=w_rt_hi, w_rt_lo=w_rt_lo, b_rt=row(b_rt))


def kernel(x_prompt, x_sample, cache_k, cache_v, state_conv, state_lru_h, w_in, b_in, w_conv, b_conv, w_rg, b_rg,
           w_ig, b_ig, lru_lambda, sinks, w_lru_out, w_attn_out, w_o, ln1_g, ln1_b, w_group, b_group, w_router,
           b_router, w_gate, w_up, w_down, ln2_g, ln2_b):
    B, S, _ = x_prompt.shape
    n_prompt = B * S
    n_sample = x_sample.shape[0]
    n_all = n_prompt + n_sample
    wts = _prepare_weights(w_in, b_in, w_conv, b_conv, w_rg, b_rg, w_ig, b_ig, lru_lambda, w_lru_out, w_attn_out,
                           w_o, ln1_g, ln1_b, w_group, b_group, w_router, b_router)

    x_s = x_sample.reshape(n_sample, D_MODEL)
    u_s = _sample_proj_call(x_s, wts["w_in"], wts["b_in"])
    q3 = u_s[:, OFF_Q:OFF_K].reshape(n_sample, N_HEADS, HEAD_DIM)
    k_new = u_s[:, OFF_K:OFF_V]
    v_new = u_s[:, OFF_V:OFF_GL]
    att3, k_win_s, v_win_s = _sample_attn_call(
        q3, k_new.reshape(n_sample, N_KV, HEAD_DIM), v_new.reshape(n_sample, N_KV, HEAD_DIM),
        k_new.reshape(n_sample, 1, KV_WIDTH), v_new.reshape(n_sample, 1, KV_WIDTH),
        cache_k.reshape(n_sample, WINDOW, KV_WIDTH), cache_v.reshape(n_sample, WINDOW, KV_WIDTH),
        sinks.reshape(N_KV, GROUP, 1))
    x1_s, conv_s_t, h_s = _sample_mix_call(
        x_s, u_s, att3.reshape(n_sample, N_HEADS * HEAD_DIM), jnp.transpose(state_conv, (1, 0, 2)), state_lru_h, wts)

    x1_all, route_all, k_win_p, v_win_p, conv_p, h_p, xs, goff, gtot = _mixer_call(x_prompt, x1_s, sinks, wts)

    ys = _moe_call(gtot, xs, w_gate, w_up, w_down, n_all)
    goff = jnp.concatenate([goff, gtot[None]], axis=0)
    ln2_g2, ln2_b2 = ln2_g.reshape(1, -1), ln2_b.reshape(1, -1)
    y_p = _combine_call(goff, x1_all, route_all, ys, ln2_g2, ln2_b2, 0, n_prompt, DISPATCH_TILE, n_all)
    y_s = _combine_call(goff, x1_all, route_all, ys, ln2_g2, ln2_b2, n_prompt, n_sample, n_sample, n_all)

    kv_shape = (WINDOW, N_KV, HEAD_DIM)
    return (y_p.reshape(B, S, D_MODEL), y_s.reshape(n_sample, 1, D_MODEL),
            k_win_p.reshape((B,) + kv_shape), v_win_p.reshape((B,) + kv_shape), conv_p, h_p.reshape(B, LRU_WIDTH),
            k_win_s.reshape((n_sample,) + kv_shape), v_win_s.reshape((n_sample,) + kv_shape),
            jnp.transpose(conv_s_t, (1, 0, 2)), h_s)
```

```python
import functools

import jax
import jax.numpy as jnp
from jax import lax
from jax.experimental import pallas as pl
from jax.experimental.pallas import tpu as pltpu

F32 = jnp.float32
BF16 = jnp.bfloat16

D_MODEL = 1024
LRU_WIDTH = 1024
LRU_BLOCK = 64
CONV_W = 4
LRU_C = 8.0
N_HEADS = 16
N_KV = 4
GROUP = N_HEADS // N_KV
HEAD_DIM = 64
KV_WIDTH = N_KV * HEAD_DIM
WINDOW = 128
NEG_INF = -1e30
N_GROUPS = 4
EXPERTS_PER_GROUP = 8
N_EXPERTS = N_GROUPS * EXPERTS_PER_GROUP
TOP_K = 2
EXPERT_FF = D_MODEL // 2
DEPTH = 1
ALPHA = (2 * DEPTH) ** 0.25
LN_EPS = 1e-5
ATTN_SCALE = HEAD_DIM ** -0.5
LOG2_E = 1.4426950408889634

OFF_XL = 0
OFF_YL = OFF_XL + LRU_WIDTH
OFF_Q = OFF_YL + LRU_WIDTH
OFF_K = OFF_Q + N_HEADS * HEAD_DIM
OFF_V = OFF_K + KV_WIDTH
OFF_GL = OFF_V + KV_WIDTH
OFF_GA = OFF_GL + D_MODEL
IN_WIDTH = OFF_GA + D_MODEL

LANES = 128
SUBLANES = 8
MXU_DIM = 256
VMEM_LIMIT_BYTES = 56 * 1024 * 1024

GATE_CHUNK = MXU_DIM
N_GATE_CHUNKS = LRU_WIDTH // GATE_CHUNK
ROUTE_WIDTH = LANES

SEQ_TILE = 256
MOE_TILE = 1392
DISPATCH_TILE = 256
SLOT_CAP = 32
N_SLOTS = N_EXPERTS * SLOT_CAP
PAD_CHUNK = 128
SEG_PAD = MOE_TILE + SLOT_CAP + PAD_CHUNK
PACKED_WIDTH = D_MODEL // 2
SAMPLE_ATTN_TILE = 16
SAMPLE_PROJ_TILE = 512


def _const_spec(shape):
    nd = len(shape)
    return pl.BlockSpec(shape, lambda *_: (0,) * nd)


def _layer_norm(z, g, b):
    mu = jnp.mean(z, axis=-1, keepdims=True)
    zc = z - mu
    var = jnp.mean(zc * zc, axis=-1, keepdims=True)
    return zc * lax.rsqrt(var + LN_EPS) * g + b


def _sigmoid(x):
    return 1.0 / (1.0 + jnp.exp2(x * -LOG2_E))


def _softplus(x):
    return jnp.maximum(x, 0.0) + jnp.log1p(jnp.exp(-jnp.abs(x)))


def _lru_gates(xc, w_gates_ref, b_rg, b_ig, lam):
    xcb = xc.astype(BF16)
    r_parts, i_parts = [], []
    for c in range(N_GATE_CHUNKS):
        g = jnp.dot(xcb[:, c * GATE_CHUNK:(c + 1) * GATE_CHUNK], w_gates_ref[c], preferred_element_type=F32)
        r_parts.append(g[:, :GATE_CHUNK])
        i_parts.append(g[:, GATE_CHUNK:])
    r = _sigmoid(jnp.concatenate(r_parts, axis=1) + b_rg)
    i = _sigmoid(jnp.concatenate(i_parts, axis=1) + b_ig)
    log_a = (-LRU_C * r) * _softplus(-lam)
    a = jnp.exp(log_a)
    gain_sq = 1.0 - a * a
    gain = jnp.where(gain_sq > 0.0, gain_sq * lax.rsqrt(gain_sq), 0.0)
    u = gain * (i * xc)
    return a, u


def _linear_scan(a, u, h_in):
    n, w = a.shape
    groups = n // SUBLANES
    a3 = a.reshape(groups, SUBLANES, w)
    u3 = u.reshape(groups, SUBLANES, w)
    row = lax.broadcasted_iota(jnp.int32, a3.shape, 1)
    d = 1
    while d < SUBLANES:
        has_prev = row >= d
        u3 = u3 + a3 * jnp.where(has_prev, pltpu.roll(u3, d, axis=1), 0.0)
        a3 = a3 * jnp.where(has_prev, pltpu.roll(a3, d, axis=1), 1.0)
        d *= 2
    carry = h_in
    out = []
    for g in range(groups):
        h_g = u3[g] + a3[g] * carry
        out.append(h_g)
        carry = h_g[SUBLANES - 1:SUBLANES, :]
    return jnp.concatenate(out, axis=0)


def _route(x1, w_hi_ref, w_hilo_ref, b_rt):
    x_hi = x1.astype(BF16)
    x_lo = (x1 - x_hi.astype(F32)).astype(BF16)
    both = jnp.dot(x_hi, w_hilo_ref[...], preferred_element_type=F32)
    logits = (both[:, :ROUTE_WIDTH]
              + (jnp.dot(x_lo, w_hi_ref[...], preferred_element_type=F32) + both[:, ROUTE_WIDTH:])) + b_rt
    col = lax.broadcasted_iota(jnp.int32, logits.shape, 1)
    big = jnp.int32(ROUTE_WIDTH)
    is_g = col < N_GROUPS
    gl = jnp.where(is_g, logits, -jnp.inf)
    gmax = jnp.max(gl, axis=-1, keepdims=True)
    g_idx = jnp.min(jnp.where(gl == gmax, col, big), axis=-1, keepdims=True)
    p_g = 1.0 / jnp.sum(jnp.where(is_g, jnp.exp(gl - gmax), 0.0), axis=-1, keepdims=True)
    lo = N_GROUPS + g_idx * EXPERTS_PER_GROUP
    in_grp = (col >= lo) & (col < lo + EXPERTS_PER_GROUP)
    el = jnp.where(in_grp, logits, -jnp.inf)
    v1 = jnp.max(el, axis=-1, keepdims=True)
    i1 = jnp.min(jnp.where(el == v1, col, big), axis=-1, keepdims=True)
    el2 = jnp.where(col == i1, -jnp.inf, el)
    v2 = jnp.max(el2, axis=-1, keepdims=True)
    i2 = jnp.min(jnp.where(el2 == v2, col, big), axis=-1, keepdims=True)
    e21 = jnp.exp(v2 - v1)
    inv = 1.0 / (1.0 + e21)
    w1 = p_g * inv
    w2 = p_g * (e21 * inv)
    e1 = (i1 - N_GROUPS).astype(F32)
    e2 = (i2 - N_GROUPS).astype(F32)
    return jnp.where(col == 0, e1, jnp.where(col == 1, e2, jnp.where(col == 2, w1, jnp.where(col == 3, w2, 0.0))))


def _merge_norm(x, rec, att, g_l, g_a, w_out_ref, ln_g, ln_b):
    rec_o = jnp.dot(rec.astype(BF16), w_out_ref[0], preferred_element_type=F32)
    att_o = jnp.dot(att.astype(BF16), w_out_ref[1], preferred_element_type=F32)
    merged = _sigmoid(g_l) * rec_o + _sigmoid(g_a) * att_o
    mix = jnp.dot(merged.astype(BF16), w_out_ref[2], preferred_element_type=F32)
    return _layer_norm(ALPHA * x + mix, ln_g, ln_b)


def _mixer_kernel(sinks_ref, x_ref, w_in_ref, b_in_ref, w_conv_ref, b_conv_ref, w_gates_ref, b_rg_ref, b_ig_ref,
                  lam_ref, w_out_ref, ln_g_ref, ln_b_ref, w_rt_hi_ref, w_rt_lo_ref,
                  b_rt_ref, x1_s_ref,
                  x1_ref, route_ref, kwin_ref, vwin_ref, conv_ref, h_ref, xs_hbm, goff_ref, gtot_ref,
                  conv_buf, h_carry, kcat, vcat, att_buf, prev_x1, stage, zero_rows, tri, xb_keep, key_keep,
                  g_vmem, gvec, g_smem, nd_smem, rounds_smem, sem_stage, sem_g, sem_zero,
                  *, tiles_per_seq, n_tiles, seg_cap):
    step = pl.program_id(0)
    last_step = pl.num_programs(0) - 1
    buf = lax.rem(step, 2)
    n_sample = x1_s_ref.shape[0]
    disp = _Dispatcher(xs_hbm, goff_ref, gtot_ref, stage, zero_rows, tri, xb_keep, key_keep, g_vmem, gvec,
                       g_smem, nd_smem, rounds_smem, sem_stage, sem_g, sem_zero, seg_cap)
    n_valid = jnp.where(step == 0, 0, jnp.where(step == last_step, n_sample, SEQ_TILE))

    @pl.when(step == 0)
    def _():
        disp.init()
        prev_x1[...] = jnp.zeros_like(prev_x1)

    def route_and_plan():
        x1_prev = prev_x1[...]
        route = _route(x1_prev, w_rt_hi_ref, w_rt_lo_ref, b_rt_ref[...])
        route_ref[...] = route
        disp.plan(x1_prev, route, n_valid, buf)

    @pl.when(step < n_tiles)
    def _():
        @pl.when(lax.rem(step, tiles_per_seq) == 0)
        def _():
            conv_buf[...] = jnp.zeros_like(conv_buf)
            h_carry[...] = jnp.zeros_like(h_carry)
            kcat[0:WINDOW, :] = jnp.zeros((WINDOW, KV_WIDTH), BF16)
            vcat[0:WINDOW, :] = jnp.zeros((WINDOW, KV_WIDTH), BF16)

        route_and_plan()
        _mixer_tile(lax.rem(step, tiles_per_seq), sinks_ref, x_ref, w_in_ref, b_in_ref, w_conv_ref, b_conv_ref,
                    w_gates_ref, b_rg_ref, b_ig_ref, lam_ref, w_out_ref, ln_g_ref,
                    ln_b_ref, x1_ref, prev_x1, kwin_ref, vwin_ref, conv_ref, h_ref,
                    conv_buf, h_carry, kcat, vcat, att_buf)

    @pl.when(step >= n_tiles)
    def _():
        route_and_plan()

        @pl.when(step == n_tiles)
        def _():
            x1_ref[0:n_sample, :] = x1_s_ref[...]
            prev_x1[0:n_sample, :] = x1_s_ref[...]

    disp.flush(buf, step == last_step)


def _mixer_tile(t, sinks_ref, x_ref, w_in_ref, b_in_ref, w_conv_ref, b_conv_ref, w_gates_ref, b_rg_ref, b_ig_ref,
                lam_ref, w_out_ref, ln_g_ref, ln_b_ref,
                x1_ref, x1_keep, kwin_ref, vwin_ref, conv_ref, h_ref,
                conv_buf, h_carry, kcat, vcat, att_buf):
    T = SEQ_TILE
    x = x_ref[0]
    xb = x.astype(BF16)

    def proj(lo, width):
        return jnp.dot(xb, w_in_ref[:, lo:lo + width], preferred_element_type=F32) + b_in_ref[:, lo:lo + width]

    xl = proj(OFF_XL, LRU_WIDTH)
    xl_ext = jnp.concatenate([conv_buf[...], xl], axis=0)

    def lagged(k):
        return pltpu.roll(xl_ext, k, axis=0)[SUBLANES:, :]
    wc = w_conv_ref[...]
    xc = wc[0:1] * lagged(3)
    xc = xc + wc[1:2] * lagged(2)
    xc = xc + wc[2:3] * lagged(1)
    xc = xc + wc[3:4] * xl + b_conv_ref[...]
    conv_ref[0] = xl[T - (CONV_W - 1):, :]
    conv_buf[...] = xl[T - SUBLANES:, :]

    a, u = _lru_gates(xc, w_gates_ref, b_rg_ref[...], b_ig_ref[...], lam_ref[...])
    h = _linear_scan(a, u, h_carry[0:1, :])
    h_last = h[T - 1:T, :]
    h_carry[0:1, :] = h_last
    h_ref[0] = h_last
    rec = h * jax.nn.gelu(proj(OFF_YL, LRU_WIDTH))

    q = proj(OFF_Q, N_HEADS * HEAD_DIM) * (ATTN_SCALE * LOG2_E)
    k = proj(OFF_K, KV_WIDTH)
    v = proj(OFF_V, KV_WIDTH)
    kwin_ref[0] = k[T - WINDOW:, :]
    vwin_ref[0] = v[T - WINDOW:, :]
    kcat[WINDOW:WINDOW + T, :] = k.astype(BF16)
    vcat[WINDOW:WINDOW + T, :] = v.astype(BF16)

    qi = lax.broadcasted_iota(jnp.int32, (WINDOW, 2 * WINDOW), 0)
    kj = lax.broadcasted_iota(jnp.int32, (WINDOW, 2 * WINDOW), 1)
    band = (kj > qi) & (kj <= qi + WINDOW)
    grp_row = lax.broadcasted_iota(jnp.int32, (GROUP * WINDOW, 1), 0) // WINDOW
    for qb in range(T // WINDOW):
        if qb == 0:
            first_key = jnp.where(t == 0, WINDOW, 0)
            mask1 = band & (kj >= first_key)
        else:
            mask1 = band
        bias = jnp.concatenate([jnp.where(mask1, 0.0, NEG_INF)] * GROUP, axis=0)
        r0 = qb * WINDOW
        qq = q[r0:r0 + WINDOW, :]
        for j in range(N_KV):
            kjb = kcat[r0:r0 + 2 * WINDOW, j * HEAD_DIM:(j + 1) * HEAD_DIM]
            vjb = vcat[r0:r0 + 2 * WINDOW, j * HEAD_DIM:(j + 1) * HEAD_DIM]
            qs = jnp.concatenate(
                [qq[:, (j * GROUP + g) * HEAD_DIM:(j * GROUP + g + 1) * HEAD_DIM] for g in range(GROUP)], axis=0)
            s = lax.dot_general(qs.astype(BF16), kjb, (((1,), (1,)), ((), ())), preferred_element_type=F32) + bias
            sink = jnp.zeros((GROUP * WINDOW, 1), F32)
            for g in range(GROUP):
                sink = jnp.where(grp_row == g, sinks_ref[j * GROUP + g] * LOG2_E, sink)
            m = jnp.maximum(jnp.max(s, axis=-1, keepdims=True), sink)
            p = jnp.exp2(s - m)
            inv = 1.0 / (jnp.sum(p, axis=-1, keepdims=True) + jnp.exp2(sink - m))
            o = jnp.dot((p * inv).astype(BF16), vjb, preferred_element_type=F32)
            for g in range(GROUP):
                hcol = (j * GROUP + g) * HEAD_DIM
                att_buf[r0:r0 + WINDOW, hcol:hcol + HEAD_DIM] = o[g * WINDOW:(g + 1) * WINDOW, :]
    kcat[0:WINDOW, :] = kcat[T:T + WINDOW, :]
    vcat[0:WINDOW, :] = vcat[T:T + WINDOW, :]

    x1 = _merge_norm(x, rec, att_buf[...], proj(OFF_GL, D_MODEL), proj(OFF_GA, D_MODEL),
                     w_out_ref, ln_g_ref[...], ln_b_ref[...])
    x1_ref[...] = x1
    x1_keep[...] = x1


def _mixer_call(x_prompt, x1_s, sinks, wts):
    B, S, _ = x_prompt.shape
    T = SEQ_TILE
    assert T == DISPATCH_TILE and x1_s.shape[0] <= T
    nt = S // T
    n_tiles = B * nt
    n_rows_total = B * S + x1_s.shape[0]
    n_dispatch = _dispatch_tiles(n_rows_total)
    assert n_dispatch == n_tiles + 1
    seg_cap = _segment_capacity(n_rows_total)
    n_assign = TOP_K * T
    weight_args = (wts["w_in"], wts["b_in"], wts["w_conv"], wts["b_conv"], wts["w_gates"], wts["b_rg"], wts["b_ig"],
                   wts["lam"], wts["w_out"], wts["ln1_g"], wts["ln1_b"],
                   wts["w_rt_hi"], wts["w_rt_lo"], wts["b_rt"], x1_s)
    mixed = lambda i: jnp.minimum(i, n_tiles - 1)
    seq = lambda i: mixed(i) // nt
    routed = lambda i: jnp.clip(i - 1, 0, n_dispatch - 1)
    in_specs = [pl.BlockSpec(memory_space=pltpu.SMEM),
                pl.BlockSpec((1, T, D_MODEL), lambda i: (seq(i), lax.rem(mixed(i), nt), 0))]
    in_specs += [_const_spec(w.shape) for w in weight_args]
    out_shape = (
        jax.ShapeDtypeStruct((n_rows_total, D_MODEL), F32),
        jax.ShapeDtypeStruct((n_rows_total, ROUTE_WIDTH), F32),
        jax.ShapeDtypeStruct((B, WINDOW, KV_WIDTH), F32),
        jax.ShapeDtypeStruct((B, WINDOW, KV_WIDTH), F32),
        jax.ShapeDtypeStruct((B, CONV_W - 1, LRU_WIDTH), F32),
        jax.ShapeDtypeStruct((B, 1, LRU_WIDTH), F32),
        jax.ShapeDtypeStruct((N_EXPERTS * seg_cap, PACKED_WIDTH), jnp.uint32),
        jax.ShapeDtypeStruct((n_dispatch, 1, ROUTE_WIDTH), jnp.int32),
        jax.ShapeDtypeStruct((1, ROUTE_WIDTH), jnp.int32),
    )
    out_specs = (
        pl.BlockSpec((T, D_MODEL), lambda i: (jnp.minimum(i, n_tiles), 0)),
        pl.BlockSpec((T, ROUTE_WIDTH), lambda i: (routed(i), 0)),
        pl.BlockSpec((1, WINDOW, KV_WIDTH), lambda i: (seq(i), 0, 0)),
        pl.BlockSpec((1, WINDOW, KV_WIDTH), lambda i: (seq(i), 0, 0)),
        pl.BlockSpec((1, CONV_W - 1, LRU_WIDTH), lambda i: (seq(i), 0, 0)),
        pl.BlockSpec((1, 1, LRU_WIDTH), lambda i: (seq(i), 0, 0)),
        pl.BlockSpec(memory_space=pl.ANY),
        pl.BlockSpec((1, 1, ROUTE_WIDTH), lambda i: (routed(i), 0, 0)),
        pl.BlockSpec((1, ROUTE_WIDTH), lambda i: (0, 0)),
    )
    scratch = [
        pltpu.VMEM((SUBLANES, LRU_WIDTH), F32),
        pltpu.VMEM((SUBLANES, LRU_WIDTH), F32),
        pltpu.VMEM((T + WINDOW, KV_WIDTH), BF16),
        pltpu.VMEM((T + WINDOW, KV_WIDTH), BF16),
        pltpu.VMEM((T, N_HEADS * HEAD_DIM), F32),
        pltpu.VMEM((T, D_MODEL), F32),
        pltpu.VMEM((2, N_SLOTS, PACKED_WIDTH), jnp.uint32),
        pltpu.VMEM((PAD_CHUNK, PACKED_WIDTH), jnp.uint32),
        pltpu.VMEM((n_assign, n_assign), BF16),
        pltpu.VMEM((T, D_MODEL), BF16),
        pltpu.VMEM((SUBLANES, n_assign), F32),
        pltpu.VMEM((SUBLANES, ROUTE_WIDTH), jnp.int32),
        pltpu.VMEM((SUBLANES, ROUTE_WIDTH), F32),
        pltpu.SMEM((1, ROUTE_WIDTH), jnp.int32),
        pltpu.SMEM((1,), jnp.int32),
        pltpu.SMEM((1,), jnp.int32),
        pltpu.SemaphoreType.DMA(()),
        pltpu.SemaphoreType.DMA(()),
        pltpu.SemaphoreType.DMA(()),
    ]
    return pl.pallas_call(
        functools.partial(_mixer_kernel, tiles_per_seq=nt, n_tiles=n_tiles, seg_cap=seg_cap),
        grid=(n_tiles + 2,),
        in_specs=in_specs,
        out_specs=out_specs,
        out_shape=out_shape,
        scratch_shapes=scratch,
        compiler_params=pltpu.CompilerParams(dimension_semantics=("arbitrary",),
                                             vmem_limit_bytes=VMEM_LIMIT_BYTES),
        name="mixer_prompt",
    )(sinks, x_prompt, *weight_args)


def _sample_proj_kernel(x_ref, w_ref, b_ref, u_ref):
    u_ref[...] = jnp.dot(x_ref[...].astype(BF16), w_ref[...], preferred_element_type=F32) + b_ref[...]


def _sample_proj_call(x_s, w_in, b_in):
    n = x_s.shape[0]
    tn = SAMPLE_PROJ_TILE
    return pl.pallas_call(
        _sample_proj_kernel,
        grid=(IN_WIDTH // tn,),
        in_specs=[pl.BlockSpec((n, D_MODEL), lambda c: (0, 0)),
                  pl.BlockSpec((D_MODEL, tn), lambda c: (0, c)),
                  pl.BlockSpec((1, tn), lambda c: (0, c))],
        out_specs=pl.BlockSpec((n, tn), lambda c: (0, c)),
        out_shape=jax.ShapeDtypeStruct((n, IN_WIDTH), F32),
        compiler_params=pltpu.CompilerParams(dimension_semantics=("arbitrary",)),
        name="sample_proj",
    )(x_s, w_in, b_in)


def _sample_attn_kernel(q_ref, kn_ref, vn_ref, kn_row_ref, vn_row_ref, ck_ref, cv_ref, sinks_ref,
                        att_ref, kwin_ref, vwin_ref):
    tb = q_ref.shape[0]
    key_pos = lax.broadcasted_iota(jnp.int32, (tb, GROUP, WINDOW), 2)
    for j in range(N_KV):
        qj = q_ref[:, j * GROUP:(j + 1) * GROUP, :]
        kc = ck_ref[:, :, j * HEAD_DIM:(j + 1) * HEAD_DIM]
        vc = cv_ref[:, :, j * HEAD_DIM:(j + 1) * HEAD_DIM]
        s_c = jnp.einsum("bgd,bsd->bgs", qj.astype(BF16), kc.astype(BF16), preferred_element_type=F32) * ATTN_SCALE
        s_c = jnp.where(key_pos >= 1, s_c, NEG_INF)
        kn = kn_ref[:, j:j + 1, :]
        vn = vn_ref[:, j:j + 1, :]
        s_n = jnp.sum(qj * kn, axis=-1, keepdims=True) * ATTN_SCALE
        sink = sinks_ref[j][None]
        m = jnp.maximum(jnp.maximum(jnp.max(s_c, axis=-1, keepdims=True), s_n), sink)
        p_c = jnp.exp(s_c - m)
        p_n = jnp.exp(s_n - m)
        inv = 1.0 / (jnp.sum(p_c, axis=-1, keepdims=True) + p_n + jnp.exp(sink - m))
        o = jnp.einsum("bgs,bsd->bgd", (p_c * inv).astype(BF16), vc.astype(BF16), preferred_element_type=F32)
        att_ref[:, j * GROUP:(j + 1) * GROUP, :] = o + (p_n * inv) * vn
    kwin_ref[:, 0:WINDOW - 1, :] = ck_ref[:, 1:WINDOW, :]
    kwin_ref[:, WINDOW - 1:WINDOW, :] = kn_row_ref[...]
    vwin_ref[:, 0:WINDOW - 1, :] = cv_ref[:, 1:WINDOW, :]
    vwin_ref[:, WINDOW - 1:WINDOW, :] = vn_row_ref[...]


def _sample_attn_call(q3, kn3, vn3, kn_row, vn_row, ck, cv, sinks3):
    n = q3.shape[0]
    tb = SAMPLE_ATTN_TILE
    b3 = lambda i: (i, 0, 0)
    return pl.pallas_call(
        _sample_attn_kernel,
        grid=(n // tb,),
        in_specs=[pl.BlockSpec((tb, N_HEADS, HEAD_DIM), b3),
                  pl.BlockSpec((tb, N_KV, HEAD_DIM), b3),
                  pl.BlockSpec((tb, N_KV, HEAD_DIM), b3),
                  pl.BlockSpec((tb, 1, KV_WIDTH), b3),
                  pl.BlockSpec((tb, 1, KV_WIDTH), b3),
                  pl.BlockSpec((tb, WINDOW, KV_WIDTH), b3),
                  pl.BlockSpec((tb, WINDOW, KV_WIDTH), b3),
                  pl.BlockSpec((N_KV, GROUP, 1), lambda i: (0, 0, 0))],
        out_specs=(pl.BlockSpec((tb, N_HEADS, HEAD_DIM), b3),
                   pl.BlockSpec((tb, WINDOW, KV_WIDTH), b3),
                   pl.BlockSpec((tb, WINDOW, KV_WIDTH), b3)),
        out_shape=(jax.ShapeDtypeStruct((n, N_HEADS, HEAD_DIM), F32),
                   jax.ShapeDtypeStruct((n, WINDOW, KV_WIDTH), F32),
                   jax.ShapeDtypeStruct((n, WINDOW, KV_WIDTH), F32)),
        compiler_params=pltpu.CompilerParams(dimension_semantics=("arbitrary",)),
        name="sample_attn",
    )(q3, kn3, vn3, kn_row, vn_row, ck, cv, sinks3)


def _sample_mix_kernel(x_ref, u_ref, att_ref, st_ref, h0_ref, w_conv_ref, b_conv_ref, w_gates_ref, b_rg_ref,
                       b_ig_ref, lam_ref, w_out_ref, ln_g_ref, ln_b_ref,
                       x1_ref, conv_ref, h_ref):
    xl = u_ref[:, OFF_XL:OFF_XL + LRU_WIDTH]
    wc = w_conv_ref[...]
    xc = wc[0:1] * st_ref[0]
    xc = xc + wc[1:2] * st_ref[1]
    xc = xc + wc[2:3] * st_ref[2]
    xc = xc + wc[3:4] * xl + b_conv_ref[...]
    conv_ref[0] = st_ref[1]
    conv_ref[1] = st_ref[2]
    conv_ref[2] = xl
    a, u = _lru_gates(xc, w_gates_ref, b_rg_ref[...], b_ig_ref[...], lam_ref[...])
    h = a * h0_ref[...] + u
    h_ref[...] = h
    rec = h * jax.nn.gelu(u_ref[:, OFF_YL:OFF_YL + LRU_WIDTH])
    x1_ref[...] = _merge_norm(x_ref[...], rec, att_ref[...], u_ref[:, OFF_GL:OFF_GL + D_MODEL],
                              u_ref[:, OFF_GA:OFF_GA + D_MODEL], w_out_ref,
                              ln_g_ref[...], ln_b_ref[...])


def _sample_mix_call(x_s, u_s, att, st_t, h0, wts):
    n = x_s.shape[0]
    weight_args = (wts["w_conv"], wts["b_conv"], wts["w_gates"], wts["b_rg"], wts["b_ig"], wts["lam"],
                   wts["w_out"], wts["ln1_g"], wts["ln1_b"])
    args = (x_s, u_s, att, st_t, h0) + weight_args
    out_shapes = ((n, D_MODEL), (CONV_W - 1, n, LRU_WIDTH), (n, LRU_WIDTH))
    return pl.pallas_call(
        _sample_mix_kernel,
        grid=(1,),
        in_specs=[_const_spec(a.shape) for a in args],
        out_specs=tuple(_const_spec(s) for s in out_shapes),
        out_shape=tuple(jax.ShapeDtypeStruct(s, F32) for s in out_shapes),
        compiler_params=pltpu.CompilerParams(dimension_semantics=("arbitrary",),
                                             vmem_limit_bytes=VMEM_LIMIT_BYTES),
        name="sample_mix",
    )(*args)


def _one_hot(mask):
    return jnp.where(mask, 1.0, 0.0).astype(BF16)


def _pack_rows(x):
    half = x.shape[1] // 2
    lo = lax.shift_right_logical(lax.bitcast_convert_type(x[:, :half], jnp.uint32), jnp.uint32(16))
    hi = lax.bitcast_convert_type(x[:, half:], jnp.uint32) & jnp.uint32(0xFFFF0000)
    return lo | hi


def _unpack_rows(words):
    lo = lax.bitcast_convert_type(lax.shift_left(words, jnp.uint32(16)), F32)
    hi = lax.bitcast_convert_type(words & jnp.uint32(0xFFFF0000), F32)
    return jnp.concatenate([lo.astype(BF16), hi.astype(BF16)], axis=1)


class _Dispatcher:
    def __init__(self, xs_hbm, goff_ref, gtot_ref, stage, zero_rows, tri, xb_keep, key_keep, g_vmem, gvec,
                 g_smem, nd_smem, rounds_smem, sem_stage, sem_g, sem_zero, seg_cap):
        self.xs_hbm, self.goff_ref, self.gtot_ref = xs_hbm, goff_ref, gtot_ref
        self.stage, self.zero_rows, self.tri = stage, zero_rows, tri
        self.xb_keep, self.key_keep = xb_keep, key_keep
        self.g_vmem, self.gvec, self.g_smem = g_vmem, gvec, g_smem
        self.nd_smem, self.rounds_smem = nd_smem, rounds_smem
        self.sem_stage, self.sem_g, self.sem_zero = sem_stage, sem_g, sem_zero
        self.seg_cap = seg_cap

    def _g_copy(self):
        return pltpu.make_async_copy(self.g_vmem.at[pl.ds(0, 1)], self.g_smem, self.sem_g)

    def init(self):
        n_assign = self.tri.shape[0]
        r = lax.broadcasted_iota(jnp.int32, (n_assign, n_assign), 0)
        c = lax.broadcasted_iota(jnp.int32, (n_assign, n_assign), 1)
        self.tri[...] = _one_hot(r < c)
        self.gvec[...] = jnp.zeros_like(self.gvec)
        self.g_vmem[...] = jnp.zeros_like(self.g_vmem)
        self.nd_smem[0] = 0
        self._g_copy().start()

    def _sorted_rows(self, e_row, rank_row, xb, lo):
        td = xb.shape[0]
        slot = lax.broadcasted_iota(jnp.int32, (N_SLOTS, td), 0).astype(F32)
        lo_f = lo.astype(F32)
        in_round = (rank_row >= lo_f) & (rank_row < lo_f + SLOT_CAP) & (e_row >= 0.0)
        key = jnp.where(in_round, e_row * SLOT_CAP + (rank_row - lo_f), -1.0)
        perm = jnp.where(slot == key[:, :td], 1.0, jnp.where(slot == key[:, td:], 1.0, 0.0)).astype(BF16)
        return _pack_rows(jnp.dot(perm, xb, preferred_element_type=F32))

    def plan(self, x1, route, n_valid, buf):
        td = x1.shape[0]
        n_assign = TOP_K * td
        valid_col = lax.broadcasted_iota(jnp.int32, (td, 1), 0) < n_valid
        lane = lax.broadcasted_iota(jnp.int32, (td, ROUTE_WIDTH), 1).astype(F32)
        e0c = jnp.where(valid_col, route[:, 0:1], -1.0)
        e1c = jnp.where(valid_col, route[:, 1:2], -1.0)
        cnt_row = jnp.sum(jnp.where(lane == e0c, 1.0, 0.0) + jnp.where(lane == e1c, 1.0, 0.0),
                          axis=0, keepdims=True)
        self.rounds_smem[0] = (jnp.max(cnt_row).astype(jnp.int32) + SLOT_CAP - 1) // SLOT_CAP
        self.goff_ref[0] = self.gvec[0:1, :].astype(jnp.int32)
        self.gvec[0:1, :] = self.gvec[0:1, :] + jnp.ceil(cnt_row * (1.0 / SUBLANES)) * SUBLANES
        self.gtot_ref[...] = self.gvec[0:1, :].astype(jnp.int32)

        route_t = route.T
        valid_row = lax.broadcasted_iota(jnp.int32, (1, td), 1) < n_valid
        e_row = jnp.concatenate([jnp.where(valid_row, route_t[0:1, :], -1.0),
                                 jnp.where(valid_row, route_t[1:2, :], -1.0)], axis=1)
        expert_sub = lax.broadcasted_iota(jnp.int32, (N_EXPERTS, n_assign), 0).astype(F32)
        onehot_t = expert_sub == e_row
        rank_t = jnp.dot(_one_hot(onehot_t), self.tri[...], preferred_element_type=F32)
        rank_row = jnp.sum(jnp.where(onehot_t, rank_t, 0.0), axis=0, keepdims=True)
        xb = jnp.where(valid_col, x1, 0.0).astype(BF16)
        self.xb_keep[...] = xb
        self.key_keep[0:1, :] = e_row
        self.key_keep[1:2, :] = rank_row
        self.stage[buf] = self._sorted_rows(e_row, rank_row, xb, jnp.int32(0))

    def _stage_copy(self, buf, e, dst_row):
        src = self.stage.at[buf, pl.ds(e * SLOT_CAP, SLOT_CAP)]
        return pltpu.make_async_copy(src, self.xs_hbm.at[pl.ds(dst_row, SLOT_CAP)], self.sem_stage)

    def _wait_outstanding(self):
        @pl.when(self.nd_smem[0] > 0)
        def _():
            for _ in range(N_EXPERTS):
                self._stage_copy(0, 0, 0).wait()
        self.nd_smem[0] = 0

    def _start_round(self, buf, lo):
        for e in range(N_EXPERTS):
            dst_row = pl.multiple_of(e * self.seg_cap + self.g_smem[0, e] + lo, SUBLANES)
            self._stage_copy(buf, e, dst_row).start()
        self.nd_smem[0] = N_EXPERTS

    def flush(self, buf, is_last):
        self._g_copy().wait()
        self._wait_outstanding()
        self._start_round(buf, 0)

        def later_round(r, carry):
            lo = r * SLOT_CAP
            rows = self._sorted_rows(self.key_keep[0:1, :], self.key_keep[1:2, :], self.xb_keep[...], lo)
            self._wait_outstanding()
            self.stage[buf] = rows
            self._start_round(buf, lo)
            return carry
        lax.fori_loop(1, self.rounds_smem[0], later_round, 0)

        self.g_vmem[0:1, :] = self.gvec[0:1, :].astype(jnp.int32)
        self._g_copy().start()

        @pl.when(is_last)
        def _():
            self._g_copy().wait()
            self._wait_outstanding()
            self.zero_rows[...] = jnp.zeros_like(self.zero_rows)

            def pad_copy(dst_row):
                dst = self.xs_hbm.at[pl.ds(pl.multiple_of(dst_row, SUBLANES), PAD_CHUNK)]
                return pltpu.make_async_copy(self.zero_rows, dst, self.sem_zero)

            def pad_segment(e, n_started):
                fill = self.g_smem[0, e]
                block_end = (fill + SLOT_CAP + MOE_TILE - 1) // MOE_TILE * MOE_TILE
                n_chunks = (block_end - fill + PAD_CHUNK - 1) // PAD_CHUNK

                def start(c, carry):
                    pad_copy(e * self.seg_cap + fill + c * PAD_CHUNK).start()
                    return carry
                lax.fori_loop(0, n_chunks, start, 0)
                return n_started + n_chunks
            n_started = lax.fori_loop(0, N_EXPERTS, pad_segment, 0)

            def wait(_, carry):
                pad_copy(0).wait()
                return carry
            lax.fori_loop(0, n_started, wait, 0)


def _dispatch_tiles(n_tokens):
    return (n_tokens + DISPATCH_TILE - 1) // DISPATCH_TILE


def _segment_capacity(n_tokens):
    assert SEG_PAD >= DISPATCH_TILE
    alignment_slack = (SUBLANES - 1) * _dispatch_tiles(n_tokens)
    return (n_tokens + alignment_slack + SEG_PAD + MOE_TILE - 1) // MOE_TILE * MOE_TILE


def _moe_kernel(blk_e_ref, blk_j_ref, n_used_ref, xs_ref, wg_ref, wu_ref, wd_ref, ys_ref, wg_bf, wu_bf, wd_bf):
    del blk_j_ref
    b = pl.program_id(0)

    @pl.when(b < n_used_ref[0])
    def _():
        prev_e = blk_e_ref[jnp.maximum(b - 1, 0)]

        @pl.when((b == 0) | (blk_e_ref[b] != prev_e))
        def _():
            wg_bf[...] = wg_ref[0].astype(BF16)
            wu_bf[...] = wu_ref[0].astype(BF16)
            wd_bf[...] = wd_ref[0].astype(BF16)

        xb = _unpack_rows(xs_ref[...])
        g = jnp.dot(xb, wg_bf[...], preferred_element_type=F32)
        u = jnp.dot(xb, wu_bf[...], preferred_element_type=F32)
        hmid = (g * _sigmoid(g)) * u
        y = jnp.dot(hmid.astype(BF16), wd_bf[...], preferred_element_type=F32)
        ys_ref[...] = _pack_rows(y.astype(BF16).astype(F32))


def _expert_blocks(gtot, n_blocks):
    rows = gtot[0, :N_EXPERTS]
    nb = (rows + SLOT_CAP + MOE_TILE - 1) // MOE_TILE
    ends = jnp.cumsum(nb)
    n_used = ends[-1]
    step = jnp.minimum(jnp.arange(n_blocks, dtype=jnp.int32), n_used - 1)
    blk_e = jnp.sum((step[:, None] >= ends[None, :]).astype(jnp.int32), axis=1)
    blk_j = step - (ends - nb)[blk_e]
    return blk_e, blk_j.astype(jnp.int32), n_used.reshape(1).astype(jnp.int32)


def _moe_call(gtot, xs, w_gate, w_up, w_down, n_tokens):
    seg_cap = _segment_capacity(n_tokens)
    seg_blocks = seg_cap // MOE_TILE
    max_rows = n_tokens * TOP_K + N_EXPERTS * (SUBLANES - 1) * _dispatch_tiles(n_tokens)
    n_blocks = (max_rows + N_EXPERTS * (SLOT_CAP + MOE_TILE - 1)) // MOE_TILE
    blk_e, blk_j, n_used = _expert_blocks(gtot, n_blocks)
    row_block = lambda b, be, bj, nu: (be[b] * seg_blocks + bj[b], 0)
    weight_block = lambda b, be, bj, nu: (be[b], 0, 0)
    grid_spec = pltpu.PrefetchScalarGridSpec(
        num_scalar_prefetch=3,
        grid=(n_blocks,),
        in_specs=[
            pl.BlockSpec((MOE_TILE, PACKED_WIDTH), row_block),
            pl.BlockSpec((1, D_MODEL, EXPERT_FF), weight_block),
            pl.BlockSpec((1, D_MODEL, EXPERT_FF), weight_block),
            pl.BlockSpec((1, EXPERT_FF, D_MODEL), weight_block),
        ],
        out_specs=pl.BlockSpec((MOE_TILE, PACKED_WIDTH), row_block),
        scratch_shapes=[
            pltpu.VMEM((D_MODEL, EXPERT_FF), BF16),
            pltpu.VMEM((D_MODEL, EXPERT_FF), BF16),
            pltpu.VMEM((EXPERT_FF, D_MODEL), BF16),
        ],
    )
    return pl.pallas_call(
        _moe_kernel,
        grid_spec=grid_spec,
        out_shape=jax.ShapeDtypeStruct(xs.shape, xs.dtype),
        compiler_params=pltpu.CompilerParams(dimension_semantics=("arbitrary",),
                                             vmem_limit_bytes=VMEM_LIMIT_BYTES),
        name="moe_experts",
    )(blk_e, blk_j, n_used, xs, w_gate, w_up, w_down)


def _combine_kernel(gcur_ref, gnext_ref, x1_ref, route_ref, ys_hbm, ln_g_ref, ln_b_ref, y_ref,
                    stage, acc, tri, sem, *, seg_cap):
    td = x1_ref.shape[0]
    n_assign = TOP_K * td
    i = pl.program_id(0)
    buf = lax.rem(i, 2)

    def stage_copy(g_ref, e, first_rank, to_buf):
        src_row = pl.multiple_of(e * seg_cap + g_ref[0, 0, e] + first_rank, SUBLANES)
        slot0 = e * SLOT_CAP
        if not isinstance(slot0, int):
            slot0 = pl.multiple_of(slot0, SLOT_CAP)
        dst = stage.at[to_buf, pl.ds(slot0, SLOT_CAP)]
        return pltpu.make_async_copy(ys_hbm.at[pl.ds(src_row, SLOT_CAP)], dst, sem.at[to_buf])

    def fetch_first_round(g_ref, to_buf):
        for e in range(N_EXPERTS):
            stage_copy(g_ref, e, 0, to_buf).start()

    @pl.when(i == 0)
    def _():
        r = lax.broadcasted_iota(jnp.int32, (n_assign, n_assign), 0)
        c = lax.broadcasted_iota(jnp.int32, (n_assign, n_assign), 1)
        tri[...] = _one_hot(c < r)
        fetch_first_round(gcur_ref, 0)

    def count(e):
        return gnext_ref[0, 0, e] - gcur_ref[0, 0, e]

    route = route_ref[...]
    e0, e1, w0, w1 = route[:, 0:1], route[:, 1:2], route[:, 2:3], route[:, 3:4]
    lane = lax.broadcasted_iota(jnp.int32, (td, ROUTE_WIDTH), 1).astype(F32)
    onehot = jnp.concatenate([lane == e0, lane == e1], axis=0)
    onehot_f = jnp.where(onehot, 1.0, 0.0)
    rank_mat = jnp.dot(tri[...], onehot_f.astype(BF16), preferred_element_type=F32)
    rank = jnp.sum(jnp.where(onehot, rank_mat, 0.0), axis=1, keepdims=True)
    r0, r1 = rank[:td], rank[td:]
    cnt_row = jnp.sum(onehot_f, axis=0, keepdims=True)
    rounds = (jnp.max(cnt_row).astype(jnp.int32) + SLOT_CAP - 1) // SLOT_CAP

    slot_lane = lax.broadcasted_iota(jnp.int32, (td, N_SLOTS), 1).astype(F32)

    def wait_copies(n):
        def wait_one(_, c):
            stage_copy(gcur_ref, 0, 0, buf).wait()
            return c
        lax.fori_loop(0, n, wait_one, 0)

    def weighted_rows(lo):
        lo_f = lo.astype(F32)
        rows = _unpack_rows(stage[buf])

        def selector(e_col, r_col, w_col):
            in_round = (r_col >= lo_f) & (r_col < lo_f + SLOT_CAP)
            key = jnp.where(in_round, e_col * SLOT_CAP + (r_col - lo_f), -1.0)
            return jnp.where(slot_lane == key, w_col, 0.0)
        mix = (selector(e0, r0, w0) + selector(e1, r1, w1)).astype(BF16)
        return jnp.dot(mix, rows, preferred_element_type=F32)

    for _ in range(N_EXPERTS):
        stage_copy(gcur_ref, 0, 0, buf).wait()

    @pl.when(i + 1 < pl.num_programs(0))
    def _():
        fetch_first_round(gnext_ref, 1 - buf)

    acc[...] = weighted_rows(jnp.int32(0))

    def later_round(r, carry):
        lo = r * SLOT_CAP

        def start(e, n_started):
            has_rows = count(e) > lo

            @pl.when(has_rows)
            def _():
                stage_copy(gcur_ref, e, lo, buf).start()
            return n_started + has_rows.astype(jnp.int32)
        wait_copies(lax.fori_loop(0, N_EXPERTS, start, 0))
        acc[...] = acc[...] + weighted_rows(lo)
        return carry
    lax.fori_loop(1, rounds, later_round, 0)

    y_ref[...] = _layer_norm(ALPHA * x1_ref[...] + acc[...], ln_g_ref[...], ln_b_ref[...])


def _combine_call(goff, x1_all, route_all, ys, ln_g, ln_b, first_row, n_rows, tile, n_tokens):
    assert first_row % DISPATCH_TILE == 0 and first_row % tile == 0
    assert tile == DISPATCH_TILE or n_rows == tile
    first_block = first_row // tile
    first_goff = first_row // DISPATCH_TILE
    n_assign = TOP_K * tile
    grid_spec = pl.GridSpec(
        grid=(n_rows // tile,),
        in_specs=[
            pl.BlockSpec((1, 1, ROUTE_WIDTH), lambda i: (first_goff + i, 0, 0), memory_space=pltpu.SMEM),
            pl.BlockSpec((1, 1, ROUTE_WIDTH), lambda i: (first_goff + i + 1, 0, 0), memory_space=pltpu.SMEM),
            pl.BlockSpec((tile, D_MODEL), lambda i: (first_block + i, 0)),
            pl.BlockSpec((tile, ROUTE_WIDTH), lambda i: (first_block + i, 0)),
            pl.BlockSpec(memory_space=pl.ANY),
            _const_spec(ln_g.shape),
            _const_spec(ln_b.shape),
        ],
        out_specs=pl.BlockSpec((tile, D_MODEL), lambda i: (i, 0)),
        scratch_shapes=[pltpu.VMEM((2, N_SLOTS, PACKED_WIDTH), jnp.uint32),
                        pltpu.VMEM((tile, D_MODEL), F32),
                        pltpu.VMEM((n_assign, n_assign), BF16),
                        pltpu.SemaphoreType.DMA((2,))],
    )
    return pl.pallas_call(
        functools.partial(_combine_kernel, seg_cap=_segment_capacity(n_tokens)),
        grid_spec=grid_spec,
        out_shape=jax.ShapeDtypeStruct((n_rows, D_MODEL), F32),
        compiler_params=pltpu.CompilerParams(dimension_semantics=("arbitrary",),
                                             vmem_limit_bytes=VMEM_LIMIT_BYTES),
        name="moe_combine",
    )(goff, goff, x1_all, route_all, ys, ln_g, ln_b)


def _prepare_weights(w_in, b_in, w_conv, b_conv, w_rg, b_rg, w_ig, b_ig, lru_lambda, w_lru_out, w_attn_out, w_o,
                     ln1_g, ln1_b, w_group, b_group, w_router, b_router):
    blocks_per_chunk = GATE_CHUNK // LRU_BLOCK

    def chunked_block_diag(w):
        w = w.reshape(N_GATE_CHUNKS, blocks_per_chunk, LRU_BLOCK, LRU_BLOCK)
        eye = jnp.eye(blocks_per_chunk, dtype=w.dtype)
        return jnp.einsum("cbij,bd->cbidj", w, eye).reshape(N_GATE_CHUNKS, GATE_CHUNK, GATE_CHUNK)

    w_gates = jnp.concatenate([chunked_block_diag(w_rg), chunked_block_diag(w_ig)], axis=-1).astype(BF16)
    w_rt = jnp.concatenate([w_group, w_router], axis=1)
    w_rt = jnp.pad(w_rt, ((0, 0), (0, ROUTE_WIDTH - w_rt.shape[1])))
    w_rt_hi = w_rt.astype(BF16)
    w_rt_lo = jnp.concatenate([w_rt_hi, (w_rt - w_rt_hi.astype(F32)).astype(BF16)], axis=1)
    b_rt = jnp.pad(jnp.concatenate([b_group, b_router]), (0, ROUTE_WIDTH - N_GROUPS - N_EXPERTS))
    row = lambda v: v.reshape(1, -1)
    return dict(
        w_in=w_in.astype(BF16), b_in=row(b_in), w_conv=w_conv, b_conv=row(b_conv), w_gates=w_gates,
        b_rg=row(b_rg), b_ig=row(b_ig), lam=row(lru_lambda),
        w_out=jnp.stack([w_lru_out, w_attn_out, w_o]).astype(BF16),
        ln1_g=row(ln1_g), ln1_b=row(ln1_b), w_rt_hi=w_rt_hi, w_rt_lo=w_rt_lo, b_rt=row(b_rt))


def kernel(x_prompt, x_sample, cache_k, cache_v, state_conv, state_lru_h, w_in, b_in, w_conv, b_conv, w_rg, b_rg,
           w_ig, b_ig, lru_lambda, sinks, w_lru_out, w_attn_out, w_o, ln1_g, ln1_b, w_group, b_group, w_router,
           b_router, w_gate, w_up, w_down, ln2_g, ln2_b):
    B, S, _ = x_prompt.shape
    n_prompt = B * S
    n_sample = x_sample.shape[0]
    n_all = n_prompt + n_sample
    wts = _prepare_weights(w_in, b_in, w_conv, b_conv, w_rg, b_rg, w_ig, b_ig, lru_lambda, w_lru_out, w_attn_out,
                           w_o, ln1_g, ln1_b, w_group, b_group, w_router, b_router)

    x_s = x_sample.reshape(n_sample, D_MODEL)
    u_s = _sample_proj_call(x_s, wts["w_in"], wts["b_in"])
    q3 = u_s[:, OFF_Q:OFF_K].reshape(n_sample, N_HEADS, HEAD_DIM)
    k_new = u_s[:, OFF_K:OFF_V]
    v_new = u_s[:, OFF_V:OFF_GL]
    att3, k_win_s, v_win_s = _sample_attn_call(
        q3, k_new.reshape(n_sample, N_KV, HEAD_DIM), v_new.reshape(n_sample, N_KV, HEAD_DIM),
        k_new.reshape(n_sample, 1, KV_WIDTH), v_new.reshape(n_sample, 1, KV_WIDTH),
        cache_k.reshape(n_sample, WINDOW, KV_WIDTH), cache_v.reshape(n_sample, WINDOW, KV_WIDTH),
        sinks.reshape(N_KV, GROUP, 1))
    x1_s, conv_s_t, h_s = _sample_mix_call(
        x_s, u_s, att3.reshape(n_sample, N_HEADS * HEAD_DIM), jnp.transpose(state_conv, (1, 0, 2)), state_lru_h, wts)

    x1_all, route_all, k_win_p, v_win_p, conv_p, h_p, xs, goff, gtot = _mixer_call(x_prompt, x1_s, sinks, wts)

    ys = _moe_call(gtot, xs, w_gate, w_up, w_down, n_all)
    goff = jnp.concatenate([goff, gtot[None]], axis=0)
    ln2_g2, ln2_b2 = ln2_g.reshape(1, -1), ln2_b.reshape(1, -1)
    y_p = _combine_call(goff, x1_all, route_all, ys, ln2_g2, ln2_b2, 0, n_prompt, DISPATCH_TILE, n_all)
    y_s = _combine_call(goff, x1_all, route_all, ys, ln2_g2, ln2_b2, n_prompt, n_sample, n_sample, n_all)

    kv_shape = (WINDOW, N_KV, HEAD_DIM)
    return (y_p.reshape(B, S, D_MODEL), y_s.reshape(n_sample, 1, D_MODEL),
            k_win_p.reshape((B,) + kv_shape), v_win_p.reshape((B,) + kv_shape), conv_p, h_p.reshape(B, LRU_WIDTH),
            k_win_s.reshape((n_sample,) + kv_shape), v_win_s.reshape((n_sample,) + kv_shape),
            jnp.transpose(conv_s_t, (1, 0, 2)), h_s)
```

```python
import functools

import jax
import jax.numpy as jnp
from jax import lax
from jax.experimental import pallas as pl
from jax.experimental.pallas import tpu as pltpu

F32 = jnp.float32
BF16 = jnp.bfloat16

D_MODEL = 1024
LRU_WIDTH = 1024
LRU_BLOCK = 64
CONV_W = 4
LRU_C = 8.0
N_HEADS = 16
N_KV = 4
GROUP = N_HEADS // N_KV
HEAD_DIM = 64
KV_WIDTH = N_KV * HEAD_DIM
WINDOW = 128
NEG_INF = -1e30
N_GROUPS = 4
EXPERTS_PER_GROUP = 8
N_EXPERTS = N_GROUPS * EXPERTS_PER_GROUP
TOP_K = 2
EXPERT_FF = D_MODEL // 2
DEPTH = 1
ALPHA = (2 * DEPTH) ** 0.25
LN_EPS = 1e-5
ATTN_SCALE = HEAD_DIM ** -0.5
LOG2_E = 1.4426950408889634

OFF_XL = 0
OFF_YL = OFF_XL + LRU_WIDTH
OFF_Q = OFF_YL + LRU_WIDTH
OFF_K = OFF_Q + N_HEADS * HEAD_DIM
OFF_V = OFF_K + KV_WIDTH
OFF_GL = OFF_V + KV_WIDTH
OFF_GA = OFF_GL + D_MODEL
IN_WIDTH = OFF_GA + D_MODEL

LANES = 128
SUBLANES = 8
MXU_DIM = 256
VMEM_LIMIT_BYTES = 56 * 1024 * 1024

GATE_CHUNK = MXU_DIM
N_GATE_CHUNKS = LRU_WIDTH // GATE_CHUNK
ROUTE_WIDTH = LANES

SEQ_TILE = 256
MOE_TILE = 1392
DISPATCH_TILE = 256
SLOT_CAP = 32
N_SLOTS = N_EXPERTS * SLOT_CAP
PAD_CHUNK = 128
SEG_PAD = MOE_TILE + SLOT_CAP + PAD_CHUNK
PACKED_WIDTH = D_MODEL // 2
SAMPLE_ATTN_TILE = 16
SAMPLE_PROJ_TILE = 512


def _const_spec(shape):
    nd = len(shape)
    return pl.BlockSpec(shape, lambda *_: (0,) * nd)


def _layer_norm(z, g, b):
    mu = jnp.mean(z, axis=-1, keepdims=True)
    zc = z - mu
    var = jnp.mean(zc * zc, axis=-1, keepdims=True)
    return zc * lax.rsqrt(var + LN_EPS) * g + b


def _sigmoid(x):
    return 1.0 / (1.0 + jnp.exp2(x * -LOG2_E))


def _softplus(x):
    return jnp.maximum(x, 0.0) + jnp.log1p(jnp.exp(-jnp.abs(x)))


def _lru_gates(xc, w_gates_ref, b_rg, b_ig, lam):
    xcb = xc.astype(BF16)
    r_parts, i_parts = [], []
    for c in range(N_GATE_CHUNKS):
        g = jnp.dot(xcb[:, c * GATE_CHUNK:(c + 1) * GATE_CHUNK], w_gates_ref[c], preferred_element_type=F32)
        r_parts.append(g[:, :GATE_CHUNK])
        i_parts.append(g[:, GATE_CHUNK:])
    r = _sigmoid(jnp.concatenate(r_parts, axis=1) + b_rg)
    i = _sigmoid(jnp.concatenate(i_parts, axis=1) + b_ig)
    log_a = (-LRU_C * r) * _softplus(-lam)
    a = jnp.exp(log_a)
    gain_sq = 1.0 - a * a
    gain = jnp.where(gain_sq > 0.0, gain_sq * lax.rsqrt(gain_sq), 0.0)
    u = gain * (i * xc)
    return a, u


def _linear_scan(a, u, h_in):
    n, w = a.shape
    groups = n // SUBLANES
    a3 = a.reshape(groups, SUBLANES, w)
    u3 = u.reshape(groups, SUBLANES, w)
    row = lax.broadcasted_iota(jnp.int32, a3.shape, 1)
    d = 1
    while d < SUBLANES:
        has_prev = row >= d
        u3 = u3 + a3 * jnp.where(has_prev, pltpu.roll(u3, d, axis=1), 0.0)
        a3 = a3 * jnp.where(has_prev, pltpu.roll(a3, d, axis=1), 1.0)
        d *= 2
    carry = h_in
    out = []
    for g in range(groups):
        h_g = u3[g] + a3[g] * carry
        out.append(h_g)
        carry = h_g[SUBLANES - 1:SUBLANES, :]
    return jnp.concatenate(out, axis=0)


def _route(x1, w_hi_ref, w_hilo_ref, b_rt):
    x_hi = x1.astype(BF16)
    x_lo = (x1 - x_hi.astype(F32)).astype(BF16)
    both = jnp.dot(x_hi, w_hilo_ref[...], preferred_element_type=F32)
    logits = (both[:, :ROUTE_WIDTH]
              + (jnp.dot(x_lo, w_hi_ref[...], preferred_element_type=F32) + both[:, ROUTE_WIDTH:])) + b_rt
    col = lax.broadcasted_iota(jnp.int32, logits.shape, 1)
    big = jnp.int32(ROUTE_WIDTH)
    is_g = col < N_GROUPS
    gl = jnp.where(is_g, logits, -jnp.inf)
    gmax = jnp.max(gl, axis=-1, keepdims=True)
    g_idx = jnp.min(jnp.where(gl == gmax, col, big), axis=-1, keepdims=True)
    p_g = 1.0 / jnp.sum(jnp.where(is_g, jnp.exp(gl - gmax), 0.0), axis=-1, keepdims=True)
    lo = N_GROUPS + g_idx * EXPERTS_PER_GROUP
    in_grp = (col >= lo) & (col < lo + EXPERTS_PER_GROUP)
    el = jnp.where(in_grp, logits, -jnp.inf)
    v1 = jnp.max(el, axis=-1, keepdims=True)
    i1 = jnp.min(jnp.where(el == v1, col, big), axis=-1, keepdims=True)
    el2 = jnp.where(col == i1, -jnp.inf, el)
    v2 = jnp.max(el2, axis=-1, keepdims=True)
    i2 = jnp.min(jnp.where(el2 == v2, col, big), axis=-1, keepdims=True)
    e21 = jnp.exp(v2 - v1)
    inv = 1.0 / (1.0 + e21)
    w1 = p_g * inv
    w2 = p_g * (e21 * inv)
    e1 = (i1 - N_GROUPS).astype(F32)
    e2 = (i2 - N_GROUPS).astype(F32)
    return jnp.where(col == 0, e1, jnp.where(col == 1, e2, jnp.where(col == 2, w1, jnp.where(col == 3, w2, 0.0))))


def _merge_norm(x, rec, att, g_l, g_a, w_out_ref, ln_g, ln_b):
    rec_o = jnp.dot(rec.astype(BF16), w_out_ref[0], preferred_element_type=F32)
    att_o = jnp.dot(att.astype(BF16), w_out_ref[1], preferred_element_type=F32)
    merged = _sigmoid(g_l) * rec_o + _sigmoid(g_a) * att_o
    mix = jnp.dot(merged.astype(BF16), w_out_ref[2], preferred_element_type=F32)
    return _layer_norm(ALPHA * x + mix, ln_g, ln_b)


def _mixer_kernel(sinks_ref, x_ref, w_in_ref, b_in_ref, w_conv_ref, b_conv_ref, w_gates_ref, b_rg_ref, b_ig_ref,
                  lam_ref, w_out_ref, ln_g_ref, ln_b_ref, w_rt_hi_ref, w_rt_lo_ref,
                  b_rt_ref, x1_s_ref,
                  x1_ref, route_ref, kwin_ref, vwin_ref, conv_ref, h_ref, xs_hbm, goff_ref, gtot_ref,
                  conv_buf, h_carry, kcat, vcat, att_buf, prev_x1, stage, zero_rows, tri, xb_keep, key_keep,
                  g_vmem, gvec, g_smem, nd_smem, rounds_smem, sem_stage, sem_g, sem_zero,
                  *, tiles_per_seq, n_tiles, seg_cap):
    step = pl.program_id(0)
    last_step = pl.num_programs(0) - 1
    buf = lax.rem(step, 2)
    n_sample = x1_s_ref.shape[0]
    disp = _Dispatcher(xs_hbm, goff_ref, gtot_ref, stage, zero_rows, tri, xb_keep, key_keep, g_vmem, gvec,
                       g_smem, nd_smem, rounds_smem, sem_stage, sem_g, sem_zero, seg_cap)
    n_valid = jnp.where(step == 0, 0, jnp.where(step == last_step, n_sample, SEQ_TILE))

    @pl.when(step == 0)
    def _():
        disp.init()
        prev_x1[...] = jnp.zeros_like(prev_x1)

    def route_and_plan():
        x1_prev = prev_x1[...]
        route = _route(x1_prev, w_rt_hi_ref, w_rt_lo_ref, b_rt_ref[...])
        route_ref[...] = route
        disp.plan(x1_prev, route, n_valid, buf)

    @pl.when(step < n_tiles)
    def _():
        @pl.when(lax.rem(step, tiles_per_seq) == 0)
        def _():
            conv_buf[...] = jnp.zeros_like(conv_buf)
            h_carry[...] = jnp.zeros_like(h_carry)
            kcat[0:WINDOW, :] = jnp.zeros((WINDOW, KV_WIDTH), BF16)
            vcat[0:WINDOW, :] = jnp.zeros((WINDOW, KV_WIDTH), BF16)

        route_and_plan()
        _mixer_tile(lax.rem(step, tiles_per_seq), sinks_ref, x_ref, w_in_ref, b_in_ref, w_conv_ref, b_conv_ref,
                    w_gates_ref, b_rg_ref, b_ig_ref, lam_ref, w_out_ref, ln_g_ref,
                    ln_b_ref, x1_ref, prev_x1, kwin_ref, vwin_ref, conv_ref, h_ref,
                    conv_buf, h_carry, kcat, vcat, att_buf)

    @pl.when(step >= n_tiles)
    def _():
        route_and_plan()

        @pl.when(step == n_tiles)
        def _():
            x1_ref[0:n_sample, :] = x1_s_ref[...]
            prev_x1[0:n_sample, :] = x1_s_ref[...]

    disp.flush(buf, step == last_step)


def _mixer_tile(t, sinks_ref, x_ref, w_in_ref, b_in_ref, w_conv_ref, b_conv_ref, w_gates_ref, b_rg_ref, b_ig_ref,
                lam_ref, w_out_ref, ln_g_ref, ln_b_ref,
                x1_ref, x1_keep, kwin_ref, vwin_ref, conv_ref, h_ref,
                conv_buf, h_carry, kcat, vcat, att_buf):
    T = SEQ_TILE
    x = x_ref[0]
    xb = x.astype(BF16)

    def proj(lo, width):
        return jnp.dot(xb, w_in_ref[:, lo:lo + width], preferred_element_type=F32) + b_in_ref[:, lo:lo + width]

    xl = proj(OFF_XL, LRU_WIDTH)
    xl_ext = jnp.concatenate([conv_buf[...], xl], axis=0)

    def lagged(k):
        return pltpu.roll(xl_ext, k, axis=0)[SUBLANES:, :]
    wc = w_conv_ref[...]
    xc = wc[0:1] * lagged(3)
    xc = xc + wc[1:2] * lagged(2)
    xc = xc + wc[2:3] * lagged(1)
    xc = xc + wc[3:4] * xl + b_conv_ref[...]
    conv_ref[0] = xl[T - (CONV_W - 1):, :]
    conv_buf[...] = xl[T - SUBLANES:, :]

    a, u = _lru_gates(xc, w_gates_ref, b_rg_ref[...], b_ig_ref[...], lam_ref[...])
    h = _linear_scan(a, u, h_carry[0:1, :])
    h_last = h[T - 1:T, :]
    h_carry[0:1, :] = h_last
    h_ref[0] = h_last
    rec = h * jax.nn.gelu(proj(OFF_YL, LRU_WIDTH))

    q = proj(OFF_Q, N_HEADS * HEAD_DIM) * (ATTN_SCALE * LOG2_E)
    k = proj(OFF_K, KV_WIDTH)
    v = proj(OFF_V, KV_WIDTH)
    kwin_ref[0] = k[T - WINDOW:, :]
    vwin_ref[0] = v[T - WINDOW:, :]
    kcat[WINDOW:WINDOW + T, :] = k.astype(BF16)
    vcat[WINDOW:WINDOW + T, :] = v.astype(BF16)

    qi = lax.broadcasted_iota(jnp.int32, (WINDOW, 2 * WINDOW), 0)
    kj = lax.broadcasted_iota(jnp.int32, (WINDOW, 2 * WINDOW), 1)
    band = (kj > qi) & (kj <= qi + WINDOW)
    grp_row = lax.broadcasted_iota(jnp.int32, (GROUP * WINDOW, 1), 0) // WINDOW
    for qb in range(T // WINDOW):
        if qb == 0:
            first_key = jnp.where(t == 0, WINDOW, 0)
            mask1 = band & (kj >= first_key)
        else:
            mask1 = band
        bias = jnp.concatenate([jnp.where(mask1, 0.0, NEG_INF)] * GROUP, axis=0)
        r0 = qb * WINDOW
        qq = q[r0:r0 + WINDOW, :]
        for j in range(N_KV):
            kjb = kcat[r0:r0 + 2 * WINDOW, j * HEAD_DIM:(j + 1) * HEAD_DIM]
            vjb = vcat[r0:r0 + 2 * WINDOW, j * HEAD_DIM:(j + 1) * HEAD_DIM]
            qs = jnp.concatenate(
                [qq[:, (j * GROUP + g) * HEAD_DIM:(j * GROUP + g + 1) * HEAD_DIM] for g in range(GROUP)], axis=0)
            s = lax.dot_general(qs.astype(BF16), kjb, (((1,), (1,)), ((), ())), preferred_element_type=F32) + bias
            sink = jnp.zeros((GROUP * WINDOW, 1), F32)
            for g in range(GROUP):
                sink = jnp.where(grp_row == g, sinks_ref[j * GROUP + g] * LOG2_E, sink)
            m = jnp.maximum(jnp.max(s, axis=-1, keepdims=True), sink)
            p = jnp.exp2(s - m)
            inv = 1.0 / (jnp.sum(p, axis=-1, keepdims=True) + jnp.exp2(sink - m))
            o = jnp.dot((p * inv).astype(BF16), vjb, preferred_element_type=F32)
            for g in range(GROUP):
                hcol = (j * GROUP + g) * HEAD_DIM
                att_buf[r0:r0 + WINDOW, hcol:hcol + HEAD_DIM] = o[g * WINDOW:(g + 1) * WINDOW, :]
    kcat[0:WINDOW, :] = kcat[T:T + WINDOW, :]
    vcat[0:WINDOW, :] = vcat[T:T + WINDOW, :]

    x1 = _merge_norm(x, rec, att_buf[...], proj(OFF_GL, D_MODEL), proj(OFF_GA, D_MODEL),
                     w_out_ref, ln_g_ref[...], ln_b_ref[...])
    x1_ref[...] = x1
    x1_keep[...] = x1


def _mixer_call(x_prompt, x1_s, sinks, wts):
    B, S, _ = x_prompt.shape
    T = SEQ_TILE
    assert T == DISPATCH_TILE and x1_s.shape[0] <= T
    nt = S // T
    n_tiles = B * nt
    n_rows_total = B * S + x1_s.shape[0]
    n_dispatch = _dispatch_tiles(n_rows_total)
    assert n_dispatch == n_tiles + 1
    seg_cap = _segment_capacity(n_rows_total)
    n_assign = TOP_K * T
    weight_args = (wts["w_in"], wts["b_in"], wts["w_conv"], wts["b_conv"], wts["w_gates"], wts["b_rg"], wts["b_ig"],
                   wts["lam"], wts["w_out"], wts["ln1_g"], wts["ln1_b"],
                   wts["w_rt_hi"], wts["w_rt_lo"], wts["b_rt"], x1_s)
    mixed = lambda i: jnp.minimum(i, n_tiles - 1)
    seq = lambda i: mixed(i) // nt
    routed = lambda i: jnp.clip(i - 1, 0, n_dispatch - 1)
    in_specs = [pl.BlockSpec(memory_space=pltpu.SMEM),
                pl.BlockSpec((1, T, D_MODEL), lambda i: (seq(i), lax.rem(mixed(i), nt), 0))]
    in_specs += [_const_spec(w.shape) for w in weight_args]
    out_shape = (
        jax.ShapeDtypeStruct((n_rows_total, D_MODEL), F32),
        jax.ShapeDtypeStruct((n_rows_total, ROUTE_WIDTH), F32),
        jax.ShapeDtypeStruct((B, WINDOW, KV_WIDTH), F32),
        jax.ShapeDtypeStruct((B, WINDOW, KV_WIDTH), F32),
        jax.ShapeDtypeStruct((B, CONV_W - 1, LRU_WIDTH), F32),
        jax.ShapeDtypeStruct((B, 1, LRU_WIDTH), F32),
        jax.ShapeDtypeStruct((N_EXPERTS * seg_cap, PACKED_WIDTH), jnp.uint32),
        jax.ShapeDtypeStruct((n_dispatch, 1, ROUTE_WIDTH), jnp.int32),
        jax.ShapeDtypeStruct((1, ROUTE_WIDTH), jnp.int32),
    )
    out_specs = (
        pl.BlockSpec((T, D_MODEL), lambda i: (jnp.minimum(i, n_tiles), 0)),
        pl.BlockSpec((T, ROUTE_WIDTH), lambda i: (routed(i), 0)),
        pl.BlockSpec((1, WINDOW, KV_WIDTH), lambda i: (seq(i), 0, 0)),
        pl.BlockSpec((1, WINDOW, KV_WIDTH), lambda i: (seq(i), 0, 0)),
        pl.BlockSpec((1, CONV_W - 1, LRU_WIDTH), lambda i: (seq(i), 0, 0)),
        pl.BlockSpec((1, 1, LRU_WIDTH), lambda i: (seq(i), 0, 0)),
        pl.BlockSpec(memory_space=pl.ANY),
        pl.BlockSpec((1, 1, ROUTE_WIDTH), lambda i: (routed(i), 0, 0)),
        pl.BlockSpec((1, ROUTE_WIDTH), lambda i: (0, 0)),
    )
    scratch = [
        pltpu.VMEM((SUBLANES, LRU_WIDTH), F32),
        pltpu.VMEM((SUBLANES, LRU_WIDTH), F32),
        pltpu.VMEM((T + WINDOW, KV_WIDTH), BF16),
        pltpu.VMEM((T + WINDOW, KV_WIDTH), BF16),
        pltpu.VMEM((T, N_HEADS * HEAD_DIM), F32),
        pltpu.VMEM((T, D_MODEL), F32),
        pltpu.VMEM((2, N_SLOTS, PACKED_WIDTH), jnp.uint32),
        pltpu.VMEM((PAD_CHUNK, PACKED_WIDTH), jnp.uint32),
        pltpu.VMEM((n_assign, n_assign), BF16),
        pltpu.VMEM((T, D_MODEL), BF16),
        pltpu.VMEM((SUBLANES, n_assign), F32),
        pltpu.VMEM((SUBLANES, ROUTE_WIDTH), jnp.int32),
        pltpu.VMEM((SUBLANES, ROUTE_WIDTH), F32),
        pltpu.SMEM((1, ROUTE_WIDTH), jnp.int32),
        pltpu.SMEM((1,), jnp.int32),
        pltpu.SMEM((1,), jnp.int32),
        pltpu.SemaphoreType.DMA(()),
        pltpu.SemaphoreType.DMA(()),
        pltpu.SemaphoreType.DMA(()),
    ]
    return pl.pallas_call(
        functools.partial(_mixer_kernel, tiles_per_seq=nt, n_tiles=n_tiles, seg_cap=seg_cap),
        grid=(n_tiles + 2,),
        in_specs=in_specs,
        out_specs=out_specs,
        out_shape=out_shape,
        scratch_shapes=scratch,
        compiler_params=pltpu.CompilerParams(dimension_semantics=("arbitrary",),
                                             vmem_limit_bytes=VMEM_LIMIT_BYTES),
        name="mixer_prompt",
    )(sinks, x_prompt, *weight_args)


def _sample_proj_kernel(x_ref, w_ref, b_ref, u_ref):
    u_ref[...] = jnp.dot(x_ref[...].astype(BF16), w_ref[...], preferred_element_type=F32) + b_ref[...]


def _sample_proj_call(x_s, w_in, b_in):
    n = x_s.shape[0]
    tn = SAMPLE_PROJ_TILE
    return pl.pallas_call(
        _sample_proj_kernel,
        grid=(IN_WIDTH // tn,),
        in_specs=[pl.BlockSpec((n, D_MODEL), lambda c: (0, 0)),
                  pl.BlockSpec((D_MODEL, tn), lambda c: (0, c)),
                  pl.BlockSpec((1, tn), lambda c: (0, c))],
        out_specs=pl.BlockSpec((n, tn), lambda c: (0, c)),
        out_shape=jax.ShapeDtypeStruct((n, IN_WIDTH), F32),
        compiler_params=pltpu.CompilerParams(dimension_semantics=("arbitrary",)),
        name="sample_proj",
    )(x_s, w_in, b_in)


def _sample_attn_kernel(q_ref, kn_ref, vn_ref, kn_row_ref, vn_row_ref, ck_ref, cv_ref, sinks_ref,
                        att_ref, kwin_ref, vwin_ref):
    tb = q_ref.shape[0]
    key_pos = lax.broadcasted_iota(jnp.int32, (tb, GROUP, WINDOW), 2)
    for j in range(N_KV):
        qj = q_ref[:, j * GROUP:(j + 1) * GROUP, :]
        kc = ck_ref[:, :, j * HEAD_DIM:(j + 1) * HEAD_DIM]
        vc = cv_ref[:, :, j * HEAD_DIM:(j + 1) * HEAD_DIM]
        s_c = jnp.einsum("bgd,bsd->bgs", qj.astype(BF16), kc.astype(BF16), preferred_element_type=F32) * ATTN_SCALE
        s_c = jnp.where(key_pos >= 1, s_c, NEG_INF)
        kn = kn_ref[:, j:j + 1, :]
        vn = vn_ref[:, j:j + 1, :]
        s_n = jnp.sum(qj * kn, axis=-1, keepdims=True) * ATTN_SCALE
        sink = sinks_ref[j][None]
        m = jnp.maximum(jnp.maximum(jnp.max(s_c, axis=-1, keepdims=True), s_n), sink)
        p_c = jnp.exp(s_c - m)
        p_n = jnp.exp(s_n - m)
        inv = 1.0 / (jnp.sum(p_c, axis=-1, keepdims=True) + p_n + jnp.exp(sink - m))
        o = jnp.einsum("bgs,bsd->bgd", (p_c * inv).astype(BF16), vc.astype(BF16), preferred_element_type=F32)
        att_ref[:, j * GROUP:(j + 1) * GROUP, :] = o + (p_n * inv) * vn
    kwin_ref[:, 0:WINDOW - 1, :] = ck_ref[:, 1:WINDOW, :]
    kwin_ref[:, WINDOW - 1:WINDOW, :] = kn_row_ref[...]
    vwin_ref[:, 0:WINDOW - 1, :] = cv_ref[:, 1:WINDOW, :]
    vwin_ref[:, WINDOW - 1:WINDOW, :] = vn_row_ref[...]


def _sample_attn_call(q3, kn3, vn3, kn_row, vn_row, ck, cv, sinks3):
    n = q3.shape[0]
    tb = SAMPLE_ATTN_TILE
    b3 = lambda i: (i, 0, 0)
    return pl.pallas_call(
        _sample_attn_kernel,
        grid=(n // tb,),
        in_specs=[pl.BlockSpec((tb, N_HEADS, HEAD_DIM), b3),
                  pl.BlockSpec((tb, N_KV, HEAD_DIM), b3),
                  pl.BlockSpec((tb, N_KV, HEAD_DIM), b3),
                  pl.BlockSpec((tb, 1, KV_WIDTH), b3),
                  pl.BlockSpec((tb, 1, KV_WIDTH), b3),
                  pl.BlockSpec((tb, WINDOW, KV_WIDTH), b3),
                  pl.BlockSpec((tb, WINDOW, KV_WIDTH), b3),
                  pl.BlockSpec((N_KV, GROUP, 1), lambda i: (0, 0, 0))],
        out_specs=(pl.BlockSpec((tb, N_HEADS, HEAD_DIM), b3),
                   pl.BlockSpec((tb, WINDOW, KV_WIDTH), b3),
                   pl.BlockSpec((tb, WINDOW, KV_WIDTH), b3)),
        out_shape=(jax.ShapeDtypeStruct((n, N_HEADS, HEAD_DIM), F32),
                   jax.ShapeDtypeStruct((n, WINDOW, KV_WIDTH), F32),
                   jax.ShapeDtypeStruct((n, WINDOW, KV_WIDTH), F32)),
        compiler_params=pltpu.CompilerParams(dimension_semantics=("arbitrary",)),
        name="sample_attn",
    )(q3, kn3, vn3, kn_row, vn_row, ck, cv, sinks3)


def _sample_mix_kernel(x_ref, u_ref, att_ref, st_ref, h0_ref, w_conv_ref, b_conv_ref, w_gates_ref, b_rg_ref,
                       b_ig_ref, lam_ref, w_out_ref, ln_g_ref, ln_b_ref,
                       x1_ref, conv_ref, h_ref):
    xl = u_ref[:, OFF_XL:OFF_XL + LRU_WIDTH]
    wc = w_conv_ref[...]
    xc = wc[0:1] * st_ref[0]
    xc = xc + wc[1:2] * st_ref[1]
    xc = xc + wc[2:3] * st_ref[2]
    xc = xc + wc[3:4] * xl + b_conv_ref[...]
    conv_ref[0] = st_ref[1]
    conv_ref[1] = st_ref[2]
    conv_ref[2] = xl
    a, u = _lru_gates(xc, w_gates_ref, b_rg_ref[...], b_ig_ref[...], lam_ref[...])
    h = a * h0_ref[...] + u
    h_ref[...] = h
    rec = h * jax.nn.gelu(u_ref[:, OFF_YL:OFF_YL + LRU_WIDTH])
    x1_ref[...] = _merge_norm(x_ref[...], rec, att_ref[...], u_ref[:, OFF_GL:OFF_GL + D_MODEL],
                              u_ref[:, OFF_GA:OFF_GA + D_MODEL], w_out_ref,
                              ln_g_ref[...], ln_b_ref[...])


def _sample_mix_call(x_s, u_s, att, st_t, h0, wts):
    n = x_s.shape[0]
    weight_args = (wts["w_conv"], wts["b_conv"], wts["w_gates"], wts["b_rg"], wts["b_ig"], wts["lam"],
                   wts["w_out"], wts["ln1_g"], wts["ln1_b"])
    args = (x_s, u_s, att, st_t, h0) + weight_args
    out_shapes = ((n, D_MODEL), (CONV_W - 1, n, LRU_WIDTH), (n, LRU_WIDTH))
    return pl.pallas_call(
        _sample_mix_kernel,
        grid=(1,),
        in_specs=[_const_spec(a.shape) for a in args],
        out_specs=tuple(_const_spec(s) for s in out_shapes),
        out_shape=tuple(jax.ShapeDtypeStruct(s, F32) for s in out_shapes),
        compiler_params=pltpu.CompilerParams(dimension_semantics=("arbitrary",),
                                             vmem_limit_bytes=VMEM_LIMIT_BYTES),
        name="sample_mix",
    )(*args)


def _one_hot(mask):
    return jnp.where(mask, 1.0, 0.0).astype(BF16)


def _pack_rows(x):
    half = x.shape[1] // 2
    lo = lax.shift_right_logical(lax.bitcast_convert_type(x[:, :half], jnp.uint32), jnp.uint32(16))
    hi = lax.bitcast_convert_type(x[:, half:], jnp.uint32) & jnp.uint32(0xFFFF0000)
    return lo | hi


def _unpack_rows(words):
    lo = lax.bitcast_convert_type(lax.shift_left(words, jnp.uint32(16)), F32)
    hi = lax.bitcast_convert_type(words & jnp.uint32(0xFFFF0000), F32)
    return jnp.concatenate([lo.astype(BF16), hi.astype(BF16)], axis=1)


class _Dispatcher:
    def __init__(self, xs_hbm, goff_ref, gtot_ref, stage, zero_rows, tri, xb_keep, key_keep, g_vmem, gvec,
                 g_smem, nd_smem, rounds_smem, sem_stage, sem_g, sem_zero, seg_cap):
        self.xs_hbm, self.goff_ref, self.gtot_ref = xs_hbm, goff_ref, gtot_ref
        self.stage, self.zero_rows, self.tri = stage, zero_rows, tri
        self.xb_keep, self.key_keep = xb_keep, key_keep
        self.g_vmem, self.gvec, self.g_smem = g_vmem, gvec, g_smem
        self.nd_smem, self.rounds_smem = nd_smem, rounds_smem
        self.sem_stage, self.sem_g, self.sem_zero = sem_stage, sem_g, sem_zero
        self.seg_cap = seg_cap

    def _g_copy(self):
        return pltpu.make_async_copy(self.g_vmem.at[pl.ds(0, 1)], self.g_smem, self.sem_g)

    def init(self):
        n_assign = self.tri.shape[0]
        r = lax.broadcasted_iota(jnp.int32, (n_assign, n_assign), 0)
        c = lax.broadcasted_iota(jnp.int32, (n_assign, n_assign), 1)
        self.tri[...] = _one_hot(r < c)
        self.gvec[...] = jnp.zeros_like(self.gvec)
        self.g_vmem[...] = jnp.zeros_like(self.g_vmem)
        self.nd_smem[0] = 0
        self._g_copy().start()

    def _sorted_rows(self, e_row, rank_row, xb, lo):
        td = xb.shape[0]
        slot = lax.broadcasted_iota(jnp.int32, (N_SLOTS, td), 0).astype(F32)
        lo_f = lo.astype(F32)
        in_round = (rank_row >= lo_f) & (rank_row < lo_f + SLOT_CAP) & (e_row >= 0.0)
        key = jnp.where(in_round, e_row * SLOT_CAP + (rank_row - lo_f), -1.0)
        perm = jnp.where(slot == key[:, :td], 1.0, jnp.where(slot == key[:, td:], 1.0, 0.0)).astype(BF16)
        return _pack_rows(jnp.dot(perm, xb, preferred_element_type=F32))

    def plan(self, x1, route, n_valid, buf):
        td = x1.shape[0]
        n_assign = TOP_K * td
        valid_col = lax.broadcasted_iota(jnp.int32, (td, 1), 0) < n_valid
        lane = lax.broadcasted_iota(jnp.int32, (td, ROUTE_WIDTH), 1).astype(F32)
        e0c = jnp.where(valid_col, route[:, 0:1], -1.0)
        e1c = jnp.where(valid_col, route[:, 1:2], -1.0)
        cnt_row = jnp.sum(jnp.where(lane == e0c, 1.0, 0.0) + jnp.where(lane == e1c, 1.0, 0.0),
                          axis=0, keepdims=True)
        self.rounds_smem[0] = (jnp.max(cnt_row).astype(jnp.int32) + SLOT_CAP - 1) // SLOT_CAP
        self.goff_ref[0] = self.gvec[0:1, :].astype(jnp.int32)
        self.gvec[0:1, :] = self.gvec[0:1, :] + jnp.ceil(cnt_row * (1.0 / SUBLANES)) * SUBLANES
        self.gtot_ref[...] = self.gvec[0:1, :].astype(jnp.int32)

        route_t = route.T
        valid_row = lax.broadcasted_iota(jnp.int32, (1, td), 1) < n_valid
        e_row = jnp.concatenate([jnp.where(valid_row, route_t[0:1, :], -1.0),
                                 jnp.where(valid_row, route_t[1:2, :], -1.0)], axis=1)
        expert_sub = lax.broadcasted_iota(jnp.int32, (N_EXPERTS, n_assign), 0).astype(F32)
        onehot_t = expert_sub == e_row
        rank_t = jnp.dot(_one_hot(onehot_t), self.tri[...], preferred_element_type=F32)
        rank_row = jnp.sum(jnp.where(onehot_t, rank_t, 0.0), axis=0, keepdims=True)
        xb = jnp.where(valid_col, x1, 0.0).astype(BF16)
        self.xb_keep[...] = xb
        self.key_keep[0:1, :] = e_row
        self.key_keep[1:2, :] = rank_row
        self.stage[buf] = self._sorted_rows(e_row, rank_row, xb, jnp.int32(0))

    def _stage_copy(self, buf, e, dst_row):
        src = self.stage.at[buf, pl.ds(e * SLOT_CAP, SLOT_CAP)]
        return pltpu.make_async_copy(src, self.xs_hbm.at[pl.ds(dst_row, SLOT_CAP)], self.sem_stage)

    def _wait_outstanding(self):
        @pl.when(self.nd_smem[0] > 0)
        def _():
            for _ in range(N_EXPERTS):
                self._stage_copy(0, 0, 0).wait()
        self.nd_smem[0] = 0

    def _start_round(self, buf, lo):
        for e in range(N_EXPERTS):
            dst_row = pl.multiple_of(e * self.seg_cap + self.g_smem[0, e] + lo, SUBLANES)
            self._stage_copy(buf, e, dst_row).start()
        self.nd_smem[0] = N_EXPERTS

    def flush(self, buf, is_last):
        self._g_copy().wait()
        self._wait_outstanding()
        self._start_round(buf, 0)

        def later_round(r, carry):
            lo = r * SLOT_CAP
            rows = self._sorted_rows(self.key_keep[0:1, :], self.key_keep[1:2, :], self.xb_keep[...], lo)
            self._wait_outstanding()
            self.stage[buf] = rows
            self._start_round(buf, lo)
            return carry
        lax.fori_loop(1, self.rounds_smem[0], later_round, 0)

        self.g_vmem[0:1, :] = self.gvec[0:1, :].astype(jnp.int32)
        self._g_copy().start()

        @pl.when(is_last)
        def _():
            self._g_copy().wait()
            self._wait_outstanding()
            self.zero_rows[...] = jnp.zeros_like(self.zero_rows)

            def pad_copy(dst_row):
                dst = self.xs_hbm.at[pl.ds(pl.multiple_of(dst_row, SUBLANES), PAD_CHUNK)]
                return pltpu.make_async_copy(self.zero_rows, dst, self.sem_zero)

            def pad_segment(e, n_started):
                fill = self.g_smem[0, e]
                block_end = (fill + SLOT_CAP + MOE_TILE - 1) // MOE_TILE * MOE_TILE
                n_chunks = (block_end - fill + PAD_CHUNK - 1) // PAD_CHUNK

                def start(c, carry):
                    pad_copy(e * self.seg_cap + fill + c * PAD_CHUNK).start()
                    return carry
                lax.fori_loop(0, n_chunks, start, 0)
                return n_started + n_chunks
            n_started = lax.fori_loop(0, N_EXPERTS, pad_segment, 0)

            def wait(_, carry):
                pad_copy(0).wait()
                return carry
            lax.fori_loop(0, n_started, wait, 0)


def _dispatch_tiles(n_tokens):
    return (n_tokens + DISPATCH_TILE - 1) // DISPATCH_TILE


def _segment_capacity(n_tokens):
    assert SEG_PAD >= DISPATCH_TILE
    alignment_slack = (SUBLANES - 1) * _dispatch_tiles(n_tokens)
    return (n_tokens + alignment_slack + SEG_PAD + MOE_TILE - 1) // MOE_TILE * MOE_TILE


def _moe_kernel(blk_e_ref, blk_j_ref, n_used_ref, xs_ref, wg_ref, wu_ref, wd_ref, ys_ref, wg_bf, wu_bf, wd_bf):
    del blk_j_ref
    b = pl.program_id(0)

    @pl.when(b < n_used_ref[0])
    def _():
        prev_e = blk_e_ref[jnp.maximum(b - 1, 0)]

        @pl.when((b == 0) | (blk_e_ref[b] != prev_e))
        def _():
            wg_bf[...] = wg_ref[0].astype(BF16)
            wu_bf[...] = wu_ref[0].astype(BF16)
            wd_bf[...] = wd_ref[0].astype(BF16)

        xb = _unpack_rows(xs_ref[...])
        g = jnp.dot(xb, wg_bf[...], preferred_element_type=F32)
        u = jnp.dot(xb, wu_bf[...], preferred_element_type=F32)
        hmid = (g * _sigmoid(g)) * u
        y = jnp.dot(hmid.astype(BF16), wd_bf[...], preferred_element_type=F32)
        ys_ref[...] = _pack_rows(y.astype(BF16).astype(F32))


def _expert_blocks(gtot, n_blocks):
    rows = gtot[0, :N_EXPERTS]
    nb = (rows + SLOT_CAP + MOE_TILE - 1) // MOE_TILE
    ends = jnp.cumsum(nb)
    n_used = ends[-1]
    step = jnp.minimum(jnp.arange(n_blocks, dtype=jnp.int32), n_used - 1)
    blk_e = jnp.sum((step[:, None] >= ends[None, :]).astype(jnp.int32), axis=1)
    blk_j = step - (ends - nb)[blk_e]
    return blk_e, blk_j.astype(jnp.int32), n_used.reshape(1).astype(jnp.int32)


def _moe_call(gtot, xs, w_gate, w_up, w_down, n_tokens):
    seg_cap = _segment_capacity(n_tokens)
    seg_blocks = seg_cap // MOE_TILE
    max_rows = n_tokens * TOP_K + N_EXPERTS * (SUBLANES - 1) * _dispatch_tiles(n_tokens)
    n_blocks = (max_rows + N_EXPERTS * (SLOT_CAP + MOE_TILE - 1)) // MOE_TILE
    blk_e, blk_j, n_used = _expert_blocks(gtot, n_blocks)
    row_block = lambda b, be, bj, nu: (be[b] * seg_blocks + bj[b], 0)
    weight_block = lambda b, be, bj, nu: (be[b], 0, 0)
    grid_spec = pltpu.PrefetchScalarGridSpec(
        num_scalar_prefetch=3,
        grid=(n_blocks,),
        in_specs=[
            pl.BlockSpec((MOE_TILE, PACKED_WIDTH), row_block),
            pl.BlockSpec((1, D_MODEL, EXPERT_FF), weight_block),
            pl.BlockSpec((1, D_MODEL, EXPERT_FF), weight_block),
            pl.BlockSpec((1, EXPERT_FF, D_MODEL), weight_block),
        ],
        out_specs=pl.BlockSpec((MOE_TILE, PACKED_WIDTH), row_block),
        scratch_shapes=[
            pltpu.VMEM((D_MODEL, EXPERT_FF), BF16),
            pltpu.VMEM((D_MODEL, EXPERT_FF), BF16),
            pltpu.VMEM((EXPERT_FF, D_MODEL), BF16),
        ],
    )
    return pl.pallas_call(
        _moe_kernel,
        grid_spec=grid_spec,
        out_shape=jax.ShapeDtypeStruct(xs.shape, xs.dtype),
        compiler_params=pltpu.CompilerParams(dimension_semantics=("arbitrary",),
                                             vmem_limit_bytes=VMEM_LIMIT_BYTES),
        name="moe_experts",
    )(blk_e, blk_j, n_used, xs, w_gate, w_up, w_down)


def _combine_kernel(gcur_ref, gnext_ref, x1_ref, route_ref, ys_hbm, ln_g_ref, ln_b_ref, y_ref,
                    stage, acc, tri, sem, *, seg_cap):
    td = x1_ref.shape[0]
    n_assign = TOP_K * td
    i = pl.program_id(0)
    buf = lax.rem(i, 2)

    def stage_copy(g_ref, e, first_rank, to_buf):
        src_row = pl.multiple_of(e * seg_cap + g_ref[0, 0, e] + first_rank, SUBLANES)
        slot0 = e * SLOT_CAP
        if not isinstance(slot0, int):
            slot0 = pl.multiple_of(slot0, SLOT_CAP)
        dst = stage.at[to_buf, pl.ds(slot0, SLOT_CAP)]
        return pltpu.make_async_copy(ys_hbm.at[pl.ds(src_row, SLOT_CAP)], dst, sem.at[to_buf])

    def fetch_first_round(g_ref, to_buf):
        for e in range(N_EXPERTS):
            stage_copy(g_ref, e, 0, to_buf).start()

    @pl.when(i == 0)
    def _():
        r = lax.broadcasted_iota(jnp.int32, (n_assign, n_assign), 0)
        c = lax.broadcasted_iota(jnp.int32, (n_assign, n_assign), 1)
        tri[...] = _one_hot(c < r)
        fetch_first_round(gcur_ref, 0)

    def count(e):
        return gnext_ref[0, 0, e] - gcur_ref[0, 0, e]

    route = route_ref[...]
    e0, e1, w0, w1 = route[:, 0:1], route[:, 1:2], route[:, 2:3], route[:, 3:4]
    lane = lax.broadcasted_iota(jnp.int32, (td, ROUTE_WIDTH), 1).astype(F32)
    onehot = jnp.concatenate([lane == e0, lane == e1], axis=0)
    onehot_f = jnp.where(onehot, 1.0, 0.0)
    rank_mat = jnp.dot(tri[...], onehot_f.astype(BF16), preferred_element_type=F32)
    rank = jnp.sum(jnp.where(onehot, rank_mat, 0.0), axis=1, keepdims=True)
    r0, r1 = rank[:td], rank[td:]
    cnt_row = jnp.sum(onehot_f, axis=0, keepdims=True)
    rounds = (jnp.max(cnt_row).astype(jnp.int32) + SLOT_CAP - 1) // SLOT_CAP

    slot_lane = lax.broadcasted_iota(jnp.int32, (td, N_SLOTS), 1).astype(F32)

    def wait_copies(n):
        def wait_one(_, c):
            stage_copy(gcur_ref, 0, 0, buf).wait()
            return c
        lax.fori_loop(0, n, wait_one, 0)

    def weighted_rows(lo):
        lo_f = lo.astype(F32)
        rows = _unpack_rows(stage[buf])

        def selector(e_col, r_col, w_col):
            in_round = (r_col >= lo_f) & (r_col < lo_f + SLOT_CAP)
            key = jnp.where(in_round, e_col * SLOT_CAP + (r_col - lo_f), -1.0)
            return jnp.where(slot_lane == key, w_col, 0.0)
        mix = (selector(e0, r0, w0) + selector(e1, r1, w1)).astype(BF16)
        return jnp.dot(mix, rows, preferred_element_type=F32)

    for _ in range(N_EXPERTS):
        stage_copy(gcur_ref, 0, 0, buf).wait()

    fetch_first_round(gnext_ref, 1 - buf)

    acc[...] = weighted_rows(jnp.int32(0))

    def later_round(r, carry):
        lo = r * SLOT_CAP

        def start(e, n_started):
            has_rows = count(e) > lo

            @pl.when(has_rows)
            def _():
                stage_copy(gcur_ref, e, lo, buf).start()
            return n_started + has_rows.astype(jnp.int32)
        wait_copies(lax.fori_loop(0, N_EXPERTS, start, 0))
        acc[...] = acc[...] + weighted_rows(lo)
        return carry
    lax.fori_loop(1, rounds, later_round, 0)

    y_ref[...] = _layer_norm(ALPHA * x1_ref[...] + acc[...], ln_g_ref[...], ln_b_ref[...])

    @pl.when(i + 1 == pl.num_programs(0))
    def _():
        for _ in range(N_EXPERTS):
            stage_copy(gnext_ref, 0, 0, 1 - buf).wait()


def _combine_call(goff, x1_all, route_all, ys, ln_g, ln_b, first_row, n_rows, tile, n_tokens):
    assert first_row % DISPATCH_TILE == 0 and first_row % tile == 0
    assert tile == DISPATCH_TILE or n_rows == tile
    first_block = first_row // tile
    first_goff = first_row // DISPATCH_TILE
    n_assign = TOP_K * tile
    grid_spec = pl.GridSpec(
        grid=(n_rows // tile,),
        in_specs=[
            pl.BlockSpec((1, 1, ROUTE_WIDTH), lambda i: (first_goff + i, 0, 0), memory_space=pltpu.SMEM),
            pl.BlockSpec((1, 1, ROUTE_WIDTH), lambda i: (first_goff + i + 1, 0, 0), memory_space=pltpu.SMEM),
            pl.BlockSpec((tile, D_MODEL), lambda i: (first_block + i, 0)),
            pl.BlockSpec((tile, ROUTE_WIDTH), lambda i: (first_block + i, 0)),
            pl.BlockSpec(memory_space=pl.ANY),
            _const_spec(ln_g.shape),
            _const_spec(ln_b.shape),
        ],
        out_specs=pl.BlockSpec((tile, D_MODEL), lambda i: (i, 0)),
        scratch_shapes=[pltpu.VMEM((2, N_SLOTS, PACKED_WIDTH), jnp.uint32),
                        pltpu.VMEM((tile, D_MODEL), F32),
                        pltpu.VMEM((n_assign, n_assign), BF16),
                        pltpu.SemaphoreType.DMA((2,))],
    )
    return pl.pallas_call(
        functools.partial(_combine_kernel, seg_cap=_segment_capacity(n_tokens)),
        grid_spec=grid_spec,
        out_shape=jax.ShapeDtypeStruct((n_rows, D_MODEL), F32),
        compiler_params=pltpu.CompilerParams(dimension_semantics=("arbitrary",),
                                             vmem_limit_bytes=VMEM_LIMIT_BYTES),
        name="moe_combine",
    )(goff, goff, x1_all, route_all, ys, ln_g, ln_b)


def _prepare_weights(w_in, b_in, w_conv, b_conv, w_rg, b_rg, w_ig, b_ig, lru_lambda, w_lru_out, w_attn_out, w_o,
                     ln1_g, ln1_b, w_group, b_group, w_router, b_router):
    blocks_per_chunk = GATE_CHUNK // LRU_BLOCK

    def chunked_block_diag(w):
        w = w.reshape(N_GATE_CHUNKS, blocks_per_chunk, LRU_BLOCK, LRU_BLOCK)
        eye = jnp.eye(blocks_per_chunk, dtype=w.dtype)
        return jnp.einsum("cbij,bd->cbidj", w, eye).reshape(N_GATE_CHUNKS, GATE_CHUNK, GATE_CHUNK)

    w_gates = jnp.concatenate([chunked_block_diag(w_rg), chunked_block_diag(w_ig)], axis=-1).astype(BF16)
    w_rt = jnp.concatenate([w_group, w_router], axis=1)
    w_rt = jnp.pad(w_rt, ((0, 0), (0, ROUTE_WIDTH - w_rt.shape[1])))
    w_rt_hi = w_rt.astype(BF16)
    w_rt_lo = jnp.concatenate([w_rt_hi, (w_rt - w_rt_hi.astype(F32)).astype(BF16)], axis=1)
    b_rt = jnp.pad(jnp.concatenate([b_group, b_router]), (0, ROUTE_WIDTH - N_GROUPS - N_EXPERTS))
    row = lambda v: v.reshape(1, -1)
    return dict(
        w_in=w_in.astype(BF16), b_in=row(b_in), w_conv=w_conv, b_conv=row(b_conv), w_gates=w_gates,
        b_rg=row(b_rg), b_ig=row(b_ig), lam=row(lru_lambda),
        w_out=jnp.stack([w_lru_out, w_attn_out, w_o]).astype(BF16),
        ln1_g=row(ln1_g), ln1_b=row(ln1_b), w_rt_hi=w_rt_hi, w_rt_lo=w_rt_lo, b_rt=row(b_rt))


def kernel(x_prompt, x_sample, cache_k, cache_v, state_conv, state_lru_h, w_in, b_in, w_conv, b_conv, w_rg, b_rg,
           w_ig, b_ig, lru_lambda, sinks, w_lru_out, w_attn_out, w_o, ln1_g, ln1_b, w_group, b_group, w_router,
           b_router, w_gate, w_up, w_down, ln2_g, ln2_b):
    B, S, _ = x_prompt.shape
    n_prompt = B * S
    n_sample = x_sample.shape[0]
    n_all = n_prompt + n_sample
    wts = _prepare_weights(w_in, b_in, w_conv, b_conv, w_rg, b_rg, w_ig, b_ig, lru_lambda, w_lru_out, w_attn_out,
                           w_o, ln1_g, ln1_b, w_group, b_group, w_router, b_router)

    x_s = x_sample.reshape(n_sample, D_MODEL)
    u_s = _sample_proj_call(x_s, wts["w_in"], wts["b_in"])
    q3 = u_s[:, OFF_Q:OFF_K].reshape(n_sample, N_HEADS, HEAD_DIM)
    k_new = u_s[:, OFF_K:OFF_V]
    v_new = u_s[:, OFF_V:OFF_GL]
    att3, k_win_s, v_win_s = _sample_attn_call(
        q3, k_new.reshape(n_sample, N_KV, HEAD_DIM), v_new.reshape(n_sample, N_KV, HEAD_DIM),
        k_new.reshape(n_sample, 1, KV_WIDTH), v_new.reshape(n_sample, 1, KV_WIDTH),
        cache_k.reshape(n_sample, WINDOW, KV_WIDTH), cache_v.reshape(n_sample, WINDOW, KV_WIDTH),
        sinks.reshape(N_KV, GROUP, 1))
    x1_s, conv_s_t, h_s = _sample_mix_call(
        x_s, u_s, att3.reshape(n_sample, N_HEADS * HEAD_DIM), jnp.transpose(state_conv, (1, 0, 2)), state_lru_h, wts)

    x1_all, route_all, k_win_p, v_win_p, conv_p, h_p, xs, goff, gtot = _mixer_call(x_prompt, x1_s, sinks, wts)

    ys = _moe_call(gtot, xs, w_gate, w_up, w_down, n_all)
    goff = jnp.concatenate([goff, gtot[None]], axis=0)
    ln2_g2, ln2_b2 = ln2_g.reshape(1, -1), ln2_b.reshape(1, -1)
    y_p = _combine_call(goff, x1_all, route_all, ys, ln2_g2, ln2_b2, 0, n_prompt, DISPATCH_TILE, n_all)
    y_s = _combine_call(goff, x1_all, route_all, ys, ln2_g2, ln2_b2, n_prompt, n_sample, n_sample, n_all)

    kv_shape = (WINDOW, N_KV, HEAD_DIM)
    return (y_p.reshape(B, S, D_MODEL), y_s.reshape(n_sample, 1, D_MODEL),
            k_win_p.reshape((B,) + kv_shape), v_win_p.reshape((B,) + kv_shape), conv_p, h_p.reshape(B, LRU_WIDTH),
            k_win_s.reshape((n_sample,) + kv_shape), v_win_s.reshape((n_sample,) + kv_shape),
            jnp.transpose(conv_s_t, (1, 0, 2)), h_s)
```

```python
import functools

import jax
import jax.numpy as jnp
from jax import lax
from jax.experimental import pallas as pl
from jax.experimental.pallas import tpu as pltpu

F32 = jnp.float32
BF16 = jnp.bfloat16

D_MODEL = 1024
LRU_WIDTH = 1024
LRU_BLOCK = 64
CONV_W = 4
LRU_C = 8.0
N_HEADS = 16
N_KV = 4
GROUP = N_HEADS // N_KV
HEAD_DIM = 64
KV_WIDTH = N_KV * HEAD_DIM
WINDOW = 128
NEG_INF = -1e30
N_GROUPS = 4
EXPERTS_PER_GROUP = 8
N_EXPERTS = N_GROUPS * EXPERTS_PER_GROUP
TOP_K = 2
EXPERT_FF = D_MODEL // 2
DEPTH = 1
ALPHA = (2 * DEPTH) ** 0.25
LN_EPS = 1e-5
ATTN_SCALE = HEAD_DIM ** -0.5
LOG2_E = 1.4426950408889634

OFF_XL = 0
OFF_YL = OFF_XL + LRU_WIDTH
OFF_Q = OFF_YL + LRU_WIDTH
OFF_K = OFF_Q + N_HEADS * HEAD_DIM
OFF_V = OFF_K + KV_WIDTH
OFF_GL = OFF_V + KV_WIDTH
OFF_GA = OFF_GL + D_MODEL
IN_WIDTH = OFF_GA + D_MODEL

LANES = 128
SUBLANES = 8
MXU_DIM = 256
VMEM_LIMIT_BYTES = 56 * 1024 * 1024

GATE_CHUNK = MXU_DIM
N_GATE_CHUNKS = LRU_WIDTH // GATE_CHUNK
ROUTE_WIDTH = LANES

SEQ_TILE = 256
MOE_TILE = 1392
DISPATCH_TILE = 256
SLOT_CAP = 32
N_SLOTS = N_EXPERTS * SLOT_CAP
PAD_CHUNK = 128
SEG_PAD = MOE_TILE + SLOT_CAP + PAD_CHUNK
PACKED_WIDTH = D_MODEL // 2
SAMPLE_ATTN_TILE = 16
SAMPLE_PROJ_TILE = 512


def _const_spec(shape):
    nd = len(shape)
    return pl.BlockSpec(shape, lambda *_: (0,) * nd)


def _layer_norm(z, g, b):
    mu = jnp.mean(z, axis=-1, keepdims=True)
    zc = z - mu
    var = jnp.mean(zc * zc, axis=-1, keepdims=True)
    return zc * lax.rsqrt(var + LN_EPS) * g + b


def _sigmoid(x):
    return 1.0 / (1.0 + jnp.exp2(x * -LOG2_E))


def _softplus(x):
    return jnp.maximum(x, 0.0) + jnp.log1p(jnp.exp(-jnp.abs(x)))


def _lru_gates(xc, w_gates_ref, b_rg, b_ig, lam):
    xcb = xc.astype(BF16)
    r_parts, i_parts = [], []
    for c in range(N_GATE_CHUNKS):
        g = jnp.dot(xcb[:, c * GATE_CHUNK:(c + 1) * GATE_CHUNK], w_gates_ref[c], preferred_element_type=F32)
        r_parts.append(g[:, :GATE_CHUNK])
        i_parts.append(g[:, GATE_CHUNK:])
    r = _sigmoid(jnp.concatenate(r_parts, axis=1) + b_rg)
    i = _sigmoid(jnp.concatenate(i_parts, axis=1) + b_ig)
    log_a = (-LRU_C * r) * _softplus(-lam)
    a = jnp.exp(log_a)
    gain_sq = 1.0 - a * a
    gain = jnp.where(gain_sq > 0.0, gain_sq * lax.rsqrt(gain_sq), 0.0)
    u = gain * (i * xc)
    return a, u


def _linear_scan(a, u, h_in):
    n, w = a.shape
    groups = n // SUBLANES
    a3 = a.reshape(groups, SUBLANES, w)
    u3 = u.reshape(groups, SUBLANES, w)
    row = lax.broadcasted_iota(jnp.int32, a3.shape, 1)
    d = 1
    while d < SUBLANES:
        has_prev = row >= d
        u3 = u3 + a3 * jnp.where(has_prev, pltpu.roll(u3, d, axis=1), 0.0)
        a3 = a3 * jnp.where(has_prev, pltpu.roll(a3, d, axis=1), 1.0)
        d *= 2
    carry = h_in
    out = []
    for g in range(groups):
        h_g = u3[g] + a3[g] * carry
        out.append(h_g)
        carry = h_g[SUBLANES - 1:SUBLANES, :]
    return jnp.concatenate(out, axis=0)


def _route(x1, w_hi_ref, w_hilo_ref, b_rt):
    x_hi = x1.astype(BF16)
    x_lo = (x1 - x_hi.astype(F32)).astype(BF16)
    both = jnp.dot(x_hi, w_hilo_ref[...], preferred_element_type=F32)
    logits = (both[:, :ROUTE_WIDTH]
              + (jnp.dot(x_lo, w_hi_ref[...], preferred_element_type=F32) + both[:, ROUTE_WIDTH:])) + b_rt
    col = lax.broadcasted_iota(jnp.int32, logits.shape, 1)
    big = jnp.int32(ROUTE_WIDTH)
    is_g = col < N_GROUPS
    gl = jnp.where(is_g, logits, -jnp.inf)
    gmax = jnp.max(gl, axis=-1, keepdims=True)
    g_idx = jnp.min(jnp.where(gl == gmax, col, big), axis=-1, keepdims=True)
    p_g = 1.0 / jnp.sum(jnp.where(is_g, jnp.exp(gl - gmax), 0.0), axis=-1, keepdims=True)
    lo = N_GROUPS + g_idx * EXPERTS_PER_GROUP
    in_grp = (col >= lo) & (col < lo + EXPERTS_PER_GROUP)
    el = jnp.where(in_grp, logits, -jnp.inf)
    v1 = jnp.max(el, axis=-1, keepdims=True)
    i1 = jnp.min(jnp.where(el == v1, col, big), axis=-1, keepdims=True)
    el2 = jnp.where(col == i1, -jnp.inf, el)
    v2 = jnp.max(el2, axis=-1, keepdims=True)
    i2 = jnp.min(jnp.where(el2 == v2, col, big), axis=-1, keepdims=True)
    e21 = jnp.exp(v2 - v1)
    inv = 1.0 / (1.0 + e21)
    w1 = p_g * inv
    w2 = p_g * (e21 * inv)
    e1 = (i1 - N_GROUPS).astype(F32)
    e2 = (i2 - N_GROUPS).astype(F32)
    return jnp.where(col == 0, e1, jnp.where(col == 1, e2, jnp.where(col == 2, w1, jnp.where(col == 3, w2, 0.0))))


def _merge_norm(x, rec, att, g_l, g_a, w_out_ref, ln_g, ln_b):
    rec_o = jnp.dot(rec.astype(BF16), w_out_ref[0], preferred_element_type=F32)
    att_o = jnp.dot(att.astype(BF16), w_out_ref[1], preferred_element_type=F32)
    merged = _sigmoid(g_l) * rec_o + _sigmoid(g_a) * att_o
    mix = jnp.dot(merged.astype(BF16), w_out_ref[2], preferred_element_type=F32)
    return _layer_norm(ALPHA * x + mix, ln_g, ln_b)


def _mixer_kernel(sinks_ref, x_ref, w_in_ref, b_in_ref, w_conv_ref, b_conv_ref, w_gates_ref, b_rg_ref, b_ig_ref,
                  lam_ref, w_out_ref, ln_g_ref, ln_b_ref, w_rt_hi_ref, w_rt_lo_ref,
                  b_rt_ref, x1_s_ref,
                  x1_ref, route_ref, kwin_ref, vwin_ref, conv_ref, h_ref, xs_hbm, goff_ref, gtot_ref,
                  conv_buf, h_carry, kcat, vcat, att_buf, prev_x1, stage, zero_rows, tri, xb_keep, key_keep,
                  g_vmem, gvec, g_smem, nd_smem, rounds_smem, sem_stage, sem_g, sem_zero,
                  *, tiles_per_seq, n_tiles, seg_cap):
    step = pl.program_id(0)
    last_step = pl.num_programs(0) - 1
    buf = lax.rem(step, 2)
    n_sample = x1_s_ref.shape[0]
    disp = _Dispatcher(xs_hbm, goff_ref, gtot_ref, stage, zero_rows, tri, xb_keep, key_keep, g_vmem, gvec,
                       g_smem, nd_smem, rounds_smem, sem_stage, sem_g, sem_zero, seg_cap)
    n_valid = jnp.where(step == 0, 0, jnp.where(step == last_step, n_sample, SEQ_TILE))

    @pl.when(step == 0)
    def _():
        disp.init()
        prev_x1[...] = jnp.zeros_like(prev_x1)

    def route_and_plan():
        x1_prev = prev_x1[...]
        route = _route(x1_prev, w_rt_hi_ref, w_rt_lo_ref, b_rt_ref[...])
        route_ref[...] = route
        disp.plan(x1_prev, route, n_valid, buf)

    @pl.when(step < n_tiles)
    def _():
        @pl.when(lax.rem(step, tiles_per_seq) == 0)
        def _():
            conv_buf[...] = jnp.zeros_like(conv_buf)
            h_carry[...] = jnp.zeros_like(h_carry)
            kcat[0:WINDOW, :] = jnp.zeros((WINDOW, KV_WIDTH), BF16)
            vcat[0:WINDOW, :] = jnp.zeros((WINDOW, KV_WIDTH), BF16)

        route_and_plan()
        _mixer_tile(lax.rem(step, tiles_per_seq), sinks_ref, x_ref, w_in_ref, b_in_ref, w_conv_ref, b_conv_ref,
                    w_gates_ref, b_rg_ref, b_ig_ref, lam_ref, w_out_ref, ln_g_ref,
                    ln_b_ref, x1_ref, prev_x1, kwin_ref, vwin_ref, conv_ref, h_ref,
                    conv_buf, h_carry, kcat, vcat, att_buf)
        disp.start_copies(buf)

    @pl.when(step >= n_tiles)
    def _():
        route_and_plan()
        disp.start_copies(buf)

        @pl.when(step == n_tiles)
        def _():
            x1_ref[0:n_sample, :] = x1_s_ref[...]
            prev_x1[0:n_sample, :] = x1_s_ref[...]

    disp.finish_step(buf, step == last_step)


def _mixer_tile(t, sinks_ref, x_ref, w_in_ref, b_in_ref, w_conv_ref, b_conv_ref, w_gates_ref, b_rg_ref, b_ig_ref,
                lam_ref, w_out_ref, ln_g_ref, ln_b_ref,
                x1_ref, x1_keep, kwin_ref, vwin_ref, conv_ref, h_ref,
                conv_buf, h_carry, kcat, vcat, att_buf):
    T = SEQ_TILE
    x = x_ref[0]
    xb = x.astype(BF16)

    def proj(lo, width):
        return jnp.dot(xb, w_in_ref[:, lo:lo + width], preferred_element_type=F32) + b_in_ref[:, lo:lo + width]

    xl = proj(OFF_XL, LRU_WIDTH)
    xl_ext = jnp.concatenate([conv_buf[...], xl], axis=0)

    def lagged(k):
        return pltpu.roll(xl_ext, k, axis=0)[SUBLANES:, :]
    wc = w_conv_ref[...]
    xc = wc[0:1] * lagged(3)
    xc = xc + wc[1:2] * lagged(2)
    xc = xc + wc[2:3] * lagged(1)
    xc = xc + wc[3:4] * xl + b_conv_ref[...]
    conv_ref[0] = xl[T - (CONV_W - 1):, :]
    conv_buf[...] = xl[T - SUBLANES:, :]

    a, u = _lru_gates(xc, w_gates_ref, b_rg_ref[...], b_ig_ref[...], lam_ref[...])
    h = _linear_scan(a, u, h_carry[0:1, :])
    h_last = h[T - 1:T, :]
    h_carry[0:1, :] = h_last
    h_ref[0] = h_last
    rec = h * jax.nn.gelu(proj(OFF_YL, LRU_WIDTH))

    q = proj(OFF_Q, N_HEADS * HEAD_DIM) * (ATTN_SCALE * LOG2_E)
    k = proj(OFF_K, KV_WIDTH)
    v = proj(OFF_V, KV_WIDTH)
    kwin_ref[0] = k[T - WINDOW:, :]
    vwin_ref[0] = v[T - WINDOW:, :]
    kcat[WINDOW:WINDOW + T, :] = k.astype(BF16)
    vcat[WINDOW:WINDOW + T, :] = v.astype(BF16)

    qi = lax.broadcasted_iota(jnp.int32, (WINDOW, 2 * WINDOW), 0)
    kj = lax.broadcasted_iota(jnp.int32, (WINDOW, 2 * WINDOW), 1)
    band = (kj > qi) & (kj <= qi + WINDOW)
    grp_row = lax.broadcasted_iota(jnp.int32, (GROUP * WINDOW, 1), 0) // WINDOW
    for qb in range(T // WINDOW):
        if qb == 0:
            first_key = jnp.where(t == 0, WINDOW, 0)
            mask1 = band & (kj >= first_key)
        else:
            mask1 = band
        bias = jnp.concatenate([jnp.where(mask1, 0.0, NEG_INF)] * GROUP, axis=0)
        r0 = qb * WINDOW
        qq = q[r0:r0 + WINDOW, :]
        for j in range(N_KV):
            kjb = kcat[r0:r0 + 2 * WINDOW, j * HEAD_DIM:(j + 1) * HEAD_DIM]
            vjb = vcat[r0:r0 + 2 * WINDOW, j * HEAD_DIM:(j + 1) * HEAD_DIM]
            qs = jnp.concatenate(
                [qq[:, (j * GROUP + g) * HEAD_DIM:(j * GROUP + g + 1) * HEAD_DIM] for g in range(GROUP)], axis=0)
            s = lax.dot_general(qs.astype(BF16), kjb, (((1,), (1,)), ((), ())), preferred_element_type=F32) + bias
            sink = jnp.zeros((GROUP * WINDOW, 1), F32)
            for g in range(GROUP):
                sink = jnp.where(grp_row == g, sinks_ref[j * GROUP + g] * LOG2_E, sink)
            m = jnp.maximum(jnp.max(s, axis=-1, keepdims=True), sink)
            p = jnp.exp2(s - m)
            inv = 1.0 / (jnp.sum(p, axis=-1, keepdims=True) + jnp.exp2(sink - m))
            o = jnp.dot((p * inv).astype(BF16), vjb, preferred_element_type=F32)
            for g in range(GROUP):
                hcol = (j * GROUP + g) * HEAD_DIM
                att_buf[r0:r0 + WINDOW, hcol:hcol + HEAD_DIM] = o[g * WINDOW:(g + 1) * WINDOW, :]
    kcat[0:WINDOW, :] = kcat[T:T + WINDOW, :]
    vcat[0:WINDOW, :] = vcat[T:T + WINDOW, :]

    x1 = _merge_norm(x, rec, att_buf[...], proj(OFF_GL, D_MODEL), proj(OFF_GA, D_MODEL),
                     w_out_ref, ln_g_ref[...], ln_b_ref[...])
    x1_ref[...] = x1
    x1_keep[...] = x1


def _mixer_call(x_prompt, x1_s, sinks, wts):
    B, S, _ = x_prompt.shape
    T = SEQ_TILE
    assert T == DISPATCH_TILE and x1_s.shape[0] <= T
    nt = S // T
    n_tiles = B * nt
    n_rows_total = B * S + x1_s.shape[0]
    n_dispatch = _dispatch_tiles(n_rows_total)
    assert n_dispatch == n_tiles + 1
    seg_cap = _segment_capacity(n_rows_total)
    n_assign = TOP_K * T
    weight_args = (wts["w_in"], wts["b_in"], wts["w_conv"], wts["b_conv"], wts["w_gates"], wts["b_rg"], wts["b_ig"],
                   wts["lam"], wts["w_out"], wts["ln1_g"], wts["ln1_b"],
                   wts["w_rt_hi"], wts["w_rt_lo"], wts["b_rt"], x1_s)
    mixed = lambda i: jnp.minimum(i, n_tiles - 1)
    seq = lambda i: mixed(i) // nt
    routed = lambda i: jnp.clip(i - 1, 0, n_dispatch - 1)
    in_specs = [pl.BlockSpec(memory_space=pltpu.SMEM),
                pl.BlockSpec((1, T, D_MODEL), lambda i: (seq(i), lax.rem(mixed(i), nt), 0))]
    in_specs += [_const_spec(w.shape) for w in weight_args]
    out_shape = (
        jax.ShapeDtypeStruct((n_rows_total, D_MODEL), F32),
        jax.ShapeDtypeStruct((n_rows_total, ROUTE_WIDTH), F32),
        jax.ShapeDtypeStruct((B, WINDOW, KV_WIDTH), F32),
        jax.ShapeDtypeStruct((B, WINDOW, KV_WIDTH), F32),
        jax.ShapeDtypeStruct((B, CONV_W - 1, LRU_WIDTH), F32),
        jax.ShapeDtypeStruct((B, 1, LRU_WIDTH), F32),
        jax.ShapeDtypeStruct((N_EXPERTS * seg_cap, PACKED_WIDTH), jnp.uint32),
        jax.ShapeDtypeStruct((n_dispatch, 1, ROUTE_WIDTH), jnp.int32),
        jax.ShapeDtypeStruct((1, ROUTE_WIDTH), jnp.int32),
    )
    out_specs = (
        pl.BlockSpec((T, D_MODEL), lambda i: (jnp.minimum(i, n_tiles), 0)),
        pl.BlockSpec((T, ROUTE_WIDTH), lambda i: (routed(i), 0)),
        pl.BlockSpec((1, WINDOW, KV_WIDTH), lambda i: (seq(i), 0, 0)),
        pl.BlockSpec((1, WINDOW, KV_WIDTH), lambda i: (seq(i), 0, 0)),
        pl.BlockSpec((1, CONV_W - 1, LRU_WIDTH), lambda i: (seq(i), 0, 0)),
        pl.BlockSpec((1, 1, LRU_WIDTH), lambda i: (seq(i), 0, 0)),
        pl.BlockSpec(memory_space=pl.ANY),
        pl.BlockSpec((1, 1, ROUTE_WIDTH), lambda i: (routed(i), 0, 0)),
        pl.BlockSpec((1, ROUTE_WIDTH), lambda i: (0, 0)),
    )
    scratch = [
        pltpu.VMEM((SUBLANES, LRU_WIDTH), F32),
        pltpu.VMEM((SUBLANES, LRU_WIDTH), F32),
        pltpu.VMEM((T + WINDOW, KV_WIDTH), BF16),
        pltpu.VMEM((T + WINDOW, KV_WIDTH), BF16),
        pltpu.VMEM((T, N_HEADS * HEAD_DIM), F32),
        pltpu.VMEM((T, D_MODEL), F32),
        pltpu.VMEM((2, N_SLOTS, PACKED_WIDTH), jnp.uint32),
        pltpu.VMEM((PAD_CHUNK, PACKED_WIDTH), jnp.uint32),
        pltpu.VMEM((n_assign, n_assign), BF16),
        pltpu.VMEM((T, D_MODEL), BF16),
        pltpu.VMEM((SUBLANES, n_assign), F32),
        pltpu.VMEM((SUBLANES, ROUTE_WIDTH), jnp.int32),
        pltpu.VMEM((SUBLANES, ROUTE_WIDTH), F32),
        pltpu.SMEM((2, ROUTE_WIDTH), jnp.int32),
        pltpu.SMEM((1,), jnp.int32),
        pltpu.SMEM((1,), jnp.int32),
        pltpu.SemaphoreType.DMA(()),
        pltpu.SemaphoreType.DMA(()),
        pltpu.SemaphoreType.DMA(()),
    ]
    return pl.pallas_call(
        functools.partial(_mixer_kernel, tiles_per_seq=nt, n_tiles=n_tiles, seg_cap=seg_cap),
        grid=(n_tiles + 2,),
        in_specs=in_specs,
        out_specs=out_specs,
        out_shape=out_shape,
        scratch_shapes=scratch,
        compiler_params=pltpu.CompilerParams(dimension_semantics=("arbitrary",),
                                             vmem_limit_bytes=VMEM_LIMIT_BYTES),
        name="mixer_prompt",
    )(sinks, x_prompt, *weight_args)


def _sample_proj_kernel(x_ref, w_ref, b_ref, u_ref):
    u_ref[...] = jnp.dot(x_ref[...].astype(BF16), w_ref[...], preferred_element_type=F32) + b_ref[...]


def _sample_proj_call(x_s, w_in, b_in):
    n = x_s.shape[0]
    tn = SAMPLE_PROJ_TILE
    return pl.pallas_call(
        _sample_proj_kernel,
        grid=(IN_WIDTH // tn,),
        in_specs=[pl.BlockSpec((n, D_MODEL), lambda c: (0, 0)),
                  pl.BlockSpec((D_MODEL, tn), lambda c: (0, c)),
                  pl.BlockSpec((1, tn), lambda c: (0, c))],
        out_specs=pl.BlockSpec((n, tn), lambda c: (0, c)),
        out_shape=jax.ShapeDtypeStruct((n, IN_WIDTH), F32),
        compiler_params=pltpu.CompilerParams(dimension_semantics=("arbitrary",)),
        name="sample_proj",
    )(x_s, w_in, b_in)


def _sample_attn_kernel(q_ref, kn_ref, vn_ref, kn_row_ref, vn_row_ref, ck_ref, cv_ref, sinks_ref,
                        att_ref, kwin_ref, vwin_ref):
    tb = q_ref.shape[0]
    key_pos = lax.broadcasted_iota(jnp.int32, (tb, GROUP, WINDOW), 2)
    for j in range(N_KV):
        qj = q_ref[:, j * GROUP:(j + 1) * GROUP, :]
        kc = ck_ref[:, :, j * HEAD_DIM:(j + 1) * HEAD_DIM]
        vc = cv_ref[:, :, j * HEAD_DIM:(j + 1) * HEAD_DIM]
        s_c = jnp.einsum("bgd,bsd->bgs", qj.astype(BF16), kc.astype(BF16), preferred_element_type=F32) * ATTN_SCALE
        s_c = jnp.where(key_pos >= 1, s_c, NEG_INF)
        kn = kn_ref[:, j:j + 1, :]
        vn = vn_ref[:, j:j + 1, :]
        s_n = jnp.sum(qj * kn, axis=-1, keepdims=True) * ATTN_SCALE
        sink = sinks_ref[j][None]
        m = jnp.maximum(jnp.maximum(jnp.max(s_c, axis=-1, keepdims=True), s_n), sink)
        p_c = jnp.exp(s_c - m)
        p_n = jnp.exp(s_n - m)
        inv = 1.0 / (jnp.sum(p_c, axis=-1, keepdims=True) + p_n + jnp.exp(sink - m))
        o = jnp.einsum("bgs,bsd->bgd", (p_c * inv).astype(BF16), vc.astype(BF16), preferred_element_type=F32)
        att_ref[:, j * GROUP:(j + 1) * GROUP, :] = o + (p_n * inv) * vn
    kwin_ref[:, 0:WINDOW - 1, :] = ck_ref[:, 1:WINDOW, :]
    kwin_ref[:, WINDOW - 1:WINDOW, :] = kn_row_ref[...]
    vwin_ref[:, 0:WINDOW - 1, :] = cv_ref[:, 1:WINDOW, :]
    vwin_ref[:, WINDOW - 1:WINDOW, :] = vn_row_ref[...]


def _sample_attn_call(q3, kn3, vn3, kn_row, vn_row, ck, cv, sinks3):
    n = q3.shape[0]
    tb = SAMPLE_ATTN_TILE
    b3 = lambda i: (i, 0, 0)
    return pl.pallas_call(
        _sample_attn_kernel,
        grid=(n // tb,),
        in_specs=[pl.BlockSpec((tb, N_HEADS, HEAD_DIM), b3),
                  pl.BlockSpec((tb, N_KV, HEAD_DIM), b3),
                  pl.BlockSpec((tb, N_KV, HEAD_DIM), b3),
                  pl.BlockSpec((tb, 1, KV_WIDTH), b3),
                  pl.BlockSpec((tb, 1, KV_WIDTH), b3),
                  pl.BlockSpec((tb, WINDOW, KV_WIDTH), b3),
                  pl.BlockSpec((tb, WINDOW, KV_WIDTH), b3),
                  pl.BlockSpec((N_KV, GROUP, 1), lambda i: (0, 0, 0))],
        out_specs=(pl.BlockSpec((tb, N_HEADS, HEAD_DIM), b3),
                   pl.BlockSpec((tb, WINDOW, KV_WIDTH), b3),
                   pl.BlockSpec((tb, WINDOW, KV_WIDTH), b3)),
        out_shape=(jax.ShapeDtypeStruct((n, N_HEADS, HEAD_DIM), F32),
                   jax.ShapeDtypeStruct((n, WINDOW, KV_WIDTH), F32),
                   jax.ShapeDtypeStruct((n, WINDOW, KV_WIDTH), F32)),
        compiler_params=pltpu.CompilerParams(dimension_semantics=("arbitrary",)),
        name="sample_attn",
    )(q3, kn3, vn3, kn_row, vn_row, ck, cv, sinks3)


def _sample_mix_kernel(x_ref, u_ref, att_ref, st_ref, h0_ref, w_conv_ref, b_conv_ref, w_gates_ref, b_rg_ref,
                       b_ig_ref, lam_ref, w_out_ref, ln_g_ref, ln_b_ref,
                       x1_ref, conv_ref, h_ref):
    xl = u_ref[:, OFF_XL:OFF_XL + LRU_WIDTH]
    wc = w_conv_ref[...]
    xc = wc[0:1] * st_ref[0]
    xc = xc + wc[1:2] * st_ref[1]
    xc = xc + wc[2:3] * st_ref[2]
    xc = xc + wc[3:4] * xl + b_conv_ref[...]
    conv_ref[0] = st_ref[1]
    conv_ref[1] = st_ref[2]
    conv_ref[2] = xl
    a, u = _lru_gates(xc, w_gates_ref, b_rg_ref[...], b_ig_ref[...], lam_ref[...])
    h = a * h0_ref[...] + u
    h_ref[...] = h
    rec = h * jax.nn.gelu(u_ref[:, OFF_YL:OFF_YL + LRU_WIDTH])
    x1_ref[...] = _merge_norm(x_ref[...], rec, att_ref[...], u_ref[:, OFF_GL:OFF_GL + D_MODEL],
                              u_ref[:, OFF_GA:OFF_GA + D_MODEL], w_out_ref,
                              ln_g_ref[...], ln_b_ref[...])


def _sample_mix_call(x_s, u_s, att, st_t, h0, wts):
    n = x_s.shape[0]
    weight_args = (wts["w_conv"], wts["b_conv"], wts["w_gates"], wts["b_rg"], wts["b_ig"], wts["lam"],
                   wts["w_out"], wts["ln1_g"], wts["ln1_b"])
    args = (x_s, u_s, att, st_t, h0) + weight_args
    out_shapes = ((n, D_MODEL), (CONV_W - 1, n, LRU_WIDTH), (n, LRU_WIDTH))
    return pl.pallas_call(
        _sample_mix_kernel,
        grid=(1,),
        in_specs=[_const_spec(a.shape) for a in args],
        out_specs=tuple(_const_spec(s) for s in out_shapes),
        out_shape=tuple(jax.ShapeDtypeStruct(s, F32) for s in out_shapes),
        compiler_params=pltpu.CompilerParams(dimension_semantics=("arbitrary",),
                                             vmem_limit_bytes=VMEM_LIMIT_BYTES),
        name="sample_mix",
    )(*args)


def _one_hot(mask):
    return jnp.where(mask, 1.0, 0.0).astype(BF16)


def _pack_rows(x):
    half = x.shape[1] // 2
    lo = lax.shift_right_logical(lax.bitcast_convert_type(x[:, :half], jnp.uint32), jnp.uint32(16))
    hi = lax.bitcast_convert_type(x[:, half:], jnp.uint32) & jnp.uint32(0xFFFF0000)
    return lo | hi


def _unpack_rows(words):
    lo = lax.bitcast_convert_type(lax.shift_left(words, jnp.uint32(16)), F32)
    hi = lax.bitcast_convert_type(words & jnp.uint32(0xFFFF0000), F32)
    return jnp.concatenate([lo.astype(BF16), hi.astype(BF16)], axis=1)


class _Dispatcher:
    def __init__(self, xs_hbm, goff_ref, gtot_ref, stage, zero_rows, tri, xb_keep, key_keep, g_vmem, gvec,
                 g_smem, nd_smem, rounds_smem, sem_stage, sem_g, sem_zero, seg_cap):
        self.xs_hbm, self.goff_ref, self.gtot_ref = xs_hbm, goff_ref, gtot_ref
        self.stage, self.zero_rows, self.tri = stage, zero_rows, tri
        self.xb_keep, self.key_keep = xb_keep, key_keep
        self.g_vmem, self.gvec, self.g_smem = g_vmem, gvec, g_smem
        self.nd_smem, self.rounds_smem = nd_smem, rounds_smem
        self.sem_stage, self.sem_g, self.sem_zero = sem_stage, sem_g, sem_zero
        self.seg_cap = seg_cap

    def _g_copy(self, slot):
        return pltpu.make_async_copy(self.g_vmem.at[pl.ds(0, 1)], self.g_smem.at[pl.ds(slot, 1)], self.sem_g)

    def init(self):
        n_assign = self.tri.shape[0]
        r = lax.broadcasted_iota(jnp.int32, (n_assign, n_assign), 0)
        c = lax.broadcasted_iota(jnp.int32, (n_assign, n_assign), 1)
        self.tri[...] = _one_hot(r < c)
        self.gvec[...] = jnp.zeros_like(self.gvec)
        self.g_vmem[...] = jnp.zeros_like(self.g_vmem)
        self._g_copy(0).start()
        self.stage[1] = jnp.zeros(self.stage.shape[1:], self.stage.dtype)
        for e in range(N_EXPERTS):
            self._stage_copy(1, e, e * self.seg_cap).start()
        self.nd_smem[0] = N_EXPERTS

    def _sorted_rows(self, e_row, rank_row, xb, lo):
        td = xb.shape[0]
        slot = lax.broadcasted_iota(jnp.int32, (N_SLOTS, td), 0).astype(F32)
        lo_f = lo.astype(F32)
        in_round = (rank_row >= lo_f) & (rank_row < lo_f + SLOT_CAP) & (e_row >= 0.0)
        key = jnp.where(in_round, e_row * SLOT_CAP + (rank_row - lo_f), -1.0)
        perm = jnp.where(slot == key[:, :td], 1.0, jnp.where(slot == key[:, td:], 1.0, 0.0)).astype(BF16)
        return _pack_rows(jnp.dot(perm, xb, preferred_element_type=F32))

    def plan(self, x1, route, n_valid, buf):
        td = x1.shape[0]
        n_assign = TOP_K * td
        valid_col = lax.broadcasted_iota(jnp.int32, (td, 1), 0) < n_valid
        lane = lax.broadcasted_iota(jnp.int32, (td, ROUTE_WIDTH), 1).astype(F32)
        e0c = jnp.where(valid_col, route[:, 0:1], -1.0)
        e1c = jnp.where(valid_col, route[:, 1:2], -1.0)
        cnt_row = jnp.sum(jnp.where(lane == e0c, 1.0, 0.0) + jnp.where(lane == e1c, 1.0, 0.0),
                          axis=0, keepdims=True)
        self.rounds_smem[0] = (jnp.max(cnt_row).astype(jnp.int32) + SLOT_CAP - 1) // SLOT_CAP
        self.goff_ref[0] = self.gvec[0:1, :].astype(jnp.int32)
        self.gvec[0:1, :] = self.gvec[0:1, :] + jnp.ceil(cnt_row * (1.0 / SUBLANES)) * SUBLANES
        self.gtot_ref[...] = self.gvec[0:1, :].astype(jnp.int32)

        route_t = route.T
        valid_row = lax.broadcasted_iota(jnp.int32, (1, td), 1) < n_valid
        e_row = jnp.concatenate([jnp.where(valid_row, route_t[0:1, :], -1.0),
                                 jnp.where(valid_row, route_t[1:2, :], -1.0)], axis=1)
        expert_sub = lax.broadcasted_iota(jnp.int32, (N_EXPERTS, n_assign), 0).astype(F32)
        onehot_t = expert_sub == e_row
        rank_t = jnp.dot(_one_hot(onehot_t), self.tri[...], preferred_element_type=F32)
        rank_row = jnp.sum(jnp.where(onehot_t, rank_t, 0.0), axis=0, keepdims=True)
        xb = jnp.where(valid_col, x1, 0.0).astype(BF16)
        self.xb_keep[...] = xb
        self.key_keep[0:1, :] = e_row
        self.key_keep[1:2, :] = rank_row
        self.stage[buf] = self._sorted_rows(e_row, rank_row, xb, jnp.int32(0))

    def _stage_copy(self, buf, e, dst_row):
        src = self.stage.at[buf, pl.ds(e * SLOT_CAP, SLOT_CAP)]
        return pltpu.make_async_copy(src, self.xs_hbm.at[pl.ds(dst_row, SLOT_CAP)], self.sem_stage)

    def _wait_outstanding(self):
        @pl.when(self.nd_smem[0] > 0)
        def _():
            for _ in range(N_EXPERTS):
                self._stage_copy(0, 0, 0).wait()
        self.nd_smem[0] = 0

    def _start_round(self, buf, lo):
        for e in range(N_EXPERTS):
            dst_row = pl.multiple_of(e * self.seg_cap + self.g_smem[buf, e] + lo, SUBLANES)
            self._stage_copy(buf, e, dst_row).start()
        self.nd_smem[0] = N_EXPERTS

    def start_copies(self, buf):
        self._g_copy(buf).wait()
        for _ in range(N_EXPERTS):
            self._stage_copy(0, 0, 0).wait()
        self._start_round(buf, 0)
        self.g_vmem[0:1, :] = self.gvec[0:1, :].astype(jnp.int32)
        self._g_copy(1 - buf).start()

    def finish_step(self, buf, is_last):
        def later_round(r, carry):
            lo = r * SLOT_CAP
            rows = self._sorted_rows(self.key_keep[0:1, :], self.key_keep[1:2, :], self.xb_keep[...], lo)
            self._wait_outstanding()
            self.stage[buf] = rows
            self._start_round(buf, lo)
            return carry
        lax.fori_loop(1, self.rounds_smem[0], later_round, 0)

        @pl.when(is_last)
        def _():
            self._g_copy(1 - buf).wait()
            self._wait_outstanding()
            self.zero_rows[...] = jnp.zeros_like(self.zero_rows)

            def pad_copy(dst_row):
                dst = self.xs_hbm.at[pl.ds(pl.multiple_of(dst_row, SUBLANES), PAD_CHUNK)]
                return pltpu.make_async_copy(self.zero_rows, dst, self.sem_zero)

            def pad_segment(e, n_started):
                fill = self.g_smem[1 - buf, e]
                block_end = (fill + SLOT_CAP + MOE_TILE - 1) // MOE_TILE * MOE_TILE
                n_chunks = (block_end - fill + PAD_CHUNK - 1) // PAD_CHUNK

                def start(c, carry):
                    pad_copy(e * self.seg_cap + fill + c * PAD_CHUNK).start()
                    return carry
                lax.fori_loop(0, n_chunks, start, 0)
                return n_started + n_chunks
            n_started = lax.fori_loop(0, N_EXPERTS, pad_segment, 0)

            def wait(_, carry):
                pad_copy(0).wait()
                return carry
            lax.fori_loop(0, n_started, wait, 0)


def _dispatch_tiles(n_tokens):
    return (n_tokens + DISPATCH_TILE - 1) // DISPATCH_TILE


def _segment_capacity(n_tokens):
    assert SEG_PAD >= DISPATCH_TILE
    alignment_slack = (SUBLANES - 1) * _dispatch_tiles(n_tokens)
    return (n_tokens + alignment_slack + SEG_PAD + MOE_TILE - 1) // MOE_TILE * MOE_TILE


def _moe_kernel(blk_e_ref, blk_j_ref, n_used_ref, xs_ref, wg_ref, wu_ref, wd_ref, ys_ref, wg_bf, wu_bf, wd_bf):
    del blk_j_ref
    b = pl.program_id(0)

    @pl.when(b < n_used_ref[0])
    def _():
        prev_e = blk_e_ref[jnp.maximum(b - 1, 0)]

        @pl.when((b == 0) | (blk_e_ref[b] != prev_e))
        def _():
            wg_bf[...] = wg_ref[0].astype(BF16)
            wu_bf[...] = wu_ref[0].astype(BF16)
            wd_bf[...] = wd_ref[0].astype(BF16)

        xb = _unpack_rows(xs_ref[...])
        g = jnp.dot(xb, wg_bf[...], preferred_element_type=F32)
        u = jnp.dot(xb, wu_bf[...], preferred_element_type=F32)
        hmid = (g * _sigmoid(g)) * u
        y = jnp.dot(hmid.astype(BF16), wd_bf[...], preferred_element_type=F32)
        ys_ref[...] = _pack_rows(y.astype(BF16).astype(F32))


def _expert_blocks(gtot, n_blocks):
    rows = gtot[0, :N_EXPERTS]
    nb = (rows + SLOT_CAP + MOE_TILE - 1) // MOE_TILE
    ends = jnp.cumsum(nb)
    n_used = ends[-1]
    step = jnp.minimum(jnp.arange(n_blocks, dtype=jnp.int32), n_used - 1)
    blk_e = jnp.sum((step[:, None] >= ends[None, :]).astype(jnp.int32), axis=1)
    blk_j = step - (ends - nb)[blk_e]
    return blk_e, blk_j.astype(jnp.int32), n_used.reshape(1).astype(jnp.int32)


def _moe_call(gtot, xs, w_gate, w_up, w_down, n_tokens):
    seg_cap = _segment_capacity(n_tokens)
    seg_blocks = seg_cap // MOE_TILE
    max_rows = n_tokens * TOP_K + N_EXPERTS * (SUBLANES - 1) * _dispatch_tiles(n_tokens)
    n_blocks = (max_rows + N_EXPERTS * (SLOT_CAP + MOE_TILE - 1)) // MOE_TILE
    blk_e, blk_j, n_used = _expert_blocks(gtot, n_blocks)
    row_block = lambda b, be, bj, nu: (be[b] * seg_blocks + bj[b], 0)
    weight_block = lambda b, be, bj, nu: (be[b], 0, 0)
    grid_spec = pltpu.PrefetchScalarGridSpec(
        num_scalar_prefetch=3,
        grid=(n_blocks,),
        in_specs=[
            pl.BlockSpec((MOE_TILE, PACKED_WIDTH), row_block),
            pl.BlockSpec((1, D_MODEL, EXPERT_FF), weight_block),
            pl.BlockSpec((1, D_MODEL, EXPERT_FF), weight_block),
            pl.BlockSpec((1, EXPERT_FF, D_MODEL), weight_block),
        ],
        out_specs=pl.BlockSpec((MOE_TILE, PACKED_WIDTH), row_block),
        scratch_shapes=[
            pltpu.VMEM((D_MODEL, EXPERT_FF), BF16),
            pltpu.VMEM((D_MODEL, EXPERT_FF), BF16),
            pltpu.VMEM((EXPERT_FF, D_MODEL), BF16),
        ],
    )
    return pl.pallas_call(
        _moe_kernel,
        grid_spec=grid_spec,
        out_shape=jax.ShapeDtypeStruct(xs.shape, xs.dtype),
        compiler_params=pltpu.CompilerParams(dimension_semantics=("arbitrary",),
                                             vmem_limit_bytes=VMEM_LIMIT_BYTES),
        name="moe_experts",
    )(blk_e, blk_j, n_used, xs, w_gate, w_up, w_down)


def _combine_kernel(gcur_ref, gnext_ref, x1_ref, route_ref, ys_hbm, ln_g_ref, ln_b_ref, y_ref,
                    stage, acc, tri, sem, *, seg_cap):
    td = x1_ref.shape[0]
    n_assign = TOP_K * td
    i = pl.program_id(0)
    buf = lax.rem(i, 2)

    def stage_copy(g_ref, e, first_rank, to_buf):
        src_row = pl.multiple_of(e * seg_cap + g_ref[0, 0, e] + first_rank, SUBLANES)
        slot0 = e * SLOT_CAP
        if not isinstance(slot0, int):
            slot0 = pl.multiple_of(slot0, SLOT_CAP)
        dst = stage.at[to_buf, pl.ds(slot0, SLOT_CAP)]
        return pltpu.make_async_copy(ys_hbm.at[pl.ds(src_row, SLOT_CAP)], dst, sem.at[to_buf])

    def fetch_first_round(g_ref, to_buf):
        for e in range(N_EXPERTS):
            stage_copy(g_ref, e, 0, to_buf).start()

    @pl.when(i == 0)
    def _():
        r = lax.broadcasted_iota(jnp.int32, (n_assign, n_assign), 0)
        c = lax.broadcasted_iota(jnp.int32, (n_assign, n_assign), 1)
        tri[...] = _one_hot(c < r)
        fetch_first_round(gcur_ref, 0)

    def count(e):
        return gnext_ref[0, 0, e] - gcur_ref[0, 0, e]

    route = route_ref[...]
    e0, e1, w0, w1 = route[:, 0:1], route[:, 1:2], route[:, 2:3], route[:, 3:4]
    lane = lax.broadcasted_iota(jnp.int32, (td, ROUTE_WIDTH), 1).astype(F32)
    onehot = jnp.concatenate([lane == e0, lane == e1], axis=0)
    onehot_f = jnp.where(onehot, 1.0, 0.0)
    rank_mat = jnp.dot(tri[...], onehot_f.astype(BF16), preferred_element_type=F32)
    rank = jnp.sum(jnp.where(onehot, rank_mat, 0.0), axis=1, keepdims=True)
    r0, r1 = rank[:td], rank[td:]
    cnt_row = jnp.sum(onehot_f, axis=0, keepdims=True)
    rounds = (jnp.max(cnt_row).astype(jnp.int32) + SLOT_CAP - 1) // SLOT_CAP

    slot_lane = lax.broadcasted_iota(jnp.int32, (td, N_SLOTS), 1).astype(F32)

    def wait_copies(n):
        def wait_one(_, c):
            stage_copy(gcur_ref, 0, 0, buf).wait()
            return c
        lax.fori_loop(0, n, wait_one, 0)

    def weighted_rows(lo):
        lo_f = lo.astype(F32)
        rows = _unpack_rows(stage[buf])

        def selector(e_col, r_col, w_col):
            in_round = (r_col >= lo_f) & (r_col < lo_f + SLOT_CAP)
            key = jnp.where(in_round, e_col * SLOT_CAP + (r_col - lo_f), -1.0)
            return jnp.where(slot_lane == key, w_col, 0.0)
        mix = (selector(e0, r0, w0) + selector(e1, r1, w1)).astype(BF16)
        return jnp.dot(mix, rows, preferred_element_type=F32)

    for _ in range(N_EXPERTS):
        stage_copy(gcur_ref, 0, 0, buf).wait()

    fetch_first_round(gnext_ref, 1 - buf)

    acc[...] = weighted_rows(jnp.int32(0))

    def later_round(r, carry):
        lo = r * SLOT_CAP

        def start(e, n_started):
            has_rows = count(e) > lo

            @pl.when(has_rows)
            def _():
                stage_copy(gcur_ref, e, lo, buf).start()
            return n_started + has_rows.astype(jnp.int32)
        wait_copies(lax.fori_loop(0, N_EXPERTS, start, 0))
        acc[...] = acc[...] + weighted_rows(lo)
        return carry
    lax.fori_loop(1, rounds, later_round, 0)

    y_ref[...] = _layer_norm(ALPHA * x1_ref[...] + acc[...], ln_g_ref[...], ln_b_ref[...])

    @pl.when(i + 1 == pl.num_programs(0))
    def _():
        for _ in range(N_EXPERTS):
            stage_copy(gnext_ref, 0, 0, 1 - buf).wait()


def _combine_call(goff, x1_all, route_all, ys, ln_g, ln_b, first_row, n_rows, tile, n_tokens):
    assert first_row % DISPATCH_TILE == 0 and first_row % tile == 0
    assert tile == DISPATCH_TILE or n_rows == tile
    first_block = first_row // tile
    first_goff = first_row // DISPATCH_TILE
    n_assign = TOP_K * tile
    grid_spec = pl.GridSpec(
        grid=(n_rows // tile,),
        in_specs=[
            pl.BlockSpec((1, 1, ROUTE_WIDTH), lambda i: (first_goff + i, 0, 0), memory_space=pltpu.SMEM),
            pl.BlockSpec((1, 1, ROUTE_WIDTH), lambda i: (first_goff + i + 1, 0, 0), memory_space=pltpu.SMEM),
            pl.BlockSpec((tile, D_MODEL), lambda i: (first_block + i, 0)),
            pl.BlockSpec((tile, ROUTE_WIDTH), lambda i: (first_block + i, 0)),
            pl.BlockSpec(memory_space=pl.ANY),
            _const_spec(ln_g.shape),
            _const_spec(ln_b.shape),
        ],
        out_specs=pl.BlockSpec((tile, D_MODEL), lambda i: (i, 0)),
        scratch_shapes=[pltpu.VMEM((2, N_SLOTS, PACKED_WIDTH), jnp.uint32),
                        pltpu.VMEM((tile, D_MODEL), F32),
                        pltpu.VMEM((n_assign, n_assign), BF16),
                        pltpu.SemaphoreType.DMA((2,))],
    )
    return pl.pallas_call(
        functools.partial(_combine_kernel, seg_cap=_segment_capacity(n_tokens)),
        grid_spec=grid_spec,
        out_shape=jax.ShapeDtypeStruct((n_rows, D_MODEL), F32),
        compiler_params=pltpu.CompilerParams(dimension_semantics=("arbitrary",),
                                             vmem_limit_bytes=VMEM_LIMIT_BYTES),
        name="moe_combine",
    )(goff, goff, x1_all, route_all, ys, ln_g, ln_b)


def _prepare_weights(w_in, b_in, w_conv, b_conv, w_rg, b_rg, w_ig, b_ig, lru_lambda, w_lru_out, w_attn_out, w_o,
                     ln1_g, ln1_b, w_group, b_group, w_router, b_router):
    blocks_per_chunk = GATE_CHUNK // LRU_BLOCK

    def chunked_block_diag(w):
        w = w.reshape(N_GATE_CHUNKS, blocks_per_chunk, LRU_BLOCK, LRU_BLOCK)
        eye = jnp.eye(blocks_per_chunk, dtype=w.dtype)
        return jnp.einsum("cbij,bd->cbidj", w, eye).reshape(N_GATE_CHUNKS, GATE_CHUNK, GATE_CHUNK)

    w_gates = jnp.concatenate([chunked_block_diag(w_rg), chunked_block_diag(w_ig)], axis=-1).astype(BF16)
    w_rt = jnp.concatenate([w_group, w_router], axis=1)
    w_rt = jnp.pad(w_rt, ((0, 0), (0, ROUTE_WIDTH - w_rt.shape[1])))
    w_rt_hi = w_rt.astype(BF16)
    w_rt_lo = jnp.concatenate([w_rt_hi, (w_rt - w_rt_hi.astype(F32)).astype(BF16)], axis=1)
    b_rt = jnp.pad(jnp.concatenate([b_group, b_router]), (0, ROUTE_WIDTH - N_GROUPS - N_EXPERTS))
    row = lambda v: v.reshape(1, -1)
    return dict(
        w_in=w_in.astype(BF16), b_in=row(b_in), w_conv=w_conv, b_conv=row(b_conv), w_gates=w_gates,
        b_rg=row(b_rg), b_ig=row(b_ig), lam=row(lru_lambda),
        w_out=jnp.stack([w_lru_out, w_attn_out, w_o]).astype(BF16),
        ln1_g=row(ln1_g), ln1_b=row(ln1_b), w_rt_hi=w_rt_hi, w_rt_lo=w_rt_lo, b_rt=row(b_rt))


def kernel(x_prompt, x_sample, cache_k, cache_v, state_conv, state_lru_h, w_in, b_in, w_conv, b_conv, w_rg, b_rg,
           w_ig, b_ig, lru_lambda, sinks, w_lru_out, w_attn_out, w_o, ln1_g, ln1_b, w_group, b_group, w_router,
           b_router, w_gate, w_up, w_down, ln2_g, ln2_b):
    B, S, _ = x_prompt.shape
    n_prompt = B * S
    n_sample = x_sample.shape[0]
    n_all = n_prompt + n_sample
    wts = _prepare_weights(w_in, b_in, w_conv, b_conv, w_rg, b_rg, w_ig, b_ig, lru_lambda, w_lru_out, w_attn_out,
                           w_o, ln1_g, ln1_b, w_group, b_group, w_router, b_router)

    x_s = x_sample.reshape(n_sample, D_MODEL)
    u_s = _sample_proj_call(x_s, wts["w_in"], wts["b_in"])
    q3 = u_s[:, OFF_Q:OFF_K].reshape(n_sample, N_HEADS, HEAD_DIM)
    k_new = u_s[:, OFF_K:OFF_V]
    v_new = u_s[:, OFF_V:OFF_GL]
    att3, k_win_s, v_win_s = _sample_attn_call(
        q3, k_new.reshape(n_sample, N_KV, HEAD_DIM), v_new.reshape(n_sample, N_KV, HEAD_DIM),
        k_new.reshape(n_sample, 1, KV_WIDTH), v_new.reshape(n_sample, 1, KV_WIDTH),
        cache_k.reshape(n_sample, WINDOW, KV_WIDTH), cache_v.reshape(n_sample, WINDOW, KV_WIDTH),
        sinks.reshape(N_KV, GROUP, 1))
    x1_s, conv_s_t, h_s = _sample_mix_call(
        x_s, u_s, att3.reshape(n_sample, N_HEADS * HEAD_DIM), jnp.transpose(state_conv, (1, 0, 2)), state_lru_h, wts)

    x1_all, route_all, k_win_p, v_win_p, conv_p, h_p, xs, goff, gtot = _mixer_call(x_prompt, x1_s, sinks, wts)

    ys = _moe_call(gtot, xs, w_gate, w_up, w_down, n_all)
    goff = jnp.concatenate([goff, gtot[None]], axis=0)
    ln2_g2, ln2_b2 = ln2_g.reshape(1, -1), ln2_b.reshape(1, -1)
    y_p = _combine_call(goff, x1_all, route_all, ys, ln2_g2, ln2_b2, 0, n_prompt, DISPATCH_TILE, n_all)
    y_s = _combine_call(goff, x1_all, route_all, ys, ln2_g2, ln2_b2, n_prompt, n_sample, n_sample, n_all)

    kv_shape = (WINDOW, N_KV, HEAD_DIM)
    return (y_p.reshape(B, S, D_MODEL), y_s.reshape(n_sample, 1, D_MODEL),
            k_win_p.reshape((B,) + kv_shape), v_win_p.reshape((B,) + kv_shape), conv_p, h_p.reshape(B, LRU_WIDTH),
            k_win_s.reshape((n_sample,) + kv_shape), v_win_s.reshape((n_sample,) + kv_shape),
            jnp.transpose(conv_s_t, (1, 0, 2)), h_s)
```

```python
import functools

import jax
import jax.numpy as jnp
from jax import lax
from jax.experimental import pallas as pl
from jax.experimental.pallas import tpu as pltpu

F32 = jnp.float32
BF16 = jnp.bfloat16

D_MODEL = 1024
LRU_WIDTH = 1024
LRU_BLOCK = 64
CONV_W = 4
LRU_C = 8.0
N_HEADS = 16
N_KV = 4
GROUP = N_HEADS // N_KV
HEAD_DIM = 64
KV_WIDTH = N_KV * HEAD_DIM
WINDOW = 128
NEG_INF = -1e30
N_GROUPS = 4
EXPERTS_PER_GROUP = 8
N_EXPERTS = N_GROUPS * EXPERTS_PER_GROUP
TOP_K = 2
EXPERT_FF = D_MODEL // 2
DEPTH = 1
ALPHA = (2 * DEPTH) ** 0.25
LN_EPS = 1e-5
ATTN_SCALE = HEAD_DIM ** -0.5
LOG2_E = 1.4426950408889634

OFF_XL = 0
OFF_YL = OFF_XL + LRU_WIDTH
OFF_Q = OFF_YL + LRU_WIDTH
OFF_K = OFF_Q + N_HEADS * HEAD_DIM
OFF_V = OFF_K + KV_WIDTH
OFF_GL = OFF_V + KV_WIDTH
OFF_GA = OFF_GL + D_MODEL
IN_WIDTH = OFF_GA + D_MODEL

LANES = 128
SUBLANES = 8
MXU_DIM = 256
VMEM_LIMIT_BYTES = 56 * 1024 * 1024

GATE_CHUNK = MXU_DIM
N_GATE_CHUNKS = LRU_WIDTH // GATE_CHUNK
ROUTE_WIDTH = LANES

SEQ_TILE = 256
MOE_TILE = 1392
DISPATCH_TILE = 256
SLOT_CAP = 32
N_SLOTS = N_EXPERTS * SLOT_CAP
PAD_CHUNK = 128
SEG_PAD = MOE_TILE + SLOT_CAP + PAD_CHUNK
PACKED_WIDTH = D_MODEL // 2
SAMPLE_ATTN_TILE = 16
SAMPLE_PROJ_TILE = 512


def _const_spec(shape):
    nd = len(shape)
    return pl.BlockSpec(shape, lambda *_: (0,) * nd)


def _layer_norm(z, g, b):
    mu = jnp.mean(z, axis=-1, keepdims=True)
    zc = z - mu
    var = jnp.mean(zc * zc, axis=-1, keepdims=True)
    return zc * lax.rsqrt(var + LN_EPS) * g + b


def _sigmoid(x):
    return 1.0 / (1.0 + jnp.exp2(x * -LOG2_E))


def _softplus(x):
    return jnp.maximum(x, 0.0) + jnp.log1p(jnp.exp(-jnp.abs(x)))


def _lru_gates(xc, w_gates_ref, b_rg, b_ig, lam):
    xcb = xc.astype(BF16)
    r_parts, i_parts = [], []
    for c in range(N_GATE_CHUNKS):
        g = jnp.dot(xcb[:, c * GATE_CHUNK:(c + 1) * GATE_CHUNK], w_gates_ref[c], preferred_element_type=F32)
        r_parts.append(g[:, :GATE_CHUNK])
        i_parts.append(g[:, GATE_CHUNK:])
    r = _sigmoid(jnp.concatenate(r_parts, axis=1) + b_rg)
    i = _sigmoid(jnp.concatenate(i_parts, axis=1) + b_ig)
    log_a = (-LRU_C * r) * _softplus(-lam)
    a = jnp.exp(log_a)
    gain_sq = 1.0 - a * a
    gain = jnp.where(gain_sq > 0.0, gain_sq * lax.rsqrt(gain_sq), 0.0)
    u = gain * (i * xc)
    return a, u


def _linear_scan(a, u, h_in):
    n, w = a.shape
    groups = n // SUBLANES
    a3 = a.reshape(groups, SUBLANES, w)
    u3 = u.reshape(groups, SUBLANES, w)
    row = lax.broadcasted_iota(jnp.int32, a3.shape, 1)
    d = 1
    while d < SUBLANES:
        has_prev = row >= d
        u3 = u3 + a3 * jnp.where(has_prev, pltpu.roll(u3, d, axis=1), 0.0)
        a3 = a3 * jnp.where(has_prev, pltpu.roll(a3, d, axis=1), 1.0)
        d *= 2
    carry = h_in
    out = []
    for g in range(groups):
        h_g = u3[g] + a3[g] * carry
        out.append(h_g)
        carry = h_g[SUBLANES - 1:SUBLANES, :]
    return jnp.concatenate(out, axis=0)


def _route(x1, w_hi_ref, w_hilo_ref, b_rt):
    x_hi = x1.astype(BF16)
    x_lo = (x1 - x_hi.astype(F32)).astype(BF16)
    both = jnp.dot(x_hi, w_hilo_ref[...], preferred_element_type=F32)
    logits = (both[:, :ROUTE_WIDTH]
              + (jnp.dot(x_lo, w_hi_ref[...], preferred_element_type=F32) + both[:, ROUTE_WIDTH:])) + b_rt
    col = lax.broadcasted_iota(jnp.int32, logits.shape, 1)
    big = jnp.int32(ROUTE_WIDTH)
    is_g = col < N_GROUPS
    gl = jnp.where(is_g, logits, -jnp.inf)
    gmax = jnp.max(gl, axis=-1, keepdims=True)
    g_idx = jnp.min(jnp.where(gl == gmax, col, big), axis=-1, keepdims=True)
    p_g = 1.0 / jnp.sum(jnp.where(is_g, jnp.exp(gl - gmax), 0.0), axis=-1, keepdims=True)
    lo = N_GROUPS + g_idx * EXPERTS_PER_GROUP
    in_grp = (col >= lo) & (col < lo + EXPERTS_PER_GROUP)
    el = jnp.where(in_grp, logits, -jnp.inf)
    v1 = jnp.max(el, axis=-1, keepdims=True)
    i1 = jnp.min(jnp.where(el == v1, col, big), axis=-1, keepdims=True)
    el2 = jnp.where(col == i1, -jnp.inf, el)
    v2 = jnp.max(el2, axis=-1, keepdims=True)
    i2 = jnp.min(jnp.where(el2 == v2, col, big), axis=-1, keepdims=True)
    e21 = jnp.exp(v2 - v1)
    inv = 1.0 / (1.0 + e21)
    w1 = p_g * inv
    w2 = p_g * (e21 * inv)
    e1 = (i1 - N_GROUPS).astype(F32)
    e2 = (i2 - N_GROUPS).astype(F32)
    return jnp.where(col == 0, e1, jnp.where(col == 1, e2, jnp.where(col == 2, w1, jnp.where(col == 3, w2, 0.0))))


def _merge_norm(x, rec, att, g_l, g_a, w_out_ref, ln_g, ln_b):
    rec_o = jnp.dot(rec.astype(BF16), w_out_ref[0], preferred_element_type=F32)
    att_o = jnp.dot(att.astype(BF16), w_out_ref[1], preferred_element_type=F32)
    merged = _sigmoid(g_l) * rec_o + _sigmoid(g_a) * att_o
    mix = jnp.dot(merged.astype(BF16), w_out_ref[2], preferred_element_type=F32)
    return _layer_norm(ALPHA * x + mix, ln_g, ln_b)


def _mixer_kernel(sinks_ref, x_ref, w_in_ref, b_in_ref, w_conv_ref, b_conv_ref, w_gates_ref, b_rg_ref, b_ig_ref,
                  lam_ref, w_out_ref, ln_g_ref, ln_b_ref, w_rt_hi_ref, w_rt_lo_ref,
                  b_rt_ref, x1_s_ref,
                  x1_ref, route_ref, kwin_ref, vwin_ref, conv_ref, h_ref, xs_hbm, goff_ref, gtot_ref,
                  conv_buf, h_carry, kcat, vcat, att_buf, prev_x1, stage, zero_rows, tri, xb_keep, key_keep,
                  g_vmem, gvec, g_smem, nd_smem, rounds_smem, sem_stage, sem_g, sem_zero,
                  *, tiles_per_seq, n_tiles, seg_cap):
    step = pl.program_id(0)
    last_step = pl.num_programs(0) - 1
    buf = lax.rem(step, 2)
    n_sample = x1_s_ref.shape[0]
    disp = _Dispatcher(xs_hbm, goff_ref, gtot_ref, stage, zero_rows, tri, xb_keep, key_keep, g_vmem, gvec,
                       g_smem, nd_smem, rounds_smem, sem_stage, sem_g, sem_zero, seg_cap)
    n_valid = jnp.where(step == 0, 0, jnp.where(step == last_step, n_sample, SEQ_TILE))

    @pl.when(step == 0)
    def _():
        disp.init()
        prev_x1[...] = jnp.zeros_like(prev_x1)

    def route_and_plan():
        x1_prev = prev_x1[...]
        route = _route(x1_prev, w_rt_hi_ref, w_rt_lo_ref, b_rt_ref[...])
        route_ref[...] = route
        disp.plan(x1_prev, route, n_valid, buf)

    @pl.when(step < n_tiles)
    def _():
        @pl.when(lax.rem(step, tiles_per_seq) == 0)
        def _():
            conv_buf[...] = jnp.zeros_like(conv_buf)
            h_carry[...] = jnp.zeros_like(h_carry)
            kcat[0:WINDOW, :] = jnp.zeros((WINDOW, KV_WIDTH), BF16)
            vcat[0:WINDOW, :] = jnp.zeros((WINDOW, KV_WIDTH), BF16)

        route_and_plan()
        _mixer_tile(lax.rem(step, tiles_per_seq), sinks_ref, x_ref, w_in_ref, b_in_ref, w_conv_ref, b_conv_ref,
                    w_gates_ref, b_rg_ref, b_ig_ref, lam_ref, w_out_ref, ln_g_ref,
                    ln_b_ref, x1_ref, prev_x1, kwin_ref, vwin_ref, conv_ref, h_ref,
                    conv_buf, h_carry, kcat, vcat, att_buf)

    @pl.when(step >= n_tiles)
    def _():
        route_and_plan()

        @pl.when(step == n_tiles)
        def _():
            x1_ref[0:n_sample, :] = x1_s_ref[...]
            prev_x1[0:n_sample, :] = x1_s_ref[...]

    disp.flush(buf, step == last_step)


def _mixer_tile(t, sinks_ref, x_ref, w_in_ref, b_in_ref, w_conv_ref, b_conv_ref, w_gates_ref, b_rg_ref, b_ig_ref,
                lam_ref, w_out_ref, ln_g_ref, ln_b_ref,
                x1_ref, x1_keep, kwin_ref, vwin_ref, conv_ref, h_ref,
                conv_buf, h_carry, kcat, vcat, att_buf):
    T = SEQ_TILE
    x = x_ref[0]
    xb = x.astype(BF16)

    def proj(lo, width):
        return jnp.dot(xb, w_in_ref[:, lo:lo + width], preferred_element_type=F32) + b_in_ref[:, lo:lo + width]

    xl = proj(OFF_XL, LRU_WIDTH)
    xl_ext = jnp.concatenate([conv_buf[...], xl], axis=0)

    def lagged(k):
        return pltpu.roll(xl_ext, k, axis=0)[SUBLANES:, :]
    wc = w_conv_ref[...]
    xc = wc[0:1] * lagged(3)
    xc = xc + wc[1:2] * lagged(2)
    xc = xc + wc[2:3] * lagged(1)
    xc = xc + wc[3:4] * xl + b_conv_ref[...]
    conv_ref[0] = xl[T - (CONV_W - 1):, :]
    conv_buf[...] = xl[T - SUBLANES:, :]

    a, u = _lru_gates(xc, w_gates_ref, b_rg_ref[...], b_ig_ref[...], lam_ref[...])
    h = _linear_scan(a, u, h_carry[0:1, :])
    h_last = h[T - 1:T, :]
    h_carry[0:1, :] = h_last
    h_ref[0] = h_last
    rec = h * jax.nn.gelu(proj(OFF_YL, LRU_WIDTH))

    q = proj(OFF_Q, N_HEADS * HEAD_DIM) * (ATTN_SCALE * LOG2_E)
    k = proj(OFF_K, KV_WIDTH)
    v = proj(OFF_V, KV_WIDTH)
    kwin_ref[0] = k[T - WINDOW:, :]
    vwin_ref[0] = v[T - WINDOW:, :]
    kcat[WINDOW:WINDOW + T, :] = k.astype(BF16)
    vcat[WINDOW:WINDOW + T, :] = v.astype(BF16)

    qi = lax.broadcasted_iota(jnp.int32, (WINDOW, 2 * WINDOW), 0)
    kj = lax.broadcasted_iota(jnp.int32, (WINDOW, 2 * WINDOW), 1)
    band = (kj > qi) & (kj <= qi + WINDOW)
    grp_row = lax.broadcasted_iota(jnp.int32, (GROUP * WINDOW, 1), 0) // WINDOW
    for qb in range(T // WINDOW):
        if qb == 0:
            first_key = jnp.where(t == 0, WINDOW, 0)
            mask1 = band & (kj >= first_key)
        else:
            mask1 = band
        bias = jnp.concatenate([jnp.where(mask1, 0.0, NEG_INF)] * GROUP, axis=0)
        r0 = qb * WINDOW
        qq = q[r0:r0 + WINDOW, :]
        for j in range(N_KV):
            kjb = kcat[r0:r0 + 2 * WINDOW, j * HEAD_DIM:(j + 1) * HEAD_DIM]
            vjb = vcat[r0:r0 + 2 * WINDOW, j * HEAD_DIM:(j + 1) * HEAD_DIM]
            qs = jnp.concatenate(
                [qq[:, (j * GROUP + g) * HEAD_DIM:(j * GROUP + g + 1) * HEAD_DIM] for g in range(GROUP)], axis=0)
            s = lax.dot_general(qs.astype(BF16), kjb, (((1,), (1,)), ((), ())), preferred_element_type=F32) + bias
            sink = jnp.zeros((GROUP * WINDOW, 1), F32)
            for g in range(GROUP):
                sink = jnp.where(grp_row == g, sinks_ref[j * GROUP + g] * LOG2_E, sink)
            m = jnp.maximum(jnp.max(s, axis=-1, keepdims=True), sink)
            p = jnp.exp2(s - m)
            inv = 1.0 / (jnp.sum(p, axis=-1, keepdims=True) + jnp.exp2(sink - m))
            o = jnp.dot((p * inv).astype(BF16), vjb, preferred_element_type=F32)
            for g in range(GROUP):
                hcol = (j * GROUP + g) * HEAD_DIM
                att_buf[r0:r0 + WINDOW, hcol:hcol + HEAD_DIM] = o[g * WINDOW:(g + 1) * WINDOW, :]
    kcat[0:WINDOW, :] = kcat[T:T + WINDOW, :]
    vcat[0:WINDOW, :] = vcat[T:T + WINDOW, :]

    x1 = _merge_norm(x, rec, att_buf[...], proj(OFF_GL, D_MODEL), proj(OFF_GA, D_MODEL),
                     w_out_ref, ln_g_ref[...], ln_b_ref[...])
    x1_ref[...] = x1
    x1_keep[...] = x1


def _mixer_call(x_prompt, x1_s, sinks, wts):
    B, S, _ = x_prompt.shape
    T = SEQ_TILE
    assert T == DISPATCH_TILE and x1_s.shape[0] <= T
    nt = S // T
    n_tiles = B * nt
    n_rows_total = B * S + x1_s.shape[0]
    n_dispatch = _dispatch_tiles(n_rows_total)
    assert n_dispatch == n_tiles + 1
    seg_cap = _segment_capacity(n_rows_total)
    n_assign = TOP_K * T
    weight_args = (wts["w_in"], wts["b_in"], wts["w_conv"], wts["b_conv"], wts["w_gates"], wts["b_rg"], wts["b_ig"],
                   wts["lam"], wts["w_out"], wts["ln1_g"], wts["ln1_b"],
                   wts["w_rt_hi"], wts["w_rt_lo"], wts["b_rt"], x1_s)
    mixed = lambda i: jnp.minimum(i, n_tiles - 1)
    seq = lambda i: mixed(i) // nt
    routed = lambda i: jnp.clip(i - 1, 0, n_dispatch - 1)
    in_specs = [pl.BlockSpec(memory_space=pltpu.SMEM),
                pl.BlockSpec((1, T, D_MODEL), lambda i: (seq(i), lax.rem(mixed(i), nt), 0))]
    in_specs += [_const_spec(w.shape) for w in weight_args]
    out_shape = (
        jax.ShapeDtypeStruct((n_rows_total, D_MODEL), F32),
        jax.ShapeDtypeStruct((n_rows_total, ROUTE_WIDTH), F32),
        jax.ShapeDtypeStruct((B, WINDOW, KV_WIDTH), F32),
        jax.ShapeDtypeStruct((B, WINDOW, KV_WIDTH), F32),
        jax.ShapeDtypeStruct((B, CONV_W - 1, LRU_WIDTH), F32),
        jax.ShapeDtypeStruct((B, 1, LRU_WIDTH), F32),
        jax.ShapeDtypeStruct((N_EXPERTS * seg_cap, PACKED_WIDTH), jnp.uint32),
        jax.ShapeDtypeStruct((n_dispatch, 1, ROUTE_WIDTH), jnp.int32),
        jax.ShapeDtypeStruct((1, ROUTE_WIDTH), jnp.int32),
    )
    out_specs = (
        pl.BlockSpec((T, D_MODEL), lambda i: (jnp.minimum(i, n_tiles), 0)),
        pl.BlockSpec((T, ROUTE_WIDTH), lambda i: (routed(i), 0)),
        pl.BlockSpec((1, WINDOW, KV_WIDTH), lambda i: (seq(i), 0, 0)),
        pl.BlockSpec((1, WINDOW, KV_WIDTH), lambda i: (seq(i), 0, 0)),
        pl.BlockSpec((1, CONV_W - 1, LRU_WIDTH), lambda i: (seq(i), 0, 0)),
        pl.BlockSpec((1, 1, LRU_WIDTH), lambda i: (seq(i), 0, 0)),
        pl.BlockSpec(memory_space=pl.ANY),
        pl.BlockSpec((1, 1, ROUTE_WIDTH), lambda i: (routed(i), 0, 0)),
        pl.BlockSpec((1, ROUTE_WIDTH), lambda i: (0, 0)),
    )
    scratch = [
        pltpu.VMEM((SUBLANES, LRU_WIDTH), F32),
        pltpu.VMEM((SUBLANES, LRU_WIDTH), F32),
        pltpu.VMEM((T + WINDOW, KV_WIDTH), BF16),
        pltpu.VMEM((T + WINDOW, KV_WIDTH), BF16),
        pltpu.VMEM((T, N_HEADS * HEAD_DIM), F32),
        pltpu.VMEM((T, D_MODEL), F32),
        pltpu.VMEM((2, N_SLOTS, PACKED_WIDTH), jnp.uint32),
        pltpu.VMEM((PAD_CHUNK, PACKED_WIDTH), jnp.uint32),
        pltpu.VMEM((n_assign, n_assign), BF16),
        pltpu.VMEM((T, D_MODEL), BF16),
        pltpu.VMEM((SUBLANES, n_assign), F32),
        pltpu.VMEM((SUBLANES, ROUTE_WIDTH), jnp.int32),
        pltpu.VMEM((SUBLANES, ROUTE_WIDTH), F32),
        pltpu.SMEM((1, ROUTE_WIDTH), jnp.int32),
        pltpu.SMEM((1,), jnp.int32),
        pltpu.SMEM((1,), jnp.int32),
        pltpu.SemaphoreType.DMA(()),
        pltpu.SemaphoreType.DMA(()),
        pltpu.SemaphoreType.DMA(()),
    ]
    return pl.pallas_call(
        functools.partial(_mixer_kernel, tiles_per_seq=nt, n_tiles=n_tiles, seg_cap=seg_cap),
        grid=(n_tiles + 2,),
        in_specs=in_specs,
        out_specs=out_specs,
        out_shape=out_shape,
        scratch_shapes=scratch,
        compiler_params=pltpu.CompilerParams(dimension_semantics=("arbitrary",),
                                             vmem_limit_bytes=VMEM_LIMIT_BYTES),
        name="mixer_prompt",
    )(sinks, x_prompt, *weight_args)


def _sample_proj_kernel(x_ref, w_ref, b_ref, u_ref):
    u_ref[...] = jnp.dot(x_ref[...].astype(BF16), w_ref[...], preferred_element_type=F32) + b_ref[...]


def _sample_proj_call(x_s, w_in, b_in):
    n = x_s.shape[0]
    tn = SAMPLE_PROJ_TILE
    return pl.pallas_call(
        _sample_proj_kernel,
        grid=(IN_WIDTH // tn,),
        in_specs=[pl.BlockSpec((n, D_MODEL), lambda c: (0, 0)),
                  pl.BlockSpec((D_MODEL, tn), lambda c: (0, c)),
                  pl.BlockSpec((1, tn), lambda c: (0, c))],
        out_specs=pl.BlockSpec((n, tn), lambda c: (0, c)),
        out_shape=jax.ShapeDtypeStruct((n, IN_WIDTH), F32),
        compiler_params=pltpu.CompilerParams(dimension_semantics=("arbitrary",)),
        name="sample_proj",
    )(x_s, w_in, b_in)


def _sample_attn_kernel(q_ref, kn_row_ref, vn_row_ref, ck_ref, cv_ref, sinks_ref,
                        att_ref, kwin_ref, vwin_ref):
    tb = q_ref.shape[0]
    ck = ck_ref[...]
    cv = cv_ref[...]
    ckb = ck.astype(BF16)
    cvb = cv.astype(BF16)
    kn = kn_row_ref[...]
    vn = vn_row_ref[...]
    key_pos = lax.broadcasted_iota(jnp.int32, (tb, GROUP, WINDOW), 2)
    for j in range(N_KV):
        qj = q_ref[:, j * GROUP:(j + 1) * GROUP, :]
        s_c = jnp.einsum("bgd,bsd->bgs", qj.astype(BF16), ckb, preferred_element_type=F32) * ATTN_SCALE
        s_c = jnp.where(key_pos >= 1, s_c, NEG_INF)
        s_n = jnp.sum(qj * kn, axis=-1, keepdims=True) * ATTN_SCALE
        sink = sinks_ref[j][None]
        m = jnp.maximum(jnp.maximum(jnp.max(s_c, axis=-1, keepdims=True), s_n), sink)
        p_c = jnp.exp(s_c - m)
        p_n = jnp.exp(s_n - m)
        inv = 1.0 / (jnp.sum(p_c, axis=-1, keepdims=True) + p_n + jnp.exp(sink - m))
        o = jnp.einsum("bgs,bsd->bgd", (p_c * inv).astype(BF16), cvb, preferred_element_type=F32)
        att_ref[:, j * GROUP:(j + 1) * GROUP, :] = o + (p_n * inv) * vn
    last = lax.broadcasted_iota(jnp.int32, ck.shape, 1) == WINDOW - 1
    kwin_ref[...] = jnp.where(last, kn, pltpu.roll(ck, WINDOW - 1, axis=1))
    vwin_ref[...] = jnp.where(last, vn, pltpu.roll(cv, WINDOW - 1, axis=1))


def _sample_attn_call(q_wide, kn_row, vn_row, ck, cv, sinks3):
    n = q_wide.shape[0]
    tb = SAMPLE_ATTN_TILE
    b3 = lambda i: (i, 0, 0)
    return pl.pallas_call(
        _sample_attn_kernel,
        grid=(n // tb,),
        in_specs=[pl.BlockSpec((tb, N_HEADS, KV_WIDTH), b3),
                  pl.BlockSpec((tb, 1, KV_WIDTH), b3),
                  pl.BlockSpec((tb, 1, KV_WIDTH), b3),
                  pl.BlockSpec((tb, WINDOW, KV_WIDTH), b3),
                  pl.BlockSpec((tb, WINDOW, KV_WIDTH), b3),
                  pl.BlockSpec((N_KV, GROUP, 1), lambda i: (0, 0, 0))],
        out_specs=(pl.BlockSpec((tb, N_HEADS, KV_WIDTH), b3),
                   pl.BlockSpec((tb, WINDOW, KV_WIDTH), b3),
                   pl.BlockSpec((tb, WINDOW, KV_WIDTH), b3)),
        out_shape=(jax.ShapeDtypeStruct((n, N_HEADS, KV_WIDTH), F32),
                   jax.ShapeDtypeStruct((n, WINDOW, KV_WIDTH), F32),
                   jax.ShapeDtypeStruct((n, WINDOW, KV_WIDTH), F32)),
        compiler_params=pltpu.CompilerParams(dimension_semantics=("arbitrary",),
                                             vmem_limit_bytes=VMEM_LIMIT_BYTES),
        name="sample_attn",
    )(q_wide, kn_row, vn_row, ck, cv, sinks3)


def _sample_mix_kernel(x_ref, u_ref, att_ref, st_ref, h0_ref, w_conv_ref, b_conv_ref, w_gates_ref, b_rg_ref,
                       b_ig_ref, lam_ref, w_out_ref, ln_g_ref, ln_b_ref,
                       x1_ref, conv_ref, h_ref):
    xl = u_ref[:, OFF_XL:OFF_XL + LRU_WIDTH]
    wc = w_conv_ref[...]
    xc = wc[0:1] * st_ref[0]
    xc = xc + wc[1:2] * st_ref[1]
    xc = xc + wc[2:3] * st_ref[2]
    xc = xc + wc[3:4] * xl + b_conv_ref[...]
    conv_ref[0] = st_ref[1]
    conv_ref[1] = st_ref[2]
    conv_ref[2] = xl
    a, u = _lru_gates(xc, w_gates_ref, b_rg_ref[...], b_ig_ref[...], lam_ref[...])
    h = a * h0_ref[...] + u
    h_ref[...] = h
    rec = h * jax.nn.gelu(u_ref[:, OFF_YL:OFF_YL + LRU_WIDTH])
    x1_ref[...] = _merge_norm(x_ref[...], rec, att_ref[...], u_ref[:, OFF_GL:OFF_GL + D_MODEL],
                              u_ref[:, OFF_GA:OFF_GA + D_MODEL], w_out_ref,
                              ln_g_ref[...], ln_b_ref[...])


def _sample_mix_call(x_s, u_s, att, st_t, h0, wts):
    n = x_s.shape[0]
    weight_args = (wts["w_conv"], wts["b_conv"], wts["w_gates"], wts["b_rg"], wts["b_ig"], wts["lam"],
                   wts["w_out"], wts["ln1_g"], wts["ln1_b"])
    args = (x_s, u_s, att, st_t, h0) + weight_args
    out_shapes = ((n, D_MODEL), (CONV_W - 1, n, LRU_WIDTH), (n, LRU_WIDTH))
    return pl.pallas_call(
        _sample_mix_kernel,
        grid=(1,),
        in_specs=[_const_spec(a.shape) for a in args],
        out_specs=tuple(_const_spec(s) for s in out_shapes),
        out_shape=tuple(jax.ShapeDtypeStruct(s, F32) for s in out_shapes),
        compiler_params=pltpu.CompilerParams(dimension_semantics=("arbitrary",),
                                             vmem_limit_bytes=VMEM_LIMIT_BYTES),
        name="sample_mix",
    )(*args)


def _one_hot(mask):
    return jnp.where(mask, 1.0, 0.0).astype(BF16)


def _pack_rows(x):
    half = x.shape[1] // 2
    lo = lax.shift_right_logical(lax.bitcast_convert_type(x[:, :half], jnp.uint32), jnp.uint32(16))
    hi = lax.bitcast_convert_type(x[:, half:], jnp.uint32) & jnp.uint32(0xFFFF0000)
    return lo | hi


def _unpack_rows(words):
    lo = lax.bitcast_convert_type(lax.shift_left(words, jnp.uint32(16)), F32)
    hi = lax.bitcast_convert_type(words & jnp.uint32(0xFFFF0000), F32)
    return jnp.concatenate([lo.astype(BF16), hi.astype(BF16)], axis=1)


class _Dispatcher:
    def __init__(self, xs_hbm, goff_ref, gtot_ref, stage, zero_rows, tri, xb_keep, key_keep, g_vmem, gvec,
                 g_smem, nd_smem, rounds_smem, sem_stage, sem_g, sem_zero, seg_cap):
        self.xs_hbm, self.goff_ref, self.gtot_ref = xs_hbm, goff_ref, gtot_ref
        self.stage, self.zero_rows, self.tri = stage, zero_rows, tri
        self.xb_keep, self.key_keep = xb_keep, key_keep
        self.g_vmem, self.gvec, self.g_smem = g_vmem, gvec, g_smem
        self.nd_smem, self.rounds_smem = nd_smem, rounds_smem
        self.sem_stage, self.sem_g, self.sem_zero = sem_stage, sem_g, sem_zero
        self.seg_cap = seg_cap

    def _g_copy(self):
        return pltpu.make_async_copy(self.g_vmem.at[pl.ds(0, 1)], self.g_smem, self.sem_g)

    def init(self):
        n_assign = self.tri.shape[0]
        r = lax.broadcasted_iota(jnp.int32, (n_assign, n_assign), 0)
        c = lax.broadcasted_iota(jnp.int32, (n_assign, n_assign), 1)
        self.tri[...] = _one_hot(r < c)
        self.gvec[...] = jnp.zeros_like(self.gvec)
        self.g_vmem[...] = jnp.zeros_like(self.g_vmem)
        self.nd_smem[0] = 0
        self._g_copy().start()

    def _sorted_rows(self, e_row, rank_row, xb, lo):
        td = xb.shape[0]
        slot = lax.broadcasted_iota(jnp.int32, (N_SLOTS, td), 0).astype(F32)
        lo_f = lo.astype(F32)
        in_round = (rank_row >= lo_f) & (rank_row < lo_f + SLOT_CAP) & (e_row >= 0.0)
        key = jnp.where(in_round, e_row * SLOT_CAP + (rank_row - lo_f), -1.0)
        perm = jnp.where(slot == key[:, :td], 1.0, jnp.where(slot == key[:, td:], 1.0, 0.0)).astype(BF16)
        return _pack_rows(jnp.dot(perm, xb, preferred_element_type=F32))

    def plan(self, x1, route, n_valid, buf):
        td = x1.shape[0]
        n_assign = TOP_K * td
        valid_col = lax.broadcasted_iota(jnp.int32, (td, 1), 0) < n_valid
        lane = lax.broadcasted_iota(jnp.int32, (td, ROUTE_WIDTH), 1).astype(F32)
        e0c = jnp.where(valid_col, route[:, 0:1], -1.0)
        e1c = jnp.where(valid_col, route[:, 1:2], -1.0)
        cnt_row = jnp.sum(jnp.where(lane == e0c, 1.0, 0.0) + jnp.where(lane == e1c, 1.0, 0.0),
                          axis=0, keepdims=True)
        self.rounds_smem[0] = (jnp.max(cnt_row).astype(jnp.int32) + SLOT_CAP - 1) // SLOT_CAP
        self.goff_ref[0] = self.gvec[0:1, :].astype(jnp.int32)
        self.gvec[0:1, :] = self.gvec[0:1, :] + jnp.ceil(cnt_row * (1.0 / SUBLANES)) * SUBLANES
        self.gtot_ref[...] = self.gvec[0:1, :].astype(jnp.int32)

        route_t = route.T
        valid_row = lax.broadcasted_iota(jnp.int32, (1, td), 1) < n_valid
        e_row = jnp.concatenate([jnp.where(valid_row, route_t[0:1, :], -1.0),
                                 jnp.where(valid_row, route_t[1:2, :], -1.0)], axis=1)
        expert_sub = lax.broadcasted_iota(jnp.int32, (N_EXPERTS, n_assign), 0).astype(F32)
        onehot_t = expert_sub == e_row
        rank_t = jnp.dot(_one_hot(onehot_t), self.tri[...], preferred_element_type=F32)
        rank_row = jnp.sum(jnp.where(onehot_t, rank_t, 0.0), axis=0, keepdims=True)
        xb = jnp.where(valid_col, x1, 0.0).astype(BF16)
        self.xb_keep[...] = xb
        self.key_keep[0:1, :] = e_row
        self.key_keep[1:2, :] = rank_row
        self.stage[buf] = self._sorted_rows(e_row, rank_row, xb, jnp.int32(0))

    def _stage_copy(self, buf, e, dst_row):
        src = self.stage.at[buf, pl.ds(e * SLOT_CAP, SLOT_CAP)]
        return pltpu.make_async_copy(src, self.xs_hbm.at[pl.ds(dst_row, SLOT_CAP)], self.sem_stage)

    def _wait_outstanding(self):
        @pl.when(self.nd_smem[0] > 0)
        def _():
            for _ in range(N_EXPERTS):
                self._stage_copy(0, 0, 0).wait()
        self.nd_smem[0] = 0

    def _start_round(self, buf, lo):
        for e in range(N_EXPERTS):
            dst_row = pl.multiple_of(e * self.seg_cap + self.g_smem[0, e] + lo, SUBLANES)
            self._stage_copy(buf, e, dst_row).start()
        self.nd_smem[0] = N_EXPERTS

    def flush(self, buf, is_last):
        self._g_copy().wait()
        self._wait_outstanding()
        self._start_round(buf, 0)

        def later_round(r, carry):
            lo = r * SLOT_CAP
            rows = self._sorted_rows(self.key_keep[0:1, :], self.key_keep[1:2, :], self.xb_keep[...], lo)
            self._wait_outstanding()
            self.stage[buf] = rows
            self._start_round(buf, lo)
            return carry
        lax.fori_loop(1, self.rounds_smem[0], later_round, 0)

        self.g_vmem[0:1, :] = self.gvec[0:1, :].astype(jnp.int32)
        self._g_copy().start()

        @pl.when(is_last)
        def _():
            self._g_copy().wait()
            self._wait_outstanding()
            self.zero_rows[...] = jnp.zeros_like(self.zero_rows)

            def pad_copy(dst_row):
                dst = self.xs_hbm.at[pl.ds(pl.multiple_of(dst_row, SUBLANES), PAD_CHUNK)]
                return pltpu.make_async_copy(self.zero_rows, dst, self.sem_zero)

            def pad_segment(e, n_started):
                fill = self.g_smem[0, e]
                block_end = (fill + SLOT_CAP + MOE_TILE - 1) // MOE_TILE * MOE_TILE
                n_chunks = (block_end - fill + PAD_CHUNK - 1) // PAD_CHUNK

                def start(c, carry):
                    pad_copy(e * self.seg_cap + fill + c * PAD_CHUNK).start()
                    return carry
                lax.fori_loop(0, n_chunks, start, 0)
                return n_started + n_chunks
            n_started = lax.fori_loop(0, N_EXPERTS, pad_segment, 0)

            def wait(_, carry):
                pad_copy(0).wait()
                return carry
            lax.fori_loop(0, n_started, wait, 0)


def _dispatch_tiles(n_tokens):
    return (n_tokens + DISPATCH_TILE - 1) // DISPATCH_TILE


def _segment_capacity(n_tokens):
    assert SEG_PAD >= DISPATCH_TILE
    alignment_slack = (SUBLANES - 1) * _dispatch_tiles(n_tokens)
    return (n_tokens + alignment_slack + SEG_PAD + MOE_TILE - 1) // MOE_TILE * MOE_TILE


def _moe_kernel(blk_e_ref, blk_j_ref, n_used_ref, xs_ref, wg_ref, wu_ref, wd_ref, ys_ref, wg_bf, wu_bf, wd_bf):
    del blk_j_ref
    b = pl.program_id(0)

    @pl.when(b < n_used_ref[0])
    def _():
        prev_e = blk_e_ref[jnp.maximum(b - 1, 0)]

        @pl.when((b == 0) | (blk_e_ref[b] != prev_e))
        def _():
            wg_bf[...] = wg_ref[0].astype(BF16)
            wu_bf[...] = wu_ref[0].astype(BF16)
            wd_bf[...] = wd_ref[0].astype(BF16)

        xb = _unpack_rows(xs_ref[...])
        g = jnp.dot(xb, wg_bf[...], preferred_element_type=F32)
        u = jnp.dot(xb, wu_bf[...], preferred_element_type=F32)
        hmid = (g * _sigmoid(g)) * u
        y = jnp.dot(hmid.astype(BF16), wd_bf[...], preferred_element_type=F32)
        ys_ref[...] = _pack_rows(y.astype(BF16).astype(F32))


def _expert_blocks(gtot, n_blocks):
    rows = gtot[0, :N_EXPERTS]
    nb = (rows + SLOT_CAP + MOE_TILE - 1) // MOE_TILE
    ends = jnp.cumsum(nb)
    n_used = ends[-1]
    step = jnp.minimum(jnp.arange(n_blocks, dtype=jnp.int32), n_used - 1)
    blk_e = jnp.sum((step[:, None] >= ends[None, :]).astype(jnp.int32), axis=1)
    blk_j = step - (ends - nb)[blk_e]
    return blk_e, blk_j.astype(jnp.int32), n_used.reshape(1).astype(jnp.int32)


def _moe_call(gtot, xs, w_gate, w_up, w_down, n_tokens):
    seg_cap = _segment_capacity(n_tokens)
    seg_blocks = seg_cap // MOE_TILE
    max_rows = n_tokens * TOP_K + N_EXPERTS * (SUBLANES - 1) * _dispatch_tiles(n_tokens)
    n_blocks = (max_rows + N_EXPERTS * (SLOT_CAP + MOE_TILE - 1)) // MOE_TILE
    blk_e, blk_j, n_used = _expert_blocks(gtot, n_blocks)
    row_block = lambda b, be, bj, nu: (be[b] * seg_blocks + bj[b], 0)
    weight_block = lambda b, be, bj, nu: (be[b], 0, 0)
    grid_spec = pltpu.PrefetchScalarGridSpec(
        num_scalar_prefetch=3,
        grid=(n_blocks,),
        in_specs=[
            pl.BlockSpec((MOE_TILE, PACKED_WIDTH), row_block),
            pl.BlockSpec((1, D_MODEL, EXPERT_FF), weight_block),
            pl.BlockSpec((1, D_MODEL, EXPERT_FF), weight_block),
            pl.BlockSpec((1, EXPERT_FF, D_MODEL), weight_block),
        ],
        out_specs=pl.BlockSpec((MOE_TILE, PACKED_WIDTH), row_block),
        scratch_shapes=[
            pltpu.VMEM((D_MODEL, EXPERT_FF), BF16),
            pltpu.VMEM((D_MODEL, EXPERT_FF), BF16),
            pltpu.VMEM((EXPERT_FF, D_MODEL), BF16),
        ],
    )
    return pl.pallas_call(
        _moe_kernel,
        grid_spec=grid_spec,
        out_shape=jax.ShapeDtypeStruct(xs.shape, xs.dtype),
        compiler_params=pltpu.CompilerParams(dimension_semantics=("arbitrary",),
                                             vmem_limit_bytes=VMEM_LIMIT_BYTES),
        name="moe_experts",
    )(blk_e, blk_j, n_used, xs, w_gate, w_up, w_down)


def _combine_kernel(gcur_ref, gnext_ref, x1_ref, route_ref, ys_hbm, ln_g_ref, ln_b_ref, y_ref,
                    stage, acc, tri, sem, *, seg_cap):
    td = x1_ref.shape[0]
    n_assign = TOP_K * td
    i = pl.program_id(0)
    buf = lax.rem(i, 2)

    def stage_copy(g_ref, e, first_rank, to_buf):
        src_row = pl.multiple_of(e * seg_cap + g_ref[0, 0, e] + first_rank, SUBLANES)
        slot0 = e * SLOT_CAP
        if not isinstance(slot0, int):
            slot0 = pl.multiple_of(slot0, SLOT_CAP)
        dst = stage.at[to_buf, pl.ds(slot0, SLOT_CAP)]
        return pltpu.make_async_copy(ys_hbm.at[pl.ds(src_row, SLOT_CAP)], dst, sem.at[to_buf])

    def fetch_first_round(g_ref, to_buf):
        for e in range(N_EXPERTS):
            stage_copy(g_ref, e, 0, to_buf).start()

    @pl.when(i == 0)
    def _():
        r = lax.broadcasted_iota(jnp.int32, (n_assign, n_assign), 0)
        c = lax.broadcasted_iota(jnp.int32, (n_assign, n_assign), 1)
        tri[...] = _one_hot(c < r)
        fetch_first_round(gcur_ref, 0)

    def count(e):
        return gnext_ref[0, 0, e] - gcur_ref[0, 0, e]

    route = route_ref[...]
    e0, e1, w0, w1 = route[:, 0:1], route[:, 1:2], route[:, 2:3], route[:, 3:4]
    lane = lax.broadcasted_iota(jnp.int32, (td, ROUTE_WIDTH), 1).astype(F32)
    onehot = jnp.concatenate([lane == e0, lane == e1], axis=0)
    onehot_f = jnp.where(onehot, 1.0, 0.0)
    rank_mat = jnp.dot(tri[...], onehot_f.astype(BF16), preferred_element_type=F32)
    rank = jnp.sum(jnp.where(onehot, rank_mat, 0.0), axis=1, keepdims=True)
    r0, r1 = rank[:td], rank[td:]
    cnt_row = jnp.sum(onehot_f, axis=0, keepdims=True)
    rounds = (jnp.max(cnt_row).astype(jnp.int32) + SLOT_CAP - 1) // SLOT_CAP

    slot_lane = lax.broadcasted_iota(jnp.int32, (td, N_SLOTS), 1).astype(F32)

    def wait_copies(n):
        def wait_one(_, c):
            stage_copy(gcur_ref, 0, 0, buf).wait()
            return c
        lax.fori_loop(0, n, wait_one, 0)

    def weighted_rows(lo):
        lo_f = lo.astype(F32)
        rows = _unpack_rows(stage[buf])

        def selector(e_col, r_col, w_col):
            in_round = (r_col >= lo_f) & (r_col < lo_f + SLOT_CAP)
            key = jnp.where(in_round, e_col * SLOT_CAP + (r_col - lo_f), -1.0)
            return jnp.where(slot_lane == key, w_col, 0.0)
        mix = (selector(e0, r0, w0) + selector(e1, r1, w1)).astype(BF16)
        return jnp.dot(mix, rows, preferred_element_type=F32)

    for _ in range(N_EXPERTS):
        stage_copy(gcur_ref, 0, 0, buf).wait()

    fetch_first_round(gnext_ref, 1 - buf)

    acc[...] = weighted_rows(jnp.int32(0))

    def later_round(r, carry):
        lo = r * SLOT_CAP

        def start(e, n_started):
            has_rows = count(e) > lo

            @pl.when(has_rows)
            def _():
                stage_copy(gcur_ref, e, lo, buf).start()
            return n_started + has_rows.astype(jnp.int32)
        wait_copies(lax.fori_loop(0, N_EXPERTS, start, 0))
        acc[...] = acc[...] + weighted_rows(lo)
        return carry
    lax.fori_loop(1, rounds, later_round, 0)

    y_ref[...] = _layer_norm(ALPHA * x1_ref[...] + acc[...], ln_g_ref[...], ln_b_ref[...])

    @pl.when(i + 1 == pl.num_programs(0))
    def _():
        for _ in range(N_EXPERTS):
            stage_copy(gnext_ref, 0, 0, 1 - buf).wait()


def _combine_call(goff, x1_all, route_all, ys, ln_g, ln_b, first_row, n_rows, tile, n_tokens):
    assert first_row % DISPATCH_TILE == 0 and first_row % tile == 0
    assert tile == DISPATCH_TILE or n_rows == tile
    first_block = first_row // tile
    first_goff = first_row // DISPATCH_TILE
    n_assign = TOP_K * tile
    grid_spec = pl.GridSpec(
        grid=(n_rows // tile,),
        in_specs=[
            pl.BlockSpec((1, 1, ROUTE_WIDTH), lambda i: (first_goff + i, 0, 0), memory_space=pltpu.SMEM),
            pl.BlockSpec((1, 1, ROUTE_WIDTH), lambda i: (first_goff + i + 1, 0, 0), memory_space=pltpu.SMEM),
            pl.BlockSpec((tile, D_MODEL), lambda i: (first_block + i, 0)),
            pl.BlockSpec((tile, ROUTE_WIDTH), lambda i: (first_block + i, 0)),
            pl.BlockSpec(memory_space=pl.ANY),
            _const_spec(ln_g.shape),
            _const_spec(ln_b.shape),
        ],
        out_specs=pl.BlockSpec((tile, D_MODEL), lambda i: (i, 0)),
        scratch_shapes=[pltpu.VMEM((2, N_SLOTS, PACKED_WIDTH), jnp.uint32),
                        pltpu.VMEM((tile, D_MODEL), F32),
                        pltpu.VMEM((n_assign, n_assign), BF16),
                        pltpu.SemaphoreType.DMA((2,))],
    )
    return pl.pallas_call(
        functools.partial(_combine_kernel, seg_cap=_segment_capacity(n_tokens)),
        grid_spec=grid_spec,
        out_shape=jax.ShapeDtypeStruct((n_rows, D_MODEL), F32),
        compiler_params=pltpu.CompilerParams(dimension_semantics=("arbitrary",),
                                             vmem_limit_bytes=VMEM_LIMIT_BYTES),
        name="moe_combine",
    )(goff, goff, x1_all, route_all, ys, ln_g, ln_b)


def _prepare_weights(w_in, b_in, w_conv, b_conv, w_rg, b_rg, w_ig, b_ig, lru_lambda, w_lru_out, w_attn_out, w_o,
                     ln1_g, ln1_b, w_group, b_group, w_router, b_router):
    blocks_per_chunk = GATE_CHUNK // LRU_BLOCK

    def chunked_block_diag(w):
        w = w.reshape(N_GATE_CHUNKS, blocks_per_chunk, LRU_BLOCK, LRU_BLOCK)
        eye = jnp.eye(blocks_per_chunk, dtype=w.dtype)
        return jnp.einsum("cbij,bd->cbidj", w, eye).reshape(N_GATE_CHUNKS, GATE_CHUNK, GATE_CHUNK)

    w_gates = jnp.concatenate([chunked_block_diag(w_rg), chunked_block_diag(w_ig)], axis=-1).astype(BF16)
    w_rt = jnp.concatenate([w_group, w_router], axis=1)
    w_rt = jnp.pad(w_rt, ((0, 0), (0, ROUTE_WIDTH - w_rt.shape[1])))
    w_rt_hi = w_rt.astype(BF16)
    w_rt_lo = jnp.concatenate([w_rt_hi, (w_rt - w_rt_hi.astype(F32)).astype(BF16)], axis=1)
    b_rt = jnp.pad(jnp.concatenate([b_group, b_router]), (0, ROUTE_WIDTH - N_GROUPS - N_EXPERTS))
    row = lambda v: v.reshape(1, -1)
    return dict(
        w_in=w_in.astype(BF16), b_in=row(b_in), w_conv=w_conv, b_conv=row(b_conv), w_gates=w_gates,
        b_rg=row(b_rg), b_ig=row(b_ig), lam=row(lru_lambda),
        w_out=jnp.stack([w_lru_out, w_attn_out, w_o]).astype(BF16),
        ln1_g=row(ln1_g), ln1_b=row(ln1_b), w_rt_hi=w_rt_hi, w_rt_lo=w_rt_lo, b_rt=row(b_rt))


def kernel(x_prompt, x_sample, cache_k, cache_v, state_conv, state_lru_h, w_in, b_in, w_conv, b_conv, w_rg, b_rg,
           w_ig, b_ig, lru_lambda, sinks, w_lru_out, w_attn_out, w_o, ln1_g, ln1_b, w_group, b_group, w_router,
           b_router, w_gate, w_up, w_down, ln2_g, ln2_b):
    B, S, _ = x_prompt.shape
    n_prompt = B * S
    n_sample = x_sample.shape[0]
    n_all = n_prompt + n_sample
    wts = _prepare_weights(w_in, b_in, w_conv, b_conv, w_rg, b_rg, w_ig, b_ig, lru_lambda, w_lru_out, w_attn_out,
                           w_o, ln1_g, ln1_b, w_group, b_group, w_router, b_router)

    x_s = x_sample.reshape(n_sample, D_MODEL)
    u_s = _sample_proj_call(x_s, wts["w_in"], wts["b_in"])
    q4 = u_s[:, OFF_Q:OFF_K].reshape(n_sample, N_KV, GROUP, HEAD_DIM)
    q_wide = jnp.einsum("njgd,jk->njgkd", q4, jnp.eye(N_KV, dtype=F32)).reshape(n_sample, N_HEADS, KV_WIDTH)
    k_new = u_s[:, OFF_K:OFF_V]
    v_new = u_s[:, OFF_V:OFF_GL]
    att_wide, k_win_s, v_win_s = _sample_attn_call(
        q_wide, k_new.reshape(n_sample, 1, KV_WIDTH), v_new.reshape(n_sample, 1, KV_WIDTH),
        cache_k.reshape(n_sample, WINDOW, KV_WIDTH), cache_v.reshape(n_sample, WINDOW, KV_WIDTH),
        sinks.reshape(N_KV, GROUP, 1))
    att5 = att_wide.reshape(n_sample, N_KV, GROUP, N_KV, HEAD_DIM)
    att_s = jnp.stack([att5[:, j, :, j, :] for j in range(N_KV)], axis=1).reshape(n_sample, N_HEADS * HEAD_DIM)
    x1_s, conv_s_t, h_s = _sample_mix_call(
        x_s, u_s, att_s, jnp.transpose(state_conv, (1, 0, 2)), state_lru_h, wts)

    x1_all, route_all, k_win_p, v_win_p, conv_p, h_p, xs, goff, gtot = _mixer_call(x_prompt, x1_s, sinks, wts)

    ys = _moe_call(gtot, xs, w_gate, w_up, w_down, n_all)
    goff = jnp.concatenate([goff, gtot[None]], axis=0)
    ln2_g2, ln2_b2 = ln2_g.reshape(1, -1), ln2_b.reshape(1, -1)
    y_p = _combine_call(goff, x1_all, route_all, ys, ln2_g2, ln2_b2, 0, n_prompt, DISPATCH_TILE, n_all)
    y_s = _combine_call(goff, x1_all, route_all, ys, ln2_g2, ln2_b2, n_prompt, n_sample, n_sample, n_all)

    kv_shape = (WINDOW, N_KV, HEAD_DIM)
    return (y_p.reshape(B, S, D_MODEL), y_s.reshape(n_sample, 1, D_MODEL),
            k_win_p.reshape((B,) + kv_shape), v_win_p.reshape((B,) + kv_shape), conv_p, h_p.reshape(B, LRU_WIDTH),
            k_win_s.reshape((n_sample,) + kv_shape), v_win_s.reshape((n_sample,) + kv_shape),
            jnp.transpose(conv_s_t, (1, 0, 2)), h_s)
```

```python
import functools

import jax
import jax.numpy as jnp
from jax import lax
from jax.experimental import pallas as pl
from jax.experimental.pallas import tpu as pltpu

F32 = jnp.float32
BF16 = jnp.bfloat16

D_MODEL = 1024
LRU_WIDTH = 1024
LRU_BLOCK = 64
CONV_W = 4
LRU_C = 8.0
N_HEADS = 16
N_KV = 4
GROUP = N_HEADS // N_KV
HEAD_DIM = 64
KV_WIDTH = N_KV * HEAD_DIM
WINDOW = 128
NEG_INF = -1e30
N_GROUPS = 4
EXPERTS_PER_GROUP = 8
N_EXPERTS = N_GROUPS * EXPERTS_PER_GROUP
TOP_K = 2
EXPERT_FF = D_MODEL // 2
DEPTH = 1
ALPHA = (2 * DEPTH) ** 0.25
LN_EPS = 1e-5
ATTN_SCALE = HEAD_DIM ** -0.5
LOG2_E = 1.4426950408889634

OFF_XL = 0
OFF_YL = OFF_XL + LRU_WIDTH
OFF_Q = OFF_YL + LRU_WIDTH
OFF_K = OFF_Q + N_HEADS * HEAD_DIM
OFF_V = OFF_K + KV_WIDTH
OFF_GL = OFF_V + KV_WIDTH
OFF_GA = OFF_GL + D_MODEL
IN_WIDTH = OFF_GA + D_MODEL

LANES = 128
SUBLANES = 8
MXU_DIM = 256
VMEM_LIMIT_BYTES = 56 * 1024 * 1024

GATE_CHUNK = MXU_DIM
N_GATE_CHUNKS = LRU_WIDTH // GATE_CHUNK
ROUTE_WIDTH = LANES

SEQ_TILE = 256
MOE_TILE = 1392
DISPATCH_TILE = 256
SLOT_CAP = 32
N_SLOTS = N_EXPERTS * SLOT_CAP
PAD_CHUNK = 128
SEG_PAD = MOE_TILE + SLOT_CAP + PAD_CHUNK
PACKED_WIDTH = D_MODEL // 2
SAMPLE_ATTN_TILE = 16
SAMPLE_PROJ_TILE = 1408


def _const_spec(shape):
    nd = len(shape)
    return pl.BlockSpec(shape, lambda *_: (0,) * nd)


def _layer_norm(z, g, b):
    mu = jnp.mean(z, axis=-1, keepdims=True)
    zc = z - mu
    var = jnp.mean(zc * zc, axis=-1, keepdims=True)
    return zc * lax.rsqrt(var + LN_EPS) * g + b


def _sigmoid(x):
    return 1.0 / (1.0 + jnp.exp2(x * -LOG2_E))


def _softplus(x):
    return jnp.maximum(x, 0.0) + jnp.log1p(jnp.exp(-jnp.abs(x)))


def _lru_gates(xc, w_gates_ref, b_rg, b_ig, lam):
    xcb = xc.astype(BF16)
    r_parts, i_parts = [], []
    for c in range(N_GATE_CHUNKS):
        g = jnp.dot(xcb[:, c * GATE_CHUNK:(c + 1) * GATE_CHUNK], w_gates_ref[c], preferred_element_type=F32)
        r_parts.append(g[:, :GATE_CHUNK])
        i_parts.append(g[:, GATE_CHUNK:])
    r = _sigmoid(jnp.concatenate(r_parts, axis=1) + b_rg)
    i = _sigmoid(jnp.concatenate(i_parts, axis=1) + b_ig)
    log_a = (-LRU_C * r) * _softplus(-lam)
    a = jnp.exp(log_a)
    gain_sq = 1.0 - a * a
    gain = jnp.where(gain_sq > 0.0, gain_sq * lax.rsqrt(gain_sq), 0.0)
    u = gain * (i * xc)
    return a, u


def _linear_scan(a, u, h_in):
    n, w = a.shape
    groups = n // SUBLANES
    a3 = a.reshape(groups, SUBLANES, w)
    u3 = u.reshape(groups, SUBLANES, w)
    row = lax.broadcasted_iota(jnp.int32, a3.shape, 1)
    d = 1
    while d < SUBLANES:
        has_prev = row >= d
        u3 = u3 + a3 * jnp.where(has_prev, pltpu.roll(u3, d, axis=1), 0.0)
        a3 = a3 * jnp.where(has_prev, pltpu.roll(a3, d, axis=1), 1.0)
        d *= 2
    carry = h_in
    out = []
    for g in range(groups):
        h_g = u3[g] + a3[g] * carry
        out.append(h_g)
        carry = h_g[SUBLANES - 1:SUBLANES, :]
    return jnp.concatenate(out, axis=0)


def _route(x1, w_hi_ref, w_hilo_ref, b_rt):
    x_hi = x1.astype(BF16)
    x_lo = (x1 - x_hi.astype(F32)).astype(BF16)
    both = jnp.dot(x_hi, w_hilo_ref[...], preferred_element_type=F32)
    logits = (both[:, :ROUTE_WIDTH]
              + (jnp.dot(x_lo, w_hi_ref[...], preferred_element_type=F32) + both[:, ROUTE_WIDTH:])) + b_rt
    col = lax.broadcasted_iota(jnp.int32, logits.shape, 1)
    big = jnp.int32(ROUTE_WIDTH)
    is_g = col < N_GROUPS
    gl = jnp.where(is_g, logits, -jnp.inf)
    gmax = jnp.max(gl, axis=-1, keepdims=True)
    g_idx = jnp.min(jnp.where(gl == gmax, col, big), axis=-1, keepdims=True)
    p_g = 1.0 / jnp.sum(jnp.where(is_g, jnp.exp(gl - gmax), 0.0), axis=-1, keepdims=True)
    lo = N_GROUPS + g_idx * EXPERTS_PER_GROUP
    in_grp = (col >= lo) & (col < lo + EXPERTS_PER_GROUP)
    el = jnp.where(in_grp, logits, -jnp.inf)
    v1 = jnp.max(el, axis=-1, keepdims=True)
    i1 = jnp.min(jnp.where(el == v1, col, big), axis=-1, keepdims=True)
    el2 = jnp.where(col == i1, -jnp.inf, el)
    v2 = jnp.max(el2, axis=-1, keepdims=True)
    i2 = jnp.min(jnp.where(el2 == v2, col, big), axis=-1, keepdims=True)
    e21 = jnp.exp(v2 - v1)
    inv = 1.0 / (1.0 + e21)
    w1 = p_g * inv
    w2 = p_g * (e21 * inv)
    e1 = (i1 - N_GROUPS).astype(F32)
    e2 = (i2 - N_GROUPS).astype(F32)
    return jnp.where(col == 0, e1, jnp.where(col == 1, e2, jnp.where(col == 2, w1, jnp.where(col == 3, w2, 0.0))))


def _merge_norm(x, rec, att, g_l, g_a, w_out_ref, ln_g, ln_b):
    rec_o = jnp.dot(rec.astype(BF16), w_out_ref[0], preferred_element_type=F32)
    att_o = jnp.dot(att.astype(BF16), w_out_ref[1], preferred_element_type=F32)
    merged = _sigmoid(g_l) * rec_o + _sigmoid(g_a) * att_o
    mix = jnp.dot(merged.astype(BF16), w_out_ref[2], preferred_element_type=F32)
    return _layer_norm(ALPHA * x + mix, ln_g, ln_b)


def _mixer_kernel(sinks_ref, x_ref, w_in_ref, b_in_ref, w_conv_ref, b_conv_ref, w_gates_ref, b_rg_ref, b_ig_ref,
                  lam_ref, w_out_ref, ln_g_ref, ln_b_ref, w_rt_hi_ref, w_rt_lo_ref,
                  b_rt_ref, x1_s_ref,
                  x1_ref, route_ref, kwin_ref, vwin_ref, conv_ref, h_ref, xs_hbm, goff_ref, gtot_ref,
                  conv_buf, h_carry, kcat, vcat, att_buf, prev_x1, stage, zero_rows, tri, xb_keep, key_keep,
                  g_vmem, gvec, g_smem, nd_smem, rounds_smem, sem_stage, sem_g, sem_zero,
                  *, tiles_per_seq, n_tiles, seg_cap):
    step = pl.program_id(0)
    last_step = pl.num_programs(0) - 1
    buf = lax.rem(step, 2)
    n_sample = x1_s_ref.shape[0]
    disp = _Dispatcher(xs_hbm, goff_ref, gtot_ref, stage, zero_rows, tri, xb_keep, key_keep, g_vmem, gvec,
                       g_smem, nd_smem, rounds_smem, sem_stage, sem_g, sem_zero, seg_cap)
    n_valid = jnp.where(step == 0, 0, jnp.where(step == last_step, n_sample, SEQ_TILE))

    @pl.when(step == 0)
    def _():
        disp.init()
        prev_x1[...] = jnp.zeros_like(prev_x1)

    def route_and_plan():
        x1_prev = prev_x1[...]
        route = _route(x1_prev, w_rt_hi_ref, w_rt_lo_ref, b_rt_ref[...])
        route_ref[...] = route
        disp.plan(x1_prev, route, n_valid, buf)

    @pl.when(step < n_tiles)
    def _():
        @pl.when(lax.rem(step, tiles_per_seq) == 0)
        def _():
            conv_buf[...] = jnp.zeros_like(conv_buf)
            h_carry[...] = jnp.zeros_like(h_carry)
            kcat[0:WINDOW, :] = jnp.zeros((WINDOW, KV_WIDTH), BF16)
            vcat[0:WINDOW, :] = jnp.zeros((WINDOW, KV_WIDTH), BF16)

        route_and_plan()
        _mixer_tile(lax.rem(step, tiles_per_seq), sinks_ref, x_ref, w_in_ref, b_in_ref, w_conv_ref, b_conv_ref,
                    w_gates_ref, b_rg_ref, b_ig_ref, lam_ref, w_out_ref, ln_g_ref,
                    ln_b_ref, x1_ref, prev_x1, kwin_ref, vwin_ref, conv_ref, h_ref,
                    conv_buf, h_carry, kcat, vcat, att_buf)

    @pl.when(step >= n_tiles)
    def _():
        route_and_plan()

        @pl.when(step == n_tiles)
        def _():
            x1_ref[0:n_sample, :] = x1_s_ref[...]
            prev_x1[0:n_sample, :] = x1_s_ref[...]

    disp.flush(buf, step == last_step)


def _mixer_tile(t, sinks_ref, x_ref, w_in_ref, b_in_ref, w_conv_ref, b_conv_ref, w_gates_ref, b_rg_ref, b_ig_ref,
                lam_ref, w_out_ref, ln_g_ref, ln_b_ref,
                x1_ref, x1_keep, kwin_ref, vwin_ref, conv_ref, h_ref,
                conv_buf, h_carry, kcat, vcat, att_buf):
    T = SEQ_TILE
    x = x_ref[0]
    xb = x.astype(BF16)

    def proj(lo, width):
        return jnp.dot(xb, w_in_ref[:, lo:lo + width], preferred_element_type=F32) + b_in_ref[:, lo:lo + width]

    xl = proj(OFF_XL, LRU_WIDTH)
    xl_ext = jnp.concatenate([conv_buf[...], xl], axis=0)

    def lagged(k):
        return pltpu.roll(xl_ext, k, axis=0)[SUBLANES:, :]
    wc = w_conv_ref[...]
    xc = wc[0:1] * lagged(3)
    xc = xc + wc[1:2] * lagged(2)
    xc = xc + wc[2:3] * lagged(1)
    xc = xc + wc[3:4] * xl + b_conv_ref[...]
    conv_ref[0] = xl[T - (CONV_W - 1):, :]
    conv_buf[...] = xl[T - SUBLANES:, :]

    a, u = _lru_gates(xc, w_gates_ref, b_rg_ref[...], b_ig_ref[...], lam_ref[...])
    h = _linear_scan(a, u, h_carry[0:1, :])
    h_last = h[T - 1:T, :]
    h_carry[0:1, :] = h_last
    h_ref[0] = h_last
    rec = h * jax.nn.gelu(proj(OFF_YL, LRU_WIDTH))

    q = proj(OFF_Q, N_HEADS * HEAD_DIM) * (ATTN_SCALE * LOG2_E)
    k = proj(OFF_K, KV_WIDTH)
    v = proj(OFF_V, KV_WIDTH)
    kwin_ref[0] = k[T - WINDOW:, :]
    vwin_ref[0] = v[T - WINDOW:, :]
    kcat[WINDOW:WINDOW + T, :] = k.astype(BF16)
    vcat[WINDOW:WINDOW + T, :] = v.astype(BF16)

    qi = lax.broadcasted_iota(jnp.int32, (WINDOW, 2 * WINDOW), 0)
    kj = lax.broadcasted_iota(jnp.int32, (WINDOW, 2 * WINDOW), 1)
    band = (kj > qi) & (kj <= qi + WINDOW)
    grp_row = lax.broadcasted_iota(jnp.int32, (GROUP * WINDOW, 1), 0) // WINDOW
    for qb in range(T // WINDOW):
        if qb == 0:
            first_key = jnp.where(t == 0, WINDOW, 0)
            mask1 = band & (kj >= first_key)
        else:
            mask1 = band
        bias = jnp.concatenate([jnp.where(mask1, 0.0, NEG_INF)] * GROUP, axis=0)
        r0 = qb * WINDOW
        qq = q[r0:r0 + WINDOW, :]
        for j in range(N_KV):
            kjb = kcat[r0:r0 + 2 * WINDOW, j * HEAD_DIM:(j + 1) * HEAD_DIM]
            vjb = vcat[r0:r0 + 2 * WINDOW, j * HEAD_DIM:(j + 1) * HEAD_DIM]
            qs = jnp.concatenate(
                [qq[:, (j * GROUP + g) * HEAD_DIM:(j * GROUP + g + 1) * HEAD_DIM] for g in range(GROUP)], axis=0)
            s = lax.dot_general(qs.astype(BF16), kjb, (((1,), (1,)), ((), ())), preferred_element_type=F32) + bias
            sink = jnp.zeros((GROUP * WINDOW, 1), F32)
            for g in range(GROUP):
                sink = jnp.where(grp_row == g, sinks_ref[j * GROUP + g] * LOG2_E, sink)
            m = jnp.maximum(jnp.max(s, axis=-1, keepdims=True), sink)
            p = jnp.exp2(s - m)
            inv = 1.0 / (jnp.sum(p, axis=-1, keepdims=True) + jnp.exp2(sink - m))
            o = jnp.dot((p * inv).astype(BF16), vjb, preferred_element_type=F32)
            for g in range(GROUP):
                hcol = (j * GROUP + g) * HEAD_DIM
                att_buf[r0:r0 + WINDOW, hcol:hcol + HEAD_DIM] = o[g * WINDOW:(g + 1) * WINDOW, :]
    kcat[0:WINDOW, :] = kcat[T:T + WINDOW, :]
    vcat[0:WINDOW, :] = vcat[T:T + WINDOW, :]

    x1 = _merge_norm(x, rec, att_buf[...], proj(OFF_GL, D_MODEL), proj(OFF_GA, D_MODEL),
                     w_out_ref, ln_g_ref[...], ln_b_ref[...])
    x1_ref[...] = x1
    x1_keep[...] = x1


def _mixer_call(x_prompt, x1_s, sinks, wts):
    B, S, _ = x_prompt.shape
    T = SEQ_TILE
    assert T == DISPATCH_TILE and x1_s.shape[0] <= T
    nt = S // T
    n_tiles = B * nt
    n_rows_total = B * S + x1_s.shape[0]
    n_dispatch = _dispatch_tiles(n_rows_total)
    assert n_dispatch == n_tiles + 1
    seg_cap = _segment_capacity(n_rows_total)
    n_assign = TOP_K * T
    weight_args = (wts["w_in"], wts["b_in"], wts["w_conv"], wts["b_conv"], wts["w_gates"], wts["b_rg"], wts["b_ig"],
                   wts["lam"], wts["w_out"], wts["ln1_g"], wts["ln1_b"],
                   wts["w_rt_hi"], wts["w_rt_lo"], wts["b_rt"], x1_s)
    mixed = lambda i: jnp.minimum(i, n_tiles - 1)
    seq = lambda i: mixed(i) // nt
    routed = lambda i: jnp.clip(i - 1, 0, n_dispatch - 1)
    in_specs = [pl.BlockSpec(memory_space=pltpu.SMEM),
                pl.BlockSpec((1, T, D_MODEL), lambda i: (seq(i), lax.rem(mixed(i), nt), 0))]
    in_specs += [_const_spec(w.shape) for w in weight_args]
    out_shape = (
        jax.ShapeDtypeStruct((n_rows_total, D_MODEL), F32),
        jax.ShapeDtypeStruct((n_rows_total, ROUTE_WIDTH), F32),
        jax.ShapeDtypeStruct((B, WINDOW, KV_WIDTH), F32),
        jax.ShapeDtypeStruct((B, WINDOW, KV_WIDTH), F32),
        jax.ShapeDtypeStruct((B, CONV_W - 1, LRU_WIDTH), F32),
        jax.ShapeDtypeStruct((B, 1, LRU_WIDTH), F32),
        jax.ShapeDtypeStruct((N_EXPERTS * seg_cap, PACKED_WIDTH), jnp.uint32),
        jax.ShapeDtypeStruct((n_dispatch, 1, ROUTE_WIDTH), jnp.int32),
        jax.ShapeDtypeStruct((1, ROUTE_WIDTH), jnp.int32),
    )
    out_specs = (
        pl.BlockSpec((T, D_MODEL), lambda i: (jnp.minimum(i, n_tiles), 0)),
        pl.BlockSpec((T, ROUTE_WIDTH), lambda i: (routed(i), 0)),
        pl.BlockSpec((1, WINDOW, KV_WIDTH), lambda i: (seq(i), 0, 0)),
        pl.BlockSpec((1, WINDOW, KV_WIDTH), lambda i: (seq(i), 0, 0)),
        pl.BlockSpec((1, CONV_W - 1, LRU_WIDTH), lambda i: (seq(i), 0, 0)),
        pl.BlockSpec((1, 1, LRU_WIDTH), lambda i: (seq(i), 0, 0)),
        pl.BlockSpec(memory_space=pl.ANY),
        pl.BlockSpec((1, 1, ROUTE_WIDTH), lambda i: (routed(i), 0, 0)),
        pl.BlockSpec((1, ROUTE_WIDTH), lambda i: (0, 0)),
    )
    scratch = [
        pltpu.VMEM((SUBLANES, LRU_WIDTH), F32),
        pltpu.VMEM((SUBLANES, LRU_WIDTH), F32),
        pltpu.VMEM((T + WINDOW, KV_WIDTH), BF16),
        pltpu.VMEM((T + WINDOW, KV_WIDTH), BF16),
        pltpu.VMEM((T, N_HEADS * HEAD_DIM), F32),
        pltpu.VMEM((T, D_MODEL), F32),
        pltpu.VMEM((2, N_SLOTS, PACKED_WIDTH), jnp.uint32),
        pltpu.VMEM((PAD_CHUNK, PACKED_WIDTH), jnp.uint32),
        pltpu.VMEM((n_assign, n_assign), BF16),
        pltpu.VMEM((T, D_MODEL), BF16),
        pltpu.VMEM((SUBLANES, n_assign), F32),
        pltpu.VMEM((SUBLANES, ROUTE_WIDTH), jnp.int32),
        pltpu.VMEM((SUBLANES, ROUTE_WIDTH), F32),
        pltpu.SMEM((1, ROUTE_WIDTH), jnp.int32),
        pltpu.SMEM((1,), jnp.int32),
        pltpu.SMEM((1,), jnp.int32),
        pltpu.SemaphoreType.DMA(()),
        pltpu.SemaphoreType.DMA(()),
        pltpu.SemaphoreType.DMA(()),
    ]
    return pl.pallas_call(
        functools.partial(_mixer_kernel, tiles_per_seq=nt, n_tiles=n_tiles, seg_cap=seg_cap),
        grid=(n_tiles + 2,),
        in_specs=in_specs,
        out_specs=out_specs,
        out_shape=out_shape,
        scratch_shapes=scratch,
        compiler_params=pltpu.CompilerParams(dimension_semantics=("arbitrary",),
                                             vmem_limit_bytes=VMEM_LIMIT_BYTES),
        name="mixer_prompt",
    )(sinks, x_prompt, *weight_args)


def _sample_proj_kernel(x_ref, w_ref, b_ref, u_ref):
    u_ref[...] = jnp.dot(x_ref[...].astype(BF16), w_ref[...], preferred_element_type=F32) + b_ref[...]


def _sample_proj_call(x_s, w_in, b_in):
    n = x_s.shape[0]
    tn = SAMPLE_PROJ_TILE
    return pl.pallas_call(
        _sample_proj_kernel,
        grid=(IN_WIDTH // tn,),
        in_specs=[pl.BlockSpec((n, D_MODEL), lambda c: (0, 0)),
                  pl.BlockSpec((D_MODEL, tn), lambda c: (0, c)),
                  pl.BlockSpec((1, tn), lambda c: (0, c))],
        out_specs=pl.BlockSpec((n, tn), lambda c: (0, c)),
        out_shape=jax.ShapeDtypeStruct((n, IN_WIDTH), F32),
        compiler_params=pltpu.CompilerParams(dimension_semantics=("arbitrary",)),
        name="sample_proj",
    )(x_s, w_in, b_in)


def _sample_attn_kernel(q_ref, kn_row_ref, vn_row_ref, ck_ref, cv_ref, sinks_ref,
                        att_ref, kwin_ref, vwin_ref):
    tb = q_ref.shape[0]
    ck = ck_ref[...]
    cv = cv_ref[...]
    ckb = ck.astype(BF16)
    cvb = cv.astype(BF16)
    kn = kn_row_ref[...]
    vn = vn_row_ref[...]
    key_pos = lax.broadcasted_iota(jnp.int32, (tb, GROUP, WINDOW), 2)
    for j in range(N_KV):
        qj = q_ref[:, j * GROUP:(j + 1) * GROUP, :]
        s_c = jnp.einsum("bgd,bsd->bgs", qj.astype(BF16), ckb, preferred_element_type=F32) * ATTN_SCALE
        s_c = jnp.where(key_pos >= 1, s_c, NEG_INF)
        s_n = jnp.sum(qj * kn, axis=-1, keepdims=True) * ATTN_SCALE
        sink = sinks_ref[j][None]
        m = jnp.maximum(jnp.maximum(jnp.max(s_c, axis=-1, keepdims=True), s_n), sink)
        p_c = jnp.exp(s_c - m)
        p_n = jnp.exp(s_n - m)
        inv = 1.0 / (jnp.sum(p_c, axis=-1, keepdims=True) + p_n + jnp.exp(sink - m))
        o = jnp.einsum("bgs,bsd->bgd", (p_c * inv).astype(BF16), cvb, preferred_element_type=F32)
        att_ref[:, j * GROUP:(j + 1) * GROUP, :] = o + (p_n * inv) * vn
    last = lax.broadcasted_iota(jnp.int32, ck.shape, 1) == WINDOW - 1
    kwin_ref[...] = jnp.where(last, kn, pltpu.roll(ck, WINDOW - 1, axis=1))
    vwin_ref[...] = jnp.where(last, vn, pltpu.roll(cv, WINDOW - 1, axis=1))


def _sample_attn_call(q_wide, kn_row, vn_row, ck, cv, sinks3):
    n = q_wide.shape[0]
    tb = SAMPLE_ATTN_TILE
    b3 = lambda i: (i, 0, 0)
    return pl.pallas_call(
        _sample_attn_kernel,
        grid=(n // tb,),
        in_specs=[pl.BlockSpec((tb, N_HEADS, KV_WIDTH), b3),
                  pl.BlockSpec((tb, 1, KV_WIDTH), b3),
                  pl.BlockSpec((tb, 1, KV_WIDTH), b3),
                  pl.BlockSpec((tb, WINDOW, KV_WIDTH), b3),
                  pl.BlockSpec((tb, WINDOW, KV_WIDTH), b3),
                  pl.BlockSpec((N_KV, GROUP, 1), lambda i: (0, 0, 0))],
        out_specs=(pl.BlockSpec((tb, N_HEADS, KV_WIDTH), b3),
                   pl.BlockSpec((tb, WINDOW, KV_WIDTH), b3),
                   pl.BlockSpec((tb, WINDOW, KV_WIDTH), b3)),
        out_shape=(jax.ShapeDtypeStruct((n, N_HEADS, KV_WIDTH), F32),
                   jax.ShapeDtypeStruct((n, WINDOW, KV_WIDTH), F32),
                   jax.ShapeDtypeStruct((n, WINDOW, KV_WIDTH), F32)),
        compiler_params=pltpu.CompilerParams(dimension_semantics=("arbitrary",),
                                             vmem_limit_bytes=VMEM_LIMIT_BYTES),
        name="sample_attn",
    )(q_wide, kn_row, vn_row, ck, cv, sinks3)


def _sample_mix_kernel(x_ref, u_ref, att_ref, st_ref, h0_ref, w_conv_ref, b_conv_ref, w_gates_ref, b_rg_ref,
                       b_ig_ref, lam_ref, w_out_ref, ln_g_ref, ln_b_ref,
                       x1_ref, conv_ref, h_ref):
    xl = u_ref[:, OFF_XL:OFF_XL + LRU_WIDTH]
    wc = w_conv_ref[...]
    xc = wc[0:1] * st_ref[0]
    xc = xc + wc[1:2] * st_ref[1]
    xc = xc + wc[2:3] * st_ref[2]
    xc = xc + wc[3:4] * xl + b_conv_ref[...]
    conv_ref[0] = st_ref[1]
    conv_ref[1] = st_ref[2]
    conv_ref[2] = xl
    a, u = _lru_gates(xc, w_gates_ref, b_rg_ref[...], b_ig_ref[...], lam_ref[...])
    h = a * h0_ref[...] + u
    h_ref[...] = h
    rec = h * jax.nn.gelu(u_ref[:, OFF_YL:OFF_YL + LRU_WIDTH])
    x1_ref[...] = _merge_norm(x_ref[...], rec, att_ref[...], u_ref[:, OFF_GL:OFF_GL + D_MODEL],
                              u_ref[:, OFF_GA:OFF_GA + D_MODEL], w_out_ref,
                              ln_g_ref[...], ln_b_ref[...])


def _sample_mix_call(x_s, u_s, att, st_t, h0, wts):
    n = x_s.shape[0]
    weight_args = (wts["w_conv"], wts["b_conv"], wts["w_gates"], wts["b_rg"], wts["b_ig"], wts["lam"],
                   wts["w_out"], wts["ln1_g"], wts["ln1_b"])
    args = (x_s, u_s, att, st_t, h0) + weight_args
    out_shapes = ((n, D_MODEL), (CONV_W - 1, n, LRU_WIDTH), (n, LRU_WIDTH))
    return pl.pallas_call(
        _sample_mix_kernel,
        grid=(1,),
        in_specs=[_const_spec(a.shape) for a in args],
        out_specs=tuple(_const_spec(s) for s in out_shapes),
        out_shape=tuple(jax.ShapeDtypeStruct(s, F32) for s in out_shapes),
        compiler_params=pltpu.CompilerParams(dimension_semantics=("arbitrary",),
                                             vmem_limit_bytes=VMEM_LIMIT_BYTES),
        name="sample_mix",
    )(*args)


def _one_hot(mask):
    return jnp.where(mask, 1.0, 0.0).astype(BF16)


def _pack_rows(x):
    half = x.shape[1] // 2
    lo = lax.shift_right_logical(lax.bitcast_convert_type(x[:, :half], jnp.uint32), jnp.uint32(16))
    hi = lax.bitcast_convert_type(x[:, half:], jnp.uint32) & jnp.uint32(0xFFFF0000)
    return lo | hi


def _unpack_rows(words):
    lo = lax.bitcast_convert_type(lax.shift_left(words, jnp.uint32(16)), F32)
    hi = lax.bitcast_convert_type(words & jnp.uint32(0xFFFF0000), F32)
    return jnp.concatenate([lo.astype(BF16), hi.astype(BF16)], axis=1)


class _Dispatcher:
    def __init__(self, xs_hbm, goff_ref, gtot_ref, stage, zero_rows, tri, xb_keep, key_keep, g_vmem, gvec,
                 g_smem, nd_smem, rounds_smem, sem_stage, sem_g, sem_zero, seg_cap):
        self.xs_hbm, self.goff_ref, self.gtot_ref = xs_hbm, goff_ref, gtot_ref
        self.stage, self.zero_rows, self.tri = stage, zero_rows, tri
        self.xb_keep, self.key_keep = xb_keep, key_keep
        self.g_vmem, self.gvec, self.g_smem = g_vmem, gvec, g_smem
        self.nd_smem, self.rounds_smem = nd_smem, rounds_smem
        self.sem_stage, self.sem_g, self.sem_zero = sem_stage, sem_g, sem_zero
        self.seg_cap = seg_cap

    def _g_copy(self):
        return pltpu.make_async_copy(self.g_vmem.at[pl.ds(0, 1)], self.g_smem, self.sem_g)

    def init(self):
        n_assign = self.tri.shape[0]
        r = lax.broadcasted_iota(jnp.int32, (n_assign, n_assign), 0)
        c = lax.broadcasted_iota(jnp.int32, (n_assign, n_assign), 1)
        self.tri[...] = _one_hot(r < c)
        self.gvec[...] = jnp.zeros_like(self.gvec)
        self.g_vmem[...] = jnp.zeros_like(self.g_vmem)
        self.nd_smem[0] = 0
        self._g_copy().start()

    def _sorted_rows(self, e_row, rank_row, xb, lo):
        td = xb.shape[0]
        slot = lax.broadcasted_iota(jnp.int32, (N_SLOTS, td), 0).astype(F32)
        lo_f = lo.astype(F32)
        in_round = (rank_row >= lo_f) & (rank_row < lo_f + SLOT_CAP) & (e_row >= 0.0)
        key = jnp.where(in_round, e_row * SLOT_CAP + (rank_row - lo_f), -1.0)
        perm = jnp.where(slot == key[:, :td], 1.0, jnp.where(slot == key[:, td:], 1.0, 0.0)).astype(BF16)
        return _pack_rows(jnp.dot(perm, xb, preferred_element_type=F32))

    def plan(self, x1, route, n_valid, buf):
        td = x1.shape[0]
        n_assign = TOP_K * td
        valid_col = lax.broadcasted_iota(jnp.int32, (td, 1), 0) < n_valid
        lane = lax.broadcasted_iota(jnp.int32, (td, ROUTE_WIDTH), 1).astype(F32)
        e0c = jnp.where(valid_col, route[:, 0:1], -1.0)
        e1c = jnp.where(valid_col, route[:, 1:2], -1.0)
        cnt_row = jnp.sum(jnp.where(lane == e0c, 1.0, 0.0) + jnp.where(lane == e1c, 1.0, 0.0),
                          axis=0, keepdims=True)
        self.rounds_smem[0] = (jnp.max(cnt_row).astype(jnp.int32) + SLOT_CAP - 1) // SLOT_CAP
        self.goff_ref[0] = self.gvec[0:1, :].astype(jnp.int32)
        self.gvec[0:1, :] = self.gvec[0:1, :] + jnp.ceil(cnt_row * (1.0 / SUBLANES)) * SUBLANES
        self.gtot_ref[...] = self.gvec[0:1, :].astype(jnp.int32)

        route_t = route.T
        valid_row = lax.broadcasted_iota(jnp.int32, (1, td), 1) < n_valid
        e_row = jnp.concatenate([jnp.where(valid_row, route_t[0:1, :], -1.0),
                                 jnp.where(valid_row, route_t[1:2, :], -1.0)], axis=1)
        expert_sub = lax.broadcasted_iota(jnp.int32, (N_EXPERTS, n_assign), 0).astype(F32)
        onehot_t = expert_sub == e_row
        rank_t = jnp.dot(_one_hot(onehot_t), self.tri[...], preferred_element_type=F32)
        rank_row = jnp.sum(jnp.where(onehot_t, rank_t, 0.0), axis=0, keepdims=True)
        xb = jnp.where(valid_col, x1, 0.0).astype(BF16)
        self.xb_keep[...] = xb
        self.key_keep[0:1, :] = e_row
        self.key_keep[1:2, :] = rank_row
        self.stage[buf] = self._sorted_rows(e_row, rank_row, xb, jnp.int32(0))

    def _stage_copy(self, buf, e, dst_row):
        src = self.stage.at[buf, pl.ds(e * SLOT_CAP, SLOT_CAP)]
        return pltpu.make_async_copy(src, self.xs_hbm.at[pl.ds(dst_row, SLOT_CAP)], self.sem_stage)

    def _wait_outstanding(self):
        @pl.when(self.nd_smem[0] > 0)
        def _():
            for _ in range(N_EXPERTS):
                self._stage_copy(0, 0, 0).wait()
        self.nd_smem[0] = 0

    def _start_round(self, buf, lo):
        for e in range(N_EXPERTS):
            dst_row = pl.multiple_of(e * self.seg_cap + self.g_smem[0, e] + lo, SUBLANES)
            self._stage_copy(buf, e, dst_row).start()
        self.nd_smem[0] = N_EXPERTS

    def flush(self, buf, is_last):
        self._g_copy().wait()
        self._wait_outstanding()
        self._start_round(buf, 0)

        def later_round(r, carry):
            lo = r * SLOT_CAP
            rows = self._sorted_rows(self.key_keep[0:1, :], self.key_keep[1:2, :], self.xb_keep[...], lo)
            self._wait_outstanding()
            self.stage[buf] = rows
            self._start_round(buf, lo)
            return carry
        lax.fori_loop(1, self.rounds_smem[0], later_round, 0)

        self.g_vmem[0:1, :] = self.gvec[0:1, :].astype(jnp.int32)
        self._g_copy().start()

        @pl.when(is_last)
        def _():
            self._g_copy().wait()
            self._wait_outstanding()
            self.zero_rows[...] = jnp.zeros_like(self.zero_rows)

            def pad_copy(dst_row):
                dst = self.xs_hbm.at[pl.ds(pl.multiple_of(dst_row, SUBLANES), PAD_CHUNK)]
                return pltpu.make_async_copy(self.zero_rows, dst, self.sem_zero)

            def pad_segment(e, n_started):
                fill = self.g_smem[0, e]
                block_end = (fill + SLOT_CAP + MOE_TILE - 1) // MOE_TILE * MOE_TILE
                n_chunks = (block_end - fill + PAD_CHUNK - 1) // PAD_CHUNK

                def start(c, carry):
                    pad_copy(e * self.seg_cap + fill + c * PAD_CHUNK).start()
                    return carry
                lax.fori_loop(0, n_chunks, start, 0)
                return n_started + n_chunks
            n_started = lax.fori_loop(0, N_EXPERTS, pad_segment, 0)

            def wait(_, carry):
                pad_copy(0).wait()
                return carry
            lax.fori_loop(0, n_started, wait, 0)


def _dispatch_tiles(n_tokens):
    return (n_tokens + DISPATCH_TILE - 1) // DISPATCH_TILE


def _segment_capacity(n_tokens):
    assert SEG_PAD >= DISPATCH_TILE
    alignment_slack = (SUBLANES - 1) * _dispatch_tiles(n_tokens)
    return (n_tokens + alignment_slack + SEG_PAD + MOE_TILE - 1) // MOE_TILE * MOE_TILE


def _moe_kernel(blk_e_ref, blk_j_ref, n_used_ref, xs_ref, wg_ref, wu_ref, wd_ref, ys_ref, wg_bf, wu_bf, wd_bf):
    del blk_j_ref
    b = pl.program_id(0)

    @pl.when(b < n_used_ref[0])
    def _():
        prev_e = blk_e_ref[jnp.maximum(b - 1, 0)]

        @pl.when((b == 0) | (blk_e_ref[b] != prev_e))
        def _():
            wg_bf[...] = wg_ref[0].astype(BF16)
            wu_bf[...] = wu_ref[0].astype(BF16)
            wd_bf[...] = wd_ref[0].astype(BF16)

        xb = _unpack_rows(xs_ref[...])
        g = jnp.dot(xb, wg_bf[...], preferred_element_type=F32)
        u = jnp.dot(xb, wu_bf[...], preferred_element_type=F32)
        hmid = (g * _sigmoid(g)) * u
        y = jnp.dot(hmid.astype(BF16), wd_bf[...], preferred_element_type=F32)
        ys_ref[...] = _pack_rows(y.astype(BF16).astype(F32))


def _expert_blocks(gtot, n_blocks):
    rows = gtot[0, :N_EXPERTS]
    nb = (rows + SLOT_CAP + MOE_TILE - 1) // MOE_TILE
    ends = jnp.cumsum(nb)
    n_used = ends[-1]
    step = jnp.minimum(jnp.arange(n_blocks, dtype=jnp.int32), n_used - 1)
    blk_e = jnp.sum((step[:, None] >= ends[None, :]).astype(jnp.int32), axis=1)
    blk_j = step - (ends - nb)[blk_e]
    return blk_e, blk_j.astype(jnp.int32), n_used.reshape(1).astype(jnp.int32)


def _moe_call(gtot, xs, w_gate, w_up, w_down, n_tokens):
    seg_cap = _segment_capacity(n_tokens)
    seg_blocks = seg_cap // MOE_TILE
    max_rows = n_tokens * TOP_K + N_EXPERTS * (SUBLANES - 1) * _dispatch_tiles(n_tokens)
    n_blocks = (max_rows + N_EXPERTS * (SLOT_CAP + MOE_TILE - 1)) // MOE_TILE
    blk_e, blk_j, n_used = _expert_blocks(gtot, n_blocks)
    row_block = lambda b, be, bj, nu: (be[b] * seg_blocks + bj[b], 0)
    weight_block = lambda b, be, bj, nu: (be[b], 0, 0)
    grid_spec = pltpu.PrefetchScalarGridSpec(
        num_scalar_prefetch=3,
        grid=(n_blocks,),
        in_specs=[
            pl.BlockSpec((MOE_TILE, PACKED_WIDTH), row_block),
            pl.BlockSpec((1, D_MODEL, EXPERT_FF), weight_block),
            pl.BlockSpec((1, D_MODEL, EXPERT_FF), weight_block),
            pl.BlockSpec((1, EXPERT_FF, D_MODEL), weight_block),
        ],
        out_specs=pl.BlockSpec((MOE_TILE, PACKED_WIDTH), row_block),
        scratch_shapes=[
            pltpu.VMEM((D_MODEL, EXPERT_FF), BF16),
            pltpu.VMEM((D_MODEL, EXPERT_FF), BF16),
            pltpu.VMEM((EXPERT_FF, D_MODEL), BF16),
        ],
    )
    return pl.pallas_call(
        _moe_kernel,
        grid_spec=grid_spec,
        out_shape=jax.ShapeDtypeStruct(xs.shape, xs.dtype),
        compiler_params=pltpu.CompilerParams(dimension_semantics=("arbitrary",),
                                             vmem_limit_bytes=VMEM_LIMIT_BYTES),
        name="moe_experts",
    )(blk_e, blk_j, n_used, xs, w_gate, w_up, w_down)


def _combine_kernel(gcur_ref, gnext_ref, x1_ref, route_ref, ys_hbm, ln_g_ref, ln_b_ref, y_ref,
                    stage, acc, tri, sem, *, seg_cap):
    td = x1_ref.shape[0]
    n_assign = TOP_K * td
    i = pl.program_id(0)
    buf = lax.rem(i, 2)

    def stage_copy(g_ref, e, first_rank, to_buf):
        src_row = pl.multiple_of(e * seg_cap + g_ref[0, 0, e] + first_rank, SUBLANES)
        slot0 = e * SLOT_CAP
        if not isinstance(slot0, int):
            slot0 = pl.multiple_of(slot0, SLOT_CAP)
        dst = stage.at[to_buf, pl.ds(slot0, SLOT_CAP)]
        return pltpu.make_async_copy(ys_hbm.at[pl.ds(src_row, SLOT_CAP)], dst, sem.at[to_buf])

    def fetch_first_round(g_ref, to_buf):
        for e in range(N_EXPERTS):
            stage_copy(g_ref, e, 0, to_buf).start()

    @pl.when(i == 0)
    def _():
        r = lax.broadcasted_iota(jnp.int32, (n_assign, n_assign), 0)
        c = lax.broadcasted_iota(jnp.int32, (n_assign, n_assign), 1)
        tri[...] = _one_hot(c < r)
        fetch_first_round(gcur_ref, 0)

    def count(e):
        return gnext_ref[0, 0, e] - gcur_ref[0, 0, e]

    route = route_ref[...]
    e0, e1, w0, w1 = route[:, 0:1], route[:, 1:2], route[:, 2:3], route[:, 3:4]
    lane = lax.broadcasted_iota(jnp.int32, (td, ROUTE_WIDTH), 1).astype(F32)
    onehot = jnp.concatenate([lane == e0, lane == e1], axis=0)
    onehot_f = jnp.where(onehot, 1.0, 0.0)
    rank_mat = jnp.dot(tri[...], onehot_f.astype(BF16), preferred_element_type=F32)
    rank = jnp.sum(jnp.where(onehot, rank_mat, 0.0), axis=1, keepdims=True)
    r0, r1 = rank[:td], rank[td:]
    cnt_row = jnp.sum(onehot_f, axis=0, keepdims=True)
    rounds = (jnp.max(cnt_row).astype(jnp.int32) + SLOT_CAP - 1) // SLOT_CAP

    slot_lane = lax.broadcasted_iota(jnp.int32, (td, N_SLOTS), 1).astype(F32)

    def wait_copies(n):
        def wait_one(_, c):
            stage_copy(gcur_ref, 0, 0, buf).wait()
            return c
        lax.fori_loop(0, n, wait_one, 0)

    def weighted_rows(lo):
        lo_f = lo.astype(F32)
        rows = _unpack_rows(stage[buf])

        def selector(e_col, r_col, w_col):
            in_round = (r_col >= lo_f) & (r_col < lo_f + SLOT_CAP)
            key = jnp.where(in_round, e_col * SLOT_CAP + (r_col - lo_f), -1.0)
            return jnp.where(slot_lane == key, w_col, 0.0)
        mix = (selector(e0, r0, w0) + selector(e1, r1, w1)).astype(BF16)
        return jnp.dot(mix, rows, preferred_element_type=F32)

    for _ in range(N_EXPERTS):
        stage_copy(gcur_ref, 0, 0, buf).wait()

    fetch_first_round(gnext_ref, 1 - buf)

    acc[...] = weighted_rows(jnp.int32(0))

    def later_round(r, carry):
        lo = r * SLOT_CAP

        def start(e, n_started):
            has_rows = count(e) > lo

            @pl.when(has_rows)
            def _():
                stage_copy(gcur_ref, e, lo, buf).start()
            return n_started + has_rows.astype(jnp.int32)
        wait_copies(lax.fori_loop(0, N_EXPERTS, start, 0))
        acc[...] = acc[...] + weighted_rows(lo)
        return carry
    lax.fori_loop(1, rounds, later_round, 0)

    y_ref[...] = _layer_norm(ALPHA * x1_ref[...] + acc[...], ln_g_ref[...], ln_b_ref[...])

    @pl.when(i + 1 == pl.num_programs(0))
    def _():
        for _ in range(N_EXPERTS):
            stage_copy(gnext_ref, 0, 0, 1 - buf).wait()


def _combine_call(goff, x1_all, route_all, ys, ln_g, ln_b, first_row, n_rows, tile, n_tokens):
    assert first_row % DISPATCH_TILE == 0 and first_row % tile == 0
    assert tile == DISPATCH_TILE or n_rows == tile
    first_block = first_row // tile
    first_goff = first_row // DISPATCH_TILE
    n_assign = TOP_K * tile
    grid_spec = pl.GridSpec(
        grid=(n_rows // tile,),
        in_specs=[
            pl.BlockSpec((1, 1, ROUTE_WIDTH), lambda i: (first_goff + i, 0, 0), memory_space=pltpu.SMEM),
            pl.BlockSpec((1, 1, ROUTE_WIDTH), lambda i: (first_goff + i + 1, 0, 0), memory_space=pltpu.SMEM),
            pl.BlockSpec((tile, D_MODEL), lambda i: (first_block + i, 0)),
            pl.BlockSpec((tile, ROUTE_WIDTH), lambda i: (first_block + i, 0)),
            pl.BlockSpec(memory_space=pl.ANY),
            _const_spec(ln_g.shape),
            _const_spec(ln_b.shape),
        ],
        out_specs=pl.BlockSpec((tile, D_MODEL), lambda i: (i, 0)),
        scratch_shapes=[pltpu.VMEM((2, N_SLOTS, PACKED_WIDTH), jnp.uint32),
                        pltpu.VMEM((tile, D_MODEL), F32),
                        pltpu.VMEM((n_assign, n_assign), BF16),
                        pltpu.SemaphoreType.DMA((2,))],
    )
    return pl.pallas_call(
        functools.partial(_combine_kernel, seg_cap=_segment_capacity(n_tokens)),
        grid_spec=grid_spec,
        out_shape=jax.ShapeDtypeStruct((n_rows, D_MODEL), F32),
        compiler_params=pltpu.CompilerParams(dimension_semantics=("arbitrary",),
                                             vmem_limit_bytes=VMEM_LIMIT_BYTES),
        name="moe_combine",
    )(goff, goff, x1_all, route_all, ys, ln_g, ln_b)


def _prepare_weights(w_in, b_in, w_conv, b_conv, w_rg, b_rg, w_ig, b_ig, lru_lambda, w_lru_out, w_attn_out, w_o,
                     ln1_g, ln1_b, w_group, b_group, w_router, b_router):
    blocks_per_chunk = GATE_CHUNK // LRU_BLOCK

    def chunked_block_diag(w):
        w = w.reshape(N_GATE_CHUNKS, blocks_per_chunk, LRU_BLOCK, LRU_BLOCK)
        eye = jnp.eye(blocks_per_chunk, dtype=w.dtype)
        return jnp.einsum("cbij,bd->cbidj", w, eye).reshape(N_GATE_CHUNKS, GATE_CHUNK, GATE_CHUNK)

    w_gates = jnp.concatenate([chunked_block_diag(w_rg), chunked_block_diag(w_ig)], axis=-1).astype(BF16)
    w_rt = jnp.concatenate([w_group, w_router], axis=1)
    w_rt = jnp.pad(w_rt, ((0, 0), (0, ROUTE_WIDTH - w_rt.shape[1])))
    w_rt_hi = w_rt.astype(BF16)
    w_rt_lo = jnp.concatenate([w_rt_hi, (w_rt - w_rt_hi.astype(F32)).astype(BF16)], axis=1)
    b_rt = jnp.pad(jnp.concatenate([b_group, b_router]), (0, ROUTE_WIDTH - N_GROUPS - N_EXPERTS))
    row = lambda v: v.reshape(1, -1)
    return dict(
        w_in=w_in.astype(BF16), b_in=row(b_in), w_conv=w_conv, b_conv=row(b_conv), w_gates=w_gates,
        b_rg=row(b_rg), b_ig=row(b_ig), lam=row(lru_lambda),
        w_out=jnp.stack([w_lru_out, w_attn_out, w_o]).astype(BF16),
        ln1_g=row(ln1_g), ln1_b=row(ln1_b), w_rt_hi=w_rt_hi, w_rt_lo=w_rt_lo, b_rt=row(b_rt))


def kernel(x_prompt, x_sample, cache_k, cache_v, state_conv, state_lru_h, w_in, b_in, w_conv, b_conv, w_rg, b_rg,
           w_ig, b_ig, lru_lambda, sinks, w_lru_out, w_attn_out, w_o, ln1_g, ln1_b, w_group, b_group, w_router,
           b_router, w_gate, w_up, w_down, ln2_g, ln2_b):
    B, S, _ = x_prompt.shape
    n_prompt = B * S
    n_sample = x_sample.shape[0]
    n_all = n_prompt + n_sample
    wts = _prepare_weights(w_in, b_in, w_conv, b_conv, w_rg, b_rg, w_ig, b_ig, lru_lambda, w_lru_out, w_attn_out,
                           w_o, ln1_g, ln1_b, w_group, b_group, w_router, b_router)

    x_s = x_sample.reshape(n_sample, D_MODEL)
    u_s = _sample_proj_call(x_s, wts["w_in"], wts["b_in"])
    q4 = u_s[:, OFF_Q:OFF_K].reshape(n_sample, N_KV, GROUP, HEAD_DIM)
    q_wide = jnp.einsum("njgd,jk->njgkd", q4, jnp.eye(N_KV, dtype=F32)).reshape(n_sample, N_HEADS, KV_WIDTH)
    k_new = u_s[:, OFF_K:OFF_V]
    v_new = u_s[:, OFF_V:OFF_GL]
    att_wide, k_win_s, v_win_s = _sample_attn_call(
        q_wide, k_new.reshape(n_sample, 1, KV_WIDTH), v_new.reshape(n_sample, 1, KV_WIDTH),
        cache_k.reshape(n_sample, WINDOW, KV_WIDTH), cache_v.reshape(n_sample, WINDOW, KV_WIDTH),
        sinks.reshape(N_KV, GROUP, 1))
    att5 = att_wide.reshape(n_sample, N_KV, GROUP, N_KV, HEAD_DIM)
    att_s = jnp.stack([att5[:, j, :, j, :] for j in range(N_KV)], axis=1).reshape(n_sample, N_HEADS * HEAD_DIM)
    x1_s, conv_s_t, h_s = _sample_mix_call(
        x_s, u_s, att_s, jnp.transpose(state_conv, (1, 0, 2)), state_lru_h, wts)

    x1_all, route_all, k_win_p, v_win_p, conv_p, h_p, xs, goff, gtot = _mixer_call(x_prompt, x1_s, sinks, wts)

    ys = _moe_call(gtot, xs, w_gate, w_up, w_down, n_all)
    goff = jnp.concatenate([goff, gtot[None]], axis=0)
    ln2_g2, ln2_b2 = ln2_g.reshape(1, -1), ln2_b.reshape(1, -1)
    y_p = _combine_call(goff, x1_all, route_all, ys, ln2_g2, ln2_b2, 0, n_prompt, DISPATCH_TILE, n_all)
    y_s = _combine_call(goff, x1_all, route_all, ys, ln2_g2, ln2_b2, n_prompt, n_sample, n_sample, n_all)

    kv_shape = (WINDOW, N_KV, HEAD_DIM)
    return (y_p.reshape(B, S, D_MODEL), y_s.reshape(n_sample, 1, D_MODEL),
            k_win_p.reshape((B,) + kv_shape), v_win_p.reshape((B,) + kv_shape), conv_p, h_p.reshape(B, LRU_WIDTH),
            k_win_s.reshape((n_sample,) + kv_shape), v_win_s.reshape((n_sample,) + kv_shape),
            jnp.transpose(conv_s_t, (1, 0, 2)), h_s)
```

```python
import functools

import jax
import jax.numpy as jnp
from jax import lax
from jax.experimental import pallas as pl
from jax.experimental.pallas import tpu as pltpu

F32 = jnp.float32
BF16 = jnp.bfloat16

D_MODEL = 1024
LRU_WIDTH = 1024
LRU_BLOCK = 64
CONV_W = 4
LRU_C = 8.0
N_HEADS = 16
N_KV = 4
GROUP = N_HEADS // N_KV
HEAD_DIM = 64
KV_WIDTH = N_KV * HEAD_DIM
WINDOW = 128
NEG_INF = -1e30
N_GROUPS = 4
EXPERTS_PER_GROUP = 8
N_EXPERTS = N_GROUPS * EXPERTS_PER_GROUP
TOP_K = 2
EXPERT_FF = D_MODEL // 2
DEPTH = 1
ALPHA = (2 * DEPTH) ** 0.25
LN_EPS = 1e-5
ATTN_SCALE = HEAD_DIM ** -0.5
LOG2_E = 1.4426950408889634

OFF_XL = 0
OFF_YL = OFF_XL + LRU_WIDTH
OFF_Q = OFF_YL + LRU_WIDTH
OFF_K = OFF_Q + N_HEADS * HEAD_DIM
OFF_V = OFF_K + KV_WIDTH
OFF_GL = OFF_V + KV_WIDTH
OFF_GA = OFF_GL + D_MODEL
IN_WIDTH = OFF_GA + D_MODEL

LANES = 128
SUBLANES = 8
MXU_DIM = 256
VMEM_LIMIT_BYTES = 56 * 1024 * 1024

GATE_CHUNK = MXU_DIM
N_GATE_CHUNKS = LRU_WIDTH // GATE_CHUNK
ROUTE_WIDTH = LANES

SEQ_TILE = 256
MOE_TILE = 1392
DISPATCH_TILE = 256
SLOT_CAP = 32
N_SLOTS = N_EXPERTS * SLOT_CAP
PAD_CHUNK = 128
SEG_PAD = MOE_TILE + SLOT_CAP + PAD_CHUNK
PACKED_WIDTH = D_MODEL // 2
SAMPLE_ATTN_TILE = 16
SAMPLE_PROJ_TILE = 1408


def _const_spec(shape):
    nd = len(shape)
    return pl.BlockSpec(shape, lambda *_: (0,) * nd)


def _layer_norm(z, g, b):
    mu = jnp.mean(z, axis=-1, keepdims=True)
    zc = z - mu
    var = jnp.mean(zc * zc, axis=-1, keepdims=True)
    return zc * lax.rsqrt(var + LN_EPS) * g + b


def _sigmoid(x):
    return 1.0 / (1.0 + jnp.exp2(x * -LOG2_E))


def _softplus(x):
    return jnp.maximum(x, 0.0) + jnp.log1p(jnp.exp(-jnp.abs(x)))


def _lru_gates(xc, w_gates_ref, b_rg, b_ig, lam):
    xcb = xc.astype(BF16)
    r_parts, i_parts = [], []
    for c in range(N_GATE_CHUNKS):
        g = jnp.dot(xcb[:, c * GATE_CHUNK:(c + 1) * GATE_CHUNK], w_gates_ref[c], preferred_element_type=F32)
        r_parts.append(g[:, :GATE_CHUNK])
        i_parts.append(g[:, GATE_CHUNK:])
    r = _sigmoid(jnp.concatenate(r_parts, axis=1) + b_rg)
    i = _sigmoid(jnp.concatenate(i_parts, axis=1) + b_ig)
    log_a = (-LRU_C * r) * _softplus(-lam)
    a = jnp.exp(log_a)
    gain_sq = 1.0 - a * a
    gain = jnp.where(gain_sq > 0.0, gain_sq * lax.rsqrt(gain_sq), 0.0)
    u = gain * (i * xc)
    return a, u


def _linear_scan(a, u, h_in):
    n, w = a.shape
    groups = n // SUBLANES
    a3 = a.reshape(groups, SUBLANES, w)
    u3 = u.reshape(groups, SUBLANES, w)
    row = lax.broadcasted_iota(jnp.int32, a3.shape, 1)
    d = 1
    while d < SUBLANES:
        has_prev = row >= d
        u3 = u3 + a3 * jnp.where(has_prev, pltpu.roll(u3, d, axis=1), 0.0)
        a3 = a3 * jnp.where(has_prev, pltpu.roll(a3, d, axis=1), 1.0)
        d *= 2
    carry = h_in
    out = []
    for g in range(groups):
        h_g = u3[g] + a3[g] * carry
        out.append(h_g)
        carry = h_g[SUBLANES - 1:SUBLANES, :]
    return jnp.concatenate(out, axis=0)


def _route(x1, w_hi_ref, w_hilo_ref, b_rt):
    x_hi = x1.astype(BF16)
    x_lo = (x1 - x_hi.astype(F32)).astype(BF16)
    both = jnp.dot(x_hi, w_hilo_ref[...], preferred_element_type=F32)
    logits = (both[:, :ROUTE_WIDTH]
              + (jnp.dot(x_lo, w_hi_ref[...], preferred_element_type=F32) + both[:, ROUTE_WIDTH:])) + b_rt
    col = lax.broadcasted_iota(jnp.int32, logits.shape, 1)
    big = jnp.int32(ROUTE_WIDTH)
    is_g = col < N_GROUPS
    gl = jnp.where(is_g, logits, -jnp.inf)
    gmax = jnp.max(gl, axis=-1, keepdims=True)
    g_idx = jnp.min(jnp.where(gl == gmax, col, big), axis=-1, keepdims=True)
    p_g = 1.0 / jnp.sum(jnp.where(is_g, jnp.exp(gl - gmax), 0.0), axis=-1, keepdims=True)
    lo = N_GROUPS + g_idx * EXPERTS_PER_GROUP
    in_grp = (col >= lo) & (col < lo + EXPERTS_PER_GROUP)
    el = jnp.where(in_grp, logits, -jnp.inf)
    v1 = jnp.max(el, axis=-1, keepdims=True)
    i1 = jnp.min(jnp.where(el == v1, col, big), axis=-1, keepdims=True)
    el2 = jnp.where(col == i1, -jnp.inf, el)
    v2 = jnp.max(el2, axis=-1, keepdims=True)
    i2 = jnp.min(jnp.where(el2 == v2, col, big), axis=-1, keepdims=True)
    e21 = jnp.exp(v2 - v1)
    inv = 1.0 / (1.0 + e21)
    w1 = p_g * inv
    w2 = p_g * (e21 * inv)
    e1 = (i1 - N_GROUPS).astype(F32)
    e2 = (i2 - N_GROUPS).astype(F32)
    return jnp.where(col == 0, e1, jnp.where(col == 1, e2, jnp.where(col == 2, w1, jnp.where(col == 3, w2, 0.0))))


def _merge_norm(x, rec, att, g_l, g_a, w_out_ref, ln_g, ln_b):
    rec_o = jnp.dot(rec.astype(BF16), w_out_ref[0], preferred_element_type=F32)
    att_o = jnp.dot(att.astype(BF16), w_out_ref[1], preferred_element_type=F32)
    merged = _sigmoid(g_l) * rec_o + _sigmoid(g_a) * att_o
    mix = jnp.dot(merged.astype(BF16), w_out_ref[2], preferred_element_type=F32)
    return _layer_norm(ALPHA * x + mix, ln_g, ln_b)


def _mixer_kernel(sinks_ref, x_ref, w_in_ref, b_in_ref, w_conv_ref, b_conv_ref, w_gates_ref, b_rg_ref, b_ig_ref,
                  lam_ref, w_out_ref, ln_g_ref, ln_b_ref, w_rt_hi_ref, w_rt_lo_ref,
                  b_rt_ref, x1_s_ref,
                  x1_ref, route_ref, kwin_ref, vwin_ref, conv_ref, h_ref, xs_hbm, goff_ref, gtot_ref,
                  conv_buf, h_carry, kcat, vcat, att_buf, prev_x1, stage, zero_rows, tri, xb_keep, key_keep,
                  g_vmem, gvec, g_smem, nd_smem, rounds_smem, sem_stage, sem_g, sem_zero,
                  *, tiles_per_seq, n_tiles, seg_cap):
    step = pl.program_id(0)
    last_step = pl.num_programs(0) - 1
    buf = lax.rem(step, 2)
    n_sample = x1_s_ref.shape[0]
    disp = _Dispatcher(xs_hbm, goff_ref, gtot_ref, stage, zero_rows, tri, xb_keep, key_keep, g_vmem, gvec,
                       g_smem, nd_smem, rounds_smem, sem_stage, sem_g, sem_zero, seg_cap)
    n_valid = jnp.where(step == 0, 0, jnp.where(step == last_step, n_sample, SEQ_TILE))

    @pl.when(step == 0)
    def _():
        disp.init()
        prev_x1[...] = jnp.zeros_like(prev_x1)

    def route_and_plan():
        x1_prev = prev_x1[...]
        route = _route(x1_prev, w_rt_hi_ref, w_rt_lo_ref, b_rt_ref[...])
        route_ref[...] = route
        disp.plan(x1_prev, route, n_valid, buf)

    @pl.when(step < n_tiles)
    def _():
        @pl.when(lax.rem(step, tiles_per_seq) == 0)
        def _():
            conv_buf[...] = jnp.zeros_like(conv_buf)
            h_carry[...] = jnp.zeros_like(h_carry)
            kcat[0:WINDOW, :] = jnp.zeros((WINDOW, KV_WIDTH), BF16)
            vcat[0:WINDOW, :] = jnp.zeros((WINDOW, KV_WIDTH), BF16)

        route_and_plan()
        _mixer_tile(lax.rem(step, tiles_per_seq), sinks_ref, x_ref, w_in_ref, b_in_ref, w_conv_ref, b_conv_ref,
                    w_gates_ref, b_rg_ref, b_ig_ref, lam_ref, w_out_ref, ln_g_ref,
                    ln_b_ref, x1_ref, prev_x1, kwin_ref, vwin_ref, conv_ref, h_ref,
                    conv_buf, h_carry, kcat, vcat, att_buf)

    @pl.when(step >= n_tiles)
    def _():
        route_and_plan()

        @pl.when(step == n_tiles)
        def _():
            x1_ref[0:n_sample, :] = x1_s_ref[...]
            prev_x1[0:n_sample, :] = x1_s_ref[...]

    disp.flush(buf, step == last_step)


def _mixer_tile(t, sinks_ref, x_ref, w_in_ref, b_in_ref, w_conv_ref, b_conv_ref, w_gates_ref, b_rg_ref, b_ig_ref,
                lam_ref, w_out_ref, ln_g_ref, ln_b_ref,
                x1_ref, x1_keep, kwin_ref, vwin_ref, conv_ref, h_ref,
                conv_buf, h_carry, kcat, vcat, att_buf):
    T = SEQ_TILE
    x = x_ref[0]
    xb = x.astype(BF16)

    def proj(lo, width):
        return jnp.dot(xb, w_in_ref[:, lo:lo + width], preferred_element_type=F32) + b_in_ref[:, lo:lo + width]

    xl = proj(OFF_XL, LRU_WIDTH)
    xl_ext = jnp.concatenate([conv_buf[...], xl], axis=0)

    def lagged(k):
        return pltpu.roll(xl_ext, k, axis=0)[SUBLANES:, :]
    wc = w_conv_ref[...]
    xc = wc[0:1] * lagged(3)
    xc = xc + wc[1:2] * lagged(2)
    xc = xc + wc[2:3] * lagged(1)
    xc = xc + wc[3:4] * xl + b_conv_ref[...]
    conv_ref[0] = xl[T - (CONV_W - 1):, :]
    conv_buf[...] = xl[T - SUBLANES:, :]

    a, u = _lru_gates(xc, w_gates_ref, b_rg_ref[...], b_ig_ref[...], lam_ref[...])
    h = _linear_scan(a, u, h_carry[0:1, :])
    h_last = h[T - 1:T, :]
    h_carry[0:1, :] = h_last
    h_ref[0] = h_last
    rec = h * jax.nn.gelu(proj(OFF_YL, LRU_WIDTH))

    q = proj(OFF_Q, N_HEADS * HEAD_DIM) * (ATTN_SCALE * LOG2_E)
    k = proj(OFF_K, KV_WIDTH)
    v = proj(OFF_V, KV_WIDTH)
    kwin_ref[0] = k[T - WINDOW:, :]
    vwin_ref[0] = v[T - WINDOW:, :]
    kcat[WINDOW:WINDOW + T, :] = k.astype(BF16)
    vcat[WINDOW:WINDOW + T, :] = v.astype(BF16)

    qi = lax.broadcasted_iota(jnp.int32, (WINDOW, 2 * WINDOW), 0)
    kj = lax.broadcasted_iota(jnp.int32, (WINDOW, 2 * WINDOW), 1)
    band = (kj > qi) & (kj <= qi + WINDOW)
    grp_row = lax.broadcasted_iota(jnp.int32, (GROUP * WINDOW, 1), 0) // WINDOW
    for qb in range(T // WINDOW):
        if qb == 0:
            first_key = jnp.where(t == 0, WINDOW, 0)
            mask1 = band & (kj >= first_key)
        else:
            mask1 = band
        bias = jnp.concatenate([jnp.where(mask1, 0.0, NEG_INF)] * GROUP, axis=0)
        r0 = qb * WINDOW
        qq = q[r0:r0 + WINDOW, :]
        for j in range(N_KV):
            kjb = kcat[r0:r0 + 2 * WINDOW, j * HEAD_DIM:(j + 1) * HEAD_DIM]
            vjb = vcat[r0:r0 + 2 * WINDOW, j * HEAD_DIM:(j + 1) * HEAD_DIM]
            qs = jnp.concatenate(
                [qq[:, (j * GROUP + g) * HEAD_DIM:(j * GROUP + g + 1) * HEAD_DIM] for g in range(GROUP)], axis=0)
            s = lax.dot_general(qs.astype(BF16), kjb, (((1,), (1,)), ((), ())), preferred_element_type=F32) + bias
            sink = jnp.zeros((GROUP * WINDOW, 1), F32)
            for g in range(GROUP):
                sink = jnp.where(grp_row == g, sinks_ref[j * GROUP + g] * LOG2_E, sink)
            m = jnp.maximum(jnp.max(s, axis=-1, keepdims=True), sink)
            p = jnp.exp2(s - m)
            inv = 1.0 / (jnp.sum(p, axis=-1, keepdims=True) + jnp.exp2(sink - m))
            o = jnp.dot((p * inv).astype(BF16), vjb, preferred_element_type=F32)
            for g in range(GROUP):
                hcol = (j * GROUP + g) * HEAD_DIM
                att_buf[r0:r0 + WINDOW, hcol:hcol + HEAD_DIM] = o[g * WINDOW:(g + 1) * WINDOW, :]
    kcat[0:WINDOW, :] = kcat[T:T + WINDOW, :]
    vcat[0:WINDOW, :] = vcat[T:T + WINDOW, :]

    x1 = _merge_norm(x, rec, att_buf[...], proj(OFF_GL, D_MODEL), proj(OFF_GA, D_MODEL),
                     w_out_ref, ln_g_ref[...], ln_b_ref[...])
    x1_ref[...] = x1
    x1_keep[...] = x1


def _mixer_call(x_prompt, x1_s, sinks, wts):
    B, S, _ = x_prompt.shape
    T = SEQ_TILE
    assert T == DISPATCH_TILE and x1_s.shape[0] <= T
    nt = S // T
    n_tiles = B * nt
    n_rows_total = B * S + x1_s.shape[0]
    n_dispatch = _dispatch_tiles(n_rows_total)
    assert n_dispatch == n_tiles + 1
    seg_cap = _segment_capacity(n_rows_total)
    n_assign = TOP_K * T
    weight_args = (wts["w_in"], wts["b_in"], wts["w_conv"], wts["b_conv"], wts["w_gates"], wts["b_rg"], wts["b_ig"],
                   wts["lam"], wts["w_out"], wts["ln1_g"], wts["ln1_b"],
                   wts["w_rt_hi"], wts["w_rt_lo"], wts["b_rt"], x1_s)
    mixed = lambda i: jnp.minimum(i, n_tiles - 1)
    seq = lambda i: mixed(i) // nt
    routed = lambda i: jnp.clip(i - 1, 0, n_dispatch - 1)
    in_specs = [pl.BlockSpec(memory_space=pltpu.SMEM),
                pl.BlockSpec((1, T, D_MODEL), lambda i: (seq(i), lax.rem(mixed(i), nt), 0))]
    in_specs += [_const_spec(w.shape) for w in weight_args]
    out_shape = (
        jax.ShapeDtypeStruct((n_rows_total, D_MODEL), F32),
        jax.ShapeDtypeStruct((n_rows_total, ROUTE_WIDTH), F32),
        jax.ShapeDtypeStruct((B, WINDOW, KV_WIDTH), F32),
        jax.ShapeDtypeStruct((B, WINDOW, KV_WIDTH), F32),
        jax.ShapeDtypeStruct((B, CONV_W - 1, LRU_WIDTH), F32),
        jax.ShapeDtypeStruct((B, 1, LRU_WIDTH), F32),
        jax.ShapeDtypeStruct((N_EXPERTS * seg_cap, PACKED_WIDTH), jnp.uint32),
        jax.ShapeDtypeStruct((n_dispatch, 1, ROUTE_WIDTH), jnp.int32),
        jax.ShapeDtypeStruct((1, ROUTE_WIDTH), jnp.int32),
    )
    out_specs = (
        pl.BlockSpec((T, D_MODEL), lambda i: (jnp.minimum(i, n_tiles), 0)),
        pl.BlockSpec((T, ROUTE_WIDTH), lambda i: (routed(i), 0)),
        pl.BlockSpec((1, WINDOW, KV_WIDTH), lambda i: (seq(i), 0, 0)),
        pl.BlockSpec((1, WINDOW, KV_WIDTH), lambda i: (seq(i), 0, 0)),
        pl.BlockSpec((1, CONV_W - 1, LRU_WIDTH), lambda i: (seq(i), 0, 0)),
        pl.BlockSpec((1, 1, LRU_WIDTH), lambda i: (seq(i), 0, 0)),
        pl.BlockSpec(memory_space=pl.ANY),
        pl.BlockSpec((1, 1, ROUTE_WIDTH), lambda i: (routed(i), 0, 0)),
        pl.BlockSpec((1, ROUTE_WIDTH), lambda i: (0, 0)),
    )
    scratch = [
        pltpu.VMEM((SUBLANES, LRU_WIDTH), F32),
        pltpu.VMEM((SUBLANES, LRU_WIDTH), F32),
        pltpu.VMEM((T + WINDOW, KV_WIDTH), BF16),
        pltpu.VMEM((T + WINDOW, KV_WIDTH), BF16),
        pltpu.VMEM((T, N_HEADS * HEAD_DIM), F32),
        pltpu.VMEM((T, D_MODEL), F32),
        pltpu.VMEM((2, N_SLOTS, PACKED_WIDTH), jnp.uint32),
        pltpu.VMEM((PAD_CHUNK, PACKED_WIDTH), jnp.uint32),
        pltpu.VMEM((n_assign, n_assign), BF16),
        pltpu.VMEM((T, D_MODEL), BF16),
        pltpu.VMEM((SUBLANES, n_assign), F32),
        pltpu.VMEM((SUBLANES, ROUTE_WIDTH), jnp.int32),
        pltpu.VMEM((SUBLANES, ROUTE_WIDTH), F32),
        pltpu.SMEM((1, ROUTE_WIDTH), jnp.int32),
        pltpu.SMEM((1,), jnp.int32),
        pltpu.SMEM((1,), jnp.int32),
        pltpu.SemaphoreType.DMA(()),
        pltpu.SemaphoreType.DMA(()),
        pltpu.SemaphoreType.DMA(()),
    ]
    return pl.pallas_call(
        functools.partial(_mixer_kernel, tiles_per_seq=nt, n_tiles=n_tiles, seg_cap=seg_cap),
        grid=(n_tiles + 2,),
        in_specs=in_specs,
        out_specs=out_specs,
        out_shape=out_shape,
        scratch_shapes=scratch,
        compiler_params=pltpu.CompilerParams(dimension_semantics=("arbitrary",),
                                             vmem_limit_bytes=VMEM_LIMIT_BYTES),
        name="mixer_prompt",
    )(sinks, x_prompt, *weight_args)


def _sample_proj_kernel(x_ref, w_ref, b_ref, u_ref):
    u_ref[...] = jnp.dot(x_ref[...].astype(BF16), w_ref[...], preferred_element_type=F32) + b_ref[...]


def _sample_proj_call(x_s, w_in, b_in):
    n = x_s.shape[0]
    tn = SAMPLE_PROJ_TILE
    return pl.pallas_call(
        _sample_proj_kernel,
        grid=(IN_WIDTH // tn,),
        in_specs=[pl.BlockSpec((n, D_MODEL), lambda c: (0, 0)),
                  pl.BlockSpec((D_MODEL, tn), lambda c: (0, c)),
                  pl.BlockSpec((1, tn), lambda c: (0, c))],
        out_specs=pl.BlockSpec((n, tn), lambda c: (0, c)),
        out_shape=jax.ShapeDtypeStruct((n, IN_WIDTH), F32),
        compiler_params=pltpu.CompilerParams(dimension_semantics=("arbitrary",)),
        name="sample_proj",
    )(x_s, w_in, b_in)


def _sample_attn_kernel(q_ref, kn_row_ref, vn_row_ref, ck_ref, cv_ref, sinks_ref,
                        att_ref, kwin_ref, vwin_ref):
    tb = q_ref.shape[0]
    ck = ck_ref[...]
    cv = cv_ref[...]
    ckb = ck.astype(BF16)
    cvb = cv.astype(BF16)
    kn = kn_row_ref[...]
    vn = vn_row_ref[...]
    key_pos = lax.broadcasted_iota(jnp.int32, (tb, GROUP, WINDOW), 2)
    for j in range(N_KV):
        qj = q_ref[:, j * GROUP:(j + 1) * GROUP, :]
        s_c = jnp.einsum("bgd,bsd->bgs", qj.astype(BF16), ckb, preferred_element_type=F32) * ATTN_SCALE
        s_c = jnp.where(key_pos >= 1, s_c, NEG_INF)
        s_n = jnp.sum(qj * kn, axis=-1, keepdims=True) * ATTN_SCALE
        sink = sinks_ref[j][None]
        m = jnp.maximum(jnp.maximum(jnp.max(s_c, axis=-1, keepdims=True), s_n), sink)
        p_c = jnp.exp(s_c - m)
        p_n = jnp.exp(s_n - m)
        inv = 1.0 / (jnp.sum(p_c, axis=-1, keepdims=True) + p_n + jnp.exp(sink - m))
        o = jnp.einsum("bgs,bsd->bgd", (p_c * inv).astype(BF16), cvb, preferred_element_type=F32)
        att_ref[:, j * GROUP:(j + 1) * GROUP, :] = o + (p_n * inv) * vn
    last = lax.broadcasted_iota(jnp.int32, ck.shape, 1) == WINDOW - 1
    kwin_ref[...] = jnp.where(last, kn, pltpu.roll(ck, WINDOW - 1, axis=1))
    vwin_ref[...] = jnp.where(last, vn, pltpu.roll(cv, WINDOW - 1, axis=1))


def _sample_attn_call(q_wide, kn_row, vn_row, ck, cv, sinks3):
    n = q_wide.shape[0]
    tb = SAMPLE_ATTN_TILE
    b3 = lambda i: (i, 0, 0)
    return pl.pallas_call(
        _sample_attn_kernel,
        grid=(n // tb,),
        in_specs=[pl.BlockSpec((tb, N_HEADS, KV_WIDTH), b3),
                  pl.BlockSpec((tb, 1, KV_WIDTH), b3),
                  pl.BlockSpec((tb, 1, KV_WIDTH), b3),
                  pl.BlockSpec((tb, WINDOW, KV_WIDTH), b3),
                  pl.BlockSpec((tb, WINDOW, KV_WIDTH), b3),
                  pl.BlockSpec((N_KV, GROUP, 1), lambda i: (0, 0, 0))],
        out_specs=(pl.BlockSpec((tb, N_HEADS, KV_WIDTH), b3),
                   pl.BlockSpec((tb, WINDOW, KV_WIDTH), b3),
                   pl.BlockSpec((tb, WINDOW, KV_WIDTH), b3)),
        out_shape=(jax.ShapeDtypeStruct((n, N_HEADS, KV_WIDTH), F32),
                   jax.ShapeDtypeStruct((n, WINDOW, KV_WIDTH), F32),
                   jax.ShapeDtypeStruct((n, WINDOW, KV_WIDTH), F32)),
        compiler_params=pltpu.CompilerParams(dimension_semantics=("arbitrary",),
                                             vmem_limit_bytes=VMEM_LIMIT_BYTES),
        name="sample_attn",
    )(q_wide, kn_row, vn_row, ck, cv, sinks3)


def _sample_mix_kernel(x_ref, u_ref, att_ref, st_ref, h0_ref, w_conv_ref, b_conv_ref, w_gates_ref, b_rg_ref,
                       b_ig_ref, lam_ref, w_out_ref, ln_g_ref, ln_b_ref,
                       x1_ref, conv_ref, h_ref):
    xl = u_ref[:, OFF_XL:OFF_XL + LRU_WIDTH]
    wc = w_conv_ref[...]
    xc = wc[0:1] * st_ref[0]
    xc = xc + wc[1:2] * st_ref[1]
    xc = xc + wc[2:3] * st_ref[2]
    xc = xc + wc[3:4] * xl + b_conv_ref[...]
    conv_ref[0] = st_ref[1]
    conv_ref[1] = st_ref[2]
    conv_ref[2] = xl
    a, u = _lru_gates(xc, w_gates_ref, b_rg_ref[...], b_ig_ref[...], lam_ref[...])
    h = a * h0_ref[...] + u
    h_ref[...] = h
    rec = h * jax.nn.gelu(u_ref[:, OFF_YL:OFF_YL + LRU_WIDTH])
    x1_ref[...] = _merge_norm(x_ref[...], rec, att_ref[...], u_ref[:, OFF_GL:OFF_GL + D_MODEL],
                              u_ref[:, OFF_GA:OFF_GA + D_MODEL], w_out_ref,
                              ln_g_ref[...], ln_b_ref[...])


def _sample_mix_call(x_s, u_s, att, st_t, h0, wts):
    n = x_s.shape[0]
    weight_args = (wts["w_conv"], wts["b_conv"], wts["w_gates"], wts["b_rg"], wts["b_ig"], wts["lam"],
                   wts["w_out"], wts["ln1_g"], wts["ln1_b"])
    args = (x_s, u_s, att, st_t, h0) + weight_args
    out_shapes = ((n, D_MODEL), (CONV_W - 1, n, LRU_WIDTH), (n, LRU_WIDTH))
    return pl.pallas_call(
        _sample_mix_kernel,
        grid=(1,),
        in_specs=[_const_spec(a.shape) for a in args],
        out_specs=tuple(_const_spec(s) for s in out_shapes),
        out_shape=tuple(jax.ShapeDtypeStruct(s, F32) for s in out_shapes),
        compiler_params=pltpu.CompilerParams(dimension_semantics=("arbitrary",),
                                             vmem_limit_bytes=VMEM_LIMIT_BYTES),
        name="sample_mix",
    )(*args)


def _one_hot(mask):
    return jnp.where(mask, 1.0, 0.0).astype(BF16)


def _pack_rows(x):
    half = x.shape[1] // 2
    lo = lax.shift_right_logical(lax.bitcast_convert_type(x[:, :half], jnp.uint32), jnp.uint32(16))
    hi = lax.bitcast_convert_type(x[:, half:], jnp.uint32) & jnp.uint32(0xFFFF0000)
    return lo | hi


def _unpack_rows(words):
    lo = lax.bitcast_convert_type(lax.shift_left(words, jnp.uint32(16)), F32)
    hi = lax.bitcast_convert_type(words & jnp.uint32(0xFFFF0000), F32)
    return jnp.concatenate([lo.astype(BF16), hi.astype(BF16)], axis=1)


class _Dispatcher:
    def __init__(self, xs_hbm, goff_ref, gtot_ref, stage, zero_rows, tri, xb_keep, key_keep, g_vmem, gvec,
                 g_smem, nd_smem, rounds_smem, sem_stage, sem_g, sem_zero, seg_cap):
        self.xs_hbm, self.goff_ref, self.gtot_ref = xs_hbm, goff_ref, gtot_ref
        self.stage, self.zero_rows, self.tri = stage, zero_rows, tri
        self.xb_keep, self.key_keep = xb_keep, key_keep
        self.g_vmem, self.gvec, self.g_smem = g_vmem, gvec, g_smem
        self.nd_smem, self.rounds_smem = nd_smem, rounds_smem
        self.sem_stage, self.sem_g, self.sem_zero = sem_stage, sem_g, sem_zero
        self.seg_cap = seg_cap

    def _g_copy(self):
        return pltpu.make_async_copy(self.g_vmem.at[pl.ds(0, 1)], self.g_smem, self.sem_g)

    def init(self):
        n_assign = self.tri.shape[0]
        r = lax.broadcasted_iota(jnp.int32, (n_assign, n_assign), 0)
        c = lax.broadcasted_iota(jnp.int32, (n_assign, n_assign), 1)
        self.tri[...] = _one_hot(r < c)
        self.gvec[...] = jnp.zeros_like(self.gvec)
        self.g_vmem[...] = jnp.zeros_like(self.g_vmem)
        self.nd_smem[0] = 0
        self._g_copy().start()

    def _sorted_rows(self, e_row, rank_row, xb, lo):
        td = xb.shape[0]
        slot = lax.broadcasted_iota(jnp.int32, (N_SLOTS, td), 0).astype(F32)
        lo_f = lo.astype(F32)
        in_round = (rank_row >= lo_f) & (rank_row < lo_f + SLOT_CAP) & (e_row >= 0.0)
        key = jnp.where(in_round, e_row * SLOT_CAP + (rank_row - lo_f), -1.0)
        perm = jnp.where(slot == key[:, :td], 1.0, jnp.where(slot == key[:, td:], 1.0, 0.0)).astype(BF16)
        return _pack_rows(jnp.dot(perm, xb, preferred_element_type=F32))

    def plan(self, x1, route, n_valid, buf):
        td = x1.shape[0]
        n_assign = TOP_K * td
        valid_col = lax.broadcasted_iota(jnp.int32, (td, 1), 0) < n_valid
        lane = lax.broadcasted_iota(jnp.int32, (td, ROUTE_WIDTH), 1).astype(F32)
        e0c = jnp.where(valid_col, route[:, 0:1], -1.0)
        e1c = jnp.where(valid_col, route[:, 1:2], -1.0)
        cnt_row = jnp.sum(jnp.where(lane == e0c, 1.0, 0.0) + jnp.where(lane == e1c, 1.0, 0.0),
                          axis=0, keepdims=True)
        self.rounds_smem[0] = (jnp.max(cnt_row).astype(jnp.int32) + SLOT_CAP - 1) // SLOT_CAP
        self.goff_ref[0] = self.gvec[0:1, :].astype(jnp.int32)
        self.gvec[0:1, :] = self.gvec[0:1, :] + jnp.ceil(cnt_row * (1.0 / SUBLANES)) * SUBLANES
        self.gtot_ref[...] = self.gvec[0:1, :].astype(jnp.int32)

        route_t = route.T
        valid_row = lax.broadcasted_iota(jnp.int32, (1, td), 1) < n_valid
        e_row = jnp.concatenate([jnp.where(valid_row, route_t[0:1, :], -1.0),
                                 jnp.where(valid_row, route_t[1:2, :], -1.0)], axis=1)
        expert_sub = lax.broadcasted_iota(jnp.int32, (N_EXPERTS, n_assign), 0).astype(F32)
        onehot_t = expert_sub == e_row
        rank_t = jnp.dot(_one_hot(onehot_t), self.tri[...], preferred_element_type=F32)
        rank_row = jnp.sum(jnp.where(onehot_t, rank_t, 0.0), axis=0, keepdims=True)
        xb = jnp.where(valid_col, x1, 0.0).astype(BF16)
        self.xb_keep[...] = xb
        self.key_keep[0:1, :] = e_row
        self.key_keep[1:2, :] = rank_row
        self.stage[buf] = self._sorted_rows(e_row, rank_row, xb, jnp.int32(0))

    def _stage_copy(self, buf, e, dst_row):
        src = self.stage.at[buf, pl.ds(e * SLOT_CAP, SLOT_CAP)]
        return pltpu.make_async_copy(src, self.xs_hbm.at[pl.ds(dst_row, SLOT_CAP)], self.sem_stage)

    def _wait_outstanding(self):
        @pl.when(self.nd_smem[0] > 0)
        def _():
            for _ in range(N_EXPERTS):
                self._stage_copy(0, 0, 0).wait()
        self.nd_smem[0] = 0

    def _start_round(self, buf, lo):
        for e in range(N_EXPERTS):
            dst_row = pl.multiple_of(e * self.seg_cap + self.g_smem[0, e] + lo, SUBLANES)
            self._stage_copy(buf, e, dst_row).start()
        self.nd_smem[0] = N_EXPERTS

    def flush(self, buf, is_last):
        self._g_copy().wait()
        self._wait_outstanding()
        self._start_round(buf, 0)

        def later_round(r, carry):
            lo = r * SLOT_CAP
            rows = self._sorted_rows(self.key_keep[0:1, :], self.key_keep[1:2, :], self.xb_keep[...], lo)
            self._wait_outstanding()
            self.stage[buf] = rows
            self._start_round(buf, lo)
            return carry
        lax.fori_loop(1, self.rounds_smem[0], later_round, 0)

        self.g_vmem[0:1, :] = self.gvec[0:1, :].astype(jnp.int32)
        self._g_copy().start()

        @pl.when(is_last)
        def _():
            self._g_copy().wait()
            self._wait_outstanding()
            self.zero_rows[...] = jnp.zeros_like(self.zero_rows)

            def pad_copy(dst_row):
                dst = self.xs_hbm.at[pl.ds(pl.multiple_of(dst_row, SUBLANES), PAD_CHUNK)]
                return pltpu.make_async_copy(self.zero_rows, dst, self.sem_zero)

            def pad_segment(e, n_started):
                fill = self.g_smem[0, e]
                block_end = (fill + SLOT_CAP + MOE_TILE - 1) // MOE_TILE * MOE_TILE
                n_chunks = (block_end - fill + PAD_CHUNK - 1) // PAD_CHUNK

                def start(c, carry):
                    pad_copy(e * self.seg_cap + fill + c * PAD_CHUNK).start()
                    return carry
                lax.fori_loop(0, n_chunks, start, 0)
                return n_started + n_chunks
            n_started = lax.fori_loop(0, N_EXPERTS, pad_segment, 0)

            def wait(_, carry):
                pad_copy(0).wait()
                return carry
            lax.fori_loop(0, n_started, wait, 0)


def _dispatch_tiles(n_tokens):
    return (n_tokens + DISPATCH_TILE - 1) // DISPATCH_TILE


def _segment_capacity(n_tokens):
    assert SEG_PAD >= DISPATCH_TILE
    alignment_slack = (SUBLANES - 1) * _dispatch_tiles(n_tokens)
    return (n_tokens + alignment_slack + SEG_PAD + MOE_TILE - 1) // MOE_TILE * MOE_TILE


def _moe_kernel(blk_e_ref, blk_j_ref, n_used_ref, xs_ref, wg_ref, wu_ref, wd_ref, ys_ref, wg_bf, wu_bf, wd_bf):
    del blk_j_ref
    b = pl.program_id(0)

    @pl.when(b < n_used_ref[0])
    def _():
        prev_e = blk_e_ref[jnp.maximum(b - 1, 0)]

        @pl.when((b == 0) | (blk_e_ref[b] != prev_e))
        def _():
            wg_bf[...] = wg_ref[0].astype(BF16)
            wu_bf[...] = wu_ref[0].astype(BF16)
            wd_bf[...] = wd_ref[0].astype(BF16)

        xb = _unpack_rows(xs_ref[...])
        g = jnp.dot(xb, wg_bf[...], preferred_element_type=F32)
        u = jnp.dot(xb, wu_bf[...], preferred_element_type=F32)
        hmid = (g * _sigmoid(g)) * u
        y = jnp.dot(hmid.astype(BF16), wd_bf[...], preferred_element_type=F32)
        ys_ref[...] = _pack_rows(y.astype(BF16).astype(F32))


def _expert_blocks(gtot, n_blocks):
    rows = gtot[0, :N_EXPERTS]
    nb = (rows + SLOT_CAP + MOE_TILE - 1) // MOE_TILE
    idx = jnp.arange(N_EXPERTS, dtype=jnp.int32)
    ends = jnp.sum(jnp.where(idx[None, :] <= idx[:, None], nb[None, :], 0), axis=1)
    n_used = ends[-1]
    step = jnp.minimum(jnp.arange(n_blocks, dtype=jnp.int32), n_used - 1)
    blk_e = jnp.sum((step[:, None] >= ends[None, :]).astype(jnp.int32), axis=1)
    blk_j = step - (ends - nb)[blk_e]
    return blk_e, blk_j.astype(jnp.int32), n_used.reshape(1).astype(jnp.int32)


def _moe_call(gtot, xs, w_gate, w_up, w_down, n_tokens):
    seg_cap = _segment_capacity(n_tokens)
    seg_blocks = seg_cap // MOE_TILE
    max_rows = n_tokens * TOP_K + N_EXPERTS * (SUBLANES - 1) * _dispatch_tiles(n_tokens)
    n_blocks = (max_rows + N_EXPERTS * (SLOT_CAP + MOE_TILE - 1)) // MOE_TILE
    blk_e, blk_j, n_used = _expert_blocks(gtot, n_blocks)
    row_block = lambda b, be, bj, nu: (be[b] * seg_blocks + bj[b], 0)
    weight_block = lambda b, be, bj, nu: (be[b], 0, 0)
    grid_spec = pltpu.PrefetchScalarGridSpec(
        num_scalar_prefetch=3,
        grid=(n_blocks,),
        in_specs=[
            pl.BlockSpec((MOE_TILE, PACKED_WIDTH), row_block),
            pl.BlockSpec((1, D_MODEL, EXPERT_FF), weight_block),
            pl.BlockSpec((1, D_MODEL, EXPERT_FF), weight_block),
            pl.BlockSpec((1, EXPERT_FF, D_MODEL), weight_block),
        ],
        out_specs=pl.BlockSpec((MOE_TILE, PACKED_WIDTH), row_block),
        scratch_shapes=[
            pltpu.VMEM((D_MODEL, EXPERT_FF), BF16),
            pltpu.VMEM((D_MODEL, EXPERT_FF), BF16),
            pltpu.VMEM((EXPERT_FF, D_MODEL), BF16),
        ],
    )
    return pl.pallas_call(
        _moe_kernel,
        grid_spec=grid_spec,
        out_shape=jax.ShapeDtypeStruct(xs.shape, xs.dtype),
        compiler_params=pltpu.CompilerParams(dimension_semantics=("arbitrary",),
                                             vmem_limit_bytes=VMEM_LIMIT_BYTES),
        name="moe_experts",
    )(blk_e, blk_j, n_used, xs, w_gate, w_up, w_down)


def _combine_kernel(gcur_ref, gnext_ref, x1_ref, route_ref, ys_hbm, ln_g_ref, ln_b_ref, y_ref,
                    stage, acc, tri, sem, *, seg_cap):
    td = x1_ref.shape[0]
    n_assign = TOP_K * td
    i = pl.program_id(0)
    buf = lax.rem(i, 2)

    def stage_copy(g_ref, e, first_rank, to_buf):
        src_row = pl.multiple_of(e * seg_cap + g_ref[0, 0, e] + first_rank, SUBLANES)
        slot0 = e * SLOT_CAP
        if not isinstance(slot0, int):
            slot0 = pl.multiple_of(slot0, SLOT_CAP)
        dst = stage.at[to_buf, pl.ds(slot0, SLOT_CAP)]
        return pltpu.make_async_copy(ys_hbm.at[pl.ds(src_row, SLOT_CAP)], dst, sem.at[to_buf])

    def fetch_first_round(g_ref, to_buf):
        for e in range(N_EXPERTS):
            stage_copy(g_ref, e, 0, to_buf).start()

    @pl.when(i == 0)
    def _():
        r = lax.broadcasted_iota(jnp.int32, (n_assign, n_assign), 0)
        c = lax.broadcasted_iota(jnp.int32, (n_assign, n_assign), 1)
        tri[...] = _one_hot(c < r)
        fetch_first_round(gcur_ref, 0)

    def count(e):
        return gnext_ref[0, 0, e] - gcur_ref[0, 0, e]

    route = route_ref[...]
    e0, e1, w0, w1 = route[:, 0:1], route[:, 1:2], route[:, 2:3], route[:, 3:4]
    lane = lax.broadcasted_iota(jnp.int32, (td, ROUTE_WIDTH), 1).astype(F32)
    onehot = jnp.concatenate([lane == e0, lane == e1], axis=0)
    onehot_f = jnp.where(onehot, 1.0, 0.0)
    rank_mat = jnp.dot(tri[...], onehot_f.astype(BF16), preferred_element_type=F32)
    rank = jnp.sum(jnp.where(onehot, rank_mat, 0.0), axis=1, keepdims=True)
    r0, r1 = rank[:td], rank[td:]
    cnt_row = jnp.sum(onehot_f, axis=0, keepdims=True)
    rounds = (jnp.max(cnt_row).astype(jnp.int32) + SLOT_CAP - 1) // SLOT_CAP

    slot_lane = lax.broadcasted_iota(jnp.int32, (td, N_SLOTS), 1).astype(F32)

    def wait_copies(n):
        def wait_one(_, c):
            stage_copy(gcur_ref, 0, 0, buf).wait()
            return c
        lax.fori_loop(0, n, wait_one, 0)

    def weighted_rows(lo):
        lo_f = lo.astype(F32)
        rows = _unpack_rows(stage[buf])

        def selector(e_col, r_col, w_col):
            in_round = (r_col >= lo_f) & (r_col < lo_f + SLOT_CAP)
            key = jnp.where(in_round, e_col * SLOT_CAP + (r_col - lo_f), -1.0)
            return jnp.where(slot_lane == key, w_col, 0.0)
        mix = (selector(e0, r0, w0) + selector(e1, r1, w1)).astype(BF16)
        return jnp.dot(mix, rows, preferred_element_type=F32)

    for _ in range(N_EXPERTS):
        stage_copy(gcur_ref, 0, 0, buf).wait()

    fetch_first_round(gnext_ref, 1 - buf)

    acc[...] = weighted_rows(jnp.int32(0))

    def later_round(r, carry):
        lo = r * SLOT_CAP

        def start(e, n_started):
            has_rows = count(e) > lo

            @pl.when(has_rows)
            def _():
                stage_copy(gcur_ref, e, lo, buf).start()
            return n_started + has_rows.astype(jnp.int32)
        wait_copies(lax.fori_loop(0, N_EXPERTS, start, 0))
        acc[...] = acc[...] + weighted_rows(lo)
        return carry
    lax.fori_loop(1, rounds, later_round, 0)

    y_ref[...] = _layer_norm(ALPHA * x1_ref[...] + acc[...], ln_g_ref[...], ln_b_ref[...])

    @pl.when(i + 1 == pl.num_programs(0))
    def _():
        for _ in range(N_EXPERTS):
            stage_copy(gnext_ref, 0, 0, 1 - buf).wait()


def _combine_call(goff, x1_all, route_all, ys, ln_g, ln_b, first_row, n_rows, tile, n_tokens):
    assert first_row % DISPATCH_TILE == 0 and first_row % tile == 0
    assert tile == DISPATCH_TILE or n_rows == tile
    first_block = first_row // tile
    first_goff = first_row // DISPATCH_TILE
    n_assign = TOP_K * tile
    grid_spec = pl.GridSpec(
        grid=(n_rows // tile,),
        in_specs=[
            pl.BlockSpec((1, 1, ROUTE_WIDTH), lambda i: (first_goff + i, 0, 0), memory_space=pltpu.SMEM),
            pl.BlockSpec((1, 1, ROUTE_WIDTH), lambda i: (first_goff + i + 1, 0, 0), memory_space=pltpu.SMEM),
            pl.BlockSpec((tile, D_MODEL), lambda i: (first_block + i, 0)),
            pl.BlockSpec((tile, ROUTE_WIDTH), lambda i: (first_block + i, 0)),
            pl.BlockSpec(memory_space=pl.ANY),
            _const_spec(ln_g.shape),
            _const_spec(ln_b.shape),
        ],
        out_specs=pl.BlockSpec((tile, D_MODEL), lambda i: (i, 0)),
        scratch_shapes=[pltpu.VMEM((2, N_SLOTS, PACKED_WIDTH), jnp.uint32),
                        pltpu.VMEM((tile, D_MODEL), F32),
                        pltpu.VMEM((n_assign, n_assign), BF16),
                        pltpu.SemaphoreType.DMA((2,))],
    )
    return pl.pallas_call(
        functools.partial(_combine_kernel, seg_cap=_segment_capacity(n_tokens)),
        grid_spec=grid_spec,
        out_shape=jax.ShapeDtypeStruct((n_rows, D_MODEL), F32),
        compiler_params=pltpu.CompilerParams(dimension_semantics=("arbitrary",),
                                             vmem_limit_bytes=VMEM_LIMIT_BYTES),
        name="moe_combine",
    )(goff, goff, x1_all, route_all, ys, ln_g, ln_b)


def _prepare_weights(w_in, b_in, w_conv, b_conv, w_rg, b_rg, w_ig, b_ig, lru_lambda, w_lru_out, w_attn_out, w_o,
                     ln1_g, ln1_b, w_group, b_group, w_router, b_router):
    blocks_per_chunk = GATE_CHUNK // LRU_BLOCK

    def chunked_block_diag(w):
        w = w.reshape(N_GATE_CHUNKS, blocks_per_chunk, LRU_BLOCK, LRU_BLOCK)
        eye = jnp.eye(blocks_per_chunk, dtype=w.dtype)
        return jnp.einsum("cbij,bd->cbidj", w, eye).reshape(N_GATE_CHUNKS, GATE_CHUNK, GATE_CHUNK)

    w_gates = jnp.concatenate([chunked_block_diag(w_rg), chunked_block_diag(w_ig)], axis=-1).astype(BF16)
    w_rt = jnp.concatenate([w_group, w_router], axis=1)
    w_rt = jnp.pad(w_rt, ((0, 0), (0, ROUTE_WIDTH - w_rt.shape[1])))
    w_rt_hi = w_rt.astype(BF16)
    w_rt_lo = jnp.concatenate([w_rt_hi, (w_rt - w_rt_hi.astype(F32)).astype(BF16)], axis=1)
    b_rt = jnp.pad(jnp.concatenate([b_group, b_router]), (0, ROUTE_WIDTH - N_GROUPS - N_EXPERTS))
    row = lambda v: v.reshape(1, -1)
    return dict(
        w_in=w_in.astype(BF16), b_in=row(b_in), w_conv=w_conv, b_conv=row(b_conv), w_gates=w_gates,
        b_rg=row(b_rg), b_ig=row(b_ig), lam=row(lru_lambda),
        w_out=jnp.stack([w_lru_out, w_attn_out, w_o]).astype(BF16),
        ln1_g=row(ln1_g), ln1_b=row(ln1_b), w_rt_hi=w_rt_hi, w_rt_lo=w_rt_lo, b_rt=row(b_rt))


def kernel(x_prompt, x_sample, cache_k, cache_v, state_conv, state_lru_h, w_in, b_in, w_conv, b_conv, w_rg, b_rg,
           w_ig, b_ig, lru_lambda, sinks, w_lru_out, w_attn_out, w_o, ln1_g, ln1_b, w_group, b_group, w_router,
           b_router, w_gate, w_up, w_down, ln2_g, ln2_b):
    B, S, _ = x_prompt.shape
    n_prompt = B * S
    n_sample = x_sample.shape[0]
    n_all = n_prompt + n_sample
    wts = _prepare_weights(w_in, b_in, w_conv, b_conv, w_rg, b_rg, w_ig, b_ig, lru_lambda, w_lru_out, w_attn_out,
                           w_o, ln1_g, ln1_b, w_group, b_group, w_router, b_router)

    x_s = x_sample.reshape(n_sample, D_MODEL)
    u_s = _sample_proj_call(x_s, wts["w_in"], wts["b_in"])
    q4 = u_s[:, OFF_Q:OFF_K].reshape(n_sample, N_KV, GROUP, HEAD_DIM)
    q_wide = jnp.einsum("njgd,jk->njgkd", q4, jnp.eye(N_KV, dtype=F32)).reshape(n_sample, N_HEADS, KV_WIDTH)
    k_new = u_s[:, OFF_K:OFF_V]
    v_new = u_s[:, OFF_V:OFF_GL]
    att_wide, k_win_s, v_win_s = _sample_attn_call(
        q_wide, k_new.reshape(n_sample, 1, KV_WIDTH), v_new.reshape(n_sample, 1, KV_WIDTH),
        cache_k.reshape(n_sample, WINDOW, KV_WIDTH), cache_v.reshape(n_sample, WINDOW, KV_WIDTH),
        sinks.reshape(N_KV, GROUP, 1))
    att5 = att_wide.reshape(n_sample, N_KV, GROUP, N_KV, HEAD_DIM)
    att_s = jnp.stack([att5[:, j, :, j, :] for j in range(N_KV)], axis=1).reshape(n_sample, N_HEADS * HEAD_DIM)
    x1_s, conv_s_t, h_s = _sample_mix_call(
        x_s, u_s, att_s, jnp.transpose(state_conv, (1, 0, 2)), state_lru_h, wts)

    x1_all, route_all, k_win_p, v_win_p, conv_p, h_p, xs, goff, gtot = _mixer_call(x_prompt, x1_s, sinks, wts)

    ys = _moe_call(gtot, xs, w_gate, w_up, w_down, n_all)
    goff = jnp.concatenate([goff, gtot[None]], axis=0)
    ln2_g2, ln2_b2 = ln2_g.reshape(1, -1), ln2_b.reshape(1, -1)
    y_p = _combine_call(goff, x1_all, route_all, ys, ln2_g2, ln2_b2, 0, n_prompt, DISPATCH_TILE, n_all)
    y_s = _combine_call(goff, x1_all, route_all, ys, ln2_g2, ln2_b2, n_prompt, n_sample, n_sample, n_all)

    kv_shape = (WINDOW, N_KV, HEAD_DIM)
    return (y_p.reshape(B, S, D_MODEL), y_s.reshape(n_sample, 1, D_MODEL),
            k_win_p.reshape((B,) + kv_shape), v_win_p.reshape((B,) + kv_shape), conv_p, h_p.reshape(B, LRU_WIDTH),
            k_win_s.reshape((n_sample,) + kv_shape), v_win_s.reshape((n_sample,) + kv_shape),
            jnp.transpose(conv_s_t, (1, 0, 2)), h_s)
```

```python
import functools

import jax
import jax.numpy as jnp
from jax import lax
from jax.experimental import pallas as pl
from jax.experimental.pallas import tpu as pltpu

F32 = jnp.float32
BF16 = jnp.bfloat16

D_MODEL = 1024
LRU_WIDTH = 1024
LRU_BLOCK = 64
CONV_W = 4
LRU_C = 8.0
N_HEADS = 16
N_KV = 4
GROUP = N_HEADS // N_KV
HEAD_DIM = 64
KV_WIDTH = N_KV * HEAD_DIM
WINDOW = 128
NEG_INF = -1e30
N_GROUPS = 4
EXPERTS_PER_GROUP = 8
N_EXPERTS = N_GROUPS * EXPERTS_PER_GROUP
TOP_K = 2
EXPERT_FF = D_MODEL // 2
DEPTH = 1
ALPHA = (2 * DEPTH) ** 0.25
LN_EPS = 1e-5
ATTN_SCALE = HEAD_DIM ** -0.5
LOG2_E = 1.4426950408889634

OFF_XL = 0
OFF_YL = OFF_XL + LRU_WIDTH
OFF_Q = OFF_YL + LRU_WIDTH
OFF_K = OFF_Q + N_HEADS * HEAD_DIM
OFF_V = OFF_K + KV_WIDTH
OFF_GL = OFF_V + KV_WIDTH
OFF_GA = OFF_GL + D_MODEL
IN_WIDTH = OFF_GA + D_MODEL

LANES = 128
SUBLANES = 8
MXU_DIM = 256
N_DMA_PRIORITIES = 2
VMEM_LIMIT_BYTES = 56 * 1024 * 1024

GATE_CHUNK = MXU_DIM
N_GATE_CHUNKS = LRU_WIDTH // GATE_CHUNK
ROUTE_WIDTH = LANES

SEQ_TILE = 256
MOE_TILE = 1392
DISPATCH_TILE = 256
SLOT_CAP = 32
N_SLOTS = N_EXPERTS * SLOT_CAP
PAD_CHUNK = 128
SEG_PAD = MOE_TILE + SLOT_CAP + PAD_CHUNK
PACKED_WIDTH = D_MODEL // 2
SAMPLE_ATTN_TILE = 16
SAMPLE_PROJ_TILE = 1408


def _const_spec(shape):
    nd = len(shape)
    return pl.BlockSpec(shape, lambda *_: (0,) * nd)


def _layer_norm(z, g, b):
    mu = jnp.mean(z, axis=-1, keepdims=True)
    zc = z - mu
    var = jnp.mean(zc * zc, axis=-1, keepdims=True)
    return zc * lax.rsqrt(var + LN_EPS) * g + b


def _sigmoid(x):
    return 1.0 / (1.0 + jnp.exp2(x * -LOG2_E))


def _softplus(x):
    return jnp.maximum(x, 0.0) + jnp.log1p(jnp.exp(-jnp.abs(x)))


def _lru_gates(xc, w_gates_ref, b_rg, b_ig, lam):
    xcb = xc.astype(BF16)
    r_parts, i_parts = [], []
    for c in range(N_GATE_CHUNKS):
        g = jnp.dot(xcb[:, c * GATE_CHUNK:(c + 1) * GATE_CHUNK], w_gates_ref[c], preferred_element_type=F32)
        r_parts.append(g[:, :GATE_CHUNK])
        i_parts.append(g[:, GATE_CHUNK:])
    r = _sigmoid(jnp.concatenate(r_parts, axis=1) + b_rg)
    i = _sigmoid(jnp.concatenate(i_parts, axis=1) + b_ig)
    log_a = (-LRU_C * r) * _softplus(-lam)
    a = jnp.exp(log_a)
    gain_sq = 1.0 - a * a
    gain = jnp.where(gain_sq > 0.0, gain_sq * lax.rsqrt(gain_sq), 0.0)
    u = gain * (i * xc)
    return a, u


def _linear_scan(a, u, h_in):
    n, w = a.shape
    groups = n // SUBLANES
    a3 = a.reshape(groups, SUBLANES, w)
    u3 = u.reshape(groups, SUBLANES, w)
    row = lax.broadcasted_iota(jnp.int32, a3.shape, 1)
    d = 1
    while d < SUBLANES:
        has_prev = row >= d
        u3 = u3 + a3 * jnp.where(has_prev, pltpu.roll(u3, d, axis=1), 0.0)
        a3 = a3 * jnp.where(has_prev, pltpu.roll(a3, d, axis=1), 1.0)
        d *= 2
    carry = h_in
    out = []
    for g in range(groups):
        h_g = u3[g] + a3[g] * carry
        out.append(h_g)
        carry = h_g[SUBLANES - 1:SUBLANES, :]
    return jnp.concatenate(out, axis=0)


def _route(x1, w_hi_ref, w_hilo_ref, b_rt):
    x_hi = x1.astype(BF16)
    x_lo = (x1 - x_hi.astype(F32)).astype(BF16)
    both = jnp.dot(x_hi, w_hilo_ref[...], preferred_element_type=F32)
    logits = (both[:, :ROUTE_WIDTH]
              + (jnp.dot(x_lo, w_hi_ref[...], preferred_element_type=F32) + both[:, ROUTE_WIDTH:])) + b_rt
    col = lax.broadcasted_iota(jnp.int32, logits.shape, 1)
    big = jnp.int32(ROUTE_WIDTH)
    is_g = col < N_GROUPS
    gl = jnp.where(is_g, logits, -jnp.inf)
    gmax = jnp.max(gl, axis=-1, keepdims=True)
    g_idx = jnp.min(jnp.where(gl == gmax, col, big), axis=-1, keepdims=True)
    p_g = 1.0 / jnp.sum(jnp.where(is_g, jnp.exp(gl - gmax), 0.0), axis=-1, keepdims=True)
    lo = N_GROUPS + g_idx * EXPERTS_PER_GROUP
    in_grp = (col >= lo) & (col < lo + EXPERTS_PER_GROUP)
    el = jnp.where(in_grp, logits, -jnp.inf)
    v1 = jnp.max(el, axis=-1, keepdims=True)
    i1 = jnp.min(jnp.where(el == v1, col, big), axis=-1, keepdims=True)
    el2 = jnp.where(col == i1, -jnp.inf, el)
    v2 = jnp.max(el2, axis=-1, keepdims=True)
    i2 = jnp.min(jnp.where(el2 == v2, col, big), axis=-1, keepdims=True)
    e21 = jnp.exp(v2 - v1)
    inv = 1.0 / (1.0 + e21)
    w1 = p_g * inv
    w2 = p_g * (e21 * inv)
    e1 = (i1 - N_GROUPS).astype(F32)
    e2 = (i2 - N_GROUPS).astype(F32)
    return jnp.where(col == 0, e1, jnp.where(col == 1, e2, jnp.where(col == 2, w1, jnp.where(col == 3, w2, 0.0))))


def _merge_norm(x, rec, att, g_l, g_a, w_out_ref, ln_g, ln_b):
    rec_o = jnp.dot(rec.astype(BF16), w_out_ref[0], preferred_element_type=F32)
    att_o = jnp.dot(att.astype(BF16), w_out_ref[1], preferred_element_type=F32)
    merged = _sigmoid(g_l) * rec_o + _sigmoid(g_a) * att_o
    mix = jnp.dot(merged.astype(BF16), w_out_ref[2], preferred_element_type=F32)
    return _layer_norm(ALPHA * x + mix, ln_g, ln_b)


def _mixer_kernel(sinks_ref, x_ref, w_in_ref, b_in_ref, w_conv_ref, b_conv_ref, w_gates_ref, b_rg_ref, b_ig_ref,
                  lam_ref, w_out_ref, ln_g_ref, ln_b_ref, w_rt_hi_ref, w_rt_lo_ref,
                  b_rt_ref, x1_s_ref,
                  x1_ref, route_ref, kwin_ref, vwin_ref, conv_ref, h_ref, xs_hbm, goff_ref, gtot_ref,
                  conv_buf, h_carry, kcat, vcat, att_buf, prev_x1, stage, zero_rows, tri, xb_keep, key_keep,
                  g_vmem, gvec, g_smem, nd_smem, rounds_smem, sem_stage, sem_g, sem_zero,
                  *, tiles_per_seq, n_tiles, seg_cap):
    step = pl.program_id(0)
    last_step = pl.num_programs(0) - 1
    buf = lax.rem(step, 2)
    n_sample = x1_s_ref.shape[0]
    disp = _Dispatcher(xs_hbm, goff_ref, gtot_ref, stage, zero_rows, tri, xb_keep, key_keep, g_vmem, gvec,
                       g_smem, nd_smem, rounds_smem, sem_stage, sem_g, sem_zero, seg_cap)
    n_valid = jnp.where(step == 0, 0, jnp.where(step == last_step, n_sample, SEQ_TILE))

    @pl.when(step == 0)
    def _():
        disp.init()
        prev_x1[...] = jnp.zeros_like(prev_x1)

    def route_and_plan():
        x1_prev = prev_x1[...]
        route = _route(x1_prev, w_rt_hi_ref, w_rt_lo_ref, b_rt_ref[...])
        route_ref[...] = route
        disp.plan(x1_prev, route, n_valid, buf)

    @pl.when(step < n_tiles)
    def _():
        @pl.when(lax.rem(step, tiles_per_seq) == 0)
        def _():
            conv_buf[...] = jnp.zeros_like(conv_buf)
            h_carry[...] = jnp.zeros_like(h_carry)
            kcat[0:WINDOW, :] = jnp.zeros((WINDOW, KV_WIDTH), BF16)
            vcat[0:WINDOW, :] = jnp.zeros((WINDOW, KV_WIDTH), BF16)

        route_and_plan()
        _mixer_tile(lax.rem(step, tiles_per_seq), sinks_ref, x_ref, w_in_ref, b_in_ref, w_conv_ref, b_conv_ref,
                    w_gates_ref, b_rg_ref, b_ig_ref, lam_ref, w_out_ref, ln_g_ref,
                    ln_b_ref, x1_ref, prev_x1, kwin_ref, vwin_ref, conv_ref, h_ref,
                    conv_buf, h_carry, kcat, vcat, att_buf)

    @pl.when(step >= n_tiles)
    def _():
        route_and_plan()

        @pl.when(step == n_tiles)
        def _():
            x1_ref[0:n_sample, :] = x1_s_ref[...]
            prev_x1[0:n_sample, :] = x1_s_ref[...]

    disp.flush(buf, step == last_step)


def _mixer_tile(t, sinks_ref, x_ref, w_in_ref, b_in_ref, w_conv_ref, b_conv_ref, w_gates_ref, b_rg_ref, b_ig_ref,
                lam_ref, w_out_ref, ln_g_ref, ln_b_ref,
                x1_ref, x1_keep, kwin_ref, vwin_ref, conv_ref, h_ref,
                conv_buf, h_carry, kcat, vcat, att_buf):
    T = SEQ_TILE
    x = x_ref[0]
    xb = x.astype(BF16)

    def proj(lo, width):
        return jnp.dot(xb, w_in_ref[:, lo:lo + width], preferred_element_type=F32) + b_in_ref[:, lo:lo + width]

    xl = proj(OFF_XL, LRU_WIDTH)
    xl_ext = jnp.concatenate([conv_buf[...], xl], axis=0)

    def lagged(k):
        return pltpu.roll(xl_ext, k, axis=0)[SUBLANES:, :]
    wc = w_conv_ref[...]
    xc = wc[0:1] * lagged(3)
    xc = xc + wc[1:2] * lagged(2)
    xc = xc + wc[2:3] * lagged(1)
    xc = xc + wc[3:4] * xl + b_conv_ref[...]
    conv_ref[0] = xl[T - (CONV_W - 1):, :]
    conv_buf[...] = xl[T - SUBLANES:, :]

    a, u = _lru_gates(xc, w_gates_ref, b_rg_ref[...], b_ig_ref[...], lam_ref[...])
    h = _linear_scan(a, u, h_carry[0:1, :])
    h_last = h[T - 1:T, :]
    h_carry[0:1, :] = h_last
    h_ref[0] = h_last
    rec = h * jax.nn.gelu(proj(OFF_YL, LRU_WIDTH))

    q = proj(OFF_Q, N_HEADS * HEAD_DIM) * (ATTN_SCALE * LOG2_E)
    k = proj(OFF_K, KV_WIDTH)
    v = proj(OFF_V, KV_WIDTH)
    kwin_ref[0] = k[T - WINDOW:, :]
    vwin_ref[0] = v[T - WINDOW:, :]
    kcat[WINDOW:WINDOW + T, :] = k.astype(BF16)
    vcat[WINDOW:WINDOW + T, :] = v.astype(BF16)

    qi = lax.broadcasted_iota(jnp.int32, (WINDOW, 2 * WINDOW), 0)
    kj = lax.broadcasted_iota(jnp.int32, (WINDOW, 2 * WINDOW), 1)
    band = (kj > qi) & (kj <= qi + WINDOW)
    grp_row = lax.broadcasted_iota(jnp.int32, (GROUP * WINDOW, 1), 0) // WINDOW
    for qb in range(T // WINDOW):
        if qb == 0:
            first_key = jnp.where(t == 0, WINDOW, 0)
            mask1 = band & (kj >= first_key)
        else:
            mask1 = band
        bias = jnp.concatenate([jnp.where(mask1, 0.0, NEG_INF)] * GROUP, axis=0)
        r0 = qb * WINDOW
        qq = q[r0:r0 + WINDOW, :]
        for j in range(N_KV):
            kjb = kcat[r0:r0 + 2 * WINDOW, j * HEAD_DIM:(j + 1) * HEAD_DIM]
            vjb = vcat[r0:r0 + 2 * WINDOW, j * HEAD_DIM:(j + 1) * HEAD_DIM]
            qs = jnp.concatenate(
                [qq[:, (j * GROUP + g) * HEAD_DIM:(j * GROUP + g + 1) * HEAD_DIM] for g in range(GROUP)], axis=0)
            s = lax.dot_general(qs.astype(BF16), kjb, (((1,), (1,)), ((), ())), preferred_element_type=F32) + bias
            sink = jnp.zeros((GROUP * WINDOW, 1), F32)
            for g in range(GROUP):
                sink = jnp.where(grp_row == g, sinks_ref[j * GROUP + g] * LOG2_E, sink)
            m = jnp.maximum(jnp.max(s, axis=-1, keepdims=True), sink)
            p = jnp.exp2(s - m)
            inv = 1.0 / (jnp.sum(p, axis=-1, keepdims=True) + jnp.exp2(sink - m))
            o = jnp.dot((p * inv).astype(BF16), vjb, preferred_element_type=F32)
            for g in range(GROUP):
                hcol = (j * GROUP + g) * HEAD_DIM
                att_buf[r0:r0 + WINDOW, hcol:hcol + HEAD_DIM] = o[g * WINDOW:(g + 1) * WINDOW, :]
    kcat[0:WINDOW, :] = kcat[T:T + WINDOW, :]
    vcat[0:WINDOW, :] = vcat[T:T + WINDOW, :]

    x1 = _merge_norm(x, rec, att_buf[...], proj(OFF_GL, D_MODEL), proj(OFF_GA, D_MODEL),
                     w_out_ref, ln_g_ref[...], ln_b_ref[...])
    x1_ref[...] = x1
    x1_keep[...] = x1


def _mixer_call(x_prompt, x1_s, sinks, wts):
    B, S, _ = x_prompt.shape
    T = SEQ_TILE
    assert T == DISPATCH_TILE and x1_s.shape[0] <= T
    nt = S // T
    n_tiles = B * nt
    n_rows_total = B * S + x1_s.shape[0]
    n_dispatch = _dispatch_tiles(n_rows_total)
    assert n_dispatch == n_tiles + 1
    seg_cap = _segment_capacity(n_rows_total)
    n_assign = TOP_K * T
    weight_args = (wts["w_in"], wts["b_in"], wts["w_conv"], wts["b_conv"], wts["w_gates"], wts["b_rg"], wts["b_ig"],
                   wts["lam"], wts["w_out"], wts["ln1_g"], wts["ln1_b"],
                   wts["w_rt_hi"], wts["w_rt_lo"], wts["b_rt"], x1_s)
    mixed = lambda i: jnp.minimum(i, n_tiles - 1)
    seq = lambda i: mixed(i) // nt
    routed = lambda i: jnp.clip(i - 1, 0, n_dispatch - 1)
    in_specs = [pl.BlockSpec(memory_space=pltpu.SMEM),
                pl.BlockSpec((1, T, D_MODEL), lambda i: (seq(i), lax.rem(mixed(i), nt), 0))]
    in_specs += [_const_spec(w.shape) for w in weight_args]
    out_shape = (
        jax.ShapeDtypeStruct((n_rows_total, D_MODEL), F32),
        jax.ShapeDtypeStruct((n_rows_total, ROUTE_WIDTH), F32),
        jax.ShapeDtypeStruct((B, WINDOW, KV_WIDTH), F32),
        jax.ShapeDtypeStruct((B, WINDOW, KV_WIDTH), F32),
        jax.ShapeDtypeStruct((B, CONV_W - 1, LRU_WIDTH), F32),
        jax.ShapeDtypeStruct((B, 1, LRU_WIDTH), F32),
        jax.ShapeDtypeStruct((N_EXPERTS * seg_cap, PACKED_WIDTH), jnp.uint32),
        jax.ShapeDtypeStruct((n_dispatch, 1, ROUTE_WIDTH), jnp.int32),
        jax.ShapeDtypeStruct((1, ROUTE_WIDTH), jnp.int32),
    )
    out_specs = (
        pl.BlockSpec((T, D_MODEL), lambda i: (jnp.minimum(i, n_tiles), 0)),
        pl.BlockSpec((T, ROUTE_WIDTH), lambda i: (routed(i), 0)),
        pl.BlockSpec((1, WINDOW, KV_WIDTH), lambda i: (seq(i), 0, 0)),
        pl.BlockSpec((1, WINDOW, KV_WIDTH), lambda i: (seq(i), 0, 0)),
        pl.BlockSpec((1, CONV_W - 1, LRU_WIDTH), lambda i: (seq(i), 0, 0)),
        pl.BlockSpec((1, 1, LRU_WIDTH), lambda i: (seq(i), 0, 0)),
        pl.BlockSpec(memory_space=pl.ANY),
        pl.BlockSpec((1, 1, ROUTE_WIDTH), lambda i: (routed(i), 0, 0)),
        pl.BlockSpec((1, ROUTE_WIDTH), lambda i: (0, 0)),
    )
    scratch = [
        pltpu.VMEM((SUBLANES, LRU_WIDTH), F32),
        pltpu.VMEM((SUBLANES, LRU_WIDTH), F32),
        pltpu.VMEM((T + WINDOW, KV_WIDTH), BF16),
        pltpu.VMEM((T + WINDOW, KV_WIDTH), BF16),
        pltpu.VMEM((T, N_HEADS * HEAD_DIM), F32),
        pltpu.VMEM((T, D_MODEL), F32),
        pltpu.VMEM((2, N_SLOTS, PACKED_WIDTH), jnp.uint32),
        pltpu.VMEM((PAD_CHUNK, PACKED_WIDTH), jnp.uint32),
        pltpu.VMEM((n_assign, n_assign), BF16),
        pltpu.VMEM((T, D_MODEL), BF16),
        pltpu.VMEM((SUBLANES, n_assign), F32),
        pltpu.VMEM((SUBLANES, ROUTE_WIDTH), jnp.int32),
        pltpu.VMEM((SUBLANES, ROUTE_WIDTH), F32),
        pltpu.SMEM((1, ROUTE_WIDTH), jnp.int32),
        pltpu.SMEM((1,), jnp.int32),
        pltpu.SMEM((1,), jnp.int32),
        pltpu.SemaphoreType.DMA(()),
        pltpu.SemaphoreType.DMA(()),
        pltpu.SemaphoreType.DMA(()),
    ]
    return pl.pallas_call(
        functools.partial(_mixer_kernel, tiles_per_seq=nt, n_tiles=n_tiles, seg_cap=seg_cap),
        grid=(n_tiles + 2,),
        in_specs=in_specs,
        out_specs=out_specs,
        out_shape=out_shape,
        scratch_shapes=scratch,
        compiler_params=pltpu.CompilerParams(dimension_semantics=("arbitrary",),
                                             vmem_limit_bytes=VMEM_LIMIT_BYTES),
        name="mixer_prompt",
    )(sinks, x_prompt, *weight_args)


def _sample_proj_kernel(x_ref, w_ref, b_ref, u_ref):
    u_ref[...] = jnp.dot(x_ref[...].astype(BF16), w_ref[...], preferred_element_type=F32) + b_ref[...]


def _sample_proj_call(x_s, w_in, b_in):
    n = x_s.shape[0]
    tn = SAMPLE_PROJ_TILE
    return pl.pallas_call(
        _sample_proj_kernel,
        grid=(IN_WIDTH // tn,),
        in_specs=[pl.BlockSpec((n, D_MODEL), lambda c: (0, 0)),
                  pl.BlockSpec((D_MODEL, tn), lambda c: (0, c)),
                  pl.BlockSpec((1, tn), lambda c: (0, c))],
        out_specs=pl.BlockSpec((n, tn), lambda c: (0, c)),
        out_shape=jax.ShapeDtypeStruct((n, IN_WIDTH), F32),
        compiler_params=pltpu.CompilerParams(dimension_semantics=("arbitrary",)),
        name="sample_proj",
    )(x_s, w_in, b_in)


def _sample_attn_kernel(q_ref, kn_row_ref, vn_row_ref, ck_ref, cv_ref, sinks_ref,
                        att_ref, kwin_ref, vwin_ref):
    tb = q_ref.shape[0]
    ck = ck_ref[...]
    cv = cv_ref[...]
    ckb = ck.astype(BF16)
    cvb = cv.astype(BF16)
    kn = kn_row_ref[...]
    vn = vn_row_ref[...]
    key_pos = lax.broadcasted_iota(jnp.int32, (tb, GROUP, WINDOW), 2)
    for j in range(N_KV):
        qj = q_ref[:, j * GROUP:(j + 1) * GROUP, :]
        s_c = jnp.einsum("bgd,bsd->bgs", qj.astype(BF16), ckb, preferred_element_type=F32) * ATTN_SCALE
        s_c = jnp.where(key_pos >= 1, s_c, NEG_INF)
        s_n = jnp.sum(qj * kn, axis=-1, keepdims=True) * ATTN_SCALE
        sink = sinks_ref[j][None]
        m = jnp.maximum(jnp.maximum(jnp.max(s_c, axis=-1, keepdims=True), s_n), sink)
        p_c = jnp.exp(s_c - m)
        p_n = jnp.exp(s_n - m)
        inv = 1.0 / (jnp.sum(p_c, axis=-1, keepdims=True) + p_n + jnp.exp(sink - m))
        o = jnp.einsum("bgs,bsd->bgd", (p_c * inv).astype(BF16), cvb, preferred_element_type=F32)
        att_ref[:, j * GROUP:(j + 1) * GROUP, :] = o + (p_n * inv) * vn
    last = lax.broadcasted_iota(jnp.int32, ck.shape, 1) == WINDOW - 1
    kwin_ref[...] = jnp.where(last, kn, pltpu.roll(ck, WINDOW - 1, axis=1))
    vwin_ref[...] = jnp.where(last, vn, pltpu.roll(cv, WINDOW - 1, axis=1))


def _sample_attn_call(q_wide, kn_row, vn_row, ck, cv, sinks3):
    n = q_wide.shape[0]
    tb = SAMPLE_ATTN_TILE
    b3 = lambda i: (i, 0, 0)
    return pl.pallas_call(
        _sample_attn_kernel,
        grid=(n // tb,),
        in_specs=[pl.BlockSpec((tb, N_HEADS, KV_WIDTH), b3),
                  pl.BlockSpec((tb, 1, KV_WIDTH), b3),
                  pl.BlockSpec((tb, 1, KV_WIDTH), b3),
                  pl.BlockSpec((tb, WINDOW, KV_WIDTH), b3),
                  pl.BlockSpec((tb, WINDOW, KV_WIDTH), b3),
                  pl.BlockSpec((N_KV, GROUP, 1), lambda i: (0, 0, 0))],
        out_specs=(pl.BlockSpec((tb, N_HEADS, KV_WIDTH), b3),
                   pl.BlockSpec((tb, WINDOW, KV_WIDTH), b3),
                   pl.BlockSpec((tb, WINDOW, KV_WIDTH), b3)),
        out_shape=(jax.ShapeDtypeStruct((n, N_HEADS, KV_WIDTH), F32),
                   jax.ShapeDtypeStruct((n, WINDOW, KV_WIDTH), F32),
                   jax.ShapeDtypeStruct((n, WINDOW, KV_WIDTH), F32)),
        compiler_params=pltpu.CompilerParams(dimension_semantics=("arbitrary",),
                                             vmem_limit_bytes=VMEM_LIMIT_BYTES),
        name="sample_attn",
    )(q_wide, kn_row, vn_row, ck, cv, sinks3)


def _sample_mix_kernel(x_ref, u_ref, att_ref, st_ref, h0_ref, w_conv_ref, b_conv_ref, w_gates_ref, b_rg_ref,
                       b_ig_ref, lam_ref, w_out_ref, ln_g_ref, ln_b_ref,
                       x1_ref, conv_ref, h_ref):
    xl = u_ref[:, OFF_XL:OFF_XL + LRU_WIDTH]
    wc = w_conv_ref[...]
    xc = wc[0:1] * st_ref[0]
    xc = xc + wc[1:2] * st_ref[1]
    xc = xc + wc[2:3] * st_ref[2]
    xc = xc + wc[3:4] * xl + b_conv_ref[...]
    conv_ref[0] = st_ref[1]
    conv_ref[1] = st_ref[2]
    conv_ref[2] = xl
    a, u = _lru_gates(xc, w_gates_ref, b_rg_ref[...], b_ig_ref[...], lam_ref[...])
    h = a * h0_ref[...] + u
    h_ref[...] = h
    rec = h * jax.nn.gelu(u_ref[:, OFF_YL:OFF_YL + LRU_WIDTH])
    x1_ref[...] = _merge_norm(x_ref[...], rec, att_ref[...], u_ref[:, OFF_GL:OFF_GL + D_MODEL],
                              u_ref[:, OFF_GA:OFF_GA + D_MODEL], w_out_ref,
                              ln_g_ref[...], ln_b_ref[...])


def _sample_mix_call(x_s, u_s, att, st_t, h0, wts):
    n = x_s.shape[0]
    weight_args = (wts["w_conv"], wts["b_conv"], wts["w_gates"], wts["b_rg"], wts["b_ig"], wts["lam"],
                   wts["w_out"], wts["ln1_g"], wts["ln1_b"])
    args = (x_s, u_s, att, st_t, h0) + weight_args
    out_shapes = ((n, D_MODEL), (CONV_W - 1, n, LRU_WIDTH), (n, LRU_WIDTH))
    return pl.pallas_call(
        _sample_mix_kernel,
        grid=(1,),
        in_specs=[_const_spec(a.shape) for a in args],
        out_specs=tuple(_const_spec(s) for s in out_shapes),
        out_shape=tuple(jax.ShapeDtypeStruct(s, F32) for s in out_shapes),
        compiler_params=pltpu.CompilerParams(dimension_semantics=("arbitrary",),
                                             vmem_limit_bytes=VMEM_LIMIT_BYTES),
        name="sample_mix",
    )(*args)


def _one_hot(mask):
    return jnp.where(mask, 1.0, 0.0).astype(BF16)


def _pack_rows(x):
    half = x.shape[1] // 2
    lo = lax.shift_right_logical(lax.bitcast_convert_type(x[:, :half], jnp.uint32), jnp.uint32(16))
    hi = lax.bitcast_convert_type(x[:, half:], jnp.uint32) & jnp.uint32(0xFFFF0000)
    return lo | hi


def _unpack_rows(words):
    lo = lax.bitcast_convert_type(lax.shift_left(words, jnp.uint32(16)), F32)
    hi = lax.bitcast_convert_type(words & jnp.uint32(0xFFFF0000), F32)
    return jnp.concatenate([lo.astype(BF16), hi.astype(BF16)], axis=1)


class _Dispatcher:
    def __init__(self, xs_hbm, goff_ref, gtot_ref, stage, zero_rows, tri, xb_keep, key_keep, g_vmem, gvec,
                 g_smem, nd_smem, rounds_smem, sem_stage, sem_g, sem_zero, seg_cap):
        self.xs_hbm, self.goff_ref, self.gtot_ref = xs_hbm, goff_ref, gtot_ref
        self.stage, self.zero_rows, self.tri = stage, zero_rows, tri
        self.xb_keep, self.key_keep = xb_keep, key_keep
        self.g_vmem, self.gvec, self.g_smem = g_vmem, gvec, g_smem
        self.nd_smem, self.rounds_smem = nd_smem, rounds_smem
        self.sem_stage, self.sem_g, self.sem_zero = sem_stage, sem_g, sem_zero
        self.seg_cap = seg_cap

    def _g_copy(self):
        return pltpu.make_async_copy(self.g_vmem.at[pl.ds(0, 1)], self.g_smem, self.sem_g)

    def init(self):
        n_assign = self.tri.shape[0]
        r = lax.broadcasted_iota(jnp.int32, (n_assign, n_assign), 0)
        c = lax.broadcasted_iota(jnp.int32, (n_assign, n_assign), 1)
        self.tri[...] = _one_hot(r < c)
        self.gvec[...] = jnp.zeros_like(self.gvec)
        self.g_vmem[...] = jnp.zeros_like(self.g_vmem)
        self.nd_smem[0] = 0
        self._g_copy().start()

    def _sorted_rows(self, e_row, rank_row, xb, lo):
        td = xb.shape[0]
        slot = lax.broadcasted_iota(jnp.int32, (N_SLOTS, td), 0).astype(F32)
        lo_f = lo.astype(F32)
        in_round = (rank_row >= lo_f) & (rank_row < lo_f + SLOT_CAP) & (e_row >= 0.0)
        key = jnp.where(in_round, e_row * SLOT_CAP + (rank_row - lo_f), -1.0)
        perm = jnp.where(slot == key[:, :td], 1.0, jnp.where(slot == key[:, td:], 1.0, 0.0)).astype(BF16)
        return _pack_rows(jnp.dot(perm, xb, preferred_element_type=F32))

    def plan(self, x1, route, n_valid, buf):
        td = x1.shape[0]
        n_assign = TOP_K * td
        valid_col = lax.broadcasted_iota(jnp.int32, (td, 1), 0) < n_valid
        lane = lax.broadcasted_iota(jnp.int32, (td, ROUTE_WIDTH), 1).astype(F32)
        e0c = jnp.where(valid_col, route[:, 0:1], -1.0)
        e1c = jnp.where(valid_col, route[:, 1:2], -1.0)
        cnt_row = jnp.sum(jnp.where(lane == e0c, 1.0, 0.0) + jnp.where(lane == e1c, 1.0, 0.0),
                          axis=0, keepdims=True)
        self.rounds_smem[0] = (jnp.max(cnt_row).astype(jnp.int32) + SLOT_CAP - 1) // SLOT_CAP
        self.goff_ref[0] = self.gvec[0:1, :].astype(jnp.int32)
        self.gvec[0:1, :] = self.gvec[0:1, :] + jnp.ceil(cnt_row * (1.0 / SUBLANES)) * SUBLANES
        self.gtot_ref[...] = self.gvec[0:1, :].astype(jnp.int32)

        route_t = route.T
        valid_row = lax.broadcasted_iota(jnp.int32, (1, td), 1) < n_valid
        e_row = jnp.concatenate([jnp.where(valid_row, route_t[0:1, :], -1.0),
                                 jnp.where(valid_row, route_t[1:2, :], -1.0)], axis=1)
        expert_sub = lax.broadcasted_iota(jnp.int32, (N_EXPERTS, n_assign), 0).astype(F32)
        onehot_t = expert_sub == e_row
        rank_t = jnp.dot(_one_hot(onehot_t), self.tri[...], preferred_element_type=F32)
        rank_row = jnp.sum(jnp.where(onehot_t, rank_t, 0.0), axis=0, keepdims=True)
        xb = jnp.where(valid_col, x1, 0.0).astype(BF16)
        self.xb_keep[...] = xb
        self.key_keep[0:1, :] = e_row
        self.key_keep[1:2, :] = rank_row
        self.stage[buf] = self._sorted_rows(e_row, rank_row, xb, jnp.int32(0))

    def _stage_copy(self, buf, e, dst_row):
        src = self.stage.at[buf, pl.ds(e * SLOT_CAP, SLOT_CAP)]
        return pltpu.make_async_copy(src, self.xs_hbm.at[pl.ds(dst_row, SLOT_CAP)], self.sem_stage)

    def _wait_outstanding(self):
        @pl.when(self.nd_smem[0] > 0)
        def _():
            for _ in range(N_EXPERTS):
                self._stage_copy(0, 0, 0).wait()
        self.nd_smem[0] = 0

    def _start_round(self, buf, lo):
        for e in range(N_EXPERTS):
            dst_row = pl.multiple_of(e * self.seg_cap + self.g_smem[0, e] + lo, SUBLANES)
            self._stage_copy(buf, e, dst_row).start(priority=e % N_DMA_PRIORITIES)
        self.nd_smem[0] = N_EXPERTS

    def flush(self, buf, is_last):
        self._g_copy().wait()
        self._wait_outstanding()
        self._start_round(buf, 0)

        def later_round(r, carry):
            lo = r * SLOT_CAP
            rows = self._sorted_rows(self.key_keep[0:1, :], self.key_keep[1:2, :], self.xb_keep[...], lo)
            self._wait_outstanding()
            self.stage[buf] = rows
            self._start_round(buf, lo)
            return carry
        lax.fori_loop(1, self.rounds_smem[0], later_round, 0)

        self.g_vmem[0:1, :] = self.gvec[0:1, :].astype(jnp.int32)
        self._g_copy().start()

        @pl.when(is_last)
        def _():
            self._g_copy().wait()
            self._wait_outstanding()
            self.zero_rows[...] = jnp.zeros_like(self.zero_rows)

            def pad_copy(dst_row):
                dst = self.xs_hbm.at[pl.ds(pl.multiple_of(dst_row, SUBLANES), PAD_CHUNK)]
                return pltpu.make_async_copy(self.zero_rows, dst, self.sem_zero)

            def pad_segment(e, n_started):
                fill = self.g_smem[0, e]
                block_end = (fill + SLOT_CAP + MOE_TILE - 1) // MOE_TILE * MOE_TILE
                n_chunks = (block_end - fill + PAD_CHUNK - 1) // PAD_CHUNK

                def start(c, carry):
                    pad_copy(e * self.seg_cap + fill + c * PAD_CHUNK).start()
                    return carry
                lax.fori_loop(0, n_chunks, start, 0)
                return n_started + n_chunks
            n_started = lax.fori_loop(0, N_EXPERTS, pad_segment, 0)

            def wait(_, carry):
                pad_copy(0).wait()
                return carry
            lax.fori_loop(0, n_started, wait, 0)


def _dispatch_tiles(n_tokens):
    return (n_tokens + DISPATCH_TILE - 1) // DISPATCH_TILE


def _segment_capacity(n_tokens):
    assert SEG_PAD >= DISPATCH_TILE
    alignment_slack = (SUBLANES - 1) * _dispatch_tiles(n_tokens)
    return (n_tokens + alignment_slack + SEG_PAD + MOE_TILE - 1) // MOE_TILE * MOE_TILE


def _moe_kernel(blk_e_ref, blk_j_ref, n_used_ref, xs_ref, wg_ref, wu_ref, wd_ref, ys_ref, wg_bf, wu_bf, wd_bf):
    del blk_j_ref
    b = pl.program_id(0)

    @pl.when(b < n_used_ref[0])
    def _():
        prev_e = blk_e_ref[jnp.maximum(b - 1, 0)]

        @pl.when((b == 0) | (blk_e_ref[b] != prev_e))
        def _():
            wg_bf[...] = wg_ref[0].astype(BF16)
            wu_bf[...] = wu_ref[0].astype(BF16)
            wd_bf[...] = wd_ref[0].astype(BF16)

        xb = _unpack_rows(xs_ref[...])
        g = jnp.dot(xb, wg_bf[...], preferred_element_type=F32)
        u = jnp.dot(xb, wu_bf[...], preferred_element_type=F32)
        hmid = (g * _sigmoid(g)) * u
        y = jnp.dot(hmid.astype(BF16), wd_bf[...], preferred_element_type=F32)
        ys_ref[...] = _pack_rows(y.astype(BF16).astype(F32))


def _expert_blocks(gtot, n_blocks):
    rows = gtot[0, :N_EXPERTS]
    nb = (rows + SLOT_CAP + MOE_TILE - 1) // MOE_TILE
    ends = jnp.cumsum(nb)
    n_used = ends[-1]
    step = jnp.minimum(jnp.arange(n_blocks, dtype=jnp.int32), n_used - 1)
    blk_e = jnp.sum((step[:, None] >= ends[None, :]).astype(jnp.int32), axis=1)
    blk_j = step - (ends - nb)[blk_e]
    return blk_e, blk_j.astype(jnp.int32), n_used.reshape(1).astype(jnp.int32)


def _moe_call(gtot, xs, w_gate, w_up, w_down, n_tokens):
    seg_cap = _segment_capacity(n_tokens)
    seg_blocks = seg_cap // MOE_TILE
    max_rows = n_tokens * TOP_K + N_EXPERTS * (SUBLANES - 1) * _dispatch_tiles(n_tokens)
    n_blocks = (max_rows + N_EXPERTS * (SLOT_CAP + MOE_TILE - 1)) // MOE_TILE
    blk_e, blk_j, n_used = _expert_blocks(gtot, n_blocks)
    row_block = lambda b, be, bj, nu: (be[b] * seg_blocks + bj[b], 0)
    weight_block = lambda b, be, bj, nu: (be[b], 0, 0)
    grid_spec = pltpu.PrefetchScalarGridSpec(
        num_scalar_prefetch=3,
        grid=(n_blocks,),
        in_specs=[
            pl.BlockSpec((MOE_TILE, PACKED_WIDTH), row_block),
            pl.BlockSpec((1, D_MODEL, EXPERT_FF), weight_block),
            pl.BlockSpec((1, D_MODEL, EXPERT_FF), weight_block),
            pl.BlockSpec((1, EXPERT_FF, D_MODEL), weight_block),
        ],
        out_specs=pl.BlockSpec((MOE_TILE, PACKED_WIDTH), row_block),
        scratch_shapes=[
            pltpu.VMEM((D_MODEL, EXPERT_FF), BF16),
            pltpu.VMEM((D_MODEL, EXPERT_FF), BF16),
            pltpu.VMEM((EXPERT_FF, D_MODEL), BF16),
        ],
    )
    return pl.pallas_call(
        _moe_kernel,
        grid_spec=grid_spec,
        out_shape=jax.ShapeDtypeStruct(xs.shape, xs.dtype),
        compiler_params=pltpu.CompilerParams(dimension_semantics=("arbitrary",),
                                             vmem_limit_bytes=VMEM_LIMIT_BYTES),
        name="moe_experts",
    )(blk_e, blk_j, n_used, xs, w_gate, w_up, w_down)


def _combine_kernel(gcur_ref, gnext_ref, x1_ref, route_ref, ys_hbm, ln_g_ref, ln_b_ref, y_ref,
                    stage, acc, tri, sem, *, seg_cap):
    td = x1_ref.shape[0]
    n_assign = TOP_K * td
    i = pl.program_id(0)
    buf = lax.rem(i, 2)

    def stage_copy(g_ref, e, first_rank, to_buf):
        src_row = pl.multiple_of(e * seg_cap + g_ref[0, 0, e] + first_rank, SUBLANES)
        slot0 = e * SLOT_CAP
        if not isinstance(slot0, int):
            slot0 = pl.multiple_of(slot0, SLOT_CAP)
        dst = stage.at[to_buf, pl.ds(slot0, SLOT_CAP)]
        return pltpu.make_async_copy(ys_hbm.at[pl.ds(src_row, SLOT_CAP)], dst, sem.at[to_buf])

    def fetch_first_round(g_ref, to_buf):
        for e in range(N_EXPERTS):
            stage_copy(g_ref, e, 0, to_buf).start(priority=e % N_DMA_PRIORITIES)

    @pl.when(i == 0)
    def _():
        r = lax.broadcasted_iota(jnp.int32, (n_assign, n_assign), 0)
        c = lax.broadcasted_iota(jnp.int32, (n_assign, n_assign), 1)
        tri[...] = _one_hot(c < r)
        fetch_first_round(gcur_ref, 0)

    def count(e):
        return gnext_ref[0, 0, e] - gcur_ref[0, 0, e]

    route = route_ref[...]
    e0, e1, w0, w1 = route[:, 0:1], route[:, 1:2], route[:, 2:3], route[:, 3:4]
    lane = lax.broadcasted_iota(jnp.int32, (td, ROUTE_WIDTH), 1).astype(F32)
    onehot = jnp.concatenate([lane == e0, lane == e1], axis=0)
    onehot_f = jnp.where(onehot, 1.0, 0.0)
    rank_mat = jnp.dot(tri[...], onehot_f.astype(BF16), preferred_element_type=F32)
    rank = jnp.sum(jnp.where(onehot, rank_mat, 0.0), axis=1, keepdims=True)
    r0, r1 = rank[:td], rank[td:]
    cnt_row = jnp.sum(onehot_f, axis=0, keepdims=True)
    rounds = (jnp.max(cnt_row).astype(jnp.int32) + SLOT_CAP - 1) // SLOT_CAP

    slot_lane = lax.broadcasted_iota(jnp.int32, (td, N_SLOTS), 1).astype(F32)

    def wait_copies(n):
        def wait_one(_, c):
            stage_copy(gcur_ref, 0, 0, buf).wait()
            return c
        lax.fori_loop(0, n, wait_one, 0)

    def weighted_rows(lo):
        lo_f = lo.astype(F32)
        rows = _unpack_rows(stage[buf])

        def selector(e_col, r_col, w_col):
            in_round = (r_col >= lo_f) & (r_col < lo_f + SLOT_CAP)
            key = jnp.where(in_round, e_col * SLOT_CAP + (r_col - lo_f), -1.0)
            return jnp.where(slot_lane == key, w_col, 0.0)
        mix = (selector(e0, r0, w0) + selector(e1, r1, w1)).astype(BF16)
        return jnp.dot(mix, rows, preferred_element_type=F32)

    for _ in range(N_EXPERTS):
        stage_copy(gcur_ref, 0, 0, buf).wait()

    fetch_first_round(gnext_ref, 1 - buf)

    acc[...] = weighted_rows(jnp.int32(0))

    def later_round(r, carry):
        lo = r * SLOT_CAP

        def start(e, n_started):
            has_rows = count(e) > lo

            @pl.when(has_rows)
            def _():
                stage_copy(gcur_ref, e, lo, buf).start()
            return n_started + has_rows.astype(jnp.int32)
        wait_copies(lax.fori_loop(0, N_EXPERTS, start, 0))
        acc[...] = acc[...] + weighted_rows(lo)
        return carry
    lax.fori_loop(1, rounds, later_round, 0)

    y_ref[...] = _layer_norm(ALPHA * x1_ref[...] + acc[...], ln_g_ref[...], ln_b_ref[...])

    @pl.when(i + 1 == pl.num_programs(0))
    def _():
        for _ in range(N_EXPERTS):
            stage_copy(gnext_ref, 0, 0, 1 - buf).wait()


def _combine_call(goff, x1_all, route_all, ys, ln_g, ln_b, first_row, n_rows, tile, n_tokens):
    assert first_row % DISPATCH_TILE == 0 and first_row % tile == 0
    assert tile == DISPATCH_TILE or n_rows == tile
    first_block = first_row // tile
    first_goff = first_row // DISPATCH_TILE
    n_assign = TOP_K * tile
    grid_spec = pl.GridSpec(
        grid=(n_rows // tile,),
        in_specs=[
            pl.BlockSpec((1, 1, ROUTE_WIDTH), lambda i: (first_goff + i, 0, 0), memory_space=pltpu.SMEM),
            pl.BlockSpec((1, 1, ROUTE_WIDTH), lambda i: (first_goff + i + 1, 0, 0), memory_space=pltpu.SMEM),
            pl.BlockSpec((tile, D_MODEL), lambda i: (first_block + i, 0)),
            pl.BlockSpec((tile, ROUTE_WIDTH), lambda i: (first_block + i, 0)),
            pl.BlockSpec(memory_space=pl.ANY),
            _const_spec(ln_g.shape),
            _const_spec(ln_b.shape),
        ],
        out_specs=pl.BlockSpec((tile, D_MODEL), lambda i: (i, 0)),
        scratch_shapes=[pltpu.VMEM((2, N_SLOTS, PACKED_WIDTH), jnp.uint32),
                        pltpu.VMEM((tile, D_MODEL), F32),
                        pltpu.VMEM((n_assign, n_assign), BF16),
                        pltpu.SemaphoreType.DMA((2,))],
    )
    return pl.pallas_call(
        functools.partial(_combine_kernel, seg_cap=_segment_capacity(n_tokens)),
        grid_spec=grid_spec,
        out_shape=jax.ShapeDtypeStruct((n_rows, D_MODEL), F32),
        compiler_params=pltpu.CompilerParams(dimension_semantics=("arbitrary",),
                                             vmem_limit_bytes=VMEM_LIMIT_BYTES),
        name="moe_combine",
    )(goff, goff, x1_all, route_all, ys, ln_g, ln_b)


def _prepare_weights(w_in, b_in, w_conv, b_conv, w_rg, b_rg, w_ig, b_ig, lru_lambda, w_lru_out, w_attn_out, w_o,
                     ln1_g, ln1_b, w_group, b_group, w_router, b_router):
    blocks_per_chunk = GATE_CHUNK // LRU_BLOCK

    def chunked_block_diag(w):
        w = w.reshape(N_GATE_CHUNKS, blocks_per_chunk, LRU_BLOCK, LRU_BLOCK)
        eye = jnp.eye(blocks_per_chunk, dtype=w.dtype)
        return jnp.einsum("cbij,bd->cbidj", w, eye).reshape(N_GATE_CHUNKS, GATE_CHUNK, GATE_CHUNK)

    w_gates = jnp.concatenate([chunked_block_diag(w_rg), chunked_block_diag(w_ig)], axis=-1).astype(BF16)
    w_rt = jnp.concatenate([w_group, w_router], axis=1)
    w_rt = jnp.pad(w_rt, ((0, 0), (0, ROUTE_WIDTH - w_rt.shape[1])))
    w_rt_hi = w_rt.astype(BF16)
    w_rt_lo = jnp.concatenate([w_rt_hi, (w_rt - w_rt_hi.astype(F32)).astype(BF16)], axis=1)
    b_rt = jnp.pad(jnp.concatenate([b_group, b_router]), (0, ROUTE_WIDTH - N_GROUPS - N_EXPERTS))
    row = lambda v: v.reshape(1, -1)
    return dict(
        w_in=w_in.astype(BF16), b_in=row(b_in), w_conv=w_conv, b_conv=row(b_conv), w_gates=w_gates,
        b_rg=row(b_rg), b_ig=row(b_ig), lam=row(lru_lambda),
        w_out=jnp.stack([w_lru_out, w_attn_out, w_o]).astype(BF16),
        ln1_g=row(ln1_g), ln1_b=row(ln1_b), w_rt_hi=w_rt_hi, w_rt_lo=w_rt_lo, b_rt=row(b_rt))


def kernel(x_prompt, x_sample, cache_k, cache_v, state_conv, state_lru_h, w_in, b_in, w_conv, b_conv, w_rg, b_rg,
           w_ig, b_ig, lru_lambda, sinks, w_lru_out, w_attn_out, w_o, ln1_g, ln1_b, w_group, b_group, w_router,
           b_router, w_gate, w_up, w_down, ln2_g, ln2_b):
    B, S, _ = x_prompt.shape
    n_prompt = B * S
    n_sample = x_sample.shape[0]
    n_all = n_prompt + n_sample
    wts = _prepare_weights(w_in, b_in, w_conv, b_conv, w_rg, b_rg, w_ig, b_ig, lru_lambda, w_lru_out, w_attn_out,
                           w_o, ln1_g, ln1_b, w_group, b_group, w_router, b_router)

    x_s = x_sample.reshape(n_sample, D_MODEL)
    u_s = _sample_proj_call(x_s, wts["w_in"], wts["b_in"])
    q4 = u_s[:, OFF_Q:OFF_K].reshape(n_sample, N_KV, GROUP, HEAD_DIM)
    q_wide = jnp.einsum("njgd,jk->njgkd", q4, jnp.eye(N_KV, dtype=F32)).reshape(n_sample, N_HEADS, KV_WIDTH)
    k_new = u_s[:, OFF_K:OFF_V]
    v_new = u_s[:, OFF_V:OFF_GL]
    att_wide, k_win_s, v_win_s = _sample_attn_call(
        q_wide, k_new.reshape(n_sample, 1, KV_WIDTH), v_new.reshape(n_sample, 1, KV_WIDTH),
        cache_k.reshape(n_sample, WINDOW, KV_WIDTH), cache_v.reshape(n_sample, WINDOW, KV_WIDTH),
        sinks.reshape(N_KV, GROUP, 1))
    att5 = att_wide.reshape(n_sample, N_KV, GROUP, N_KV, HEAD_DIM)
    att_s = jnp.stack([att5[:, j, :, j, :] for j in range(N_KV)], axis=1).reshape(n_sample, N_HEADS * HEAD_DIM)
    x1_s, conv_s_t, h_s = _sample_mix_call(
        x_s, u_s, att_s, jnp.transpose(state_conv, (1, 0, 2)), state_lru_h, wts)

    x1_all, route_all, k_win_p, v_win_p, conv_p, h_p, xs, goff, gtot = _mixer_call(x_prompt, x1_s, sinks, wts)

    ys = _moe_call(gtot, xs, w_gate, w_up, w_down, n_all)
    goff = jnp.concatenate([goff, gtot[None]], axis=0)
    ln2_g2, ln2_b2 = ln2_g.reshape(1, -1), ln2_b.reshape(1, -1)
    y_p = _combine_call(goff, x1_all, route_all, ys, ln2_g2, ln2_b2, 0, n_prompt, DISPATCH_TILE, n_all)
    y_s = _combine_call(goff, x1_all, route_all, ys, ln2_g2, ln2_b2, n_prompt, n_sample, n_sample, n_all)

    kv_shape = (WINDOW, N_KV, HEAD_DIM)
    return (y_p.reshape(B, S, D_MODEL), y_s.reshape(n_sample, 1, D_MODEL),
            k_win_p.reshape((B,) + kv_shape), v_win_p.reshape((B,) + kv_shape), conv_p, h_p.reshape(B, LRU_WIDTH),
            k_win_s.reshape((n_sample,) + kv_shape), v_win_s.reshape((n_sample,) + kv_shape),
            jnp.transpose(conv_s_t, (1, 0, 2)), h_s)
```
